```python
import math
import jax, jax.numpy as jnp
from jax import lax
import numpy as np

D_MODEL = 2048
BATCH = 8
SEQ = 4096
DEPTH = 1

HEAD_DIM = 64
DILATION_PATTERNS = ((128, 1), (512, 4), (2048, 16))
N_ATT_GROUPS = len(DILATION_PATTERNS)
HEADS_PER_GROUP = 6
N_Q_HEADS = N_ATT_GROUPS * HEADS_PER_GROUP
KV_HEADS = HEADS_PER_GROUP
ATT_Q_WIDTH = N_Q_HEADS * HEAD_DIM
KV_WIDTH = KV_HEADS * HEAD_DIM
ROT_DIM = HEAD_DIM // 4
ROPE_THETA = 500000.0
BLK = 128
SSM_WIDTH = D_MODEL - ATT_Q_WIDTH
SSM_GROUP_CH = 16
SSM_GROUPS = SSM_WIDTH // SSM_GROUP_CH
SSM_STATE = 64
IN_WIDTH = ATT_Q_WIDTH + 2 * KV_WIDTH + SSM_WIDTH
OUT_IN_WIDTH = KV_WIDTH + SSM_WIDTH
D_FF = 4 * D_MODEL
N_MOD = 6
EPS = 1e-6

kernel_name = "hymba_s5_longnet_sandwich_adaln_block"


def rms_norm(x, g):
    xf = x.astype(jnp.float32)
    y = xf * lax.rsqrt(jnp.mean(xf * xf, axis=-1, keepdims=True) + EPS)
    return (y * g.astype(jnp.float32)).astype(x.dtype)


def rope_partial(t, positions):
    freqs = ROPE_THETA ** (-jnp.arange(0, ROT_DIM, 2, dtype=jnp.float32) / ROT_DIM)
    ang = positions.astype(jnp.float32)[..., None] * freqs
    cos = jnp.cos(ang)[:, :, None, :]
    sin = jnp.sin(ang)[:, :, None, :]
    tr = t[..., :ROT_DIM].astype(jnp.float32)
    x1, x2 = tr[..., :ROT_DIM // 2], tr[..., ROT_DIM // 2:]
    rot = jnp.concatenate([x1 * cos - x2 * sin, x2 * cos + x1 * sin], axis=-1).astype(t.dtype)
    return jnp.concatenate([rot, t[..., ROT_DIM:]], axis=-1)


def dilated_window_attention(q, k, v, dilation, span):
    assert span <= BLK
    B, L, H, Dh = q.shape
    n = L // dilation

    def to_sub(t):
        return t.reshape(B, n, dilation, H, Dh).transpose(0, 2, 3, 1, 4)

    qs, ks, vs = to_sub(q), to_sub(k), to_sub(v)
    n_pad = -(-n // BLK) * BLK
    pad = ((0, 0), (0, 0), (0, 0), (0, n_pad - n), (0, 0))
    qs, ks, vs = (jnp.pad(t, pad) for t in (qs, ks, vs))
    nb = n_pad // BLK
    qb = qs.reshape(B, dilation, H, nb, BLK, Dh)
    kb = ks.reshape(B, dilation, H, nb, BLK, Dh)
    vb = vs.reshape(B, dilation, H, nb, BLK, Dh)

    def with_prev(t):
        prev = jnp.concatenate([jnp.zeros_like(t[:, :, :, :1]), t[:, :, :, :-1]], axis=3)
        return jnp.concatenate([prev, t], axis=4)

    kc, vc = with_prev(kb), with_prev(vb)
    scores = jnp.einsum('brhnqe,brhnke->brhnqk', qb, kc).astype(jnp.float32) / math.sqrt(Dh)
    blk = jnp.arange(nb)[:, None, None]
    qi = jnp.arange(BLK)[None, :, None]
    kj = jnp.arange(2 * BLK)[None, None, :]
    dist = qi + BLK - kj
    valid = (dist >= 0) & (dist <= span) & (blk * BLK - BLK + kj >= 0)
    scores = jnp.where(valid, scores, -jnp.inf)
    lse = jax.nn.logsumexp(scores, axis=-1)
    p = jnp.exp(scores - lse[..., None]).astype(v.dtype)
    out = jnp.einsum('brhnqk,brhnke->brhnqe', p, vc)
    out = out.reshape(B, dilation, H, n_pad, Dh)[:, :, :, :n]
    out = out.transpose(0, 3, 1, 2, 4).reshape(B, L, H, Dh)
    lse = lse.reshape(B, dilation, H, n_pad)[:, :, :, :n].transpose(0, 3, 1, 2).reshape(B, L, H)
    return out, lse


def _scan_op(e1, e2):
    a1, b1 = e1
    a2, b2 = e2
    return a1 * a2, a2 * b1 + b2


def s5_mixer(u, a_re, a_im, log_dt, b_re, b_im, c_re, c_im, d_skip, w_glu, b_glu):
    B, L, G, P = u.shape
    u32 = u.astype(jnp.float32)
    lam = lax.complex(a_re.astype(jnp.float32), a_im.astype(jnp.float32))
    dt = jnp.exp(log_dt.astype(jnp.float32))[:, None]
    a_bar = jnp.exp(lam * dt)
    b_mat = lax.complex(b_re.astype(jnp.float32), b_im.astype(jnp.float32))
    b_bar = ((a_bar - 1.0) / lam)[..., None] * b_mat
    c_mat = lax.complex(c_re.astype(jnp.float32), c_im.astype(jnp.float32))
    bu = jnp.einsum('blgp,gnp->blgn', u32.astype(jnp.complex64), b_bar)
    a_all = jnp.broadcast_to(a_bar, bu.shape)
    _, state = lax.associative_scan(_scan_op, (a_all, bu), axis=1)
    y = jnp.einsum('blgn,gpn->blgp', state, c_mat).real + d_skip.astype(jnp.float32) * u32
    y = y.reshape(B, L, G * P)
    y = jax.nn.gelu(y)
    y = y * jax.nn.sigmoid(y @ w_glu.astype(jnp.float32) + b_glu.astype(jnp.float32))
    return y.astype(u.dtype)


def _fwd_setup_inputs(seed: int = 0) -> dict:
    key = jax.random.key(seed)
    ks = jax.random.split(key, 32)
    f32 = jnp.float32
    nrm = lambda k, shape, s: jax.random.normal(k, shape, f32) * s
    x = jax.random.normal(ks[0], (BATCH, SEQ, D_MODEL), f32)
    c = jax.random.normal(ks[1], (BATCH, D_MODEL), f32)
    offset = jax.random.randint(ks[2], (BATCH, 1), 0, 1024, dtype=jnp.int32)
    positions = (offset + jnp.arange(SEQ, dtype=jnp.int32)[None, :]).astype(jnp.int32)
    n_idx = jnp.arange(SSM_STATE, dtype=f32)
    return {
        "x": x,
        "c": c,
        "positions": positions,
        "w_ada": nrm(ks[3], (DEPTH, D_MODEL, N_MOD * D_MODEL), 0.5 * D_MODEL ** -0.5),
        "b_ada": nrm(ks[4], (DEPTH, N_MOD * D_MODEL), 0.01),
        "g_pre_mix": 1.0 + nrm(ks[5], (DEPTH, D_MODEL), 0.02),
        "g_post_mix": 1.0 + nrm(ks[6], (DEPTH, D_MODEL), 0.02),
        "w_in": nrm(ks[7], (DEPTH, D_MODEL, IN_WIDTH), D_MODEL ** -0.5),
        "ssm_a_re": -0.5 + nrm(ks[8], (DEPTH, SSM_GROUPS, SSM_STATE), 0.01),
        "ssm_a_im": math.pi * n_idx + nrm(ks[9], (DEPTH, SSM_GROUPS, SSM_STATE), 0.01),
        "ssm_log_dt": jax.random.uniform(ks[10], (DEPTH, SSM_GROUPS), f32, math.log(1e-3), math.log(1e-1)),
        "ssm_b_re": nrm(ks[11], (DEPTH, SSM_GROUPS, SSM_STATE, SSM_GROUP_CH), (2 * SSM_GROUP_CH) ** -0.5),
        "ssm_b_im": nrm(ks[12], (DEPTH, SSM_GROUPS, SSM_STATE, SSM_GROUP_CH), (2 * SSM_GROUP_CH) ** -0.5),
        "ssm_c_re": nrm(ks[13], (DEPTH, SSM_GROUPS, SSM_GROUP_CH, SSM_STATE), (2 * SSM_STATE) ** -0.5),
        "ssm_c_im": nrm(ks[14], (DEPTH, SSM_GROUPS, SSM_GROUP_CH, SSM_STATE), (2 * SSM_STATE) ** -0.5),
        "ssm_d": nrm(ks[15], (DEPTH, SSM_GROUPS, SSM_GROUP_CH), 1.0),
        "w_glu": nrm(ks[16], (DEPTH, SSM_WIDTH, SSM_WIDTH), SSM_WIDTH ** -0.5),
        "b_glu": nrm(ks[17], (DEPTH, SSM_WIDTH), 0.01),
        "g_attn_out": 1.0 + nrm(ks[18], (DEPTH, KV_WIDTH), 0.02),
        "g_ssm_out": 1.0 + nrm(ks[19], (DEPTH, SSM_WIDTH), 0.02),
        "w_out": nrm(ks[20], (DEPTH, OUT_IN_WIDTH, D_MODEL), OUT_IN_WIDTH ** -0.5),
        "g_pre_mlp": 1.0 + nrm(ks[21], (DEPTH, D_MODEL), 0.02),
        "g_post_mlp": 1.0 + nrm(ks[22], (DEPTH, D_MODEL), 0.02),
        "w_mlp_in": nrm(ks[23], (DEPTH, D_MODEL, D_FF), D_MODEL ** -0.5),
        "w_mlp_out": nrm(ks[24], (DEPTH, D_FF, D_MODEL), D_FF ** -0.5),
    }


def _fwd_reference(x, c, positions, w_ada, b_ada, g_pre_mix, g_post_mix, w_in,
              ssm_a_re, ssm_a_im, ssm_log_dt, ssm_b_re, ssm_b_im, ssm_c_re, ssm_c_im,
              ssm_d, w_glu, b_glu, g_attn_out, g_ssm_out, w_out,
              g_pre_mlp, g_post_mlp, w_mlp_in, w_mlp_out):
    B, L, _ = x.shape
    for l in range(DEPTH):
        mod = jax.nn.silu(c) @ w_ada[l] + b_ada[l]
        sh1, sc1, gt1, sh2, sc2, gt2 = (m[:, None, :] for m in jnp.split(mod, N_MOD, axis=-1))

        h = rms_norm(x, g_pre_mix[l]) * (1.0 + sc1) + sh1
        proj = h @ w_in[l]
        q = proj[..., :ATT_Q_WIDTH].reshape(B, L, N_Q_HEADS, HEAD_DIM)
        k = proj[..., ATT_Q_WIDTH:ATT_Q_WIDTH + KV_WIDTH].reshape(B, L, KV_HEADS, HEAD_DIM)
        v = proj[..., ATT_Q_WIDTH + KV_WIDTH:ATT_Q_WIDTH + 2 * KV_WIDTH].reshape(B, L, KV_HEADS, HEAD_DIM)
        u = proj[..., ATT_Q_WIDTH + 2 * KV_WIDTH:].reshape(B, L, SSM_GROUPS, SSM_GROUP_CH)

        q = rope_partial(q, positions).reshape(B, L, N_ATT_GROUPS, HEADS_PER_GROUP, HEAD_DIM)
        k = rope_partial(k, positions)
        outs, lses = [], []
        for gi, (window, dilation) in enumerate(DILATION_PATTERNS):
            o_g, lse_g = dilated_window_attention(q[:, :, gi], k, v, dilation, window // dilation)
            outs.append(o_g)
            lses.append(lse_g)
        wts = jax.nn.softmax(jnp.stack(lses, axis=0), axis=0)
        att = jnp.sum(wts[..., None].astype(x.dtype) * jnp.stack(outs, axis=0), axis=0)
        att = rms_norm(att.reshape(B, L, KV_WIDTH), g_attn_out[l])

        ssm = s5_mixer(u, ssm_a_re[l], ssm_a_im[l], ssm_log_dt[l], ssm_b_re[l], ssm_b_im[l],
                       ssm_c_re[l], ssm_c_im[l], ssm_d[l], w_glu[l], b_glu[l])
        ssm = rms_norm(ssm, g_ssm_out[l])

        mix = jnp.concatenate([att, ssm], axis=-1) @ w_out[l]
        x = x + gt1 * rms_norm(mix, g_post_mix[l])

        h = rms_norm(x, g_pre_mlp[l]) * (1.0 + sc2) + sh2
        y = jnp.square(jax.nn.relu(h @ w_mlp_in[l])) @ w_mlp_out[l]
        x = x + gt2 * rms_norm(y, g_post_mlp[l])
    return x


import jax as _jax
import jax.numpy as _jnp

TWIN_FORMAT = 'train_step'
FWD_PARAMS = ['x', 'c', 'positions', 'w_ada', 'b_ada', 'g_pre_mix', 'g_post_mix', 'w_in', 'ssm_a_re', 'ssm_a_im', 'ssm_log_dt', 'ssm_b_re', 'ssm_b_im', 'ssm_c_re', 'ssm_c_im', 'ssm_d', 'w_glu', 'b_glu', 'g_attn_out', 'g_ssm_out', 'w_out', 'g_pre_mlp', 'g_post_mlp', 'w_mlp_in', 'w_mlp_out']
TWIN_WEIGHTS = ['w_ada', 'b_ada', 'g_pre_mix', 'g_post_mix', 'w_in', 'ssm_a_re', 'ssm_a_im', 'ssm_log_dt', 'ssm_b_re', 'ssm_b_im', 'ssm_c_re', 'ssm_c_im', 'ssm_d', 'w_glu', 'b_glu', 'g_attn_out', 'g_ssm_out', 'w_out', 'g_pre_mlp', 'g_post_mlp', 'w_mlp_in', 'w_mlp_out']
TWIN_DIFF_INPUT = 'x'
TWIN_INPUTS = ['x', 'c', 'positions', 'w_ada', 'b_ada', 'g_pre_mix', 'g_post_mix', 'w_in', 'ssm_a_re', 'ssm_a_im', 'ssm_log_dt', 'ssm_b_re', 'ssm_b_im', 'ssm_c_re', 'ssm_c_im', 'ssm_d', 'w_glu', 'b_glu', 'g_attn_out', 'g_ssm_out', 'w_out', 'g_pre_mlp', 'g_post_mlp', 'w_mlp_in', 'w_mlp_out', 'loss_target', 'm_w_ada', 'm_b_ada', 'm_g_pre_mix', 'm_g_post_mix', 'm_w_in', 'm_ssm_a_re', 'm_ssm_a_im', 'm_ssm_log_dt', 'm_ssm_b_re', 'm_ssm_b_im', 'm_ssm_c_re', 'm_ssm_c_im', 'm_ssm_d', 'm_w_glu', 'm_b_glu', 'm_g_attn_out', 'm_g_ssm_out', 'm_w_out', 'm_g_pre_mlp', 'm_g_post_mlp', 'm_w_mlp_in', 'm_w_mlp_out', 'v_w_ada', 'v_b_ada', 'v_g_pre_mix', 'v_g_post_mix', 'v_w_in', 'v_ssm_a_re', 'v_ssm_a_im', 'v_ssm_log_dt', 'v_ssm_b_re', 'v_ssm_b_im', 'v_ssm_c_re', 'v_ssm_c_im', 'v_ssm_d', 'v_w_glu', 'v_b_glu', 'v_g_attn_out', 'v_g_ssm_out', 'v_w_out', 'v_g_pre_mlp', 'v_g_post_mlp', 'v_w_mlp_in', 'v_w_mlp_out']
TWIN_OUTPUTS = ['loss', 'grad_x', 'grad_w_ada', 'grad_b_ada', 'grad_g_pre_mix', 'grad_g_post_mix', 'grad_w_in', 'grad_ssm_a_re', 'grad_ssm_a_im', 'grad_ssm_log_dt', 'grad_ssm_b_re', 'grad_ssm_b_im', 'grad_ssm_c_re', 'grad_ssm_c_im', 'grad_ssm_d', 'grad_w_glu', 'grad_b_glu', 'grad_g_attn_out', 'grad_g_ssm_out', 'grad_w_out', 'grad_g_pre_mlp', 'grad_g_post_mlp', 'grad_w_mlp_in', 'grad_w_mlp_out', 'delta_w_ada', 'delta_b_ada', 'delta_g_pre_mix', 'delta_g_post_mix', 'delta_w_in', 'delta_ssm_a_re', 'delta_ssm_a_im', 'delta_ssm_log_dt', 'delta_ssm_b_re', 'delta_ssm_b_im', 'delta_ssm_c_re', 'delta_ssm_c_im', 'delta_ssm_d', 'delta_w_glu', 'delta_b_glu', 'delta_g_attn_out', 'delta_g_ssm_out', 'delta_w_out', 'delta_g_pre_mlp', 'delta_g_post_mlp', 'delta_w_mlp_in', 'delta_w_mlp_out', 'new_m_w_ada', 'new_m_b_ada', 'new_m_g_pre_mix', 'new_m_g_post_mix', 'new_m_w_in', 'new_m_ssm_a_re', 'new_m_ssm_a_im', 'new_m_ssm_log_dt', 'new_m_ssm_b_re', 'new_m_ssm_b_im', 'new_m_ssm_c_re', 'new_m_ssm_c_im', 'new_m_ssm_d', 'new_m_w_glu', 'new_m_b_glu', 'new_m_g_attn_out', 'new_m_g_ssm_out', 'new_m_w_out', 'new_m_g_pre_mlp', 'new_m_g_post_mlp', 'new_m_w_mlp_in', 'new_m_w_mlp_out', 'new_v_w_ada', 'new_v_b_ada', 'new_v_g_pre_mix', 'new_v_g_post_mix', 'new_v_w_in', 'new_v_ssm_a_re', 'new_v_ssm_a_im', 'new_v_ssm_log_dt', 'new_v_ssm_b_re', 'new_v_ssm_b_im', 'new_v_ssm_c_re', 'new_v_ssm_c_im', 'new_v_ssm_d', 'new_v_w_glu', 'new_v_b_glu', 'new_v_g_attn_out', 'new_v_g_ssm_out', 'new_v_w_out', 'new_v_g_pre_mlp', 'new_v_g_post_mlp', 'new_v_w_mlp_in', 'new_v_w_mlp_out']
TWIN_LEAF_KINDS = {'loss': 'loss', 'grad_x': 'grad_x', 'grad_w_ada': 'grad_w', 'grad_b_ada': 'grad_w', 'grad_g_pre_mix': 'grad_w', 'grad_g_post_mix': 'grad_w', 'grad_w_in': 'grad_w', 'grad_ssm_a_re': 'grad_w', 'grad_ssm_a_im': 'grad_w', 'grad_ssm_log_dt': 'grad_w', 'grad_ssm_b_re': 'grad_w', 'grad_ssm_b_im': 'grad_w', 'grad_ssm_c_re': 'grad_w', 'grad_ssm_c_im': 'grad_w', 'grad_ssm_d': 'grad_w', 'grad_w_glu': 'grad_w', 'grad_b_glu': 'grad_w', 'grad_g_attn_out': 'grad_w', 'grad_g_ssm_out': 'grad_w', 'grad_w_out': 'grad_w', 'grad_g_pre_mlp': 'grad_w', 'grad_g_post_mlp': 'grad_w', 'grad_w_mlp_in': 'grad_w', 'grad_w_mlp_out': 'grad_w', 'delta_w_ada': 'delta_w', 'delta_b_ada': 'delta_w', 'delta_g_pre_mix': 'delta_w', 'delta_g_post_mix': 'delta_w', 'delta_w_in': 'delta_w', 'delta_ssm_a_re': 'delta_w', 'delta_ssm_a_im': 'delta_w', 'delta_ssm_log_dt': 'delta_w', 'delta_ssm_b_re': 'delta_w', 'delta_ssm_b_im': 'delta_w', 'delta_ssm_c_re': 'delta_w', 'delta_ssm_c_im': 'delta_w', 'delta_ssm_d': 'delta_w', 'delta_w_glu': 'delta_w', 'delta_b_glu': 'delta_w', 'delta_g_attn_out': 'delta_w', 'delta_g_ssm_out': 'delta_w', 'delta_w_out': 'delta_w', 'delta_g_pre_mlp': 'delta_w', 'delta_g_post_mlp': 'delta_w', 'delta_w_mlp_in': 'delta_w', 'delta_w_mlp_out': 'delta_w', 'new_m_w_ada': 'new_m', 'new_m_b_ada': 'new_m', 'new_m_g_pre_mix': 'new_m', 'new_m_g_post_mix': 'new_m', 'new_m_w_in': 'new_m', 'new_m_ssm_a_re': 'new_m', 'new_m_ssm_a_im': 'new_m', 'new_m_ssm_log_dt': 'new_m', 'new_m_ssm_b_re': 'new_m', 'new_m_ssm_b_im': 'new_m', 'new_m_ssm_c_re': 'new_m', 'new_m_ssm_c_im': 'new_m', 'new_m_ssm_d': 'new_m', 'new_m_w_glu': 'new_m', 'new_m_b_glu': 'new_m', 'new_m_g_attn_out': 'new_m', 'new_m_g_ssm_out': 'new_m', 'new_m_w_out': 'new_m', 'new_m_g_pre_mlp': 'new_m', 'new_m_g_post_mlp': 'new_m', 'new_m_w_mlp_in': 'new_m', 'new_m_w_mlp_out': 'new_m', 'new_v_w_ada': 'new_v', 'new_v_b_ada': 'new_v', 'new_v_g_pre_mix': 'new_v', 'new_v_g_post_mix': 'new_v', 'new_v_w_in': 'new_v', 'new_v_ssm_a_re': 'new_v', 'new_v_ssm_a_im': 'new_v', 'new_v_ssm_log_dt': 'new_v', 'new_v_ssm_b_re': 'new_v', 'new_v_ssm_b_im': 'new_v', 'new_v_ssm_c_re': 'new_v', 'new_v_ssm_c_im': 'new_v', 'new_v_ssm_d': 'new_v', 'new_v_w_glu': 'new_v', 'new_v_b_glu': 'new_v', 'new_v_g_attn_out': 'new_v', 'new_v_g_ssm_out': 'new_v', 'new_v_w_out': 'new_v', 'new_v_g_pre_mlp': 'new_v', 'new_v_g_post_mlp': 'new_v', 'new_v_w_mlp_in': 'new_v', 'new_v_w_mlp_out': 'new_v'}


def _forward(args):
    return _fwd_reference(*[args[k] for k in FWD_PARAMS])


def _output_shape():
    def fwd():
        inp = _fwd_setup_inputs(0)
        return _fwd_reference(*[inp[k] for k in FWD_PARAMS])
    out = _jax.eval_shape(fwd)
    return out.shape, out.dtype

N_MICROBATCH = 1
ADAM_LR = 0.001
ADAM_B1 = 0.9
ADAM_B2 = 0.999
ADAM_EPS = 1e-08
ADAM_WD = 0.01
ADAM_STEP = 10
PER_EXAMPLE_BATCH_AXIS = {'x': 0, 'c': 0, 'positions': 0, 'loss_target': 0}
SHARED_INPUTS = []
_WEIGHT_DTYPES = {'w_ada': _jnp.float32, 'b_ada': _jnp.float32, 'g_pre_mix': _jnp.float32, 'g_post_mix': _jnp.float32, 'w_in': _jnp.float32, 'ssm_a_re': _jnp.float32, 'ssm_a_im': _jnp.float32, 'ssm_log_dt': _jnp.float32, 'ssm_b_re': _jnp.float32, 'ssm_b_im': _jnp.float32, 'ssm_c_re': _jnp.float32, 'ssm_c_im': _jnp.float32, 'ssm_d': _jnp.float32, 'w_glu': _jnp.float32, 'b_glu': _jnp.float32, 'g_attn_out': _jnp.float32, 'g_ssm_out': _jnp.float32, 'w_out': _jnp.float32, 'g_pre_mlp': _jnp.float32, 'g_post_mlp': _jnp.float32, 'w_mlp_in': _jnp.float32, 'w_mlp_out': _jnp.float32}
MOMENT_SCALE = {'w_ada': 7.565907e-01, 'b_ada': 1.546624e+00, 'g_pre_mix': 4.797436e-02, 'g_post_mix': 1.801460e+00, 'w_in': 2.452067e-01, 'ssm_a_re': 1.418244e-02, 'ssm_a_im': 1.388698e-02, 'ssm_log_dt': 1.247828e+00, 'ssm_b_re': 1.009245e-02, 'ssm_b_im': 1.016029e-02, 'ssm_c_re': 2.061238e-02, 'ssm_c_im': 1.996923e-02, 'ssm_d': 3.571803e-01, 'w_glu': 5.346621e-02, 'b_glu': 1.439610e-01, 'g_attn_out': 6.359846e-01, 'g_ssm_out': 3.278955e-01, 'w_out': 3.778185e-01, 'g_pre_mlp': 6.748297e-02, 'g_post_mlp': 1.699325e+00, 'w_mlp_in': 5.246183e-02, 'w_mlp_out': 2.170353e-01}


def _to_microbatches(a, axis):
    t = _jnp.moveaxis(a, axis, 0)
    t = t.reshape((N_MICROBATCH, t.shape[0] // N_MICROBATCH) + t.shape[1:])
    return _jnp.moveaxis(t, 1, axis + 1)


def setup_inputs(seed: int = 0) -> dict:
    inp = _fwd_setup_inputs(seed)
    key = _jax.random.fold_in(_jax.random.key(seed), 7919)
    shape, _ = _output_shape()
    out = dict(inp)
    out["loss_target"] = _jax.random.normal(_jax.random.fold_in(key, 0), shape, _jnp.float32)
    for i, name in enumerate(TWIN_WEIGHTS):
        w = inp[name].astype(_jnp.float32)
        if MOMENT_SCALE is None:
            s = _jnp.sqrt(_jnp.mean(_jnp.square(w)) + 1e-30)
        else:
            s = MOMENT_SCALE[name]
        km, kv = _jax.random.split(_jax.random.fold_in(key, i + 1))
        out[name] = w
        out["m_" + name] = s * _jax.random.normal(km, w.shape, _jnp.float32)
        out["v_" + name] = (s * s) * _jax.random.uniform(kv, w.shape, _jnp.float32, 0.5, 1.5)
    if N_MICROBATCH > 1:
        for name, axis in PER_EXAMPLE_BATCH_AXIS.items():
            out[name] = _to_microbatches(out[name], axis)
    return {'x': out['x'], 'c': out['c'], 'positions': out['positions'], 'w_ada': out['w_ada'], 'b_ada': out['b_ada'], 'g_pre_mix': out['g_pre_mix'], 'g_post_mix': out['g_post_mix'], 'w_in': out['w_in'], 'ssm_a_re': out['ssm_a_re'], 'ssm_a_im': out['ssm_a_im'], 'ssm_log_dt': out['ssm_log_dt'], 'ssm_b_re': out['ssm_b_re'], 'ssm_b_im': out['ssm_b_im'], 'ssm_c_re': out['ssm_c_re'], 'ssm_c_im': out['ssm_c_im'], 'ssm_d': out['ssm_d'], 'w_glu': out['w_glu'], 'b_glu': out['b_glu'], 'g_attn_out': out['g_attn_out'], 'g_ssm_out': out['g_ssm_out'], 'w_out': out['w_out'], 'g_pre_mlp': out['g_pre_mlp'], 'g_post_mlp': out['g_post_mlp'], 'w_mlp_in': out['w_mlp_in'], 'w_mlp_out': out['w_mlp_out'], 'loss_target': out['loss_target'], 'm_w_ada': out['m_w_ada'], 'm_b_ada': out['m_b_ada'], 'm_g_pre_mix': out['m_g_pre_mix'], 'm_g_post_mix': out['m_g_post_mix'], 'm_w_in': out['m_w_in'], 'm_ssm_a_re': out['m_ssm_a_re'], 'm_ssm_a_im': out['m_ssm_a_im'], 'm_ssm_log_dt': out['m_ssm_log_dt'], 'm_ssm_b_re': out['m_ssm_b_re'], 'm_ssm_b_im': out['m_ssm_b_im'], 'm_ssm_c_re': out['m_ssm_c_re'], 'm_ssm_c_im': out['m_ssm_c_im'], 'm_ssm_d': out['m_ssm_d'], 'm_w_glu': out['m_w_glu'], 'm_b_glu': out['m_b_glu'], 'm_g_attn_out': out['m_g_attn_out'], 'm_g_ssm_out': out['m_g_ssm_out'], 'm_w_out': out['m_w_out'], 'm_g_pre_mlp': out['m_g_pre_mlp'], 'm_g_post_mlp': out['m_g_post_mlp'], 'm_w_mlp_in': out['m_w_mlp_in'], 'm_w_mlp_out': out['m_w_mlp_out'], 'v_w_ada': out['v_w_ada'], 'v_b_ada': out['v_b_ada'], 'v_g_pre_mix': out['v_g_pre_mix'], 'v_g_post_mix': out['v_g_post_mix'], 'v_w_in': out['v_w_in'], 'v_ssm_a_re': out['v_ssm_a_re'], 'v_ssm_a_im': out['v_ssm_a_im'], 'v_ssm_log_dt': out['v_ssm_log_dt'], 'v_ssm_b_re': out['v_ssm_b_re'], 'v_ssm_b_im': out['v_ssm_b_im'], 'v_ssm_c_re': out['v_ssm_c_re'], 'v_ssm_c_im': out['v_ssm_c_im'], 'v_ssm_d': out['v_ssm_d'], 'v_w_glu': out['v_w_glu'], 'v_b_glu': out['v_b_glu'], 'v_g_attn_out': out['v_g_attn_out'], 'v_g_ssm_out': out['v_g_ssm_out'], 'v_w_out': out['v_w_out'], 'v_g_pre_mlp': out['v_g_pre_mlp'], 'v_g_post_mlp': out['v_g_post_mlp'], 'v_w_mlp_in': out['v_w_mlp_in'], 'v_w_mlp_out': out['v_w_mlp_out']}


def _loss(weights, diff, rest, loss_target):
    with _jax.named_scope("forward"):
        args = {**rest, TWIN_DIFF_INPUT: diff, **{k: w.astype(_WEIGHT_DTYPES[k]) for k, w in weights.items()}}
        y = _forward(args)
    with _jax.named_scope("loss_head"):
        err = _jnp.square(y.astype(_jnp.float32) - loss_target)
        return 0.5 * _jnp.sum(_jnp.mean(err, axis=-1)) if err.ndim else 0.5 * err


def _adamw(w, g, m, v):
    m = ADAM_B1 * m + (1.0 - ADAM_B1) * g
    v = ADAM_B2 * v + (1.0 - ADAM_B2) * _jnp.square(g)
    m_hat = m / (1.0 - ADAM_B1 ** ADAM_STEP)
    v_hat = v / (1.0 - ADAM_B2 ** ADAM_STEP)
    delta = -ADAM_LR * (m_hat / (_jnp.sqrt(v_hat) + ADAM_EPS) + ADAM_WD * w)
    return delta, m, v


def reference(x, c, positions, w_ada, b_ada, g_pre_mix, g_post_mix, w_in, ssm_a_re, ssm_a_im, ssm_log_dt, ssm_b_re, ssm_b_im, ssm_c_re, ssm_c_im, ssm_d, w_glu, b_glu, g_attn_out, g_ssm_out, w_out, g_pre_mlp, g_post_mlp, w_mlp_in, w_mlp_out, loss_target, m_w_ada, m_b_ada, m_g_pre_mix, m_g_post_mix, m_w_in, m_ssm_a_re, m_ssm_a_im, m_ssm_log_dt, m_ssm_b_re, m_ssm_b_im, m_ssm_c_re, m_ssm_c_im, m_ssm_d, m_w_glu, m_b_glu, m_g_attn_out, m_g_ssm_out, m_w_out, m_g_pre_mlp, m_g_post_mlp, m_w_mlp_in, m_w_mlp_out, v_w_ada, v_b_ada, v_g_pre_mix, v_g_post_mix, v_w_in, v_ssm_a_re, v_ssm_a_im, v_ssm_log_dt, v_ssm_b_re, v_ssm_b_im, v_ssm_c_re, v_ssm_c_im, v_ssm_d, v_w_glu, v_b_glu, v_g_attn_out, v_g_ssm_out, v_w_out, v_g_pre_mlp, v_g_post_mlp, v_w_mlp_in, v_w_mlp_out):
    given = dict(x=x, c=c, positions=positions, w_ada=w_ada, b_ada=b_ada, g_pre_mix=g_pre_mix, g_post_mix=g_post_mix, w_in=w_in, ssm_a_re=ssm_a_re, ssm_a_im=ssm_a_im, ssm_log_dt=ssm_log_dt, ssm_b_re=ssm_b_re, ssm_b_im=ssm_b_im, ssm_c_re=ssm_c_re, ssm_c_im=ssm_c_im, ssm_d=ssm_d, w_glu=w_glu, b_glu=b_glu, g_attn_out=g_attn_out, g_ssm_out=g_ssm_out, w_out=w_out, g_pre_mlp=g_pre_mlp, g_post_mlp=g_post_mlp, w_mlp_in=w_mlp_in, w_mlp_out=w_mlp_out, loss_target=loss_target, m_w_ada=m_w_ada, m_b_ada=m_b_ada, m_g_pre_mix=m_g_pre_mix, m_g_post_mix=m_g_post_mix, m_w_in=m_w_in, m_ssm_a_re=m_ssm_a_re, m_ssm_a_im=m_ssm_a_im, m_ssm_log_dt=m_ssm_log_dt, m_ssm_b_re=m_ssm_b_re, m_ssm_b_im=m_ssm_b_im, m_ssm_c_re=m_ssm_c_re, m_ssm_c_im=m_ssm_c_im, m_ssm_d=m_ssm_d, m_w_glu=m_w_glu, m_b_glu=m_b_glu, m_g_attn_out=m_g_attn_out, m_g_ssm_out=m_g_ssm_out, m_w_out=m_w_out, m_g_pre_mlp=m_g_pre_mlp, m_g_post_mlp=m_g_post_mlp, m_w_mlp_in=m_w_mlp_in, m_w_mlp_out=m_w_mlp_out, v_w_ada=v_w_ada, v_b_ada=v_b_ada, v_g_pre_mix=v_g_pre_mix, v_g_post_mix=v_g_post_mix, v_w_in=v_w_in, v_ssm_a_re=v_ssm_a_re, v_ssm_a_im=v_ssm_a_im, v_ssm_log_dt=v_ssm_log_dt, v_ssm_b_re=v_ssm_b_re, v_ssm_b_im=v_ssm_b_im, v_ssm_c_re=v_ssm_c_re, v_ssm_c_im=v_ssm_c_im, v_ssm_d=v_ssm_d, v_w_glu=v_w_glu, v_b_glu=v_b_glu, v_g_attn_out=v_g_attn_out, v_g_ssm_out=v_g_ssm_out, v_w_out=v_w_out, v_g_pre_mlp=v_g_pre_mlp, v_g_post_mlp=v_g_post_mlp, v_w_mlp_in=v_w_mlp_in, v_w_mlp_out=v_w_mlp_out)
    weights = {n: given[n] for n in TWIN_WEIGHTS}
    shared = {n: given[n] for n in SHARED_INPUTS}
    per_example = {n: given[n] for n in ['x', 'c', 'positions']}
    grad_fn = _jax.value_and_grad(_loss, argnums=(0, 1))

    def one_microbatch(ex, loss_target):
        ex = dict(ex)
        diff = ex.pop(TWIN_DIFF_INPUT)
        return grad_fn(weights, diff, {**shared, **ex}, loss_target)

    if N_MICROBATCH == 1:
        loss, (grad_w, grad_x) = one_microbatch(per_example, given["loss_target"])
    else:
        def body(carry, xs):
            loss_sum, grad_sum = carry
            l_k, (gw_k, gx_k) = one_microbatch(xs[0], xs[1])
            with _jax.named_scope("update"):
                return (loss_sum + l_k, _jax.tree.map(_jnp.add, grad_sum, gw_k)), gx_k

        init = (_jnp.zeros((), _jnp.float32), _jax.tree.map(_jnp.zeros_like, weights))
        (loss, grad_w), grad_x = _jax.lax.scan(body, init, (per_example, given["loss_target"]))
    with _jax.named_scope("update"):
        delta_w, new_m, new_v = {}, {}, {}
        for n in TWIN_WEIGHTS:
            delta_w[n], new_m[n], new_v[n] = _adamw(weights[n], grad_w[n], given["m_" + n], given["v_" + n])
    return (loss, grad_x, *[grad_w[n] for n in TWIN_WEIGHTS], *[delta_w[n] for n in TWIN_WEIGHTS],
            *[new_m[n] for n in TWIN_WEIGHTS], *[new_v[n] for n in TWIN_WEIGHTS])
```

```python
import functools
import math

import jax
import jax.numpy as jnp
import numpy as np
from jax import lax
from jax.experimental import pallas as pl
from jax.experimental.pallas import tpu as pltpu

F32 = jnp.float32
BF16 = jnp.bfloat16

D_MODEL = 2048
HEAD_DIM = 64
DILATIONS = (1, 4, 16)
ATT_SPAN = 128
ATT_BLK = 128
HEADS_PER_GROUP = 6
KV_WIDTH = HEADS_PER_GROUP * HEAD_DIM
ATT_Q_WIDTH = 3 * KV_WIDTH
ROT_DIM = 16
ROPE_THETA = 500000.0
SSM_WIDTH = 896
SSM_P = 16
SSM_G = 56
SSM_N = 64
SSM_GN = SSM_G * SSM_N
SSM_TILES = SSM_WIDTH // 128
SSM_TILE_GN = 8 * SSM_N
IN_WIDTH = 2816
OUT_IN_WIDTH = 1280
D_FF = 8192
N_MOD = 6
EPS = 1e-6
LANES = 128
SSM_SEGS = 8
SSM_CHUNK = 256
SSM_SEG_LEN = SSM_CHUNK // SSM_SEGS

ADAM_LR = 0.001
ADAM_B1 = 0.9
ADAM_B2 = 0.999
ADAM_EPS = 1e-08
ADAM_WD = 0.01
ADAM_STEP = 10

VMEM_LIMIT = 56 * 1024 * 1024


def _params(sem=None):
    return pltpu.CompilerParams(dimension_semantics=sem, vmem_limit_bytes=VMEM_LIMIT)


def _dot(a, b, dims):
    return lax.dot_general(a, b, (dims, ((), ())), preferred_element_type=F32)


def _nn(a, b):
    return _dot(a, b, ((1,), (0,)))


def _nt(a, b):
    return _dot(a, b, ((1,), (1,)))


def _tn(a, b):
    return _dot(a, b, ((0,), (0,)))


def _matmul(a, b, mode, out_dtype, tm, tn, tk, name):
    if mode == "nn":
        (m, k), (_, n) = a.shape, b.shape
        a_spec = pl.BlockSpec((tm, tk), lambda i, j, kk: (i, kk))
        b_spec = pl.BlockSpec((tk, tn), lambda i, j, kk: (kk, j))
        op = _nn
    elif mode == "nt":
        (m, k), (n, _) = a.shape, b.shape
        a_spec = pl.BlockSpec((tm, tk), lambda i, j, kk: (i, kk))
        b_spec = pl.BlockSpec((tn, tk), lambda i, j, kk: (j, kk))
        op = _nt
    else:
        (k, m), (_, n) = a.shape, b.shape
        a_spec = pl.BlockSpec((tk, tm), lambda i, j, kk: (kk, i))
        b_spec = pl.BlockSpec((tk, tn), lambda i, j, kk: (kk, j))
        op = _tn
    assert m % tm == 0 and n % tn == 0 and k % tk == 0, (name, m, n, k)
    nk = k // tk

    def body(a_ref, b_ref, o_ref, acc_ref):
        kk = pl.program_id(2)

        @pl.when(kk == 0)
        def _():
            acc_ref[...] = jnp.zeros_like(acc_ref)

        acc_ref[...] += op(a_ref[...], b_ref[...])

        @pl.when(kk == nk - 1)
        def _():
            o_ref[...] = acc_ref[...].astype(o_ref.dtype)

    return pl.pallas_call(
        body,
        name=name,
        grid=(m // tm, n // tn, nk),
        in_specs=[a_spec, b_spec],
        out_specs=pl.BlockSpec((tm, tn), lambda i, j, kk: (i, j)),
        out_shape=jax.ShapeDtypeStruct((m, n), out_dtype),
        scratch_shapes=[pltpu.VMEM((tm, tn), F32)],
        compiler_params=_params(("parallel", "parallel", "arbitrary")),
    )(a, b)


def _rowwise(fn, rows, consts, out_rows, out_accs, tm, name):
    n_rows = rows[0].shape[0]
    assert n_rows % tm == 0
    nr, nc, no = len(rows), len(consts), len(out_rows)

    def body(*refs):
        r_in, c_in = refs[:nr], refs[nr:nr + nc]
        o_row, o_acc = refs[nr + nc:nr + nc + no], refs[nr + nc + no:]
        outs = fn([r[...] for r in r_in], [c[...] for c in c_in])
        assert len(outs) == len(o_row) + len(o_acc), name
        for ref, v in zip(o_row, outs[:no]):
            ref[...] = v.astype(ref.dtype)
        first = pl.program_id(0) == 0
        for ref, v in zip(o_acc, outs[no:]):
            @pl.when(first)
            def _(ref=ref, v=v):
                ref[...] = v.astype(F32)

            @pl.when(jnp.logical_not(first))
            def _(ref=ref, v=v):
                ref[...] += v.astype(F32)

    in_specs = [pl.BlockSpec((tm, r.shape[1]), lambda i: (i, 0)) for r in rows]
    in_specs += [pl.BlockSpec(c.shape, lambda i: (0, 0)) for c in consts]
    out_specs = [pl.BlockSpec((tm, w), lambda i: (i, 0)) for w, _ in out_rows]
    out_specs += [pl.BlockSpec(s, lambda i: (0, 0)) for s in out_accs]
    out_shape = [jax.ShapeDtypeStruct((n_rows, w), dt) for w, dt in out_rows]
    out_shape += [jax.ShapeDtypeStruct(s, F32) for s in out_accs]
    return pl.pallas_call(
        body,
        name=name,
        grid=(n_rows // tm,),
        in_specs=in_specs,
        out_specs=out_specs,
        out_shape=out_shape,
        compiler_params=_params(("arbitrary",)),
    )(*rows, *consts)


def _rms(x, g):
    return x * lax.rsqrt(jnp.mean(x * x, axis=-1, keepdims=True) + EPS) * g


def _mod_norm(x, g, sc, sh):
    return _rms(x, g) * (1.0 + sc) + sh


def _gelu(x):
    return 0.5 * x * (1.0 + jnp.tanh(math.sqrt(2.0 / math.pi) * (x + 0.044715 * (x * x * x))))


def _sigmoid(x):
    return 1.0 / (1.0 + jnp.exp(-x))


def _post_mix(x, mix, g_post, gt1, g_pre, sc2, sh2):
    x1 = x + gt1 * _rms(mix, g_post)
    return x1, _mod_norm(x1, g_pre, sc2, sh2)


def _att_mix(o0, o1, o2, l0, l1, l2, g):
    m = jnp.maximum(jnp.maximum(l0, l1), l2)
    e0, e1, e2 = jnp.exp(l0 - m), jnp.exp(l1 - m), jnp.exp(l2 - m)
    att = (e0 * o0 + e1 * o1 + e2 * o2) / (e0 + e1 + e2)
    return _rms(att, g)


def _glu_out(y2, z, g):
    return _rms(y2 * _sigmoid(z), g)


def _rope_tables(pos_col, freq_lane, name):
    n_rows = pos_col.shape[0]
    tm = 512

    def body(p_ref, f_ref, cos_ref, lo_ref, hi_ref):
        ang = p_ref[...].astype(F32) * f_ref[...]
        lane = lax.broadcasted_iota(jnp.int32, ang.shape, 1) % HEAD_DIM
        c, s = jnp.cos(ang), jnp.sin(ang)
        cos_ref[...] = jnp.where(lane < ROT_DIM, c, 1.0)
        lo_ref[...] = jnp.where(lane < ROT_DIM // 2, -s, 0.0)
        hi_ref[...] = jnp.where((lane >= ROT_DIM // 2) & (lane < ROT_DIM), s, 0.0)

    tab = jax.ShapeDtypeStruct((n_rows, LANES), F32)
    return pl.pallas_call(
        body,
        name=name,
        grid=(n_rows // tm,),
        in_specs=[pl.BlockSpec((tm, 1), lambda i: (i, 0)), pl.BlockSpec((1, LANES), lambda i: (0, 0))],
        out_specs=[pl.BlockSpec((tm, LANES), lambda i: (i, 0))] * 3,
        out_shape=[tab] * 3,
        compiler_params=_params(("parallel",)),
    )(pos_col, freq_lane)


def _rope(x, cos_t, lo_t, hi_t):
    half = ROT_DIM // 2
    return x * cos_t + pltpu.roll(x, LANES - half, 1) * lo_t + pltpu.roll(x, half, 1) * hi_t


def _rope_transposed(dy, cos_t, lo_t, hi_t):
    half = ROT_DIM // 2
    return dy * cos_t + pltpu.roll(dy * lo_t, half, 1) + pltpu.roll(dy * hi_t, LANES - half, 1)


def _att_masks(i, k0):
    q_pos = i * ATT_BLK + lax.broadcasted_iota(jnp.int32, (ATT_BLK, 2 * ATT_BLK), 0)
    k_pos = k0 + lax.broadcasted_iota(jnp.int32, (ATT_BLK, 2 * ATT_BLK), 1)
    dist = q_pos - k_pos
    return (dist >= 0) & (dist <= ATT_SPAN)


def _head_lane_masks():
    lane = lax.broadcasted_iota(jnp.int32, (1, LANES), 1)
    return lane < HEAD_DIM, lane >= HEAD_DIM


def _att_specs(gi, d, n):
    cols = IN_WIDTH // LANES
    qkv = [
        pl.BlockSpec((n, LANES), lambda r, hp: (0, r * cols + gi * 3 + hp)),
        pl.BlockSpec((n, LANES), lambda r, hp: (0, r * cols + 9 + hp)),
        pl.BlockSpec((n, LANES), lambda r, hp: (0, r * cols + 12 + hp)),
    ]
    tabs = [pl.BlockSpec((n, LANES), lambda r, hp: (0, r))] * 3
    head = pl.BlockSpec((n, LANES), lambda r, hp: (0, r * 3 + hp))
    return qkv, tabs, head


def _att_load(q_ref, k_ref, v_ref, cos_ref, lo_ref, hi_ref, qs, ks, vs):
    cos_t, lo_t, hi_t = cos_ref[...], lo_ref[...], hi_ref[...]
    qs[...] = (_rope(q_ref[...], cos_t, lo_t, hi_t) * (1.0 / math.sqrt(HEAD_DIM))).astype(BF16)
    ks[...] = _rope(k_ref[...], cos_t, lo_t, hi_t).astype(BF16)
    vs[...] = v_ref[...].astype(BF16)


def _att_fwd(proj, tabs, gi, name):
    n_rows = proj.shape[0]
    d = DILATIONS[gi]
    n = n_rows // d
    nb = n // ATT_BLK
    proj_v = proj.reshape(n, d * IN_WIDTH)
    tabs_v = [t.reshape(n, d * LANES) for t in tabs]

    def body(q_ref, k_ref, v_ref, cos_ref, lo_ref, hi_ref, o_ref, l_ref, qs, ks, vs):
        _att_load(q_ref, k_ref, v_ref, cos_ref, lo_ref, hi_ref, qs, ks, vs)
        m0, m1 = _head_lane_masks()

        def step(i, carry):
            k0 = pl.multiple_of(jnp.maximum(i - 1, 0) * ATT_BLK, ATT_BLK)
            q0 = pl.multiple_of(i * ATT_BLK, ATT_BLK)
            q = qs[pl.ds(q0, ATT_BLK), :]
            k = ks[pl.ds(k0, 2 * ATT_BLK), :]
            v = vs[pl.ds(k0, 2 * ATT_BLK), :]
            valid = _att_masks(i, k0)
            outs, lses = [], []
            for hm in (m0, m1):
                s = _nt(jnp.where(hm, q, jnp.zeros_like(q)), k)
                s = jnp.where(valid, s, -1e30)
                mx = jnp.max(s, axis=1, keepdims=True)
                p = jnp.exp(s - mx)
                den = jnp.sum(p, axis=1, keepdims=True)
                outs.append(_nn(p.astype(BF16), v) / den)
                lses.append(mx + jnp.log(den))
            o_ref[pl.ds(q0, ATT_BLK), :] = jnp.where(m0, outs[0], outs[1])
            l_ref[pl.ds(q0, ATT_BLK), :] = jnp.where(m0, lses[0], lses[1])
            return carry

        lax.fori_loop(0, nb, step, 0)

    qkv, tab_specs, head = _att_specs(gi, d, n)
    out = jax.ShapeDtypeStruct((n, d * KV_WIDTH), F32)
    o, l = pl.pallas_call(
        body,
        name=name,
        grid=(d, 3),
        in_specs=qkv + tab_specs,
        out_specs=[head, head],
        out_shape=[out, out],
        scratch_shapes=[pltpu.VMEM((n, LANES), BF16)] * 3,
        compiler_params=_params(("parallel", "parallel")),
    )(proj_v, proj_v, proj_v, *tabs_v)
    return o.reshape(n_rows, KV_WIDTH), l.reshape(n_rows, KV_WIDTH)


def _att_bwd(proj, tabs, o, l, do, dl, gi, name):
    n_rows = proj.shape[0]
    d = DILATIONS[gi]
    n = n_rows // d
    nb = n // ATT_BLK
    proj_v = proj.reshape(n, d * IN_WIDTH)
    tabs_v = [t.reshape(n, d * LANES) for t in tabs]
    heads_v = [t.reshape(n, d * KV_WIDTH) for t in (o, l, do, dl)]

    def body(q_ref, k_ref, v_ref, cos_ref, lo_ref, hi_ref, o_ref, l_ref, do_ref, dl_ref,
             dq_ref, dk_ref, dv_ref, qs, ks, vs, dk_acc, dv_acc):
        _att_load(q_ref, k_ref, v_ref, cos_ref, lo_ref, hi_ref, qs, ks, vs)
        dk_acc[...] = jnp.zeros_like(dk_acc)
        dv_acc[...] = jnp.zeros_like(dv_acc)
        m0, m1 = _head_lane_masks()

        def step(i, carry):
            k0 = pl.multiple_of(jnp.maximum(i - 1, 0) * ATT_BLK, ATT_BLK)
            q0 = pl.multiple_of(i * ATT_BLK, ATT_BLK)
            rows = pl.ds(q0, ATT_BLK)
            keys = pl.ds(k0, 2 * ATT_BLK)
            q, k, v = qs[rows, :], ks[keys, :], vs[keys, :]
            d_o, lse = do_ref[rows, :], l_ref[rows, :]
            o_do = o_ref[rows, :] * d_o
            d_l = dl_ref[rows, :]
            valid = _att_masks(i, k0)
            dq = jnp.zeros((ATT_BLK, LANES), F32)
            dk = jnp.zeros((2 * ATT_BLK, LANES), F32)
            dv = jnp.zeros((2 * ATT_BLK, LANES), F32)
            for hm in (m0, m1):
                qh, kh = jnp.where(hm, q, jnp.zeros_like(q)), jnp.where(hm, k, jnp.zeros_like(k))
                doh = jnp.where(hm, d_o, 0.0).astype(BF16)
                lse_h = jnp.max(jnp.where(hm, lse, -1e30), axis=1, keepdims=True)
                delta = jnp.sum(jnp.where(hm, o_do, 0.0), axis=1, keepdims=True)
                dlse = jnp.sum(jnp.where(hm, d_l, 0.0), axis=1, keepdims=True)
                s = jnp.where(valid, _nt(qh, k), -1e30)
                p = jnp.exp(s - lse_h)
                dv = dv + _tn(p.astype(BF16), doh)
                ds = (p * (_nt(doh, v) - delta + dlse)).astype(BF16)
                dq = dq + _nn(ds, kh)
                dk = dk + _tn(ds, qh)
            cos_t, lo_t, hi_t = cos_ref[rows, :], lo_ref[rows, :], hi_ref[rows, :]
            dq_ref[rows, :] = _rope_transposed(dq * (1.0 / math.sqrt(HEAD_DIM)), cos_t, lo_t, hi_t)
            dk_acc[keys, :] += dk
            dv_acc[keys, :] += dv
            return carry

        lax.fori_loop(0, nb, step, 0)
        dk_ref[...] = _rope_transposed(dk_acc[...], cos_ref[...], lo_ref[...], hi_ref[...])
        dv_ref[...] = dv_acc[...]

    qkv, tab_specs, head = _att_specs(gi, d, n)
    out = jax.ShapeDtypeStruct((n, d * KV_WIDTH), F32)
    res = pl.pallas_call(
        body,
        name=name,
        grid=(d, 3),
        in_specs=qkv + tab_specs + [head] * 4,
        out_specs=[head] * 3,
        out_shape=[out] * 3,
        scratch_shapes=[pltpu.VMEM((n, LANES), BF16)] * 3 + [pltpu.VMEM((n, LANES), F32)] * 2,
        compiler_params=_params(("parallel", "parallel")),
    )(proj_v, proj_v, proj_v, *tabs_v, *heads_v)
    return [t.reshape(n_rows, KV_WIDTH) for t in res]


def _expand_np():
    e = np.zeros((SSM_N, SSM_N * SSM_P), np.float32)
    for nn in range(SSM_N):
        e[nn, nn * SSM_P:(nn + 1) * SSM_P] = 1.0
    return e


def _ssm_prep_math(a_re, a_im, log_dt, b_re, b_im, expand):
    dt = jnp.exp(log_dt)
    mag = jnp.exp(a_re * dt)
    ab_re, ab_im = mag * jnp.cos(a_im * dt), mag * jnp.sin(a_im * dt)
    den = a_re * a_re + a_im * a_im
    num_re, num_im = ab_re - 1.0, ab_im
    co_re = (num_re * a_re + num_im * a_im) / den
    co_im = (num_im * a_re - num_re * a_im) / den
    hi = lax.Precision.HIGHEST
    co_re_x = jnp.dot(co_re, expand, precision=hi, preferred_element_type=F32)
    co_im_x = jnp.dot(co_im, expand, precision=hi, preferred_element_type=F32)
    bb_re = co_re_x * b_re - co_im_x * b_im
    bb_im = co_re_x * b_im + co_im_x * b_re
    return ab_re, ab_im, bb_re, bb_im


def _ssm_prep(a_re, a_im, log_dt, b_re, b_im, expand, name):
    def body(ar, ai, ld, br, bi, ex, o0, o1, o2, o3):
        outs = _ssm_prep_math(ar[...], ai[...], ld[...], br[...], bi[...], ex[...])
        for ref, v in zip((o0, o1, o2, o3), outs):
            ref[...] = v

    gn = jax.ShapeDtypeStruct((SSM_G, SSM_N), F32)
    gnp = jax.ShapeDtypeStruct((SSM_G, SSM_N * SSM_P), F32)
    return pl.pallas_call(body, name=name, out_shape=[gn, gn, gnp, gnp], compiler_params=_params())(
        a_re, a_im, log_dt, b_re, b_im, expand)


def _ssm_prep_bwd(a_re, a_im, log_dt, b_re, b_im, expand, cts, name):
    def body(ar, ai, ld, br, bi, ex, c0, c1, c2, c3, o0, o1, o2, o3, o4):
        ex_v = ex[...]
        _, vjp = jax.vjp(lambda *p: _ssm_prep_math(*p, ex_v), ar[...], ai[...], ld[...], br[...], bi[...])
        for ref, v in zip((o0, o1, o2, o3, o4), vjp((c0[...], c1[...], c2[...], c3[...]))):
            ref[...] = v

    gn = jax.ShapeDtypeStruct((SSM_G, SSM_N), F32)
    gnp = jax.ShapeDtypeStruct((SSM_G, SSM_N * SSM_P), F32)
    g1 = jax.ShapeDtypeStruct((SSM_G, 1), F32)
    return pl.pallas_call(body, name=name, out_shape=[gn, gn, g1, gnp, gnp], compiler_params=_params())(
        a_re, a_im, log_dt, b_re, b_im, expand, *cts)


def _block_diag_in(bb):
    t = bb.reshape(SSM_TILES, 8, SSM_N, SSM_P).transpose(0, 1, 3, 2)
    eye = jnp.eye(8, dtype=bb.dtype)
    return (t[:, :, :, None, :] * eye[None, :, None, :, None]).reshape(SSM_TILES, LANES, SSM_TILE_GN)


def _block_diag_in_grad(dblk):
    t = dblk.reshape(SSM_TILES, 8, SSM_P, 8, SSM_N)
    t = jnp.einsum("tapbn,ab->tapn", t, jnp.eye(8, dtype=dblk.dtype))
    return t.transpose(0, 1, 3, 2).reshape(SSM_G, SSM_N, SSM_P)


def _block_diag_out(cm):
    t = cm.reshape(SSM_TILES, 8, SSM_P, SSM_N).transpose(0, 1, 3, 2)
    eye = jnp.eye(8, dtype=cm.dtype)
    return (t[:, :, :, None, :] * eye[None, :, None, :, None]).reshape(SSM_TILES, SSM_TILE_GN, LANES)


def _block_diag_out_grad(dblk):
    t = dblk.reshape(SSM_TILES, 8, SSM_N, 8, SSM_P)
    t = jnp.einsum("tanbp,ab->tanp", t, jnp.eye(8, dtype=dblk.dtype))
    return t.transpose(0, 1, 3, 2).reshape(SSM_G, SSM_P, SSM_N)


def _cmul_add(a_re, a_im, s_re, s_im, b_re, b_im):
    return a_re * s_re - a_im * s_im + b_re, a_re * s_im + a_im * s_re + b_im


def _lane_tiled(a):
    return a.reshape(a.shape[0], SSM_TILES, LANES).transpose(1, 0, 2)


def _lane_untiled(a):
    return a.transpose(1, 0, 2).reshape(a.shape[1], SSM_WIDTH)


def _ssm_load_rows(src_ref, dst):
    for t in range(SSM_TILES):
        for i in range(SSM_SEG_LEN):
            dst[i * SSM_SEGS:(i + 1) * SSM_SEGS, t * LANES:(t + 1) * LANES] = (
                src_ref[t, pl.ds(i, SSM_SEGS, stride=SSM_SEG_LEN), :])


def _ssm_store_rows(src, dst_ref):
    for t in range(SSM_TILES):
        for i in range(SSM_SEG_LEN):
            dst_ref[t, pl.ds(i, SSM_SEGS, stride=SSM_SEG_LEN), :] = (
                src[i * SSM_SEGS:(i + 1) * SSM_SEGS, t * LANES:(t + 1) * LANES])


def _ssm_powers(ab_re_ref, ab_im_ref, pw_re, pw_im):
    a_re, a_im = ab_re_ref[...], ab_im_ref[...]
    p_re, p_im = a_re, a_im
    for i in range(SSM_SEG_LEN):
        pw_re[i:i + 1, :] = p_re
        pw_im[i:i + 1, :] = p_im
        p_re, p_im = _cmul_add(a_re, a_im, p_re, p_im, 0.0, 0.0)


def _ssm_input_proj(u_s, bblk_re_ref, bblk_im_ref, s_re, s_im):
    for t in range(SSM_TILES):
        ub = u_s[:, t * LANES:(t + 1) * LANES].astype(BF16)
        cols = slice(t * SSM_TILE_GN, (t + 1) * SSM_TILE_GN)
        s_re[:, cols] = _nn(ub, bblk_re_ref[t])
        s_im[:, cols] = _nn(ub, bblk_im_ref[t])


def _ssm_scan(ab_re_ref, ab_im_ref, s_re, s_im, init_re, init_im, conj, reverse):
    sign = -1.0 if conj else 1.0
    for t in range(SSM_TILES):
        cols = slice(t * SSM_TILE_GN, (t + 1) * SSM_TILE_GN)
        a_re = jnp.broadcast_to(ab_re_ref[:, cols], (SSM_SEGS, SSM_TILE_GN))
        a_im = jnp.broadcast_to(ab_im_ref[:, cols], (SSM_SEGS, SSM_TILE_GN)) * sign
        if init_re is None:
            st = (jnp.zeros((SSM_SEGS, SSM_TILE_GN), F32),) * 2
        else:
            st = (init_re[:, cols], init_im[:, cols])

        def step(i, st, cols=cols, a_re=a_re, a_im=a_im):
            idx = (SSM_SEG_LEN - 1 - i) if reverse else i
            rows = pl.ds(pl.multiple_of(idx * SSM_SEGS, SSM_SEGS), SSM_SEGS)
            n_re, n_im = _cmul_add(a_re, a_im, st[0], st[1], s_re[rows, cols], s_im[rows, cols])
            s_re[rows, cols] = n_re
            s_im[rows, cols] = n_im
            return n_re, n_im
        lax.fori_loop(0, SSM_SEG_LEN, step, st, unroll=4)


def _ssm_fixup(pw_re, pw_im, s_re, s_im, cin_re, cin_im, conj, reverse):
    sign = -1.0 if conj else 1.0
    c_re, c_im = cin_re[...], cin_im[...]

    def step(i, c):
        k = (SSM_SEG_LEN - 1 - i) if reverse else i
        rows = pl.ds(pl.multiple_of(i * SSM_SEGS, SSM_SEGS), SSM_SEGS)
        p_re = jnp.broadcast_to(pw_re[pl.ds(k, 1), :], (SSM_SEGS, SSM_GN))
        p_im = jnp.broadcast_to(pw_im[pl.ds(k, 1), :], (SSM_SEGS, SSM_GN)) * sign
        n_re, n_im = _cmul_add(p_re, p_im, c_re, c_im, s_re[rows, :], s_im[rows, :])
        s_re[rows, :] = n_re
        s_im[rows, :] = n_im
        return c
    lax.fori_loop(0, SSM_SEG_LEN, step, 0)


def _ssm_fwd(u, ab_re, ab_im, bblk_re, bblk_im, cblk_re, cblk_im, d_row, name):
    n_rows = u.shape[1]
    nchunk = n_rows // SSM_CHUNK
    last = SSM_SEG_LEN - 1

    def body(u_ref, ar_ref, ai_ref, br_ref, bi_ref, cr_ref, ci_ref, d_ref, y_ref, cin_re_ref, cin_im_ref,
             u_s, s_re, s_im, pw_re, pw_im, st_re, st_im):
        @pl.when(pl.program_id(0) == 0)
        def _():
            _ssm_powers(ar_ref, ai_ref, pw_re, pw_im)
            st_re[...] = jnp.zeros_like(st_re)
            st_im[...] = jnp.zeros_like(st_im)

        _ssm_load_rows(u_ref, u_s)
        _ssm_input_proj(u_s, br_ref, bi_ref, s_re, s_im)
        _ssm_scan(ar_ref, ai_ref, s_re, s_im, None, None, conj=False, reverse=False)
        p_re, p_im = pw_re[last:last + 1, :], pw_im[last:last + 1, :]
        c_re, c_im = st_re[...], st_im[...]
        for j in range(SSM_SEGS):
            cin_re_ref[j:j + 1, :] = c_re
            cin_im_ref[j:j + 1, :] = c_im
            row = last * SSM_SEGS + j
            c_re, c_im = _cmul_add(p_re, p_im, c_re, c_im, s_re[row:row + 1, :], s_im[row:row + 1, :])
        st_re[...] = c_re
        st_im[...] = c_im
        _ssm_fixup(pw_re, pw_im, s_re, s_im, cin_re_ref, cin_im_ref, conj=False, reverse=False)
        for t in range(SSM_TILES):
            cols = slice(t * SSM_TILE_GN, (t + 1) * SSM_TILE_GN)
            lanes = slice(t * LANES, (t + 1) * LANES)
            y = _nn(s_re[:, cols].astype(BF16), cr_ref[t]) - _nn(s_im[:, cols].astype(BF16), ci_ref[t])
            u_s[:, lanes] = y + d_ref[:, lanes] * u_s[:, lanes]
        _ssm_store_rows(u_s, y_ref)

    whole2 = lambda a: pl.BlockSpec(a.shape, lambda c: (0, 0))
    whole3 = lambda a: pl.BlockSpec(a.shape, lambda c: (0, 0, 0))
    chunk = pl.BlockSpec((SSM_TILES, SSM_CHUNK, LANES), lambda c: (0, c, 0))
    seg = pl.BlockSpec((SSM_SEGS, SSM_GN), lambda c: (c, 0))
    seg_shape = jax.ShapeDtypeStruct((nchunk * SSM_SEGS, SSM_GN), F32)
    return pl.pallas_call(
        body,
        name=name,
        grid=(nchunk,),
        in_specs=[chunk, whole2(ab_re), whole2(ab_im), whole3(bblk_re), whole3(bblk_im), whole3(cblk_re),
                  whole3(cblk_im), whole2(d_row)],
        out_specs=[chunk, seg, seg],
        out_shape=[jax.ShapeDtypeStruct((SSM_TILES, n_rows, LANES), F32), seg_shape, seg_shape],
        scratch_shapes=[pltpu.VMEM((SSM_CHUNK, SSM_WIDTH), F32), pltpu.VMEM((SSM_CHUNK, SSM_GN), F32),
                        pltpu.VMEM((SSM_CHUNK, SSM_GN), F32), pltpu.VMEM((SSM_SEG_LEN, SSM_GN), F32),
                        pltpu.VMEM((SSM_SEG_LEN, SSM_GN), F32), pltpu.VMEM((1, SSM_GN), F32),
                        pltpu.VMEM((1, SSM_GN), F32)],
        compiler_params=_params(("arbitrary",)),
    )(u, ab_re, ab_im, bblk_re, bblk_im, cblk_re, cblk_im, d_row)


def _ssm_bwd(u, dy, cin_re, cin_im, ab_re, ab_im, bblk_re, bblk_im, cblk_re, cblk_im, d_row, name):
    n_rows = u.shape[1]
    nchunk = n_rows // SSM_CHUNK

    def body(u_ref, dy_ref, cin_re_ref, cin_im_ref, ar_ref, ai_ref, br_ref, bi_ref, cr_ref, ci_ref, d_ref,
             du_ref, dar_ref, dai_ref, dbr_ref, dbi_ref, dcr_ref, dci_ref, dd_ref,
             u_s, dy_s, s_re, s_im, q_re, q_im, pw_re, pw_im, qst_re, qst_im, qin_re, qin_im):
        @pl.when(pl.program_id(0) == 0)
        def _():
            _ssm_powers(ar_ref, ai_ref, pw_re, pw_im)
            qst_re[...] = jnp.zeros_like(qst_re)
            qst_im[...] = jnp.zeros_like(qst_im)
            for ref in (dar_ref, dai_ref, dbr_ref, dbi_ref, dcr_ref, dci_ref, dd_ref):
                ref[...] = jnp.zeros_like(ref)

        _ssm_load_rows(u_ref, u_s)
        _ssm_load_rows(dy_ref, dy_s)
        _ssm_input_proj(u_s, br_ref, bi_ref, s_re, s_im)
        _ssm_scan(ar_ref, ai_ref, s_re, s_im, cin_re_ref, cin_im_ref, conj=False, reverse=False)
        for t in range(SSM_TILES):
            cols = slice(t * SSM_TILE_GN, (t + 1) * SSM_TILE_GN)
            dyb = dy_s[:, t * LANES:(t + 1) * LANES].astype(BF16)
            q_re[:, cols] = _nt(dyb, cr_ref[t])
            q_im[:, cols] = -_nt(dyb, ci_ref[t])
            dcr_ref[t] += _tn(s_re[:, cols].astype(BF16), dyb)
            dci_ref[t] -= _tn(s_im[:, cols].astype(BF16), dyb)
        _ssm_scan(ar_ref, ai_ref, q_re, q_im, None, None, conj=True, reverse=True)
        last = SSM_SEG_LEN - 1
        p_re, p_im = pw_re[last:last + 1, :], -pw_im[last:last + 1, :]
        c_re, c_im = qst_re[...], qst_im[...]
        for j in reversed(range(SSM_SEGS)):
            qin_re[j:j + 1, :] = c_re
            qin_im[j:j + 1, :] = c_im
            c_re, c_im = _cmul_add(p_re, p_im, c_re, c_im, q_re[j:j + 1, :], q_im[j:j + 1, :])
        qst_re[...] = c_re
        qst_im[...] = c_im
        _ssm_fixup(pw_re, pw_im, q_re, q_im, qin_re, qin_im, conj=True, reverse=True)
        for t in range(SSM_TILES):
            cols = slice(t * SSM_TILE_GN, (t + 1) * SSM_TILE_GN)

            def step(i, acc, cols=cols):
                rows = pl.ds(pl.multiple_of(i * SSM_SEGS, SSM_SEGS), SSM_SEGS)
                prev = pl.ds(pl.multiple_of((i - 1) * SSM_SEGS, SSM_SEGS), SSM_SEGS)
                qr, qi = q_re[rows, cols], q_im[rows, cols]
                sr, si = s_re[prev, cols], s_im[prev, cols]
                return acc[0] + qr * sr + qi * si, acc[1] + qi * sr - qr * si

            qr, qi = q_re[0:SSM_SEGS, cols], q_im[0:SSM_SEGS, cols]
            sr, si = cin_re_ref[:, cols], cin_im_ref[:, cols]
            acc = lax.fori_loop(1, SSM_SEG_LEN, step, (qr * sr + qi * si, qi * sr - qr * si))
            dar_ref[:, cols] += jnp.sum(acc[0], axis=0, keepdims=True)
            dai_ref[:, cols] += jnp.sum(acc[1], axis=0, keepdims=True)
        for t in range(SSM_TILES):
            cols = slice(t * SSM_TILE_GN, (t + 1) * SSM_TILE_GN)
            lanes = slice(t * LANES, (t + 1) * LANES)
            qrb, qib = q_re[:, cols].astype(BF16), q_im[:, cols].astype(BF16)
            u_t, dy_t = u_s[:, lanes], dy_s[:, lanes]
            ub = u_t.astype(BF16)
            dbr_ref[t] += _tn(ub, qrb)
            dbi_ref[t] += _tn(ub, qib)
            dd_ref[:, lanes] += jnp.sum(dy_t * u_t, axis=0, keepdims=True)
            u_s[:, lanes] = _nt(qrb, br_ref[t]) + _nt(qib, bi_ref[t]) + dy_t * d_ref[:, lanes]
        _ssm_store_rows(u_s, du_ref)

    whole2 = lambda a: pl.BlockSpec(a.shape, lambda c: (0, 0))
    whole3 = lambda a: pl.BlockSpec(a.shape, lambda c: (0, 0, 0))
    chunk = pl.BlockSpec((SSM_TILES, SSM_CHUNK, LANES), lambda c: (0, nchunk - 1 - c, 0))
    seg = pl.BlockSpec((SSM_SEGS, SSM_GN), lambda c: (nchunk - 1 - c, 0))
    gn_row = jax.ShapeDtypeStruct((1, SSM_GN), F32)
    b_shape = jax.ShapeDtypeStruct((SSM_TILES, LANES, SSM_TILE_GN), F32)
    c_shape = jax.ShapeDtypeStruct((SSM_TILES, SSM_TILE_GN, LANES), F32)
    d_shape = jax.ShapeDtypeStruct((1, SSM_WIDTH), F32)
    big = pltpu.VMEM((SSM_CHUNK, SSM_GN), F32)
    return pl.pallas_call(
        body,
        name=name,
        grid=(nchunk,),
        in_specs=[chunk, chunk, seg, seg, whole2(ab_re), whole2(ab_im), whole3(bblk_re), whole3(bblk_im),
                  whole3(cblk_re), whole3(cblk_im), whole2(d_row)],
        out_specs=[chunk, whole2(ab_re), whole2(ab_im), whole3(bblk_re), whole3(bblk_im), whole3(cblk_re),
                   whole3(cblk_im), whole2(d_row)],
        out_shape=[jax.ShapeDtypeStruct((SSM_TILES, n_rows, LANES), F32), gn_row, gn_row, b_shape, b_shape, c_shape,
                   c_shape, d_shape],
        scratch_shapes=[pltpu.VMEM((SSM_CHUNK, SSM_WIDTH), F32), pltpu.VMEM((SSM_CHUNK, SSM_WIDTH), F32),
                        big, big, big, big,
                        pltpu.VMEM((SSM_SEG_LEN, SSM_GN), F32), pltpu.VMEM((SSM_SEG_LEN, SSM_GN), F32),
                        pltpu.VMEM((1, SSM_GN), F32), pltpu.VMEM((1, SSM_GN), F32),
                        pltpu.VMEM((SSM_SEGS, SSM_GN), F32), pltpu.VMEM((SSM_SEGS, SSM_GN), F32)],
        compiler_params=_params(("arbitrary",)),
    )(u, dy, cin_re, cin_im, ab_re, ab_im, bblk_re, bblk_im, cblk_re, cblk_im, d_row)


def _mlp_fwd(h2, w1, w2, tm, tf, name):
    n_rows, dm = h2.shape
    dff = w1.shape[1]

    def body(h_ref, w1_ref, w2_ref, a_ref, y_ref):
        a = _nn(h_ref[...], w1_ref[...])
        a_ref[...] = a.astype(BF16)
        r = jnp.maximum(a, 0.0)
        part = _nn((r * r).astype(BF16), w2_ref[...])
        j = pl.program_id(1)

        @pl.when(j == 0)
        def _():
            y_ref[...] = part

        @pl.when(j > 0)
        def _():
            y_ref[...] += part

    return pl.pallas_call(
        body,
        name=name,
        grid=(n_rows // tm, dff // tf),
        in_specs=[pl.BlockSpec((tm, dm), lambda i, j: (i, 0)), pl.BlockSpec((dm, tf), lambda i, j: (0, j)),
                  pl.BlockSpec((tf, dm), lambda i, j: (j, 0))],
        out_specs=[pl.BlockSpec((tm, tf), lambda i, j: (i, j)), pl.BlockSpec((tm, dm), lambda i, j: (i, 0))],
        out_shape=[jax.ShapeDtypeStruct((n_rows, dff), BF16), jax.ShapeDtypeStruct((n_rows, dm), F32)],
        compiler_params=_params(("parallel", "arbitrary")),
    )(h2, w1, w2)


def _mlp_bwd(dy, h2, a, w2, tm, tf, name):
    n_rows, dm = h2.shape
    dff = a.shape[1]

    def body(dy_ref, h_ref, a_ref, w2_ref, da_ref, dw2_ref, dw1_ref):
        dyb = dy_ref[...]
        r = jnp.maximum(a_ref[...].astype(F32), 0.0)
        da = (_nt(dyb, w2_ref[...]) * (2.0 * r)).astype(BF16)
        da_ref[...] = da
        p2 = _tn((r * r).astype(BF16), dyb)
        p1 = _tn(h_ref[...], da)
        i = pl.program_id(1)

        @pl.when(i == 0)
        def _():
            dw2_ref[...] = p2
            dw1_ref[...] = p1

        @pl.when(i > 0)
        def _():
            dw2_ref[...] += p2
            dw1_ref[...] += p1

    return pl.pallas_call(
        body,
        name=name,
        grid=(dff // tf, n_rows // tm),
        in_specs=[pl.BlockSpec((tm, dm), lambda j, i: (i, 0)), pl.BlockSpec((tm, dm), lambda j, i: (i, 0)),
                  pl.BlockSpec((tm, tf), lambda j, i: (i, j)), pl.BlockSpec((tf, dm), lambda j, i: (j, 0))],
        out_specs=[pl.BlockSpec((tm, tf), lambda j, i: (i, j)), pl.BlockSpec((tf, dm), lambda j, i: (j, 0)),
                   pl.BlockSpec((dm, tf), lambda j, i: (0, j))],
        out_shape=[jax.ShapeDtypeStruct((n_rows, dff), BF16), jax.ShapeDtypeStruct((dff, dm), F32),
                   jax.ShapeDtypeStruct((dm, dff), F32)],
        compiler_params=_params(("parallel", "arbitrary")),
    )(dy, h2, a, w2)


def _local_step(x, pos_col, mod, target, wts, small):
    n_rows = x.shape[0]
    sh1, sc1, gt1, sh2, sc2, gt2 = (mod[:, i * D_MODEL:(i + 1) * D_MODEL] for i in range(N_MOD))
    tm = 256
    d_acc = (1, D_MODEL)

    (h1,) = _rowwise(lambda r, c: [_mod_norm(r[0], *c)], [x], [small["g_pre_mix"], sc1, sh1],
                     [(D_MODEL, BF16)], [], tm, "pre_mix_fwd")
    proj = _matmul(h1, wts["w_in"], "nn", F32, 512, 1408, 2048, "in_proj")

    freqs = ROPE_THETA ** (-jnp.arange(0, ROT_DIM, 2, dtype=F32) / ROT_DIM)
    freq_lane = jnp.tile(freqs, LANES // (ROT_DIM // 2))[None, :]
    tabs = _rope_tables(pos_col, freq_lane, "rope_tables")
    att = [_att_fwd(proj, tabs, gi, f"att_fwd_{gi}") for gi in range(3)]

    expand = jnp.asarray(_expand_np())
    b_re2, b_im2 = small["ssm_b_re"].reshape(SSM_G, -1), small["ssm_b_im"].reshape(SSM_G, -1)
    log_dt = small["ssm_log_dt"].reshape(SSM_G, 1)
    prep_in = (small["ssm_a_re"], small["ssm_a_im"], log_dt, b_re2, b_im2, expand)
    ab_re, ab_im, bb_re, bb_im = _ssm_prep(*prep_in, "ssm_prep")
    ab_re_row, ab_im_row = ab_re.reshape(1, SSM_GN), ab_im.reshape(1, SSM_GN)
    bblk = [_block_diag_in(t.reshape(SSM_G, SSM_N, SSM_P)).astype(BF16) for t in (bb_re, bb_im)]
    cblk = [_block_diag_out(small[k]).astype(BF16) for k in ("ssm_c_re", "ssm_c_im")]
    d_row = small["ssm_d"].reshape(1, SSM_WIDTH)
    u = _lane_tiled(proj[:, IN_WIDTH - SSM_WIDTH:])
    y_ssm, cin_re, cin_im = _ssm_fwd(u, ab_re_row, ab_im_row, *bblk, *cblk, d_row, "ssm_fwd")
    y_ssm = _lane_untiled(y_ssm)

    def mixers_out(r, c):
        w_glu, b_glu, g_att, g_ssm = c
        att_n = _att_mix(*r[:6], g_att)
        y2 = _gelu(r[6])
        z = _nn(y2.astype(BF16), w_glu) + b_glu
        return [jnp.concatenate([att_n.astype(BF16), _glu_out(y2, z, g_ssm).astype(BF16)], axis=1)]

    att_rows = [a[0] for a in att] + [a[1] for a in att]
    mix_consts = [wts["w_glu"], small["b_glu"], small["g_attn_out"], small["g_ssm_out"]]
    (cat,) = _rowwise(mixers_out, att_rows + [y_ssm], mix_consts, [(OUT_IN_WIDTH, BF16)], [], tm, "mixers_out_fwd")
    mix = _matmul(cat, wts["w_out"], "nn", F32, 512, 1024, 1280, "out_proj")

    post_consts = [small["g_post_mix"], gt1, small["g_pre_mlp"], sc2, sh2]
    x1, h2 = _rowwise(lambda r, c: list(_post_mix(r[0], r[1], *c)), [x, mix], post_consts,
                      [(D_MODEL, F32), (D_MODEL, BF16)], [], tm, "post_mix_fwd")
    a_mlp, y_mlp = _mlp_fwd(h2, wts["w_mlp_in"], wts["w_mlp_out"], 512, 512, "mlp_fwd")

    def loss_head(r, c):
        x1_v, y_v, t_v = r
        g, gt = c
        fn = lambda y_, g_, gt_: gt_ * _rms(y_, g_)
        out, vjp = jax.vjp(fn, y_v, g, gt)
        err = x1_v + out - t_v
        dx2 = err * (1.0 / D_MODEL)
        dy, dg, dgt = vjp(dx2)
        loss = 0.5 * jnp.sum(jnp.sum(err * err, axis=1, keepdims=True), axis=0, keepdims=True) * (1.0 / D_MODEL)
        return [dx2, dy, loss, dg, dgt]

    dx2, dy_mlp, loss, dg_post_mlp, dgt2 = _rowwise(
        loss_head, [x1, y_mlp, target], [small["g_post_mlp"], gt2],
        [(D_MODEL, F32), (D_MODEL, BF16)], [(1, 1), d_acc, d_acc], tm, "loss_head")

    da_mlp, dw_mlp_out, dw_mlp_in = _mlp_bwd(dy_mlp, h2, a_mlp, wts["w_mlp_out"], 512, 512, "mlp_bwd")
    dh2 = _matmul(da_mlp, wts["w_mlp_in"], "nt", F32, 512, 1024, 2048, "mlp_in_bwd")

    def post_mix_bwd(r, c):
        x_v, mix_v, dx1_v, dh2_v = r
        _, vjp = jax.vjp(_post_mix, x_v, mix_v, *c)
        return list(vjp((dx1_v, dh2_v)))

    dx_a, dmix, dg_post_mix, dgt1, dg_pre_mlp, dsc2, dsh2 = _rowwise(
        post_mix_bwd, [x, mix, dx2, dh2], post_consts, [(D_MODEL, F32), (D_MODEL, BF16)], [d_acc] * 5, tm,
        "post_mix_bwd")

    dcat = _matmul(dmix, wts["w_out"], "nt", F32, 512, 1280, 2048, "out_proj_bwd")
    dw_out = _matmul(cat, dmix, "tn", F32, 1280, 1024, 512, "out_proj_wgrad")

    def mixers_out_bwd(r, c):
        w_glu, b_glu, g_att, g_ssm = c
        dcat_v = r[7]
        _, vjp_att = jax.vjp(_att_mix, *r[:6], g_att)
        *d_ol, dg_att = vjp_att(dcat_v[:, :KV_WIDTH])
        y2, vjp_gelu = jax.vjp(_gelu, r[6])
        y2b = y2.astype(BF16)
        z = _nn(y2b, w_glu) + b_glu
        _, vjp_glu = jax.vjp(_glu_out, y2, z, g_ssm)
        dy2, dz, dg_ssm = vjp_glu(dcat_v[:, KV_WIDTH:])
        dzb = dz.astype(BF16)
        (dy,) = vjp_gelu(dy2 + _nt(dzb, w_glu))
        return d_ol + [dy, dg_att, _tn(y2b, dzb), jnp.sum(dz, axis=0, keepdims=True), dg_ssm]

    *d_att, dy_ssm, dg_attn_out, dw_glu, db_glu, dg_ssm_out = _rowwise(
        mixers_out_bwd, att_rows + [y_ssm, dcat], mix_consts,
        [(KV_WIDTH, F32)] * 6 + [(SSM_WIDTH, F32)],
        [(1, KV_WIDTH), (SSM_WIDTH, SSM_WIDTH), (1, SSM_WIDTH), (1, SSM_WIDTH)], tm, "mixers_out_bwd")

    du, dab_re, dab_im, dbblk_re, dbblk_im, dcblk_re, dcblk_im, dd_row = _ssm_bwd(
        u, _lane_tiled(dy_ssm), cin_re, cin_im, ab_re_row, ab_im_row, *bblk, *cblk, d_row, "ssm_bwd")
    du = _lane_untiled(du)
    prep_cts = (dab_re.reshape(SSM_G, SSM_N), dab_im.reshape(SSM_G, SSM_N),
                _block_diag_in_grad(dbblk_re).reshape(SSM_G, -1), _block_diag_in_grad(dbblk_im).reshape(SSM_G, -1))
    da_re, da_im, dlog_dt, db_re, db_im = _ssm_prep_bwd(*prep_in, prep_cts, "ssm_prep_bwd")

    dqkv = [_att_bwd(proj, tabs, att[gi][0], att[gi][1], d_att[gi], d_att[3 + gi], gi, f"att_bwd_{gi}")
            for gi in range(3)]

    def gather_dproj(r, c):
        dq = [r[0], r[3], r[6]]
        dk = r[1] + r[4] + r[7]
        dv = r[2] + r[5] + r[8]
        return [jnp.concatenate([t.astype(BF16) for t in dq + [dk, dv, r[9]]], axis=1)]

    (dproj,) = _rowwise(gather_dproj, [t for g in dqkv for t in g] + [du], [], [(IN_WIDTH, BF16)], [], tm,
                        "gather_dproj")
    dh1 = _matmul(dproj, wts["w_in"], "nt", F32, 512, 1024, 2816, "in_proj_bwd")
    dw_in = _matmul(h1, dproj, "tn", F32, 1024, 1408, 512, "in_proj_wgrad")

    def pre_mix_bwd(r, c):
        x_v, dh1_v, dxa_v = r
        _, vjp = jax.vjp(_mod_norm, x_v, *c)
        dx, dg, dsc, dsh = vjp(dh1_v)
        return [dx + dxa_v, dg, dsc, dsh]

    grad_x, dg_pre_mix, dsc1, dsh1 = _rowwise(
        pre_mix_bwd, [x, dh1, dx_a], [small["g_pre_mix"], sc1, sh1], [(D_MODEL, F32)], [d_acc] * 3, tm, "pre_mix_bwd")

    dmod = jnp.concatenate([dsh1, dsc1, dgt1, dsh2, dsc2, dgt2], axis=1)
    big = dict(w_in=dw_in, w_out=dw_out, w_mlp_in=dw_mlp_in, w_mlp_out=dw_mlp_out, w_glu=dw_glu)
    small_g = dict(
        g_pre_mix=dg_pre_mix, g_post_mix=dg_post_mix, ssm_a_re=da_re, ssm_a_im=da_im,
        ssm_log_dt=dlog_dt.reshape(1, SSM_G), ssm_b_re=db_re.reshape(SSM_G, SSM_N, SSM_P),
        ssm_b_im=db_im.reshape(SSM_G, SSM_N, SSM_P), ssm_c_re=_block_diag_out_grad(dcblk_re),
        ssm_c_im=_block_diag_out_grad(dcblk_im), ssm_d=dd_row.reshape(SSM_G, SSM_P), b_glu=db_glu,
        g_attn_out=dg_attn_out, g_ssm_out=dg_ssm_out, g_pre_mlp=dg_pre_mlp, g_post_mlp=dg_post_mlp)
    return loss, grad_x, dmod, big, small_g


MESH_ID = pl.DeviceIdType.MESH
N_DEV = 8
N_CHIPS = 4
HBM_SPEC = pl.BlockSpec(memory_space=pltpu.HBM)


def _place():
    x, y, c = lax.axis_index("x"), lax.axis_index("y"), lax.axis_index("c")
    other_chips = [(1 - x, y), (x, 1 - y), (1 - x, 1 - y)]
    return x, y, c, other_chips


def _half_rows(index, half):
    return pl.ds(pl.multiple_of(index * half, ROW_PAD), half)


def _remote(src, dst, send_sem, recv_sem, dev):
    return pltpu.make_async_remote_copy(src_ref=src, dst_ref=dst, send_sem=send_sem, recv_sem=recv_sem,
                                        device_id=dev, device_id_type=MESH_ID)


def _all_gather8(block, name):
    m_per, n = block.shape

    def body(x_ref, out_ref, send_sems, recv_sems, local_sem):
        x, y, c, chips = _place()
        me, sibling = (x, y, c), (x, y, 1 - c)

        def rows(px, py, pc):
            return out_ref.at[pl.ds((4 * px + 2 * py + pc) * m_per, m_per), :]

        def copy(k, blk, to, src=None):
            return _remote(rows(*blk) if src is None else src, rows(*blk), send_sems.at[k], recv_sems.at[k], to)

        mine = pltpu.make_async_copy(x_ref, rows(*me), local_sem)
        mine.start()
        first = [copy(0, me, sibling, src=x_ref)]
        first += [copy(1 + j, me, (*chip, c), src=x_ref) for j, chip in enumerate(chips)]
        for cp in first:
            cp.start()
        passed = [copy(4 + j, (*chip, c), sibling) for j, chip in enumerate(chips)]
        for j, chip in enumerate(chips):
            copy(1 + j, (*chip, c), me).wait_recv()
            passed[j].start()
        copy(0, sibling, me).wait_recv()
        for j, chip in enumerate(chips):
            copy(4 + j, (*chip, 1 - c), me).wait_recv()
        for cp in first + passed:
            cp.wait_send()
        mine.wait()

    return pl.pallas_call(
        body,
        name=name,
        out_shape=jax.ShapeDtypeStruct((N_DEV * m_per, n), block.dtype),
        in_specs=[pl.BlockSpec(memory_space=pltpu.VMEM)],
        out_specs=pl.BlockSpec(memory_space=pltpu.VMEM),
        scratch_shapes=[pltpu.SemaphoreType.DMA((7,)), pltpu.SemaphoreType.DMA((7,)), pltpu.SemaphoreType.DMA],
        compiler_params=_params(),
    )(block)


def _weight_gather(shards, name):
    n = len(shards)
    shapes = [s.shape for s in shards]

    def body(*refs):
        ins, outs = refs[:n], refs[n:2 * n]
        send, recv, fsend, frecv, lsem = refs[2 * n:]
        x, y, c, chips = _place()
        k_me = 2 * x + y
        sibling = (x, y, 1 - c)
        pending = []
        for a in range(n):
            half = shapes[a][0] // 2
            loc = pltpu.make_async_copy(ins[a], outs[a].at[k_me], lsem.at[a])
            loc.start()
            pending.append(loc.wait)
            mine = _half_rows(c, half)
            for j, chip in enumerate(chips):
                cp = _remote(ins[a].at[mine, :], outs[a].at[k_me, mine, :], send.at[a, j], recv.at[a, j], (*chip, c))
                cp.start()
                pending.append(cp.wait_send)
        for a in range(n):
            half = shapes[a][0] // 2
            for j, (px, py) in enumerate(chips):
                piece = outs[a].at[2 * px + py, _half_rows(c, half), :]
                _remote(piece, piece, send.at[a, j], recv.at[a, j], (px, py, c)).wait_recv()
                fw = _remote(piece, piece, fsend.at[a, j], frecv.at[a, j], sibling)
                fw.start()
                pending.append(fw.wait_send)
        for a in range(n):
            half = shapes[a][0] // 2
            for j, (px, py) in enumerate(chips):
                piece = outs[a].at[2 * px + py, _half_rows(1 - c, half), :]
                _remote(piece, piece, fsend.at[a, j], frecv.at[a, j], sibling).wait_recv()
        for wait in pending:
            wait()

    sems = pltpu.SemaphoreType.DMA((n, 3))
    return pl.pallas_call(
        body,
        name=name,
        out_shape=[jax.ShapeDtypeStruct((N_CHIPS,) + s, BF16) for s in shapes],
        in_specs=[HBM_SPEC] * n,
        out_specs=[HBM_SPEC] * n,
        scratch_shapes=[sems, sems, sems, sems, pltpu.SemaphoreType.DMA((n,))],
        compiler_params=_params(),
    )(*shards)


def _sibling_halves(stacks, name):
    n = len(stacks)
    shapes = [s.shape for s in stacks]

    def body(*refs):
        ins, outs = refs[:n], refs[n:2 * n]
        send, recv = refs[2 * n:]
        x, y, c, _ = _place()
        copies = []
        for a in range(n):
            half = shapes[a][1] // 2
            cp = _remote(ins[a].at[:, _half_rows(1 - c, half), :], outs[a], send.at[a], recv.at[a], (x, y, 1 - c))
            cp.start()
            copies.append(cp)
        for cp in copies:
            cp.wait()

    return pl.pallas_call(
        body,
        name=name,
        out_shape=[jax.ShapeDtypeStruct((N_CHIPS, s[1] // 2, s[2]), F32) for s in shapes],
        in_specs=[HBM_SPEC] * n,
        out_specs=[HBM_SPEC] * n,
        scratch_shapes=[pltpu.SemaphoreType.DMA((n,)), pltpu.SemaphoreType.DMA((n,))],
        compiler_params=_params(),
    )(*stacks)


def _chip_exchange(parts, name):
    n = len(parts)
    shapes = [p.shape for p in parts]

    def body(*refs):
        ins, outs = refs[:n], refs[n:2 * n]
        send, recv = refs[2 * n:]
        x, y, c, chips = _place()
        copies = []
        for a in range(n):
            for j, (px, py) in enumerate(chips):
                cp = _remote(ins[a].at[2 * px + py], outs[a].at[j], send.at[a, j], recv.at[a, j], (px, py, c))
                cp.start()
                copies.append(cp)
        for cp in copies:
            cp.wait()

    return pl.pallas_call(
        body,
        name=name,
        out_shape=[jax.ShapeDtypeStruct((3,) + s[1:], BF16) for s in shapes],
        in_specs=[HBM_SPEC] * n,
        out_specs=[HBM_SPEC] * n,
        scratch_shapes=[pltpu.SemaphoreType.DMA((n, 3)), pltpu.SemaphoreType.DMA((n, 3))],
        compiler_params=_params(),
    )(*parts)


def _sibling_swap(halves, name):
    n = len(halves)
    shapes = [h.shape for h in halves]

    def body(*refs):
        ins, outs = refs[:n], refs[n:2 * n]
        send, recv, lsem = refs[2 * n:]
        x, y, c, _ = _place()
        pending = []
        for a in range(n):
            half = shapes[a][0]
            mine = outs[a].at[_half_rows(c, half), :]
            loc = pltpu.make_async_copy(ins[a], mine, lsem.at[a])
            loc.start()
            cp = _remote(ins[a], mine, send.at[a], recv.at[a], (x, y, 1 - c))
            cp.start()
            pending += [loc.wait, cp.wait_send]
        for a in range(n):
            half = shapes[a][0]
            theirs = outs[a].at[_half_rows(1 - c, half), :]
            _remote(theirs, theirs, send.at[a], recv.at[a], (x, y, 1 - c)).wait_recv()
        for wait in pending:
            wait()

    return pl.pallas_call(
        body,
        name=name,
        out_shape=[jax.ShapeDtypeStruct((2 * s[0], s[1]), F32) for s in shapes],
        in_specs=[HBM_SPEC] * n,
        out_specs=[HBM_SPEC] * n,
        scratch_shapes=[pltpu.SemaphoreType.DMA((n,)), pltpu.SemaphoreType.DMA((n,)), pltpu.SemaphoreType.DMA((n,))],
        compiler_params=_params(),
    )(*halves)


ROW_PAD = 16


def _silu(x):
    return x * _sigmoid(x)


def _ada_fwd(c_all, w_ada, b_ada, name):
    dm, cols = w_ada.shape
    tn = 512

    def body(c_ref, w_ref, b_ref, o_ref):
        o_ref[...] = _nn(_silu(c_ref[...]).astype(BF16), w_ref[...].astype(BF16)) + b_ref[...]

    return pl.pallas_call(
        body,
        name=name,
        grid=(cols // tn,),
        in_specs=[pl.BlockSpec((ROW_PAD, dm), lambda j: (0, 0)), pl.BlockSpec((dm, tn), lambda j: (0, j)),
                  pl.BlockSpec((1, tn), lambda j: (0, j))],
        out_specs=pl.BlockSpec((ROW_PAD, tn), lambda j: (0, j)),
        out_shape=jax.ShapeDtypeStruct((ROW_PAD, cols), F32),
        compiler_params=_params(("parallel",)),
    )(c_all, w_ada, b_ada)


def _adamw(w, g, m, v):
    m = ADAM_B1 * m + (1.0 - ADAM_B1) * g
    v = ADAM_B2 * v + (1.0 - ADAM_B2) * (g * g)
    m_hat = m / (1.0 - ADAM_B1 ** ADAM_STEP)
    v_hat = v / (1.0 - ADAM_B2 ** ADAM_STEP)
    delta = -ADAM_LR * (m_hat / (jnp.sqrt(v_hat) + ADAM_EPS) + ADAM_WD * w)
    return delta, m, v


def _ada_bwd_adamw(c_all, dmod_cols, w, m, v, name):
    dm, cols = w.shape
    tm, tn = 512, 512

    def body(c_ref, d_ref, w_ref, m_ref, v_ref, g_ref, dl_ref, nm_ref, nv_ref):
        g = _tn(_silu(c_ref[...]).astype(BF16), d_ref[...].astype(BF16))
        g_ref[...] = g
        dl_ref[...], nm_ref[...], nv_ref[...] = _adamw(w_ref[...], g, m_ref[...], v_ref[...])

    tile = pl.BlockSpec((tm, tn), lambda i, j: (i, j))
    shape = jax.ShapeDtypeStruct((dm, cols), F32)
    return pl.pallas_call(
        body,
        name=name,
        grid=(dm // tm, cols // tn),
        in_specs=[pl.BlockSpec((ROW_PAD, tm), lambda i, j: (0, i)), pl.BlockSpec((ROW_PAD, tn), lambda i, j: (0, j)),
                  tile, tile, tile],
        out_specs=[tile] * 4,
        out_shape=[shape] * 4,
        compiler_params=_params(("parallel", "parallel")),
    )(c_all, dmod_cols, w, m, v)


def _sum_blocks(parts, nblk, name):
    rows, cols = parts.shape[0] // nblk, parts.shape[1]

    def body(p_ref, o_ref):
        tot = p_ref[0:rows, :]
        for b in range(1, nblk):
            tot = tot + p_ref[b * rows:(b + 1) * rows, :]
        o_ref[...] = tot

    return pl.pallas_call(body, name=name, out_shape=jax.ShapeDtypeStruct((rows, cols), F32), compiler_params=_params())(parts)


def _adamw_rows(w, g, m, v, tm, name):
    return _rowwise(lambda r, c: list(_adamw(*r)), [w, g, m, v], [], [(w.shape[1], F32)] * 3, [], tm, name)


BIG = ("w_in", "w_out", "w_mlp_in", "w_mlp_out", "w_glu")
COL_SHARDED = ("w_in", "w_out", "w_mlp_in")
SMALL = ("b_ada", "g_pre_mix", "g_post_mix", "ssm_a_re", "ssm_a_im", "ssm_log_dt", "ssm_b_re", "ssm_b_im",
         "ssm_c_re", "ssm_c_im", "ssm_d", "b_glu", "g_attn_out", "g_ssm_out", "g_pre_mlp", "g_post_mlp")
WEIGHTS = ("w_ada", "b_ada", "g_pre_mix", "g_post_mix", "w_in", "ssm_a_re", "ssm_a_im", "ssm_log_dt", "ssm_b_re",
           "ssm_b_im", "ssm_c_re", "ssm_c_im", "ssm_d", "w_glu", "b_glu", "g_attn_out", "g_ssm_out", "w_out",
           "g_pre_mlp", "g_post_mlp", "w_mlp_in", "w_mlp_out")
FLAT_COLS = 1024
FLAT_ROWS = 256
ROW_TILE = {"w_in": 256, "w_out": 128, "w_mlp_in": 256, "w_mlp_out": 256, "w_glu": 112}


def _flatten_small(tree):
    flat = jnp.concatenate([tree[k].reshape(-1) for k in SMALL])
    return jnp.pad(flat, (0, FLAT_ROWS * FLAT_COLS - flat.shape[0])).reshape(FLAT_ROWS, FLAT_COLS)


def _unflatten_small(flat, like):
    flat = flat.reshape(-1)
    out, at = {}, 0
    for k in SMALL:
        size = math.prod(like[k].shape)
        out[k] = flat[at:at + size].reshape(like[k].shape)
        at += size
    return out


def _unstack(stack, name):
    if name in COL_SHARDED:
        return stack.transpose(1, 0, 2).reshape(stack.shape[1], N_CHIPS * stack.shape[2])
    return stack.reshape(N_CHIPS * stack.shape[1], stack.shape[2])


def _stack(full, name):
    if name in COL_SHARDED:
        return full.reshape(full.shape[0], N_CHIPS, full.shape[1] // N_CHIPS).transpose(1, 0, 2)
    return full.reshape(N_CHIPS, full.shape[0] // N_CHIPS, full.shape[1])


def _pad_rows(row):
    return jnp.pad(row, ((0, 8 - row.shape[0]), (0, 0)))


def _every_eighth(gathered):
    rows = gathered.reshape(N_DEV, 8, gathered.shape[1])[:, 0, :]
    return jnp.pad(rows, ((0, ROW_PAD - N_DEV), (0, 0)))


def kernel(x, c, positions, w_ada, b_ada, g_pre_mix, g_post_mix, w_in, ssm_a_re, ssm_a_im, ssm_log_dt, ssm_b_re, ssm_b_im, ssm_c_re, ssm_c_im, ssm_d, w_glu, b_glu, g_attn_out, g_ssm_out, w_out, g_pre_mlp, g_post_mlp, w_mlp_in, w_mlp_out, loss_target, m_w_ada, m_b_ada, m_g_pre_mix, m_g_post_mix, m_w_in, m_ssm_a_re, m_ssm_a_im, m_ssm_log_dt, m_ssm_b_re, m_ssm_b_im, m_ssm_c_re, m_ssm_c_im, m_ssm_d, m_w_glu, m_b_glu, m_g_attn_out, m_g_ssm_out, m_w_out, m_g_pre_mlp, m_g_post_mlp, m_w_mlp_in, m_w_mlp_out, v_w_ada, v_b_ada, v_g_pre_mix, v_g_post_mix, v_w_in, v_ssm_a_re, v_ssm_a_im, v_ssm_log_dt, v_ssm_b_re, v_ssm_b_im, v_ssm_c_re, v_ssm_c_im, v_ssm_d, v_w_glu, v_b_glu, v_g_attn_out, v_g_ssm_out, v_w_out, v_g_pre_mlp, v_g_post_mlp, v_w_mlp_in, v_w_mlp_out):
    given = dict(locals())
    w = {k: given[k][0] for k in WEIGHTS}
    mom = {k: given["m_" + k][0] for k in WEIGHTS}
    var = {k: given["v_" + k][0] for k in WEIGHTS}
    for tree in (w, mom, var):
        for k in ("b_ada", "g_pre_mix", "g_post_mix", "ssm_log_dt", "b_glu", "g_attn_out", "g_ssm_out", "g_pre_mlp",
                  "g_post_mlp"):
            tree[k] = tree[k].reshape(1, -1)
    ix, iy, ic = lax.axis_index("x"), lax.axis_index("y"), lax.axis_index("c")
    chip = 2 * ix + iy
    me = 4 * ix + 2 * iy + ic
    shard_cols = w["w_ada"].shape[1]

    c_all = _every_eighth(_all_gather8(_pad_rows(c), "gather_c"))
    b_ada_cols = lax.dynamic_slice_in_dim(w["b_ada"], chip * shard_cols, shard_cols, axis=1)
    mod_cols = _ada_fwd(c_all, w["w_ada"], b_ada_cols, "ada_fwd")[:N_DEV]
    mod_all = _all_gather8(mod_cols, "gather_mod").reshape(N_CHIPS, 2, N_DEV, shard_cols)[:, 0]
    mod = lax.dynamic_index_in_dim(mod_all, me, axis=1, keepdims=False).reshape(1, N_MOD * D_MODEL)

    stacks = _weight_gather([w[k].astype(BF16) for k in BIG], "weight_gather")
    wts = {k: _unstack(s, k) for k, s in zip(BIG, stacks)}

    small = {k: w[k] for k in SMALL if k != "b_ada"}
    loss, grad_x, dmod, big_g, small_g = _local_step(x[0], positions.reshape(-1, 1), mod, loss_target[0], wts, small)
    loss = lax.psum(loss[0, 0], ("x", "y", "c"))

    g_stacks = [_stack(big_g[k], k) for k in BIG]
    from_sibling = _sibling_halves(g_stacks, "grad_sibling_halves")
    chip_f32, chip_bf16 = [], []
    for k, gs, fs in zip(BIG, g_stacks, from_sibling):
        half = gs.shape[1] // 2
        mine = lax.dynamic_slice_in_dim(gs, ic * half, half, axis=1).reshape(N_CHIPS * half, gs.shape[2])
        s32, s16 = _rowwise(lambda r, cc: [r[0] + r[1]] * 2, [mine, fs.reshape(mine.shape)], [],
                            [(gs.shape[2], F32), (gs.shape[2], BF16)], [], ROW_TILE[k], "grad_chip_sum_" + k)
        chip_f32.append(lax.dynamic_slice_in_dim(s32, chip * half, half, axis=0))
        chip_bf16.append(s16.reshape(N_CHIPS, half, gs.shape[2]))
    from_chips = _chip_exchange(chip_bf16, "grad_chip_exchange")
    reduced = []
    for k, own, fc in zip(BIG, chip_f32, from_chips):
        (tot,) = _rowwise(lambda r, cc: [r[0] + r[1].astype(F32) + r[2].astype(F32) + r[3].astype(F32)],
                          [own, fc[0], fc[1], fc[2]], [], [(own.shape[1], F32)], [], ROW_TILE[k], "grad_total_" + k)
        reduced.append(tot)
    grads = dict(zip(BIG, _sibling_swap(reduced, "grad_sibling_swap")))

    small_g["b_ada"] = dmod
    parts = _all_gather8(_flatten_small(small_g), "gather_small_grads")

    small_flat = _sum_blocks(parts, N_DEV, "small_grad_sum")
    grads.update(_unflatten_small(small_flat, w))

    dmod_all = _every_eighth(_all_gather8(_pad_rows(dmod), "gather_dmod"))
    dmod_cols = lax.dynamic_slice_in_dim(dmod_all, chip * shard_cols, shard_cols, axis=1)
    g_ada, d_ada, m_ada, v_ada = _ada_bwd_adamw(c_all, dmod_cols, w["w_ada"], mom["w_ada"], var["w_ada"], "ada_bwd_adamw")
    grads["w_ada"] = g_ada

    delta, new_m, new_v = {"w_ada": d_ada}, {"w_ada": m_ada}, {"w_ada": v_ada}
    for k in BIG:
        delta[k], new_m[k], new_v[k] = _adamw_rows(w[k], grads[k], mom[k], var[k], ROW_TILE[k], "adamw_" + k)
    flat_upd = _adamw_rows(_flatten_small(w), small_flat, _flatten_small(mom), _flatten_small(var), FLAT_ROWS,
                           "adamw_small")
    for tree, flat in zip((delta, new_m, new_v), flat_upd):
        tree.update(_unflatten_small(flat, w))

    shaped = lambda tree: [tree[k].reshape(given[k].shape) for k in WEIGHTS]
    return (loss, grad_x[None], *shaped(grads), *shaped(delta), *shaped(new_m), *shaped(new_v))
```

```python
import functools
import math

import jax
import jax.numpy as jnp
import numpy as np
from jax import lax
from jax.experimental import pallas as pl
from jax.experimental.pallas import tpu as pltpu

F32 = jnp.float32
BF16 = jnp.bfloat16

D_MODEL = 2048
HEAD_DIM = 64
DILATIONS = (1, 4, 16)
ATT_SPAN = 128
ATT_BLK = 128
HEADS_PER_GROUP = 6
KV_WIDTH = HEADS_PER_GROUP * HEAD_DIM
ATT_Q_WIDTH = 3 * KV_WIDTH
ROT_DIM = 16
ROPE_THETA = 500000.0
SSM_WIDTH = 896
SSM_P = 16
SSM_G = 56
SSM_N = 64
SSM_GN = SSM_G * SSM_N
SSM_TILES = SSM_WIDTH // 128
SSM_TILE_GN = 8 * SSM_N
IN_WIDTH = 2816
OUT_IN_WIDTH = 1280
D_FF = 8192
N_MOD = 6
EPS = 1e-6
LANES = 128
SSM_SEGS = 8
SSM_CHUNK = 256
SSM_SEG_LEN = SSM_CHUNK // SSM_SEGS

ADAM_LR = 0.001
ADAM_B1 = 0.9
ADAM_B2 = 0.999
ADAM_EPS = 1e-08
ADAM_WD = 0.01
ADAM_STEP = 10

VMEM_LIMIT = 56 * 1024 * 1024


def _params(sem=None):
    return pltpu.CompilerParams(dimension_semantics=sem, vmem_limit_bytes=VMEM_LIMIT)


def _dot(a, b, dims):
    return lax.dot_general(a, b, (dims, ((), ())), preferred_element_type=F32)


def _nn(a, b):
    return _dot(a, b, ((1,), (0,)))


def _nt(a, b):
    return _dot(a, b, ((1,), (1,)))


def _tn(a, b):
    return _dot(a, b, ((0,), (0,)))


def _matmul(a, b, mode, out_dtype, tm, tn, tk, name):
    if mode == "nn":
        (m, k), (_, n) = a.shape, b.shape
        a_spec = pl.BlockSpec((tm, tk), lambda i, j, kk: (i, kk))
        b_spec = pl.BlockSpec((tk, tn), lambda i, j, kk: (kk, j))
        op = _nn
    elif mode == "nt":
        (m, k), (n, _) = a.shape, b.shape
        a_spec = pl.BlockSpec((tm, tk), lambda i, j, kk: (i, kk))
        b_spec = pl.BlockSpec((tn, tk), lambda i, j, kk: (j, kk))
        op = _nt
    else:
        (k, m), (_, n) = a.shape, b.shape
        a_spec = pl.BlockSpec((tk, tm), lambda i, j, kk: (kk, i))
        b_spec = pl.BlockSpec((tk, tn), lambda i, j, kk: (kk, j))
        op = _tn
    assert m % tm == 0 and n % tn == 0 and k % tk == 0, (name, m, n, k)
    nk = k // tk

    def body(a_ref, b_ref, o_ref, acc_ref):
        kk = pl.program_id(2)

        @pl.when(kk == 0)
        def _():
            acc_ref[...] = jnp.zeros_like(acc_ref)

        acc_ref[...] += op(a_ref[...], b_ref[...])

        @pl.when(kk == nk - 1)
        def _():
            o_ref[...] = acc_ref[...].astype(o_ref.dtype)

    return pl.pallas_call(
        body,
        name=name,
        grid=(m // tm, n // tn, nk),
        in_specs=[a_spec, b_spec],
        out_specs=pl.BlockSpec((tm, tn), lambda i, j, kk: (i, j)),
        out_shape=jax.ShapeDtypeStruct((m, n), out_dtype),
        scratch_shapes=[pltpu.VMEM((tm, tn), F32)],
        compiler_params=_params(("parallel", "parallel", "arbitrary")),
    )(a, b)


def _rowwise(fn, rows, consts, out_rows, out_accs, tm, name):
    n_rows = rows[0].shape[0]
    assert n_rows % tm == 0
    nr, nc, no = len(rows), len(consts), len(out_rows)

    def body(*refs):
        r_in, c_in = refs[:nr], refs[nr:nr + nc]
        o_row, o_acc = refs[nr + nc:nr + nc + no], refs[nr + nc + no:]
        outs = fn([r[...] for r in r_in], [c[...] for c in c_in])
        assert len(outs) == len(o_row) + len(o_acc), name
        for ref, v in zip(o_row, outs[:no]):
            ref[...] = v.astype(ref.dtype)
        first = pl.program_id(0) == 0
        for ref, v in zip(o_acc, outs[no:]):
            @pl.when(first)
            def _(ref=ref, v=v):
                ref[...] = v.astype(F32)

            @pl.when(jnp.logical_not(first))
            def _(ref=ref, v=v):
                ref[...] += v.astype(F32)

    in_specs = [pl.BlockSpec((tm, r.shape[1]), lambda i: (i, 0)) for r in rows]
    in_specs += [pl.BlockSpec(c.shape, lambda i: (0, 0)) for c in consts]
    out_specs = [pl.BlockSpec((tm, w), lambda i: (i, 0)) for w, _ in out_rows]
    out_specs += [pl.BlockSpec(s, lambda i: (0, 0)) for s in out_accs]
    out_shape = [jax.ShapeDtypeStruct((n_rows, w), dt) for w, dt in out_rows]
    out_shape += [jax.ShapeDtypeStruct(s, F32) for s in out_accs]
    return pl.pallas_call(
        body,
        name=name,
        grid=(n_rows // tm,),
        in_specs=in_specs,
        out_specs=out_specs,
        out_shape=out_shape,
        compiler_params=_params(("arbitrary",)),
    )(*rows, *consts)


def _rms(x, g):
    return x * lax.rsqrt(jnp.mean(x * x, axis=-1, keepdims=True) + EPS) * g


def _mod_norm(x, g, sc, sh):
    return _rms(x, g) * (1.0 + sc) + sh


def _gelu(x):
    return 0.5 * x * (1.0 + jnp.tanh(math.sqrt(2.0 / math.pi) * (x + 0.044715 * (x * x * x))))


def _sigmoid(x):
    return 1.0 / (1.0 + jnp.exp(-x))


def _post_mix(x, mix, g_post, gt1, g_pre, sc2, sh2):
    x1 = x + gt1 * _rms(mix, g_post)
    return x1, _mod_norm(x1, g_pre, sc2, sh2)


def _att_mix(o0, o1, o2, l0, l1, l2, g):
    m = jnp.maximum(jnp.maximum(l0, l1), l2)
    e0, e1, e2 = jnp.exp(l0 - m), jnp.exp(l1 - m), jnp.exp(l2 - m)
    att = (e0 * o0 + e1 * o1 + e2 * o2) / (e0 + e1 + e2)
    return _rms(att, g)


def _glu_out(y2, z, g):
    return _rms(y2 * _sigmoid(z), g)


def _rope_tables(pos_col, freq_lane, name):
    n_rows = pos_col.shape[0]
    tm = 512

    def body(p_ref, f_ref, cos_ref, lo_ref, hi_ref):
        ang = p_ref[...].astype(F32) * f_ref[...]
        lane = lax.broadcasted_iota(jnp.int32, ang.shape, 1) % HEAD_DIM
        c, s = jnp.cos(ang), jnp.sin(ang)
        cos_ref[...] = jnp.where(lane < ROT_DIM, c, 1.0)
        lo_ref[...] = jnp.where(lane < ROT_DIM // 2, -s, 0.0)
        hi_ref[...] = jnp.where((lane >= ROT_DIM // 2) & (lane < ROT_DIM), s, 0.0)

    tab = jax.ShapeDtypeStruct((n_rows, LANES), F32)
    return pl.pallas_call(
        body,
        name=name,
        grid=(n_rows // tm,),
        in_specs=[pl.BlockSpec((tm, 1), lambda i: (i, 0)), pl.BlockSpec((1, LANES), lambda i: (0, 0))],
        out_specs=[pl.BlockSpec((tm, LANES), lambda i: (i, 0))] * 3,
        out_shape=[tab] * 3,
        compiler_params=_params(("parallel",)),
    )(pos_col, freq_lane)


def _rope(x, cos_t, lo_t, hi_t):
    half = ROT_DIM // 2
    return x * cos_t + pltpu.roll(x, LANES - half, 1) * lo_t + pltpu.roll(x, half, 1) * hi_t


def _rope_transposed(dy, cos_t, lo_t, hi_t):
    half = ROT_DIM // 2
    return dy * cos_t + pltpu.roll(dy * lo_t, half, 1) + pltpu.roll(dy * hi_t, LANES - half, 1)


def _att_masks(i, k0):
    q_pos = i * ATT_BLK + lax.broadcasted_iota(jnp.int32, (ATT_BLK, 2 * ATT_BLK), 0)
    k_pos = k0 + lax.broadcasted_iota(jnp.int32, (ATT_BLK, 2 * ATT_BLK), 1)
    dist = q_pos - k_pos
    return (dist >= 0) & (dist <= ATT_SPAN)


def _head_lane_masks():
    lane = lax.broadcasted_iota(jnp.int32, (1, LANES), 1)
    return lane < HEAD_DIM, lane >= HEAD_DIM


def _att_specs(gi, d, n):
    cols = IN_WIDTH // LANES
    qkv = [
        pl.BlockSpec((n, LANES), lambda r, hp: (0, r * cols + gi * 3 + hp)),
        pl.BlockSpec((n, LANES), lambda r, hp: (0, r * cols + 9 + hp)),
        pl.BlockSpec((n, LANES), lambda r, hp: (0, r * cols + 12 + hp)),
    ]
    tabs = [pl.BlockSpec((n, LANES), lambda r, hp: (0, r))] * 3
    head = pl.BlockSpec((n, LANES), lambda r, hp: (0, r * 3 + hp))
    return qkv, tabs, head


def _att_load(q_ref, k_ref, v_ref, cos_ref, lo_ref, hi_ref, qs, ks, vs):
    cos_t, lo_t, hi_t = cos_ref[...], lo_ref[...], hi_ref[...]
    qs[...] = (_rope(q_ref[...], cos_t, lo_t, hi_t) * (1.0 / math.sqrt(HEAD_DIM))).astype(BF16)
    ks[...] = _rope(k_ref[...], cos_t, lo_t, hi_t).astype(BF16)
    vs[...] = v_ref[...].astype(BF16)


def _att_fwd(proj, tabs, gi, name):
    n_rows = proj.shape[0]
    d = DILATIONS[gi]
    n = n_rows // d
    nb = n // ATT_BLK
    proj_v = proj.reshape(n, d * IN_WIDTH)
    tabs_v = [t.reshape(n, d * LANES) for t in tabs]

    def body(q_ref, k_ref, v_ref, cos_ref, lo_ref, hi_ref, o_ref, l_ref, qs, ks, vs):
        _att_load(q_ref, k_ref, v_ref, cos_ref, lo_ref, hi_ref, qs, ks, vs)
        m0, m1 = _head_lane_masks()

        def step(i, carry):
            k0 = pl.multiple_of(jnp.maximum(i - 1, 0) * ATT_BLK, ATT_BLK)
            q0 = pl.multiple_of(i * ATT_BLK, ATT_BLK)
            q = qs[pl.ds(q0, ATT_BLK), :]
            k = ks[pl.ds(k0, 2 * ATT_BLK), :]
            v = vs[pl.ds(k0, 2 * ATT_BLK), :]
            valid = _att_masks(i, k0)
            outs, lses = [], []
            for hm in (m0, m1):
                s = _nt(jnp.where(hm, q, jnp.zeros_like(q)), k)
                s = jnp.where(valid, s, -1e30)
                mx = jnp.max(s, axis=1, keepdims=True)
                p = jnp.exp(s - mx)
                den = jnp.sum(p, axis=1, keepdims=True)
                outs.append(_nn(p.astype(BF16), v) / den)
                lses.append(mx + jnp.log(den))
            o_ref[pl.ds(q0, ATT_BLK), :] = jnp.where(m0, outs[0], outs[1])
            l_ref[pl.ds(q0, ATT_BLK), :] = jnp.where(m0, lses[0], lses[1])
            return carry

        lax.fori_loop(0, nb, step, 0)

    qkv, tab_specs, head = _att_specs(gi, d, n)
    out = jax.ShapeDtypeStruct((n, d * KV_WIDTH), F32)
    o, l = pl.pallas_call(
        body,
        name=name,
        grid=(d, 3),
        in_specs=qkv + tab_specs,
        out_specs=[head, head],
        out_shape=[out, out],
        scratch_shapes=[pltpu.VMEM((n, LANES), BF16)] * 3,
        compiler_params=_params(("parallel", "parallel")),
    )(proj_v, proj_v, proj_v, *tabs_v)
    return o.reshape(n_rows, KV_WIDTH), l.reshape(n_rows, KV_WIDTH)


def _att_bwd(proj, tabs, o, l, do, dl, gi, name):
    n_rows = proj.shape[0]
    d = DILATIONS[gi]
    n = n_rows // d
    nb = n // ATT_BLK
    proj_v = proj.reshape(n, d * IN_WIDTH)
    tabs_v = [t.reshape(n, d * LANES) for t in tabs]
    heads_v = [t.reshape(n, d * KV_WIDTH) for t in (o, l, do, dl)]

    def body(q_ref, k_ref, v_ref, cos_ref, lo_ref, hi_ref, o_ref, l_ref, do_ref, dl_ref,
             dq_ref, dk_ref, dv_ref, qs, ks, vs, dk_acc, dv_acc):
        _att_load(q_ref, k_ref, v_ref, cos_ref, lo_ref, hi_ref, qs, ks, vs)
        dk_acc[...] = jnp.zeros_like(dk_acc)
        dv_acc[...] = jnp.zeros_like(dv_acc)
        m0, m1 = _head_lane_masks()

        def step(i, carry):
            k0 = pl.multiple_of(jnp.maximum(i - 1, 0) * ATT_BLK, ATT_BLK)
            q0 = pl.multiple_of(i * ATT_BLK, ATT_BLK)
            rows = pl.ds(q0, ATT_BLK)
            keys = pl.ds(k0, 2 * ATT_BLK)
            q, k, v = qs[rows, :], ks[keys, :], vs[keys, :]
            d_o, lse = do_ref[rows, :], l_ref[rows, :]
            o_do = o_ref[rows, :] * d_o
            d_l = dl_ref[rows, :]
            valid = _att_masks(i, k0)
            dq = jnp.zeros((ATT_BLK, LANES), F32)
            dk = jnp.zeros((2 * ATT_BLK, LANES), F32)
            dv = jnp.zeros((2 * ATT_BLK, LANES), F32)
            for hm in (m0, m1):
                qh, kh = jnp.where(hm, q, jnp.zeros_like(q)), jnp.where(hm, k, jnp.zeros_like(k))
                doh = jnp.where(hm, d_o, 0.0).astype(BF16)
                lse_h = jnp.max(jnp.where(hm, lse, -1e30), axis=1, keepdims=True)
                delta = jnp.sum(jnp.where(hm, o_do, 0.0), axis=1, keepdims=True)
                dlse = jnp.sum(jnp.where(hm, d_l, 0.0), axis=1, keepdims=True)
                s = jnp.where(valid, _nt(qh, k), -1e30)
                p = jnp.exp(s - lse_h)
                dv = dv + _tn(p.astype(BF16), doh)
                ds = (p * (_nt(doh, v) - delta + dlse)).astype(BF16)
                dq = dq + _nn(ds, kh)
                dk = dk + _tn(ds, qh)
            cos_t, lo_t, hi_t = cos_ref[rows, :], lo_ref[rows, :], hi_ref[rows, :]
            dq_ref[rows, :] = _rope_transposed(dq * (1.0 / math.sqrt(HEAD_DIM)), cos_t, lo_t, hi_t)
            dk_acc[keys, :] += dk
            dv_acc[keys, :] += dv
            return carry

        lax.fori_loop(0, nb, step, 0)
        dk_ref[...] = _rope_transposed(dk_acc[...], cos_ref[...], lo_ref[...], hi_ref[...])
        dv_ref[...] = dv_acc[...]

    qkv, tab_specs, head = _att_specs(gi, d, n)
    out = jax.ShapeDtypeStruct((n, d * KV_WIDTH), F32)
    res = pl.pallas_call(
        body,
        name=name,
        grid=(d, 3),
        in_specs=qkv + tab_specs + [head] * 4,
        out_specs=[head] * 3,
        out_shape=[out] * 3,
        scratch_shapes=[pltpu.VMEM((n, LANES), BF16)] * 3 + [pltpu.VMEM((n, LANES), F32)] * 2,
        compiler_params=_params(("parallel", "parallel")),
    )(proj_v, proj_v, proj_v, *tabs_v, *heads_v)
    return [t.reshape(n_rows, KV_WIDTH) for t in res]


def _expand_np():
    e = np.zeros((SSM_N, SSM_N * SSM_P), np.float32)
    for nn in range(SSM_N):
        e[nn, nn * SSM_P:(nn + 1) * SSM_P] = 1.0
    return e


def _ssm_prep_math(a_re, a_im, log_dt, b_re, b_im, expand):
    dt = jnp.exp(log_dt)
    mag = jnp.exp(a_re * dt)
    ab_re, ab_im = mag * jnp.cos(a_im * dt), mag * jnp.sin(a_im * dt)
    den = a_re * a_re + a_im * a_im
    num_re, num_im = ab_re - 1.0, ab_im
    co_re = (num_re * a_re + num_im * a_im) / den
    co_im = (num_im * a_re - num_re * a_im) / den
    hi = lax.Precision.HIGHEST
    co_re_x = jnp.dot(co_re, expand, precision=hi, preferred_element_type=F32)
    co_im_x = jnp.dot(co_im, expand, precision=hi, preferred_element_type=F32)
    bb_re = co_re_x * b_re - co_im_x * b_im
    bb_im = co_re_x * b_im + co_im_x * b_re
    return ab_re, ab_im, bb_re, bb_im


def _ssm_prep(a_re, a_im, log_dt, b_re, b_im, expand, name):
    def body(ar, ai, ld, br, bi, ex, o0, o1, o2, o3):
        outs = _ssm_prep_math(ar[...], ai[...], ld[...], br[...], bi[...], ex[...])
        for ref, v in zip((o0, o1, o2, o3), outs):
            ref[...] = v

    gn = jax.ShapeDtypeStruct((SSM_G, SSM_N), F32)
    gnp = jax.ShapeDtypeStruct((SSM_G, SSM_N * SSM_P), F32)
    return pl.pallas_call(body, name=name, out_shape=[gn, gn, gnp, gnp], compiler_params=_params())(
        a_re, a_im, log_dt, b_re, b_im, expand)


def _ssm_prep_bwd(a_re, a_im, log_dt, b_re, b_im, expand, cts, name):
    def body(ar, ai, ld, br, bi, ex, c0, c1, c2, c3, o0, o1, o2, o3, o4):
        ex_v = ex[...]
        _, vjp = jax.vjp(lambda *p: _ssm_prep_math(*p, ex_v), ar[...], ai[...], ld[...], br[...], bi[...])
        for ref, v in zip((o0, o1, o2, o3, o4), vjp((c0[...], c1[...], c2[...], c3[...]))):
            ref[...] = v

    gn = jax.ShapeDtypeStruct((SSM_G, SSM_N), F32)
    gnp = jax.ShapeDtypeStruct((SSM_G, SSM_N * SSM_P), F32)
    g1 = jax.ShapeDtypeStruct((SSM_G, 1), F32)
    return pl.pallas_call(body, name=name, out_shape=[gn, gn, g1, gnp, gnp], compiler_params=_params())(
        a_re, a_im, log_dt, b_re, b_im, expand, *cts)


def _block_diag_in(bb):
    t = bb.reshape(SSM_TILES, 8, SSM_N, SSM_P).transpose(0, 1, 3, 2)
    eye = jnp.eye(8, dtype=bb.dtype)
    return (t[:, :, :, None, :] * eye[None, :, None, :, None]).reshape(SSM_TILES, LANES, SSM_TILE_GN)


def _block_diag_in_grad(dblk):
    t = dblk.reshape(SSM_TILES, 8, SSM_P, 8, SSM_N)
    t = jnp.einsum("tapbn,ab->tapn", t, jnp.eye(8, dtype=dblk.dtype))
    return t.transpose(0, 1, 3, 2).reshape(SSM_G, SSM_N, SSM_P)


def _block_diag_out(cm):
    t = cm.reshape(SSM_TILES, 8, SSM_P, SSM_N).transpose(0, 1, 3, 2)
    eye = jnp.eye(8, dtype=cm.dtype)
    return (t[:, :, :, None, :] * eye[None, :, None, :, None]).reshape(SSM_TILES, SSM_TILE_GN, LANES)


def _block_diag_out_grad(dblk):
    t = dblk.reshape(SSM_TILES, 8, SSM_N, 8, SSM_P)
    t = jnp.einsum("tanbp,ab->tanp", t, jnp.eye(8, dtype=dblk.dtype))
    return t.transpose(0, 1, 3, 2).reshape(SSM_G, SSM_P, SSM_N)


def _cmul_add(a_re, a_im, s_re, s_im, b_re, b_im):
    return a_re * s_re - a_im * s_im + b_re, a_re * s_im + a_im * s_re + b_im


def _lane_tiled(a):
    return a.reshape(a.shape[0], SSM_TILES, LANES).transpose(1, 0, 2)


def _lane_untiled(a):
    return a.transpose(1, 0, 2).reshape(a.shape[1], SSM_WIDTH)


def _ssm_load_rows(src_ref, dst):
    for t in range(SSM_TILES):
        for i in range(SSM_SEG_LEN):
            dst[i * SSM_SEGS:(i + 1) * SSM_SEGS, t * LANES:(t + 1) * LANES] = (
                src_ref[t, pl.ds(i, SSM_SEGS, stride=SSM_SEG_LEN), :])


def _ssm_store_rows(src, dst_ref):
    for t in range(SSM_TILES):
        for i in range(SSM_SEG_LEN):
            dst_ref[t, pl.ds(i, SSM_SEGS, stride=SSM_SEG_LEN), :] = (
                src[i * SSM_SEGS:(i + 1) * SSM_SEGS, t * LANES:(t + 1) * LANES])


def _ssm_powers(ab_re_ref, ab_im_ref, pw_re, pw_im):
    a_re, a_im = ab_re_ref[...], ab_im_ref[...]
    p_re, p_im = a_re, a_im
    for i in range(SSM_SEG_LEN):
        pw_re[i:i + 1, :] = p_re
        pw_im[i:i + 1, :] = p_im
        p_re, p_im = _cmul_add(a_re, a_im, p_re, p_im, 0.0, 0.0)


def _ssm_input_proj(u_s, bblk_re_ref, bblk_im_ref, s_re, s_im):
    for t in range(SSM_TILES):
        ub = u_s[:, t * LANES:(t + 1) * LANES].astype(BF16)
        cols = slice(t * SSM_TILE_GN, (t + 1) * SSM_TILE_GN)
        s_re[:, cols] = _nn(ub, bblk_re_ref[t])
        s_im[:, cols] = _nn(ub, bblk_im_ref[t])


def _ssm_scan(ab_re_ref, ab_im_ref, s_re, s_im, init_re, init_im, conj, reverse):
    sign = -1.0 if conj else 1.0
    for t in range(SSM_TILES):
        cols = slice(t * SSM_TILE_GN, (t + 1) * SSM_TILE_GN)
        a_re = jnp.broadcast_to(ab_re_ref[:, cols], (SSM_SEGS, SSM_TILE_GN))
        a_im = jnp.broadcast_to(ab_im_ref[:, cols], (SSM_SEGS, SSM_TILE_GN)) * sign
        if init_re is None:
            st = (jnp.zeros((SSM_SEGS, SSM_TILE_GN), F32),) * 2
        else:
            st = (init_re[:, cols], init_im[:, cols])

        def step(i, st, cols=cols, a_re=a_re, a_im=a_im):
            idx = (SSM_SEG_LEN - 1 - i) if reverse else i
            rows = pl.ds(pl.multiple_of(idx * SSM_SEGS, SSM_SEGS), SSM_SEGS)
            n_re, n_im = _cmul_add(a_re, a_im, st[0], st[1], s_re[rows, cols], s_im[rows, cols])
            s_re[rows, cols] = n_re
            s_im[rows, cols] = n_im
            return n_re, n_im
        lax.fori_loop(0, SSM_SEG_LEN, step, st, unroll=4)


def _ssm_fixup(pw_re, pw_im, s_re, s_im, cin_re, cin_im, conj, reverse):
    sign = -1.0 if conj else 1.0
    c_re, c_im = cin_re[...], cin_im[...]

    def step(i, c):
        k = (SSM_SEG_LEN - 1 - i) if reverse else i
        rows = pl.ds(pl.multiple_of(i * SSM_SEGS, SSM_SEGS), SSM_SEGS)
        p_re = jnp.broadcast_to(pw_re[pl.ds(k, 1), :], (SSM_SEGS, SSM_GN))
        p_im = jnp.broadcast_to(pw_im[pl.ds(k, 1), :], (SSM_SEGS, SSM_GN)) * sign
        n_re, n_im = _cmul_add(p_re, p_im, c_re, c_im, s_re[rows, :], s_im[rows, :])
        s_re[rows, :] = n_re
        s_im[rows, :] = n_im
        return c
    lax.fori_loop(0, SSM_SEG_LEN, step, 0)


def _ssm_fwd(u, ab_re, ab_im, bblk_re, bblk_im, cblk_re, cblk_im, d_row, name):
    n_rows = u.shape[1]
    nchunk = n_rows // SSM_CHUNK
    last = SSM_SEG_LEN - 1

    def body(u_ref, ar_ref, ai_ref, br_ref, bi_ref, cr_ref, ci_ref, d_ref, y_ref, cin_re_ref, cin_im_ref,
             u_s, s_re, s_im, pw_re, pw_im, st_re, st_im):
        @pl.when(pl.program_id(0) == 0)
        def _():
            _ssm_powers(ar_ref, ai_ref, pw_re, pw_im)
            st_re[...] = jnp.zeros_like(st_re)
            st_im[...] = jnp.zeros_like(st_im)

        _ssm_load_rows(u_ref, u_s)
        _ssm_input_proj(u_s, br_ref, bi_ref, s_re, s_im)
        _ssm_scan(ar_ref, ai_ref, s_re, s_im, None, None, conj=False, reverse=False)
        p_re, p_im = pw_re[last:last + 1, :], pw_im[last:last + 1, :]
        c_re, c_im = st_re[...], st_im[...]
        for j in range(SSM_SEGS):
            cin_re_ref[j:j + 1, :] = c_re
            cin_im_ref[j:j + 1, :] = c_im
            row = last * SSM_SEGS + j
            c_re, c_im = _cmul_add(p_re, p_im, c_re, c_im, s_re[row:row + 1, :], s_im[row:row + 1, :])
        st_re[...] = c_re
        st_im[...] = c_im
        _ssm_fixup(pw_re, pw_im, s_re, s_im, cin_re_ref, cin_im_ref, conj=False, reverse=False)
        for t in range(SSM_TILES):
            cols = slice(t * SSM_TILE_GN, (t + 1) * SSM_TILE_GN)
            lanes = slice(t * LANES, (t + 1) * LANES)
            y = _nn(s_re[:, cols].astype(BF16), cr_ref[t]) - _nn(s_im[:, cols].astype(BF16), ci_ref[t])
            u_s[:, lanes] = y + d_ref[:, lanes] * u_s[:, lanes]
        _ssm_store_rows(u_s, y_ref)

    whole2 = lambda a: pl.BlockSpec(a.shape, lambda c: (0, 0))
    whole3 = lambda a: pl.BlockSpec(a.shape, lambda c: (0, 0, 0))
    chunk = pl.BlockSpec((SSM_TILES, SSM_CHUNK, LANES), lambda c: (0, c, 0))
    seg = pl.BlockSpec((SSM_SEGS, SSM_GN), lambda c: (c, 0))
    seg_shape = jax.ShapeDtypeStruct((nchunk * SSM_SEGS, SSM_GN), F32)
    return pl.pallas_call(
        body,
        name=name,
        grid=(nchunk,),
        in_specs=[chunk, whole2(ab_re), whole2(ab_im), whole3(bblk_re), whole3(bblk_im), whole3(cblk_re),
                  whole3(cblk_im), whole2(d_row)],
        out_specs=[chunk, seg, seg],
        out_shape=[jax.ShapeDtypeStruct((SSM_TILES, n_rows, LANES), F32), seg_shape, seg_shape],
        scratch_shapes=[pltpu.VMEM((SSM_CHUNK, SSM_WIDTH), F32), pltpu.VMEM((SSM_CHUNK, SSM_GN), F32),
                        pltpu.VMEM((SSM_CHUNK, SSM_GN), F32), pltpu.VMEM((SSM_SEG_LEN, SSM_GN), F32),
                        pltpu.VMEM((SSM_SEG_LEN, SSM_GN), F32), pltpu.VMEM((1, SSM_GN), F32),
                        pltpu.VMEM((1, SSM_GN), F32)],
        compiler_params=_params(("arbitrary",)),
    )(u, ab_re, ab_im, bblk_re, bblk_im, cblk_re, cblk_im, d_row)


def _ssm_bwd(u, dy, cin_re, cin_im, ab_re, ab_im, bblk_re, bblk_im, cblk_re, cblk_im, d_row, name):
    n_rows = u.shape[1]
    nchunk = n_rows // SSM_CHUNK

    def body(u_ref, dy_ref, cin_re_ref, cin_im_ref, ar_ref, ai_ref, br_ref, bi_ref, cr_ref, ci_ref, d_ref,
             du_ref, dar_ref, dai_ref, dbr_ref, dbi_ref, dcr_ref, dci_ref, dd_ref,
             u_s, dy_s, s_re, s_im, q_re, q_im, pw_re, pw_im, qst_re, qst_im, qin_re, qin_im):
        @pl.when(pl.program_id(0) == 0)
        def _():
            _ssm_powers(ar_ref, ai_ref, pw_re, pw_im)
            qst_re[...] = jnp.zeros_like(qst_re)
            qst_im[...] = jnp.zeros_like(qst_im)
            for ref in (dar_ref, dai_ref, dbr_ref, dbi_ref, dcr_ref, dci_ref, dd_ref):
                ref[...] = jnp.zeros_like(ref)

        _ssm_load_rows(u_ref, u_s)
        _ssm_load_rows(dy_ref, dy_s)
        _ssm_input_proj(u_s, br_ref, bi_ref, s_re, s_im)
        _ssm_scan(ar_ref, ai_ref, s_re, s_im, cin_re_ref, cin_im_ref, conj=False, reverse=False)
        for t in range(SSM_TILES):
            cols = slice(t * SSM_TILE_GN, (t + 1) * SSM_TILE_GN)
            dyb = dy_s[:, t * LANES:(t + 1) * LANES].astype(BF16)
            q_re[:, cols] = _nt(dyb, cr_ref[t])
            q_im[:, cols] = -_nt(dyb, ci_ref[t])
            dcr_ref[t] += _tn(s_re[:, cols].astype(BF16), dyb)
            dci_ref[t] -= _tn(s_im[:, cols].astype(BF16), dyb)
        _ssm_scan(ar_ref, ai_ref, q_re, q_im, None, None, conj=True, reverse=True)
        last = SSM_SEG_LEN - 1
        p_re, p_im = pw_re[last:last + 1, :], -pw_im[last:last + 1, :]
        c_re, c_im = qst_re[...], qst_im[...]
        for j in reversed(range(SSM_SEGS)):
            qin_re[j:j + 1, :] = c_re
            qin_im[j:j + 1, :] = c_im
            c_re, c_im = _cmul_add(p_re, p_im, c_re, c_im, q_re[j:j + 1, :], q_im[j:j + 1, :])
        qst_re[...] = c_re
        qst_im[...] = c_im
        _ssm_fixup(pw_re, pw_im, q_re, q_im, qin_re, qin_im, conj=True, reverse=True)
        for t in range(SSM_TILES):
            cols = slice(t * SSM_TILE_GN, (t + 1) * SSM_TILE_GN)

            def step(i, acc, cols=cols):
                rows = pl.ds(pl.multiple_of(i * SSM_SEGS, SSM_SEGS), SSM_SEGS)
                prev = pl.ds(pl.multiple_of((i - 1) * SSM_SEGS, SSM_SEGS), SSM_SEGS)
                qr, qi = q_re[rows, cols], q_im[rows, cols]
                sr, si = s_re[prev, cols], s_im[prev, cols]
                return acc[0] + qr * sr + qi * si, acc[1] + qi * sr - qr * si

            qr, qi = q_re[0:SSM_SEGS, cols], q_im[0:SSM_SEGS, cols]
            sr, si = cin_re_ref[:, cols], cin_im_ref[:, cols]
            acc = lax.fori_loop(1, SSM_SEG_LEN, step, (qr * sr + qi * si, qi * sr - qr * si))
            dar_ref[:, cols] += jnp.sum(acc[0], axis=0, keepdims=True)
            dai_ref[:, cols] += jnp.sum(acc[1], axis=0, keepdims=True)
        for t in range(SSM_TILES):
            cols = slice(t * SSM_TILE_GN, (t + 1) * SSM_TILE_GN)
            lanes = slice(t * LANES, (t + 1) * LANES)
            qrb, qib = q_re[:, cols].astype(BF16), q_im[:, cols].astype(BF16)
            u_t, dy_t = u_s[:, lanes], dy_s[:, lanes]
            ub = u_t.astype(BF16)
            dbr_ref[t] += _tn(ub, qrb)
            dbi_ref[t] += _tn(ub, qib)
            dd_ref[:, lanes] += jnp.sum(dy_t * u_t, axis=0, keepdims=True)
            u_s[:, lanes] = _nt(qrb, br_ref[t]) + _nt(qib, bi_ref[t]) + dy_t * d_ref[:, lanes]
        _ssm_store_rows(u_s, du_ref)

    whole2 = lambda a: pl.BlockSpec(a.shape, lambda c: (0, 0))
    whole3 = lambda a: pl.BlockSpec(a.shape, lambda c: (0, 0, 0))
    chunk = pl.BlockSpec((SSM_TILES, SSM_CHUNK, LANES), lambda c: (0, nchunk - 1 - c, 0))
    seg = pl.BlockSpec((SSM_SEGS, SSM_GN), lambda c: (nchunk - 1 - c, 0))
    gn_row = jax.ShapeDtypeStruct((1, SSM_GN), F32)
    b_shape = jax.ShapeDtypeStruct((SSM_TILES, LANES, SSM_TILE_GN), F32)
    c_shape = jax.ShapeDtypeStruct((SSM_TILES, SSM_TILE_GN, LANES), F32)
    d_shape = jax.ShapeDtypeStruct((1, SSM_WIDTH), F32)
    big = pltpu.VMEM((SSM_CHUNK, SSM_GN), F32)
    return pl.pallas_call(
        body,
        name=name,
        grid=(nchunk,),
        in_specs=[chunk, chunk, seg, seg, whole2(ab_re), whole2(ab_im), whole3(bblk_re), whole3(bblk_im),
                  whole3(cblk_re), whole3(cblk_im), whole2(d_row)],
        out_specs=[chunk, whole2(ab_re), whole2(ab_im), whole3(bblk_re), whole3(bblk_im), whole3(cblk_re),
                   whole3(cblk_im), whole2(d_row)],
        out_shape=[jax.ShapeDtypeStruct((SSM_TILES, n_rows, LANES), F32), gn_row, gn_row, b_shape, b_shape, c_shape,
                   c_shape, d_shape],
        scratch_shapes=[pltpu.VMEM((SSM_CHUNK, SSM_WIDTH), F32), pltpu.VMEM((SSM_CHUNK, SSM_WIDTH), F32),
                        big, big, big, big,
                        pltpu.VMEM((SSM_SEG_LEN, SSM_GN), F32), pltpu.VMEM((SSM_SEG_LEN, SSM_GN), F32),
                        pltpu.VMEM((1, SSM_GN), F32), pltpu.VMEM((1, SSM_GN), F32),
                        pltpu.VMEM((SSM_SEGS, SSM_GN), F32), pltpu.VMEM((SSM_SEGS, SSM_GN), F32)],
        compiler_params=_params(("arbitrary",)),
    )(u, dy, cin_re, cin_im, ab_re, ab_im, bblk_re, bblk_im, cblk_re, cblk_im, d_row)


def _mlp_fwd(h2, w1, w2, tm, tf, name):
    n_rows, dm = h2.shape
    dff = w1.shape[1]

    def body(h_ref, w1_ref, w2_ref, a_ref, y_ref):
        a = _nn(h_ref[...], w1_ref[...])
        a_ref[...] = a.astype(BF16)
        r = jnp.maximum(a, 0.0)
        part = _nn((r * r).astype(BF16), w2_ref[...])
        j = pl.program_id(1)

        @pl.when(j == 0)
        def _():
            y_ref[...] = part

        @pl.when(j > 0)
        def _():
            y_ref[...] += part

    return pl.pallas_call(
        body,
        name=name,
        grid=(n_rows // tm, dff // tf),
        in_specs=[pl.BlockSpec((tm, dm), lambda i, j: (i, 0)), pl.BlockSpec((dm, tf), lambda i, j: (0, j)),
                  pl.BlockSpec((tf, dm), lambda i, j: (j, 0))],
        out_specs=[pl.BlockSpec((tm, tf), lambda i, j: (i, j)), pl.BlockSpec((tm, dm), lambda i, j: (i, 0))],
        out_shape=[jax.ShapeDtypeStruct((n_rows, dff), BF16), jax.ShapeDtypeStruct((n_rows, dm), F32)],
        compiler_params=_params(("parallel", "arbitrary")),
    )(h2, w1, w2)


def _mlp_bwd(dy, h2, a, w2, tm, tf, name):
    n_rows, dm = h2.shape
    dff = a.shape[1]

    def body(dy_ref, h_ref, a_ref, w2_ref, da_ref, dw2_ref, dw1_ref):
        dyb = dy_ref[...]
        r = jnp.maximum(a_ref[...].astype(F32), 0.0)
        da = (_nt(dyb, w2_ref[...]) * (2.0 * r)).astype(BF16)
        da_ref[...] = da
        p2 = _tn((r * r).astype(BF16), dyb)
        p1 = _tn(h_ref[...], da)
        i = pl.program_id(1)

        @pl.when(i == 0)
        def _():
            dw2_ref[...] = p2
            dw1_ref[...] = p1

        @pl.when(i > 0)
        def _():
            dw2_ref[...] += p2
            dw1_ref[...] += p1

    return pl.pallas_call(
        body,
        name=name,
        grid=(dff // tf, n_rows // tm),
        in_specs=[pl.BlockSpec((tm, dm), lambda j, i: (i, 0)), pl.BlockSpec((tm, dm), lambda j, i: (i, 0)),
                  pl.BlockSpec((tm, tf), lambda j, i: (i, j)), pl.BlockSpec((tf, dm), lambda j, i: (j, 0))],
        out_specs=[pl.BlockSpec((tm, tf), lambda j, i: (i, j)), pl.BlockSpec((tf, dm), lambda j, i: (j, 0)),
                   pl.BlockSpec((dm, tf), lambda j, i: (0, j))],
        out_shape=[jax.ShapeDtypeStruct((n_rows, dff), BF16), jax.ShapeDtypeStruct((dff, dm), F32),
                   jax.ShapeDtypeStruct((dm, dff), F32)],
        compiler_params=_params(("parallel", "arbitrary")),
    )(dy, h2, a, w2)


def _local_step(x, pos_col, mod, target, wts, small):
    n_rows = x.shape[0]
    sh1, sc1, gt1, sh2, sc2, gt2 = (mod[:, i * D_MODEL:(i + 1) * D_MODEL] for i in range(N_MOD))
    tm = 256
    d_acc = (1, D_MODEL)

    (h1,) = _rowwise(lambda r, c: [_mod_norm(r[0], *c)], [x], [small["g_pre_mix"], sc1, sh1],
                     [(D_MODEL, BF16)], [], tm, "pre_mix_fwd")
    proj = _matmul(h1, wts["w_in"], "nn", F32, 512, 1408, 2048, "in_proj")

    freqs = ROPE_THETA ** (-jnp.arange(0, ROT_DIM, 2, dtype=F32) / ROT_DIM)
    freq_lane = jnp.tile(freqs, LANES // (ROT_DIM // 2))[None, :]
    tabs = _rope_tables(pos_col, freq_lane, "rope_tables")
    att = [_att_fwd(proj, tabs, gi, f"att_fwd_{gi}") for gi in range(3)]

    expand = jnp.asarray(_expand_np())
    b_re2, b_im2 = small["ssm_b_re"].reshape(SSM_G, -1), small["ssm_b_im"].reshape(SSM_G, -1)
    log_dt = small["ssm_log_dt"].reshape(SSM_G, 1)
    prep_in = (small["ssm_a_re"], small["ssm_a_im"], log_dt, b_re2, b_im2, expand)
    ab_re, ab_im, bb_re, bb_im = _ssm_prep(*prep_in, "ssm_prep")
    ab_re_row, ab_im_row = ab_re.reshape(1, SSM_GN), ab_im.reshape(1, SSM_GN)
    bblk = [_block_diag_in(t.reshape(SSM_G, SSM_N, SSM_P)).astype(BF16) for t in (bb_re, bb_im)]
    cblk = [_block_diag_out(small[k]).astype(BF16) for k in ("ssm_c_re", "ssm_c_im")]
    d_row = small["ssm_d"].reshape(1, SSM_WIDTH)
    u = _lane_tiled(proj[:, IN_WIDTH - SSM_WIDTH:])
    y_ssm, cin_re, cin_im = _ssm_fwd(u, ab_re_row, ab_im_row, *bblk, *cblk, d_row, "ssm_fwd")
    y_ssm = _lane_untiled(y_ssm)

    def mixers_out(r, c):
        w_glu, b_glu, g_att, g_ssm = c
        att_n = _att_mix(*r[:6], g_att)
        y2 = _gelu(r[6])
        z = _nn(y2.astype(BF16), w_glu) + b_glu
        return [jnp.concatenate([att_n.astype(BF16), _glu_out(y2, z, g_ssm).astype(BF16)], axis=1)]

    att_rows = [a[0] for a in att] + [a[1] for a in att]
    mix_consts = [wts["w_glu"], small["b_glu"], small["g_attn_out"], small["g_ssm_out"]]
    (cat,) = _rowwise(mixers_out, att_rows + [y_ssm], mix_consts, [(OUT_IN_WIDTH, BF16)], [], tm, "mixers_out_fwd")
    mix = _matmul(cat, wts["w_out"], "nn", F32, 512, 1024, 1280, "out_proj")

    post_consts = [small["g_post_mix"], gt1, small["g_pre_mlp"], sc2, sh2]
    x1, h2 = _rowwise(lambda r, c: list(_post_mix(r[0], r[1], *c)), [x, mix], post_consts,
                      [(D_MODEL, F32), (D_MODEL, BF16)], [], tm, "post_mix_fwd")
    a_mlp, y_mlp = _mlp_fwd(h2, wts["w_mlp_in"], wts["w_mlp_out"], 512, 512, "mlp_fwd")

    def loss_head(r, c):
        x1_v, y_v, t_v = r
        g, gt = c
        fn = lambda y_, g_, gt_: gt_ * _rms(y_, g_)
        out, vjp = jax.vjp(fn, y_v, g, gt)
        err = x1_v + out - t_v
        dx2 = err * (1.0 / D_MODEL)
        dy, dg, dgt = vjp(dx2)
        loss = 0.5 * jnp.sum(jnp.sum(err * err, axis=1, keepdims=True), axis=0, keepdims=True) * (1.0 / D_MODEL)
        return [dx2, dy, loss, dg, dgt]

    dx2, dy_mlp, loss, dg_post_mlp, dgt2 = _rowwise(
        loss_head, [x1, y_mlp, target], [small["g_post_mlp"], gt2],
        [(D_MODEL, F32), (D_MODEL, BF16)], [(1, 1), d_acc, d_acc], tm, "loss_head")

    da_mlp, dw_mlp_out, dw_mlp_in = _mlp_bwd(dy_mlp, h2, a_mlp, wts["w_mlp_out"], 512, 512, "mlp_bwd")
    dh2 = _matmul(da_mlp, wts["w_mlp_in"], "nt", F32, 512, 1024, 2048, "mlp_in_bwd")

    def post_mix_bwd(r, c):
        x_v, mix_v, dx1_v, dh2_v = r
        _, vjp = jax.vjp(_post_mix, x_v, mix_v, *c)
        return list(vjp((dx1_v, dh2_v)))

    dx_a, dmix, dg_post_mix, dgt1, dg_pre_mlp, dsc2, dsh2 = _rowwise(
        post_mix_bwd, [x, mix, dx2, dh2], post_consts, [(D_MODEL, F32), (D_MODEL, BF16)], [d_acc] * 5, tm,
        "post_mix_bwd")

    dcat = _matmul(dmix, wts["w_out"], "nt", F32, 512, 1280, 2048, "out_proj_bwd")
    dw_out = _matmul(cat, dmix, "tn", F32, 1280, 1024, 512, "out_proj_wgrad")

    def mixers_out_bwd(r, c):
        w_glu, b_glu, g_att, g_ssm = c
        dcat_v = r[7]
        _, vjp_att = jax.vjp(_att_mix, *r[:6], g_att)
        *d_ol, dg_att = vjp_att(dcat_v[:, :KV_WIDTH])
        y2, vjp_gelu = jax.vjp(_gelu, r[6])
        y2b = y2.astype(BF16)
        z = _nn(y2b, w_glu) + b_glu
        _, vjp_glu = jax.vjp(_glu_out, y2, z, g_ssm)
        dy2, dz, dg_ssm = vjp_glu(dcat_v[:, KV_WIDTH:])
        dzb = dz.astype(BF16)
        (dy,) = vjp_gelu(dy2 + _nt(dzb, w_glu))
        return d_ol + [dy, dg_att, _tn(y2b, dzb), jnp.sum(dz, axis=0, keepdims=True), dg_ssm]

    *d_att, dy_ssm, dg_attn_out, dw_glu, db_glu, dg_ssm_out = _rowwise(
        mixers_out_bwd, att_rows + [y_ssm, dcat], mix_consts,
        [(KV_WIDTH, F32)] * 6 + [(SSM_WIDTH, F32)],
        [(1, KV_WIDTH), (SSM_WIDTH, SSM_WIDTH), (1, SSM_WIDTH), (1, SSM_WIDTH)], tm, "mixers_out_bwd")

    du, dab_re, dab_im, dbblk_re, dbblk_im, dcblk_re, dcblk_im, dd_row = _ssm_bwd(
        u, _lane_tiled(dy_ssm), cin_re, cin_im, ab_re_row, ab_im_row, *bblk, *cblk, d_row, "ssm_bwd")
    du = _lane_untiled(du)
    prep_cts = (dab_re.reshape(SSM_G, SSM_N), dab_im.reshape(SSM_G, SSM_N),
                _block_diag_in_grad(dbblk_re).reshape(SSM_G, -1), _block_diag_in_grad(dbblk_im).reshape(SSM_G, -1))
    da_re, da_im, dlog_dt, db_re, db_im = _ssm_prep_bwd(*prep_in, prep_cts, "ssm_prep_bwd")

    dqkv = [_att_bwd(proj, tabs, att[gi][0], att[gi][1], d_att[gi], d_att[3 + gi], gi, f"att_bwd_{gi}")
            for gi in range(3)]

    def gather_dproj(r, c):
        dq = [r[0], r[3], r[6]]
        dk = r[1] + r[4] + r[7]
        dv = r[2] + r[5] + r[8]
        return [jnp.concatenate([t.astype(BF16) for t in dq + [dk, dv, r[9]]], axis=1)]

    (dproj,) = _rowwise(gather_dproj, [t for g in dqkv for t in g] + [du], [], [(IN_WIDTH, BF16)], [], tm,
                        "gather_dproj")
    dh1 = _matmul(dproj, wts["w_in"], "nt", F32, 512, 1024, 2816, "in_proj_bwd")
    dw_in = _matmul(h1, dproj, "tn", F32, 1024, 1408, 512, "in_proj_wgrad")

    def pre_mix_bwd(r, c):
        x_v, dh1_v, dxa_v = r
        _, vjp = jax.vjp(_mod_norm, x_v, *c)
        dx, dg, dsc, dsh = vjp(dh1_v)
        return [dx + dxa_v, dg, dsc, dsh]

    grad_x, dg_pre_mix, dsc1, dsh1 = _rowwise(
        pre_mix_bwd, [x, dh1, dx_a], [small["g_pre_mix"], sc1, sh1], [(D_MODEL, F32)], [d_acc] * 3, tm, "pre_mix_bwd")

    dmod = jnp.concatenate([dsh1, dsc1, dgt1, dsh2, dsc2, dgt2], axis=1)
    big = dict(w_in=dw_in, w_out=dw_out, w_mlp_in=dw_mlp_in, w_mlp_out=dw_mlp_out, w_glu=dw_glu)
    small_g = dict(
        g_pre_mix=dg_pre_mix, g_post_mix=dg_post_mix, ssm_a_re=da_re, ssm_a_im=da_im,
        ssm_log_dt=dlog_dt.reshape(1, SSM_G), ssm_b_re=db_re.reshape(SSM_G, SSM_N, SSM_P),
        ssm_b_im=db_im.reshape(SSM_G, SSM_N, SSM_P), ssm_c_re=_block_diag_out_grad(dcblk_re),
        ssm_c_im=_block_diag_out_grad(dcblk_im), ssm_d=dd_row.reshape(SSM_G, SSM_P), b_glu=db_glu,
        g_attn_out=dg_attn_out, g_ssm_out=dg_ssm_out, g_pre_mlp=dg_pre_mlp, g_post_mlp=dg_post_mlp)
    return loss, grad_x, dmod, big, small_g


MESH_ID = pl.DeviceIdType.MESH
N_DEV = 8
N_CHIPS = 4
HBM_SPEC = pl.BlockSpec(memory_space=pltpu.HBM)


def _place():
    x, y, c = lax.axis_index("x"), lax.axis_index("y"), lax.axis_index("c")
    other_chips = [(1 - x, y), (x, 1 - y), (1 - x, 1 - y)]
    return x, y, c, other_chips


def _half_rows(index, half):
    return pl.ds(pl.multiple_of(index * half, ROW_PAD), half)


def _remote(src, dst, send_sem, recv_sem, dev):
    return pltpu.make_async_remote_copy(src_ref=src, dst_ref=dst, send_sem=send_sem, recv_sem=recv_sem,
                                        device_id=dev, device_id_type=MESH_ID)


def _all_gather8(block, name):
    m_per, n = block.shape

    def body(x_ref, out_ref, send_sems, recv_sems, local_sem):
        x, y, c, chips = _place()
        me, sibling = (x, y, c), (x, y, 1 - c)

        def rows(px, py, pc):
            return out_ref.at[pl.ds((4 * px + 2 * py + pc) * m_per, m_per), :]

        def copy(k, blk, to, src=None):
            return _remote(rows(*blk) if src is None else src, rows(*blk), send_sems.at[k], recv_sems.at[k], to)

        mine = pltpu.make_async_copy(x_ref, rows(*me), local_sem)
        mine.start()
        first = [copy(0, me, sibling, src=x_ref)]
        first += [copy(1 + j, me, (*chip, c), src=x_ref) for j, chip in enumerate(chips)]
        for cp in first:
            cp.start()
        passed = [copy(4 + j, (*chip, c), sibling) for j, chip in enumerate(chips)]
        for j, chip in enumerate(chips):
            copy(1 + j, (*chip, c), me).wait_recv()
            passed[j].start()
        copy(0, sibling, me).wait_recv()
        for j, chip in enumerate(chips):
            copy(4 + j, (*chip, 1 - c), me).wait_recv()
        for cp in first + passed:
            cp.wait_send()
        mine.wait()

    return pl.pallas_call(
        body,
        name=name,
        out_shape=jax.ShapeDtypeStruct((N_DEV * m_per, n), block.dtype),
        in_specs=[pl.BlockSpec(memory_space=pltpu.VMEM)],
        out_specs=pl.BlockSpec(memory_space=pltpu.VMEM),
        scratch_shapes=[pltpu.SemaphoreType.DMA((7,)), pltpu.SemaphoreType.DMA((7,)), pltpu.SemaphoreType.DMA],
        compiler_params=_params(),
    )(block)


def _weight_gather(shards, name):
    n = len(shards)
    shapes = [s.shape for s in shards]

    def body(*refs):
        ins, outs = refs[:n], refs[n:2 * n]
        send, recv, fsend, frecv = refs[2 * n:]
        x, y, c, chips = _place()
        k_me = 2 * x + y
        sibling = (x, y, 1 - c)
        pending = []
        for a in range(n):
            half = shapes[a][0] // 2
            mine = _half_rows(c, half)
            for j, chip in enumerate(chips):
                cp = _remote(ins[a].at[mine, :], outs[a].at[k_me, mine, :], send.at[a, j], recv.at[a, j], (*chip, c))
                cp.start()
                pending.append(cp.wait_send)
        for a in range(n):
            half = shapes[a][0] // 2
            for j, (px, py) in enumerate(chips):
                piece = outs[a].at[2 * px + py, _half_rows(c, half), :]
                _remote(piece, piece, send.at[a, j], recv.at[a, j], (px, py, c)).wait_recv()
                fw = _remote(piece, piece, fsend.at[a, j], frecv.at[a, j], sibling)
                fw.start()
                pending.append(fw.wait_send)
        for a in range(n):
            half = shapes[a][0] // 2
            for j, (px, py) in enumerate(chips):
                piece = outs[a].at[2 * px + py, _half_rows(1 - c, half), :]
                _remote(piece, piece, fsend.at[a, j], frecv.at[a, j], sibling).wait_recv()
        for wait in pending:
            wait()

    sems = pltpu.SemaphoreType.DMA((n, 3))
    return pl.pallas_call(
        body,
        name=name,
        out_shape=[jax.ShapeDtypeStruct((N_CHIPS,) + s, BF16) for s in shapes],
        in_specs=[HBM_SPEC] * n,
        out_specs=[HBM_SPEC] * n,
        scratch_shapes=[sems, sems, sems, sems],
        compiler_params=_params(),
    )(*shards)


def _sibling_halves(stacks, name):
    n = len(stacks)
    shapes = [s.shape for s in stacks]

    def body(*refs):
        ins, outs = refs[:n], refs[n:2 * n]
        send, recv = refs[2 * n:]
        x, y, c, _ = _place()
        copies = []
        for a in range(n):
            half = shapes[a][1] // 2
            cp = _remote(ins[a].at[:, _half_rows(1 - c, half), :], outs[a], send.at[a], recv.at[a], (x, y, 1 - c))
            cp.start()
            copies.append(cp)
        for cp in copies:
            cp.wait()

    return pl.pallas_call(
        body,
        name=name,
        out_shape=[jax.ShapeDtypeStruct((N_CHIPS, s[1] // 2, s[2]), F32) for s in shapes],
        in_specs=[HBM_SPEC] * n,
        out_specs=[HBM_SPEC] * n,
        scratch_shapes=[pltpu.SemaphoreType.DMA((n,)), pltpu.SemaphoreType.DMA((n,))],
        compiler_params=_params(),
    )(*stacks)


def _chip_exchange(parts, name):
    n = len(parts)
    shapes = [p.shape for p in parts]

    def body(*refs):
        ins, outs = refs[:n], refs[n:2 * n]
        send, recv = refs[2 * n:]
        x, y, c, chips = _place()
        copies = []
        for a in range(n):
            for j, (px, py) in enumerate(chips):
                cp = _remote(ins[a].at[2 * px + py], outs[a].at[j], send.at[a, j], recv.at[a, j], (px, py, c))
                cp.start()
                copies.append(cp)
        for cp in copies:
            cp.wait()

    return pl.pallas_call(
        body,
        name=name,
        out_shape=[jax.ShapeDtypeStruct((3,) + s[1:], BF16) for s in shapes],
        in_specs=[HBM_SPEC] * n,
        out_specs=[HBM_SPEC] * n,
        scratch_shapes=[pltpu.SemaphoreType.DMA((n, 3)), pltpu.SemaphoreType.DMA((n, 3))],
        compiler_params=_params(),
    )(*parts)


def _sibling_swap(halves, name):
    n = len(halves)
    shapes = [h.shape for h in halves]

    def body(*refs):
        ins, outs = refs[:n], refs[n:2 * n]
        send, recv = refs[2 * n:]
        x, y, c, _ = _place()
        pending = []
        for a in range(n):
            half = shapes[a][0]
            mine = outs[a].at[_half_rows(c, half), :]
            cp = _remote(ins[a], mine, send.at[a], recv.at[a], (x, y, 1 - c))
            cp.start()
            pending.append(cp.wait_send)
        for a in range(n):
            half = shapes[a][0]
            theirs = outs[a].at[_half_rows(1 - c, half), :]
            _remote(theirs, theirs, send.at[a], recv.at[a], (x, y, 1 - c)).wait_recv()
        for wait in pending:
            wait()

    return pl.pallas_call(
        body,
        name=name,
        out_shape=[jax.ShapeDtypeStruct((2 * s[0], s[1]), F32) for s in shapes],
        in_specs=[HBM_SPEC] * n,
        out_specs=[HBM_SPEC] * n,
        scratch_shapes=[pltpu.SemaphoreType.DMA((n,)), pltpu.SemaphoreType.DMA((n,))],
        compiler_params=_params(),
    )(*halves)


ROW_PAD = 16


def _silu(x):
    return x * _sigmoid(x)


def _ada_fwd(c_all, w_ada, b_ada, name):
    dm, cols = w_ada.shape
    tn = 512

    def body(c_ref, w_ref, b_ref, o_ref):
        o_ref[...] = _nn(_silu(c_ref[...]).astype(BF16), w_ref[...].astype(BF16)) + b_ref[...]

    return pl.pallas_call(
        body,
        name=name,
        grid=(cols // tn,),
        in_specs=[pl.BlockSpec((ROW_PAD, dm), lambda j: (0, 0)), pl.BlockSpec((dm, tn), lambda j: (0, j)),
                  pl.BlockSpec((1, tn), lambda j: (0, j))],
        out_specs=pl.BlockSpec((ROW_PAD, tn), lambda j: (0, j)),
        out_shape=jax.ShapeDtypeStruct((ROW_PAD, cols), F32),
        compiler_params=_params(("parallel",)),
    )(c_all, w_ada, b_ada)


def _adamw(w, g, m, v):
    m = ADAM_B1 * m + (1.0 - ADAM_B1) * g
    v = ADAM_B2 * v + (1.0 - ADAM_B2) * (g * g)
    m_hat = m / (1.0 - ADAM_B1 ** ADAM_STEP)
    v_hat = v / (1.0 - ADAM_B2 ** ADAM_STEP)
    delta = -ADAM_LR * (m_hat / (jnp.sqrt(v_hat) + ADAM_EPS) + ADAM_WD * w)
    return delta, m, v


def _ada_bwd_adamw(c_all, dmod_cols, w, m, v, name):
    dm, cols = w.shape
    tm, tn = 512, 512

    def body(c_ref, d_ref, w_ref, m_ref, v_ref, g_ref, dl_ref, nm_ref, nv_ref):
        g = _tn(_silu(c_ref[...]).astype(BF16), d_ref[...].astype(BF16))
        g_ref[...] = g
        dl_ref[...], nm_ref[...], nv_ref[...] = _adamw(w_ref[...], g, m_ref[...], v_ref[...])

    tile = pl.BlockSpec((tm, tn), lambda i, j: (i, j))
    shape = jax.ShapeDtypeStruct((dm, cols), F32)
    return pl.pallas_call(
        body,
        name=name,
        grid=(dm // tm, cols // tn),
        in_specs=[pl.BlockSpec((ROW_PAD, tm), lambda i, j: (0, i)), pl.BlockSpec((ROW_PAD, tn), lambda i, j: (0, j)),
                  tile, tile, tile],
        out_specs=[tile] * 4,
        out_shape=[shape] * 4,
        compiler_params=_params(("parallel", "parallel")),
    )(c_all, dmod_cols, w, m, v)


def _sum_blocks(parts, nblk, name):
    rows, cols = parts.shape[0] // nblk, parts.shape[1]

    def body(p_ref, o_ref):
        tot = p_ref[0:rows, :]
        for b in range(1, nblk):
            tot = tot + p_ref[b * rows:(b + 1) * rows, :]
        o_ref[...] = tot

    return pl.pallas_call(body, name=name, out_shape=jax.ShapeDtypeStruct((rows, cols), F32), compiler_params=_params())(parts)


def _adamw_rows(w, g, m, v, tm, name):
    return _rowwise(lambda r, c: list(_adamw(*r)), [w, g, m, v], [], [(w.shape[1], F32)] * 3, [], tm, name)


BIG = ("w_in", "w_out", "w_mlp_in", "w_mlp_out", "w_glu")
COL_SHARDED = ("w_in", "w_out", "w_mlp_in")
SMALL = ("b_ada", "g_pre_mix", "g_post_mix", "ssm_a_re", "ssm_a_im", "ssm_log_dt", "ssm_b_re", "ssm_b_im",
         "ssm_c_re", "ssm_c_im", "ssm_d", "b_glu", "g_attn_out", "g_ssm_out", "g_pre_mlp", "g_post_mlp")
WEIGHTS = ("w_ada", "b_ada", "g_pre_mix", "g_post_mix", "w_in", "ssm_a_re", "ssm_a_im", "ssm_log_dt", "ssm_b_re",
           "ssm_b_im", "ssm_c_re", "ssm_c_im", "ssm_d", "w_glu", "b_glu", "g_attn_out", "g_ssm_out", "w_out",
           "g_pre_mlp", "g_post_mlp", "w_mlp_in", "w_mlp_out")
FLAT_COLS = 1024
FLAT_ROWS = 256
ROW_TILE = {"w_in": 256, "w_out": 128, "w_mlp_in": 256, "w_mlp_out": 256, "w_glu": 112}


def _flatten_small(tree):
    flat = jnp.concatenate([tree[k].reshape(-1) for k in SMALL])
    return jnp.pad(flat, (0, FLAT_ROWS * FLAT_COLS - flat.shape[0])).reshape(FLAT_ROWS, FLAT_COLS)


def _unflatten_small(flat, like):
    flat = flat.reshape(-1)
    out, at = {}, 0
    for k in SMALL:
        size = math.prod(like[k].shape)
        out[k] = flat[at:at + size].reshape(like[k].shape)
        at += size
    return out


def _unstack(stack, name):
    if name in COL_SHARDED:
        return stack.transpose(1, 0, 2).reshape(stack.shape[1], N_CHIPS * stack.shape[2])
    return stack.reshape(N_CHIPS * stack.shape[1], stack.shape[2])


def _stack(full, name):
    if name in COL_SHARDED:
        return full.reshape(full.shape[0], N_CHIPS, full.shape[1] // N_CHIPS).transpose(1, 0, 2)
    return full.reshape(N_CHIPS, full.shape[0] // N_CHIPS, full.shape[1])


def _pad_rows(row):
    return jnp.pad(row, ((0, 8 - row.shape[0]), (0, 0)))


def _every_eighth(gathered):
    rows = gathered.reshape(N_DEV, 8, gathered.shape[1])[:, 0, :]
    return jnp.pad(rows, ((0, ROW_PAD - N_DEV), (0, 0)))


def kernel(x, c, positions, w_ada, b_ada, g_pre_mix, g_post_mix, w_in, ssm_a_re, ssm_a_im, ssm_log_dt, ssm_b_re, ssm_b_im, ssm_c_re, ssm_c_im, ssm_d, w_glu, b_glu, g_attn_out, g_ssm_out, w_out, g_pre_mlp, g_post_mlp, w_mlp_in, w_mlp_out, loss_target, m_w_ada, m_b_ada, m_g_pre_mix, m_g_post_mix, m_w_in, m_ssm_a_re, m_ssm_a_im, m_ssm_log_dt, m_ssm_b_re, m_ssm_b_im, m_ssm_c_re, m_ssm_c_im, m_ssm_d, m_w_glu, m_b_glu, m_g_attn_out, m_g_ssm_out, m_w_out, m_g_pre_mlp, m_g_post_mlp, m_w_mlp_in, m_w_mlp_out, v_w_ada, v_b_ada, v_g_pre_mix, v_g_post_mix, v_w_in, v_ssm_a_re, v_ssm_a_im, v_ssm_log_dt, v_ssm_b_re, v_ssm_b_im, v_ssm_c_re, v_ssm_c_im, v_ssm_d, v_w_glu, v_b_glu, v_g_attn_out, v_g_ssm_out, v_w_out, v_g_pre_mlp, v_g_post_mlp, v_w_mlp_in, v_w_mlp_out):
    given = dict(locals())
    w = {k: given[k][0] for k in WEIGHTS}
    mom = {k: given["m_" + k][0] for k in WEIGHTS}
    var = {k: given["v_" + k][0] for k in WEIGHTS}
    for tree in (w, mom, var):
        for k in ("b_ada", "g_pre_mix", "g_post_mix", "ssm_log_dt", "b_glu", "g_attn_out", "g_ssm_out", "g_pre_mlp",
                  "g_post_mlp"):
            tree[k] = tree[k].reshape(1, -1)
    ix, iy, ic = lax.axis_index("x"), lax.axis_index("y"), lax.axis_index("c")
    chip = 2 * ix + iy
    me = 4 * ix + 2 * iy + ic
    shard_cols = w["w_ada"].shape[1]

    c_all = _every_eighth(_all_gather8(_pad_rows(c), "gather_c"))
    b_ada_cols = lax.dynamic_slice_in_dim(w["b_ada"], chip * shard_cols, shard_cols, axis=1)
    mod_cols = _ada_fwd(c_all, w["w_ada"], b_ada_cols, "ada_fwd")[:N_DEV]
    mod_all = _all_gather8(mod_cols, "gather_mod").reshape(N_CHIPS, 2, N_DEV, shard_cols)[:, 0]
    mod = lax.dynamic_index_in_dim(mod_all, me, axis=1, keepdims=False).reshape(1, N_MOD * D_MODEL)

    own_bf16 = [w[k].astype(BF16) for k in BIG]
    stacks = _weight_gather(own_bf16, "weight_gather")
    stacks = [lax.dynamic_update_index_in_dim(s, o, chip, 0) for s, o in zip(stacks, own_bf16)]
    wts = {k: _unstack(s, k) for k, s in zip(BIG, stacks)}

    small = {k: w[k] for k in SMALL if k != "b_ada"}
    loss, grad_x, dmod, big_g, small_g = _local_step(x[0], positions.reshape(-1, 1), mod, loss_target[0], wts, small)
    loss = lax.psum(loss[0, 0], ("x", "y", "c"))

    g_stacks = [_stack(big_g[k], k) for k in BIG]
    from_sibling = _sibling_halves(g_stacks, "grad_sibling_halves")
    chip_f32, chip_bf16 = [], []
    for k, gs, fs in zip(BIG, g_stacks, from_sibling):
        half = gs.shape[1] // 2
        mine = lax.dynamic_slice_in_dim(gs, ic * half, half, axis=1).reshape(N_CHIPS * half, gs.shape[2])
        s32, s16 = _rowwise(lambda r, cc: [r[0] + r[1]] * 2, [mine, fs.reshape(mine.shape)], [],
                            [(gs.shape[2], F32), (gs.shape[2], BF16)], [], ROW_TILE[k], "grad_chip_sum_" + k)
        chip_f32.append(lax.dynamic_slice_in_dim(s32, chip * half, half, axis=0))
        chip_bf16.append(s16.reshape(N_CHIPS, half, gs.shape[2]))
    from_chips = _chip_exchange(chip_bf16, "grad_chip_exchange")
    reduced = []
    for k, own, fc in zip(BIG, chip_f32, from_chips):
        (tot,) = _rowwise(lambda r, cc: [r[0] + r[1].astype(F32) + r[2].astype(F32) + r[3].astype(F32)],
                          [own, fc[0], fc[1], fc[2]], [], [(own.shape[1], F32)], [], ROW_TILE[k], "grad_total_" + k)
        reduced.append(tot)
    swapped = _sibling_swap(reduced, "grad_sibling_swap")
    grads = {k: lax.dynamic_update_slice_in_dim(s, r, ic * r.shape[0], axis=0)
             for k, s, r in zip(BIG, swapped, reduced)}

    small_g["b_ada"] = dmod
    parts = _all_gather8(_flatten_small(small_g), "gather_small_grads")

    small_flat = _sum_blocks(parts, N_DEV, "small_grad_sum")
    grads.update(_unflatten_small(small_flat, w))

    dmod_all = _every_eighth(_all_gather8(_pad_rows(dmod), "gather_dmod"))
    dmod_cols = lax.dynamic_slice_in_dim(dmod_all, chip * shard_cols, shard_cols, axis=1)
    g_ada, d_ada, m_ada, v_ada = _ada_bwd_adamw(c_all, dmod_cols, w["w_ada"], mom["w_ada"], var["w_ada"], "ada_bwd_adamw")
    grads["w_ada"] = g_ada

    delta, new_m, new_v = {"w_ada": d_ada}, {"w_ada": m_ada}, {"w_ada": v_ada}
    for k in BIG:
        delta[k], new_m[k], new_v[k] = _adamw_rows(w[k], grads[k], mom[k], var[k], ROW_TILE[k], "adamw_" + k)
    flat_upd = _adamw_rows(_flatten_small(w), small_flat, _flatten_small(mom), _flatten_small(var), FLAT_ROWS,
                           "adamw_small")
    for tree, flat in zip((delta, new_m, new_v), flat_upd):
        tree.update(_unflatten_small(flat, w))

    shaped = lambda tree: [tree[k].reshape(given[k].shape) for k in WEIGHTS]
    return (loss, grad_x[None], *shaped(grads), *shaped(delta), *shaped(new_m), *shaped(new_v))
```

```python
import functools
import math

import jax
import jax.numpy as jnp
import numpy as np
from jax import lax
from jax.experimental import pallas as pl
from jax.experimental.pallas import tpu as pltpu

F32 = jnp.float32
BF16 = jnp.bfloat16

D_MODEL = 2048
HEAD_DIM = 64
DILATIONS = (1, 4, 16)
ATT_SPAN = 128
ATT_BLK = 128
HEADS_PER_GROUP = 6
KV_WIDTH = HEADS_PER_GROUP * HEAD_DIM
ATT_Q_WIDTH = 3 * KV_WIDTH
ROT_DIM = 16
ROPE_THETA = 500000.0
SSM_WIDTH = 896
SSM_P = 16
SSM_G = 56
SSM_N = 64
SSM_GN = SSM_G * SSM_N
SSM_TILES = SSM_WIDTH // 128
SSM_TILE_GN = 8 * SSM_N
IN_WIDTH = 2816
OUT_IN_WIDTH = 1280
D_FF = 8192
N_MOD = 6
EPS = 1e-6
LANES = 128
SSM_SEGS = 8
SSM_CHUNK = 256
SSM_SEG_LEN = SSM_CHUNK // SSM_SEGS

ADAM_LR = 0.001
ADAM_B1 = 0.9
ADAM_B2 = 0.999
ADAM_EPS = 1e-08
ADAM_WD = 0.01
ADAM_STEP = 10

VMEM_LIMIT = 56 * 1024 * 1024


def _params(sem=None):
    return pltpu.CompilerParams(dimension_semantics=sem, vmem_limit_bytes=VMEM_LIMIT)


def _dot(a, b, dims):
    return lax.dot_general(a, b, (dims, ((), ())), preferred_element_type=F32)


def _nn(a, b):
    return _dot(a, b, ((1,), (0,)))


def _nt(a, b):
    return _dot(a, b, ((1,), (1,)))


def _tn(a, b):
    return _dot(a, b, ((0,), (0,)))


def _matmul(a, b, mode, out_dtype, tm, tn, tk, name, after=None):
    if mode == "nn":
        (m, k), (_, n) = a.shape, b.shape
        a_spec = pl.BlockSpec((tm, tk), lambda i, j, kk: (i, kk))
        b_spec = pl.BlockSpec((tk, tn), lambda i, j, kk: (kk, j))
        op = _nn
    elif mode == "nt":
        (m, k), (n, _) = a.shape, b.shape
        a_spec = pl.BlockSpec((tm, tk), lambda i, j, kk: (i, kk))
        b_spec = pl.BlockSpec((tn, tk), lambda i, j, kk: (j, kk))
        op = _nt
    else:
        (k, m), (_, n) = a.shape, b.shape
        a_spec = pl.BlockSpec((tk, tm), lambda i, j, kk: (kk, i))
        b_spec = pl.BlockSpec((tk, tn), lambda i, j, kk: (kk, j))
        op = _tn
    assert m % tm == 0 and n % tn == 0 and k % tk == 0, (name, m, n, k)
    nk = k // tk

    def body(a_ref, b_ref, *rest):
        o_ref, acc_ref = rest[-2:]
        kk = pl.program_id(2)

        @pl.when(kk == 0)
        def _():
            acc_ref[...] = jnp.zeros_like(acc_ref)

        acc_ref[...] += op(a_ref[...], b_ref[...])

        @pl.when(kk == nk - 1)
        def _():
            o_ref[...] = acc_ref[...].astype(o_ref.dtype)

    extra = [] if after is None else [after]
    return pl.pallas_call(
        body,
        name=name,
        grid=(m // tm, n // tn, nk),
        in_specs=[a_spec, b_spec] + [pl.BlockSpec(t.shape, lambda i, j, kk: (0, 0)) for t in extra],
        out_specs=pl.BlockSpec((tm, tn), lambda i, j, kk: (i, j)),
        out_shape=jax.ShapeDtypeStruct((m, n), out_dtype),
        scratch_shapes=[pltpu.VMEM((tm, tn), F32)],
        compiler_params=_params(("parallel", "parallel", "arbitrary")),
    )(a, b, *extra)


def _rowwise(fn, rows, consts, out_rows, out_accs, tm, name):
    n_rows = rows[0].shape[0]
    assert n_rows % tm == 0
    nr, nc, no = len(rows), len(consts), len(out_rows)

    def body(*refs):
        r_in, c_in = refs[:nr], refs[nr:nr + nc]
        o_row, o_acc = refs[nr + nc:nr + nc + no], refs[nr + nc + no:]
        outs = fn([r[...] for r in r_in], [c[...] for c in c_in])
        assert len(outs) == len(o_row) + len(o_acc), name
        for ref, v in zip(o_row, outs[:no]):
            ref[...] = v.astype(ref.dtype)
        first = pl.program_id(0) == 0
        for ref, v in zip(o_acc, outs[no:]):
            @pl.when(first)
            def _(ref=ref, v=v):
                ref[...] = v.astype(F32)

            @pl.when(jnp.logical_not(first))
            def _(ref=ref, v=v):
                ref[...] += v.astype(F32)

    in_specs = [pl.BlockSpec((tm, r.shape[1]), lambda i: (i, 0)) for r in rows]
    in_specs += [pl.BlockSpec(c.shape, lambda i: (0, 0)) for c in consts]
    out_specs = [pl.BlockSpec((tm, w), lambda i: (i, 0)) for w, _ in out_rows]
    out_specs += [pl.BlockSpec(s, lambda i: (0, 0)) for s in out_accs]
    out_shape = [jax.ShapeDtypeStruct((n_rows, w), dt) for w, dt in out_rows]
    out_shape += [jax.ShapeDtypeStruct(s, F32) for s in out_accs]
    return pl.pallas_call(
        body,
        name=name,
        grid=(n_rows // tm,),
        in_specs=in_specs,
        out_specs=out_specs,
        out_shape=out_shape,
        compiler_params=_params(("arbitrary",)),
    )(*rows, *consts)


def _rms(x, g):
    return x * lax.rsqrt(jnp.mean(x * x, axis=-1, keepdims=True) + EPS) * g


def _mod_norm(x, g, sc, sh):
    return _rms(x, g) * (1.0 + sc) + sh


def _gelu(x):
    return 0.5 * x * (1.0 + jnp.tanh(math.sqrt(2.0 / math.pi) * (x + 0.044715 * (x * x * x))))


def _sigmoid(x):
    return 1.0 / (1.0 + jnp.exp(-x))


def _post_mix(x, mix, g_post, gt1, g_pre, sc2, sh2):
    x1 = x + gt1 * _rms(mix, g_post)
    return x1, _mod_norm(x1, g_pre, sc2, sh2)


def _att_mix(o0, o1, o2, l0, l1, l2, g):
    m = jnp.maximum(jnp.maximum(l0, l1), l2)
    e0, e1, e2 = jnp.exp(l0 - m), jnp.exp(l1 - m), jnp.exp(l2 - m)
    att = (e0 * o0 + e1 * o1 + e2 * o2) / (e0 + e1 + e2)
    return _rms(att, g)


def _glu_out(y2, z, g):
    return _rms(y2 * _sigmoid(z), g)


def _rope_tables(pos_col, freq_lane, name):
    n_rows = pos_col.shape[0]
    tm = 512

    def body(p_ref, f_ref, cos_ref, lo_ref, hi_ref):
        ang = p_ref[...].astype(F32) * f_ref[...]
        lane = lax.broadcasted_iota(jnp.int32, ang.shape, 1) % HEAD_DIM
        c, s = jnp.cos(ang), jnp.sin(ang)
        cos_ref[...] = jnp.where(lane < ROT_DIM, c, 1.0)
        lo_ref[...] = jnp.where(lane < ROT_DIM // 2, -s, 0.0)
        hi_ref[...] = jnp.where((lane >= ROT_DIM // 2) & (lane < ROT_DIM), s, 0.0)

    tab = jax.ShapeDtypeStruct((n_rows, LANES), F32)
    return pl.pallas_call(
        body,
        name=name,
        grid=(n_rows // tm,),
        in_specs=[pl.BlockSpec((tm, 1), lambda i: (i, 0)), pl.BlockSpec((1, LANES), lambda i: (0, 0))],
        out_specs=[pl.BlockSpec((tm, LANES), lambda i: (i, 0))] * 3,
        out_shape=[tab] * 3,
        compiler_params=_params(("parallel",)),
    )(pos_col, freq_lane)


def _rope(x, cos_t, lo_t, hi_t):
    half = ROT_DIM // 2
    return x * cos_t + pltpu.roll(x, LANES - half, 1) * lo_t + pltpu.roll(x, half, 1) * hi_t


def _rope_transposed(dy, cos_t, lo_t, hi_t):
    half = ROT_DIM // 2
    return dy * cos_t + pltpu.roll(dy * lo_t, half, 1) + pltpu.roll(dy * hi_t, LANES - half, 1)


def _att_masks(i, k0):
    q_pos = i * ATT_BLK + lax.broadcasted_iota(jnp.int32, (ATT_BLK, 2 * ATT_BLK), 0)
    k_pos = k0 + lax.broadcasted_iota(jnp.int32, (ATT_BLK, 2 * ATT_BLK), 1)
    dist = q_pos - k_pos
    return (dist >= 0) & (dist <= ATT_SPAN)


def _head_lane_masks():
    lane = lax.broadcasted_iota(jnp.int32, (1, LANES), 1)
    return lane < HEAD_DIM, lane >= HEAD_DIM


def _att_specs(gi, d, n):
    cols = IN_WIDTH // LANES
    qkv = [
        pl.BlockSpec((n, LANES), lambda r, hp: (0, r * cols + gi * 3 + hp)),
        pl.BlockSpec((n, LANES), lambda r, hp: (0, r * cols + 9 + hp)),
        pl.BlockSpec((n, LANES), lambda r, hp: (0, r * cols + 12 + hp)),
    ]
    tabs = [pl.BlockSpec((n, LANES), lambda r, hp: (0, r))] * 3
    head = pl.BlockSpec((n, LANES), lambda r, hp: (0, r * 3 + hp))
    return qkv, tabs, head


def _att_load(q_ref, k_ref, v_ref, cos_ref, lo_ref, hi_ref, qs, ks, vs):
    cos_t, lo_t, hi_t = cos_ref[...], lo_ref[...], hi_ref[...]
    qs[...] = (_rope(q_ref[...], cos_t, lo_t, hi_t) * (1.0 / math.sqrt(HEAD_DIM))).astype(BF16)
    ks[...] = _rope(k_ref[...], cos_t, lo_t, hi_t).astype(BF16)
    vs[...] = v_ref[...].astype(BF16)


def _att_fwd(proj, tabs, gi, name):
    n_rows = proj.shape[0]
    d = DILATIONS[gi]
    n = n_rows // d
    nb = n // ATT_BLK
    proj_v = proj.reshape(n, d * IN_WIDTH)
    tabs_v = [t.reshape(n, d * LANES) for t in tabs]

    def body(q_ref, k_ref, v_ref, cos_ref, lo_ref, hi_ref, o_ref, l_ref, qs, ks, vs):
        _att_load(q_ref, k_ref, v_ref, cos_ref, lo_ref, hi_ref, qs, ks, vs)
        m0, m1 = _head_lane_masks()

        def step(i, carry):
            k0 = pl.multiple_of(jnp.maximum(i - 1, 0) * ATT_BLK, ATT_BLK)
            q0 = pl.multiple_of(i * ATT_BLK, ATT_BLK)
            q = qs[pl.ds(q0, ATT_BLK), :]
            k = ks[pl.ds(k0, 2 * ATT_BLK), :]
            v = vs[pl.ds(k0, 2 * ATT_BLK), :]
            valid = _att_masks(i, k0)
            outs, lses = [], []
            for hm in (m0, m1):
                s = _nt(jnp.where(hm, q, jnp.zeros_like(q)), k)
                s = jnp.where(valid, s, -1e30)
                mx = jnp.max(s, axis=1, keepdims=True)
                p = jnp.exp(s - mx)
                den = jnp.sum(p, axis=1, keepdims=True)
                outs.append(_nn(p.astype(BF16), v) / den)
                lses.append(mx + jnp.log(den))
            o_ref[pl.ds(q0, ATT_BLK), :] = jnp.where(m0, outs[0], outs[1])
            l_ref[pl.ds(q0, ATT_BLK), :] = jnp.where(m0, lses[0], lses[1])
            return carry

        lax.fori_loop(0, nb, step, 0)

    qkv, tab_specs, head = _att_specs(gi, d, n)
    out = jax.ShapeDtypeStruct((n, d * KV_WIDTH), F32)
    o, l = pl.pallas_call(
        body,
        name=name,
        grid=(d, 3),
        in_specs=qkv + tab_specs,
        out_specs=[head, head],
        out_shape=[out, out],
        scratch_shapes=[pltpu.VMEM((n, LANES), BF16)] * 3,
        compiler_params=_params(("parallel", "parallel")),
    )(proj_v, proj_v, proj_v, *tabs_v)
    return o.reshape(n_rows, KV_WIDTH), l.reshape(n_rows, KV_WIDTH)


def _att_bwd(proj, tabs, o, l, do, dl, gi, name):
    n_rows = proj.shape[0]
    d = DILATIONS[gi]
    n = n_rows // d
    nb = n // ATT_BLK
    proj_v = proj.reshape(n, d * IN_WIDTH)
    tabs_v = [t.reshape(n, d * LANES) for t in tabs]
    heads_v = [t.reshape(n, d * KV_WIDTH) for t in (o, l, do, dl)]

    def body(q_ref, k_ref, v_ref, cos_ref, lo_ref, hi_ref, o_ref, l_ref, do_ref, dl_ref,
             dq_ref, dk_ref, dv_ref, qs, ks, vs, dk_acc, dv_acc):
        _att_load(q_ref, k_ref, v_ref, cos_ref, lo_ref, hi_ref, qs, ks, vs)
        dk_acc[...] = jnp.zeros_like(dk_acc)
        dv_acc[...] = jnp.zeros_like(dv_acc)
        m0, m1 = _head_lane_masks()

        def step(i, carry):
            k0 = pl.multiple_of(jnp.maximum(i - 1, 0) * ATT_BLK, ATT_BLK)
            q0 = pl.multiple_of(i * ATT_BLK, ATT_BLK)
            rows = pl.ds(q0, ATT_BLK)
            keys = pl.ds(k0, 2 * ATT_BLK)
            q, k, v = qs[rows, :], ks[keys, :], vs[keys, :]
            d_o, lse = do_ref[rows, :], l_ref[rows, :]
            o_do = o_ref[rows, :] * d_o
            d_l = dl_ref[rows, :]
            valid = _att_masks(i, k0)
            dq = jnp.zeros((ATT_BLK, LANES), F32)
            dk = jnp.zeros((2 * ATT_BLK, LANES), F32)
            dv = jnp.zeros((2 * ATT_BLK, LANES), F32)
            for hm in (m0, m1):
                qh, kh = jnp.where(hm, q, jnp.zeros_like(q)), jnp.where(hm, k, jnp.zeros_like(k))
                doh = jnp.where(hm, d_o, 0.0).astype(BF16)
                lse_h = jnp.max(jnp.where(hm, lse, -1e30), axis=1, keepdims=True)
                delta = jnp.sum(jnp.where(hm, o_do, 0.0), axis=1, keepdims=True)
                dlse = jnp.sum(jnp.where(hm, d_l, 0.0), axis=1, keepdims=True)
                s = jnp.where(valid, _nt(qh, k), -1e30)
                p = jnp.exp(s - lse_h)
                dv = dv + _tn(p.astype(BF16), doh)
                ds = (p * (_nt(doh, v) - delta + dlse)).astype(BF16)
                dq = dq + _nn(ds, kh)
                dk = dk + _tn(ds, qh)
            cos_t, lo_t, hi_t = cos_ref[rows, :], lo_ref[rows, :], hi_ref[rows, :]
            dq_ref[rows, :] = _rope_transposed(dq * (1.0 / math.sqrt(HEAD_DIM)), cos_t, lo_t, hi_t)
            dk_acc[keys, :] += dk
            dv_acc[keys, :] += dv
            return carry

        lax.fori_loop(0, nb, step, 0)
        dk_ref[...] = _rope_transposed(dk_acc[...], cos_ref[...], lo_ref[...], hi_ref[...])
        dv_ref[...] = dv_acc[...]

    qkv, tab_specs, head = _att_specs(gi, d, n)
    out = jax.ShapeDtypeStruct((n, d * KV_WIDTH), F32)
    res = pl.pallas_call(
        body,
        name=name,
        grid=(d, 3),
        in_specs=qkv + tab_specs + [head] * 4,
        out_specs=[head] * 3,
        out_shape=[out] * 3,
        scratch_shapes=[pltpu.VMEM((n, LANES), BF16)] * 3 + [pltpu.VMEM((n, LANES), F32)] * 2,
        compiler_params=_params(("parallel", "parallel")),
    )(proj_v, proj_v, proj_v, *tabs_v, *heads_v)
    return [t.reshape(n_rows, KV_WIDTH) for t in res]


def _expand_np():
    e = np.zeros((SSM_N, SSM_N * SSM_P), np.float32)
    for nn in range(SSM_N):
        e[nn, nn * SSM_P:(nn + 1) * SSM_P] = 1.0
    return e


def _ssm_prep_math(a_re, a_im, log_dt, b_re, b_im, expand):
    dt = jnp.exp(log_dt)
    mag = jnp.exp(a_re * dt)
    ab_re, ab_im = mag * jnp.cos(a_im * dt), mag * jnp.sin(a_im * dt)
    den = a_re * a_re + a_im * a_im
    num_re, num_im = ab_re - 1.0, ab_im
    co_re = (num_re * a_re + num_im * a_im) / den
    co_im = (num_im * a_re - num_re * a_im) / den
    hi = lax.Precision.HIGHEST
    co_re_x = jnp.dot(co_re, expand, precision=hi, preferred_element_type=F32)
    co_im_x = jnp.dot(co_im, expand, precision=hi, preferred_element_type=F32)
    bb_re = co_re_x * b_re - co_im_x * b_im
    bb_im = co_re_x * b_im + co_im_x * b_re
    return ab_re, ab_im, bb_re, bb_im


def _ssm_prep(a_re, a_im, log_dt, b_re, b_im, expand, name):
    def body(ar, ai, ld, br, bi, ex, o0, o1, o2, o3):
        outs = _ssm_prep_math(ar[...], ai[...], ld[...], br[...], bi[...], ex[...])
        for ref, v in zip((o0, o1, o2, o3), outs):
            ref[...] = v

    gn = jax.ShapeDtypeStruct((SSM_G, SSM_N), F32)
    gnp = jax.ShapeDtypeStruct((SSM_G, SSM_N * SSM_P), F32)
    return pl.pallas_call(body, name=name, out_shape=[gn, gn, gnp, gnp], compiler_params=_params())(
        a_re, a_im, log_dt, b_re, b_im, expand)


def _ssm_prep_bwd(a_re, a_im, log_dt, b_re, b_im, expand, cts, name):
    def body(ar, ai, ld, br, bi, ex, c0, c1, c2, c3, o0, o1, o2, o3, o4):
        ex_v = ex[...]
        _, vjp = jax.vjp(lambda *p: _ssm_prep_math(*p, ex_v), ar[...], ai[...], ld[...], br[...], bi[...])
        for ref, v in zip((o0, o1, o2, o3, o4), vjp((c0[...], c1[...], c2[...], c3[...]))):
            ref[...] = v

    gn = jax.ShapeDtypeStruct((SSM_G, SSM_N), F32)
    gnp = jax.ShapeDtypeStruct((SSM_G, SSM_N * SSM_P), F32)
    g1 = jax.ShapeDtypeStruct((SSM_G, 1), F32)
    return pl.pallas_call(body, name=name, out_shape=[gn, gn, g1, gnp, gnp], compiler_params=_params())(
        a_re, a_im, log_dt, b_re, b_im, expand, *cts)


def _block_diag_in(bb):
    t = bb.reshape(SSM_TILES, 8, SSM_N, SSM_P).transpose(0, 1, 3, 2)
    eye = jnp.eye(8, dtype=bb.dtype)
    return (t[:, :, :, None, :] * eye[None, :, None, :, None]).reshape(SSM_TILES, LANES, SSM_TILE_GN)


def _block_diag_in_grad(dblk):
    t = dblk.reshape(SSM_TILES, 8, SSM_P, 8, SSM_N)
    t = jnp.einsum("tapbn,ab->tapn", t, jnp.eye(8, dtype=dblk.dtype))
    return t.transpose(0, 1, 3, 2).reshape(SSM_G, SSM_N, SSM_P)


def _block_diag_out(cm):
    t = cm.reshape(SSM_TILES, 8, SSM_P, SSM_N).transpose(0, 1, 3, 2)
    eye = jnp.eye(8, dtype=cm.dtype)
    return (t[:, :, :, None, :] * eye[None, :, None, :, None]).reshape(SSM_TILES, SSM_TILE_GN, LANES)


def _block_diag_out_grad(dblk):
    t = dblk.reshape(SSM_TILES, 8, SSM_N, 8, SSM_P)
    t = jnp.einsum("tanbp,ab->tanp", t, jnp.eye(8, dtype=dblk.dtype))
    return t.transpose(0, 1, 3, 2).reshape(SSM_G, SSM_P, SSM_N)


def _cmul_add(a_re, a_im, s_re, s_im, b_re, b_im):
    return a_re * s_re - a_im * s_im + b_re, a_re * s_im + a_im * s_re + b_im


def _lane_tiled(a):
    return a.reshape(a.shape[0], SSM_TILES, LANES).transpose(1, 0, 2)


def _lane_untiled(a):
    return a.transpose(1, 0, 2).reshape(a.shape[1], SSM_WIDTH)


def _ssm_load_rows(src_ref, dst):
    for t in range(SSM_TILES):
        for i in range(SSM_SEG_LEN):
            dst[i * SSM_SEGS:(i + 1) * SSM_SEGS, t * LANES:(t + 1) * LANES] = (
                src_ref[t, pl.ds(i, SSM_SEGS, stride=SSM_SEG_LEN), :])


def _ssm_store_rows(src, dst_ref):
    for t in range(SSM_TILES):
        for i in range(SSM_SEG_LEN):
            dst_ref[t, pl.ds(i, SSM_SEGS, stride=SSM_SEG_LEN), :] = (
                src[i * SSM_SEGS:(i + 1) * SSM_SEGS, t * LANES:(t + 1) * LANES])


def _ssm_powers(ab_re_ref, ab_im_ref, pw_re, pw_im):
    a_re, a_im = ab_re_ref[...], ab_im_ref[...]
    p_re, p_im = a_re, a_im
    for i in range(SSM_SEG_LEN):
        pw_re[i:i + 1, :] = p_re
        pw_im[i:i + 1, :] = p_im
        p_re, p_im = _cmul_add(a_re, a_im, p_re, p_im, 0.0, 0.0)


def _ssm_input_proj(u_s, bblk_re_ref, bblk_im_ref, s_re, s_im):
    for t in range(SSM_TILES):
        ub = u_s[:, t * LANES:(t + 1) * LANES].astype(BF16)
        cols = slice(t * SSM_TILE_GN, (t + 1) * SSM_TILE_GN)
        s_re[:, cols] = _nn(ub, bblk_re_ref[t])
        s_im[:, cols] = _nn(ub, bblk_im_ref[t])


def _ssm_scan(ab_re_ref, ab_im_ref, s_re, s_im, init_re, init_im, conj, reverse):
    sign = -1.0 if conj else 1.0
    for t in range(SSM_TILES):
        cols = slice(t * SSM_TILE_GN, (t + 1) * SSM_TILE_GN)
        a_re = jnp.broadcast_to(ab_re_ref[:, cols], (SSM_SEGS, SSM_TILE_GN))
        a_im = jnp.broadcast_to(ab_im_ref[:, cols], (SSM_SEGS, SSM_TILE_GN)) * sign
        if init_re is None:
            st = (jnp.zeros((SSM_SEGS, SSM_TILE_GN), F32),) * 2
        else:
            st = (init_re[:, cols], init_im[:, cols])

        def step(i, st, cols=cols, a_re=a_re, a_im=a_im):
            idx = (SSM_SEG_LEN - 1 - i) if reverse else i
            rows = pl.ds(pl.multiple_of(idx * SSM_SEGS, SSM_SEGS), SSM_SEGS)
            n_re, n_im = _cmul_add(a_re, a_im, st[0], st[1], s_re[rows, cols], s_im[rows, cols])
            s_re[rows, cols] = n_re
            s_im[rows, cols] = n_im
            return n_re, n_im
        lax.fori_loop(0, SSM_SEG_LEN, step, st, unroll=4)


def _ssm_fixup(pw_re, pw_im, s_re, s_im, cin_re, cin_im, conj, reverse):
    sign = -1.0 if conj else 1.0
    c_re, c_im = cin_re[...], cin_im[...]

    def step(i, c):
        k = (SSM_SEG_LEN - 1 - i) if reverse else i
        rows = pl.ds(pl.multiple_of(i * SSM_SEGS, SSM_SEGS), SSM_SEGS)
        p_re = jnp.broadcast_to(pw_re[pl.ds(k, 1), :], (SSM_SEGS, SSM_GN))
        p_im = jnp.broadcast_to(pw_im[pl.ds(k, 1), :], (SSM_SEGS, SSM_GN)) * sign
        n_re, n_im = _cmul_add(p_re, p_im, c_re, c_im, s_re[rows, :], s_im[rows, :])
        s_re[rows, :] = n_re
        s_im[rows, :] = n_im
        return c
    lax.fori_loop(0, SSM_SEG_LEN, step, 0)


def _ssm_fwd(u, ab_re, ab_im, bblk_re, bblk_im, cblk_re, cblk_im, d_row, name):
    n_rows = u.shape[1]
    nchunk = n_rows // SSM_CHUNK
    last = SSM_SEG_LEN - 1

    def body(u_ref, ar_ref, ai_ref, br_ref, bi_ref, cr_ref, ci_ref, d_ref, y_ref, cin_re_ref, cin_im_ref,
             u_s, s_re, s_im, pw_re, pw_im, st_re, st_im):
        @pl.when(pl.program_id(0) == 0)
        def _():
            _ssm_powers(ar_ref, ai_ref, pw_re, pw_im)
            st_re[...] = jnp.zeros_like(st_re)
            st_im[...] = jnp.zeros_like(st_im)

        _ssm_load_rows(u_ref, u_s)
        _ssm_input_proj(u_s, br_ref, bi_ref, s_re, s_im)
        _ssm_scan(ar_ref, ai_ref, s_re, s_im, None, None, conj=False, reverse=False)
        p_re, p_im = pw_re[last:last + 1, :], pw_im[last:last + 1, :]
        c_re, c_im = st_re[...], st_im[...]
        for j in range(SSM_SEGS):
            cin_re_ref[j:j + 1, :] = c_re
            cin_im_ref[j:j + 1, :] = c_im
            row = last * SSM_SEGS + j
            c_re, c_im = _cmul_add(p_re, p_im, c_re, c_im, s_re[row:row + 1, :], s_im[row:row + 1, :])
        st_re[...] = c_re
        st_im[...] = c_im
        _ssm_fixup(pw_re, pw_im, s_re, s_im, cin_re_ref, cin_im_ref, conj=False, reverse=False)
        for t in range(SSM_TILES):
            cols = slice(t * SSM_TILE_GN, (t + 1) * SSM_TILE_GN)
            lanes = slice(t * LANES, (t + 1) * LANES)
            y = _nn(s_re[:, cols].astype(BF16), cr_ref[t]) - _nn(s_im[:, cols].astype(BF16), ci_ref[t])
            u_s[:, lanes] = y + d_ref[:, lanes] * u_s[:, lanes]
        _ssm_store_rows(u_s, y_ref)

    whole2 = lambda a: pl.BlockSpec(a.shape, lambda c: (0, 0))
    whole3 = lambda a: pl.BlockSpec(a.shape, lambda c: (0, 0, 0))
    chunk = pl.BlockSpec((SSM_TILES, SSM_CHUNK, LANES), lambda c: (0, c, 0))
    seg = pl.BlockSpec((SSM_SEGS, SSM_GN), lambda c: (c, 0))
    seg_shape = jax.ShapeDtypeStruct((nchunk * SSM_SEGS, SSM_GN), F32)
    return pl.pallas_call(
        body,
        name=name,
        grid=(nchunk,),
        in_specs=[chunk, whole2(ab_re), whole2(ab_im), whole3(bblk_re), whole3(bblk_im), whole3(cblk_re),
                  whole3(cblk_im), whole2(d_row)],
        out_specs=[chunk, seg, seg],
        out_shape=[jax.ShapeDtypeStruct((SSM_TILES, n_rows, LANES), F32), seg_shape, seg_shape],
        scratch_shapes=[pltpu.VMEM((SSM_CHUNK, SSM_WIDTH), F32), pltpu.VMEM((SSM_CHUNK, SSM_GN), F32),
                        pltpu.VMEM((SSM_CHUNK, SSM_GN), F32), pltpu.VMEM((SSM_SEG_LEN, SSM_GN), F32),
                        pltpu.VMEM((SSM_SEG_LEN, SSM_GN), F32), pltpu.VMEM((1, SSM_GN), F32),
                        pltpu.VMEM((1, SSM_GN), F32)],
        compiler_params=_params(("arbitrary",)),
    )(u, ab_re, ab_im, bblk_re, bblk_im, cblk_re, cblk_im, d_row)


def _ssm_bwd(u, dy, cin_re, cin_im, ab_re, ab_im, bblk_re, bblk_im, cblk_re, cblk_im, d_row, name):
    n_rows = u.shape[1]
    nchunk = n_rows // SSM_CHUNK

    def body(u_ref, dy_ref, cin_re_ref, cin_im_ref, ar_ref, ai_ref, br_ref, bi_ref, cr_ref, ci_ref, d_ref,
             du_ref, dar_ref, dai_ref, dbr_ref, dbi_ref, dcr_ref, dci_ref, dd_ref,
             u_s, dy_s, s_re, s_im, q_re, q_im, pw_re, pw_im, qst_re, qst_im, qin_re, qin_im):
        @pl.when(pl.program_id(0) == 0)
        def _():
            _ssm_powers(ar_ref, ai_ref, pw_re, pw_im)
            qst_re[...] = jnp.zeros_like(qst_re)
            qst_im[...] = jnp.zeros_like(qst_im)
            for ref in (dar_ref, dai_ref, dbr_ref, dbi_ref, dcr_ref, dci_ref, dd_ref):
                ref[...] = jnp.zeros_like(ref)

        _ssm_load_rows(u_ref, u_s)
        _ssm_load_rows(dy_ref, dy_s)
        _ssm_input_proj(u_s, br_ref, bi_ref, s_re, s_im)
        _ssm_scan(ar_ref, ai_ref, s_re, s_im, cin_re_ref, cin_im_ref, conj=False, reverse=False)
        for t in range(SSM_TILES):
            cols = slice(t * SSM_TILE_GN, (t + 1) * SSM_TILE_GN)
            dyb = dy_s[:, t * LANES:(t + 1) * LANES].astype(BF16)
            q_re[:, cols] = _nt(dyb, cr_ref[t])
            q_im[:, cols] = -_nt(dyb, ci_ref[t])
            dcr_ref[t] += _tn(s_re[:, cols].astype(BF16), dyb)
            dci_ref[t] -= _tn(s_im[:, cols].astype(BF16), dyb)
        _ssm_scan(ar_ref, ai_ref, q_re, q_im, None, None, conj=True, reverse=True)
        last = SSM_SEG_LEN - 1
        p_re, p_im = pw_re[last:last + 1, :], -pw_im[last:last + 1, :]
        c_re, c_im = qst_re[...], qst_im[...]
        for j in reversed(range(SSM_SEGS)):
            qin_re[j:j + 1, :] = c_re
            qin_im[j:j + 1, :] = c_im
            c_re, c_im = _cmul_add(p_re, p_im, c_re, c_im, q_re[j:j + 1, :], q_im[j:j + 1, :])
        qst_re[...] = c_re
        qst_im[...] = c_im
        _ssm_fixup(pw_re, pw_im, q_re, q_im, qin_re, qin_im, conj=True, reverse=True)
        for t in range(SSM_TILES):
            cols = slice(t * SSM_TILE_GN, (t + 1) * SSM_TILE_GN)

            def step(i, acc, cols=cols):
                rows = pl.ds(pl.multiple_of(i * SSM_SEGS, SSM_SEGS), SSM_SEGS)
                prev = pl.ds(pl.multiple_of((i - 1) * SSM_SEGS, SSM_SEGS), SSM_SEGS)
                qr, qi = q_re[rows, cols], q_im[rows, cols]
                sr, si = s_re[prev, cols], s_im[prev, cols]
                return acc[0] + qr * sr + qi * si, acc[1] + qi * sr - qr * si

            qr, qi = q_re[0:SSM_SEGS, cols], q_im[0:SSM_SEGS, cols]
            sr, si = cin_re_ref[:, cols], cin_im_ref[:, cols]
            acc = lax.fori_loop(1, SSM_SEG_LEN, step, (qr * sr + qi * si, qi * sr - qr * si))
            dar_ref[:, cols] += jnp.sum(acc[0], axis=0, keepdims=True)
            dai_ref[:, cols] += jnp.sum(acc[1], axis=0, keepdims=True)
        for t in range(SSM_TILES):
            cols = slice(t * SSM_TILE_GN, (t + 1) * SSM_TILE_GN)
            lanes = slice(t * LANES, (t + 1) * LANES)
            qrb, qib = q_re[:, cols].astype(BF16), q_im[:, cols].astype(BF16)
            u_t, dy_t = u_s[:, lanes], dy_s[:, lanes]
            ub = u_t.astype(BF16)
            dbr_ref[t] += _tn(ub, qrb)
            dbi_ref[t] += _tn(ub, qib)
            dd_ref[:, lanes] += jnp.sum(dy_t * u_t, axis=0, keepdims=True)
            u_s[:, lanes] = _nt(qrb, br_ref[t]) + _nt(qib, bi_ref[t]) + dy_t * d_ref[:, lanes]
        _ssm_store_rows(u_s, du_ref)

    whole2 = lambda a: pl.BlockSpec(a.shape, lambda c: (0, 0))
    whole3 = lambda a: pl.BlockSpec(a.shape, lambda c: (0, 0, 0))
    chunk = pl.BlockSpec((SSM_TILES, SSM_CHUNK, LANES), lambda c: (0, nchunk - 1 - c, 0))
    seg = pl.BlockSpec((SSM_SEGS, SSM_GN), lambda c: (nchunk - 1 - c, 0))
    gn_row = jax.ShapeDtypeStruct((1, SSM_GN), F32)
    b_shape = jax.ShapeDtypeStruct((SSM_TILES, LANES, SSM_TILE_GN), F32)
    c_shape = jax.ShapeDtypeStruct((SSM_TILES, SSM_TILE_GN, LANES), F32)
    d_shape = jax.ShapeDtypeStruct((1, SSM_WIDTH), F32)
    big = pltpu.VMEM((SSM_CHUNK, SSM_GN), F32)
    return pl.pallas_call(
        body,
        name=name,
        grid=(nchunk,),
        in_specs=[chunk, chunk, seg, seg, whole2(ab_re), whole2(ab_im), whole3(bblk_re), whole3(bblk_im),
                  whole3(cblk_re), whole3(cblk_im), whole2(d_row)],
        out_specs=[chunk, whole2(ab_re), whole2(ab_im), whole3(bblk_re), whole3(bblk_im), whole3(cblk_re),
                   whole3(cblk_im), whole2(d_row)],
        out_shape=[jax.ShapeDtypeStruct((SSM_TILES, n_rows, LANES), F32), gn_row, gn_row, b_shape, b_shape, c_shape,
                   c_shape, d_shape],
        scratch_shapes=[pltpu.VMEM((SSM_CHUNK, SSM_WIDTH), F32), pltpu.VMEM((SSM_CHUNK, SSM_WIDTH), F32),
                        big, big, big, big,
                        pltpu.VMEM((SSM_SEG_LEN, SSM_GN), F32), pltpu.VMEM((SSM_SEG_LEN, SSM_GN), F32),
                        pltpu.VMEM((1, SSM_GN), F32), pltpu.VMEM((1, SSM_GN), F32),
                        pltpu.VMEM((SSM_SEGS, SSM_GN), F32), pltpu.VMEM((SSM_SEGS, SSM_GN), F32)],
        compiler_params=_params(("arbitrary",)),
    )(u, dy, cin_re, cin_im, ab_re, ab_im, bblk_re, bblk_im, cblk_re, cblk_im, d_row)


def _mlp_fwd(h2, w1, w2, tm, tf, name):
    n_rows, dm = h2.shape
    dff = w1.shape[1]

    def body(h_ref, w1_ref, w2_ref, a_ref, y_ref):
        a = _nn(h_ref[...], w1_ref[...])
        a_ref[...] = a.astype(BF16)
        r = jnp.maximum(a, 0.0)
        part = _nn((r * r).astype(BF16), w2_ref[...])
        j = pl.program_id(1)

        @pl.when(j == 0)
        def _():
            y_ref[...] = part

        @pl.when(j > 0)
        def _():
            y_ref[...] += part

    return pl.pallas_call(
        body,
        name=name,
        grid=(n_rows // tm, dff // tf),
        in_specs=[pl.BlockSpec((tm, dm), lambda i, j: (i, 0)), pl.BlockSpec((dm, tf), lambda i, j: (0, j)),
                  pl.BlockSpec((tf, dm), lambda i, j: (j, 0))],
        out_specs=[pl.BlockSpec((tm, tf), lambda i, j: (i, j)), pl.BlockSpec((tm, dm), lambda i, j: (i, 0))],
        out_shape=[jax.ShapeDtypeStruct((n_rows, dff), BF16), jax.ShapeDtypeStruct((n_rows, dm), F32)],
        compiler_params=_params(("parallel", "arbitrary")),
    )(h2, w1, w2)


def _mlp_bwd(dy, h2, a, w2, tm, tf, name):
    n_rows, dm = h2.shape
    dff = a.shape[1]
    per_chip = dff // N_CHIPS // tf

    def body(dy_ref, h_ref, a_ref, w2_ref, da_ref, dw2_ref, dw1_ref):
        dyb = dy_ref[...]
        r = jnp.maximum(a_ref[...].astype(F32), 0.0)
        da = (_nt(dyb, w2_ref[...]) * (2.0 * r)).astype(BF16)
        da_ref[...] = da
        p2 = _tn((r * r).astype(BF16), dyb)
        p1 = _tn(h_ref[...], da)
        i = pl.program_id(1)

        @pl.when(i == 0)
        def _():
            dw2_ref[...] = p2
            dw1_ref[...] = p1

        @pl.when(i > 0)
        def _():
            dw2_ref[...] += p2
            dw1_ref[...] += p1

    return pl.pallas_call(
        body,
        name=name,
        grid=(dff // tf, n_rows // tm),
        in_specs=[pl.BlockSpec((tm, dm), lambda j, i: (i, 0)), pl.BlockSpec((tm, dm), lambda j, i: (i, 0)),
                  pl.BlockSpec((tm, tf), lambda j, i: (i, j)), pl.BlockSpec((tf, dm), lambda j, i: (j, 0))],
        out_specs=[pl.BlockSpec((tm, tf), lambda j, i: (i, j)), pl.BlockSpec((tf, dm), lambda j, i: (j, 0)),
                   pl.BlockSpec((None, dm, tf), lambda j, i: (j // per_chip, 0, j % per_chip))],
        out_shape=[jax.ShapeDtypeStruct((n_rows, dff), BF16), jax.ShapeDtypeStruct((dff, dm), F32),
                   jax.ShapeDtypeStruct((N_CHIPS, dm, dff // N_CHIPS), F32)],
        compiler_params=_params(("parallel", "arbitrary")),
    )(dy, h2, a, w2)


def _local_step(x, pos_col, mod, target, wts, small, hooks=None):
    n_rows = x.shape[0]
    sh1, sc1, gt1, sh2, sc2, gt2 = (mod[:, i * D_MODEL:(i + 1) * D_MODEL] for i in range(N_MOD))
    tm = 256
    d_acc = (1, D_MODEL)

    (h1,) = _rowwise(lambda r, c: [_mod_norm(r[0], *c)], [x], [small["g_pre_mix"], sc1, sh1],
                     [(D_MODEL, BF16)], [], tm, "pre_mix_fwd")
    proj = _matmul(h1, wts["w_in"], "nn", F32, 512, 1408, 2048, "in_proj")

    freqs = ROPE_THETA ** (-jnp.arange(0, ROT_DIM, 2, dtype=F32) / ROT_DIM)
    freq_lane = jnp.tile(freqs, LANES // (ROT_DIM // 2))[None, :]
    tabs = _rope_tables(pos_col, freq_lane, "rope_tables")
    att = [_att_fwd(proj, tabs, gi, f"att_fwd_{gi}") for gi in range(3)]

    expand = jnp.asarray(_expand_np())
    b_re2, b_im2 = small["ssm_b_re"].reshape(SSM_G, -1), small["ssm_b_im"].reshape(SSM_G, -1)
    log_dt = small["ssm_log_dt"].reshape(SSM_G, 1)
    prep_in = (small["ssm_a_re"], small["ssm_a_im"], log_dt, b_re2, b_im2, expand)
    ab_re, ab_im, bb_re, bb_im = _ssm_prep(*prep_in, "ssm_prep")
    ab_re_row, ab_im_row = ab_re.reshape(1, SSM_GN), ab_im.reshape(1, SSM_GN)
    bblk = [_block_diag_in(t.reshape(SSM_G, SSM_N, SSM_P)).astype(BF16) for t in (bb_re, bb_im)]
    cblk = [_block_diag_out(small[k]).astype(BF16) for k in ("ssm_c_re", "ssm_c_im")]
    d_row = small["ssm_d"].reshape(1, SSM_WIDTH)
    u = _lane_tiled(proj[:, IN_WIDTH - SSM_WIDTH:])
    y_ssm, cin_re, cin_im = _ssm_fwd(u, ab_re_row, ab_im_row, *bblk, *cblk, d_row, "ssm_fwd")
    y_ssm = _lane_untiled(y_ssm)

    def mixers_out(r, c):
        w_glu, b_glu, g_att, g_ssm = c
        att_n = _att_mix(*r[:6], g_att)
        y2 = _gelu(r[6])
        z = _nn(y2.astype(BF16), w_glu) + b_glu
        return [jnp.concatenate([att_n.astype(BF16), _glu_out(y2, z, g_ssm).astype(BF16)], axis=1)]

    att_rows = [a[0] for a in att] + [a[1] for a in att]
    mix_consts = [wts["w_glu"], small["b_glu"], small["g_attn_out"], small["g_ssm_out"]]
    (cat,) = _rowwise(mixers_out, att_rows + [y_ssm], mix_consts, [(OUT_IN_WIDTH, BF16)], [], tm, "mixers_out_fwd")
    mix = _matmul(cat, wts["w_out"], "nn", F32, 512, 1024, 1280, "out_proj")

    post_consts = [small["g_post_mix"], gt1, small["g_pre_mlp"], sc2, sh2]
    x1, h2 = _rowwise(lambda r, c: list(_post_mix(r[0], r[1], *c)), [x, mix], post_consts,
                      [(D_MODEL, F32), (D_MODEL, BF16)], [], tm, "post_mix_fwd")
    w_mlp_in, w_mlp_out = (wts["w_mlp_in"], wts["w_mlp_out"]) if hooks is None else hooks.mlp_weights(h2)
    a_mlp, y_mlp = _mlp_fwd(h2, w_mlp_in, w_mlp_out, 512, 512, "mlp_fwd")

    def loss_head(r, c):
        x1_v, y_v, t_v = r
        g, gt = c
        fn = lambda y_, g_, gt_: gt_ * _rms(y_, g_)
        out, vjp = jax.vjp(fn, y_v, g, gt)
        err = x1_v + out - t_v
        dx2 = err * (1.0 / D_MODEL)
        dy, dg, dgt = vjp(dx2)
        loss = 0.5 * jnp.sum(jnp.sum(err * err, axis=1, keepdims=True), axis=0, keepdims=True) * (1.0 / D_MODEL)
        return [dx2, dy, loss, dg, dgt]

    dx2, dy_mlp, loss, dg_post_mlp, dgt2 = _rowwise(
        loss_head, [x1, y_mlp, target], [small["g_post_mlp"], gt2],
        [(D_MODEL, F32), (D_MODEL, BF16)], [(1, 1), d_acc, d_acc], tm, "loss_head")

    da_mlp, dw_mlp_out, dw_mlp_in = _mlp_bwd(dy_mlp, h2, a_mlp, w_mlp_out, 512, 512, "mlp_bwd")
    dw_mlp_out = dw_mlp_out.reshape(dw_mlp_in.shape)
    sent = None if hooks is None else hooks.mlp_grads_to_sibling(dw_mlp_in, dw_mlp_out)
    dh2 = _matmul(da_mlp, w_mlp_in, "nt", F32, 512, 1024, 2048, "mlp_in_bwd", after=sent)
    sent = None if hooks is None else hooks.mlp_grads_to_chips(dh2)

    def post_mix_bwd(r, c):
        x_v, mix_v, dx1_v, dh2_v = r
        _, vjp = jax.vjp(_post_mix, x_v, mix_v, *c)
        return list(vjp((dx1_v, dh2_v)))

    post_consts_bwd = post_consts if sent is None else [_tie(post_consts[0], sent)] + post_consts[1:]
    dx_a, dmix, dg_post_mix, dgt1, dg_pre_mlp, dsc2, dsh2 = _rowwise(
        post_mix_bwd, [x, mix, dx2, dh2], post_consts_bwd, [(D_MODEL, F32), (D_MODEL, BF16)], [d_acc] * 5, tm,
        "post_mix_bwd")

    dcat = _matmul(dmix, wts["w_out"], "nt", F32, 512, 1280, 2048, "out_proj_bwd")
    dw_out = _matmul(cat, dmix, "tn", F32, 1280, 1024, 512, "out_proj_wgrad")

    def mixers_out_bwd(r, c):
        w_glu, b_glu, g_att, g_ssm = c
        dcat_v = r[7]
        _, vjp_att = jax.vjp(_att_mix, *r[:6], g_att)
        *d_ol, dg_att = vjp_att(dcat_v[:, :KV_WIDTH])
        y2, vjp_gelu = jax.vjp(_gelu, r[6])
        y2b = y2.astype(BF16)
        z = _nn(y2b, w_glu) + b_glu
        _, vjp_glu = jax.vjp(_glu_out, y2, z, g_ssm)
        dy2, dz, dg_ssm = vjp_glu(dcat_v[:, KV_WIDTH:])
        dzb = dz.astype(BF16)
        (dy,) = vjp_gelu(dy2 + _nt(dzb, w_glu))
        return d_ol + [dy, dg_att, _tn(y2b, dzb), jnp.sum(dz, axis=0, keepdims=True), dg_ssm]

    *d_att, dy_ssm, dg_attn_out, dw_glu, db_glu, dg_ssm_out = _rowwise(
        mixers_out_bwd, att_rows + [y_ssm, dcat], mix_consts,
        [(KV_WIDTH, F32)] * 6 + [(SSM_WIDTH, F32)],
        [(1, KV_WIDTH), (SSM_WIDTH, SSM_WIDTH), (1, SSM_WIDTH), (1, SSM_WIDTH)], tm, "mixers_out_bwd")

    du, dab_re, dab_im, dbblk_re, dbblk_im, dcblk_re, dcblk_im, dd_row = _ssm_bwd(
        u, _lane_tiled(dy_ssm), cin_re, cin_im, ab_re_row, ab_im_row, *bblk, *cblk, d_row, "ssm_bwd")
    du = _lane_untiled(du)
    prep_cts = (dab_re.reshape(SSM_G, SSM_N), dab_im.reshape(SSM_G, SSM_N),
                _block_diag_in_grad(dbblk_re).reshape(SSM_G, -1), _block_diag_in_grad(dbblk_im).reshape(SSM_G, -1))
    da_re, da_im, dlog_dt, db_re, db_im = _ssm_prep_bwd(*prep_in, prep_cts, "ssm_prep_bwd")

    dqkv = [_att_bwd(proj, tabs, att[gi][0], att[gi][1], d_att[gi], d_att[3 + gi], gi, f"att_bwd_{gi}")
            for gi in range(3)]

    def gather_dproj(r, c):
        dq = [r[0], r[3], r[6]]
        dk = r[1] + r[4] + r[7]
        dv = r[2] + r[5] + r[8]
        return [jnp.concatenate([t.astype(BF16) for t in dq + [dk, dv, r[9]]], axis=1)]

    (dproj,) = _rowwise(gather_dproj, [t for g in dqkv for t in g] + [du], [], [(IN_WIDTH, BF16)], [], tm,
                        "gather_dproj")
    dh1 = _matmul(dproj, wts["w_in"], "nt", F32, 512, 1024, 2816, "in_proj_bwd")
    dw_in = _matmul(h1, dproj, "tn", F32, 1024, 1408, 512, "in_proj_wgrad")

    def pre_mix_bwd(r, c):
        x_v, dh1_v, dxa_v = r
        _, vjp = jax.vjp(_mod_norm, x_v, *c)
        dx, dg, dsc, dsh = vjp(dh1_v)
        return [dx + dxa_v, dg, dsc, dsh]

    grad_x, dg_pre_mix, dsc1, dsh1 = _rowwise(
        pre_mix_bwd, [x, dh1, dx_a], [small["g_pre_mix"], sc1, sh1], [(D_MODEL, F32)], [d_acc] * 3, tm, "pre_mix_bwd")

    dmod = jnp.concatenate([dsh1, dsc1, dgt1, dsh2, dsc2, dgt2], axis=1)
    big = dict(w_in=dw_in, w_out=dw_out, w_mlp_in=dw_mlp_in, w_mlp_out=dw_mlp_out, w_glu=dw_glu)
    small_g = dict(
        g_pre_mix=dg_pre_mix, g_post_mix=dg_post_mix, ssm_a_re=da_re, ssm_a_im=da_im,
        ssm_log_dt=dlog_dt.reshape(1, SSM_G), ssm_b_re=db_re.reshape(SSM_G, SSM_N, SSM_P),
        ssm_b_im=db_im.reshape(SSM_G, SSM_N, SSM_P), ssm_c_re=_block_diag_out_grad(dcblk_re),
        ssm_c_im=_block_diag_out_grad(dcblk_im), ssm_d=dd_row.reshape(SSM_G, SSM_P), b_glu=db_glu,
        g_attn_out=dg_attn_out, g_ssm_out=dg_ssm_out, g_pre_mlp=dg_pre_mlp, g_post_mlp=dg_post_mlp)
    return loss, grad_x, dmod, big, small_g


MESH_ID = pl.DeviceIdType.MESH
N_DEV = 8
N_CHIPS = 4
HBM_SPEC = pl.BlockSpec(memory_space=pltpu.HBM)


def _place():
    x, y, c = lax.axis_index("x"), lax.axis_index("y"), lax.axis_index("c")
    other_chips = [(1 - x, y), (x, 1 - y), (1 - x, 1 - y)]
    return x, y, c, other_chips


def _half_rows(index, half):
    return pl.ds(pl.multiple_of(index * half, ROW_PAD), half)


def _remote(src, dst, send_sem, recv_sem, dev):
    return pltpu.make_async_remote_copy(src_ref=src, dst_ref=dst, send_sem=send_sem, recv_sem=recv_sem,
                                        device_id=dev, device_id_type=MESH_ID)


def _all_gather8(block, name):
    m_per, n = block.shape

    def body(x_ref, out_ref, send_sems, recv_sems, local_sem):
        x, y, c, chips = _place()
        me, sibling = (x, y, c), (x, y, 1 - c)

        def rows(px, py, pc):
            return out_ref.at[pl.ds((4 * px + 2 * py + pc) * m_per, m_per), :]

        def copy(k, blk, to, src=None):
            return _remote(rows(*blk) if src is None else src, rows(*blk), send_sems.at[k], recv_sems.at[k], to)

        mine = pltpu.make_async_copy(x_ref, rows(*me), local_sem)
        mine.start()
        first = [copy(0, me, sibling, src=x_ref)]
        first += [copy(1 + j, me, (*chip, c), src=x_ref) for j, chip in enumerate(chips)]
        for cp in first:
            cp.start()
        passed = [copy(4 + j, (*chip, c), sibling) for j, chip in enumerate(chips)]
        for j, chip in enumerate(chips):
            copy(1 + j, (*chip, c), me).wait_recv()
            passed[j].start()
        copy(0, sibling, me).wait_recv()
        for j, chip in enumerate(chips):
            copy(4 + j, (*chip, 1 - c), me).wait_recv()
        for cp in first + passed:
            cp.wait_send()
        mine.wait()

    return pl.pallas_call(
        body,
        name=name,
        out_shape=jax.ShapeDtypeStruct((N_DEV * m_per, n), block.dtype),
        in_specs=[pl.BlockSpec(memory_space=pltpu.VMEM)],
        out_specs=pl.BlockSpec(memory_space=pltpu.VMEM),
        scratch_shapes=[pltpu.SemaphoreType.DMA((7,)), pltpu.SemaphoreType.DMA((7,)), pltpu.SemaphoreType.DMA],
        compiler_params=_params(),
    )(block)


def _weight_gather(shards, name):
    n = len(shards)
    shapes = [s.shape for s in shards]

    def body(*refs):
        ins, outs = refs[:n], refs[n:2 * n]
        send, recv, fsend, frecv = refs[2 * n:]
        x, y, c, chips = _place()
        k_me = 2 * x + y
        sibling = (x, y, 1 - c)
        pending = []
        for a in range(n):
            half = shapes[a][0] // 2
            mine = _half_rows(c, half)
            for j, chip in enumerate(chips):
                cp = _remote(ins[a].at[mine, :], outs[a].at[k_me, mine, :], send.at[a, j], recv.at[a, j], (*chip, c))
                cp.start()
                pending.append(cp.wait_send)
        for a in range(n):
            half = shapes[a][0] // 2
            for j, (px, py) in enumerate(chips):
                piece = outs[a].at[2 * px + py, _half_rows(c, half), :]
                _remote(piece, piece, send.at[a, j], recv.at[a, j], (px, py, c)).wait_recv()
                fw = _remote(piece, piece, fsend.at[a, j], frecv.at[a, j], sibling)
                fw.start()
                pending.append(fw.wait_send)
        for a in range(n):
            half = shapes[a][0] // 2
            for j, (px, py) in enumerate(chips):
                piece = outs[a].at[2 * px + py, _half_rows(1 - c, half), :]
                _remote(piece, piece, fsend.at[a, j], frecv.at[a, j], sibling).wait_recv()
        for wait in pending:
            wait()

    sems = pltpu.SemaphoreType.DMA((n, 3))
    return pl.pallas_call(
        body,
        name=name,
        out_shape=[jax.ShapeDtypeStruct((N_CHIPS,) + s, BF16) for s in shapes],
        in_specs=[HBM_SPEC] * n,
        out_specs=[HBM_SPEC] * n,
        scratch_shapes=[sems, sems, sems, sems],
        compiler_params=_params(),
    )(*shards)


def _sibling_halves(stacks, name):
    n = len(stacks)
    shapes = [s.shape for s in stacks]

    def body(*refs):
        ins, outs = refs[:n], refs[n:2 * n]
        send, recv = refs[2 * n:]
        x, y, c, _ = _place()
        copies = []
        for a in range(n):
            half = shapes[a][1] // 2
            cp = _remote(ins[a].at[:, _half_rows(1 - c, half), :], outs[a], send.at[a], recv.at[a], (x, y, 1 - c))
            cp.start()
            copies.append(cp)
        for cp in copies:
            cp.wait()

    return pl.pallas_call(
        body,
        name=name,
        out_shape=[jax.ShapeDtypeStruct((N_CHIPS, s[1] // 2, s[2]), F32) for s in shapes],
        in_specs=[HBM_SPEC] * n,
        out_specs=[HBM_SPEC] * n,
        scratch_shapes=[pltpu.SemaphoreType.DMA((n,)), pltpu.SemaphoreType.DMA((n,))],
        compiler_params=_params(),
    )(*stacks)


def _chip_exchange(parts, name):
    n = len(parts)
    shapes = [p.shape for p in parts]

    def body(*refs):
        ins, outs = refs[:n], refs[n:2 * n]
        send, recv = refs[2 * n:]
        x, y, c, chips = _place()
        copies = []
        for a in range(n):
            for j, (px, py) in enumerate(chips):
                cp = _remote(ins[a].at[2 * px + py], outs[a].at[j], send.at[a, j], recv.at[a, j], (px, py, c))
                cp.start()
                copies.append(cp)
        for cp in copies:
            cp.wait()

    return pl.pallas_call(
        body,
        name=name,
        out_shape=[jax.ShapeDtypeStruct((3,) + s[1:], BF16) for s in shapes],
        in_specs=[HBM_SPEC] * n,
        out_specs=[HBM_SPEC] * n,
        scratch_shapes=[pltpu.SemaphoreType.DMA((n, 3)), pltpu.SemaphoreType.DMA((n, 3))],
        compiler_params=_params(),
    )(*parts)


def _sibling_swap(halves, name):
    n = len(halves)
    shapes = [h.shape for h in halves]

    def body(*refs):
        ins, outs = refs[:n], refs[n:2 * n]
        send, recv = refs[2 * n:]
        x, y, c, _ = _place()
        pending = []
        for a in range(n):
            half = shapes[a][0]
            mine = outs[a].at[_half_rows(c, half), :]
            cp = _remote(ins[a], mine, send.at[a], recv.at[a], (x, y, 1 - c))
            cp.start()
            pending.append(cp.wait_send)
        for a in range(n):
            half = shapes[a][0]
            theirs = outs[a].at[_half_rows(1 - c, half), :]
            _remote(theirs, theirs, send.at[a], recv.at[a], (x, y, 1 - c)).wait_recv()
        for wait in pending:
            wait()

    return pl.pallas_call(
        body,
        name=name,
        out_shape=[jax.ShapeDtypeStruct((2 * s[0], s[1]), F32) for s in shapes],
        in_specs=[HBM_SPEC] * n,
        out_specs=[HBM_SPEC] * n,
        scratch_shapes=[pltpu.SemaphoreType.DMA((n,)), pltpu.SemaphoreType.DMA((n,))],
        compiler_params=_params(),
    )(*halves)


SEM_SPEC = pl.BlockSpec(memory_space=pltpu.SEMAPHORE)
ANY_SPEC = pl.BlockSpec(memory_space=pl.ANY)
DATAFLOW = pltpu.SideEffectType.DATAFLOW_SIDE_EFFECTING


def _split_copy_start(srcs, lands, plan, n_sems, name):
    bufs = list(srcs) + list(lands)
    ns, nb = len(srcs), len(bufs)

    def body(*refs):
        for outgoing, _ in plan(refs[:ns], refs[ns:nb], refs[nb], refs[nb + 1]):
            outgoing.start()
        refs[-1][...] = jnp.zeros_like(refs[-1])

    sems = pltpu.SemaphoreType.DMA((n_sems,))
    return pl.pallas_call(
        body,
        name=name,
        out_shape=(sems, sems, *[pltpu.HBM(b.shape, b.dtype) for b in bufs], jax.ShapeDtypeStruct((8, LANES), F32)),
        in_specs=[HBM_SPEC] * nb,
        out_specs=(SEM_SPEC, SEM_SPEC, *[HBM_SPEC] * nb, pl.BlockSpec(memory_space=pltpu.VMEM)),
        input_output_aliases={i: 2 + i for i in range(nb)},
        compiler_params=pltpu.CompilerParams(has_side_effects=DATAFLOW),
    )(*[pltpu.with_memory_space_constraint(b, pltpu.HBM) for b in bufs])


def _split_copy_wait(started, plan, after, name):
    send, recv, *bufs = started[:-1]
    nb = len(bufs)
    ns = nb // 2

    def body(*refs):
        for outgoing, incoming in plan(refs[:ns], refs[ns:nb], refs[nb], refs[nb + 1]):
            outgoing.wait_send()
            incoming.wait_recv()

    return pl.pallas_call(
        body,
        name=name,
        out_shape=tuple(pltpu.HBM(b.shape, b.dtype) for b in bufs),
        in_specs=[HBM_SPEC] * nb + [SEM_SPEC, SEM_SPEC, ANY_SPEC],
        out_specs=tuple([HBM_SPEC] * nb),
        input_output_aliases={i: i for i in range(nb)},
        compiler_params=pltpu.CompilerParams(has_side_effects=DATAFLOW),
    )(*bufs, send, recv, after)


def _weight_plan(shapes):
    def plan(srcs, lands, send, recv):
        x, y, c, chips = _place()
        copies = []
        for a in range(len(shapes)):
            mine = _half_rows(c, shapes[a][0] // 2)
            for j, (px, py) in enumerate(chips):
                s = 3 * a + j
                arrival = lands[a].at[2 * px + py, mine, :]
                copies.append((_remote(srcs[a].at[mine, :], lands[a].at[2 * x + y, mine, :], send.at[s], recv.at[s], (px, py, c)),
                               _remote(arrival, arrival, send.at[s], recv.at[s], (px, py, c))))
        return copies
    return plan


def _halves_plan(shapes):
    def plan(srcs, lands, send, recv):
        x, y, c, _ = _place()
        copies = []
        for a in range(len(shapes)):
            theirs = srcs[a].at[:, _half_rows(1 - c, shapes[a][1] // 2), :]
            copies.append((_remote(theirs, lands[a], send.at[a], recv.at[a], (x, y, 1 - c)),
                           _remote(lands[a], lands[a], send.at[a], recv.at[a], (x, y, 1 - c))))
        return copies
    return plan


def _exchange_plan(n):
    def plan(srcs, lands, send, recv):
        x, y, c, chips = _place()
        copies = []
        for a in range(n):
            for j, (px, py) in enumerate(chips):
                s = 3 * a + j
                copies.append((_remote(srcs[a].at[2 * px + py], lands[a].at[j], send.at[s], recv.at[s], (px, py, c)),
                               _remote(lands[a].at[j], lands[a].at[j], send.at[s], recv.at[s], (px, py, c))))
        return copies
    return plan


def _forward_to_sibling(stacks, name):
    n = len(stacks)
    shapes = [s.shape for s in stacks]

    def body(*refs):
        ins, outs = refs[:n], refs[n:2 * n]
        send, recv = refs[2 * n:]
        x, y, c, chips = _place()
        sibling = (x, y, 1 - c)
        copies = []
        for a in range(n):
            half = shapes[a][1] // 2
            for j, (px, py) in enumerate(chips):
                rows = _half_rows(c, half)
                cp = _remote(ins[a].at[2 * px + py, rows, :], outs[a].at[2 * px + py, rows, :], send.at[a, j],
                             recv.at[a, j], sibling)
                cp.start()
                copies.append(cp)
        for a in range(n):
            half = shapes[a][1] // 2
            for j, (px, py) in enumerate(chips):
                theirs = outs[a].at[2 * px + py, _half_rows(1 - c, half), :]
                _remote(theirs, theirs, send.at[a, j], recv.at[a, j], sibling).wait_recv()
        for cp in copies:
            cp.wait_send()

    return pl.pallas_call(
        body,
        name=name,
        out_shape=[jax.ShapeDtypeStruct(s, BF16) for s in shapes],
        in_specs=[HBM_SPEC] * n,
        out_specs=[HBM_SPEC] * n,
        input_output_aliases={a: a for a in range(n)},
        scratch_shapes=[pltpu.SemaphoreType.DMA((n, 3)), pltpu.SemaphoreType.DMA((n, 3))],
        compiler_params=_params(),
    )(*stacks)


def _tie(x, token):
    return x + token[0:1, 0:1].astype(x.dtype)


ROW_PAD = 16


def _silu(x):
    return x * _sigmoid(x)


def _ada_fwd(c_all, w_ada, b_ada, name):
    dm, cols = w_ada.shape
    tn = 512

    def body(c_ref, w_ref, b_ref, o_ref):
        o_ref[...] = _nn(_silu(c_ref[...]).astype(BF16), w_ref[...].astype(BF16)) + b_ref[...]

    return pl.pallas_call(
        body,
        name=name,
        grid=(cols // tn,),
        in_specs=[pl.BlockSpec((ROW_PAD, dm), lambda j: (0, 0)), pl.BlockSpec((dm, tn), lambda j: (0, j)),
                  pl.BlockSpec((1, tn), lambda j: (0, j))],
        out_specs=pl.BlockSpec((ROW_PAD, tn), lambda j: (0, j)),
        out_shape=jax.ShapeDtypeStruct((ROW_PAD, cols), F32),
        compiler_params=_params(("parallel",)),
    )(c_all, w_ada, b_ada)


def _adamw(w, g, m, v):
    m = ADAM_B1 * m + (1.0 - ADAM_B1) * g
    v = ADAM_B2 * v + (1.0 - ADAM_B2) * (g * g)
    m_hat = m / (1.0 - ADAM_B1 ** ADAM_STEP)
    v_hat = v / (1.0 - ADAM_B2 ** ADAM_STEP)
    delta = -ADAM_LR * (m_hat / (jnp.sqrt(v_hat) + ADAM_EPS) + ADAM_WD * w)
    return delta, m, v


def _ada_bwd_adamw(c_all, dmod_cols, w, m, v, name):
    dm, cols = w.shape
    tm, tn = 512, 512

    def body(c_ref, d_ref, w_ref, m_ref, v_ref, g_ref, dl_ref, nm_ref, nv_ref):
        g = _tn(_silu(c_ref[...]).astype(BF16), d_ref[...].astype(BF16))
        g_ref[...] = g
        dl_ref[...], nm_ref[...], nv_ref[...] = _adamw(w_ref[...], g, m_ref[...], v_ref[...])

    tile = pl.BlockSpec((tm, tn), lambda i, j: (i, j))
    shape = jax.ShapeDtypeStruct((dm, cols), F32)
    return pl.pallas_call(
        body,
        name=name,
        grid=(dm // tm, cols // tn),
        in_specs=[pl.BlockSpec((ROW_PAD, tm), lambda i, j: (0, i)), pl.BlockSpec((ROW_PAD, tn), lambda i, j: (0, j)),
                  tile, tile, tile],
        out_specs=[tile] * 4,
        out_shape=[shape] * 4,
        compiler_params=_params(("parallel", "parallel")),
    )(c_all, dmod_cols, w, m, v)


def _sum_blocks(parts, nblk, name):
    rows, cols = parts.shape[0] // nblk, parts.shape[1]

    def body(p_ref, o_ref):
        tot = p_ref[0:rows, :]
        for b in range(1, nblk):
            tot = tot + p_ref[b * rows:(b + 1) * rows, :]
        o_ref[...] = tot

    return pl.pallas_call(body, name=name, out_shape=jax.ShapeDtypeStruct((rows, cols), F32), compiler_params=_params())(parts)


def _adamw_rows(w, g, m, v, tm, name):
    return _rowwise(lambda r, c: list(_adamw(*r)), [w, g, m, v], [], [(w.shape[1], F32)] * 3, [], tm, name)


BIG = ("w_in", "w_out", "w_mlp_in", "w_mlp_out", "w_glu")
COL_SHARDED = ("w_in", "w_out", "w_mlp_in")
SMALL = ("b_ada", "g_pre_mix", "g_post_mix", "ssm_a_re", "ssm_a_im", "ssm_log_dt", "ssm_b_re", "ssm_b_im",
         "ssm_c_re", "ssm_c_im", "ssm_d", "b_glu", "g_attn_out", "g_ssm_out", "g_pre_mlp", "g_post_mlp")
WEIGHTS = ("w_ada", "b_ada", "g_pre_mix", "g_post_mix", "w_in", "ssm_a_re", "ssm_a_im", "ssm_log_dt", "ssm_b_re",
           "ssm_b_im", "ssm_c_re", "ssm_c_im", "ssm_d", "w_glu", "b_glu", "g_attn_out", "g_ssm_out", "w_out",
           "g_pre_mlp", "g_post_mlp", "w_mlp_in", "w_mlp_out")
FLAT_COLS = 1024
FLAT_ROWS = 256
ROW_TILE = {"w_in": 256, "w_out": 128, "w_mlp_in": 256, "w_mlp_out": 256, "w_glu": 112}


def _flatten_small(tree):
    flat = jnp.concatenate([tree[k].reshape(-1) for k in SMALL])
    return jnp.pad(flat, (0, FLAT_ROWS * FLAT_COLS - flat.shape[0])).reshape(FLAT_ROWS, FLAT_COLS)


def _unflatten_small(flat, like):
    flat = flat.reshape(-1)
    out, at = {}, 0
    for k in SMALL:
        size = math.prod(like[k].shape)
        out[k] = flat[at:at + size].reshape(like[k].shape)
        at += size
    return out


def _unstack(stack, name):
    if name in COL_SHARDED:
        return stack.transpose(1, 0, 2).reshape(stack.shape[1], N_CHIPS * stack.shape[2])
    return stack.reshape(N_CHIPS * stack.shape[1], stack.shape[2])


def _stack(full, name):
    if name in COL_SHARDED:
        return full.reshape(full.shape[0], N_CHIPS, full.shape[1] // N_CHIPS).transpose(1, 0, 2)
    return full.reshape(N_CHIPS, full.shape[0] // N_CHIPS, full.shape[1])


EARLY = ("w_in", "w_out", "w_glu")
LATE = ("w_mlp_in", "w_mlp_out")


def _chip_sums(names, g_stacks, from_sibling, ic, chip):
    own, to_send = [], []
    for k, gs, fs in zip(names, g_stacks, from_sibling):
        half = gs.shape[1] // 2
        mine = lax.dynamic_slice_in_dim(gs, ic * half, half, axis=1).reshape(N_CHIPS * half, gs.shape[2])
        s32, s16 = _rowwise(lambda r, cc: [r[0] + r[1]] * 2, [mine, fs.reshape(mine.shape)], [],
                            [(gs.shape[2], F32), (gs.shape[2], BF16)], [], ROW_TILE[k], "grad_chip_sum_" + k)
        own.append(lax.dynamic_slice_in_dim(s32, chip * half, half, axis=0))
        to_send.append(s16.reshape(N_CHIPS, half, gs.shape[2]))
    return own, to_send


def _grad_totals(names, own, from_chips):
    totals = []
    for k, mine, fc in zip(names, own, from_chips):
        (tot,) = _rowwise(lambda r, cc: [r[0] + r[1].astype(F32) + r[2].astype(F32) + r[3].astype(F32)],
                          [mine, fc[0], fc[1], fc[2]], [], [(mine.shape[1], F32)], [], ROW_TILE[k], "grad_total_" + k)
        totals.append(tot)
    return totals


class _Overlap:
    def __init__(self, own_shards, ic, chip):
        self.ic, self.chip = ic, chip
        self.shapes = [o.shape for o in own_shards]
        lands = [lax.empty((N_CHIPS,) + s, BF16) for s in self.shapes]
        self.gather = _split_copy_start(own_shards, lands, _weight_plan(self.shapes), 3 * len(LATE),
                                        "mlp_weight_gather_start")
        self.token = self.gather[-1]

    def mlp_weights(self, after):
        n = len(LATE)
        done = _split_copy_wait(self.gather, _weight_plan(self.shapes), after, "mlp_weight_gather_wait")
        own, stacks = done[:n], done[n:]
        stacks = _forward_to_sibling(stacks, "mlp_weight_forward")
        stacks = [lax.dynamic_update_index_in_dim(s, o, self.chip, 0) for s, o in zip(stacks, own)]
        return [_unstack(s, k) for k, s in zip(LATE, stacks)]

    def mlp_grads_to_sibling(self, dw_in, dw_out):
        stacks = [dw_in, dw_out]
        self.g_shapes = [s.shape for s in stacks]
        lands = [lax.empty((N_CHIPS, s[1] // 2, s[2]), F32) for s in self.g_shapes]
        self.halves = _split_copy_start(stacks, lands, _halves_plan(self.g_shapes), len(LATE), "mlp_grad_halves_start")
        return self.halves[-1]

    def mlp_grads_to_chips(self, after):
        n = len(LATE)
        done = _split_copy_wait(self.halves, _halves_plan(self.g_shapes), after, "mlp_grad_halves_wait")
        self.own, to_send = _chip_sums(LATE, done[:n], done[n:], self.ic, self.chip)
        lands = [lax.empty((3,) + s.shape[1:], BF16) for s in to_send]
        self.exchange = _split_copy_start(to_send, lands, _exchange_plan(n), 3 * n, "mlp_grad_exchange_start")
        return self.exchange[-1]

    def mlp_grads_reduced(self, after):
        n = len(LATE)
        done = _split_copy_wait(self.exchange, _exchange_plan(n), after, "mlp_grad_exchange_wait")
        return _grad_totals(LATE, self.own, done[n:])


def _pad_rows(row):
    return jnp.pad(row, ((0, 8 - row.shape[0]), (0, 0)))


def _every_eighth(gathered):
    rows = gathered.reshape(N_DEV, 8, gathered.shape[1])[:, 0, :]
    return jnp.pad(rows, ((0, ROW_PAD - N_DEV), (0, 0)))


def kernel(x, c, positions, w_ada, b_ada, g_pre_mix, g_post_mix, w_in, ssm_a_re, ssm_a_im, ssm_log_dt, ssm_b_re, ssm_b_im, ssm_c_re, ssm_c_im, ssm_d, w_glu, b_glu, g_attn_out, g_ssm_out, w_out, g_pre_mlp, g_post_mlp, w_mlp_in, w_mlp_out, loss_target, m_w_ada, m_b_ada, m_g_pre_mix, m_g_post_mix, m_w_in, m_ssm_a_re, m_ssm_a_im, m_ssm_log_dt, m_ssm_b_re, m_ssm_b_im, m_ssm_c_re, m_ssm_c_im, m_ssm_d, m_w_glu, m_b_glu, m_g_attn_out, m_g_ssm_out, m_w_out, m_g_pre_mlp, m_g_post_mlp, m_w_mlp_in, m_w_mlp_out, v_w_ada, v_b_ada, v_g_pre_mix, v_g_post_mix, v_w_in, v_ssm_a_re, v_ssm_a_im, v_ssm_log_dt, v_ssm_b_re, v_ssm_b_im, v_ssm_c_re, v_ssm_c_im, v_ssm_d, v_w_glu, v_b_glu, v_g_attn_out, v_g_ssm_out, v_w_out, v_g_pre_mlp, v_g_post_mlp, v_w_mlp_in, v_w_mlp_out):
    given = dict(locals())
    w = {k: given[k][0] for k in WEIGHTS}
    mom = {k: given["m_" + k][0] for k in WEIGHTS}
    var = {k: given["v_" + k][0] for k in WEIGHTS}
    for tree in (w, mom, var):
        for k in ("b_ada", "g_pre_mix", "g_post_mix", "ssm_log_dt", "b_glu", "g_attn_out", "g_ssm_out", "g_pre_mlp",
                  "g_post_mlp"):
            tree[k] = tree[k].reshape(1, -1)
    ix, iy, ic = lax.axis_index("x"), lax.axis_index("y"), lax.axis_index("c")
    chip = 2 * ix + iy
    me = 4 * ix + 2 * iy + ic
    shard_cols = w["w_ada"].shape[1]
    overlap = _Overlap([w[k].astype(BF16) for k in LATE], ic, chip)

    c_all = _every_eighth(_all_gather8(_pad_rows(_tie(c, overlap.token)), "gather_c"))
    b_ada_cols = lax.dynamic_slice_in_dim(w["b_ada"], chip * shard_cols, shard_cols, axis=1)
    mod_cols = _ada_fwd(c_all, w["w_ada"], b_ada_cols, "ada_fwd")[:N_DEV]
    mod_all = _all_gather8(mod_cols, "gather_mod").reshape(N_CHIPS, 2, N_DEV, shard_cols)[:, 0]
    mod = lax.dynamic_index_in_dim(mod_all, me, axis=1, keepdims=False).reshape(1, N_MOD * D_MODEL)

    early_own = [w[k].astype(BF16) for k in EARLY]
    stacks = _weight_gather(early_own, "weight_gather")
    stacks = [lax.dynamic_update_index_in_dim(s, o, chip, 0) for s, o in zip(stacks, early_own)]
    wts = {k: _unstack(s, k) for k, s in zip(EARLY, stacks)}

    small = {k: w[k] for k in SMALL if k != "b_ada"}
    loss, grad_x, dmod, big_g, small_g = _local_step(x[0], positions.reshape(-1, 1), mod, loss_target[0], wts, small,
                                                     hooks=overlap)
    loss = lax.psum(loss[0, 0], ("x", "y", "c"))

    g_stacks = [_stack(big_g[k], k) for k in EARLY]
    from_sibling = _sibling_halves(g_stacks, "grad_sibling_halves")
    chip_f32, chip_bf16 = _chip_sums(EARLY, g_stacks, from_sibling, ic, chip)
    from_chips = _chip_exchange(chip_bf16, "grad_chip_exchange")
    reduced = _grad_totals(EARLY, chip_f32, from_chips) + overlap.mlp_grads_reduced(big_g["w_in"])
    swapped = _sibling_swap(reduced, "grad_sibling_swap")
    grads = {k: lax.dynamic_update_slice_in_dim(s, r, ic * r.shape[0], axis=0)
             for k, s, r in zip(EARLY + LATE, swapped, reduced)}

    small_g["b_ada"] = dmod
    parts = _all_gather8(_flatten_small(small_g), "gather_small_grads")

    small_flat = _sum_blocks(parts, N_DEV, "small_grad_sum")
    grads.update(_unflatten_small(small_flat, w))

    dmod_all = _every_eighth(_all_gather8(_pad_rows(dmod), "gather_dmod"))
    dmod_cols = lax.dynamic_slice_in_dim(dmod_all, chip * shard_cols, shard_cols, axis=1)
    g_ada, d_ada, m_ada, v_ada = _ada_bwd_adamw(c_all, dmod_cols, w["w_ada"], mom["w_ada"], var["w_ada"], "ada_bwd_adamw")
    grads["w_ada"] = g_ada

    delta, new_m, new_v = {"w_ada": d_ada}, {"w_ada": m_ada}, {"w_ada": v_ada}
    for k in BIG:
        delta[k], new_m[k], new_v[k] = _adamw_rows(w[k], grads[k], mom[k], var[k], ROW_TILE[k], "adamw_" + k)
    flat_upd = _adamw_rows(_flatten_small(w), small_flat, _flatten_small(mom), _flatten_small(var), FLAT_ROWS,
                           "adamw_small")
    for tree, flat in zip((delta, new_m, new_v), flat_upd):
        tree.update(_unflatten_small(flat, w))

    shaped = lambda tree: [tree[k].reshape(given[k].shape) for k in WEIGHTS]
    return (loss, grad_x[None], *shaped(grads), *shaped(delta), *shaped(new_m), *shaped(new_v))
```

```python
import functools
import math

import jax
import jax.numpy as jnp
import numpy as np
from jax import lax
from jax.experimental import pallas as pl
from jax.experimental.pallas import tpu as pltpu

F32 = jnp.float32
BF16 = jnp.bfloat16

D_MODEL = 2048
HEAD_DIM = 64
DILATIONS = (1, 4, 16)
ATT_SPAN = 128
ATT_BLK = 128
HEADS_PER_GROUP = 6
KV_WIDTH = HEADS_PER_GROUP * HEAD_DIM
ATT_Q_WIDTH = 3 * KV_WIDTH
ROT_DIM = 16
ROPE_THETA = 500000.0
SSM_WIDTH = 896
SSM_P = 16
SSM_G = 56
SSM_N = 64
SSM_GN = SSM_G * SSM_N
SSM_TILES = SSM_WIDTH // 128
SSM_TILE_GN = 8 * SSM_N
IN_WIDTH = 2816
OUT_IN_WIDTH = 1280
D_FF = 8192
N_MOD = 6
EPS = 1e-6
LANES = 128
SSM_SEGS = 8
SSM_CHUNK = 256
SSM_SEG_LEN = SSM_CHUNK // SSM_SEGS

ADAM_LR = 0.001
ADAM_B1 = 0.9
ADAM_B2 = 0.999
ADAM_EPS = 1e-08
ADAM_WD = 0.01
ADAM_STEP = 10

VMEM_LIMIT = 56 * 1024 * 1024


def _params(sem=None):
    return pltpu.CompilerParams(dimension_semantics=sem, vmem_limit_bytes=VMEM_LIMIT)


def _dot(a, b, dims):
    return lax.dot_general(a, b, (dims, ((), ())), preferred_element_type=F32)


def _nn(a, b):
    return _dot(a, b, ((1,), (0,)))


def _nt(a, b):
    return _dot(a, b, ((1,), (1,)))


def _tn(a, b):
    return _dot(a, b, ((0,), (0,)))


def _matmul(a, b, mode, out_dtype, tm, tn, tk, name, after=None):
    if mode == "nn":
        (m, k), (_, n) = a.shape, b.shape
        a_spec = pl.BlockSpec((tm, tk), lambda i, j, kk: (i, kk))
        b_spec = pl.BlockSpec((tk, tn), lambda i, j, kk: (kk, j))
        op = _nn
    elif mode == "nt":
        (m, k), (n, _) = a.shape, b.shape
        a_spec = pl.BlockSpec((tm, tk), lambda i, j, kk: (i, kk))
        b_spec = pl.BlockSpec((tn, tk), lambda i, j, kk: (j, kk))
        op = _nt
    else:
        (k, m), (_, n) = a.shape, b.shape
        a_spec = pl.BlockSpec((tk, tm), lambda i, j, kk: (kk, i))
        b_spec = pl.BlockSpec((tk, tn), lambda i, j, kk: (kk, j))
        op = _tn
    assert m % tm == 0 and n % tn == 0 and k % tk == 0, (name, m, n, k)
    nk = k // tk

    def body(a_ref, b_ref, *rest):
        o_ref, acc_ref = rest[-2:]
        kk = pl.program_id(2)

        @pl.when(kk == 0)
        def _():
            acc_ref[...] = jnp.zeros_like(acc_ref)

        acc_ref[...] += op(a_ref[...], b_ref[...])

        @pl.when(kk == nk - 1)
        def _():
            o_ref[...] = acc_ref[...].astype(o_ref.dtype)

    extra = [] if after is None else [after]
    return pl.pallas_call(
        body,
        name=name,
        grid=(m // tm, n // tn, nk),
        in_specs=[a_spec, b_spec] + [pl.BlockSpec(t.shape, lambda i, j, kk: (0, 0)) for t in extra],
        out_specs=pl.BlockSpec((tm, tn), lambda i, j, kk: (i, j)),
        out_shape=jax.ShapeDtypeStruct((m, n), out_dtype),
        scratch_shapes=[pltpu.VMEM((tm, tn), F32)],
        compiler_params=_params(("parallel", "parallel", "arbitrary")),
    )(a, b, *extra)


def _rowwise(fn, rows, consts, out_rows, out_accs, tm, name):
    n_rows = rows[0].shape[0]
    assert n_rows % tm == 0
    nr, nc, no = len(rows), len(consts), len(out_rows)

    def body(*refs):
        r_in, c_in = refs[:nr], refs[nr:nr + nc]
        o_row, o_acc = refs[nr + nc:nr + nc + no], refs[nr + nc + no:]
        outs = fn([r[...] for r in r_in], [c[...] for c in c_in])
        assert len(outs) == len(o_row) + len(o_acc), name
        for ref, v in zip(o_row, outs[:no]):
            ref[...] = v.astype(ref.dtype)
        first = pl.program_id(0) == 0
        for ref, v in zip(o_acc, outs[no:]):
            @pl.when(first)
            def _(ref=ref, v=v):
                ref[...] = v.astype(F32)

            @pl.when(jnp.logical_not(first))
            def _(ref=ref, v=v):
                ref[...] += v.astype(F32)

    in_specs = [pl.BlockSpec((tm, r.shape[1]), lambda i: (i, 0)) for r in rows]
    in_specs += [pl.BlockSpec(c.shape, lambda i: (0, 0)) for c in consts]
    out_specs = [pl.BlockSpec((tm, w), lambda i: (i, 0)) for w, _ in out_rows]
    out_specs += [pl.BlockSpec(s, lambda i: (0, 0)) for s in out_accs]
    out_shape = [jax.ShapeDtypeStruct((n_rows, w), dt) for w, dt in out_rows]
    out_shape += [jax.ShapeDtypeStruct(s, F32) for s in out_accs]
    return pl.pallas_call(
        body,
        name=name,
        grid=(n_rows // tm,),
        in_specs=in_specs,
        out_specs=out_specs,
        out_shape=out_shape,
        compiler_params=_params(("arbitrary",)),
    )(*rows, *consts)


def _rms(x, g):
    return x * lax.rsqrt(jnp.mean(x * x, axis=-1, keepdims=True) + EPS) * g


def _mod_norm(x, g, sc, sh):
    return _rms(x, g) * (1.0 + sc) + sh


def _gelu(x):
    return 0.5 * x * (1.0 + jnp.tanh(math.sqrt(2.0 / math.pi) * (x + 0.044715 * (x * x * x))))


def _sigmoid(x):
    return 1.0 / (1.0 + jnp.exp(-x))


def _post_mix(x, mix, g_post, gt1, g_pre, sc2, sh2):
    x1 = x + gt1 * _rms(mix, g_post)
    return x1, _mod_norm(x1, g_pre, sc2, sh2)


def _att_mix(o0, o1, o2, l0, l1, l2, g):
    m = jnp.maximum(jnp.maximum(l0, l1), l2)
    e0, e1, e2 = jnp.exp(l0 - m), jnp.exp(l1 - m), jnp.exp(l2 - m)
    att = (e0 * o0 + e1 * o1 + e2 * o2) / (e0 + e1 + e2)
    return _rms(att, g)


def _glu_out(y2, z, g):
    return _rms(y2 * _sigmoid(z), g)


def _rope_tables(pos_col, freq_lane, name):
    n_rows = pos_col.shape[0]
    tm = 512

    def body(p_ref, f_ref, cos_ref, lo_ref, hi_ref):
        ang = p_ref[...].astype(F32) * f_ref[...]
        lane = lax.broadcasted_iota(jnp.int32, ang.shape, 1) % HEAD_DIM
        c, s = jnp.cos(ang), jnp.sin(ang)
        cos_ref[...] = jnp.where(lane < ROT_DIM, c, 1.0)
        lo_ref[...] = jnp.where(lane < ROT_DIM // 2, -s, 0.0)
        hi_ref[...] = jnp.where((lane >= ROT_DIM // 2) & (lane < ROT_DIM), s, 0.0)

    tab = jax.ShapeDtypeStruct((n_rows, LANES), F32)
    return pl.pallas_call(
        body,
        name=name,
        grid=(n_rows // tm,),
        in_specs=[pl.BlockSpec((tm, 1), lambda i: (i, 0)), pl.BlockSpec((1, LANES), lambda i: (0, 0))],
        out_specs=[pl.BlockSpec((tm, LANES), lambda i: (i, 0))] * 3,
        out_shape=[tab] * 3,
        compiler_params=_params(("parallel",)),
    )(pos_col, freq_lane)


def _rope(x, cos_t, lo_t, hi_t):
    half = ROT_DIM // 2
    return x * cos_t + pltpu.roll(x, LANES - half, 1) * lo_t + pltpu.roll(x, half, 1) * hi_t


def _rope_transposed(dy, cos_t, lo_t, hi_t):
    half = ROT_DIM // 2
    return dy * cos_t + pltpu.roll(dy * lo_t, half, 1) + pltpu.roll(dy * hi_t, LANES - half, 1)


def _att_masks(i, k0):
    q_pos = i * ATT_BLK + lax.broadcasted_iota(jnp.int32, (ATT_BLK, 2 * ATT_BLK), 0)
    k_pos = k0 + lax.broadcasted_iota(jnp.int32, (ATT_BLK, 2 * ATT_BLK), 1)
    dist = q_pos - k_pos
    return (dist >= 0) & (dist <= ATT_SPAN)


def _head_lane_masks():
    lane = lax.broadcasted_iota(jnp.int32, (1, LANES), 1)
    return lane < HEAD_DIM, lane >= HEAD_DIM


def _att_specs(gi, d, n):
    cols = IN_WIDTH // LANES
    qkv = [
        pl.BlockSpec((n, LANES), lambda r, hp: (0, r * cols + gi * 3 + hp)),
        pl.BlockSpec((n, LANES), lambda r, hp: (0, r * cols + 9 + hp)),
        pl.BlockSpec((n, LANES), lambda r, hp: (0, r * cols + 12 + hp)),
    ]
    tabs = [pl.BlockSpec((n, LANES), lambda r, hp: (0, r))] * 3
    head = pl.BlockSpec((n, LANES), lambda r, hp: (0, r * 3 + hp))
    return qkv, tabs, head


def _att_load(q_ref, k_ref, v_ref, cos_ref, lo_ref, hi_ref, qs, ks, vs):
    cos_t, lo_t, hi_t = cos_ref[...], lo_ref[...], hi_ref[...]
    qs[...] = (_rope(q_ref[...], cos_t, lo_t, hi_t) * (1.0 / math.sqrt(HEAD_DIM))).astype(BF16)
    ks[...] = _rope(k_ref[...], cos_t, lo_t, hi_t).astype(BF16)
    vs[...] = v_ref[...].astype(BF16)


def _att_fwd(proj, tabs, gi, name):
    n_rows = proj.shape[0]
    d = DILATIONS[gi]
    n = n_rows // d
    nb = n // ATT_BLK
    proj_v = proj.reshape(n, d * IN_WIDTH)
    tabs_v = [t.reshape(n, d * LANES) for t in tabs]

    def body(q_ref, k_ref, v_ref, cos_ref, lo_ref, hi_ref, o_ref, l_ref, qs, ks, vs):
        _att_load(q_ref, k_ref, v_ref, cos_ref, lo_ref, hi_ref, qs, ks, vs)
        m0, m1 = _head_lane_masks()

        def step(i, carry):
            k0 = pl.multiple_of(jnp.maximum(i - 1, 0) * ATT_BLK, ATT_BLK)
            q0 = pl.multiple_of(i * ATT_BLK, ATT_BLK)
            q = qs[pl.ds(q0, ATT_BLK), :]
            k = ks[pl.ds(k0, 2 * ATT_BLK), :]
            v = vs[pl.ds(k0, 2 * ATT_BLK), :]
            valid = _att_masks(i, k0)
            outs, lses = [], []
            for hm in (m0, m1):
                s = _nt(jnp.where(hm, q, jnp.zeros_like(q)), k)
                s = jnp.where(valid, s, -1e30)
                mx = jnp.max(s, axis=1, keepdims=True)
                p = jnp.exp(s - mx)
                den = jnp.sum(p, axis=1, keepdims=True)
                outs.append(_nn(p.astype(BF16), v) / den)
                lses.append(mx + jnp.log(den))
            o_ref[pl.ds(q0, ATT_BLK), :] = jnp.where(m0, outs[0], outs[1])
            l_ref[pl.ds(q0, ATT_BLK), :] = jnp.where(m0, lses[0], lses[1])
            return carry

        lax.fori_loop(0, nb, step, 0)

    qkv, tab_specs, head = _att_specs(gi, d, n)
    out = jax.ShapeDtypeStruct((n, d * KV_WIDTH), F32)
    o, l = pl.pallas_call(
        body,
        name=name,
        grid=(d, 3),
        in_specs=qkv + tab_specs,
        out_specs=[head, head],
        out_shape=[out, out],
        scratch_shapes=[pltpu.VMEM((n, LANES), BF16)] * 3,
        compiler_params=_params(("parallel", "parallel")),
    )(proj_v, proj_v, proj_v, *tabs_v)
    return o.reshape(n_rows, KV_WIDTH), l.reshape(n_rows, KV_WIDTH)


def _att_bwd(proj, tabs, o, l, do, dl, gi, name):
    n_rows = proj.shape[0]
    d = DILATIONS[gi]
    n = n_rows // d
    nb = n // ATT_BLK
    proj_v = proj.reshape(n, d * IN_WIDTH)
    tabs_v = [t.reshape(n, d * LANES) for t in tabs]
    heads_v = [t.reshape(n, d * KV_WIDTH) for t in (o, l, do, dl)]

    def body(q_ref, k_ref, v_ref, cos_ref, lo_ref, hi_ref, o_ref, l_ref, do_ref, dl_ref,
             dq_ref, dk_ref, dv_ref, qs, ks, vs, dk_acc, dv_acc):
        _att_load(q_ref, k_ref, v_ref, cos_ref, lo_ref, hi_ref, qs, ks, vs)
        dk_acc[...] = jnp.zeros_like(dk_acc)
        dv_acc[...] = jnp.zeros_like(dv_acc)
        m0, m1 = _head_lane_masks()

        def step(i, carry):
            k0 = pl.multiple_of(jnp.maximum(i - 1, 0) * ATT_BLK, ATT_BLK)
            q0 = pl.multiple_of(i * ATT_BLK, ATT_BLK)
            rows = pl.ds(q0, ATT_BLK)
            keys = pl.ds(k0, 2 * ATT_BLK)
            q, k, v = qs[rows, :], ks[keys, :], vs[keys, :]
            d_o, lse = do_ref[rows, :], l_ref[rows, :]
            o_do = o_ref[rows, :] * d_o
            d_l = dl_ref[rows, :]
            valid = _att_masks(i, k0)
            dq = jnp.zeros((ATT_BLK, LANES), F32)
            dk = jnp.zeros((2 * ATT_BLK, LANES), F32)
            dv = jnp.zeros((2 * ATT_BLK, LANES), F32)
            for hm in (m0, m1):
                qh, kh = jnp.where(hm, q, jnp.zeros_like(q)), jnp.where(hm, k, jnp.zeros_like(k))
                doh = jnp.where(hm, d_o, 0.0).astype(BF16)
                lse_h = jnp.max(jnp.where(hm, lse, -1e30), axis=1, keepdims=True)
                delta = jnp.sum(jnp.where(hm, o_do, 0.0), axis=1, keepdims=True)
                dlse = jnp.sum(jnp.where(hm, d_l, 0.0), axis=1, keepdims=True)
                s = jnp.where(valid, _nt(qh, k), -1e30)
                p = jnp.exp(s - lse_h)
                dv = dv + _tn(p.astype(BF16), doh)
                ds = (p * (_nt(doh, v) - delta + dlse)).astype(BF16)
                dq = dq + _nn(ds, kh)
                dk = dk + _tn(ds, qh)
            cos_t, lo_t, hi_t = cos_ref[rows, :], lo_ref[rows, :], hi_ref[rows, :]
            dq_ref[rows, :] = _rope_transposed(dq * (1.0 / math.sqrt(HEAD_DIM)), cos_t, lo_t, hi_t)
            dk_acc[keys, :] += dk
            dv_acc[keys, :] += dv
            return carry

        lax.fori_loop(0, nb, step, 0)
        dk_ref[...] = _rope_transposed(dk_acc[...], cos_ref[...], lo_ref[...], hi_ref[...])
        dv_ref[...] = dv_acc[...]

    qkv, tab_specs, head = _att_specs(gi, d, n)
    out = jax.ShapeDtypeStruct((n, d * KV_WIDTH), F32)
    res = pl.pallas_call(
        body,
        name=name,
        grid=(d, 3),
        in_specs=qkv + tab_specs + [head] * 4,
        out_specs=[head] * 3,
        out_shape=[out] * 3,
        scratch_shapes=[pltpu.VMEM((n, LANES), BF16)] * 3 + [pltpu.VMEM((n, LANES), F32)] * 2,
        compiler_params=_params(("parallel", "parallel")),
    )(proj_v, proj_v, proj_v, *tabs_v, *heads_v)
    return [t.reshape(n_rows, KV_WIDTH) for t in res]


def _expand_np():
    e = np.zeros((SSM_N, SSM_N * SSM_P), np.float32)
    for nn in range(SSM_N):
        e[nn, nn * SSM_P:(nn + 1) * SSM_P] = 1.0
    return e


def _ssm_prep_math(a_re, a_im, log_dt, b_re, b_im, expand):
    dt = jnp.exp(log_dt)
    mag = jnp.exp(a_re * dt)
    ab_re, ab_im = mag * jnp.cos(a_im * dt), mag * jnp.sin(a_im * dt)
    den = a_re * a_re + a_im * a_im
    num_re, num_im = ab_re - 1.0, ab_im
    co_re = (num_re * a_re + num_im * a_im) / den
    co_im = (num_im * a_re - num_re * a_im) / den
    hi = lax.Precision.HIGHEST
    co_re_x = jnp.dot(co_re, expand, precision=hi, preferred_element_type=F32)
    co_im_x = jnp.dot(co_im, expand, precision=hi, preferred_element_type=F32)
    bb_re = co_re_x * b_re - co_im_x * b_im
    bb_im = co_re_x * b_im + co_im_x * b_re
    return ab_re, ab_im, bb_re, bb_im


def _ssm_prep(a_re, a_im, log_dt, b_re, b_im, expand, name):
    def body(ar, ai, ld, br, bi, ex, o0, o1, o2, o3):
        outs = _ssm_prep_math(ar[...], ai[...], ld[...], br[...], bi[...], ex[...])
        for ref, v in zip((o0, o1, o2, o3), outs):
            ref[...] = v

    gn = jax.ShapeDtypeStruct((SSM_G, SSM_N), F32)
    gnp = jax.ShapeDtypeStruct((SSM_G, SSM_N * SSM_P), F32)
    return pl.pallas_call(body, name=name, out_shape=[gn, gn, gnp, gnp], compiler_params=_params())(
        a_re, a_im, log_dt, b_re, b_im, expand)


def _ssm_prep_bwd(a_re, a_im, log_dt, b_re, b_im, expand, cts, name):
    def body(ar, ai, ld, br, bi, ex, c0, c1, c2, c3, o0, o1, o2, o3, o4):
        ex_v = ex[...]
        _, vjp = jax.vjp(lambda *p: _ssm_prep_math(*p, ex_v), ar[...], ai[...], ld[...], br[...], bi[...])
        for ref, v in zip((o0, o1, o2, o3, o4), vjp((c0[...], c1[...], c2[...], c3[...]))):
            ref[...] = v

    gn = jax.ShapeDtypeStruct((SSM_G, SSM_N), F32)
    gnp = jax.ShapeDtypeStruct((SSM_G, SSM_N * SSM_P), F32)
    g1 = jax.ShapeDtypeStruct((SSM_G, 1), F32)
    return pl.pallas_call(body, name=name, out_shape=[gn, gn, g1, gnp, gnp], compiler_params=_params())(
        a_re, a_im, log_dt, b_re, b_im, expand, *cts)


def _block_diag_in(bb):
    t = bb.reshape(SSM_TILES, 8, SSM_N, SSM_P).transpose(0, 1, 3, 2)
    eye = jnp.eye(8, dtype=bb.dtype)
    return (t[:, :, :, None, :] * eye[None, :, None, :, None]).reshape(SSM_TILES, LANES, SSM_TILE_GN)


def _block_diag_in_grad(dblk):
    t = dblk.reshape(SSM_TILES, 8, SSM_P, 8, SSM_N)
    t = jnp.einsum("tapbn,ab->tapn", t, jnp.eye(8, dtype=dblk.dtype))
    return t.transpose(0, 1, 3, 2).reshape(SSM_G, SSM_N, SSM_P)


def _block_diag_out(cm):
    t = cm.reshape(SSM_TILES, 8, SSM_P, SSM_N).transpose(0, 1, 3, 2)
    eye = jnp.eye(8, dtype=cm.dtype)
    return (t[:, :, :, None, :] * eye[None, :, None, :, None]).reshape(SSM_TILES, SSM_TILE_GN, LANES)


def _block_diag_out_grad(dblk):
    t = dblk.reshape(SSM_TILES, 8, SSM_N, 8, SSM_P)
    t = jnp.einsum("tanbp,ab->tanp", t, jnp.eye(8, dtype=dblk.dtype))
    return t.transpose(0, 1, 3, 2).reshape(SSM_G, SSM_P, SSM_N)


def _cmul_add(a_re, a_im, s_re, s_im, b_re, b_im):
    return a_re * s_re - a_im * s_im + b_re, a_re * s_im + a_im * s_re + b_im


def _lane_tiled(a):
    return a.reshape(a.shape[0], SSM_TILES, LANES).transpose(1, 0, 2)


def _lane_untiled(a):
    return a.transpose(1, 0, 2).reshape(a.shape[1], SSM_WIDTH)


def _ssm_load_rows(src_ref, dst):
    for t in range(SSM_TILES):
        for i in range(SSM_SEG_LEN):
            dst[i * SSM_SEGS:(i + 1) * SSM_SEGS, t * LANES:(t + 1) * LANES] = (
                src_ref[t, pl.ds(i, SSM_SEGS, stride=SSM_SEG_LEN), :])


def _ssm_store_rows(src, dst_ref):
    for t in range(SSM_TILES):
        for i in range(SSM_SEG_LEN):
            dst_ref[t, pl.ds(i, SSM_SEGS, stride=SSM_SEG_LEN), :] = (
                src[i * SSM_SEGS:(i + 1) * SSM_SEGS, t * LANES:(t + 1) * LANES])


def _ssm_powers(ab_re_ref, ab_im_ref, pw_re, pw_im):
    a_re, a_im = ab_re_ref[...], ab_im_ref[...]
    p_re, p_im = a_re, a_im
    for i in range(SSM_SEG_LEN):
        pw_re[i:i + 1, :] = p_re
        pw_im[i:i + 1, :] = p_im
        p_re, p_im = _cmul_add(a_re, a_im, p_re, p_im, 0.0, 0.0)


def _ssm_input_proj(u_s, bblk_re_ref, bblk_im_ref, s_re, s_im):
    for t in range(SSM_TILES):
        ub = u_s[:, t * LANES:(t + 1) * LANES].astype(BF16)
        cols = slice(t * SSM_TILE_GN, (t + 1) * SSM_TILE_GN)
        s_re[:, cols] = _nn(ub, bblk_re_ref[t])
        s_im[:, cols] = _nn(ub, bblk_im_ref[t])


def _ssm_scan(ab_re_ref, ab_im_ref, s_re, s_im, init_re, init_im, conj, reverse):
    sign = -1.0 if conj else 1.0
    for t in range(SSM_TILES):
        cols = slice(t * SSM_TILE_GN, (t + 1) * SSM_TILE_GN)
        a_re = jnp.broadcast_to(ab_re_ref[:, cols], (SSM_SEGS, SSM_TILE_GN))
        a_im = jnp.broadcast_to(ab_im_ref[:, cols], (SSM_SEGS, SSM_TILE_GN)) * sign
        if init_re is None:
            st = (jnp.zeros((SSM_SEGS, SSM_TILE_GN), F32),) * 2
        else:
            st = (init_re[:, cols], init_im[:, cols])

        def step(i, st, cols=cols, a_re=a_re, a_im=a_im):
            idx = (SSM_SEG_LEN - 1 - i) if reverse else i
            rows = pl.ds(pl.multiple_of(idx * SSM_SEGS, SSM_SEGS), SSM_SEGS)
            n_re, n_im = _cmul_add(a_re, a_im, st[0], st[1], s_re[rows, cols], s_im[rows, cols])
            s_re[rows, cols] = n_re
            s_im[rows, cols] = n_im
            return n_re, n_im
        lax.fori_loop(0, SSM_SEG_LEN, step, st, unroll=4)


def _ssm_fixup(pw_re, pw_im, s_re, s_im, cin_re, cin_im, conj, reverse):
    sign = -1.0 if conj else 1.0
    c_re, c_im = cin_re[...], cin_im[...]

    def step(i, c):
        k = (SSM_SEG_LEN - 1 - i) if reverse else i
        rows = pl.ds(pl.multiple_of(i * SSM_SEGS, SSM_SEGS), SSM_SEGS)
        p_re = jnp.broadcast_to(pw_re[pl.ds(k, 1), :], (SSM_SEGS, SSM_GN))
        p_im = jnp.broadcast_to(pw_im[pl.ds(k, 1), :], (SSM_SEGS, SSM_GN)) * sign
        n_re, n_im = _cmul_add(p_re, p_im, c_re, c_im, s_re[rows, :], s_im[rows, :])
        s_re[rows, :] = n_re
        s_im[rows, :] = n_im
        return c
    lax.fori_loop(0, SSM_SEG_LEN, step, 0)


def _ssm_fwd(u, ab_re, ab_im, bblk_re, bblk_im, cblk_re, cblk_im, d_row, name):
    n_rows = u.shape[1]
    nchunk = n_rows // SSM_CHUNK
    last = SSM_SEG_LEN - 1

    def body(u_ref, ar_ref, ai_ref, br_ref, bi_ref, cr_ref, ci_ref, d_ref, y_ref, cin_re_ref, cin_im_ref,
             u_s, s_re, s_im, pw_re, pw_im, st_re, st_im):
        @pl.when(pl.program_id(0) == 0)
        def _():
            _ssm_powers(ar_ref, ai_ref, pw_re, pw_im)
            st_re[...] = jnp.zeros_like(st_re)
            st_im[...] = jnp.zeros_like(st_im)

        _ssm_load_rows(u_ref, u_s)
        _ssm_input_proj(u_s, br_ref, bi_ref, s_re, s_im)
        _ssm_scan(ar_ref, ai_ref, s_re, s_im, None, None, conj=False, reverse=False)
        p_re, p_im = pw_re[last:last + 1, :], pw_im[last:last + 1, :]
        c_re, c_im = st_re[...], st_im[...]
        for j in range(SSM_SEGS):
            cin_re_ref[j:j + 1, :] = c_re
            cin_im_ref[j:j + 1, :] = c_im
            row = last * SSM_SEGS + j
            c_re, c_im = _cmul_add(p_re, p_im, c_re, c_im, s_re[row:row + 1, :], s_im[row:row + 1, :])
        st_re[...] = c_re
        st_im[...] = c_im
        _ssm_fixup(pw_re, pw_im, s_re, s_im, cin_re_ref, cin_im_ref, conj=False, reverse=False)
        for t in range(SSM_TILES):
            cols = slice(t * SSM_TILE_GN, (t + 1) * SSM_TILE_GN)
            lanes = slice(t * LANES, (t + 1) * LANES)
            y = _nn(s_re[:, cols].astype(BF16), cr_ref[t]) - _nn(s_im[:, cols].astype(BF16), ci_ref[t])
            u_s[:, lanes] = y + d_ref[:, lanes] * u_s[:, lanes]
        _ssm_store_rows(u_s, y_ref)

    whole2 = lambda a: pl.BlockSpec(a.shape, lambda c: (0, 0))
    whole3 = lambda a: pl.BlockSpec(a.shape, lambda c: (0, 0, 0))
    chunk = pl.BlockSpec((SSM_TILES, SSM_CHUNK, LANES), lambda c: (0, c, 0))
    seg = pl.BlockSpec((SSM_SEGS, SSM_GN), lambda c: (c, 0))
    seg_shape = jax.ShapeDtypeStruct((nchunk * SSM_SEGS, SSM_GN), F32)
    return pl.pallas_call(
        body,
        name=name,
        grid=(nchunk,),
        in_specs=[chunk, whole2(ab_re), whole2(ab_im), whole3(bblk_re), whole3(bblk_im), whole3(cblk_re),
                  whole3(cblk_im), whole2(d_row)],
        out_specs=[chunk, seg, seg],
        out_shape=[jax.ShapeDtypeStruct((SSM_TILES, n_rows, LANES), F32), seg_shape, seg_shape],
        scratch_shapes=[pltpu.VMEM((SSM_CHUNK, SSM_WIDTH), F32), pltpu.VMEM((SSM_CHUNK, SSM_GN), F32),
                        pltpu.VMEM((SSM_CHUNK, SSM_GN), F32), pltpu.VMEM((SSM_SEG_LEN, SSM_GN), F32),
                        pltpu.VMEM((SSM_SEG_LEN, SSM_GN), F32), pltpu.VMEM((1, SSM_GN), F32),
                        pltpu.VMEM((1, SSM_GN), F32)],
        compiler_params=_params(("arbitrary",)),
    )(u, ab_re, ab_im, bblk_re, bblk_im, cblk_re, cblk_im, d_row)


def _ssm_bwd(u, dy, cin_re, cin_im, ab_re, ab_im, bblk_re, bblk_im, cblk_re, cblk_im, d_row, name):
    n_rows = u.shape[1]
    nchunk = n_rows // SSM_CHUNK

    def body(u_ref, dy_ref, cin_re_ref, cin_im_ref, ar_ref, ai_ref, br_ref, bi_ref, cr_ref, ci_ref, d_ref,
             du_ref, dar_ref, dai_ref, dbr_ref, dbi_ref, dcr_ref, dci_ref, dd_ref,
             u_s, dy_s, s_re, s_im, q_re, q_im, pw_re, pw_im, qst_re, qst_im, qin_re, qin_im):
        @pl.when(pl.program_id(0) == 0)
        def _():
            _ssm_powers(ar_ref, ai_ref, pw_re, pw_im)
            qst_re[...] = jnp.zeros_like(qst_re)
            qst_im[...] = jnp.zeros_like(qst_im)
            for ref in (dar_ref, dai_ref, dbr_ref, dbi_ref, dcr_ref, dci_ref, dd_ref):
                ref[...] = jnp.zeros_like(ref)

        _ssm_load_rows(u_ref, u_s)
        _ssm_load_rows(dy_ref, dy_s)
        _ssm_input_proj(u_s, br_ref, bi_ref, s_re, s_im)
        _ssm_scan(ar_ref, ai_ref, s_re, s_im, cin_re_ref, cin_im_ref, conj=False, reverse=False)
        for t in range(SSM_TILES):
            cols = slice(t * SSM_TILE_GN, (t + 1) * SSM_TILE_GN)
            dyb = dy_s[:, t * LANES:(t + 1) * LANES].astype(BF16)
            q_re[:, cols] = _nt(dyb, cr_ref[t])
            q_im[:, cols] = -_nt(dyb, ci_ref[t])
            dcr_ref[t] += _tn(s_re[:, cols].astype(BF16), dyb)
            dci_ref[t] -= _tn(s_im[:, cols].astype(BF16), dyb)
        _ssm_scan(ar_ref, ai_ref, q_re, q_im, None, None, conj=True, reverse=True)
        last = SSM_SEG_LEN - 1
        p_re, p_im = pw_re[last:last + 1, :], -pw_im[last:last + 1, :]
        c_re, c_im = qst_re[...], qst_im[...]
        for j in reversed(range(SSM_SEGS)):
            qin_re[j:j + 1, :] = c_re
            qin_im[j:j + 1, :] = c_im
            c_re, c_im = _cmul_add(p_re, p_im, c_re, c_im, q_re[j:j + 1, :], q_im[j:j + 1, :])
        qst_re[...] = c_re
        qst_im[...] = c_im
        _ssm_fixup(pw_re, pw_im, q_re, q_im, qin_re, qin_im, conj=True, reverse=True)
        for t in range(SSM_TILES):
            cols = slice(t * SSM_TILE_GN, (t + 1) * SSM_TILE_GN)

            def step(i, acc, cols=cols):
                rows = pl.ds(pl.multiple_of(i * SSM_SEGS, SSM_SEGS), SSM_SEGS)
                prev = pl.ds(pl.multiple_of((i - 1) * SSM_SEGS, SSM_SEGS), SSM_SEGS)
                qr, qi = q_re[rows, cols], q_im[rows, cols]
                sr, si = s_re[prev, cols], s_im[prev, cols]
                return acc[0] + qr * sr + qi * si, acc[1] + qi * sr - qr * si

            qr, qi = q_re[0:SSM_SEGS, cols], q_im[0:SSM_SEGS, cols]
            sr, si = cin_re_ref[:, cols], cin_im_ref[:, cols]
            acc = lax.fori_loop(1, SSM_SEG_LEN, step, (qr * sr + qi * si, qi * sr - qr * si))
            dar_ref[:, cols] += jnp.sum(acc[0], axis=0, keepdims=True)
            dai_ref[:, cols] += jnp.sum(acc[1], axis=0, keepdims=True)
        for t in range(SSM_TILES):
            cols = slice(t * SSM_TILE_GN, (t + 1) * SSM_TILE_GN)
            lanes = slice(t * LANES, (t + 1) * LANES)
            qrb, qib = q_re[:, cols].astype(BF16), q_im[:, cols].astype(BF16)
            u_t, dy_t = u_s[:, lanes], dy_s[:, lanes]
            ub = u_t.astype(BF16)
            dbr_ref[t] += _tn(ub, qrb)
            dbi_ref[t] += _tn(ub, qib)
            dd_ref[:, lanes] += jnp.sum(dy_t * u_t, axis=0, keepdims=True)
            u_s[:, lanes] = _nt(qrb, br_ref[t]) + _nt(qib, bi_ref[t]) + dy_t * d_ref[:, lanes]
        _ssm_store_rows(u_s, du_ref)

    whole2 = lambda a: pl.BlockSpec(a.shape, lambda c: (0, 0))
    whole3 = lambda a: pl.BlockSpec(a.shape, lambda c: (0, 0, 0))
    chunk = pl.BlockSpec((SSM_TILES, SSM_CHUNK, LANES), lambda c: (0, nchunk - 1 - c, 0))
    seg = pl.BlockSpec((SSM_SEGS, SSM_GN), lambda c: (nchunk - 1 - c, 0))
    gn_row = jax.ShapeDtypeStruct((1, SSM_GN), F32)
    b_shape = jax.ShapeDtypeStruct((SSM_TILES, LANES, SSM_TILE_GN), F32)
    c_shape = jax.ShapeDtypeStruct((SSM_TILES, SSM_TILE_GN, LANES), F32)
    d_shape = jax.ShapeDtypeStruct((1, SSM_WIDTH), F32)
    big = pltpu.VMEM((SSM_CHUNK, SSM_GN), F32)
    return pl.pallas_call(
        body,
        name=name,
        grid=(nchunk,),
        in_specs=[chunk, chunk, seg, seg, whole2(ab_re), whole2(ab_im), whole3(bblk_re), whole3(bblk_im),
                  whole3(cblk_re), whole3(cblk_im), whole2(d_row)],
        out_specs=[chunk, whole2(ab_re), whole2(ab_im), whole3(bblk_re), whole3(bblk_im), whole3(cblk_re),
                   whole3(cblk_im), whole2(d_row)],
        out_shape=[jax.ShapeDtypeStruct((SSM_TILES, n_rows, LANES), F32), gn_row, gn_row, b_shape, b_shape, c_shape,
                   c_shape, d_shape],
        scratch_shapes=[pltpu.VMEM((SSM_CHUNK, SSM_WIDTH), F32), pltpu.VMEM((SSM_CHUNK, SSM_WIDTH), F32),
                        big, big, big, big,
                        pltpu.VMEM((SSM_SEG_LEN, SSM_GN), F32), pltpu.VMEM((SSM_SEG_LEN, SSM_GN), F32),
                        pltpu.VMEM((1, SSM_GN), F32), pltpu.VMEM((1, SSM_GN), F32),
                        pltpu.VMEM((SSM_SEGS, SSM_GN), F32), pltpu.VMEM((SSM_SEGS, SSM_GN), F32)],
        compiler_params=_params(("arbitrary",)),
    )(u, dy, cin_re, cin_im, ab_re, ab_im, bblk_re, bblk_im, cblk_re, cblk_im, d_row)


def _mlp_fwd(h2, w1, w2, tm, tf, name):
    n_rows, dm = h2.shape
    dff = w1.shape[1]

    def body(h_ref, w1_ref, w2_ref, a_ref, y_ref):
        a = _nn(h_ref[...], w1_ref[...])
        a_ref[...] = a.astype(BF16)
        r = jnp.maximum(a, 0.0)
        part = _nn((r * r).astype(BF16), w2_ref[...])
        j = pl.program_id(1)

        @pl.when(j == 0)
        def _():
            y_ref[...] = part

        @pl.when(j > 0)
        def _():
            y_ref[...] += part

    return pl.pallas_call(
        body,
        name=name,
        grid=(n_rows // tm, dff // tf),
        in_specs=[pl.BlockSpec((tm, dm), lambda i, j: (i, 0)), pl.BlockSpec((dm, tf), lambda i, j: (0, j)),
                  pl.BlockSpec((tf, dm), lambda i, j: (j, 0))],
        out_specs=[pl.BlockSpec((tm, tf), lambda i, j: (i, j)), pl.BlockSpec((tm, dm), lambda i, j: (i, 0))],
        out_shape=[jax.ShapeDtypeStruct((n_rows, dff), BF16), jax.ShapeDtypeStruct((n_rows, dm), F32)],
        compiler_params=_params(("parallel", "arbitrary")),
    )(h2, w1, w2)


def _mlp_bwd(dy, h2, a, w2, tm, tf, name):
    n_rows, dm = h2.shape
    dff = a.shape[1]
    per_chip = dff // N_CHIPS // tf

    def body(dy_ref, h_ref, a_ref, w2_ref, da_ref, dw2_ref, dw1_ref):
        dyb = dy_ref[...]
        r = jnp.maximum(a_ref[...].astype(F32), 0.0)
        da = (_nt(dyb, w2_ref[...]) * (2.0 * r)).astype(BF16)
        da_ref[...] = da
        p2 = _tn((r * r).astype(BF16), dyb)
        p1 = _tn(h_ref[...], da)
        i = pl.program_id(1)

        @pl.when(i == 0)
        def _():
            dw2_ref[...] = p2
            dw1_ref[...] = p1

        @pl.when(i > 0)
        def _():
            dw2_ref[...] += p2
            dw1_ref[...] += p1

    return pl.pallas_call(
        body,
        name=name,
        grid=(dff // tf, n_rows // tm),
        in_specs=[pl.BlockSpec((tm, dm), lambda j, i: (i, 0)), pl.BlockSpec((tm, dm), lambda j, i: (i, 0)),
                  pl.BlockSpec((tm, tf), lambda j, i: (i, j)), pl.BlockSpec((tf, dm), lambda j, i: (j, 0))],
        out_specs=[pl.BlockSpec((tm, tf), lambda j, i: (i, j)), pl.BlockSpec((tf, dm), lambda j, i: (j, 0)),
                   pl.BlockSpec((None, dm, tf), lambda j, i: (j // per_chip, 0, j % per_chip))],
        out_shape=[jax.ShapeDtypeStruct((n_rows, dff), BF16), jax.ShapeDtypeStruct((dff, dm), F32),
                   jax.ShapeDtypeStruct((N_CHIPS, dm, dff // N_CHIPS), F32)],
        compiler_params=_params(("parallel", "arbitrary")),
    )(dy, h2, a, w2)


def _local_step(x, pos_col, mod, target, wts, small, hooks=None):
    n_rows = x.shape[0]
    sh1, sc1, gt1, sh2, sc2, gt2 = (mod[:, i * D_MODEL:(i + 1) * D_MODEL] for i in range(N_MOD))
    tm = 256
    d_acc = (1, D_MODEL)

    (h1,) = _rowwise(lambda r, c: [_mod_norm(r[0], *c)], [x], [small["g_pre_mix"], sc1, sh1],
                     [(D_MODEL, BF16)], [], tm, "pre_mix_fwd")
    proj = _matmul(h1, wts["w_in"], "nn", F32, 512, 1408, 2048, "in_proj")

    freqs = ROPE_THETA ** (-jnp.arange(0, ROT_DIM, 2, dtype=F32) / ROT_DIM)
    freq_lane = jnp.tile(freqs, LANES // (ROT_DIM // 2))[None, :]
    tabs = _rope_tables(pos_col, freq_lane, "rope_tables")
    att = [_att_fwd(proj, tabs, gi, f"att_fwd_{gi}") for gi in range(3)]

    expand = jnp.asarray(_expand_np())
    b_re2, b_im2 = small["ssm_b_re"].reshape(SSM_G, -1), small["ssm_b_im"].reshape(SSM_G, -1)
    log_dt = small["ssm_log_dt"].reshape(SSM_G, 1)
    prep_in = (small["ssm_a_re"], small["ssm_a_im"], log_dt, b_re2, b_im2, expand)
    ab_re, ab_im, bb_re, bb_im = _ssm_prep(*prep_in, "ssm_prep")
    ab_re_row, ab_im_row = ab_re.reshape(1, SSM_GN), ab_im.reshape(1, SSM_GN)
    bblk = [_block_diag_in(t.reshape(SSM_G, SSM_N, SSM_P)).astype(BF16) for t in (bb_re, bb_im)]
    cblk = [_block_diag_out(small[k]).astype(BF16) for k in ("ssm_c_re", "ssm_c_im")]
    d_row = small["ssm_d"].reshape(1, SSM_WIDTH)
    u = _lane_tiled(proj[:, IN_WIDTH - SSM_WIDTH:])
    y_ssm, cin_re, cin_im = _ssm_fwd(u, ab_re_row, ab_im_row, *bblk, *cblk, d_row, "ssm_fwd")
    y_ssm = _lane_untiled(y_ssm)

    def mixers_out(r, c):
        w_glu, b_glu, g_att, g_ssm = c
        att_n = _att_mix(*r[:6], g_att)
        y2 = _gelu(r[6])
        z = _nn(y2.astype(BF16), w_glu) + b_glu
        return [jnp.concatenate([att_n.astype(BF16), _glu_out(y2, z, g_ssm).astype(BF16)], axis=1)]

    att_rows = [a[0] for a in att] + [a[1] for a in att]
    mix_consts = [wts["w_glu"], small["b_glu"], small["g_attn_out"], small["g_ssm_out"]]
    (cat,) = _rowwise(mixers_out, att_rows + [y_ssm], mix_consts, [(OUT_IN_WIDTH, BF16)], [], tm, "mixers_out_fwd")
    mix = _matmul(cat, wts["w_out"], "nn", F32, 512, 1024, 1280, "out_proj")

    post_consts = [small["g_post_mix"], gt1, small["g_pre_mlp"], sc2, sh2]
    x1, h2 = _rowwise(lambda r, c: list(_post_mix(r[0], r[1], *c)), [x, mix], post_consts,
                      [(D_MODEL, F32), (D_MODEL, BF16)], [], tm, "post_mix_fwd")
    w_mlp_in, w_mlp_out = (wts["w_mlp_in"], wts["w_mlp_out"]) if hooks is None else hooks.mlp_weights(h2)
    a_mlp, y_mlp = _mlp_fwd(h2, w_mlp_in, w_mlp_out, 512, 512, "mlp_fwd")

    def loss_head(r, c):
        x1_v, y_v, t_v = r
        g, gt = c
        fn = lambda y_, g_, gt_: gt_ * _rms(y_, g_)
        out, vjp = jax.vjp(fn, y_v, g, gt)
        err = x1_v + out - t_v
        dx2 = err * (1.0 / D_MODEL)
        dy, dg, dgt = vjp(dx2)
        loss = 0.5 * jnp.sum(jnp.sum(err * err, axis=1, keepdims=True), axis=0, keepdims=True) * (1.0 / D_MODEL)
        return [dx2, dy, loss, dg, dgt]

    dx2, dy_mlp, loss, dg_post_mlp, dgt2 = _rowwise(
        loss_head, [x1, y_mlp, target], [small["g_post_mlp"], gt2],
        [(D_MODEL, F32), (D_MODEL, BF16)], [(1, 1), d_acc, d_acc], tm, "loss_head")

    da_mlp, dw_mlp_out, dw_mlp_in = _mlp_bwd(dy_mlp, h2, a_mlp, w_mlp_out, 512, 512, "mlp_bwd")
    dw_mlp_out = dw_mlp_out.reshape(dw_mlp_in.shape)
    sent = None if hooks is None else hooks.mlp_grads_to_sibling(dw_mlp_in, dw_mlp_out)
    dh2 = _matmul(da_mlp, w_mlp_in, "nt", F32, 512, 1024, 2048, "mlp_in_bwd", after=sent)
    sent = None if hooks is None else hooks.mlp_grads_to_chips(dh2)

    def post_mix_bwd(r, c):
        x_v, mix_v, dx1_v, dh2_v = r
        _, vjp = jax.vjp(_post_mix, x_v, mix_v, *c)
        return list(vjp((dx1_v, dh2_v)))

    post_consts_bwd = post_consts if sent is None else [_tie(post_consts[0], sent)] + post_consts[1:]
    dx_a, dmix, dg_post_mix, dgt1, dg_pre_mlp, dsc2, dsh2 = _rowwise(
        post_mix_bwd, [x, mix, dx2, dh2], post_consts_bwd, [(D_MODEL, F32), (D_MODEL, BF16)], [d_acc] * 5, tm,
        "post_mix_bwd")

    dcat = _matmul(dmix, wts["w_out"], "nt", F32, 512, 1280, 2048, "out_proj_bwd")
    dw_out = _matmul(cat, dmix, "tn", F32, 1280, 1024, 512, "out_proj_wgrad")

    def mixers_out_bwd(r, c):
        w_glu, b_glu, g_att, g_ssm = c
        dcat_v = r[7]
        _, vjp_att = jax.vjp(_att_mix, *r[:6], g_att)
        *d_ol, dg_att = vjp_att(dcat_v[:, :KV_WIDTH])
        y2, vjp_gelu = jax.vjp(_gelu, r[6])
        y2b = y2.astype(BF16)
        z = _nn(y2b, w_glu) + b_glu
        _, vjp_glu = jax.vjp(_glu_out, y2, z, g_ssm)
        dy2, dz, dg_ssm = vjp_glu(dcat_v[:, KV_WIDTH:])
        dzb = dz.astype(BF16)
        (dy,) = vjp_gelu(dy2 + _nt(dzb, w_glu))
        return d_ol + [dy, dg_att, _tn(y2b, dzb), jnp.sum(dz, axis=0, keepdims=True), dg_ssm]

    *d_att, dy_ssm, dg_attn_out, dw_glu, db_glu, dg_ssm_out = _rowwise(
        mixers_out_bwd, att_rows + [y_ssm, dcat], mix_consts,
        [(KV_WIDTH, F32)] * 6 + [(SSM_WIDTH, F32)],
        [(1, KV_WIDTH), (SSM_WIDTH, SSM_WIDTH), (1, SSM_WIDTH), (1, SSM_WIDTH)], tm, "mixers_out_bwd")

    du, dab_re, dab_im, dbblk_re, dbblk_im, dcblk_re, dcblk_im, dd_row = _ssm_bwd(
        u, _lane_tiled(dy_ssm), cin_re, cin_im, ab_re_row, ab_im_row, *bblk, *cblk, d_row, "ssm_bwd")
    du = _lane_untiled(du)
    prep_cts = (dab_re.reshape(SSM_G, SSM_N), dab_im.reshape(SSM_G, SSM_N),
                _block_diag_in_grad(dbblk_re).reshape(SSM_G, -1), _block_diag_in_grad(dbblk_im).reshape(SSM_G, -1))
    da_re, da_im, dlog_dt, db_re, db_im = _ssm_prep_bwd(*prep_in, prep_cts, "ssm_prep_bwd")

    dqkv = [_att_bwd(proj, tabs, att[gi][0], att[gi][1], d_att[gi], d_att[3 + gi], gi, f"att_bwd_{gi}")
            for gi in range(3)]

    def gather_dproj(r, c):
        dq = [r[0], r[3], r[6]]
        dk = r[1] + r[4] + r[7]
        dv = r[2] + r[5] + r[8]
        return [jnp.concatenate([t.astype(BF16) for t in dq + [dk, dv, r[9]]], axis=1)]

    (dproj,) = _rowwise(gather_dproj, [t for g in dqkv for t in g] + [du], [], [(IN_WIDTH, BF16)], [], tm,
                        "gather_dproj")
    dh1 = _matmul(dproj, wts["w_in"], "nt", F32, 512, 1024, 2816, "in_proj_bwd")
    dw_in = _matmul(h1, dproj, "tn", F32, 1024, 1408, 512, "in_proj_wgrad")

    def pre_mix_bwd(r, c):
        x_v, dh1_v, dxa_v = r
        _, vjp = jax.vjp(_mod_norm, x_v, *c)
        dx, dg, dsc, dsh = vjp(dh1_v)
        return [dx + dxa_v, dg, dsc, dsh]

    grad_x, dg_pre_mix, dsc1, dsh1 = _rowwise(
        pre_mix_bwd, [x, dh1, dx_a], [small["g_pre_mix"], sc1, sh1], [(D_MODEL, F32)], [d_acc] * 3, tm, "pre_mix_bwd")

    dmod = jnp.concatenate([dsh1, dsc1, dgt1, dsh2, dsc2, dgt2], axis=1)
    big = dict(w_in=dw_in, w_out=dw_out, w_mlp_in=dw_mlp_in, w_mlp_out=dw_mlp_out, w_glu=dw_glu)
    small_g = dict(
        g_pre_mix=dg_pre_mix, g_post_mix=dg_post_mix, ssm_a_re=da_re, ssm_a_im=da_im,
        ssm_log_dt=dlog_dt.reshape(1, SSM_G), ssm_b_re=db_re.reshape(SSM_G, SSM_N, SSM_P),
        ssm_b_im=db_im.reshape(SSM_G, SSM_N, SSM_P), ssm_c_re=_block_diag_out_grad(dcblk_re),
        ssm_c_im=_block_diag_out_grad(dcblk_im), ssm_d=dd_row.reshape(SSM_G, SSM_P), b_glu=db_glu,
        g_attn_out=dg_attn_out, g_ssm_out=dg_ssm_out, g_pre_mlp=dg_pre_mlp, g_post_mlp=dg_post_mlp)
    return loss, grad_x, dmod, big, small_g


MESH_ID = pl.DeviceIdType.MESH
N_DEV = 8
N_CHIPS = 4
HBM_SPEC = pl.BlockSpec(memory_space=pltpu.HBM)


def _place():
    x, y, c = lax.axis_index("x"), lax.axis_index("y"), lax.axis_index("c")
    other_chips = [(1 - x, y), (x, 1 - y), (1 - x, 1 - y)]
    return x, y, c, other_chips


def _half_rows(index, half):
    return pl.ds(pl.multiple_of(index * half, ROW_PAD), half)


def _remote(src, dst, send_sem, recv_sem, dev):
    return pltpu.make_async_remote_copy(src_ref=src, dst_ref=dst, send_sem=send_sem, recv_sem=recv_sem,
                                        device_id=dev, device_id_type=MESH_ID)


def _all_gather8(block, name):
    m_per, n = block.shape

    def body(x_ref, out_ref, send_sems, recv_sems, local_sem):
        x, y, c, chips = _place()
        me, sibling = (x, y, c), (x, y, 1 - c)

        def rows(px, py, pc):
            return out_ref.at[pl.ds((4 * px + 2 * py + pc) * m_per, m_per), :]

        def copy(k, blk, to, src=None):
            return _remote(rows(*blk) if src is None else src, rows(*blk), send_sems.at[k], recv_sems.at[k], to)

        mine = pltpu.make_async_copy(x_ref, rows(*me), local_sem)
        mine.start()
        first = [copy(0, me, sibling, src=x_ref)]
        first += [copy(1 + j, me, (*chip, c), src=x_ref) for j, chip in enumerate(chips)]
        for cp in first:
            cp.start()
        passed = [copy(4 + j, (*chip, c), sibling) for j, chip in enumerate(chips)]
        for j, chip in enumerate(chips):
            copy(1 + j, (*chip, c), me).wait_recv()
            passed[j].start()
        copy(0, sibling, me).wait_recv()
        for j, chip in enumerate(chips):
            copy(4 + j, (*chip, 1 - c), me).wait_recv()
        for cp in first + passed:
            cp.wait_send()
        mine.wait()

    return pl.pallas_call(
        body,
        name=name,
        out_shape=jax.ShapeDtypeStruct((N_DEV * m_per, n), block.dtype),
        in_specs=[pl.BlockSpec(memory_space=pltpu.VMEM)],
        out_specs=pl.BlockSpec(memory_space=pltpu.VMEM),
        scratch_shapes=[pltpu.SemaphoreType.DMA((7,)), pltpu.SemaphoreType.DMA((7,)), pltpu.SemaphoreType.DMA],
        compiler_params=_params(),
    )(block)


def _weight_gather(shards, name):
    n = len(shards)
    shapes = [s.shape for s in shards]

    def body(*refs):
        ins, outs = refs[:n], refs[n:2 * n]
        send, recv, fsend, frecv = refs[2 * n:]
        x, y, c, chips = _place()
        k_me = 2 * x + y
        sibling = (x, y, 1 - c)
        pending = []
        for a in range(n):
            half = shapes[a][0] // 2
            mine = _half_rows(c, half)
            for j, chip in enumerate(chips):
                cp = _remote(ins[a].at[mine, :], outs[a].at[k_me, mine, :], send.at[a, j], recv.at[a, j], (*chip, c))
                cp.start()
                pending.append(cp.wait_send)
        for a in range(n):
            half = shapes[a][0] // 2
            for j, (px, py) in enumerate(chips):
                piece = outs[a].at[2 * px + py, _half_rows(c, half), :]
                _remote(piece, piece, send.at[a, j], recv.at[a, j], (px, py, c)).wait_recv()
                fw = _remote(piece, piece, fsend.at[a, j], frecv.at[a, j], sibling)
                fw.start()
                pending.append(fw.wait_send)
        for a in range(n):
            half = shapes[a][0] // 2
            for j, (px, py) in enumerate(chips):
                piece = outs[a].at[2 * px + py, _half_rows(1 - c, half), :]
                _remote(piece, piece, fsend.at[a, j], frecv.at[a, j], sibling).wait_recv()
        for wait in pending:
            wait()

    sems = pltpu.SemaphoreType.DMA((n, 3))
    return pl.pallas_call(
        body,
        name=name,
        out_shape=[jax.ShapeDtypeStruct((N_CHIPS,) + s, BF16) for s in shapes],
        in_specs=[HBM_SPEC] * n,
        out_specs=[HBM_SPEC] * n,
        scratch_shapes=[sems, sems, sems, sems],
        compiler_params=_params(),
    )(*shards)


def _sibling_halves(stacks, name):
    n = len(stacks)
    shapes = [s.shape for s in stacks]

    def body(*refs):
        ins, outs = refs[:n], refs[n:2 * n]
        send, recv = refs[2 * n:]
        x, y, c, _ = _place()
        copies = []
        for a in range(n):
            half = shapes[a][1] // 2
            cp = _remote(ins[a].at[:, _half_rows(1 - c, half), :], outs[a], send.at[a], recv.at[a], (x, y, 1 - c))
            cp.start()
            copies.append(cp)
        for cp in copies:
            cp.wait()

    return pl.pallas_call(
        body,
        name=name,
        out_shape=[jax.ShapeDtypeStruct((N_CHIPS, s[1] // 2, s[2]), F32) for s in shapes],
        in_specs=[HBM_SPEC] * n,
        out_specs=[HBM_SPEC] * n,
        scratch_shapes=[pltpu.SemaphoreType.DMA((n,)), pltpu.SemaphoreType.DMA((n,))],
        compiler_params=_params(),
    )(*stacks)


def _chip_exchange(parts, name):
    n = len(parts)
    shapes = [p.shape for p in parts]

    def body(*refs):
        ins, outs = refs[:n], refs[n:2 * n]
        send, recv = refs[2 * n:]
        x, y, c, chips = _place()
        copies = []
        for a in range(n):
            for j, (px, py) in enumerate(chips):
                cp = _remote(ins[a].at[2 * px + py], outs[a].at[j], send.at[a, j], recv.at[a, j], (px, py, c))
                cp.start()
                copies.append(cp)
        for cp in copies:
            cp.wait()

    return pl.pallas_call(
        body,
        name=name,
        out_shape=[jax.ShapeDtypeStruct((3,) + s[1:], BF16) for s in shapes],
        in_specs=[HBM_SPEC] * n,
        out_specs=[HBM_SPEC] * n,
        scratch_shapes=[pltpu.SemaphoreType.DMA((n, 3)), pltpu.SemaphoreType.DMA((n, 3))],
        compiler_params=_params(),
    )(*parts)


def _sibling_swap(halves, name):
    n = len(halves)
    shapes = [h.shape for h in halves]

    def body(*refs):
        ins, outs = refs[:n], refs[n:2 * n]
        send, recv = refs[2 * n:]
        x, y, c, _ = _place()
        pending = []
        for a in range(n):
            half = shapes[a][0]
            mine = outs[a].at[_half_rows(c, half), :]
            cp = _remote(ins[a], mine, send.at[a], recv.at[a], (x, y, 1 - c))
            cp.start()
            pending.append(cp.wait_send)
        for a in range(n):
            half = shapes[a][0]
            theirs = outs[a].at[_half_rows(1 - c, half), :]
            _remote(theirs, theirs, send.at[a], recv.at[a], (x, y, 1 - c)).wait_recv()
        for wait in pending:
            wait()

    return pl.pallas_call(
        body,
        name=name,
        out_shape=[jax.ShapeDtypeStruct((2 * s[0], s[1]), F32) for s in shapes],
        in_specs=[HBM_SPEC] * n,
        out_specs=[HBM_SPEC] * n,
        scratch_shapes=[pltpu.SemaphoreType.DMA((n,)), pltpu.SemaphoreType.DMA((n,))],
        compiler_params=_params(),
    )(*halves)


SEM_SPEC = pl.BlockSpec(memory_space=pltpu.SEMAPHORE)
ANY_SPEC = pl.BlockSpec(memory_space=pl.ANY)
DATAFLOW = pltpu.SideEffectType.DATAFLOW_SIDE_EFFECTING


def _split_copy_start(srcs, lands, plan, n_sems, name, after=None):
    bufs = list(srcs) + list(lands)
    ns, nb = len(srcs), len(bufs)
    extra = [] if after is None else [after]

    def body(*refs):
        outs = refs[nb + len(extra):]
        for outgoing, _ in plan(refs[:ns], refs[ns:nb], outs[0], outs[1]):
            outgoing.start()
        outs[-1][...] = jnp.zeros_like(outs[-1])

    sems = pltpu.SemaphoreType.DMA((n_sems,))
    return pl.pallas_call(
        body,
        name=name,
        out_shape=(sems, sems, *[pltpu.HBM(b.shape, b.dtype) for b in bufs], jax.ShapeDtypeStruct((8, LANES), F32)),
        in_specs=[HBM_SPEC] * nb + [ANY_SPEC] * len(extra),
        out_specs=(SEM_SPEC, SEM_SPEC, *[HBM_SPEC] * nb, pl.BlockSpec(memory_space=pltpu.VMEM)),
        input_output_aliases={i: 2 + i for i in range(nb)},
        compiler_params=pltpu.CompilerParams(has_side_effects=DATAFLOW),
    )(*[pltpu.with_memory_space_constraint(b, pltpu.HBM) for b in bufs], *extra)


def _split_copy_wait(started, plan, after, name):
    send, recv, *bufs = started[:-1]
    nb = len(bufs)
    ns = nb // 2

    def body(*refs):
        for outgoing, incoming in plan(refs[:ns], refs[ns:nb], refs[nb], refs[nb + 1]):
            outgoing.wait_send()
            incoming.wait_recv()

    return pl.pallas_call(
        body,
        name=name,
        out_shape=tuple(pltpu.HBM(b.shape, b.dtype) for b in bufs),
        in_specs=[HBM_SPEC] * nb + [SEM_SPEC, SEM_SPEC, ANY_SPEC],
        out_specs=tuple([HBM_SPEC] * nb),
        input_output_aliases={i: i for i in range(nb)},
        compiler_params=pltpu.CompilerParams(has_side_effects=DATAFLOW),
    )(*bufs, send, recv, after)


def _weight_plan(shapes):
    def plan(srcs, lands, send, recv):
        x, y, c, chips = _place()
        copies = []
        for a in range(len(shapes)):
            mine = _half_rows(c, shapes[a][0] // 2)
            for j, (px, py) in enumerate(chips):
                s = 3 * a + j
                arrival = lands[a].at[2 * px + py, mine, :]
                copies.append((_remote(srcs[a].at[mine, :], lands[a].at[2 * x + y, mine, :], send.at[s], recv.at[s], (px, py, c)),
                               _remote(arrival, arrival, send.at[s], recv.at[s], (px, py, c))))
        return copies
    return plan


def _halves_plan(shapes):
    def plan(srcs, lands, send, recv):
        x, y, c, _ = _place()
        copies = []
        for a in range(len(shapes)):
            theirs = srcs[a].at[:, _half_rows(1 - c, shapes[a][1] // 2), :]
            copies.append((_remote(theirs, lands[a], send.at[a], recv.at[a], (x, y, 1 - c)),
                           _remote(lands[a], lands[a], send.at[a], recv.at[a], (x, y, 1 - c))))
        return copies
    return plan


def _exchange_plan(n):
    def plan(srcs, lands, send, recv):
        x, y, c, chips = _place()
        copies = []
        for a in range(n):
            for j, (px, py) in enumerate(chips):
                s = 3 * a + j
                copies.append((_remote(srcs[a].at[2 * px + py], lands[a].at[j], send.at[s], recv.at[s], (px, py, c)),
                               _remote(lands[a].at[j], lands[a].at[j], send.at[s], recv.at[s], (px, py, c))))
        return copies
    return plan


def _forward_to_sibling(stacks, name):
    n = len(stacks)
    shapes = [s.shape for s in stacks]

    def body(*refs):
        ins, outs = refs[:n], refs[n:2 * n]
        send, recv = refs[2 * n:]
        x, y, c, chips = _place()
        sibling = (x, y, 1 - c)
        copies = []
        for a in range(n):
            half = shapes[a][1] // 2
            for j, (px, py) in enumerate(chips):
                rows = _half_rows(c, half)
                cp = _remote(ins[a].at[2 * px + py, rows, :], outs[a].at[2 * px + py, rows, :], send.at[a, j],
                             recv.at[a, j], sibling)
                cp.start()
                copies.append(cp)
        for a in range(n):
            half = shapes[a][1] // 2
            for j, (px, py) in enumerate(chips):
                theirs = outs[a].at[2 * px + py, _half_rows(1 - c, half), :]
                _remote(theirs, theirs, send.at[a, j], recv.at[a, j], sibling).wait_recv()
        for cp in copies:
            cp.wait_send()

    return pl.pallas_call(
        body,
        name=name,
        out_shape=[jax.ShapeDtypeStruct(s, BF16) for s in shapes],
        in_specs=[HBM_SPEC] * n,
        out_specs=[HBM_SPEC] * n,
        input_output_aliases={a: a for a in range(n)},
        scratch_shapes=[pltpu.SemaphoreType.DMA((n, 3)), pltpu.SemaphoreType.DMA((n, 3))],
        compiler_params=_params(),
    )(*stacks)


def _tie(x, token):
    return x + token[0:1, 0:1].astype(x.dtype)


ROW_PAD = 16


def _silu(x):
    return x * _sigmoid(x)


def _ada_fwd(c_all, w_ada, b_ada, name):
    dm, cols = w_ada.shape
    tn = 512

    def body(c_ref, w_ref, b_ref, o_ref):
        o_ref[...] = _nn(_silu(c_ref[...]).astype(BF16), w_ref[...].astype(BF16)) + b_ref[...]

    return pl.pallas_call(
        body,
        name=name,
        grid=(cols // tn,),
        in_specs=[pl.BlockSpec((ROW_PAD, dm), lambda j: (0, 0)), pl.BlockSpec((dm, tn), lambda j: (0, j)),
                  pl.BlockSpec((1, tn), lambda j: (0, j))],
        out_specs=pl.BlockSpec((ROW_PAD, tn), lambda j: (0, j)),
        out_shape=jax.ShapeDtypeStruct((ROW_PAD, cols), F32),
        compiler_params=_params(("parallel",)),
    )(c_all, w_ada, b_ada)


def _adamw(w, g, m, v):
    m = ADAM_B1 * m + (1.0 - ADAM_B1) * g
    v = ADAM_B2 * v + (1.0 - ADAM_B2) * (g * g)
    m_hat = m / (1.0 - ADAM_B1 ** ADAM_STEP)
    v_hat = v / (1.0 - ADAM_B2 ** ADAM_STEP)
    delta = -ADAM_LR * (m_hat / (jnp.sqrt(v_hat) + ADAM_EPS) + ADAM_WD * w)
    return delta, m, v


def _ada_bwd_adamw(c_all, dmod_cols, w, m, v, name):
    dm, cols = w.shape
    tm, tn = 512, 512

    def body(c_ref, d_ref, w_ref, m_ref, v_ref, g_ref, dl_ref, nm_ref, nv_ref):
        g = _tn(_silu(c_ref[...]).astype(BF16), d_ref[...].astype(BF16))
        g_ref[...] = g
        dl_ref[...], nm_ref[...], nv_ref[...] = _adamw(w_ref[...], g, m_ref[...], v_ref[...])

    tile = pl.BlockSpec((tm, tn), lambda i, j: (i, j))
    shape = jax.ShapeDtypeStruct((dm, cols), F32)
    return pl.pallas_call(
        body,
        name=name,
        grid=(dm // tm, cols // tn),
        in_specs=[pl.BlockSpec((ROW_PAD, tm), lambda i, j: (0, i)), pl.BlockSpec((ROW_PAD, tn), lambda i, j: (0, j)),
                  tile, tile, tile],
        out_specs=[tile] * 4,
        out_shape=[shape] * 4,
        compiler_params=_params(("parallel", "parallel")),
    )(c_all, dmod_cols, w, m, v)


def _sum_blocks(parts, nblk, name):
    rows, cols = parts.shape[0] // nblk, parts.shape[1]

    def body(p_ref, o_ref):
        tot = p_ref[0:rows, :]
        for b in range(1, nblk):
            tot = tot + p_ref[b * rows:(b + 1) * rows, :]
        o_ref[...] = tot

    return pl.pallas_call(body, name=name, out_shape=jax.ShapeDtypeStruct((rows, cols), F32), compiler_params=_params())(parts)


def _adamw_rows(w, g, m, v, tm, name):
    return _rowwise(lambda r, c: list(_adamw(*r)), [w, g, m, v], [], [(w.shape[1], F32)] * 3, [], tm, name)


BIG = ("w_in", "w_out", "w_mlp_in", "w_mlp_out", "w_glu")
COL_SHARDED = ("w_in", "w_out", "w_mlp_in")
SMALL = ("b_ada", "g_pre_mix", "g_post_mix", "ssm_a_re", "ssm_a_im", "ssm_log_dt", "ssm_b_re", "ssm_b_im",
         "ssm_c_re", "ssm_c_im", "ssm_d", "b_glu", "g_attn_out", "g_ssm_out", "g_pre_mlp", "g_post_mlp")
WEIGHTS = ("w_ada", "b_ada", "g_pre_mix", "g_post_mix", "w_in", "ssm_a_re", "ssm_a_im", "ssm_log_dt", "ssm_b_re",
           "ssm_b_im", "ssm_c_re", "ssm_c_im", "ssm_d", "w_glu", "b_glu", "g_attn_out", "g_ssm_out", "w_out",
           "g_pre_mlp", "g_post_mlp", "w_mlp_in", "w_mlp_out")
FLAT_COLS = 1024
FLAT_ROWS = 256
ROW_TILE = {"w_in": 256, "w_out": 128, "w_mlp_in": 256, "w_mlp_out": 256, "w_glu": 112}


def _flatten_small(tree):
    flat = jnp.concatenate([tree[k].reshape(-1) for k in SMALL])
    return jnp.pad(flat, (0, FLAT_ROWS * FLAT_COLS - flat.shape[0])).reshape(FLAT_ROWS, FLAT_COLS)


def _unflatten_small(flat, like):
    flat = flat.reshape(-1)
    out, at = {}, 0
    for k in SMALL:
        size = math.prod(like[k].shape)
        out[k] = flat[at:at + size].reshape(like[k].shape)
        at += size
    return out


def _unstack(stack, name):
    if name in COL_SHARDED:
        return stack.transpose(1, 0, 2).reshape(stack.shape[1], N_CHIPS * stack.shape[2])
    return stack.reshape(N_CHIPS * stack.shape[1], stack.shape[2])


def _stack(full, name):
    if name in COL_SHARDED:
        return full.reshape(full.shape[0], N_CHIPS, full.shape[1] // N_CHIPS).transpose(1, 0, 2)
    return full.reshape(N_CHIPS, full.shape[0] // N_CHIPS, full.shape[1])


EARLY = ("w_in", "w_out", "w_glu")
LATE = ("w_mlp_in", "w_mlp_out")


def _chip_sums(names, g_stacks, from_sibling, ic, chip):
    own, to_send = [], []
    place = jnp.stack([ic, chip]).astype(jnp.int32)
    for k, gs, fs in zip(names, g_stacks, from_sibling):
        _, rows, cols = gs.shape
        half, tm = rows // 2, ROW_TILE[k]
        nt = half // tm

        def body(place_ref, g_ref, f_ref, own_ref, send_ref):
            s = g_ref[...] + f_ref[...]
            send_ref[...] = s.astype(BF16)

            @pl.when(pl.program_id(1) == place_ref[1])
            def _():
                own_ref[...] = s

        slab = lambda index: pl.BlockSpec((None, tm, cols), index)
        mine, to_chips = pl.pallas_call(
            body,
            name="grad_chip_sum_" + k,
            grid_spec=pltpu.PrefetchScalarGridSpec(
                num_scalar_prefetch=1,
                grid=(nt, N_CHIPS),
                in_specs=[slab(lambda i, kk, p, nt=nt: (kk, p[0] * nt + i, 0)), slab(lambda i, kk, p: (kk, i, 0))],
                out_specs=[pl.BlockSpec((tm, cols), lambda i, kk, p: (i, 0)), slab(lambda i, kk, p: (kk, i, 0))]),
            out_shape=[jax.ShapeDtypeStruct((half, cols), F32), jax.ShapeDtypeStruct((N_CHIPS, half, cols), BF16)],
            compiler_params=_params(("arbitrary", "arbitrary")),
        )(place, gs, fs)
        own.append(mine)
        to_send.append(to_chips)
    return own, to_send


def _grad_totals(names, own, from_chips):
    totals = []
    for k, mine, fc in zip(names, own, from_chips):
        half, cols = mine.shape
        tm = ROW_TILE[k]

        def body(m_ref, a_ref, b_ref, c_ref, o_ref):
            o_ref[...] = m_ref[...] + a_ref[...].astype(F32) + b_ref[...].astype(F32) + c_ref[...].astype(F32)

        rows = pl.BlockSpec((tm, cols), lambda i: (i, 0))
        totals.append(pl.pallas_call(
            body,
            name="grad_total_" + k,
            grid=(half // tm,),
            in_specs=[rows] + [pl.BlockSpec((None, tm, cols), lambda i, j=j: (j, i, 0)) for j in range(3)],
            out_specs=rows,
            out_shape=jax.ShapeDtypeStruct((half, cols), F32),
            compiler_params=_params(("parallel",)),
        )(mine, fc, fc, fc))
    return totals


class _Overlap:
    def __init__(self, own_shards, ic, chip, after):
        self.ic, self.chip = ic, chip
        self.shapes = [o.shape for o in own_shards]
        lands = [lax.empty((N_CHIPS,) + s, BF16) for s in self.shapes]
        self.gather = _split_copy_start(own_shards, lands, _weight_plan(self.shapes), 3 * len(LATE),
                                        "mlp_weight_gather_start", after=after)
        self.token = self.gather[-1]

    def mlp_weights(self, after):
        n = len(LATE)
        done = _split_copy_wait(self.gather, _weight_plan(self.shapes), after, "mlp_weight_gather_wait")
        own, stacks = done[:n], done[n:]
        stacks = _forward_to_sibling(stacks, "mlp_weight_forward")
        stacks = [lax.dynamic_update_index_in_dim(s, o, self.chip, 0) for s, o in zip(stacks, own)]
        return [_unstack(s, k) for k, s in zip(LATE, stacks)]

    def mlp_grads_to_sibling(self, dw_in, dw_out):
        stacks = [dw_in, dw_out]
        self.g_shapes = [s.shape for s in stacks]
        lands = [lax.empty((N_CHIPS, s[1] // 2, s[2]), F32) for s in self.g_shapes]
        self.halves = _split_copy_start(stacks, lands, _halves_plan(self.g_shapes), len(LATE), "mlp_grad_halves_start")
        return self.halves[-1]

    def mlp_grads_to_chips(self, after):
        n = len(LATE)
        done = _split_copy_wait(self.halves, _halves_plan(self.g_shapes), after, "mlp_grad_halves_wait")
        self.own, to_send = _chip_sums(LATE, done[:n], done[n:], self.ic, self.chip)
        lands = [lax.empty((3,) + s.shape[1:], BF16) for s in to_send]
        self.exchange = _split_copy_start(to_send, lands, _exchange_plan(n), 3 * n, "mlp_grad_exchange_start")
        return self.exchange[-1]

    def mlp_grads_reduced(self, after):
        n = len(LATE)
        done = _split_copy_wait(self.exchange, _exchange_plan(n), after, "mlp_grad_exchange_wait")
        return _grad_totals(LATE, self.own, done[n:])


def _pad_rows(row):
    return jnp.pad(row, ((0, 8 - row.shape[0]), (0, 0)))


def _every_eighth(gathered):
    rows = gathered.reshape(N_DEV, 8, gathered.shape[1])[:, 0, :]
    return jnp.pad(rows, ((0, ROW_PAD - N_DEV), (0, 0)))


def kernel(x, c, positions, w_ada, b_ada, g_pre_mix, g_post_mix, w_in, ssm_a_re, ssm_a_im, ssm_log_dt, ssm_b_re, ssm_b_im, ssm_c_re, ssm_c_im, ssm_d, w_glu, b_glu, g_attn_out, g_ssm_out, w_out, g_pre_mlp, g_post_mlp, w_mlp_in, w_mlp_out, loss_target, m_w_ada, m_b_ada, m_g_pre_mix, m_g_post_mix, m_w_in, m_ssm_a_re, m_ssm_a_im, m_ssm_log_dt, m_ssm_b_re, m_ssm_b_im, m_ssm_c_re, m_ssm_c_im, m_ssm_d, m_w_glu, m_b_glu, m_g_attn_out, m_g_ssm_out, m_w_out, m_g_pre_mlp, m_g_post_mlp, m_w_mlp_in, m_w_mlp_out, v_w_ada, v_b_ada, v_g_pre_mix, v_g_post_mix, v_w_in, v_ssm_a_re, v_ssm_a_im, v_ssm_log_dt, v_ssm_b_re, v_ssm_b_im, v_ssm_c_re, v_ssm_c_im, v_ssm_d, v_w_glu, v_b_glu, v_g_attn_out, v_g_ssm_out, v_w_out, v_g_pre_mlp, v_g_post_mlp, v_w_mlp_in, v_w_mlp_out):
    given = dict(locals())
    w = {k: given[k][0] for k in WEIGHTS}
    mom = {k: given["m_" + k][0] for k in WEIGHTS}
    var = {k: given["v_" + k][0] for k in WEIGHTS}
    for tree in (w, mom, var):
        for k in ("b_ada", "g_pre_mix", "g_post_mix", "ssm_log_dt", "b_glu", "g_attn_out", "g_ssm_out", "g_pre_mlp",
                  "g_post_mlp"):
            tree[k] = tree[k].reshape(1, -1)
    ix, iy, ic = lax.axis_index("x"), lax.axis_index("y"), lax.axis_index("c")
    chip = 2 * ix + iy
    me = 4 * ix + 2 * iy + ic
    shard_cols = w["w_ada"].shape[1]

    c_all = _every_eighth(_all_gather8(_pad_rows(c), "gather_c"))
    b_ada_cols = lax.dynamic_slice_in_dim(w["b_ada"], chip * shard_cols, shard_cols, axis=1)
    mod_cols = _ada_fwd(c_all, w["w_ada"], b_ada_cols, "ada_fwd")[:N_DEV]
    mod_all = _all_gather8(mod_cols, "gather_mod").reshape(N_CHIPS, 2, N_DEV, shard_cols)[:, 0]
    mod = lax.dynamic_index_in_dim(mod_all, me, axis=1, keepdims=False).reshape(1, N_MOD * D_MODEL)

    early_own = [w[k].astype(BF16) for k in EARLY]
    stacks = _weight_gather(early_own, "weight_gather")
    stacks = [lax.dynamic_update_index_in_dim(s, o, chip, 0) for s, o in zip(stacks, early_own)]
    wts = {k: _unstack(s, k) for k, s in zip(EARLY, stacks)}
    overlap = _Overlap([w[k].astype(BF16) for k in LATE], ic, chip, after=stacks[0])
    mod = _tie(mod, overlap.token)

    small = {k: w[k] for k in SMALL if k != "b_ada"}
    loss, grad_x, dmod, big_g, small_g = _local_step(x[0], positions.reshape(-1, 1), mod, loss_target[0], wts, small,
                                                     hooks=overlap)
    loss = lax.psum(loss[0, 0], ("x", "y", "c"))

    g_stacks = [_stack(big_g[k], k) for k in EARLY]
    from_sibling = _sibling_halves(g_stacks, "grad_sibling_halves")
    chip_f32, chip_bf16 = _chip_sums(EARLY, g_stacks, from_sibling, ic, chip)
    from_chips = _chip_exchange(chip_bf16, "grad_chip_exchange")
    reduced = _grad_totals(EARLY, chip_f32, from_chips) + overlap.mlp_grads_reduced(big_g["w_in"])
    swapped = _sibling_swap(reduced, "grad_sibling_swap")
    grads = {k: lax.dynamic_update_slice_in_dim(s, r, ic * r.shape[0], axis=0)
             for k, s, r in zip(EARLY + LATE, swapped, reduced)}

    small_g["b_ada"] = dmod
    parts = _all_gather8(_flatten_small(small_g), "gather_small_grads")

    small_flat = _sum_blocks(parts, N_DEV, "small_grad_sum")
    grads.update(_unflatten_small(small_flat, w))

    dmod_all = _every_eighth(_all_gather8(_pad_rows(dmod), "gather_dmod"))
    dmod_cols = lax.dynamic_slice_in_dim(dmod_all, chip * shard_cols, shard_cols, axis=1)
    g_ada, d_ada, m_ada, v_ada = _ada_bwd_adamw(c_all, dmod_cols, w["w_ada"], mom["w_ada"], var["w_ada"], "ada_bwd_adamw")
    grads["w_ada"] = g_ada

    delta, new_m, new_v = {"w_ada": d_ada}, {"w_ada": m_ada}, {"w_ada": v_ada}
    for k in BIG:
        delta[k], new_m[k], new_v[k] = _adamw_rows(w[k], grads[k], mom[k], var[k], ROW_TILE[k], "adamw_" + k)
    flat_upd = _adamw_rows(_flatten_small(w), small_flat, _flatten_small(mom), _flatten_small(var), FLAT_ROWS,
                           "adamw_small")
    for tree, flat in zip((delta, new_m, new_v), flat_upd):
        tree.update(_unflatten_small(flat, w))

    shaped = lambda tree: [tree[k].reshape(given[k].shape) for k in WEIGHTS]
    return (loss, grad_x[None], *shaped(grads), *shaped(delta), *shaped(new_m), *shaped(new_v))
```

```python
import functools
import math

import jax
import jax.numpy as jnp
import numpy as np
from jax import lax
from jax.experimental import pallas as pl
from jax.experimental.pallas import tpu as pltpu

F32 = jnp.float32
BF16 = jnp.bfloat16

D_MODEL = 2048
HEAD_DIM = 64
DILATIONS = (1, 4, 16)
ATT_SPAN = 128
ATT_BLK = 128
HEADS_PER_GROUP = 6
KV_WIDTH = HEADS_PER_GROUP * HEAD_DIM
ATT_Q_WIDTH = 3 * KV_WIDTH
ROT_DIM = 16
ROPE_THETA = 500000.0
SSM_WIDTH = 896
SSM_P = 16
SSM_G = 56
SSM_N = 64
SSM_GN = SSM_G * SSM_N
SSM_TILES = SSM_WIDTH // 128
SSM_TILE_GN = 8 * SSM_N
IN_WIDTH = 2816
OUT_IN_WIDTH = 1280
D_FF = 8192
N_MOD = 6
EPS = 1e-6
LANES = 128
SSM_SEGS = 8
SSM_CHUNK = 256
SSM_SEG_LEN = SSM_CHUNK // SSM_SEGS

ADAM_LR = 0.001
ADAM_B1 = 0.9
ADAM_B2 = 0.999
ADAM_EPS = 1e-08
ADAM_WD = 0.01
ADAM_STEP = 10

VMEM_LIMIT = 56 * 1024 * 1024


def _params(sem=None):
    return pltpu.CompilerParams(dimension_semantics=sem, vmem_limit_bytes=VMEM_LIMIT)


def _dot(a, b, dims):
    return lax.dot_general(a, b, (dims, ((), ())), preferred_element_type=F32)


def _nn(a, b):
    return _dot(a, b, ((1,), (0,)))


def _nt(a, b):
    return _dot(a, b, ((1,), (1,)))


def _tn(a, b):
    return _dot(a, b, ((0,), (0,)))


def _matmul(a, b, mode, out_dtype, tm, tn, tk, name, after=None):
    if mode == "nn":
        (m, k), (_, n) = a.shape, b.shape
        a_spec = pl.BlockSpec((tm, tk), lambda i, j, kk: (i, kk))
        b_spec = pl.BlockSpec((tk, tn), lambda i, j, kk: (kk, j))
        op = _nn
    elif mode == "nt":
        (m, k), (n, _) = a.shape, b.shape
        a_spec = pl.BlockSpec((tm, tk), lambda i, j, kk: (i, kk))
        b_spec = pl.BlockSpec((tn, tk), lambda i, j, kk: (j, kk))
        op = _nt
    else:
        (k, m), (_, n) = a.shape, b.shape
        a_spec = pl.BlockSpec((tk, tm), lambda i, j, kk: (kk, i))
        b_spec = pl.BlockSpec((tk, tn), lambda i, j, kk: (kk, j))
        op = _tn
    assert m % tm == 0 and n % tn == 0 and k % tk == 0, (name, m, n, k)
    nk = k // tk

    def body(a_ref, b_ref, *rest):
        o_ref, acc_ref = rest[-2:]
        kk = pl.program_id(2)

        @pl.when(kk == 0)
        def _():
            acc_ref[...] = jnp.zeros_like(acc_ref)

        acc_ref[...] += op(a_ref[...], b_ref[...])

        @pl.when(kk == nk - 1)
        def _():
            o_ref[...] = acc_ref[...].astype(o_ref.dtype)

    extra = [] if after is None else [after]
    return pl.pallas_call(
        body,
        name=name,
        grid=(m // tm, n // tn, nk),
        in_specs=[a_spec, b_spec] + [pl.BlockSpec(t.shape, lambda i, j, kk: (0, 0)) for t in extra],
        out_specs=pl.BlockSpec((tm, tn), lambda i, j, kk: (i, j)),
        out_shape=jax.ShapeDtypeStruct((m, n), out_dtype),
        scratch_shapes=[pltpu.VMEM((tm, tn), F32)],
        compiler_params=_params(("parallel", "parallel", "arbitrary")),
    )(a, b, *extra)


def _rowwise(fn, rows, consts, out_rows, out_accs, tm, name):
    n_rows = rows[0].shape[0]
    assert n_rows % tm == 0
    nr, nc, no = len(rows), len(consts), len(out_rows)

    def body(*refs):
        r_in, c_in = refs[:nr], refs[nr:nr + nc]
        o_row, o_acc = refs[nr + nc:nr + nc + no], refs[nr + nc + no:]
        outs = fn([r[...] for r in r_in], [c[...] for c in c_in])
        assert len(outs) == len(o_row) + len(o_acc), name
        for ref, v in zip(o_row, outs[:no]):
            ref[...] = v.astype(ref.dtype)
        first = pl.program_id(0) == 0
        for ref, v in zip(o_acc, outs[no:]):
            @pl.when(first)
            def _(ref=ref, v=v):
                ref[...] = v.astype(F32)

            @pl.when(jnp.logical_not(first))
            def _(ref=ref, v=v):
                ref[...] += v.astype(F32)

    in_specs = [pl.BlockSpec((tm, r.shape[1]), lambda i: (i, 0)) for r in rows]
    in_specs += [pl.BlockSpec(c.shape, lambda i: (0, 0)) for c in consts]
    out_specs = [pl.BlockSpec((tm, w), lambda i: (i, 0)) for w, _ in out_rows]
    out_specs += [pl.BlockSpec(s, lambda i: (0, 0)) for s in out_accs]
    out_shape = [jax.ShapeDtypeStruct((n_rows, w), dt) for w, dt in out_rows]
    out_shape += [jax.ShapeDtypeStruct(s, F32) for s in out_accs]
    return pl.pallas_call(
        body,
        name=name,
        grid=(n_rows // tm,),
        in_specs=in_specs,
        out_specs=out_specs,
        out_shape=out_shape,
        compiler_params=_params(("arbitrary",)),
    )(*rows, *consts)


def _rms(x, g):
    return x * lax.rsqrt(jnp.mean(x * x, axis=-1, keepdims=True) + EPS) * g


def _mod_norm(x, g, sc, sh):
    return _rms(x, g) * (1.0 + sc) + sh


def _gelu(x):
    return 0.5 * x * (1.0 + jnp.tanh(math.sqrt(2.0 / math.pi) * (x + 0.044715 * (x * x * x))))


def _sigmoid(x):
    return 1.0 / (1.0 + jnp.exp(-x))


def _post_mix(x, mix, g_post, gt1, g_pre, sc2, sh2):
    x1 = x + gt1 * _rms(mix, g_post)
    return x1, _mod_norm(x1, g_pre, sc2, sh2)


def _att_mix(o0, o1, o2, l0, l1, l2, g):
    m = jnp.maximum(jnp.maximum(l0, l1), l2)
    e0, e1, e2 = jnp.exp(l0 - m), jnp.exp(l1 - m), jnp.exp(l2 - m)
    att = (e0 * o0 + e1 * o1 + e2 * o2) / (e0 + e1 + e2)
    return _rms(att, g)


def _glu_out(y2, z, g):
    return _rms(y2 * _sigmoid(z), g)


def _rope_tables(pos_col, freq_lane, name):
    n_rows = pos_col.shape[0]
    tm = 512

    def body(p_ref, f_ref, cos_ref, lo_ref, hi_ref):
        ang = p_ref[...].astype(F32) * f_ref[...]
        lane = lax.broadcasted_iota(jnp.int32, ang.shape, 1) % HEAD_DIM
        c, s = jnp.cos(ang), jnp.sin(ang)
        cos_ref[...] = jnp.where(lane < ROT_DIM, c, 1.0)
        lo_ref[...] = jnp.where(lane < ROT_DIM // 2, -s, 0.0)
        hi_ref[...] = jnp.where((lane >= ROT_DIM // 2) & (lane < ROT_DIM), s, 0.0)

    tab = jax.ShapeDtypeStruct((n_rows, LANES), F32)
    return pl.pallas_call(
        body,
        name=name,
        grid=(n_rows // tm,),
        in_specs=[pl.BlockSpec((tm, 1), lambda i: (i, 0)), pl.BlockSpec((1, LANES), lambda i: (0, 0))],
        out_specs=[pl.BlockSpec((tm, LANES), lambda i: (i, 0))] * 3,
        out_shape=[tab] * 3,
        compiler_params=_params(("parallel",)),
    )(pos_col, freq_lane)


def _rope(x, cos_t, lo_t, hi_t):
    half = ROT_DIM // 2
    return x * cos_t + pltpu.roll(x, LANES - half, 1) * lo_t + pltpu.roll(x, half, 1) * hi_t


def _rope_transposed(dy, cos_t, lo_t, hi_t):
    half = ROT_DIM // 2
    return dy * cos_t + pltpu.roll(dy * lo_t, half, 1) + pltpu.roll(dy * hi_t, LANES - half, 1)


def _att_masks(i, k0):
    q_pos = i * ATT_BLK + lax.broadcasted_iota(jnp.int32, (ATT_BLK, 2 * ATT_BLK), 0)
    k_pos = k0 + lax.broadcasted_iota(jnp.int32, (ATT_BLK, 2 * ATT_BLK), 1)
    dist = q_pos - k_pos
    return (dist >= 0) & (dist <= ATT_SPAN)


def _head_lane_masks():
    lane = lax.broadcasted_iota(jnp.int32, (1, LANES), 1)
    return lane < HEAD_DIM, lane >= HEAD_DIM


def _att_specs(gi, d, n):
    cols = IN_WIDTH // LANES
    qkv = [
        pl.BlockSpec((n, LANES), lambda r, hp: (0, r * cols + gi * 3 + hp)),
        pl.BlockSpec((n, LANES), lambda r, hp: (0, r * cols + 9 + hp)),
        pl.BlockSpec((n, LANES), lambda r, hp: (0, r * cols + 12 + hp)),
    ]
    tabs = [pl.BlockSpec((n, LANES), lambda r, hp: (0, r))] * 3
    head = pl.BlockSpec((n, LANES), lambda r, hp: (0, r * 3 + hp))
    return qkv, tabs, head


def _att_load(q_ref, k_ref, v_ref, cos_ref, lo_ref, hi_ref, qs, ks, vs):
    cos_t, lo_t, hi_t = cos_ref[...], lo_ref[...], hi_ref[...]
    qs[...] = (_rope(q_ref[...], cos_t, lo_t, hi_t) * (1.0 / math.sqrt(HEAD_DIM))).astype(BF16)
    ks[...] = _rope(k_ref[...], cos_t, lo_t, hi_t).astype(BF16)
    vs[...] = v_ref[...].astype(BF16)


def _att_fwd(proj, tabs, gi, name):
    n_rows = proj.shape[0]
    d = DILATIONS[gi]
    n = n_rows // d
    nb = n // ATT_BLK
    proj_v = proj.reshape(n, d * IN_WIDTH)
    tabs_v = [t.reshape(n, d * LANES) for t in tabs]

    def body(q_ref, k_ref, v_ref, cos_ref, lo_ref, hi_ref, o_ref, l_ref, qs, ks, vs):
        _att_load(q_ref, k_ref, v_ref, cos_ref, lo_ref, hi_ref, qs, ks, vs)
        m0, m1 = _head_lane_masks()

        def step(i, carry):
            k0 = pl.multiple_of(jnp.maximum(i - 1, 0) * ATT_BLK, ATT_BLK)
            q0 = pl.multiple_of(i * ATT_BLK, ATT_BLK)
            q = qs[pl.ds(q0, ATT_BLK), :]
            k = ks[pl.ds(k0, 2 * ATT_BLK), :]
            v = vs[pl.ds(k0, 2 * ATT_BLK), :]
            valid = _att_masks(i, k0)
            outs, lses = [], []
            for hm in (m0, m1):
                s = _nt(jnp.where(hm, q, jnp.zeros_like(q)), k)
                s = jnp.where(valid, s, -1e30)
                mx = jnp.max(s, axis=1, keepdims=True)
                p = jnp.exp(s - mx)
                den = jnp.sum(p, axis=1, keepdims=True)
                outs.append(_nn(p.astype(BF16), v) / den)
                lses.append(mx + jnp.log(den))
            o_ref[pl.ds(q0, ATT_BLK), :] = jnp.where(m0, outs[0], outs[1])
            l_ref[pl.ds(q0, ATT_BLK), :] = jnp.where(m0, lses[0], lses[1])
            return carry

        lax.fori_loop(0, nb, step, 0, unroll=2)

    qkv, tab_specs, head = _att_specs(gi, d, n)
    out = jax.ShapeDtypeStruct((n, d * KV_WIDTH), F32)
    o, l = pl.pallas_call(
        body,
        name=name,
        grid=(d, 3),
        in_specs=qkv + tab_specs,
        out_specs=[head, head],
        out_shape=[out, out],
        scratch_shapes=[pltpu.VMEM((n, LANES), BF16)] * 3,
        compiler_params=_params(("parallel", "parallel")),
    )(proj_v, proj_v, proj_v, *tabs_v)
    return o.reshape(n_rows, KV_WIDTH), l.reshape(n_rows, KV_WIDTH)


def _att_bwd(proj, tabs, o, l, do, dl, gi, name):
    n_rows = proj.shape[0]
    d = DILATIONS[gi]
    n = n_rows // d
    nb = n // ATT_BLK
    proj_v = proj.reshape(n, d * IN_WIDTH)
    tabs_v = [t.reshape(n, d * LANES) for t in tabs]
    heads_v = [t.reshape(n, d * KV_WIDTH) for t in (o, l, do, dl)]

    def body(q_ref, k_ref, v_ref, cos_ref, lo_ref, hi_ref, o_ref, l_ref, do_ref, dl_ref,
             dq_ref, dk_ref, dv_ref, qs, ks, vs, dk_acc, dv_acc):
        _att_load(q_ref, k_ref, v_ref, cos_ref, lo_ref, hi_ref, qs, ks, vs)
        dk_acc[...] = jnp.zeros_like(dk_acc)
        dv_acc[...] = jnp.zeros_like(dv_acc)
        m0, m1 = _head_lane_masks()

        def step(i, carry):
            k0 = pl.multiple_of(jnp.maximum(i - 1, 0) * ATT_BLK, ATT_BLK)
            q0 = pl.multiple_of(i * ATT_BLK, ATT_BLK)
            rows = pl.ds(q0, ATT_BLK)
            keys = pl.ds(k0, 2 * ATT_BLK)
            q, k, v = qs[rows, :], ks[keys, :], vs[keys, :]
            d_o, lse = do_ref[rows, :], l_ref[rows, :]
            o_do = o_ref[rows, :] * d_o
            d_l = dl_ref[rows, :]
            valid = _att_masks(i, k0)
            dq = jnp.zeros((ATT_BLK, LANES), F32)
            dk = jnp.zeros((2 * ATT_BLK, LANES), F32)
            dv = jnp.zeros((2 * ATT_BLK, LANES), F32)
            for hm in (m0, m1):
                qh, kh = jnp.where(hm, q, jnp.zeros_like(q)), jnp.where(hm, k, jnp.zeros_like(k))
                doh = jnp.where(hm, d_o, 0.0).astype(BF16)
                lse_h = jnp.max(jnp.where(hm, lse, -1e30), axis=1, keepdims=True)
                delta = jnp.sum(jnp.where(hm, o_do, 0.0), axis=1, keepdims=True)
                dlse = jnp.sum(jnp.where(hm, d_l, 0.0), axis=1, keepdims=True)
                s = jnp.where(valid, _nt(qh, k), -1e30)
                p = jnp.exp(s - lse_h)
                dv = dv + _tn(p.astype(BF16), doh)
                ds = (p * (_nt(doh, v) - delta + dlse)).astype(BF16)
                dq = dq + _nn(ds, kh)
                dk = dk + _tn(ds, qh)
            cos_t, lo_t, hi_t = cos_ref[rows, :], lo_ref[rows, :], hi_ref[rows, :]
            dq_ref[rows, :] = _rope_transposed(dq * (1.0 / math.sqrt(HEAD_DIM)), cos_t, lo_t, hi_t)
            dk_acc[keys, :] += dk
            dv_acc[keys, :] += dv
            return carry

        lax.fori_loop(0, nb, step, 0, unroll=2)
        dk_ref[...] = _rope_transposed(dk_acc[...], cos_ref[...], lo_ref[...], hi_ref[...])
        dv_ref[...] = dv_acc[...]

    qkv, tab_specs, head = _att_specs(gi, d, n)
    out = jax.ShapeDtypeStruct((n, d * KV_WIDTH), F32)
    res = pl.pallas_call(
        body,
        name=name,
        grid=(d, 3),
        in_specs=qkv + tab_specs + [head] * 4,
        out_specs=[head] * 3,
        out_shape=[out] * 3,
        scratch_shapes=[pltpu.VMEM((n, LANES), BF16)] * 3 + [pltpu.VMEM((n, LANES), F32)] * 2,
        compiler_params=_params(("parallel", "parallel")),
    )(proj_v, proj_v, proj_v, *tabs_v, *heads_v)
    return [t.reshape(n_rows, KV_WIDTH) for t in res]


def _expand_np():
    e = np.zeros((SSM_N, SSM_N * SSM_P), np.float32)
    for nn in range(SSM_N):
        e[nn, nn * SSM_P:(nn + 1) * SSM_P] = 1.0
    return e


def _ssm_prep_math(a_re, a_im, log_dt, b_re, b_im, expand):
    dt = jnp.exp(log_dt)
    mag = jnp.exp(a_re * dt)
    ab_re, ab_im = mag * jnp.cos(a_im * dt), mag * jnp.sin(a_im * dt)
    den = a_re * a_re + a_im * a_im
    num_re, num_im = ab_re - 1.0, ab_im
    co_re = (num_re * a_re + num_im * a_im) / den
    co_im = (num_im * a_re - num_re * a_im) / den
    hi = lax.Precision.HIGHEST
    co_re_x = jnp.dot(co_re, expand, precision=hi, preferred_element_type=F32)
    co_im_x = jnp.dot(co_im, expand, precision=hi, preferred_element_type=F32)
    bb_re = co_re_x * b_re - co_im_x * b_im
    bb_im = co_re_x * b_im + co_im_x * b_re
    return ab_re, ab_im, bb_re, bb_im


def _ssm_prep(a_re, a_im, log_dt, b_re, b_im, expand, name):
    def body(ar, ai, ld, br, bi, ex, o0, o1, o2, o3):
        outs = _ssm_prep_math(ar[...], ai[...], ld[...], br[...], bi[...], ex[...])
        for ref, v in zip((o0, o1, o2, o3), outs):
            ref[...] = v

    gn = jax.ShapeDtypeStruct((SSM_G, SSM_N), F32)
    gnp = jax.ShapeDtypeStruct((SSM_G, SSM_N * SSM_P), F32)
    return pl.pallas_call(body, name=name, out_shape=[gn, gn, gnp, gnp], compiler_params=_params())(
        a_re, a_im, log_dt, b_re, b_im, expand)


def _ssm_prep_bwd(a_re, a_im, log_dt, b_re, b_im, expand, cts, name):
    def body(ar, ai, ld, br, bi, ex, c0, c1, c2, c3, o0, o1, o2, o3, o4):
        ex_v = ex[...]
        _, vjp = jax.vjp(lambda *p: _ssm_prep_math(*p, ex_v), ar[...], ai[...], ld[...], br[...], bi[...])
        for ref, v in zip((o0, o1, o2, o3, o4), vjp((c0[...], c1[...], c2[...], c3[...]))):
            ref[...] = v

    gn = jax.ShapeDtypeStruct((SSM_G, SSM_N), F32)
    gnp = jax.ShapeDtypeStruct((SSM_G, SSM_N * SSM_P), F32)
    g1 = jax.ShapeDtypeStruct((SSM_G, 1), F32)
    return pl.pallas_call(body, name=name, out_shape=[gn, gn, g1, gnp, gnp], compiler_params=_params())(
        a_re, a_im, log_dt, b_re, b_im, expand, *cts)


def _block_diag_in(bb):
    t = bb.reshape(SSM_TILES, 8, SSM_N, SSM_P).transpose(0, 1, 3, 2)
    eye = jnp.eye(8, dtype=bb.dtype)
    return (t[:, :, :, None, :] * eye[None, :, None, :, None]).reshape(SSM_TILES, LANES, SSM_TILE_GN)


def _block_diag_in_grad(dblk):
    t = dblk.reshape(SSM_TILES, 8, SSM_P, 8, SSM_N)
    t = jnp.einsum("tapbn,ab->tapn", t, jnp.eye(8, dtype=dblk.dtype))
    return t.transpose(0, 1, 3, 2).reshape(SSM_G, SSM_N, SSM_P)


def _block_diag_out(cm):
    t = cm.reshape(SSM_TILES, 8, SSM_P, SSM_N).transpose(0, 1, 3, 2)
    eye = jnp.eye(8, dtype=cm.dtype)
    return (t[:, :, :, None, :] * eye[None, :, None, :, None]).reshape(SSM_TILES, SSM_TILE_GN, LANES)


def _block_diag_out_grad(dblk):
    t = dblk.reshape(SSM_TILES, 8, SSM_N, 8, SSM_P)
    t = jnp.einsum("tanbp,ab->tanp", t, jnp.eye(8, dtype=dblk.dtype))
    return t.transpose(0, 1, 3, 2).reshape(SSM_G, SSM_P, SSM_N)


def _cmul_add(a_re, a_im, s_re, s_im, b_re, b_im):
    return a_re * s_re - a_im * s_im + b_re, a_re * s_im + a_im * s_re + b_im


def _lane_tiled(a):
    return a.reshape(a.shape[0], SSM_TILES, LANES).transpose(1, 0, 2)


def _lane_untiled(a):
    return a.transpose(1, 0, 2).reshape(a.shape[1], SSM_WIDTH)


def _ssm_load_rows(src_ref, dst):
    for t in range(SSM_TILES):
        for i in range(SSM_SEG_LEN):
            dst[i * SSM_SEGS:(i + 1) * SSM_SEGS, t * LANES:(t + 1) * LANES] = (
                src_ref[t, pl.ds(i, SSM_SEGS, stride=SSM_SEG_LEN), :])


def _ssm_store_rows(src, dst_ref):
    for t in range(SSM_TILES):
        for i in range(SSM_SEG_LEN):
            dst_ref[t, pl.ds(i, SSM_SEGS, stride=SSM_SEG_LEN), :] = (
                src[i * SSM_SEGS:(i + 1) * SSM_SEGS, t * LANES:(t + 1) * LANES])


def _ssm_powers(ab_re_ref, ab_im_ref, pw_re, pw_im):
    a_re, a_im = ab_re_ref[...], ab_im_ref[...]
    p_re, p_im = a_re, a_im
    for i in range(SSM_SEG_LEN):
        pw_re[i:i + 1, :] = p_re
        pw_im[i:i + 1, :] = p_im
        p_re, p_im = _cmul_add(a_re, a_im, p_re, p_im, 0.0, 0.0)


def _ssm_input_proj(u_s, bblk_re_ref, bblk_im_ref, s_re, s_im):
    for t in range(SSM_TILES):
        ub = u_s[:, t * LANES:(t + 1) * LANES].astype(BF16)
        cols = slice(t * SSM_TILE_GN, (t + 1) * SSM_TILE_GN)
        s_re[:, cols] = _nn(ub, bblk_re_ref[t])
        s_im[:, cols] = _nn(ub, bblk_im_ref[t])


def _ssm_scan(ab_re_ref, ab_im_ref, s_re, s_im, init_re, init_im, conj, reverse):
    sign = -1.0 if conj else 1.0
    for t in range(SSM_TILES):
        cols = slice(t * SSM_TILE_GN, (t + 1) * SSM_TILE_GN)
        a_re = jnp.broadcast_to(ab_re_ref[:, cols], (SSM_SEGS, SSM_TILE_GN))
        a_im = jnp.broadcast_to(ab_im_ref[:, cols], (SSM_SEGS, SSM_TILE_GN)) * sign
        if init_re is None:
            st = (jnp.zeros((SSM_SEGS, SSM_TILE_GN), F32),) * 2
        else:
            st = (init_re[:, cols], init_im[:, cols])

        def step(i, st, cols=cols, a_re=a_re, a_im=a_im):
            idx = (SSM_SEG_LEN - 1 - i) if reverse else i
            rows = pl.ds(pl.multiple_of(idx * SSM_SEGS, SSM_SEGS), SSM_SEGS)
            n_re, n_im = _cmul_add(a_re, a_im, st[0], st[1], s_re[rows, cols], s_im[rows, cols])
            s_re[rows, cols] = n_re
            s_im[rows, cols] = n_im
            return n_re, n_im
        lax.fori_loop(0, SSM_SEG_LEN, step, st, unroll=4)


def _ssm_fixup(pw_re, pw_im, s_re, s_im, cin_re, cin_im, conj, reverse):
    sign = -1.0 if conj else 1.0
    c_re, c_im = cin_re[...], cin_im[...]

    def step(i, c):
        k = (SSM_SEG_LEN - 1 - i) if reverse else i
        rows = pl.ds(pl.multiple_of(i * SSM_SEGS, SSM_SEGS), SSM_SEGS)
        p_re = jnp.broadcast_to(pw_re[pl.ds(k, 1), :], (SSM_SEGS, SSM_GN))
        p_im = jnp.broadcast_to(pw_im[pl.ds(k, 1), :], (SSM_SEGS, SSM_GN)) * sign
        n_re, n_im = _cmul_add(p_re, p_im, c_re, c_im, s_re[rows, :], s_im[rows, :])
        s_re[rows, :] = n_re
        s_im[rows, :] = n_im
        return c
    lax.fori_loop(0, SSM_SEG_LEN, step, 0)


def _ssm_fwd(u, ab_re, ab_im, bblk_re, bblk_im, cblk_re, cblk_im, d_row, name):
    n_rows = u.shape[1]
    nchunk = n_rows // SSM_CHUNK
    last = SSM_SEG_LEN - 1

    def body(u_ref, ar_ref, ai_ref, br_ref, bi_ref, cr_ref, ci_ref, d_ref, y_ref, cin_re_ref, cin_im_ref,
             u_s, s_re, s_im, pw_re, pw_im, st_re, st_im):
        @pl.when(pl.program_id(0) == 0)
        def _():
            _ssm_powers(ar_ref, ai_ref, pw_re, pw_im)
            st_re[...] = jnp.zeros_like(st_re)
            st_im[...] = jnp.zeros_like(st_im)

        _ssm_load_rows(u_ref, u_s)
        _ssm_input_proj(u_s, br_ref, bi_ref, s_re, s_im)
        _ssm_scan(ar_ref, ai_ref, s_re, s_im, None, None, conj=False, reverse=False)
        p_re, p_im = pw_re[last:last + 1, :], pw_im[last:last + 1, :]
        c_re, c_im = st_re[...], st_im[...]
        for j in range(SSM_SEGS):
            cin_re_ref[j:j + 1, :] = c_re
            cin_im_ref[j:j + 1, :] = c_im
            row = last * SSM_SEGS + j
            c_re, c_im = _cmul_add(p_re, p_im, c_re, c_im, s_re[row:row + 1, :], s_im[row:row + 1, :])
        st_re[...] = c_re
        st_im[...] = c_im
        _ssm_fixup(pw_re, pw_im, s_re, s_im, cin_re_ref, cin_im_ref, conj=False, reverse=False)
        for t in range(SSM_TILES):
            cols = slice(t * SSM_TILE_GN, (t + 1) * SSM_TILE_GN)
            lanes = slice(t * LANES, (t + 1) * LANES)
            y = _nn(s_re[:, cols].astype(BF16), cr_ref[t]) - _nn(s_im[:, cols].astype(BF16), ci_ref[t])
            u_s[:, lanes] = y + d_ref[:, lanes] * u_s[:, lanes]
        _ssm_store_rows(u_s, y_ref)

    whole2 = lambda a: pl.BlockSpec(a.shape, lambda c: (0, 0))
    whole3 = lambda a: pl.BlockSpec(a.shape, lambda c: (0, 0, 0))
    chunk = pl.BlockSpec((SSM_TILES, SSM_CHUNK, LANES), lambda c: (0, c, 0))
    seg = pl.BlockSpec((SSM_SEGS, SSM_GN), lambda c: (c, 0))
    seg_shape = jax.ShapeDtypeStruct((nchunk * SSM_SEGS, SSM_GN), F32)
    return pl.pallas_call(
        body,
        name=name,
        grid=(nchunk,),
        in_specs=[chunk, whole2(ab_re), whole2(ab_im), whole3(bblk_re), whole3(bblk_im), whole3(cblk_re),
                  whole3(cblk_im), whole2(d_row)],
        out_specs=[chunk, seg, seg],
        out_shape=[jax.ShapeDtypeStruct((SSM_TILES, n_rows, LANES), F32), seg_shape, seg_shape],
        scratch_shapes=[pltpu.VMEM((SSM_CHUNK, SSM_WIDTH), F32), pltpu.VMEM((SSM_CHUNK, SSM_GN), F32),
                        pltpu.VMEM((SSM_CHUNK, SSM_GN), F32), pltpu.VMEM((SSM_SEG_LEN, SSM_GN), F32),
                        pltpu.VMEM((SSM_SEG_LEN, SSM_GN), F32), pltpu.VMEM((1, SSM_GN), F32),
                        pltpu.VMEM((1, SSM_GN), F32)],
        compiler_params=_params(("arbitrary",)),
    )(u, ab_re, ab_im, bblk_re, bblk_im, cblk_re, cblk_im, d_row)


def _ssm_bwd(u, dy, cin_re, cin_im, ab_re, ab_im, bblk_re, bblk_im, cblk_re, cblk_im, d_row, name):
    n_rows = u.shape[1]
    nchunk = n_rows // SSM_CHUNK

    def body(u_ref, dy_ref, cin_re_ref, cin_im_ref, ar_ref, ai_ref, br_ref, bi_ref, cr_ref, ci_ref, d_ref,
             du_ref, dar_ref, dai_ref, dbr_ref, dbi_ref, dcr_ref, dci_ref, dd_ref,
             u_s, dy_s, s_re, s_im, q_re, q_im, pw_re, pw_im, qst_re, qst_im, qin_re, qin_im):
        @pl.when(pl.program_id(0) == 0)
        def _():
            _ssm_powers(ar_ref, ai_ref, pw_re, pw_im)
            qst_re[...] = jnp.zeros_like(qst_re)
            qst_im[...] = jnp.zeros_like(qst_im)
            for ref in (dar_ref, dai_ref, dbr_ref, dbi_ref, dcr_ref, dci_ref, dd_ref):
                ref[...] = jnp.zeros_like(ref)

        _ssm_load_rows(u_ref, u_s)
        _ssm_load_rows(dy_ref, dy_s)
        _ssm_input_proj(u_s, br_ref, bi_ref, s_re, s_im)
        _ssm_scan(ar_ref, ai_ref, s_re, s_im, cin_re_ref, cin_im_ref, conj=False, reverse=False)
        for t in range(SSM_TILES):
            cols = slice(t * SSM_TILE_GN, (t + 1) * SSM_TILE_GN)
            dyb = dy_s[:, t * LANES:(t + 1) * LANES].astype(BF16)
            q_re[:, cols] = _nt(dyb, cr_ref[t])
            q_im[:, cols] = -_nt(dyb, ci_ref[t])
            dcr_ref[t] += _tn(s_re[:, cols].astype(BF16), dyb)
            dci_ref[t] -= _tn(s_im[:, cols].astype(BF16), dyb)
        _ssm_scan(ar_ref, ai_ref, q_re, q_im, None, None, conj=True, reverse=True)
        last = SSM_SEG_LEN - 1
        p_re, p_im = pw_re[last:last + 1, :], -pw_im[last:last + 1, :]
        c_re, c_im = qst_re[...], qst_im[...]
        for j in reversed(range(SSM_SEGS)):
            qin_re[j:j + 1, :] = c_re
            qin_im[j:j + 1, :] = c_im
            c_re, c_im = _cmul_add(p_re, p_im, c_re, c_im, q_re[j:j + 1, :], q_im[j:j + 1, :])
        qst_re[...] = c_re
        qst_im[...] = c_im
        _ssm_fixup(pw_re, pw_im, q_re, q_im, qin_re, qin_im, conj=True, reverse=True)
        for t in range(SSM_TILES):
            cols = slice(t * SSM_TILE_GN, (t + 1) * SSM_TILE_GN)

            def step(i, acc, cols=cols):
                rows = pl.ds(pl.multiple_of(i * SSM_SEGS, SSM_SEGS), SSM_SEGS)
                prev = pl.ds(pl.multiple_of((i - 1) * SSM_SEGS, SSM_SEGS), SSM_SEGS)
                qr, qi = q_re[rows, cols], q_im[rows, cols]
                sr, si = s_re[prev, cols], s_im[prev, cols]
                return acc[0] + qr * sr + qi * si, acc[1] + qi * sr - qr * si

            qr, qi = q_re[0:SSM_SEGS, cols], q_im[0:SSM_SEGS, cols]
            sr, si = cin_re_ref[:, cols], cin_im_ref[:, cols]
            acc = lax.fori_loop(1, SSM_SEG_LEN, step, (qr * sr + qi * si, qi * sr - qr * si))
            dar_ref[:, cols] += jnp.sum(acc[0], axis=0, keepdims=True)
            dai_ref[:, cols] += jnp.sum(acc[1], axis=0, keepdims=True)
        for t in range(SSM_TILES):
            cols = slice(t * SSM_TILE_GN, (t + 1) * SSM_TILE_GN)
            lanes = slice(t * LANES, (t + 1) * LANES)
            qrb, qib = q_re[:, cols].astype(BF16), q_im[:, cols].astype(BF16)
            u_t, dy_t = u_s[:, lanes], dy_s[:, lanes]
            ub = u_t.astype(BF16)
            dbr_ref[t] += _tn(ub, qrb)
            dbi_ref[t] += _tn(ub, qib)
            dd_ref[:, lanes] += jnp.sum(dy_t * u_t, axis=0, keepdims=True)
            u_s[:, lanes] = _nt(qrb, br_ref[t]) + _nt(qib, bi_ref[t]) + dy_t * d_ref[:, lanes]
        _ssm_store_rows(u_s, du_ref)

    whole2 = lambda a: pl.BlockSpec(a.shape, lambda c: (0, 0))
    whole3 = lambda a: pl.BlockSpec(a.shape, lambda c: (0, 0, 0))
    chunk = pl.BlockSpec((SSM_TILES, SSM_CHUNK, LANES), lambda c: (0, nchunk - 1 - c, 0))
    seg = pl.BlockSpec((SSM_SEGS, SSM_GN), lambda c: (nchunk - 1 - c, 0))
    gn_row = jax.ShapeDtypeStruct((1, SSM_GN), F32)
    b_shape = jax.ShapeDtypeStruct((SSM_TILES, LANES, SSM_TILE_GN), F32)
    c_shape = jax.ShapeDtypeStruct((SSM_TILES, SSM_TILE_GN, LANES), F32)
    d_shape = jax.ShapeDtypeStruct((1, SSM_WIDTH), F32)
    big = pltpu.VMEM((SSM_CHUNK, SSM_GN), F32)
    return pl.pallas_call(
        body,
        name=name,
        grid=(nchunk,),
        in_specs=[chunk, chunk, seg, seg, whole2(ab_re), whole2(ab_im), whole3(bblk_re), whole3(bblk_im),
                  whole3(cblk_re), whole3(cblk_im), whole2(d_row)],
        out_specs=[chunk, whole2(ab_re), whole2(ab_im), whole3(bblk_re), whole3(bblk_im), whole3(cblk_re),
                   whole3(cblk_im), whole2(d_row)],
        out_shape=[jax.ShapeDtypeStruct((SSM_TILES, n_rows, LANES), F32), gn_row, gn_row, b_shape, b_shape, c_shape,
                   c_shape, d_shape],
        scratch_shapes=[pltpu.VMEM((SSM_CHUNK, SSM_WIDTH), F32), pltpu.VMEM((SSM_CHUNK, SSM_WIDTH), F32),
                        big, big, big, big,
                        pltpu.VMEM((SSM_SEG_LEN, SSM_GN), F32), pltpu.VMEM((SSM_SEG_LEN, SSM_GN), F32),
                        pltpu.VMEM((1, SSM_GN), F32), pltpu.VMEM((1, SSM_GN), F32),
                        pltpu.VMEM((SSM_SEGS, SSM_GN), F32), pltpu.VMEM((SSM_SEGS, SSM_GN), F32)],
        compiler_params=_params(("arbitrary",)),
    )(u, dy, cin_re, cin_im, ab_re, ab_im, bblk_re, bblk_im, cblk_re, cblk_im, d_row)


def _mlp_fwd(h2, w1, w2, tm, tf, name):
    n_rows, dm = h2.shape
    dff = w1.shape[1]

    def body(h_ref, w1_ref, w2_ref, a_ref, y_ref):
        a = _nn(h_ref[...], w1_ref[...])
        a_ref[...] = a.astype(BF16)
        r = jnp.maximum(a, 0.0)
        part = _nn((r * r).astype(BF16), w2_ref[...])
        j = pl.program_id(1)

        @pl.when(j == 0)
        def _():
            y_ref[...] = part

        @pl.when(j > 0)
        def _():
            y_ref[...] += part

    return pl.pallas_call(
        body,
        name=name,
        grid=(n_rows // tm, dff // tf),
        in_specs=[pl.BlockSpec((tm, dm), lambda i, j: (i, 0)), pl.BlockSpec((dm, tf), lambda i, j: (0, j)),
                  pl.BlockSpec((tf, dm), lambda i, j: (j, 0))],
        out_specs=[pl.BlockSpec((tm, tf), lambda i, j: (i, j)), pl.BlockSpec((tm, dm), lambda i, j: (i, 0))],
        out_shape=[jax.ShapeDtypeStruct((n_rows, dff), BF16), jax.ShapeDtypeStruct((n_rows, dm), F32)],
        compiler_params=_params(("parallel", "arbitrary")),
    )(h2, w1, w2)


def _mlp_bwd(dy, h2, a, w2, tm, tf, name):
    n_rows, dm = h2.shape
    dff = a.shape[1]
    per_chip = dff // N_CHIPS // tf

    def body(dy_ref, h_ref, a_ref, w2_ref, da_ref, dw2_ref, dw1_ref):
        dyb = dy_ref[...]
        r = jnp.maximum(a_ref[...].astype(F32), 0.0)
        da = (_nt(dyb, w2_ref[...]) * (2.0 * r)).astype(BF16)
        da_ref[...] = da
        p2 = _tn((r * r).astype(BF16), dyb)
        p1 = _tn(h_ref[...], da)
        i = pl.program_id(1)

        @pl.when(i == 0)
        def _():
            dw2_ref[...] = p2
            dw1_ref[...] = p1

        @pl.when(i > 0)
        def _():
            dw2_ref[...] += p2
            dw1_ref[...] += p1

    return pl.pallas_call(
        body,
        name=name,
        grid=(dff // tf, n_rows // tm),
        in_specs=[pl.BlockSpec((tm, dm), lambda j, i: (i, 0)), pl.BlockSpec((tm, dm), lambda j, i: (i, 0)),
                  pl.BlockSpec((tm, tf), lambda j, i: (i, j)), pl.BlockSpec((tf, dm), lambda j, i: (j, 0))],
        out_specs=[pl.BlockSpec((tm, tf), lambda j, i: (i, j)), pl.BlockSpec((tf, dm), lambda j, i: (j, 0)),
                   pl.BlockSpec((None, dm, tf), lambda j, i: (j // per_chip, 0, j % per_chip))],
        out_shape=[jax.ShapeDtypeStruct((n_rows, dff), BF16), jax.ShapeDtypeStruct((dff, dm), F32),
                   jax.ShapeDtypeStruct((N_CHIPS, dm, dff // N_CHIPS), F32)],
        compiler_params=_params(("parallel", "arbitrary")),
    )(dy, h2, a, w2)


def _local_step(x, pos_col, mod, target, wts, small, hooks=None):
    n_rows = x.shape[0]
    sh1, sc1, gt1, sh2, sc2, gt2 = (mod[:, i * D_MODEL:(i + 1) * D_MODEL] for i in range(N_MOD))
    tm = 256
    d_acc = (1, D_MODEL)

    (h1,) = _rowwise(lambda r, c: [_mod_norm(r[0], *c)], [x], [small["g_pre_mix"], sc1, sh1],
                     [(D_MODEL, BF16)], [], tm, "pre_mix_fwd")
    proj = _matmul(h1, wts["w_in"], "nn", F32, 512, 1408, 2048, "in_proj")

    freqs = ROPE_THETA ** (-jnp.arange(0, ROT_DIM, 2, dtype=F32) / ROT_DIM)
    freq_lane = jnp.tile(freqs, LANES // (ROT_DIM // 2))[None, :]
    tabs = _rope_tables(pos_col, freq_lane, "rope_tables")
    att = [_att_fwd(proj, tabs, gi, f"att_fwd_{gi}") for gi in range(3)]

    expand = jnp.asarray(_expand_np())
    b_re2, b_im2 = small["ssm_b_re"].reshape(SSM_G, -1), small["ssm_b_im"].reshape(SSM_G, -1)
    log_dt = small["ssm_log_dt"].reshape(SSM_G, 1)
    prep_in = (small["ssm_a_re"], small["ssm_a_im"], log_dt, b_re2, b_im2, expand)
    ab_re, ab_im, bb_re, bb_im = _ssm_prep(*prep_in, "ssm_prep")
    ab_re_row, ab_im_row = ab_re.reshape(1, SSM_GN), ab_im.reshape(1, SSM_GN)
    bblk = [_block_diag_in(t.reshape(SSM_G, SSM_N, SSM_P)).astype(BF16) for t in (bb_re, bb_im)]
    cblk = [_block_diag_out(small[k]).astype(BF16) for k in ("ssm_c_re", "ssm_c_im")]
    d_row = small["ssm_d"].reshape(1, SSM_WIDTH)
    u = _lane_tiled(proj[:, IN_WIDTH - SSM_WIDTH:])
    y_ssm, cin_re, cin_im = _ssm_fwd(u, ab_re_row, ab_im_row, *bblk, *cblk, d_row, "ssm_fwd")
    y_ssm = _lane_untiled(y_ssm)

    def mixers_out(r, c):
        w_glu, b_glu, g_att, g_ssm = c
        att_n = _att_mix(*r[:6], g_att)
        y2 = _gelu(r[6])
        z = _nn(y2.astype(BF16), w_glu) + b_glu
        return [jnp.concatenate([att_n.astype(BF16), _glu_out(y2, z, g_ssm).astype(BF16)], axis=1)]

    att_rows = [a[0] for a in att] + [a[1] for a in att]
    mix_consts = [wts["w_glu"], small["b_glu"], small["g_attn_out"], small["g_ssm_out"]]
    (cat,) = _rowwise(mixers_out, att_rows + [y_ssm], mix_consts, [(OUT_IN_WIDTH, BF16)], [], tm, "mixers_out_fwd")
    mix = _matmul(cat, wts["w_out"], "nn", F32, 512, 1024, 1280, "out_proj")

    post_consts = [small["g_post_mix"], gt1, small["g_pre_mlp"], sc2, sh2]
    x1, h2 = _rowwise(lambda r, c: list(_post_mix(r[0], r[1], *c)), [x, mix], post_consts,
                      [(D_MODEL, F32), (D_MODEL, BF16)], [], tm, "post_mix_fwd")
    w_mlp_in, w_mlp_out = (wts["w_mlp_in"], wts["w_mlp_out"]) if hooks is None else hooks.mlp_weights(h2)
    a_mlp, y_mlp = _mlp_fwd(h2, w_mlp_in, w_mlp_out, 1024, 512, "mlp_fwd")

    def loss_head(r, c):
        x1_v, y_v, t_v = r
        g, gt = c
        fn = lambda y_, g_, gt_: gt_ * _rms(y_, g_)
        out, vjp = jax.vjp(fn, y_v, g, gt)
        err = x1_v + out - t_v
        dx2 = err * (1.0 / D_MODEL)
        dy, dg, dgt = vjp(dx2)
        loss = 0.5 * jnp.sum(jnp.sum(err * err, axis=1, keepdims=True), axis=0, keepdims=True) * (1.0 / D_MODEL)
        return [dx2, dy, loss, dg, dgt]

    dx2, dy_mlp, loss, dg_post_mlp, dgt2 = _rowwise(
        loss_head, [x1, y_mlp, target], [small["g_post_mlp"], gt2],
        [(D_MODEL, F32), (D_MODEL, BF16)], [(1, 1), d_acc, d_acc], tm, "loss_head")

    da_mlp, dw_mlp_out, dw_mlp_in = _mlp_bwd(dy_mlp, h2, a_mlp, w_mlp_out, 1024, 512, "mlp_bwd")
    dw_mlp_out = dw_mlp_out.reshape(dw_mlp_in.shape)
    sent = None if hooks is None else hooks.mlp_grads_to_sibling(dw_mlp_in, dw_mlp_out)
    dh2 = _matmul(da_mlp, w_mlp_in, "nt", F32, 512, 1024, 2048, "mlp_in_bwd", after=sent)
    sent = None if hooks is None else hooks.mlp_grads_to_chips(dh2)

    def post_mix_bwd(r, c):
        x_v, mix_v, dx1_v, dh2_v = r
        _, vjp = jax.vjp(_post_mix, x_v, mix_v, *c)
        return list(vjp((dx1_v, dh2_v)))

    post_consts_bwd = post_consts if sent is None else [_tie(post_consts[0], sent)] + post_consts[1:]
    dx_a, dmix, dg_post_mix, dgt1, dg_pre_mlp, dsc2, dsh2 = _rowwise(
        post_mix_bwd, [x, mix, dx2, dh2], post_consts_bwd, [(D_MODEL, F32), (D_MODEL, BF16)], [d_acc] * 5, tm,
        "post_mix_bwd")

    dcat = _matmul(dmix, wts["w_out"], "nt", F32, 512, 1280, 2048, "out_proj_bwd")
    dw_out = _matmul(cat, dmix, "tn", F32, 1280, 1024, 512, "out_proj_wgrad")

    def mixers_out_bwd(r, c):
        w_glu, b_glu, g_att, g_ssm = c
        dcat_v = r[7]
        _, vjp_att = jax.vjp(_att_mix, *r[:6], g_att)
        *d_ol, dg_att = vjp_att(dcat_v[:, :KV_WIDTH])
        y2, vjp_gelu = jax.vjp(_gelu, r[6])
        y2b = y2.astype(BF16)
        z = _nn(y2b, w_glu) + b_glu
        _, vjp_glu = jax.vjp(_glu_out, y2, z, g_ssm)
        dy2, dz, dg_ssm = vjp_glu(dcat_v[:, KV_WIDTH:])
        dzb = dz.astype(BF16)
        (dy,) = vjp_gelu(dy2 + _nt(dzb, w_glu))
        return d_ol + [dy, dg_att, _tn(y2b, dzb), jnp.sum(dz, axis=0, keepdims=True), dg_ssm]

    *d_att, dy_ssm, dg_attn_out, dw_glu, db_glu, dg_ssm_out = _rowwise(
        mixers_out_bwd, att_rows + [y_ssm, dcat], mix_consts,
        [(KV_WIDTH, F32)] * 6 + [(SSM_WIDTH, F32)],
        [(1, KV_WIDTH), (SSM_WIDTH, SSM_WIDTH), (1, SSM_WIDTH), (1, SSM_WIDTH)], tm, "mixers_out_bwd")

    du, dab_re, dab_im, dbblk_re, dbblk_im, dcblk_re, dcblk_im, dd_row = _ssm_bwd(
        u, _lane_tiled(dy_ssm), cin_re, cin_im, ab_re_row, ab_im_row, *bblk, *cblk, d_row, "ssm_bwd")
    du = _lane_untiled(du)
    prep_cts = (dab_re.reshape(SSM_G, SSM_N), dab_im.reshape(SSM_G, SSM_N),
                _block_diag_in_grad(dbblk_re).reshape(SSM_G, -1), _block_diag_in_grad(dbblk_im).reshape(SSM_G, -1))
    da_re, da_im, dlog_dt, db_re, db_im = _ssm_prep_bwd(*prep_in, prep_cts, "ssm_prep_bwd")

    dqkv = [_att_bwd(proj, tabs, att[gi][0], att[gi][1], d_att[gi], d_att[3 + gi], gi, f"att_bwd_{gi}")
            for gi in range(3)]

    def gather_dproj(r, c):
        dq = [r[0], r[3], r[6]]
        dk = r[1] + r[4] + r[7]
        dv = r[2] + r[5] + r[8]
        return [jnp.concatenate([t.astype(BF16) for t in dq + [dk, dv, r[9]]], axis=1)]

    (dproj,) = _rowwise(gather_dproj, [t for g in dqkv for t in g] + [du], [], [(IN_WIDTH, BF16)], [], tm,
                        "gather_dproj")
    dh1 = _matmul(dproj, wts["w_in"], "nt", F32, 512, 1024, 2816, "in_proj_bwd")
    dw_in = _matmul(h1, dproj, "tn", F32, 1024, 1408, 512, "in_proj_wgrad")

    def pre_mix_bwd(r, c):
        x_v, dh1_v, dxa_v = r
        _, vjp = jax.vjp(_mod_norm, x_v, *c)
        dx, dg, dsc, dsh = vjp(dh1_v)
        return [dx + dxa_v, dg, dsc, dsh]

    grad_x, dg_pre_mix, dsc1, dsh1 = _rowwise(
        pre_mix_bwd, [x, dh1, dx_a], [small["g_pre_mix"], sc1, sh1], [(D_MODEL, F32)], [d_acc] * 3, tm, "pre_mix_bwd")

    dmod = jnp.concatenate([dsh1, dsc1, dgt1, dsh2, dsc2, dgt2], axis=1)
    big = dict(w_in=dw_in, w_out=dw_out, w_mlp_in=dw_mlp_in, w_mlp_out=dw_mlp_out, w_glu=dw_glu)
    small_g = dict(
        g_pre_mix=dg_pre_mix, g_post_mix=dg_post_mix, ssm_a_re=da_re, ssm_a_im=da_im,
        ssm_log_dt=dlog_dt.reshape(1, SSM_G), ssm_b_re=db_re.reshape(SSM_G, SSM_N, SSM_P),
        ssm_b_im=db_im.reshape(SSM_G, SSM_N, SSM_P), ssm_c_re=_block_diag_out_grad(dcblk_re),
        ssm_c_im=_block_diag_out_grad(dcblk_im), ssm_d=dd_row.reshape(SSM_G, SSM_P), b_glu=db_glu,
        g_attn_out=dg_attn_out, g_ssm_out=dg_ssm_out, g_pre_mlp=dg_pre_mlp, g_post_mlp=dg_post_mlp)
    return loss, grad_x, dmod, big, small_g


MESH_ID = pl.DeviceIdType.MESH
N_DEV = 8
N_CHIPS = 4
HBM_SPEC = pl.BlockSpec(memory_space=pltpu.HBM)


def _place():
    x, y, c = lax.axis_index("x"), lax.axis_index("y"), lax.axis_index("c")
    other_chips = [(1 - x, y), (x, 1 - y), (1 - x, 1 - y)]
    return x, y, c, other_chips


def _half_rows(index, half):
    return pl.ds(pl.multiple_of(index * half, ROW_PAD), half)


def _remote(src, dst, send_sem, recv_sem, dev):
    return pltpu.make_async_remote_copy(src_ref=src, dst_ref=dst, send_sem=send_sem, recv_sem=recv_sem,
                                        device_id=dev, device_id_type=MESH_ID)


def _all_gather8(block, name):
    m_per, n = block.shape

    def body(x_ref, out_ref, send_sems, recv_sems, local_sem):
        x, y, c, chips = _place()
        me, sibling = (x, y, c), (x, y, 1 - c)

        def rows(px, py, pc):
            return out_ref.at[pl.ds((4 * px + 2 * py + pc) * m_per, m_per), :]

        def copy(k, blk, to, src=None):
            return _remote(rows(*blk) if src is None else src, rows(*blk), send_sems.at[k], recv_sems.at[k], to)

        mine = pltpu.make_async_copy(x_ref, rows(*me), local_sem)
        mine.start()
        first = [copy(0, me, sibling, src=x_ref)]
        first += [copy(1 + j, me, (*chip, c), src=x_ref) for j, chip in enumerate(chips)]
        for cp in first:
            cp.start()
        passed = [copy(4 + j, (*chip, c), sibling) for j, chip in enumerate(chips)]
        for j, chip in enumerate(chips):
            copy(1 + j, (*chip, c), me).wait_recv()
            passed[j].start()
        copy(0, sibling, me).wait_recv()
        for j, chip in enumerate(chips):
            copy(4 + j, (*chip, 1 - c), me).wait_recv()
        for cp in first + passed:
            cp.wait_send()
        mine.wait()

    return pl.pallas_call(
        body,
        name=name,
        out_shape=jax.ShapeDtypeStruct((N_DEV * m_per, n), block.dtype),
        in_specs=[pl.BlockSpec(memory_space=pltpu.VMEM)],
        out_specs=pl.BlockSpec(memory_space=pltpu.VMEM),
        scratch_shapes=[pltpu.SemaphoreType.DMA((7,)), pltpu.SemaphoreType.DMA((7,)), pltpu.SemaphoreType.DMA],
        compiler_params=_params(),
    )(block)


def _weight_gather(shards, name):
    n = len(shards)
    shapes = [s.shape for s in shards]

    def body(*refs):
        ins, outs = refs[:n], refs[n:2 * n]
        send, recv, fsend, frecv = refs[2 * n:]
        x, y, c, chips = _place()
        k_me = 2 * x + y
        sibling = (x, y, 1 - c)
        pending = []
        for a in range(n):
            half = shapes[a][0] // 2
            mine = _half_rows(c, half)
            for j, chip in enumerate(chips):
                cp = _remote(ins[a].at[mine, :], outs[a].at[k_me, mine, :], send.at[a, j], recv.at[a, j], (*chip, c))
                cp.start()
                pending.append(cp.wait_send)
        for a in range(n):
            half = shapes[a][0] // 2
            for j, (px, py) in enumerate(chips):
                piece = outs[a].at[2 * px + py, _half_rows(c, half), :]
                _remote(piece, piece, send.at[a, j], recv.at[a, j], (px, py, c)).wait_recv()
                fw = _remote(piece, piece, fsend.at[a, j], frecv.at[a, j], sibling)
                fw.start()
                pending.append(fw.wait_send)
        for a in range(n):
            half = shapes[a][0] // 2
            for j, (px, py) in enumerate(chips):
                piece = outs[a].at[2 * px + py, _half_rows(1 - c, half), :]
                _remote(piece, piece, fsend.at[a, j], frecv.at[a, j], sibling).wait_recv()
        for wait in pending:
            wait()

    sems = pltpu.SemaphoreType.DMA((n, 3))
    return pl.pallas_call(
        body,
        name=name,
        out_shape=[jax.ShapeDtypeStruct((N_CHIPS,) + s, BF16) for s in shapes],
        in_specs=[HBM_SPEC] * n,
        out_specs=[HBM_SPEC] * n,
        scratch_shapes=[sems, sems, sems, sems],
        compiler_params=_params(),
    )(*shards)


def _sibling_halves(stacks, name):
    n = len(stacks)
    shapes = [s.shape for s in stacks]

    def body(*refs):
        ins, outs = refs[:n], refs[n:2 * n]
        send, recv = refs[2 * n:]
        x, y, c, _ = _place()
        copies = []
        for a in range(n):
            half = shapes[a][1] // 2
            cp = _remote(ins[a].at[:, _half_rows(1 - c, half), :], outs[a], send.at[a], recv.at[a], (x, y, 1 - c))
            cp.start()
            copies.append(cp)
        for cp in copies:
            cp.wait()

    return pl.pallas_call(
        body,
        name=name,
        out_shape=[jax.ShapeDtypeStruct((N_CHIPS, s[1] // 2, s[2]), F32) for s in shapes],
        in_specs=[HBM_SPEC] * n,
        out_specs=[HBM_SPEC] * n,
        scratch_shapes=[pltpu.SemaphoreType.DMA((n,)), pltpu.SemaphoreType.DMA((n,))],
        compiler_params=_params(),
    )(*stacks)


def _chip_exchange(parts, name):
    n = len(parts)
    shapes = [p.shape for p in parts]

    def body(*refs):
        ins, outs = refs[:n], refs[n:2 * n]
        send, recv = refs[2 * n:]
        x, y, c, chips = _place()
        copies = []
        for a in range(n):
            for j, (px, py) in enumerate(chips):
                cp = _remote(ins[a].at[2 * px + py], outs[a].at[j], send.at[a, j], recv.at[a, j], (px, py, c))
                cp.start()
                copies.append(cp)
        for cp in copies:
            cp.wait()

    return pl.pallas_call(
        body,
        name=name,
        out_shape=[jax.ShapeDtypeStruct((3,) + s[1:], BF16) for s in shapes],
        in_specs=[HBM_SPEC] * n,
        out_specs=[HBM_SPEC] * n,
        scratch_shapes=[pltpu.SemaphoreType.DMA((n, 3)), pltpu.SemaphoreType.DMA((n, 3))],
        compiler_params=_params(),
    )(*parts)


def _sibling_swap(halves, name):
    n = len(halves)
    shapes = [h.shape for h in halves]

    def body(*refs):
        ins, outs = refs[:n], refs[n:2 * n]
        send, recv = refs[2 * n:]
        x, y, c, _ = _place()
        pending = []
        for a in range(n):
            half = shapes[a][0]
            mine = outs[a].at[_half_rows(c, half), :]
            cp = _remote(ins[a], mine, send.at[a], recv.at[a], (x, y, 1 - c))
            cp.start()
            pending.append(cp.wait_send)
        for a in range(n):
            half = shapes[a][0]
            theirs = outs[a].at[_half_rows(1 - c, half), :]
            _remote(theirs, theirs, send.at[a], recv.at[a], (x, y, 1 - c)).wait_recv()
        for wait in pending:
            wait()

    return pl.pallas_call(
        body,
        name=name,
        out_shape=[jax.ShapeDtypeStruct((2 * s[0], s[1]), F32) for s in shapes],
        in_specs=[HBM_SPEC] * n,
        out_specs=[HBM_SPEC] * n,
        scratch_shapes=[pltpu.SemaphoreType.DMA((n,)), pltpu.SemaphoreType.DMA((n,))],
        compiler_params=_params(),
    )(*halves)


SEM_SPEC = pl.BlockSpec(memory_space=pltpu.SEMAPHORE)
ANY_SPEC = pl.BlockSpec(memory_space=pl.ANY)
DATAFLOW = pltpu.SideEffectType.DATAFLOW_SIDE_EFFECTING


def _split_copy_start(srcs, lands, plan, n_sems, name, after=None):
    bufs = list(srcs) + list(lands)
    ns, nb = len(srcs), len(bufs)
    extra = [] if after is None else [after]

    def body(*refs):
        outs = refs[nb + len(extra):]
        for outgoing, _ in plan(refs[:ns], refs[ns:nb], outs[0], outs[1]):
            outgoing.start()
        outs[-1][...] = jnp.zeros_like(outs[-1])

    sems = pltpu.SemaphoreType.DMA((n_sems,))
    return pl.pallas_call(
        body,
        name=name,
        out_shape=(sems, sems, *[pltpu.HBM(b.shape, b.dtype) for b in bufs], jax.ShapeDtypeStruct((8, LANES), F32)),
        in_specs=[HBM_SPEC] * nb + [ANY_SPEC] * len(extra),
        out_specs=(SEM_SPEC, SEM_SPEC, *[HBM_SPEC] * nb, pl.BlockSpec(memory_space=pltpu.VMEM)),
        input_output_aliases={i: 2 + i for i in range(nb)},
        compiler_params=pltpu.CompilerParams(has_side_effects=DATAFLOW),
    )(*[pltpu.with_memory_space_constraint(b, pltpu.HBM) for b in bufs], *extra)


def _split_copy_wait(started, plan, after, name):
    send, recv, *bufs = started[:-1]
    nb = len(bufs)
    ns = nb // 2

    def body(*refs):
        for outgoing, incoming in plan(refs[:ns], refs[ns:nb], refs[nb], refs[nb + 1]):
            outgoing.wait_send()
            incoming.wait_recv()

    return pl.pallas_call(
        body,
        name=name,
        out_shape=tuple(pltpu.HBM(b.shape, b.dtype) for b in bufs),
        in_specs=[HBM_SPEC] * nb + [SEM_SPEC, SEM_SPEC, ANY_SPEC],
        out_specs=tuple([HBM_SPEC] * nb),
        input_output_aliases={i: i for i in range(nb)},
        compiler_params=pltpu.CompilerParams(has_side_effects=DATAFLOW),
    )(*bufs, send, recv, after)


def _weight_plan(shapes):
    def plan(srcs, lands, send, recv):
        x, y, c, chips = _place()
        copies = []
        for a in range(len(shapes)):
            mine = _half_rows(c, shapes[a][0] // 2)
            for j, (px, py) in enumerate(chips):
                s = 3 * a + j
                arrival = lands[a].at[2 * px + py, mine, :]
                copies.append((_remote(srcs[a].at[mine, :], lands[a].at[2 * x + y, mine, :], send.at[s], recv.at[s], (px, py, c)),
                               _remote(arrival, arrival, send.at[s], recv.at[s], (px, py, c))))
        return copies
    return plan


def _halves_plan(shapes):
    def plan(srcs, lands, send, recv):
        x, y, c, _ = _place()
        copies = []
        for a in range(len(shapes)):
            theirs = srcs[a].at[:, _half_rows(1 - c, shapes[a][1] // 2), :]
            copies.append((_remote(theirs, lands[a], send.at[a], recv.at[a], (x, y, 1 - c)),
                           _remote(lands[a], lands[a], send.at[a], recv.at[a], (x, y, 1 - c))))
        return copies
    return plan


def _exchange_plan(n):
    def plan(srcs, lands, send, recv):
        x, y, c, chips = _place()
        copies = []
        for a in range(n):
            for j, (px, py) in enumerate(chips):
                s = 3 * a + j
                copies.append((_remote(srcs[a].at[2 * px + py], lands[a].at[j], send.at[s], recv.at[s], (px, py, c)),
                               _remote(lands[a].at[j], lands[a].at[j], send.at[s], recv.at[s], (px, py, c))))
        return copies
    return plan


def _forward_to_sibling(stacks, name):
    n = len(stacks)
    shapes = [s.shape for s in stacks]

    def body(*refs):
        ins, outs = refs[:n], refs[n:2 * n]
        send, recv = refs[2 * n:]
        x, y, c, chips = _place()
        sibling = (x, y, 1 - c)
        copies = []
        for a in range(n):
            half = shapes[a][1] // 2
            for j, (px, py) in enumerate(chips):
                rows = _half_rows(c, half)
                cp = _remote(ins[a].at[2 * px + py, rows, :], outs[a].at[2 * px + py, rows, :], send.at[a, j],
                             recv.at[a, j], sibling)
                cp.start()
                copies.append(cp)
        for a in range(n):
            half = shapes[a][1] // 2
            for j, (px, py) in enumerate(chips):
                theirs = outs[a].at[2 * px + py, _half_rows(1 - c, half), :]
                _remote(theirs, theirs, send.at[a, j], recv.at[a, j], sibling).wait_recv()
        for cp in copies:
            cp.wait_send()

    return pl.pallas_call(
        body,
        name=name,
        out_shape=[jax.ShapeDtypeStruct(s, BF16) for s in shapes],
        in_specs=[HBM_SPEC] * n,
        out_specs=[HBM_SPEC] * n,
        input_output_aliases={a: a for a in range(n)},
        scratch_shapes=[pltpu.SemaphoreType.DMA((n, 3)), pltpu.SemaphoreType.DMA((n, 3))],
        compiler_params=_params(),
    )(*stacks)


def _tie(x, token):
    return x + token[0:1, 0:1].astype(x.dtype)


ROW_PAD = 16


def _silu(x):
    return x * _sigmoid(x)


def _ada_fwd(c_all, w_ada, b_ada, name):
    dm, cols = w_ada.shape
    tn = 512

    def body(c_ref, w_ref, b_ref, o_ref):
        o_ref[...] = _nn(_silu(c_ref[...]).astype(BF16), w_ref[...].astype(BF16)) + b_ref[...]

    return pl.pallas_call(
        body,
        name=name,
        grid=(cols // tn,),
        in_specs=[pl.BlockSpec((ROW_PAD, dm), lambda j: (0, 0)), pl.BlockSpec((dm, tn), lambda j: (0, j)),
                  pl.BlockSpec((1, tn), lambda j: (0, j))],
        out_specs=pl.BlockSpec((ROW_PAD, tn), lambda j: (0, j)),
        out_shape=jax.ShapeDtypeStruct((ROW_PAD, cols), F32),
        compiler_params=_params(("parallel",)),
    )(c_all, w_ada, b_ada)


def _adamw(w, g, m, v):
    m = ADAM_B1 * m + (1.0 - ADAM_B1) * g
    v = ADAM_B2 * v + (1.0 - ADAM_B2) * (g * g)
    m_hat = m / (1.0 - ADAM_B1 ** ADAM_STEP)
    v_hat = v / (1.0 - ADAM_B2 ** ADAM_STEP)
    delta = -ADAM_LR * (m_hat / (jnp.sqrt(v_hat) + ADAM_EPS) + ADAM_WD * w)
    return delta, m, v


def _ada_bwd_adamw(c_all, dmod_cols, w, m, v, name):
    dm, cols = w.shape
    tm, tn = 512, 512

    def body(c_ref, d_ref, w_ref, m_ref, v_ref, g_ref, dl_ref, nm_ref, nv_ref):
        g = _tn(_silu(c_ref[...]).astype(BF16), d_ref[...].astype(BF16))
        g_ref[...] = g
        dl_ref[...], nm_ref[...], nv_ref[...] = _adamw(w_ref[...], g, m_ref[...], v_ref[...])

    tile = pl.BlockSpec((tm, tn), lambda i, j: (i, j))
    shape = jax.ShapeDtypeStruct((dm, cols), F32)
    return pl.pallas_call(
        body,
        name=name,
        grid=(dm // tm, cols // tn),
        in_specs=[pl.BlockSpec((ROW_PAD, tm), lambda i, j: (0, i)), pl.BlockSpec((ROW_PAD, tn), lambda i, j: (0, j)),
                  tile, tile, tile],
        out_specs=[tile] * 4,
        out_shape=[shape] * 4,
        compiler_params=_params(("parallel", "parallel")),
    )(c_all, dmod_cols, w, m, v)


def _sum_blocks(parts, nblk, name):
    rows, cols = parts.shape[0] // nblk, parts.shape[1]

    def body(p_ref, o_ref):
        tot = p_ref[0:rows, :]
        for b in range(1, nblk):
            tot = tot + p_ref[b * rows:(b + 1) * rows, :]
        o_ref[...] = tot

    return pl.pallas_call(body, name=name, out_shape=jax.ShapeDtypeStruct((rows, cols), F32), compiler_params=_params())(parts)


def _adamw_rows(w, g, m, v, tm, name):
    return _rowwise(lambda r, c: list(_adamw(*r)), [w, g, m, v], [], [(w.shape[1], F32)] * 3, [], tm, name)


BIG = ("w_in", "w_out", "w_mlp_in", "w_mlp_out", "w_glu")
COL_SHARDED = ("w_in", "w_out", "w_mlp_in")
SMALL = ("b_ada", "g_pre_mix", "g_post_mix", "ssm_a_re", "ssm_a_im", "ssm_log_dt", "ssm_b_re", "ssm_b_im",
         "ssm_c_re", "ssm_c_im", "ssm_d", "b_glu", "g_attn_out", "g_ssm_out", "g_pre_mlp", "g_post_mlp")
WEIGHTS = ("w_ada", "b_ada", "g_pre_mix", "g_post_mix", "w_in", "ssm_a_re", "ssm_a_im", "ssm_log_dt", "ssm_b_re",
           "ssm_b_im", "ssm_c_re", "ssm_c_im", "ssm_d", "w_glu", "b_glu", "g_attn_out", "g_ssm_out", "w_out",
           "g_pre_mlp", "g_post_mlp", "w_mlp_in", "w_mlp_out")
FLAT_COLS = 1024
FLAT_ROWS = 256
ROW_TILE = {"w_in": 256, "w_out": 128, "w_mlp_in": 256, "w_mlp_out": 256, "w_glu": 112}


def _flatten_small(tree):
    flat = jnp.concatenate([tree[k].reshape(-1) for k in SMALL])
    return jnp.pad(flat, (0, FLAT_ROWS * FLAT_COLS - flat.shape[0])).reshape(FLAT_ROWS, FLAT_COLS)


def _unflatten_small(flat, like):
    flat = flat.reshape(-1)
    out, at = {}, 0
    for k in SMALL:
        size = math.prod(like[k].shape)
        out[k] = flat[at:at + size].reshape(like[k].shape)
        at += size
    return out


def _unstack(stack, name):
    if name in COL_SHARDED:
        return stack.transpose(1, 0, 2).reshape(stack.shape[1], N_CHIPS * stack.shape[2])
    return stack.reshape(N_CHIPS * stack.shape[1], stack.shape[2])


def _stack(full, name):
    if name in COL_SHARDED:
        return full.reshape(full.shape[0], N_CHIPS, full.shape[1] // N_CHIPS).transpose(1, 0, 2)
    return full.reshape(N_CHIPS, full.shape[0] // N_CHIPS, full.shape[1])


EARLY = ("w_in", "w_out", "w_glu")
LATE = ("w_mlp_in", "w_mlp_out")


def _chip_sums(names, g_stacks, from_sibling, ic, chip):
    own, to_send = [], []
    place = jnp.stack([ic, chip]).astype(jnp.int32)
    for k, gs, fs in zip(names, g_stacks, from_sibling):
        _, rows, cols = gs.shape
        half, tm = rows // 2, ROW_TILE[k]
        nt = half // tm

        def body(place_ref, g_ref, f_ref, own_ref, send_ref):
            s = g_ref[...] + f_ref[...]
            send_ref[...] = s.astype(BF16)

            @pl.when(pl.program_id(1) == place_ref[1])
            def _():
                own_ref[...] = s

        slab = lambda index: pl.BlockSpec((None, tm, cols), index)
        mine, to_chips = pl.pallas_call(
            body,
            name="grad_chip_sum_" + k,
            grid_spec=pltpu.PrefetchScalarGridSpec(
                num_scalar_prefetch=1,
                grid=(nt, N_CHIPS),
                in_specs=[slab(lambda i, kk, p, nt=nt: (kk, p[0] * nt + i, 0)), slab(lambda i, kk, p: (kk, i, 0))],
                out_specs=[pl.BlockSpec((tm, cols), lambda i, kk, p: (i, 0)), slab(lambda i, kk, p: (kk, i, 0))]),
            out_shape=[jax.ShapeDtypeStruct((half, cols), F32), jax.ShapeDtypeStruct((N_CHIPS, half, cols), BF16)],
            compiler_params=_params(("arbitrary", "arbitrary")),
        )(place, gs, fs)
        own.append(mine)
        to_send.append(to_chips)
    return own, to_send


def _grad_totals(names, own, from_chips):
    totals = []
    for k, mine, fc in zip(names, own, from_chips):
        half, cols = mine.shape
        tm = ROW_TILE[k]

        def body(m_ref, a_ref, b_ref, c_ref, o_ref):
            o_ref[...] = m_ref[...] + a_ref[...].astype(F32) + b_ref[...].astype(F32) + c_ref[...].astype(F32)

        rows = pl.BlockSpec((tm, cols), lambda i: (i, 0))
        totals.append(pl.pallas_call(
            body,
            name="grad_total_" + k,
            grid=(half // tm,),
            in_specs=[rows] + [pl.BlockSpec((None, tm, cols), lambda i, j=j: (j, i, 0)) for j in range(3)],
            out_specs=rows,
            out_shape=jax.ShapeDtypeStruct((half, cols), F32),
            compiler_params=_params(("parallel",)),
        )(mine, fc, fc, fc))
    return totals


class _Overlap:
    def __init__(self, own_shards, ic, chip, after):
        self.ic, self.chip = ic, chip
        self.shapes = [o.shape for o in own_shards]
        lands = [lax.empty((N_CHIPS,) + s, BF16) for s in self.shapes]
        self.gather = _split_copy_start(own_shards, lands, _weight_plan(self.shapes), 3 * len(LATE),
                                        "mlp_weight_gather_start", after=after)
        self.token = self.gather[-1]

    def mlp_weights(self, after):
        n = len(LATE)
        done = _split_copy_wait(self.gather, _weight_plan(self.shapes), after, "mlp_weight_gather_wait")
        own, stacks = done[:n], done[n:]
        stacks = _forward_to_sibling(stacks, "mlp_weight_forward")
        stacks = [lax.dynamic_update_index_in_dim(s, o, self.chip, 0) for s, o in zip(stacks, own)]
        return [_unstack(s, k) for k, s in zip(LATE, stacks)]

    def mlp_grads_to_sibling(self, dw_in, dw_out):
        stacks = [dw_in, dw_out]
        self.g_shapes = [s.shape for s in stacks]
        lands = [lax.empty((N_CHIPS, s[1] // 2, s[2]), F32) for s in self.g_shapes]
        self.halves = _split_copy_start(stacks, lands, _halves_plan(self.g_shapes), len(LATE), "mlp_grad_halves_start")
        return self.halves[-1]

    def mlp_grads_to_chips(self, after):
        n = len(LATE)
        done = _split_copy_wait(self.halves, _halves_plan(self.g_shapes), after, "mlp_grad_halves_wait")
        self.own, to_send = _chip_sums(LATE, done[:n], done[n:], self.ic, self.chip)
        lands = [lax.empty((3,) + s.shape[1:], BF16) for s in to_send]
        self.exchange = _split_copy_start(to_send, lands, _exchange_plan(n), 3 * n, "mlp_grad_exchange_start")
        return self.exchange[-1]

    def mlp_grads_reduced(self, after):
        n = len(LATE)
        done = _split_copy_wait(self.exchange, _exchange_plan(n), after, "mlp_grad_exchange_wait")
        return _grad_totals(LATE, self.own, done[n:])


def _pad_rows(row):
    return jnp.pad(row, ((0, 8 - row.shape[0]), (0, 0)))


def _every_eighth(gathered):
    rows = gathered.reshape(N_DEV, 8, gathered.shape[1])[:, 0, :]
    return jnp.pad(rows, ((0, ROW_PAD - N_DEV), (0, 0)))


def kernel(x, c, positions, w_ada, b_ada, g_pre_mix, g_post_mix, w_in, ssm_a_re, ssm_a_im, ssm_log_dt, ssm_b_re, ssm_b_im, ssm_c_re, ssm_c_im, ssm_d, w_glu, b_glu, g_attn_out, g_ssm_out, w_out, g_pre_mlp, g_post_mlp, w_mlp_in, w_mlp_out, loss_target, m_w_ada, m_b_ada, m_g_pre_mix, m_g_post_mix, m_w_in, m_ssm_a_re, m_ssm_a_im, m_ssm_log_dt, m_ssm_b_re, m_ssm_b_im, m_ssm_c_re, m_ssm_c_im, m_ssm_d, m_w_glu, m_b_glu, m_g_attn_out, m_g_ssm_out, m_w_out, m_g_pre_mlp, m_g_post_mlp, m_w_mlp_in, m_w_mlp_out, v_w_ada, v_b_ada, v_g_pre_mix, v_g_post_mix, v_w_in, v_ssm_a_re, v_ssm_a_im, v_ssm_log_dt, v_ssm_b_re, v_ssm_b_im, v_ssm_c_re, v_ssm_c_im, v_ssm_d, v_w_glu, v_b_glu, v_g_attn_out, v_g_ssm_out, v_w_out, v_g_pre_mlp, v_g_post_mlp, v_w_mlp_in, v_w_mlp_out):
    given = dict(locals())
    w = {k: given[k][0] for k in WEIGHTS}
    mom = {k: given["m_" + k][0] for k in WEIGHTS}
    var = {k: given["v_" + k][0] for k in WEIGHTS}
    for tree in (w, mom, var):
        for k in ("b_ada", "g_pre_mix", "g_post_mix", "ssm_log_dt", "b_glu", "g_attn_out", "g_ssm_out", "g_pre_mlp",
                  "g_post_mlp"):
            tree[k] = tree[k].reshape(1, -1)
    ix, iy, ic = lax.axis_index("x"), lax.axis_index("y"), lax.axis_index("c")
    chip = 2 * ix + iy
    me = 4 * ix + 2 * iy + ic
    shard_cols = w["w_ada"].shape[1]

    c_all = _every_eighth(_all_gather8(_pad_rows(c), "gather_c"))
    b_ada_cols = lax.dynamic_slice_in_dim(w["b_ada"], chip * shard_cols, shard_cols, axis=1)
    mod_cols = _ada_fwd(c_all, w["w_ada"], b_ada_cols, "ada_fwd")[:N_DEV]
    mod_all = _all_gather8(mod_cols, "gather_mod").reshape(N_CHIPS, 2, N_DEV, shard_cols)[:, 0]
    mod = lax.dynamic_index_in_dim(mod_all, me, axis=1, keepdims=False).reshape(1, N_MOD * D_MODEL)

    early_own = [w[k].astype(BF16) for k in EARLY]
    stacks = _weight_gather(early_own, "weight_gather")
    stacks = [lax.dynamic_update_index_in_dim(s, o, chip, 0) for s, o in zip(stacks, early_own)]
    wts = {k: _unstack(s, k) for k, s in zip(EARLY, stacks)}
    overlap = _Overlap([w[k].astype(BF16) for k in LATE], ic, chip, after=stacks[0])
    mod = _tie(mod, overlap.token)

    small = {k: w[k] for k in SMALL if k != "b_ada"}
    loss, grad_x, dmod, big_g, small_g = _local_step(x[0], positions.reshape(-1, 1), mod, loss_target[0], wts, small,
                                                     hooks=overlap)
    loss = lax.psum(loss[0, 0], ("x", "y", "c"))

    g_stacks = [_stack(big_g[k], k) for k in EARLY]
    from_sibling = _sibling_halves(g_stacks, "grad_sibling_halves")
    chip_f32, chip_bf16 = _chip_sums(EARLY, g_stacks, from_sibling, ic, chip)
    from_chips = _chip_exchange(chip_bf16, "grad_chip_exchange")
    reduced = _grad_totals(EARLY, chip_f32, from_chips) + overlap.mlp_grads_reduced(big_g["w_in"])
    swapped = _sibling_swap(reduced, "grad_sibling_swap")
    grads = {k: lax.dynamic_update_slice_in_dim(s, r, ic * r.shape[0], axis=0)
             for k, s, r in zip(EARLY + LATE, swapped, reduced)}

    small_g["b_ada"] = dmod
    parts = _all_gather8(_flatten_small(small_g), "gather_small_grads")

    small_flat = _sum_blocks(parts, N_DEV, "small_grad_sum")
    grads.update(_unflatten_small(small_flat, w))

    mod_rows = N_MOD * D_MODEL // FLAT_COLS
    dmod_all = parts.reshape(N_DEV, FLAT_ROWS, FLAT_COLS)[:, :mod_rows].reshape(N_DEV, N_MOD * D_MODEL)
    dmod_all = jnp.pad(dmod_all, ((0, ROW_PAD - N_DEV), (0, 0)))
    dmod_cols = lax.dynamic_slice_in_dim(dmod_all, chip * shard_cols, shard_cols, axis=1)
    g_ada, d_ada, m_ada, v_ada = _ada_bwd_adamw(c_all, dmod_cols, w["w_ada"], mom["w_ada"], var["w_ada"], "ada_bwd_adamw")
    grads["w_ada"] = g_ada

    delta, new_m, new_v = {"w_ada": d_ada}, {"w_ada": m_ada}, {"w_ada": v_ada}
    for k in BIG:
        delta[k], new_m[k], new_v[k] = _adamw_rows(w[k], grads[k], mom[k], var[k], ROW_TILE[k], "adamw_" + k)
    flat_upd = _adamw_rows(_flatten_small(w), small_flat, _flatten_small(mom), _flatten_small(var), FLAT_ROWS,
                           "adamw_small")
    for tree, flat in zip((delta, new_m, new_v), flat_upd):
        tree.update(_unflatten_small(flat, w))

    shaped = lambda tree: [tree[k].reshape(given[k].shape) for k in WEIGHTS]
    return (loss, grad_x[None], *shaped(grads), *shaped(delta), *shaped(new_m), *shaped(new_v))
```

```python
import functools
import math

import jax
import jax.numpy as jnp
import numpy as np
from jax import lax
from jax.experimental import pallas as pl
from jax.experimental.pallas import tpu as pltpu

F32 = jnp.float32
BF16 = jnp.bfloat16

D_MODEL = 2048
HEAD_DIM = 64
DILATIONS = (1, 4, 16)
ATT_SPAN = 128
ATT_BLK = 128
HEADS_PER_GROUP = 6
KV_WIDTH = HEADS_PER_GROUP * HEAD_DIM
ATT_Q_WIDTH = 3 * KV_WIDTH
ROT_DIM = 16
ROPE_THETA = 500000.0
SSM_WIDTH = 896
SSM_P = 16
SSM_G = 56
SSM_N = 64
SSM_GN = SSM_G * SSM_N
SSM_TILES = SSM_WIDTH // 128
SSM_TILE_GN = 8 * SSM_N
IN_WIDTH = 2816
OUT_IN_WIDTH = 1280
D_FF = 8192
N_MOD = 6
EPS = 1e-6
LANES = 128
SSM_SEGS = 8
SSM_CHUNK = 256
SSM_SEG_LEN = SSM_CHUNK // SSM_SEGS

ADAM_LR = 0.001
ADAM_B1 = 0.9
ADAM_B2 = 0.999
ADAM_EPS = 1e-08
ADAM_WD = 0.01
ADAM_STEP = 10

VMEM_LIMIT = 56 * 1024 * 1024


def _params(sem=None):
    return pltpu.CompilerParams(dimension_semantics=sem, vmem_limit_bytes=VMEM_LIMIT)


def _dot(a, b, dims):
    return lax.dot_general(a, b, (dims, ((), ())), preferred_element_type=F32)


def _nn(a, b):
    return _dot(a, b, ((1,), (0,)))


def _nt(a, b):
    return _dot(a, b, ((1,), (1,)))


def _tn(a, b):
    return _dot(a, b, ((0,), (0,)))


def _matmul(a, b, mode, out_dtype, tm, tn, tk, name, after=None):
    if mode == "nn":
        (m, k), (_, n) = a.shape, b.shape
        a_spec = pl.BlockSpec((tm, tk), lambda i, j, kk: (i, kk))
        b_spec = pl.BlockSpec((tk, tn), lambda i, j, kk: (kk, j))
        op = _nn
    elif mode == "nt":
        (m, k), (n, _) = a.shape, b.shape
        a_spec = pl.BlockSpec((tm, tk), lambda i, j, kk: (i, kk))
        b_spec = pl.BlockSpec((tn, tk), lambda i, j, kk: (j, kk))
        op = _nt
    else:
        (k, m), (_, n) = a.shape, b.shape
        a_spec = pl.BlockSpec((tk, tm), lambda i, j, kk: (kk, i))
        b_spec = pl.BlockSpec((tk, tn), lambda i, j, kk: (kk, j))
        op = _tn
    assert m % tm == 0 and n % tn == 0 and k % tk == 0, (name, m, n, k)
    nk = k // tk

    def body(a_ref, b_ref, *rest):
        o_ref, acc_ref = rest[-2:]
        kk = pl.program_id(2)

        @pl.when(kk == 0)
        def _():
            acc_ref[...] = jnp.zeros_like(acc_ref)

        acc_ref[...] += op(a_ref[...], b_ref[...])

        @pl.when(kk == nk - 1)
        def _():
            o_ref[...] = acc_ref[...].astype(o_ref.dtype)

    extra = [] if after is None else [after]
    return pl.pallas_call(
        body,
        name=name,
        grid=(m // tm, n // tn, nk),
        in_specs=[a_spec, b_spec] + [pl.BlockSpec(t.shape, lambda i, j, kk: (0, 0)) for t in extra],
        out_specs=pl.BlockSpec((tm, tn), lambda i, j, kk: (i, j)),
        out_shape=jax.ShapeDtypeStruct((m, n), out_dtype),
        scratch_shapes=[pltpu.VMEM((tm, tn), F32)],
        compiler_params=_params(("parallel", "parallel", "arbitrary")),
    )(a, b, *extra)


def _rowwise(fn, rows, consts, out_rows, out_accs, tm, name):
    n_rows = rows[0].shape[0]
    assert n_rows % tm == 0
    nr, nc, no = len(rows), len(consts), len(out_rows)

    def body(*refs):
        r_in, c_in = refs[:nr], refs[nr:nr + nc]
        o_row, o_acc = refs[nr + nc:nr + nc + no], refs[nr + nc + no:]
        outs = fn([r[...] for r in r_in], [c[...] for c in c_in])
        assert len(outs) == len(o_row) + len(o_acc), name
        for ref, v in zip(o_row, outs[:no]):
            ref[...] = v.astype(ref.dtype)
        first = pl.program_id(0) == 0
        for ref, v in zip(o_acc, outs[no:]):
            @pl.when(first)
            def _(ref=ref, v=v):
                ref[...] = v.astype(F32)

            @pl.when(jnp.logical_not(first))
            def _(ref=ref, v=v):
                ref[...] += v.astype(F32)

    in_specs = [pl.BlockSpec((tm, r.shape[1]), lambda i: (i, 0)) for r in rows]
    in_specs += [pl.BlockSpec(c.shape, lambda i: (0, 0)) for c in consts]
    out_specs = [pl.BlockSpec((tm, w), lambda i: (i, 0)) for w, _ in out_rows]
    out_specs += [pl.BlockSpec(s, lambda i: (0, 0)) for s in out_accs]
    out_shape = [jax.ShapeDtypeStruct((n_rows, w), dt) for w, dt in out_rows]
    out_shape += [jax.ShapeDtypeStruct(s, F32) for s in out_accs]
    return pl.pallas_call(
        body,
        name=name,
        grid=(n_rows // tm,),
        in_specs=in_specs,
        out_specs=out_specs,
        out_shape=out_shape,
        compiler_params=_params(("arbitrary",)),
    )(*rows, *consts)


def _rms(x, g):
    return x * lax.rsqrt(jnp.mean(x * x, axis=-1, keepdims=True) + EPS) * g


def _mod_norm(x, g, sc, sh):
    return _rms(x, g) * (1.0 + sc) + sh


def _gelu(x):
    return 0.5 * x * (1.0 + jnp.tanh(math.sqrt(2.0 / math.pi) * (x + 0.044715 * (x * x * x))))


def _sigmoid(x):
    return 1.0 / (1.0 + jnp.exp(-x))


def _post_mix(x, mix, g_post, gt1, g_pre, sc2, sh2):
    x1 = x + gt1 * _rms(mix, g_post)
    return x1, _mod_norm(x1, g_pre, sc2, sh2)


def _att_mix(o0, o1, o2, l0, l1, l2, g):
    m = jnp.maximum(jnp.maximum(l0, l1), l2)
    e0, e1, e2 = jnp.exp(l0 - m), jnp.exp(l1 - m), jnp.exp(l2 - m)
    att = (e0 * o0 + e1 * o1 + e2 * o2) / (e0 + e1 + e2)
    return _rms(att, g)


def _glu_out(y2, z, g):
    return _rms(y2 * _sigmoid(z), g)


def _rope_tables(pos_col, freq_lane, name):
    n_rows = pos_col.shape[0]
    tm = 512

    def body(p_ref, f_ref, cos_ref, lo_ref, hi_ref):
        ang = p_ref[...].astype(F32) * f_ref[...]
        lane = lax.broadcasted_iota(jnp.int32, ang.shape, 1) % HEAD_DIM
        c, s = jnp.cos(ang), jnp.sin(ang)
        cos_ref[...] = jnp.where(lane < ROT_DIM, c, 1.0)
        lo_ref[...] = jnp.where(lane < ROT_DIM // 2, -s, 0.0)
        hi_ref[...] = jnp.where((lane >= ROT_DIM // 2) & (lane < ROT_DIM), s, 0.0)

    tab = jax.ShapeDtypeStruct((n_rows, LANES), F32)
    return pl.pallas_call(
        body,
        name=name,
        grid=(n_rows // tm,),
        in_specs=[pl.BlockSpec((tm, 1), lambda i: (i, 0)), pl.BlockSpec((1, LANES), lambda i: (0, 0))],
        out_specs=[pl.BlockSpec((tm, LANES), lambda i: (i, 0))] * 3,
        out_shape=[tab] * 3,
        compiler_params=_params(("parallel",)),
    )(pos_col, freq_lane)


def _rope(x, cos_t, lo_t, hi_t):
    half = ROT_DIM // 2
    return x * cos_t + pltpu.roll(x, LANES - half, 1) * lo_t + pltpu.roll(x, half, 1) * hi_t


def _rope_transposed(dy, cos_t, lo_t, hi_t):
    half = ROT_DIM // 2
    return dy * cos_t + pltpu.roll(dy * lo_t, half, 1) + pltpu.roll(dy * hi_t, LANES - half, 1)


def _att_masks(i, k0):
    q_pos = i * ATT_BLK + lax.broadcasted_iota(jnp.int32, (ATT_BLK, 2 * ATT_BLK), 0)
    k_pos = k0 + lax.broadcasted_iota(jnp.int32, (ATT_BLK, 2 * ATT_BLK), 1)
    dist = q_pos - k_pos
    return (dist >= 0) & (dist <= ATT_SPAN)


def _head_lane_masks():
    lane = lax.broadcasted_iota(jnp.int32, (1, LANES), 1)
    return lane < HEAD_DIM, lane >= HEAD_DIM


def _att_specs(gi, n_rows):
    once = pl.Buffered(1)
    col = lambda at: pl.BlockSpec((n_rows, LANES), lambda hp: (0, at + hp), pipeline_mode=once)
    qkv = [col(gi * 3), col(9), col(12)]
    tabs = [pl.BlockSpec((n_rows, LANES), lambda hp: (0, 0), pipeline_mode=once)] * 3
    return qkv, tabs, col(0), pl.BlockSpec((n_rows, LANES), lambda hp: (0, hp))


def _sub_rows(d, n, r):
    return pl.ds(r, n, stride=d) if d > 1 else pl.ds(0, n)


def _att_load(q_ref, k_ref, v_ref, tabs, sub, qs, ks, vs):
    cos_t, lo_t, hi_t = tabs
    qs[...] = (_rope(q_ref[sub, :], cos_t, lo_t, hi_t) * (1.0 / math.sqrt(HEAD_DIM))).astype(BF16)
    ks[...] = _rope(k_ref[sub, :], cos_t, lo_t, hi_t).astype(BF16)
    vs[...] = v_ref[sub, :].astype(BF16)


def _att_fwd(proj, tabs, gi, name):
    n_rows = proj.shape[0]
    d = DILATIONS[gi]
    n = n_rows // d
    nb = n // ATT_BLK

    def body(q_ref, k_ref, v_ref, cos_ref, lo_ref, hi_ref, o_ref, l_ref, qs, ks, vs, o_s, l_s):
        m0, m1 = _head_lane_masks()

        def step(i, carry):
            k0 = pl.multiple_of(jnp.maximum(i - 1, 0) * ATT_BLK, ATT_BLK)
            q0 = pl.multiple_of(i * ATT_BLK, ATT_BLK)
            q = qs[pl.ds(q0, ATT_BLK), :]
            k = ks[pl.ds(k0, 2 * ATT_BLK), :]
            v = vs[pl.ds(k0, 2 * ATT_BLK), :]
            valid = _att_masks(i, k0)
            outs, lses = [], []
            for hm in (m0, m1):
                s = _nt(jnp.where(hm, q, jnp.zeros_like(q)), k)
                s = jnp.where(valid, s, -1e30)
                mx = jnp.max(s, axis=1, keepdims=True)
                p = jnp.exp(s - mx)
                den = jnp.sum(p, axis=1, keepdims=True)
                outs.append(_nn(p.astype(BF16), v) / den)
                lses.append(mx + jnp.log(den))
            o_s[pl.ds(q0, ATT_BLK), :] = jnp.where(m0, outs[0], outs[1])
            l_s[pl.ds(q0, ATT_BLK), :] = jnp.where(m0, lses[0], lses[1])
            return carry

        for r in range(d):
            sub = _sub_rows(d, n, r)
            _att_load(q_ref, k_ref, v_ref, (cos_ref[sub, :], lo_ref[sub, :], hi_ref[sub, :]), sub, qs, ks, vs)
            lax.fori_loop(0, nb, step, 0, unroll=2)
            o_ref[sub, :] = o_s[...]
            l_ref[sub, :] = l_s[...]

    qkv, tab_specs, _, head_out = _att_specs(gi, n_rows)
    out = jax.ShapeDtypeStruct((n_rows, KV_WIDTH), F32)
    return pl.pallas_call(
        body,
        name=name,
        grid=(3,),
        in_specs=qkv + tab_specs,
        out_specs=[head_out, head_out],
        out_shape=[out, out],
        scratch_shapes=[pltpu.VMEM((n, LANES), BF16)] * 3 + [pltpu.VMEM((n, LANES), F32)] * 2,
        compiler_params=_params(("parallel",)),
    )(proj, proj, proj, *tabs)


def _att_bwd(proj, tabs, o, l, do, dl, gi, name):
    n_rows = proj.shape[0]
    d = DILATIONS[gi]
    n = n_rows // d
    nb = n // ATT_BLK

    def body(q_ref, k_ref, v_ref, cos_ref, lo_ref, hi_ref, o_ref, l_ref, do_ref, dl_ref,
             dq_ref, dk_ref, dv_ref, qs, ks, vs, dq_s, dk_acc, dv_acc, *gathered):
        m0, m1 = _head_lane_masks()
        o_s, l_s, do_s, dl_s = gathered if d > 1 else (o_ref, l_ref, do_ref, dl_ref)

        def step(i, carry):
            k0 = pl.multiple_of(jnp.maximum(i - 1, 0) * ATT_BLK, ATT_BLK)
            q0 = pl.multiple_of(i * ATT_BLK, ATT_BLK)
            rows = pl.ds(q0, ATT_BLK)
            keys = pl.ds(k0, 2 * ATT_BLK)
            q, k, v = qs[rows, :], ks[keys, :], vs[keys, :]
            d_o, lse = do_s[rows, :], l_s[rows, :]
            o_do = o_s[rows, :] * d_o
            d_l = dl_s[rows, :]
            valid = _att_masks(i, k0)
            dq = jnp.zeros((ATT_BLK, LANES), F32)
            dk = jnp.zeros((2 * ATT_BLK, LANES), F32)
            dv = jnp.zeros((2 * ATT_BLK, LANES), F32)
            for hm in (m0, m1):
                qh, kh = jnp.where(hm, q, jnp.zeros_like(q)), jnp.where(hm, k, jnp.zeros_like(k))
                doh = jnp.where(hm, d_o, 0.0).astype(BF16)
                lse_h = jnp.max(jnp.where(hm, lse, -1e30), axis=1, keepdims=True)
                delta = jnp.sum(jnp.where(hm, o_do, 0.0), axis=1, keepdims=True)
                dlse = jnp.sum(jnp.where(hm, d_l, 0.0), axis=1, keepdims=True)
                s = jnp.where(valid, _nt(qh, k), -1e30)
                p = jnp.exp(s - lse_h)
                dv = dv + _tn(p.astype(BF16), doh)
                ds = (p * (_nt(doh, v) - delta + dlse)).astype(BF16)
                dq = dq + _nn(ds, kh)
                dk = dk + _tn(ds, qh)
            dq_s[rows, :] = dq * (1.0 / math.sqrt(HEAD_DIM))
            dk_acc[keys, :] += dk
            dv_acc[keys, :] += dv
            return carry

        for r in range(d):
            sub = _sub_rows(d, n, r)
            rot = (cos_ref[sub, :], lo_ref[sub, :], hi_ref[sub, :])
            _att_load(q_ref, k_ref, v_ref, rot, sub, qs, ks, vs)
            if d > 1:
                for dst, src in zip(gathered, (o_ref, l_ref, do_ref, dl_ref)):
                    dst[...] = src[sub, :]
            dk_acc[...] = jnp.zeros_like(dk_acc)
            dv_acc[...] = jnp.zeros_like(dv_acc)
            lax.fori_loop(0, nb, step, 0, unroll=2)
            dq_ref[sub, :] = _rope_transposed(dq_s[...], *rot)
            dk_ref[sub, :] = _rope_transposed(dk_acc[...], *rot)
            dv_ref[sub, :] = dv_acc[...]

    qkv, tab_specs, head_in, head_out = _att_specs(gi, n_rows)
    out = jax.ShapeDtypeStruct((n_rows, KV_WIDTH), F32)
    sub_f32 = pltpu.VMEM((n, LANES), F32)
    return pl.pallas_call(
        body,
        name=name,
        grid=(3,),
        in_specs=qkv + tab_specs + [head_in] * 4,
        out_specs=[head_out] * 3,
        out_shape=[out] * 3,
        scratch_shapes=[pltpu.VMEM((n, LANES), BF16)] * 3 + [sub_f32] * (3 if d == 1 else 7),
        compiler_params=_params(("parallel",)),
    )(proj, proj, proj, *tabs, o, l, do, dl)


def _expand_np():
    e = np.zeros((SSM_N, SSM_N * SSM_P), np.float32)
    for nn in range(SSM_N):
        e[nn, nn * SSM_P:(nn + 1) * SSM_P] = 1.0
    return e


def _ssm_prep_math(a_re, a_im, log_dt, b_re, b_im, expand):
    dt = jnp.exp(log_dt)
    mag = jnp.exp(a_re * dt)
    ab_re, ab_im = mag * jnp.cos(a_im * dt), mag * jnp.sin(a_im * dt)
    den = a_re * a_re + a_im * a_im
    num_re, num_im = ab_re - 1.0, ab_im
    co_re = (num_re * a_re + num_im * a_im) / den
    co_im = (num_im * a_re - num_re * a_im) / den
    hi = lax.Precision.HIGHEST
    co_re_x = jnp.dot(co_re, expand, precision=hi, preferred_element_type=F32)
    co_im_x = jnp.dot(co_im, expand, precision=hi, preferred_element_type=F32)
    bb_re = co_re_x * b_re - co_im_x * b_im
    bb_im = co_re_x * b_im + co_im_x * b_re
    return ab_re, ab_im, bb_re, bb_im


def _ssm_prep(a_re, a_im, log_dt, b_re, b_im, expand, name):
    def body(ar, ai, ld, br, bi, ex, o0, o1, o2, o3):
        outs = _ssm_prep_math(ar[...], ai[...], ld[...], br[...], bi[...], ex[...])
        for ref, v in zip((o0, o1, o2, o3), outs):
            ref[...] = v

    gn = jax.ShapeDtypeStruct((SSM_G, SSM_N), F32)
    gnp = jax.ShapeDtypeStruct((SSM_G, SSM_N * SSM_P), F32)
    return pl.pallas_call(body, name=name, out_shape=[gn, gn, gnp, gnp], compiler_params=_params())(
        a_re, a_im, log_dt, b_re, b_im, expand)


def _ssm_prep_bwd(a_re, a_im, log_dt, b_re, b_im, expand, cts, name):
    def body(ar, ai, ld, br, bi, ex, c0, c1, c2, c3, o0, o1, o2, o3, o4):
        ex_v = ex[...]
        _, vjp = jax.vjp(lambda *p: _ssm_prep_math(*p, ex_v), ar[...], ai[...], ld[...], br[...], bi[...])
        for ref, v in zip((o0, o1, o2, o3, o4), vjp((c0[...], c1[...], c2[...], c3[...]))):
            ref[...] = v

    gn = jax.ShapeDtypeStruct((SSM_G, SSM_N), F32)
    gnp = jax.ShapeDtypeStruct((SSM_G, SSM_N * SSM_P), F32)
    g1 = jax.ShapeDtypeStruct((SSM_G, 1), F32)
    return pl.pallas_call(body, name=name, out_shape=[gn, gn, g1, gnp, gnp], compiler_params=_params())(
        a_re, a_im, log_dt, b_re, b_im, expand, *cts)


def _block_diag_in(bb):
    t = bb.reshape(SSM_TILES, 8, SSM_N, SSM_P).transpose(0, 1, 3, 2)
    eye = jnp.eye(8, dtype=bb.dtype)
    return (t[:, :, :, None, :] * eye[None, :, None, :, None]).reshape(SSM_TILES, LANES, SSM_TILE_GN)


def _block_diag_in_grad(dblk):
    t = dblk.reshape(SSM_TILES, 8, SSM_P, 8, SSM_N)
    t = jnp.einsum("tapbn,ab->tapn", t, jnp.eye(8, dtype=dblk.dtype))
    return t.transpose(0, 1, 3, 2).reshape(SSM_G, SSM_N, SSM_P)


def _block_diag_out(cm):
    t = cm.reshape(SSM_TILES, 8, SSM_P, SSM_N).transpose(0, 1, 3, 2)
    eye = jnp.eye(8, dtype=cm.dtype)
    return (t[:, :, :, None, :] * eye[None, :, None, :, None]).reshape(SSM_TILES, SSM_TILE_GN, LANES)


def _block_diag_out_grad(dblk):
    t = dblk.reshape(SSM_TILES, 8, SSM_N, 8, SSM_P)
    t = jnp.einsum("tanbp,ab->tanp", t, jnp.eye(8, dtype=dblk.dtype))
    return t.transpose(0, 1, 3, 2).reshape(SSM_G, SSM_P, SSM_N)


def _cmul_add(a_re, a_im, s_re, s_im, b_re, b_im):
    return a_re * s_re - a_im * s_im + b_re, a_re * s_im + a_im * s_re + b_im


def _lane_tiled(a):
    return a.reshape(a.shape[0], SSM_TILES, LANES).transpose(1, 0, 2)


def _lane_untiled(a):
    return a.transpose(1, 0, 2).reshape(a.shape[1], SSM_WIDTH)


def _ssm_load_rows(src_ref, dst):
    for t in range(SSM_TILES):
        for i in range(SSM_SEG_LEN):
            dst[i * SSM_SEGS:(i + 1) * SSM_SEGS, t * LANES:(t + 1) * LANES] = (
                src_ref[t, pl.ds(i, SSM_SEGS, stride=SSM_SEG_LEN), :])


def _ssm_store_rows(src, dst_ref):
    for t in range(SSM_TILES):
        for i in range(SSM_SEG_LEN):
            dst_ref[t, pl.ds(i, SSM_SEGS, stride=SSM_SEG_LEN), :] = (
                src[i * SSM_SEGS:(i + 1) * SSM_SEGS, t * LANES:(t + 1) * LANES])


def _ssm_powers(ab_re_ref, ab_im_ref, pw_re, pw_im):
    a_re, a_im = ab_re_ref[...], ab_im_ref[...]
    p_re, p_im = a_re, a_im
    for i in range(SSM_SEG_LEN):
        pw_re[i:i + 1, :] = p_re
        pw_im[i:i + 1, :] = p_im
        p_re, p_im = _cmul_add(a_re, a_im, p_re, p_im, 0.0, 0.0)


def _ssm_input_proj(u_s, bblk_re_ref, bblk_im_ref, s_re, s_im):
    for t in range(SSM_TILES):
        ub = u_s[:, t * LANES:(t + 1) * LANES].astype(BF16)
        cols = slice(t * SSM_TILE_GN, (t + 1) * SSM_TILE_GN)
        s_re[:, cols] = _nn(ub, bblk_re_ref[t])
        s_im[:, cols] = _nn(ub, bblk_im_ref[t])


def _ssm_scan(ab_re_ref, ab_im_ref, s_re, s_im, init_re, init_im, conj, reverse):
    sign = -1.0 if conj else 1.0
    for t in range(SSM_TILES):
        cols = slice(t * SSM_TILE_GN, (t + 1) * SSM_TILE_GN)
        a_re = jnp.broadcast_to(ab_re_ref[:, cols], (SSM_SEGS, SSM_TILE_GN))
        a_im = jnp.broadcast_to(ab_im_ref[:, cols], (SSM_SEGS, SSM_TILE_GN)) * sign
        if init_re is None:
            st = (jnp.zeros((SSM_SEGS, SSM_TILE_GN), F32),) * 2
        else:
            st = (init_re[:, cols], init_im[:, cols])

        def step(i, st, cols=cols, a_re=a_re, a_im=a_im):
            idx = (SSM_SEG_LEN - 1 - i) if reverse else i
            rows = pl.ds(pl.multiple_of(idx * SSM_SEGS, SSM_SEGS), SSM_SEGS)
            n_re, n_im = _cmul_add(a_re, a_im, st[0], st[1], s_re[rows, cols], s_im[rows, cols])
            s_re[rows, cols] = n_re
            s_im[rows, cols] = n_im
            return n_re, n_im
        lax.fori_loop(0, SSM_SEG_LEN, step, st, unroll=4)


def _ssm_fixup(pw_re, pw_im, s_re, s_im, cin_re, cin_im, conj, reverse):
    sign = -1.0 if conj else 1.0
    c_re, c_im = cin_re[...], cin_im[...]

    def step(i, c):
        k = (SSM_SEG_LEN - 1 - i) if reverse else i
        rows = pl.ds(pl.multiple_of(i * SSM_SEGS, SSM_SEGS), SSM_SEGS)
        p_re = jnp.broadcast_to(pw_re[pl.ds(k, 1), :], (SSM_SEGS, SSM_GN))
        p_im = jnp.broadcast_to(pw_im[pl.ds(k, 1), :], (SSM_SEGS, SSM_GN)) * sign
        n_re, n_im = _cmul_add(p_re, p_im, c_re, c_im, s_re[rows, :], s_im[rows, :])
        s_re[rows, :] = n_re
        s_im[rows, :] = n_im
        return c
    lax.fori_loop(0, SSM_SEG_LEN, step, 0)


def _ssm_fwd(u, ab_re, ab_im, bblk_re, bblk_im, cblk_re, cblk_im, d_row, name):
    n_rows = u.shape[1]
    nchunk = n_rows // SSM_CHUNK
    last = SSM_SEG_LEN - 1

    def body(u_ref, ar_ref, ai_ref, br_ref, bi_ref, cr_ref, ci_ref, d_ref, y_ref, cin_re_ref, cin_im_ref,
             u_s, s_re, s_im, pw_re, pw_im, st_re, st_im):
        @pl.when(pl.program_id(0) == 0)
        def _():
            _ssm_powers(ar_ref, ai_ref, pw_re, pw_im)
            st_re[...] = jnp.zeros_like(st_re)
            st_im[...] = jnp.zeros_like(st_im)

        _ssm_load_rows(u_ref, u_s)
        _ssm_input_proj(u_s, br_ref, bi_ref, s_re, s_im)
        _ssm_scan(ar_ref, ai_ref, s_re, s_im, None, None, conj=False, reverse=False)
        p_re, p_im = pw_re[last:last + 1, :], pw_im[last:last + 1, :]
        c_re, c_im = st_re[...], st_im[...]
        for j in range(SSM_SEGS):
            cin_re_ref[j:j + 1, :] = c_re
            cin_im_ref[j:j + 1, :] = c_im
            row = last * SSM_SEGS + j
            c_re, c_im = _cmul_add(p_re, p_im, c_re, c_im, s_re[row:row + 1, :], s_im[row:row + 1, :])
        st_re[...] = c_re
        st_im[...] = c_im
        _ssm_fixup(pw_re, pw_im, s_re, s_im, cin_re_ref, cin_im_ref, conj=False, reverse=False)
        for t in range(SSM_TILES):
            cols = slice(t * SSM_TILE_GN, (t + 1) * SSM_TILE_GN)
            lanes = slice(t * LANES, (t + 1) * LANES)
            y = _nn(s_re[:, cols].astype(BF16), cr_ref[t]) - _nn(s_im[:, cols].astype(BF16), ci_ref[t])
            u_s[:, lanes] = y + d_ref[:, lanes] * u_s[:, lanes]
        _ssm_store_rows(u_s, y_ref)

    whole2 = lambda a: pl.BlockSpec(a.shape, lambda c: (0, 0))
    whole3 = lambda a: pl.BlockSpec(a.shape, lambda c: (0, 0, 0))
    chunk = pl.BlockSpec((SSM_TILES, SSM_CHUNK, LANES), lambda c: (0, c, 0))
    seg = pl.BlockSpec((SSM_SEGS, SSM_GN), lambda c: (c, 0))
    seg_shape = jax.ShapeDtypeStruct((nchunk * SSM_SEGS, SSM_GN), F32)
    return pl.pallas_call(
        body,
        name=name,
        grid=(nchunk,),
        in_specs=[chunk, whole2(ab_re), whole2(ab_im), whole3(bblk_re), whole3(bblk_im), whole3(cblk_re),
                  whole3(cblk_im), whole2(d_row)],
        out_specs=[chunk, seg, seg],
        out_shape=[jax.ShapeDtypeStruct((SSM_TILES, n_rows, LANES), F32), seg_shape, seg_shape],
        scratch_shapes=[pltpu.VMEM((SSM_CHUNK, SSM_WIDTH), F32), pltpu.VMEM((SSM_CHUNK, SSM_GN), F32),
                        pltpu.VMEM((SSM_CHUNK, SSM_GN), F32), pltpu.VMEM((SSM_SEG_LEN, SSM_GN), F32),
                        pltpu.VMEM((SSM_SEG_LEN, SSM_GN), F32), pltpu.VMEM((1, SSM_GN), F32),
                        pltpu.VMEM((1, SSM_GN), F32)],
        compiler_params=_params(("arbitrary",)),
    )(u, ab_re, ab_im, bblk_re, bblk_im, cblk_re, cblk_im, d_row)


def _ssm_bwd(u, dy, cin_re, cin_im, ab_re, ab_im, bblk_re, bblk_im, cblk_re, cblk_im, d_row, name):
    n_rows = u.shape[1]
    nchunk = n_rows // SSM_CHUNK

    def body(u_ref, dy_ref, cin_re_ref, cin_im_ref, ar_ref, ai_ref, br_ref, bi_ref, cr_ref, ci_ref, d_ref,
             du_ref, dar_ref, dai_ref, dbr_ref, dbi_ref, dcr_ref, dci_ref, dd_ref,
             u_s, dy_s, s_re, s_im, q_re, q_im, pw_re, pw_im, qst_re, qst_im, qin_re, qin_im):
        @pl.when(pl.program_id(0) == 0)
        def _():
            _ssm_powers(ar_ref, ai_ref, pw_re, pw_im)
            qst_re[...] = jnp.zeros_like(qst_re)
            qst_im[...] = jnp.zeros_like(qst_im)
            for ref in (dar_ref, dai_ref, dbr_ref, dbi_ref, dcr_ref, dci_ref, dd_ref):
                ref[...] = jnp.zeros_like(ref)

        _ssm_load_rows(u_ref, u_s)
        _ssm_load_rows(dy_ref, dy_s)
        _ssm_input_proj(u_s, br_ref, bi_ref, s_re, s_im)
        _ssm_scan(ar_ref, ai_ref, s_re, s_im, cin_re_ref, cin_im_ref, conj=False, reverse=False)
        for t in range(SSM_TILES):
            cols = slice(t * SSM_TILE_GN, (t + 1) * SSM_TILE_GN)
            dyb = dy_s[:, t * LANES:(t + 1) * LANES].astype(BF16)
            q_re[:, cols] = _nt(dyb, cr_ref[t])
            q_im[:, cols] = -_nt(dyb, ci_ref[t])
            dcr_ref[t] += _tn(s_re[:, cols].astype(BF16), dyb)
            dci_ref[t] -= _tn(s_im[:, cols].astype(BF16), dyb)
        _ssm_scan(ar_ref, ai_ref, q_re, q_im, None, None, conj=True, reverse=True)
        last = SSM_SEG_LEN - 1
        p_re, p_im = pw_re[last:last + 1, :], -pw_im[last:last + 1, :]
        c_re, c_im = qst_re[...], qst_im[...]
        for j in reversed(range(SSM_SEGS)):
            qin_re[j:j + 1, :] = c_re
            qin_im[j:j + 1, :] = c_im
            c_re, c_im = _cmul_add(p_re, p_im, c_re, c_im, q_re[j:j + 1, :], q_im[j:j + 1, :])
        qst_re[...] = c_re
        qst_im[...] = c_im
        _ssm_fixup(pw_re, pw_im, q_re, q_im, qin_re, qin_im, conj=True, reverse=True)
        for t in range(SSM_TILES):
            cols = slice(t * SSM_TILE_GN, (t + 1) * SSM_TILE_GN)

            def step(i, acc, cols=cols):
                rows = pl.ds(pl.multiple_of(i * SSM_SEGS, SSM_SEGS), SSM_SEGS)
                prev = pl.ds(pl.multiple_of((i - 1) * SSM_SEGS, SSM_SEGS), SSM_SEGS)
                qr, qi = q_re[rows, cols], q_im[rows, cols]
                sr, si = s_re[prev, cols], s_im[prev, cols]
                return acc[0] + qr * sr + qi * si, acc[1] + qi * sr - qr * si

            qr, qi = q_re[0:SSM_SEGS, cols], q_im[0:SSM_SEGS, cols]
            sr, si = cin_re_ref[:, cols], cin_im_ref[:, cols]
            acc = lax.fori_loop(1, SSM_SEG_LEN, step, (qr * sr + qi * si, qi * sr - qr * si))
            dar_ref[:, cols] += jnp.sum(acc[0], axis=0, keepdims=True)
            dai_ref[:, cols] += jnp.sum(acc[1], axis=0, keepdims=True)
        for t in range(SSM_TILES):
            cols = slice(t * SSM_TILE_GN, (t + 1) * SSM_TILE_GN)
            lanes = slice(t * LANES, (t + 1) * LANES)
            qrb, qib = q_re[:, cols].astype(BF16), q_im[:, cols].astype(BF16)
            u_t, dy_t = u_s[:, lanes], dy_s[:, lanes]
            ub = u_t.astype(BF16)
            dbr_ref[t] += _tn(ub, qrb)
            dbi_ref[t] += _tn(ub, qib)
            dd_ref[:, lanes] += jnp.sum(dy_t * u_t, axis=0, keepdims=True)
            u_s[:, lanes] = _nt(qrb, br_ref[t]) + _nt(qib, bi_ref[t]) + dy_t * d_ref[:, lanes]
        _ssm_store_rows(u_s, du_ref)

    whole2 = lambda a: pl.BlockSpec(a.shape, lambda c: (0, 0))
    whole3 = lambda a: pl.BlockSpec(a.shape, lambda c: (0, 0, 0))
    chunk = pl.BlockSpec((SSM_TILES, SSM_CHUNK, LANES), lambda c: (0, nchunk - 1 - c, 0))
    seg = pl.BlockSpec((SSM_SEGS, SSM_GN), lambda c: (nchunk - 1 - c, 0))
    gn_row = jax.ShapeDtypeStruct((1, SSM_GN), F32)
    b_shape = jax.ShapeDtypeStruct((SSM_TILES, LANES, SSM_TILE_GN), F32)
    c_shape = jax.ShapeDtypeStruct((SSM_TILES, SSM_TILE_GN, LANES), F32)
    d_shape = jax.ShapeDtypeStruct((1, SSM_WIDTH), F32)
    big = pltpu.VMEM((SSM_CHUNK, SSM_GN), F32)
    return pl.pallas_call(
        body,
        name=name,
        grid=(nchunk,),
        in_specs=[chunk, chunk, seg, seg, whole2(ab_re), whole2(ab_im), whole3(bblk_re), whole3(bblk_im),
                  whole3(cblk_re), whole3(cblk_im), whole2(d_row)],
        out_specs=[chunk, whole2(ab_re), whole2(ab_im), whole3(bblk_re), whole3(bblk_im), whole3(cblk_re),
                   whole3(cblk_im), whole2(d_row)],
        out_shape=[jax.ShapeDtypeStruct((SSM_TILES, n_rows, LANES), F32), gn_row, gn_row, b_shape, b_shape, c_shape,
                   c_shape, d_shape],
        scratch_shapes=[pltpu.VMEM((SSM_CHUNK, SSM_WIDTH), F32), pltpu.VMEM((SSM_CHUNK, SSM_WIDTH), F32),
                        big, big, big, big,
                        pltpu.VMEM((SSM_SEG_LEN, SSM_GN), F32), pltpu.VMEM((SSM_SEG_LEN, SSM_GN), F32),
                        pltpu.VMEM((1, SSM_GN), F32), pltpu.VMEM((1, SSM_GN), F32),
                        pltpu.VMEM((SSM_SEGS, SSM_GN), F32), pltpu.VMEM((SSM_SEGS, SSM_GN), F32)],
        compiler_params=_params(("arbitrary",)),
    )(u, dy, cin_re, cin_im, ab_re, ab_im, bblk_re, bblk_im, cblk_re, cblk_im, d_row)


def _mlp_fwd(h2, w1, w2, tm, tf, name):
    n_rows, dm = h2.shape
    dff = w1.shape[1]

    def body(h_ref, w1_ref, w2_ref, a_ref, y_ref):
        a = _nn(h_ref[...], w1_ref[...])
        a_ref[...] = a.astype(BF16)
        r = jnp.maximum(a, 0.0)
        part = _nn((r * r).astype(BF16), w2_ref[...])
        j = pl.program_id(1)

        @pl.when(j == 0)
        def _():
            y_ref[...] = part

        @pl.when(j > 0)
        def _():
            y_ref[...] += part

    return pl.pallas_call(
        body,
        name=name,
        grid=(n_rows // tm, dff // tf),
        in_specs=[pl.BlockSpec((tm, dm), lambda i, j: (i, 0)), pl.BlockSpec((dm, tf), lambda i, j: (0, j)),
                  pl.BlockSpec((tf, dm), lambda i, j: (j, 0))],
        out_specs=[pl.BlockSpec((tm, tf), lambda i, j: (i, j)), pl.BlockSpec((tm, dm), lambda i, j: (i, 0))],
        out_shape=[jax.ShapeDtypeStruct((n_rows, dff), BF16), jax.ShapeDtypeStruct((n_rows, dm), F32)],
        compiler_params=_params(("parallel", "arbitrary")),
    )(h2, w1, w2)


def _mlp_bwd(dy, h2, a, w2, tm, tf, name):
    n_rows, dm = h2.shape
    dff = a.shape[1]
    per_chip = dff // N_CHIPS // tf

    def body(dy_ref, h_ref, a_ref, w2_ref, da_ref, dw2_ref, dw1_ref):
        dyb = dy_ref[...]
        r = jnp.maximum(a_ref[...].astype(F32), 0.0)
        da = (_nt(dyb, w2_ref[...]) * (2.0 * r)).astype(BF16)
        da_ref[...] = da
        p2 = _tn((r * r).astype(BF16), dyb)
        p1 = _tn(h_ref[...], da)
        i = pl.program_id(1)

        @pl.when(i == 0)
        def _():
            dw2_ref[...] = p2
            dw1_ref[...] = p1

        @pl.when(i > 0)
        def _():
            dw2_ref[...] += p2
            dw1_ref[...] += p1

    return pl.pallas_call(
        body,
        name=name,
        grid=(dff // tf, n_rows // tm),
        in_specs=[pl.BlockSpec((tm, dm), lambda j, i: (i, 0)), pl.BlockSpec((tm, dm), lambda j, i: (i, 0)),
                  pl.BlockSpec((tm, tf), lambda j, i: (i, j)), pl.BlockSpec((tf, dm), lambda j, i: (j, 0))],
        out_specs=[pl.BlockSpec((tm, tf), lambda j, i: (i, j)), pl.BlockSpec((tf, dm), lambda j, i: (j, 0)),
                   pl.BlockSpec((None, dm, tf), lambda j, i: (j // per_chip, 0, j % per_chip))],
        out_shape=[jax.ShapeDtypeStruct((n_rows, dff), BF16), jax.ShapeDtypeStruct((dff, dm), F32),
                   jax.ShapeDtypeStruct((N_CHIPS, dm, dff // N_CHIPS), F32)],
        compiler_params=_params(("parallel", "arbitrary")),
    )(dy, h2, a, w2)


def _local_step(x, pos_col, mod, target, wts, small, hooks=None):
    n_rows = x.shape[0]
    sh1, sc1, gt1, sh2, sc2, gt2 = (mod[:, i * D_MODEL:(i + 1) * D_MODEL] for i in range(N_MOD))
    tm = 256
    d_acc = (1, D_MODEL)

    (h1,) = _rowwise(lambda r, c: [_mod_norm(r[0], *c)], [x], [small["g_pre_mix"], sc1, sh1],
                     [(D_MODEL, BF16)], [], tm, "pre_mix_fwd")
    proj = _matmul(h1, wts["w_in"], "nn", F32, 512, 1408, 2048, "in_proj")

    freqs = ROPE_THETA ** (-jnp.arange(0, ROT_DIM, 2, dtype=F32) / ROT_DIM)
    freq_lane = jnp.tile(freqs, LANES // (ROT_DIM // 2))[None, :]
    tabs = _rope_tables(pos_col, freq_lane, "rope_tables")
    att = [_att_fwd(proj, tabs, gi, f"att_fwd_{gi}") for gi in range(3)]

    expand = jnp.asarray(_expand_np())
    b_re2, b_im2 = small["ssm_b_re"].reshape(SSM_G, -1), small["ssm_b_im"].reshape(SSM_G, -1)
    log_dt = small["ssm_log_dt"].reshape(SSM_G, 1)
    prep_in = (small["ssm_a_re"], small["ssm_a_im"], log_dt, b_re2, b_im2, expand)
    ab_re, ab_im, bb_re, bb_im = _ssm_prep(*prep_in, "ssm_prep")
    ab_re_row, ab_im_row = ab_re.reshape(1, SSM_GN), ab_im.reshape(1, SSM_GN)
    bblk = [_block_diag_in(t.reshape(SSM_G, SSM_N, SSM_P)).astype(BF16) for t in (bb_re, bb_im)]
    cblk = [_block_diag_out(small[k]).astype(BF16) for k in ("ssm_c_re", "ssm_c_im")]
    d_row = small["ssm_d"].reshape(1, SSM_WIDTH)
    u = _lane_tiled(proj[:, IN_WIDTH - SSM_WIDTH:])
    y_ssm, cin_re, cin_im = _ssm_fwd(u, ab_re_row, ab_im_row, *bblk, *cblk, d_row, "ssm_fwd")
    y_ssm = _lane_untiled(y_ssm)

    def mixers_out(r, c):
        w_glu, b_glu, g_att, g_ssm = c
        att_n = _att_mix(*r[:6], g_att)
        y2 = _gelu(r[6])
        z = _nn(y2.astype(BF16), w_glu) + b_glu
        return [jnp.concatenate([att_n.astype(BF16), _glu_out(y2, z, g_ssm).astype(BF16)], axis=1)]

    att_rows = [a[0] for a in att] + [a[1] for a in att]
    mix_consts = [wts["w_glu"], small["b_glu"], small["g_attn_out"], small["g_ssm_out"]]
    (cat,) = _rowwise(mixers_out, att_rows + [y_ssm], mix_consts, [(OUT_IN_WIDTH, BF16)], [], tm, "mixers_out_fwd")
    mix = _matmul(cat, wts["w_out"], "nn", F32, 512, 1024, 1280, "out_proj")

    post_consts = [small["g_post_mix"], gt1, small["g_pre_mlp"], sc2, sh2]
    x1, h2 = _rowwise(lambda r, c: list(_post_mix(r[0], r[1], *c)), [x, mix], post_consts,
                      [(D_MODEL, F32), (D_MODEL, BF16)], [], tm, "post_mix_fwd")
    w_mlp_in, w_mlp_out = (wts["w_mlp_in"], wts["w_mlp_out"]) if hooks is None else hooks.mlp_weights(h2)
    a_mlp, y_mlp = _mlp_fwd(h2, w_mlp_in, w_mlp_out, 1024, 512, "mlp_fwd")

    def loss_head(r, c):
        x1_v, y_v, t_v = r
        g, gt = c
        fn = lambda y_, g_, gt_: gt_ * _rms(y_, g_)
        out, vjp = jax.vjp(fn, y_v, g, gt)
        err = x1_v + out - t_v
        dx2 = err * (1.0 / D_MODEL)
        dy, dg, dgt = vjp(dx2)
        loss = 0.5 * jnp.sum(jnp.sum(err * err, axis=1, keepdims=True), axis=0, keepdims=True) * (1.0 / D_MODEL)
        return [dx2, dy, loss, dg, dgt]

    dx2, dy_mlp, loss, dg_post_mlp, dgt2 = _rowwise(
        loss_head, [x1, y_mlp, target], [small["g_post_mlp"], gt2],
        [(D_MODEL, F32), (D_MODEL, BF16)], [(1, 1), d_acc, d_acc], tm, "loss_head")

    da_mlp, dw_mlp_out, dw_mlp_in = _mlp_bwd(dy_mlp, h2, a_mlp, w_mlp_out, 1024, 512, "mlp_bwd")
    dw_mlp_out = dw_mlp_out.reshape(dw_mlp_in.shape)
    sent = None if hooks is None else hooks.mlp_grads_to_sibling(dw_mlp_in, dw_mlp_out)
    dh2 = _matmul(da_mlp, w_mlp_in, "nt", F32, 512, 1024, 2048, "mlp_in_bwd", after=sent)
    sent = None if hooks is None else hooks.mlp_grads_to_chips(dh2)

    def post_mix_bwd(r, c):
        x_v, mix_v, dx1_v, dh2_v = r
        _, vjp = jax.vjp(_post_mix, x_v, mix_v, *c)
        return list(vjp((dx1_v, dh2_v)))

    post_consts_bwd = post_consts if sent is None else [_tie(post_consts[0], sent)] + post_consts[1:]
    dx_a, dmix, dg_post_mix, dgt1, dg_pre_mlp, dsc2, dsh2 = _rowwise(
        post_mix_bwd, [x, mix, dx2, dh2], post_consts_bwd, [(D_MODEL, F32), (D_MODEL, BF16)], [d_acc] * 5, tm,
        "post_mix_bwd")

    dcat = _matmul(dmix, wts["w_out"], "nt", F32, 512, 1280, 2048, "out_proj_bwd")
    dw_out = _matmul(cat, dmix, "tn", F32, 1280, 1024, 512, "out_proj_wgrad")

    def mixers_out_bwd(r, c):
        w_glu, b_glu, g_att, g_ssm = c
        dcat_v = r[7]
        _, vjp_att = jax.vjp(_att_mix, *r[:6], g_att)
        *d_ol, dg_att = vjp_att(dcat_v[:, :KV_WIDTH])
        y2, vjp_gelu = jax.vjp(_gelu, r[6])
        y2b = y2.astype(BF16)
        z = _nn(y2b, w_glu) + b_glu
        _, vjp_glu = jax.vjp(_glu_out, y2, z, g_ssm)
        dy2, dz, dg_ssm = vjp_glu(dcat_v[:, KV_WIDTH:])
        dzb = dz.astype(BF16)
        (dy,) = vjp_gelu(dy2 + _nt(dzb, w_glu))
        return d_ol + [dy, dg_att, _tn(y2b, dzb), jnp.sum(dz, axis=0, keepdims=True), dg_ssm]

    *d_att, dy_ssm, dg_attn_out, dw_glu, db_glu, dg_ssm_out = _rowwise(
        mixers_out_bwd, att_rows + [y_ssm, dcat], mix_consts,
        [(KV_WIDTH, F32)] * 6 + [(SSM_WIDTH, F32)],
        [(1, KV_WIDTH), (SSM_WIDTH, SSM_WIDTH), (1, SSM_WIDTH), (1, SSM_WIDTH)], tm, "mixers_out_bwd")

    du, dab_re, dab_im, dbblk_re, dbblk_im, dcblk_re, dcblk_im, dd_row = _ssm_bwd(
        u, _lane_tiled(dy_ssm), cin_re, cin_im, ab_re_row, ab_im_row, *bblk, *cblk, d_row, "ssm_bwd")
    du = _lane_untiled(du)
    prep_cts = (dab_re.reshape(SSM_G, SSM_N), dab_im.reshape(SSM_G, SSM_N),
                _block_diag_in_grad(dbblk_re).reshape(SSM_G, -1), _block_diag_in_grad(dbblk_im).reshape(SSM_G, -1))
    da_re, da_im, dlog_dt, db_re, db_im = _ssm_prep_bwd(*prep_in, prep_cts, "ssm_prep_bwd")

    dqkv = [_att_bwd(proj, tabs, att[gi][0], att[gi][1], d_att[gi], d_att[3 + gi], gi, f"att_bwd_{gi}")
            for gi in range(3)]

    def gather_dproj(r, c):
        dq = [r[0], r[3], r[6]]
        dk = r[1] + r[4] + r[7]
        dv = r[2] + r[5] + r[8]
        return [jnp.concatenate([t.astype(BF16) for t in dq + [dk, dv, r[9]]], axis=1)]

    (dproj,) = _rowwise(gather_dproj, [t for g in dqkv for t in g] + [du], [], [(IN_WIDTH, BF16)], [], tm,
                        "gather_dproj")
    dh1 = _matmul(dproj, wts["w_in"], "nt", F32, 512, 1024, 2816, "in_proj_bwd")
    dw_in = _matmul(h1, dproj, "tn", F32, 1024, 1408, 512, "in_proj_wgrad")

    def pre_mix_bwd(r, c):
        x_v, dh1_v, dxa_v = r
        _, vjp = jax.vjp(_mod_norm, x_v, *c)
        dx, dg, dsc, dsh = vjp(dh1_v)
        return [dx + dxa_v, dg, dsc, dsh]

    grad_x, dg_pre_mix, dsc1, dsh1 = _rowwise(
        pre_mix_bwd, [x, dh1, dx_a], [small["g_pre_mix"], sc1, sh1], [(D_MODEL, F32)], [d_acc] * 3, tm, "pre_mix_bwd")

    dmod = jnp.concatenate([dsh1, dsc1, dgt1, dsh2, dsc2, dgt2], axis=1)
    big = dict(w_in=dw_in, w_out=dw_out, w_mlp_in=dw_mlp_in, w_mlp_out=dw_mlp_out, w_glu=dw_glu)
    small_g = dict(
        g_pre_mix=dg_pre_mix, g_post_mix=dg_post_mix, ssm_a_re=da_re, ssm_a_im=da_im,
        ssm_log_dt=dlog_dt.reshape(1, SSM_G), ssm_b_re=db_re.reshape(SSM_G, SSM_N, SSM_P),
        ssm_b_im=db_im.reshape(SSM_G, SSM_N, SSM_P), ssm_c_re=_block_diag_out_grad(dcblk_re),
        ssm_c_im=_block_diag_out_grad(dcblk_im), ssm_d=dd_row.reshape(SSM_G, SSM_P), b_glu=db_glu,
        g_attn_out=dg_attn_out, g_ssm_out=dg_ssm_out, g_pre_mlp=dg_pre_mlp, g_post_mlp=dg_post_mlp)
    return loss, grad_x, dmod, big, small_g


MESH_ID = pl.DeviceIdType.MESH
N_DEV = 8
N_CHIPS = 4
HBM_SPEC = pl.BlockSpec(memory_space=pltpu.HBM)


def _place():
    x, y, c = lax.axis_index("x"), lax.axis_index("y"), lax.axis_index("c")
    other_chips = [(1 - x, y), (x, 1 - y), (1 - x, 1 - y)]
    return x, y, c, other_chips


def _half_rows(index, half):
    return pl.ds(pl.multiple_of(index * half, ROW_PAD), half)


def _remote(src, dst, send_sem, recv_sem, dev):
    return pltpu.make_async_remote_copy(src_ref=src, dst_ref=dst, send_sem=send_sem, recv_sem=recv_sem,
                                        device_id=dev, device_id_type=MESH_ID)


def _all_gather8(block, name):
    m_per, n = block.shape

    def body(x_ref, out_ref, send_sems, recv_sems, local_sem):
        x, y, c, chips = _place()
        me, sibling = (x, y, c), (x, y, 1 - c)

        def rows(px, py, pc):
            return out_ref.at[pl.ds((4 * px + 2 * py + pc) * m_per, m_per), :]

        def copy(k, blk, to, src=None):
            return _remote(rows(*blk) if src is None else src, rows(*blk), send_sems.at[k], recv_sems.at[k], to)

        mine = pltpu.make_async_copy(x_ref, rows(*me), local_sem)
        mine.start()
        first = [copy(0, me, sibling, src=x_ref)]
        first += [copy(1 + j, me, (*chip, c), src=x_ref) for j, chip in enumerate(chips)]
        for cp in first:
            cp.start()
        passed = [copy(4 + j, (*chip, c), sibling) for j, chip in enumerate(chips)]
        for j, chip in enumerate(chips):
            copy(1 + j, (*chip, c), me).wait_recv()
            passed[j].start()
        copy(0, sibling, me).wait_recv()
        for j, chip in enumerate(chips):
            copy(4 + j, (*chip, 1 - c), me).wait_recv()
        for cp in first + passed:
            cp.wait_send()
        mine.wait()

    return pl.pallas_call(
        body,
        name=name,
        out_shape=jax.ShapeDtypeStruct((N_DEV * m_per, n), block.dtype),
        in_specs=[pl.BlockSpec(memory_space=pltpu.VMEM)],
        out_specs=pl.BlockSpec(memory_space=pltpu.VMEM),
        scratch_shapes=[pltpu.SemaphoreType.DMA((7,)), pltpu.SemaphoreType.DMA((7,)), pltpu.SemaphoreType.DMA],
        compiler_params=_params(),
    )(block)


def _weight_gather(shards, name):
    n = len(shards)
    shapes = [s.shape for s in shards]

    def body(*refs):
        ins, outs = refs[:n], refs[n:2 * n]
        send, recv, fsend, frecv = refs[2 * n:]
        x, y, c, chips = _place()
        k_me = 2 * x + y
        sibling = (x, y, 1 - c)
        pending = []
        for a in range(n):
            half = shapes[a][0] // 2
            mine = _half_rows(c, half)
            for j, chip in enumerate(chips):
                cp = _remote(ins[a].at[mine, :], outs[a].at[k_me, mine, :], send.at[a, j], recv.at[a, j], (*chip, c))
                cp.start()
                pending.append(cp.wait_send)
        for a in range(n):
            half = shapes[a][0] // 2
            for j, (px, py) in enumerate(chips):
                piece = outs[a].at[2 * px + py, _half_rows(c, half), :]
                _remote(piece, piece, send.at[a, j], recv.at[a, j], (px, py, c)).wait_recv()
                fw = _remote(piece, piece, fsend.at[a, j], frecv.at[a, j], sibling)
                fw.start()
                pending.append(fw.wait_send)
        for a in range(n):
            half = shapes[a][0] // 2
            for j, (px, py) in enumerate(chips):
                piece = outs[a].at[2 * px + py, _half_rows(1 - c, half), :]
                _remote(piece, piece, fsend.at[a, j], frecv.at[a, j], sibling).wait_recv()
        for wait in pending:
            wait()

    sems = pltpu.SemaphoreType.DMA((n, 3))
    return pl.pallas_call(
        body,
        name=name,
        out_shape=[jax.ShapeDtypeStruct((N_CHIPS,) + s, BF16) for s in shapes],
        in_specs=[HBM_SPEC] * n,
        out_specs=[HBM_SPEC] * n,
        scratch_shapes=[sems, sems, sems, sems],
        compiler_params=_params(),
    )(*shards)


def _sibling_halves(stacks, name):
    n = len(stacks)
    shapes = [s.shape for s in stacks]

    def body(*refs):
        ins, outs = refs[:n], refs[n:2 * n]
        send, recv = refs[2 * n:]
        x, y, c, _ = _place()
        copies = []
        for a in range(n):
            half = shapes[a][1] // 2
            cp = _remote(ins[a].at[:, _half_rows(1 - c, half), :], outs[a], send.at[a], recv.at[a], (x, y, 1 - c))
            cp.start()
            copies.append(cp)
        for cp in copies:
            cp.wait()

    return pl.pallas_call(
        body,
        name=name,
        out_shape=[jax.ShapeDtypeStruct((N_CHIPS, s[1] // 2, s[2]), F32) for s in shapes],
        in_specs=[HBM_SPEC] * n,
        out_specs=[HBM_SPEC] * n,
        scratch_shapes=[pltpu.SemaphoreType.DMA((n,)), pltpu.SemaphoreType.DMA((n,))],
        compiler_params=_params(),
    )(*stacks)


def _chip_exchange(parts, name):
    n = len(parts)
    shapes = [p.shape for p in parts]

    def body(*refs):
        ins, outs = refs[:n], refs[n:2 * n]
        send, recv = refs[2 * n:]
        x, y, c, chips = _place()
        copies = []
        for a in range(n):
            for j, (px, py) in enumerate(chips):
                cp = _remote(ins[a].at[2 * px + py], outs[a].at[j], send.at[a, j], recv.at[a, j], (px, py, c))
                cp.start()
                copies.append(cp)
        for cp in copies:
            cp.wait()

    return pl.pallas_call(
        body,
        name=name,
        out_shape=[jax.ShapeDtypeStruct((3,) + s[1:], BF16) for s in shapes],
        in_specs=[HBM_SPEC] * n,
        out_specs=[HBM_SPEC] * n,
        scratch_shapes=[pltpu.SemaphoreType.DMA((n, 3)), pltpu.SemaphoreType.DMA((n, 3))],
        compiler_params=_params(),
    )(*parts)


def _sibling_swap(halves, name):
    n = len(halves)
    shapes = [h.shape for h in halves]

    def body(*refs):
        ins, outs = refs[:n], refs[n:2 * n]
        send, recv = refs[2 * n:]
        x, y, c, _ = _place()
        pending = []
        for a in range(n):
            half = shapes[a][0]
            mine = outs[a].at[_half_rows(c, half), :]
            cp = _remote(ins[a], mine, send.at[a], recv.at[a], (x, y, 1 - c))
            cp.start()
            pending.append(cp.wait_send)
        for a in range(n):
            half = shapes[a][0]
            theirs = outs[a].at[_half_rows(1 - c, half), :]
            _remote(theirs, theirs, send.at[a], recv.at[a], (x, y, 1 - c)).wait_recv()
        for wait in pending:
            wait()

    return pl.pallas_call(
        body,
        name=name,
        out_shape=[jax.ShapeDtypeStruct((2 * s[0], s[1]), F32) for s in shapes],
        in_specs=[HBM_SPEC] * n,
        out_specs=[HBM_SPEC] * n,
        scratch_shapes=[pltpu.SemaphoreType.DMA((n,)), pltpu.SemaphoreType.DMA((n,))],
        compiler_params=_params(),
    )(*halves)


SEM_SPEC = pl.BlockSpec(memory_space=pltpu.SEMAPHORE)
ANY_SPEC = pl.BlockSpec(memory_space=pl.ANY)
DATAFLOW = pltpu.SideEffectType.DATAFLOW_SIDE_EFFECTING


def _split_copy_start(srcs, lands, plan, n_sems, name, after=None):
    bufs = list(srcs) + list(lands)
    ns, nb = len(srcs), len(bufs)
    extra = [] if after is None else [after]

    def body(*refs):
        outs = refs[nb + len(extra):]
        for outgoing, _ in plan(refs[:ns], refs[ns:nb], outs[0], outs[1]):
            outgoing.start()
        outs[-1][...] = jnp.zeros_like(outs[-1])

    sems = pltpu.SemaphoreType.DMA((n_sems,))
    return pl.pallas_call(
        body,
        name=name,
        out_shape=(sems, sems, *[pltpu.HBM(b.shape, b.dtype) for b in bufs], jax.ShapeDtypeStruct((8, LANES), F32)),
        in_specs=[HBM_SPEC] * nb + [ANY_SPEC] * len(extra),
        out_specs=(SEM_SPEC, SEM_SPEC, *[HBM_SPEC] * nb, pl.BlockSpec(memory_space=pltpu.VMEM)),
        input_output_aliases={i: 2 + i for i in range(nb)},
        compiler_params=pltpu.CompilerParams(has_side_effects=DATAFLOW),
    )(*[pltpu.with_memory_space_constraint(b, pltpu.HBM) for b in bufs], *extra)


def _split_copy_wait(started, plan, after, name):
    send, recv, *bufs = started[:-1]
    nb = len(bufs)
    ns = nb // 2

    def body(*refs):
        for outgoing, incoming in plan(refs[:ns], refs[ns:nb], refs[nb], refs[nb + 1]):
            outgoing.wait_send()
            incoming.wait_recv()

    return pl.pallas_call(
        body,
        name=name,
        out_shape=tuple(pltpu.HBM(b.shape, b.dtype) for b in bufs),
        in_specs=[HBM_SPEC] * nb + [SEM_SPEC, SEM_SPEC, ANY_SPEC],
        out_specs=tuple([HBM_SPEC] * nb),
        input_output_aliases={i: i for i in range(nb)},
        compiler_params=pltpu.CompilerParams(has_side_effects=DATAFLOW),
    )(*bufs, send, recv, after)


def _weight_plan(shapes):
    def plan(srcs, lands, send, recv):
        x, y, c, chips = _place()
        copies = []
        for a in range(len(shapes)):
            mine = _half_rows(c, shapes[a][0] // 2)
            for j, (px, py) in enumerate(chips):
                s = 3 * a + j
                arrival = lands[a].at[2 * px + py, mine, :]
                copies.append((_remote(srcs[a].at[mine, :], lands[a].at[2 * x + y, mine, :], send.at[s], recv.at[s], (px, py, c)),
                               _remote(arrival, arrival, send.at[s], recv.at[s], (px, py, c))))
        return copies
    return plan


def _halves_plan(shapes):
    def plan(srcs, lands, send, recv):
        x, y, c, _ = _place()
        copies = []
        for a in range(len(shapes)):
            theirs = srcs[a].at[:, _half_rows(1 - c, shapes[a][1] // 2), :]
            copies.append((_remote(theirs, lands[a], send.at[a], recv.at[a], (x, y, 1 - c)),
                           _remote(lands[a], lands[a], send.at[a], recv.at[a], (x, y, 1 - c))))
        return copies
    return plan


def _exchange_plan(n):
    def plan(srcs, lands, send, recv):
        x, y, c, chips = _place()
        copies = []
        for a in range(n):
            for j, (px, py) in enumerate(chips):
                s = 3 * a + j
                copies.append((_remote(srcs[a].at[2 * px + py], lands[a].at[j], send.at[s], recv.at[s], (px, py, c)),
                               _remote(lands[a].at[j], lands[a].at[j], send.at[s], recv.at[s], (px, py, c))))
        return copies
    return plan


def _forward_to_sibling(stacks, name):
    n = len(stacks)
    shapes = [s.shape for s in stacks]

    def body(*refs):
        ins, outs = refs[:n], refs[n:2 * n]
        send, recv = refs[2 * n:]
        x, y, c, chips = _place()
        sibling = (x, y, 1 - c)
        copies = []
        for a in range(n):
            half = shapes[a][1] // 2
            for j, (px, py) in enumerate(chips):
                rows = _half_rows(c, half)
                cp = _remote(ins[a].at[2 * px + py, rows, :], outs[a].at[2 * px + py, rows, :], send.at[a, j],
                             recv.at[a, j], sibling)
                cp.start()
                copies.append(cp)
        for a in range(n):
            half = shapes[a][1] // 2
            for j, (px, py) in enumerate(chips):
                theirs = outs[a].at[2 * px + py, _half_rows(1 - c, half), :]
                _remote(theirs, theirs, send.at[a, j], recv.at[a, j], sibling).wait_recv()
        for cp in copies:
            cp.wait_send()

    return pl.pallas_call(
        body,
        name=name,
        out_shape=[jax.ShapeDtypeStruct(s, BF16) for s in shapes],
        in_specs=[HBM_SPEC] * n,
        out_specs=[HBM_SPEC] * n,
        input_output_aliases={a: a for a in range(n)},
        scratch_shapes=[pltpu.SemaphoreType.DMA((n, 3)), pltpu.SemaphoreType.DMA((n, 3))],
        compiler_params=_params(),
    )(*stacks)


def _tie(x, token):
    return x + token[0:1, 0:1].astype(x.dtype)


ROW_PAD = 16


def _silu(x):
    return x * _sigmoid(x)


def _ada_fwd(c_all, w_ada, b_ada, name):
    dm, cols = w_ada.shape
    tn = 512

    def body(c_ref, w_ref, b_ref, o_ref):
        o_ref[...] = _nn(_silu(c_ref[...]).astype(BF16), w_ref[...].astype(BF16)) + b_ref[...]

    return pl.pallas_call(
        body,
        name=name,
        grid=(cols // tn,),
        in_specs=[pl.BlockSpec((ROW_PAD, dm), lambda j: (0, 0)), pl.BlockSpec((dm, tn), lambda j: (0, j)),
                  pl.BlockSpec((1, tn), lambda j: (0, j))],
        out_specs=pl.BlockSpec((ROW_PAD, tn), lambda j: (0, j)),
        out_shape=jax.ShapeDtypeStruct((ROW_PAD, cols), F32),
        compiler_params=_params(("parallel",)),
    )(c_all, w_ada, b_ada)


def _adamw(w, g, m, v):
    m = ADAM_B1 * m + (1.0 - ADAM_B1) * g
    v = ADAM_B2 * v + (1.0 - ADAM_B2) * (g * g)
    m_hat = m / (1.0 - ADAM_B1 ** ADAM_STEP)
    v_hat = v / (1.0 - ADAM_B2 ** ADAM_STEP)
    delta = -ADAM_LR * (m_hat / (jnp.sqrt(v_hat) + ADAM_EPS) + ADAM_WD * w)
    return delta, m, v


def _ada_bwd_adamw(c_all, dmod_cols, w, m, v, name):
    dm, cols = w.shape
    tm, tn = 512, 512

    def body(c_ref, d_ref, w_ref, m_ref, v_ref, g_ref, dl_ref, nm_ref, nv_ref):
        g = _tn(_silu(c_ref[...]).astype(BF16), d_ref[...].astype(BF16))
        g_ref[...] = g
        dl_ref[...], nm_ref[...], nv_ref[...] = _adamw(w_ref[...], g, m_ref[...], v_ref[...])

    tile = pl.BlockSpec((tm, tn), lambda i, j: (i, j))
    shape = jax.ShapeDtypeStruct((dm, cols), F32)
    return pl.pallas_call(
        body,
        name=name,
        grid=(dm // tm, cols // tn),
        in_specs=[pl.BlockSpec((ROW_PAD, tm), lambda i, j: (0, i)), pl.BlockSpec((ROW_PAD, tn), lambda i, j: (0, j)),
                  tile, tile, tile],
        out_specs=[tile] * 4,
        out_shape=[shape] * 4,
        compiler_params=_params(("parallel", "parallel")),
    )(c_all, dmod_cols, w, m, v)


def _sum_blocks(parts, nblk, name):
    rows, cols = parts.shape[0] // nblk, parts.shape[1]

    def body(p_ref, o_ref):
        tot = p_ref[0:rows, :]
        for b in range(1, nblk):
            tot = tot + p_ref[b * rows:(b + 1) * rows, :]
        o_ref[...] = tot

    return pl.pallas_call(body, name=name, out_shape=jax.ShapeDtypeStruct((rows, cols), F32), compiler_params=_params())(parts)


def _adamw_rows(w, g, m, v, tm, name):
    return _rowwise(lambda r, c: list(_adamw(*r)), [w, g, m, v], [], [(w.shape[1], F32)] * 3, [], tm, name)


BIG = ("w_in", "w_out", "w_mlp_in", "w_mlp_out", "w_glu")
COL_SHARDED = ("w_in", "w_out", "w_mlp_in")
SMALL = ("b_ada", "g_pre_mix", "g_post_mix", "ssm_a_re", "ssm_a_im", "ssm_log_dt", "ssm_b_re", "ssm_b_im",
         "ssm_c_re", "ssm_c_im", "ssm_d", "b_glu", "g_attn_out", "g_ssm_out", "g_pre_mlp", "g_post_mlp")
WEIGHTS = ("w_ada", "b_ada", "g_pre_mix", "g_post_mix", "w_in", "ssm_a_re", "ssm_a_im", "ssm_log_dt", "ssm_b_re",
           "ssm_b_im", "ssm_c_re", "ssm_c_im", "ssm_d", "w_glu", "b_glu", "g_attn_out", "g_ssm_out", "w_out",
           "g_pre_mlp", "g_post_mlp", "w_mlp_in", "w_mlp_out")
FLAT_COLS = 1024
FLAT_ROWS = 256
ROW_TILE = {"w_in": 256, "w_out": 128, "w_mlp_in": 256, "w_mlp_out": 256, "w_glu": 112}


def _flatten_small(tree):
    flat = jnp.concatenate([tree[k].reshape(-1) for k in SMALL])
    return jnp.pad(flat, (0, FLAT_ROWS * FLAT_COLS - flat.shape[0])).reshape(FLAT_ROWS, FLAT_COLS)


def _unflatten_small(flat, like):
    flat = flat.reshape(-1)
    out, at = {}, 0
    for k in SMALL:
        size = math.prod(like[k].shape)
        out[k] = flat[at:at + size].reshape(like[k].shape)
        at += size
    return out


def _unstack(stack, name):
    if name in COL_SHARDED:
        return stack.transpose(1, 0, 2).reshape(stack.shape[1], N_CHIPS * stack.shape[2])
    return stack.reshape(N_CHIPS * stack.shape[1], stack.shape[2])


def _stack(full, name):
    if name in COL_SHARDED:
        return full.reshape(full.shape[0], N_CHIPS, full.shape[1] // N_CHIPS).transpose(1, 0, 2)
    return full.reshape(N_CHIPS, full.shape[0] // N_CHIPS, full.shape[1])


EARLY = ("w_in", "w_out", "w_glu")
LATE = ("w_mlp_in", "w_mlp_out")


def _chip_sums(names, g_stacks, from_sibling, ic, chip):
    own, to_send = [], []
    place = jnp.stack([ic, chip]).astype(jnp.int32)
    for k, gs, fs in zip(names, g_stacks, from_sibling):
        _, rows, cols = gs.shape
        half, tm = rows // 2, ROW_TILE[k]
        nt = half // tm

        def body(place_ref, g_ref, f_ref, own_ref, send_ref):
            s = g_ref[...] + f_ref[...]
            send_ref[...] = s.astype(BF16)

            @pl.when(pl.program_id(1) == place_ref[1])
            def _():
                own_ref[...] = s

        slab = lambda index: pl.BlockSpec((None, tm, cols), index)
        mine, to_chips = pl.pallas_call(
            body,
            name="grad_chip_sum_" + k,
            grid_spec=pltpu.PrefetchScalarGridSpec(
                num_scalar_prefetch=1,
                grid=(nt, N_CHIPS),
                in_specs=[slab(lambda i, kk, p, nt=nt: (kk, p[0] * nt + i, 0)), slab(lambda i, kk, p: (kk, i, 0))],
                out_specs=[pl.BlockSpec((tm, cols), lambda i, kk, p: (i, 0)), slab(lambda i, kk, p: (kk, i, 0))]),
            out_shape=[jax.ShapeDtypeStruct((half, cols), F32), jax.ShapeDtypeStruct((N_CHIPS, half, cols), BF16)],
            compiler_params=_params(("arbitrary", "arbitrary")),
        )(place, gs, fs)
        own.append(mine)
        to_send.append(to_chips)
    return own, to_send


def _grad_totals(names, own, from_chips):
    totals = []
    for k, mine, fc in zip(names, own, from_chips):
        half, cols = mine.shape
        tm = ROW_TILE[k]

        def body(m_ref, a_ref, b_ref, c_ref, o_ref):
            o_ref[...] = m_ref[...] + a_ref[...].astype(F32) + b_ref[...].astype(F32) + c_ref[...].astype(F32)

        rows = pl.BlockSpec((tm, cols), lambda i: (i, 0))
        totals.append(pl.pallas_call(
            body,
            name="grad_total_" + k,
            grid=(half // tm,),
            in_specs=[rows] + [pl.BlockSpec((None, tm, cols), lambda i, j=j: (j, i, 0)) for j in range(3)],
            out_specs=rows,
            out_shape=jax.ShapeDtypeStruct((half, cols), F32),
            compiler_params=_params(("parallel",)),
        )(mine, fc, fc, fc))
    return totals


class _Overlap:
    def __init__(self, own_shards, ic, chip, after):
        self.ic, self.chip = ic, chip
        self.shapes = [o.shape for o in own_shards]
        lands = [lax.empty((N_CHIPS,) + s, BF16) for s in self.shapes]
        self.gather = _split_copy_start(own_shards, lands, _weight_plan(self.shapes), 3 * len(LATE),
                                        "mlp_weight_gather_start", after=after)
        self.token = self.gather[-1]

    def mlp_weights(self, after):
        n = len(LATE)
        done = _split_copy_wait(self.gather, _weight_plan(self.shapes), after, "mlp_weight_gather_wait")
        own, stacks = done[:n], done[n:]
        stacks = _forward_to_sibling(stacks, "mlp_weight_forward")
        stacks = [lax.dynamic_update_index_in_dim(s, o, self.chip, 0) for s, o in zip(stacks, own)]
        return [_unstack(s, k) for k, s in zip(LATE, stacks)]

    def mlp_grads_to_sibling(self, dw_in, dw_out):
        stacks = [dw_in, dw_out]
        self.g_shapes = [s.shape for s in stacks]
        lands = [lax.empty((N_CHIPS, s[1] // 2, s[2]), F32) for s in self.g_shapes]
        self.halves = _split_copy_start(stacks, lands, _halves_plan(self.g_shapes), len(LATE), "mlp_grad_halves_start")
        return self.halves[-1]

    def mlp_grads_to_chips(self, after):
        n = len(LATE)
        done = _split_copy_wait(self.halves, _halves_plan(self.g_shapes), after, "mlp_grad_halves_wait")
        self.own, to_send = _chip_sums(LATE, done[:n], done[n:], self.ic, self.chip)
        lands = [lax.empty((3,) + s.shape[1:], BF16) for s in to_send]
        self.exchange = _split_copy_start(to_send, lands, _exchange_plan(n), 3 * n, "mlp_grad_exchange_start")
        return self.exchange[-1]

    def mlp_grads_reduced(self, after):
        n = len(LATE)
        done = _split_copy_wait(self.exchange, _exchange_plan(n), after, "mlp_grad_exchange_wait")
        return _grad_totals(LATE, self.own, done[n:])


def _pad_rows(row):
    return jnp.pad(row, ((0, 8 - row.shape[0]), (0, 0)))


def _every_eighth(gathered):
    rows = gathered.reshape(N_DEV, 8, gathered.shape[1])[:, 0, :]
    return jnp.pad(rows, ((0, ROW_PAD - N_DEV), (0, 0)))


def kernel(x, c, positions, w_ada, b_ada, g_pre_mix, g_post_mix, w_in, ssm_a_re, ssm_a_im, ssm_log_dt, ssm_b_re, ssm_b_im, ssm_c_re, ssm_c_im, ssm_d, w_glu, b_glu, g_attn_out, g_ssm_out, w_out, g_pre_mlp, g_post_mlp, w_mlp_in, w_mlp_out, loss_target, m_w_ada, m_b_ada, m_g_pre_mix, m_g_post_mix, m_w_in, m_ssm_a_re, m_ssm_a_im, m_ssm_log_dt, m_ssm_b_re, m_ssm_b_im, m_ssm_c_re, m_ssm_c_im, m_ssm_d, m_w_glu, m_b_glu, m_g_attn_out, m_g_ssm_out, m_w_out, m_g_pre_mlp, m_g_post_mlp, m_w_mlp_in, m_w_mlp_out, v_w_ada, v_b_ada, v_g_pre_mix, v_g_post_mix, v_w_in, v_ssm_a_re, v_ssm_a_im, v_ssm_log_dt, v_ssm_b_re, v_ssm_b_im, v_ssm_c_re, v_ssm_c_im, v_ssm_d, v_w_glu, v_b_glu, v_g_attn_out, v_g_ssm_out, v_w_out, v_g_pre_mlp, v_g_post_mlp, v_w_mlp_in, v_w_mlp_out):
    given = dict(locals())
    w = {k: given[k][0] for k in WEIGHTS}
    mom = {k: given["m_" + k][0] for k in WEIGHTS}
    var = {k: given["v_" + k][0] for k in WEIGHTS}
    for tree in (w, mom, var):
        for k in ("b_ada", "g_pre_mix", "g_post_mix", "ssm_log_dt", "b_glu", "g_attn_out", "g_ssm_out", "g_pre_mlp",
                  "g_post_mlp"):
            tree[k] = tree[k].reshape(1, -1)
    ix, iy, ic = lax.axis_index("x"), lax.axis_index("y"), lax.axis_index("c")
    chip = 2 * ix + iy
    me = 4 * ix + 2 * iy + ic
    shard_cols = w["w_ada"].shape[1]

    c_all = _every_eighth(_all_gather8(_pad_rows(c), "gather_c"))
    b_ada_cols = lax.dynamic_slice_in_dim(w["b_ada"], chip * shard_cols, shard_cols, axis=1)
    mod_cols = _ada_fwd(c_all, w["w_ada"], b_ada_cols, "ada_fwd")[:N_DEV]
    mod_all = _all_gather8(mod_cols, "gather_mod").reshape(N_CHIPS, 2, N_DEV, shard_cols)[:, 0]
    mod = lax.dynamic_index_in_dim(mod_all, me, axis=1, keepdims=False).reshape(1, N_MOD * D_MODEL)

    early_own = [w[k].astype(BF16) for k in EARLY]
    stacks = _weight_gather(early_own, "weight_gather")
    stacks = [lax.dynamic_update_index_in_dim(s, o, chip, 0) for s, o in zip(stacks, early_own)]
    wts = {k: _unstack(s, k) for k, s in zip(EARLY, stacks)}
    overlap = _Overlap([w[k].astype(BF16) for k in LATE], ic, chip, after=stacks[0])
    mod = _tie(mod, overlap.token)

    small = {k: w[k] for k in SMALL if k != "b_ada"}
    loss, grad_x, dmod, big_g, small_g = _local_step(x[0], positions.reshape(-1, 1), mod, loss_target[0], wts, small,
                                                     hooks=overlap)
    loss = lax.psum(loss[0, 0], ("x", "y", "c"))

    g_stacks = [_stack(big_g[k], k) for k in EARLY]
    from_sibling = _sibling_halves(g_stacks, "grad_sibling_halves")
    chip_f32, chip_bf16 = _chip_sums(EARLY, g_stacks, from_sibling, ic, chip)
    from_chips = _chip_exchange(chip_bf16, "grad_chip_exchange")
    reduced = _grad_totals(EARLY, chip_f32, from_chips) + overlap.mlp_grads_reduced(big_g["w_in"])
    swapped = _sibling_swap(reduced, "grad_sibling_swap")
    grads = {k: lax.dynamic_update_slice_in_dim(s, r, ic * r.shape[0], axis=0)
             for k, s, r in zip(EARLY + LATE, swapped, reduced)}

    small_g["b_ada"] = dmod
    parts = _all_gather8(_flatten_small(small_g), "gather_small_grads")

    small_flat = _sum_blocks(parts, N_DEV, "small_grad_sum")
    grads.update(_unflatten_small(small_flat, w))

    mod_rows = N_MOD * D_MODEL // FLAT_COLS
    dmod_all = parts.reshape(N_DEV, FLAT_ROWS, FLAT_COLS)[:, :mod_rows].reshape(N_DEV, N_MOD * D_MODEL)
    dmod_all = jnp.pad(dmod_all, ((0, ROW_PAD - N_DEV), (0, 0)))
    dmod_cols = lax.dynamic_slice_in_dim(dmod_all, chip * shard_cols, shard_cols, axis=1)
    g_ada, d_ada, m_ada, v_ada = _ada_bwd_adamw(c_all, dmod_cols, w["w_ada"], mom["w_ada"], var["w_ada"], "ada_bwd_adamw")
    grads["w_ada"] = g_ada

    delta, new_m, new_v = {"w_ada": d_ada}, {"w_ada": m_ada}, {"w_ada": v_ada}
    for k in BIG:
        delta[k], new_m[k], new_v[k] = _adamw_rows(w[k], grads[k], mom[k], var[k], ROW_TILE[k], "adamw_" + k)
    flat_upd = _adamw_rows(_flatten_small(w), small_flat, _flatten_small(mom), _flatten_small(var), FLAT_ROWS,
                           "adamw_small")
    for tree, flat in zip((delta, new_m, new_v), flat_upd):
        tree.update(_unflatten_small(flat, w))

    shaped = lambda tree: [tree[k].reshape(given[k].shape) for k in WEIGHTS]
    return (loss, grad_x[None], *shaped(grads), *shaped(delta), *shaped(new_m), *shaped(new_v))
```

```python
import functools
import math

import jax
import jax.numpy as jnp
import numpy as np
from jax import lax
from jax.experimental import pallas as pl
from jax.experimental.pallas import tpu as pltpu

F32 = jnp.float32
BF16 = jnp.bfloat16

D_MODEL = 2048
HEAD_DIM = 64
DILATIONS = (1, 4, 16)
ATT_SPAN = 128
ATT_BLK = 128
HEADS_PER_GROUP = 6
KV_WIDTH = HEADS_PER_GROUP * HEAD_DIM
ATT_Q_WIDTH = 3 * KV_WIDTH
ROT_DIM = 16
ROPE_THETA = 500000.0
SSM_WIDTH = 896
SSM_P = 16
SSM_G = 56
SSM_N = 64
SSM_GN = SSM_G * SSM_N
SSM_TILES = SSM_WIDTH // 128
SSM_TILE_GN = 8 * SSM_N
IN_WIDTH = 2816
OUT_IN_WIDTH = 1280
D_FF = 8192
N_MOD = 6
EPS = 1e-6
LANES = 128
SSM_SEGS = 8
SSM_CHUNK = 256
SSM_SEG_LEN = SSM_CHUNK // SSM_SEGS

ADAM_LR = 0.001
ADAM_B1 = 0.9
ADAM_B2 = 0.999
ADAM_EPS = 1e-08
ADAM_WD = 0.01
ADAM_STEP = 10

VMEM_LIMIT = 56 * 1024 * 1024


def _params(sem=None):
    return pltpu.CompilerParams(dimension_semantics=sem, vmem_limit_bytes=VMEM_LIMIT)


def _dot(a, b, dims):
    return lax.dot_general(a, b, (dims, ((), ())), preferred_element_type=F32)


def _nn(a, b):
    return _dot(a, b, ((1,), (0,)))


def _nt(a, b):
    return _dot(a, b, ((1,), (1,)))


def _tn(a, b):
    return _dot(a, b, ((0,), (0,)))


def _matmul(a, b, mode, out_dtype, tm, tn, tk, name, after=None):
    if mode == "nn":
        (m, k), (_, n) = a.shape, b.shape
        a_spec = pl.BlockSpec((tm, tk), lambda i, j, kk: (i, kk))
        b_spec = pl.BlockSpec((tk, tn), lambda i, j, kk: (kk, j))
        op = _nn
    elif mode == "nt":
        (m, k), (n, _) = a.shape, b.shape
        a_spec = pl.BlockSpec((tm, tk), lambda i, j, kk: (i, kk))
        b_spec = pl.BlockSpec((tn, tk), lambda i, j, kk: (j, kk))
        op = _nt
    else:
        (k, m), (_, n) = a.shape, b.shape
        a_spec = pl.BlockSpec((tk, tm), lambda i, j, kk: (kk, i))
        b_spec = pl.BlockSpec((tk, tn), lambda i, j, kk: (kk, j))
        op = _tn
    assert m % tm == 0 and n % tn == 0 and k % tk == 0, (name, m, n, k)
    nk = k // tk

    def body(a_ref, b_ref, *rest):
        o_ref, acc_ref = rest[-2:]
        kk = pl.program_id(2)

        @pl.when(kk == 0)
        def _():
            acc_ref[...] = jnp.zeros_like(acc_ref)

        acc_ref[...] += op(a_ref[...], b_ref[...])

        @pl.when(kk == nk - 1)
        def _():
            o_ref[...] = acc_ref[...].astype(o_ref.dtype)

    extra = [] if after is None else [after]
    return pl.pallas_call(
        body,
        name=name,
        grid=(m // tm, n // tn, nk),
        in_specs=[a_spec, b_spec] + [pl.BlockSpec(t.shape, lambda i, j, kk: (0, 0)) for t in extra],
        out_specs=pl.BlockSpec((tm, tn), lambda i, j, kk: (i, j)),
        out_shape=jax.ShapeDtypeStruct((m, n), out_dtype),
        scratch_shapes=[pltpu.VMEM((tm, tn), F32)],
        compiler_params=_params(("parallel", "parallel", "arbitrary")),
    )(a, b, *extra)


def _rowwise(fn, rows, consts, out_rows, out_accs, tm, name):
    n_rows = rows[0].shape[0]
    assert n_rows % tm == 0
    nr, nc, no = len(rows), len(consts), len(out_rows)

    def body(*refs):
        r_in, c_in = refs[:nr], refs[nr:nr + nc]
        o_row, o_acc = refs[nr + nc:nr + nc + no], refs[nr + nc + no:]
        outs = fn([r[...] for r in r_in], [c[...] for c in c_in])
        assert len(outs) == len(o_row) + len(o_acc), name
        for ref, v in zip(o_row, outs[:no]):
            ref[...] = v.astype(ref.dtype)
        first = pl.program_id(0) == 0
        for ref, v in zip(o_acc, outs[no:]):
            @pl.when(first)
            def _(ref=ref, v=v):
                ref[...] = v.astype(F32)

            @pl.when(jnp.logical_not(first))
            def _(ref=ref, v=v):
                ref[...] += v.astype(F32)

    in_specs = [pl.BlockSpec((tm, r.shape[1]), lambda i: (i, 0)) for r in rows]
    in_specs += [pl.BlockSpec(c.shape, lambda i: (0, 0)) for c in consts]
    out_specs = [pl.BlockSpec((tm, w), lambda i: (i, 0)) for w, _ in out_rows]
    out_specs += [pl.BlockSpec(s, lambda i: (0, 0)) for s in out_accs]
    out_shape = [jax.ShapeDtypeStruct((n_rows, w), dt) for w, dt in out_rows]
    out_shape += [jax.ShapeDtypeStruct(s, F32) for s in out_accs]
    return pl.pallas_call(
        body,
        name=name,
        grid=(n_rows // tm,),
        in_specs=in_specs,
        out_specs=out_specs,
        out_shape=out_shape,
        compiler_params=_params(("arbitrary",)),
    )(*rows, *consts)


def _rms(x, g):
    return x * lax.rsqrt(jnp.mean(x * x, axis=-1, keepdims=True) + EPS) * g


def _mod_norm(x, g, sc, sh):
    return _rms(x, g) * (1.0 + sc) + sh


def _gelu(x):
    return 0.5 * x * (1.0 + jnp.tanh(math.sqrt(2.0 / math.pi) * (x + 0.044715 * (x * x * x))))


def _sigmoid(x):
    return 1.0 / (1.0 + jnp.exp(-x))


def _post_mix(x, mix, g_post, gt1, g_pre, sc2, sh2):
    x1 = x + gt1 * _rms(mix, g_post)
    return x1, _mod_norm(x1, g_pre, sc2, sh2)


def _att_mix(o0, o1, o2, l0, l1, l2, g):
    m = jnp.maximum(jnp.maximum(l0, l1), l2)
    e0, e1, e2 = jnp.exp(l0 - m), jnp.exp(l1 - m), jnp.exp(l2 - m)
    att = (e0 * o0 + e1 * o1 + e2 * o2) / (e0 + e1 + e2)
    return _rms(att, g)


def _glu_out(y2, z, g):
    return _rms(y2 * _sigmoid(z), g)


def _rope_tables(pos_col, freq_lane, name):
    n_rows = pos_col.shape[0]
    tm = 512

    def body(p_ref, f_ref, cos_ref, lo_ref, hi_ref):
        ang = p_ref[...].astype(F32) * f_ref[...]
        lane = lax.broadcasted_iota(jnp.int32, ang.shape, 1) % HEAD_DIM
        c, s = jnp.cos(ang), jnp.sin(ang)
        cos_ref[...] = jnp.where(lane < ROT_DIM, c, 1.0)
        lo_ref[...] = jnp.where(lane < ROT_DIM // 2, -s, 0.0)
        hi_ref[...] = jnp.where((lane >= ROT_DIM // 2) & (lane < ROT_DIM), s, 0.0)

    tab = jax.ShapeDtypeStruct((n_rows, LANES), F32)
    return pl.pallas_call(
        body,
        name=name,
        grid=(n_rows // tm,),
        in_specs=[pl.BlockSpec((tm, 1), lambda i: (i, 0)), pl.BlockSpec((1, LANES), lambda i: (0, 0))],
        out_specs=[pl.BlockSpec((tm, LANES), lambda i: (i, 0))] * 3,
        out_shape=[tab] * 3,
        compiler_params=_params(("parallel",)),
    )(pos_col, freq_lane)


def _rope(x, cos_t, lo_t, hi_t):
    half = ROT_DIM // 2
    return x * cos_t + pltpu.roll(x, LANES - half, 1) * lo_t + pltpu.roll(x, half, 1) * hi_t


def _rope_transposed(dy, cos_t, lo_t, hi_t):
    half = ROT_DIM // 2
    return dy * cos_t + pltpu.roll(dy * lo_t, half, 1) + pltpu.roll(dy * hi_t, LANES - half, 1)


def _att_masks(i, k0):
    q_pos = i * ATT_BLK + lax.broadcasted_iota(jnp.int32, (ATT_BLK, 2 * ATT_BLK), 0)
    k_pos = k0 + lax.broadcasted_iota(jnp.int32, (ATT_BLK, 2 * ATT_BLK), 1)
    dist = q_pos - k_pos
    return (dist >= 0) & (dist <= ATT_SPAN)


def _head_lane_masks():
    lane = lax.broadcasted_iota(jnp.int32, (1, LANES), 1)
    return lane < HEAD_DIM, lane >= HEAD_DIM


def _att_specs(gi, n_rows):
    once = pl.Buffered(1)
    col = lambda at: pl.BlockSpec((n_rows, LANES), lambda hp: (0, at + hp), pipeline_mode=once)
    qkv = [col(gi * 3), col(9), col(12)]
    tabs = [pl.BlockSpec((n_rows, LANES), lambda hp: (0, 0), pipeline_mode=once)] * 3
    return qkv, tabs, col(0), pl.BlockSpec((n_rows, LANES), lambda hp: (0, hp))


def _sub_rows(d, n, r):
    return pl.ds(r, n, stride=d) if d > 1 else pl.ds(0, n)


def _att_load(q_ref, k_ref, v_ref, tabs, sub, qs, ks, vs):
    cos_t, lo_t, hi_t = tabs
    qs[...] = (_rope(q_ref[sub, :], cos_t, lo_t, hi_t) * (1.0 / math.sqrt(HEAD_DIM))).astype(BF16)
    ks[...] = _rope(k_ref[sub, :], cos_t, lo_t, hi_t).astype(BF16)
    vs[...] = v_ref[sub, :].astype(BF16)


def _att_fwd(proj, tabs, gi, name):
    n_rows = proj.shape[0]
    d = DILATIONS[gi]
    n = n_rows // d
    nb = n // ATT_BLK

    def body(q_ref, k_ref, v_ref, cos_ref, lo_ref, hi_ref, o_ref, l_ref, qs, ks, vs, o_s, l_s):
        m0, m1 = _head_lane_masks()

        def step(i, carry):
            k0 = pl.multiple_of(jnp.maximum(i - 1, 0) * ATT_BLK, ATT_BLK)
            q0 = pl.multiple_of(i * ATT_BLK, ATT_BLK)
            q = qs[pl.ds(q0, ATT_BLK), :]
            k = ks[pl.ds(k0, 2 * ATT_BLK), :]
            v = vs[pl.ds(k0, 2 * ATT_BLK), :]
            valid = _att_masks(i, k0)
            outs, lses = [], []
            for hm in (m0, m1):
                s = _nt(jnp.where(hm, q, jnp.zeros_like(q)), k)
                s = jnp.where(valid, s, -1e30)
                mx = jnp.max(s, axis=1, keepdims=True)
                p = jnp.exp(s - mx)
                den = jnp.sum(p, axis=1, keepdims=True)
                outs.append(_nn(p.astype(BF16), v) / den)
                lses.append(mx + jnp.log(den))
            o_s[pl.ds(q0, ATT_BLK), :] = jnp.where(m0, outs[0], outs[1])
            l_s[pl.ds(q0, ATT_BLK), :] = jnp.where(m0, lses[0], lses[1])
            return carry

        for r in range(d):
            sub = _sub_rows(d, n, r)
            _att_load(q_ref, k_ref, v_ref, (cos_ref[sub, :], lo_ref[sub, :], hi_ref[sub, :]), sub, qs, ks, vs)
            lax.fori_loop(0, nb, step, 0, unroll=2)
            o_ref[sub, :] = o_s[...]
            l_ref[sub, :] = l_s[...]

    qkv, tab_specs, _, head_out = _att_specs(gi, n_rows)
    out = jax.ShapeDtypeStruct((n_rows, KV_WIDTH), F32)
    return pl.pallas_call(
        body,
        name=name,
        grid=(3,),
        in_specs=qkv + tab_specs,
        out_specs=[head_out, head_out],
        out_shape=[out, out],
        scratch_shapes=[pltpu.VMEM((n, LANES), BF16)] * 3 + [pltpu.VMEM((n, LANES), F32)] * 2,
        compiler_params=_params(("parallel",)),
    )(proj, proj, proj, *tabs)


def _att_bwd(proj, tabs, o, l, do, dl, gi, name):
    n_rows = proj.shape[0]
    d = DILATIONS[gi]
    n = n_rows // d
    nb = n // ATT_BLK

    def body(q_ref, k_ref, v_ref, cos_ref, lo_ref, hi_ref, o_ref, l_ref, do_ref, dl_ref,
             dq_ref, dk_ref, dv_ref, qs, ks, vs, dq_s, dk_acc, dv_acc, *gathered):
        m0, m1 = _head_lane_masks()
        o_s, l_s, do_s, dl_s = gathered if d > 1 else (o_ref, l_ref, do_ref, dl_ref)

        def step(i, carry):
            k0 = pl.multiple_of(jnp.maximum(i - 1, 0) * ATT_BLK, ATT_BLK)
            q0 = pl.multiple_of(i * ATT_BLK, ATT_BLK)
            rows = pl.ds(q0, ATT_BLK)
            keys = pl.ds(k0, 2 * ATT_BLK)
            q, k, v = qs[rows, :], ks[keys, :], vs[keys, :]
            d_o, lse = do_s[rows, :], l_s[rows, :]
            o_do = o_s[rows, :] * d_o
            d_l = dl_s[rows, :]
            valid = _att_masks(i, k0)
            dq = jnp.zeros((ATT_BLK, LANES), F32)
            dk = jnp.zeros((2 * ATT_BLK, LANES), F32)
            dv = jnp.zeros((2 * ATT_BLK, LANES), F32)
            for hm in (m0, m1):
                qh, kh = jnp.where(hm, q, jnp.zeros_like(q)), jnp.where(hm, k, jnp.zeros_like(k))
                doh = jnp.where(hm, d_o, 0.0).astype(BF16)
                lse_h = jnp.max(jnp.where(hm, lse, -1e30), axis=1, keepdims=True)
                delta = jnp.sum(jnp.where(hm, o_do, 0.0), axis=1, keepdims=True)
                dlse = jnp.sum(jnp.where(hm, d_l, 0.0), axis=1, keepdims=True)
                s = jnp.where(valid, _nt(qh, k), -1e30)
                p = jnp.exp(s - lse_h)
                dv = dv + _tn(p.astype(BF16), doh)
                ds = (p * (_nt(doh, v) - delta + dlse)).astype(BF16)
                dq = dq + _nn(ds, kh)
                dk = dk + _tn(ds, qh)
            dq_s[rows, :] = dq * (1.0 / math.sqrt(HEAD_DIM))
            dk_acc[keys, :] += dk
            dv_acc[keys, :] += dv
            return carry

        for r in range(d):
            sub = _sub_rows(d, n, r)
            rot = (cos_ref[sub, :], lo_ref[sub, :], hi_ref[sub, :])
            _att_load(q_ref, k_ref, v_ref, rot, sub, qs, ks, vs)
            if d > 1:
                for dst, src in zip(gathered, (o_ref, l_ref, do_ref, dl_ref)):
                    dst[...] = src[sub, :]
            dk_acc[...] = jnp.zeros_like(dk_acc)
            dv_acc[...] = jnp.zeros_like(dv_acc)
            lax.fori_loop(0, nb, step, 0, unroll=2)
            dq_ref[sub, :] = _rope_transposed(dq_s[...], *rot)
            dk_ref[sub, :] = _rope_transposed(dk_acc[...], *rot)
            dv_ref[sub, :] = dv_acc[...]

    qkv, tab_specs, head_in, head_out = _att_specs(gi, n_rows)
    out = jax.ShapeDtypeStruct((n_rows, KV_WIDTH), F32)
    sub_f32 = pltpu.VMEM((n, LANES), F32)
    return pl.pallas_call(
        body,
        name=name,
        grid=(3,),
        in_specs=qkv + tab_specs + [head_in] * 4,
        out_specs=[head_out] * 3,
        out_shape=[out] * 3,
        scratch_shapes=[pltpu.VMEM((n, LANES), BF16)] * 3 + [sub_f32] * (3 if d == 1 else 7),
        compiler_params=_params(("parallel",)),
    )(proj, proj, proj, *tabs, o, l, do, dl)


def _expand_np():
    e = np.zeros((SSM_N, SSM_N * SSM_P), np.float32)
    for nn in range(SSM_N):
        e[nn, nn * SSM_P:(nn + 1) * SSM_P] = 1.0
    return e


def _ssm_prep_math(a_re, a_im, log_dt, b_re, b_im, expand):
    dt = jnp.exp(log_dt)
    mag = jnp.exp(a_re * dt)
    ab_re, ab_im = mag * jnp.cos(a_im * dt), mag * jnp.sin(a_im * dt)
    den = a_re * a_re + a_im * a_im
    num_re, num_im = ab_re - 1.0, ab_im
    co_re = (num_re * a_re + num_im * a_im) / den
    co_im = (num_im * a_re - num_re * a_im) / den
    hi = lax.Precision.HIGHEST
    co_re_x = jnp.dot(co_re, expand, precision=hi, preferred_element_type=F32)
    co_im_x = jnp.dot(co_im, expand, precision=hi, preferred_element_type=F32)
    bb_re = co_re_x * b_re - co_im_x * b_im
    bb_im = co_re_x * b_im + co_im_x * b_re
    return ab_re, ab_im, bb_re, bb_im


def _ssm_prep(a_re, a_im, log_dt, b_re, b_im, expand, name):
    def body(ar, ai, ld, br, bi, ex, o0, o1, o2, o3):
        outs = _ssm_prep_math(ar[...], ai[...], ld[...], br[...], bi[...], ex[...])
        for ref, v in zip((o0, o1, o2, o3), outs):
            ref[...] = v

    gn = jax.ShapeDtypeStruct((SSM_G, SSM_N), F32)
    gnp = jax.ShapeDtypeStruct((SSM_G, SSM_N * SSM_P), F32)
    return pl.pallas_call(body, name=name, out_shape=[gn, gn, gnp, gnp], compiler_params=_params())(
        a_re, a_im, log_dt, b_re, b_im, expand)


def _ssm_prep_bwd(a_re, a_im, log_dt, b_re, b_im, expand, cts, name):
    def body(ar, ai, ld, br, bi, ex, c0, c1, c2, c3, o0, o1, o2, o3, o4):
        ex_v = ex[...]
        _, vjp = jax.vjp(lambda *p: _ssm_prep_math(*p, ex_v), ar[...], ai[...], ld[...], br[...], bi[...])
        for ref, v in zip((o0, o1, o2, o3, o4), vjp((c0[...], c1[...], c2[...], c3[...]))):
            ref[...] = v

    gn = jax.ShapeDtypeStruct((SSM_G, SSM_N), F32)
    gnp = jax.ShapeDtypeStruct((SSM_G, SSM_N * SSM_P), F32)
    g1 = jax.ShapeDtypeStruct((SSM_G, 1), F32)
    return pl.pallas_call(body, name=name, out_shape=[gn, gn, g1, gnp, gnp], compiler_params=_params())(
        a_re, a_im, log_dt, b_re, b_im, expand, *cts)


def _block_diag_in(bb):
    t = bb.reshape(SSM_TILES, 8, SSM_N, SSM_P).transpose(0, 1, 3, 2)
    eye = jnp.eye(8, dtype=bb.dtype)
    return (t[:, :, :, None, :] * eye[None, :, None, :, None]).reshape(SSM_TILES, LANES, SSM_TILE_GN)


def _block_diag_in_grad(dblk):
    t = dblk.reshape(SSM_TILES, 8, SSM_P, 8, SSM_N)
    t = jnp.einsum("tapbn,ab->tapn", t, jnp.eye(8, dtype=dblk.dtype))
    return t.transpose(0, 1, 3, 2).reshape(SSM_G, SSM_N, SSM_P)


def _block_diag_out(cm):
    t = cm.reshape(SSM_TILES, 8, SSM_P, SSM_N).transpose(0, 1, 3, 2)
    eye = jnp.eye(8, dtype=cm.dtype)
    return (t[:, :, :, None, :] * eye[None, :, None, :, None]).reshape(SSM_TILES, SSM_TILE_GN, LANES)


def _block_diag_out_grad(dblk):
    t = dblk.reshape(SSM_TILES, 8, SSM_N, 8, SSM_P)
    t = jnp.einsum("tanbp,ab->tanp", t, jnp.eye(8, dtype=dblk.dtype))
    return t.transpose(0, 1, 3, 2).reshape(SSM_G, SSM_P, SSM_N)


def _cmul_add(a_re, a_im, s_re, s_im, b_re, b_im):
    return a_re * s_re - a_im * s_im + b_re, a_re * s_im + a_im * s_re + b_im


def _lane_tile_specs(first_tile, index):
    return [pl.BlockSpec((SSM_CHUNK, LANES), lambda c, t=t: (index(c), first_tile + t)) for t in range(SSM_TILES)]


def _ssm_load_rows(src_refs, dst):
    for t in range(SSM_TILES):
        for i in range(SSM_SEG_LEN):
            dst[i * SSM_SEGS:(i + 1) * SSM_SEGS, t * LANES:(t + 1) * LANES] = (
                src_refs[t][pl.ds(i, SSM_SEGS, stride=SSM_SEG_LEN), :])


def _ssm_store_rows(src, dst_refs):
    for t in range(SSM_TILES):
        for i in range(SSM_SEG_LEN):
            dst_refs[t][pl.ds(i, SSM_SEGS, stride=SSM_SEG_LEN), :] = (
                src[i * SSM_SEGS:(i + 1) * SSM_SEGS, t * LANES:(t + 1) * LANES])


def _ssm_powers(ab_re_ref, ab_im_ref, pw_re, pw_im):
    a_re, a_im = ab_re_ref[...], ab_im_ref[...]
    p_re, p_im = a_re, a_im
    for i in range(SSM_SEG_LEN):
        pw_re[i:i + 1, :] = p_re
        pw_im[i:i + 1, :] = p_im
        p_re, p_im = _cmul_add(a_re, a_im, p_re, p_im, 0.0, 0.0)


def _ssm_input_proj(u_s, bblk_re_ref, bblk_im_ref, s_re, s_im):
    for t in range(SSM_TILES):
        ub = u_s[:, t * LANES:(t + 1) * LANES].astype(BF16)
        cols = slice(t * SSM_TILE_GN, (t + 1) * SSM_TILE_GN)
        s_re[:, cols] = _nn(ub, bblk_re_ref[t])
        s_im[:, cols] = _nn(ub, bblk_im_ref[t])


def _ssm_scan(ab_re_ref, ab_im_ref, s_re, s_im, init_re, init_im, conj, reverse):
    sign = -1.0 if conj else 1.0
    for t in range(SSM_TILES):
        cols = slice(t * SSM_TILE_GN, (t + 1) * SSM_TILE_GN)
        a_re = jnp.broadcast_to(ab_re_ref[:, cols], (SSM_SEGS, SSM_TILE_GN))
        a_im = jnp.broadcast_to(ab_im_ref[:, cols], (SSM_SEGS, SSM_TILE_GN)) * sign
        if init_re is None:
            st = (jnp.zeros((SSM_SEGS, SSM_TILE_GN), F32),) * 2
        else:
            st = (init_re[:, cols], init_im[:, cols])

        def step(i, st, cols=cols, a_re=a_re, a_im=a_im):
            idx = (SSM_SEG_LEN - 1 - i) if reverse else i
            rows = pl.ds(pl.multiple_of(idx * SSM_SEGS, SSM_SEGS), SSM_SEGS)
            n_re, n_im = _cmul_add(a_re, a_im, st[0], st[1], s_re[rows, cols], s_im[rows, cols])
            s_re[rows, cols] = n_re
            s_im[rows, cols] = n_im
            return n_re, n_im
        lax.fori_loop(0, SSM_SEG_LEN, step, st, unroll=4)


def _ssm_fixup(pw_re, pw_im, s_re, s_im, cin_re, cin_im, conj, reverse):
    sign = -1.0 if conj else 1.0
    c_re, c_im = cin_re[...], cin_im[...]

    def step(i, c):
        k = (SSM_SEG_LEN - 1 - i) if reverse else i
        rows = pl.ds(pl.multiple_of(i * SSM_SEGS, SSM_SEGS), SSM_SEGS)
        p_re = jnp.broadcast_to(pw_re[pl.ds(k, 1), :], (SSM_SEGS, SSM_GN))
        p_im = jnp.broadcast_to(pw_im[pl.ds(k, 1), :], (SSM_SEGS, SSM_GN)) * sign
        n_re, n_im = _cmul_add(p_re, p_im, c_re, c_im, s_re[rows, :], s_im[rows, :])
        s_re[rows, :] = n_re
        s_im[rows, :] = n_im
        return c
    lax.fori_loop(0, SSM_SEG_LEN, step, 0)


def _ssm_fwd(proj, ab_re, ab_im, bblk_re, bblk_im, cblk_re, cblk_im, d_row, name):
    n_rows = proj.shape[0]
    nchunk = n_rows // SSM_CHUNK
    last = SSM_SEG_LEN - 1
    nt = SSM_TILES

    def body(*refs):
        u_ref, y_ref = refs[:nt], refs[nt + 7:2 * nt + 7]
        ar_ref, ai_ref, br_ref, bi_ref, cr_ref, ci_ref, d_ref = refs[nt:nt + 7]
        cin_re_ref, cin_im_ref, u_s, s_re, s_im, pw_re, pw_im, st_re, st_im = refs[2 * nt + 7:]

        @pl.when(pl.program_id(0) == 0)
        def _():
            _ssm_powers(ar_ref, ai_ref, pw_re, pw_im)
            st_re[...] = jnp.zeros_like(st_re)
            st_im[...] = jnp.zeros_like(st_im)

        _ssm_load_rows(u_ref, u_s)
        _ssm_input_proj(u_s, br_ref, bi_ref, s_re, s_im)
        _ssm_scan(ar_ref, ai_ref, s_re, s_im, None, None, conj=False, reverse=False)
        p_re, p_im = pw_re[last:last + 1, :], pw_im[last:last + 1, :]
        c_re, c_im = st_re[...], st_im[...]
        for j in range(SSM_SEGS):
            cin_re_ref[j:j + 1, :] = c_re
            cin_im_ref[j:j + 1, :] = c_im
            row = last * SSM_SEGS + j
            c_re, c_im = _cmul_add(p_re, p_im, c_re, c_im, s_re[row:row + 1, :], s_im[row:row + 1, :])
        st_re[...] = c_re
        st_im[...] = c_im
        _ssm_fixup(pw_re, pw_im, s_re, s_im, cin_re_ref, cin_im_ref, conj=False, reverse=False)
        for t in range(SSM_TILES):
            cols = slice(t * SSM_TILE_GN, (t + 1) * SSM_TILE_GN)
            lanes = slice(t * LANES, (t + 1) * LANES)
            y = _nn(s_re[:, cols].astype(BF16), cr_ref[t]) - _nn(s_im[:, cols].astype(BF16), ci_ref[t])
            u_s[:, lanes] = y + d_ref[:, lanes] * u_s[:, lanes]
        _ssm_store_rows(u_s, y_ref)

    whole2 = lambda a: pl.BlockSpec(a.shape, lambda c: (0, 0))
    whole3 = lambda a: pl.BlockSpec(a.shape, lambda c: (0, 0, 0))
    seg = pl.BlockSpec((SSM_SEGS, SSM_GN), lambda c: (c, 0))
    seg_shape = jax.ShapeDtypeStruct((nchunk * SSM_SEGS, SSM_GN), F32)
    res = pl.pallas_call(
        body,
        name=name,
        grid=(nchunk,),
        in_specs=_lane_tile_specs((IN_WIDTH - SSM_WIDTH) // LANES, lambda c: c) + [
            whole2(ab_re), whole2(ab_im), whole3(bblk_re), whole3(bblk_im), whole3(cblk_re), whole3(cblk_im),
            whole2(d_row)],
        out_specs=[pl.BlockSpec((SSM_CHUNK, LANES), lambda c: (c, 0))] * nt + [seg, seg],
        out_shape=[jax.ShapeDtypeStruct((n_rows, LANES), F32)] * nt + [seg_shape, seg_shape],
        scratch_shapes=[pltpu.VMEM((SSM_CHUNK, SSM_WIDTH), F32), pltpu.VMEM((SSM_CHUNK, SSM_GN), F32),
                        pltpu.VMEM((SSM_CHUNK, SSM_GN), F32), pltpu.VMEM((SSM_SEG_LEN, SSM_GN), F32),
                        pltpu.VMEM((SSM_SEG_LEN, SSM_GN), F32), pltpu.VMEM((1, SSM_GN), F32),
                        pltpu.VMEM((1, SSM_GN), F32)],
        compiler_params=_params(("arbitrary",)),
    )(*[proj] * nt, ab_re, ab_im, bblk_re, bblk_im, cblk_re, cblk_im, d_row)
    return res[:nt], res[nt], res[nt + 1]


def _ssm_bwd(proj, dy, cin_re, cin_im, ab_re, ab_im, bblk_re, bblk_im, cblk_re, cblk_im, d_row, name):
    n_rows = proj.shape[0]
    nchunk = n_rows // SSM_CHUNK
    nt = SSM_TILES

    def body(*refs):
        u_ref, dy_ref, du_ref = refs[:nt], refs[nt:2 * nt], refs[2 * nt + 9:3 * nt + 9]
        cin_re_ref, cin_im_ref, ar_ref, ai_ref, br_ref, bi_ref, cr_ref, ci_ref, d_ref = refs[2 * nt:2 * nt + 9]
        (dar_ref, dai_ref, dbr_ref, dbi_ref, dcr_ref, dci_ref, dd_ref,
         u_s, dy_s, s_re, s_im, q_re, q_im, pw_re, pw_im, qst_re, qst_im, qin_re, qin_im) = refs[3 * nt + 9:]

        @pl.when(pl.program_id(0) == 0)
        def _():
            _ssm_powers(ar_ref, ai_ref, pw_re, pw_im)
            qst_re[...] = jnp.zeros_like(qst_re)
            qst_im[...] = jnp.zeros_like(qst_im)
            for ref in (dar_ref, dai_ref, dbr_ref, dbi_ref, dcr_ref, dci_ref, dd_ref):
                ref[...] = jnp.zeros_like(ref)

        _ssm_load_rows(u_ref, u_s)
        _ssm_load_rows(dy_ref, dy_s)
        _ssm_input_proj(u_s, br_ref, bi_ref, s_re, s_im)
        _ssm_scan(ar_ref, ai_ref, s_re, s_im, cin_re_ref, cin_im_ref, conj=False, reverse=False)
        for t in range(SSM_TILES):
            cols = slice(t * SSM_TILE_GN, (t + 1) * SSM_TILE_GN)
            dyb = dy_s[:, t * LANES:(t + 1) * LANES].astype(BF16)
            q_re[:, cols] = _nt(dyb, cr_ref[t])
            q_im[:, cols] = -_nt(dyb, ci_ref[t])
            dcr_ref[t] += _tn(s_re[:, cols].astype(BF16), dyb)
            dci_ref[t] -= _tn(s_im[:, cols].astype(BF16), dyb)
        _ssm_scan(ar_ref, ai_ref, q_re, q_im, None, None, conj=True, reverse=True)
        last = SSM_SEG_LEN - 1
        p_re, p_im = pw_re[last:last + 1, :], -pw_im[last:last + 1, :]
        c_re, c_im = qst_re[...], qst_im[...]
        for j in reversed(range(SSM_SEGS)):
            qin_re[j:j + 1, :] = c_re
            qin_im[j:j + 1, :] = c_im
            c_re, c_im = _cmul_add(p_re, p_im, c_re, c_im, q_re[j:j + 1, :], q_im[j:j + 1, :])
        qst_re[...] = c_re
        qst_im[...] = c_im
        _ssm_fixup(pw_re, pw_im, q_re, q_im, qin_re, qin_im, conj=True, reverse=True)
        for t in range(SSM_TILES):
            cols = slice(t * SSM_TILE_GN, (t + 1) * SSM_TILE_GN)

            def step(i, acc, cols=cols):
                rows = pl.ds(pl.multiple_of(i * SSM_SEGS, SSM_SEGS), SSM_SEGS)
                prev = pl.ds(pl.multiple_of((i - 1) * SSM_SEGS, SSM_SEGS), SSM_SEGS)
                qr, qi = q_re[rows, cols], q_im[rows, cols]
                sr, si = s_re[prev, cols], s_im[prev, cols]
                return acc[0] + qr * sr + qi * si, acc[1] + qi * sr - qr * si

            qr, qi = q_re[0:SSM_SEGS, cols], q_im[0:SSM_SEGS, cols]
            sr, si = cin_re_ref[:, cols], cin_im_ref[:, cols]
            acc = lax.fori_loop(1, SSM_SEG_LEN, step, (qr * sr + qi * si, qi * sr - qr * si))
            dar_ref[:, cols] += jnp.sum(acc[0], axis=0, keepdims=True)
            dai_ref[:, cols] += jnp.sum(acc[1], axis=0, keepdims=True)
        for t in range(SSM_TILES):
            cols = slice(t * SSM_TILE_GN, (t + 1) * SSM_TILE_GN)
            lanes = slice(t * LANES, (t + 1) * LANES)
            qrb, qib = q_re[:, cols].astype(BF16), q_im[:, cols].astype(BF16)
            u_t, dy_t = u_s[:, lanes], dy_s[:, lanes]
            ub = u_t.astype(BF16)
            dbr_ref[t] += _tn(ub, qrb)
            dbi_ref[t] += _tn(ub, qib)
            dd_ref[:, lanes] += jnp.sum(dy_t * u_t, axis=0, keepdims=True)
            u_s[:, lanes] = _nt(qrb, br_ref[t]) + _nt(qib, bi_ref[t]) + dy_t * d_ref[:, lanes]
        _ssm_store_rows(u_s, du_ref)

    whole2 = lambda a: pl.BlockSpec(a.shape, lambda c: (0, 0))
    whole3 = lambda a: pl.BlockSpec(a.shape, lambda c: (0, 0, 0))
    back = lambda c: nchunk - 1 - c
    seg = pl.BlockSpec((SSM_SEGS, SSM_GN), lambda c: (back(c), 0))
    gn_row = jax.ShapeDtypeStruct((1, SSM_GN), F32)
    b_shape = jax.ShapeDtypeStruct((SSM_TILES, LANES, SSM_TILE_GN), F32)
    c_shape = jax.ShapeDtypeStruct((SSM_TILES, SSM_TILE_GN, LANES), F32)
    d_shape = jax.ShapeDtypeStruct((1, SSM_WIDTH), F32)
    big = pltpu.VMEM((SSM_CHUNK, SSM_GN), F32)
    res = pl.pallas_call(
        body,
        name=name,
        grid=(nchunk,),
        in_specs=_lane_tile_specs((IN_WIDTH - SSM_WIDTH) // LANES, back) + _lane_tile_specs(0, back) + [
            seg, seg, whole2(ab_re), whole2(ab_im), whole3(bblk_re), whole3(bblk_im), whole3(cblk_re),
            whole3(cblk_im), whole2(d_row)],
        out_specs=[pl.BlockSpec((SSM_CHUNK, LANES), lambda c: (back(c), 0))] * nt + [
            whole2(ab_re), whole2(ab_im), whole3(bblk_re), whole3(bblk_im), whole3(cblk_re), whole3(cblk_im),
            whole2(d_row)],
        out_shape=[jax.ShapeDtypeStruct((n_rows, LANES), F32)] * nt + [gn_row, gn_row, b_shape, b_shape, c_shape,
                                                                       c_shape, d_shape],
        scratch_shapes=[pltpu.VMEM((SSM_CHUNK, SSM_WIDTH), F32), pltpu.VMEM((SSM_CHUNK, SSM_WIDTH), F32),
                        big, big, big, big,
                        pltpu.VMEM((SSM_SEG_LEN, SSM_GN), F32), pltpu.VMEM((SSM_SEG_LEN, SSM_GN), F32),
                        pltpu.VMEM((1, SSM_GN), F32), pltpu.VMEM((1, SSM_GN), F32),
                        pltpu.VMEM((SSM_SEGS, SSM_GN), F32), pltpu.VMEM((SSM_SEGS, SSM_GN), F32)],
        compiler_params=_params(("arbitrary",)),
    )(*[proj] * nt, *[dy] * nt, cin_re, cin_im, ab_re, ab_im, bblk_re, bblk_im, cblk_re, cblk_im, d_row)
    return (res[:nt], *res[nt:])


def _mlp_fwd(h2, w1, w2, tm, tf, name):
    n_rows, dm = h2.shape
    dff = w1.shape[1]

    def body(h_ref, w1_ref, w2_ref, a_ref, y_ref):
        a = _nn(h_ref[...], w1_ref[...])
        a_ref[...] = a.astype(BF16)
        r = jnp.maximum(a, 0.0)
        part = _nn((r * r).astype(BF16), w2_ref[...])
        j = pl.program_id(1)

        @pl.when(j == 0)
        def _():
            y_ref[...] = part

        @pl.when(j > 0)
        def _():
            y_ref[...] += part

    return pl.pallas_call(
        body,
        name=name,
        grid=(n_rows // tm, dff // tf),
        in_specs=[pl.BlockSpec((tm, dm), lambda i, j: (i, 0)), pl.BlockSpec((dm, tf), lambda i, j: (0, j)),
                  pl.BlockSpec((tf, dm), lambda i, j: (j, 0))],
        out_specs=[pl.BlockSpec((tm, tf), lambda i, j: (i, j)), pl.BlockSpec((tm, dm), lambda i, j: (i, 0))],
        out_shape=[jax.ShapeDtypeStruct((n_rows, dff), BF16), jax.ShapeDtypeStruct((n_rows, dm), F32)],
        compiler_params=_params(("parallel", "arbitrary")),
    )(h2, w1, w2)


def _mlp_bwd(dy, h2, a, w2, tm, tf, name):
    n_rows, dm = h2.shape
    dff = a.shape[1]
    per_chip = dff // N_CHIPS // tf

    def body(dy_ref, h_ref, a_ref, w2_ref, da_ref, dw2_ref, dw1_ref):
        dyb = dy_ref[...]
        r = jnp.maximum(a_ref[...].astype(F32), 0.0)
        da = (_nt(dyb, w2_ref[...]) * (2.0 * r)).astype(BF16)
        da_ref[...] = da
        p2 = _tn((r * r).astype(BF16), dyb)
        p1 = _tn(h_ref[...], da)
        i = pl.program_id(1)

        @pl.when(i == 0)
        def _():
            dw2_ref[...] = p2
            dw1_ref[...] = p1

        @pl.when(i > 0)
        def _():
            dw2_ref[...] += p2
            dw1_ref[...] += p1

    return pl.pallas_call(
        body,
        name=name,
        grid=(dff // tf, n_rows // tm),
        in_specs=[pl.BlockSpec((tm, dm), lambda j, i: (i, 0)), pl.BlockSpec((tm, dm), lambda j, i: (i, 0)),
                  pl.BlockSpec((tm, tf), lambda j, i: (i, j)), pl.BlockSpec((tf, dm), lambda j, i: (j, 0))],
        out_specs=[pl.BlockSpec((tm, tf), lambda j, i: (i, j)), pl.BlockSpec((tf, dm), lambda j, i: (j, 0)),
                   pl.BlockSpec((None, dm, tf), lambda j, i: (j // per_chip, 0, j % per_chip))],
        out_shape=[jax.ShapeDtypeStruct((n_rows, dff), BF16), jax.ShapeDtypeStruct((dff, dm), F32),
                   jax.ShapeDtypeStruct((N_CHIPS, dm, dff // N_CHIPS), F32)],
        compiler_params=_params(("parallel", "arbitrary")),
    )(dy, h2, a, w2)


def _local_step(x, pos_col, mod, target, wts, small, hooks=None):
    n_rows = x.shape[0]
    sh1, sc1, gt1, sh2, sc2, gt2 = (mod[:, i * D_MODEL:(i + 1) * D_MODEL] for i in range(N_MOD))
    tm = 256
    d_acc = (1, D_MODEL)

    (h1,) = _rowwise(lambda r, c: [_mod_norm(r[0], *c)], [x], [small["g_pre_mix"], sc1, sh1],
                     [(D_MODEL, BF16)], [], tm, "pre_mix_fwd")
    proj = _matmul(h1, wts["w_in"], "nn", F32, 512, 1408, 2048, "in_proj")

    freqs = ROPE_THETA ** (-jnp.arange(0, ROT_DIM, 2, dtype=F32) / ROT_DIM)
    freq_lane = jnp.tile(freqs, LANES // (ROT_DIM // 2))[None, :]
    tabs = _rope_tables(pos_col, freq_lane, "rope_tables")
    att = [_att_fwd(proj, tabs, gi, f"att_fwd_{gi}") for gi in range(3)]

    expand = jnp.asarray(_expand_np())
    b_re2, b_im2 = small["ssm_b_re"].reshape(SSM_G, -1), small["ssm_b_im"].reshape(SSM_G, -1)
    log_dt = small["ssm_log_dt"].reshape(SSM_G, 1)
    prep_in = (small["ssm_a_re"], small["ssm_a_im"], log_dt, b_re2, b_im2, expand)
    ab_re, ab_im, bb_re, bb_im = _ssm_prep(*prep_in, "ssm_prep")
    ab_re_row, ab_im_row = ab_re.reshape(1, SSM_GN), ab_im.reshape(1, SSM_GN)
    bblk = [_block_diag_in(t.reshape(SSM_G, SSM_N, SSM_P)).astype(BF16) for t in (bb_re, bb_im)]
    cblk = [_block_diag_out(small[k]).astype(BF16) for k in ("ssm_c_re", "ssm_c_im")]
    d_row = small["ssm_d"].reshape(1, SSM_WIDTH)
    y_tiles, cin_re, cin_im = _ssm_fwd(proj, ab_re_row, ab_im_row, *bblk, *cblk, d_row, "ssm_fwd")
    y_tiles = list(y_tiles)
    n_mix = 6 + SSM_TILES

    def mixers_out(r, c):
        w_glu, b_glu, g_att, g_ssm = c
        att_n = _att_mix(*r[:6], g_att)
        y2 = _gelu(jnp.concatenate(r[6:n_mix], axis=1))
        z = _nn(y2.astype(BF16), w_glu) + b_glu
        return [jnp.concatenate([att_n.astype(BF16), _glu_out(y2, z, g_ssm).astype(BF16)], axis=1)]

    att_rows = [a[0] for a in att] + [a[1] for a in att]
    mix_consts = [wts["w_glu"], small["b_glu"], small["g_attn_out"], small["g_ssm_out"]]
    (cat,) = _rowwise(mixers_out, att_rows + y_tiles, mix_consts, [(OUT_IN_WIDTH, BF16)], [], tm, "mixers_out_fwd")
    mix = _matmul(cat, wts["w_out"], "nn", F32, 512, 1024, 1280, "out_proj")

    post_consts = [small["g_post_mix"], gt1, small["g_pre_mlp"], sc2, sh2]
    x1, h2 = _rowwise(lambda r, c: list(_post_mix(r[0], r[1], *c)), [x, mix], post_consts,
                      [(D_MODEL, F32), (D_MODEL, BF16)], [], tm, "post_mix_fwd")
    w_mlp_in, w_mlp_out = (wts["w_mlp_in"], wts["w_mlp_out"]) if hooks is None else hooks.mlp_weights(h2)
    a_mlp, y_mlp = _mlp_fwd(h2, w_mlp_in, w_mlp_out, 1024, 512, "mlp_fwd")

    def loss_head(r, c):
        x1_v, y_v, t_v = r
        g, gt = c
        fn = lambda y_, g_, gt_: gt_ * _rms(y_, g_)
        out, vjp = jax.vjp(fn, y_v, g, gt)
        err = x1_v + out - t_v
        dx2 = err * (1.0 / D_MODEL)
        dy, dg, dgt = vjp(dx2)
        loss = 0.5 * jnp.sum(jnp.sum(err * err, axis=1, keepdims=True), axis=0, keepdims=True) * (1.0 / D_MODEL)
        return [dx2, dy, loss, dg, dgt]

    dx2, dy_mlp, loss, dg_post_mlp, dgt2 = _rowwise(
        loss_head, [x1, y_mlp, target], [small["g_post_mlp"], gt2],
        [(D_MODEL, F32), (D_MODEL, BF16)], [(1, 1), d_acc, d_acc], tm, "loss_head")

    da_mlp, dw_mlp_out, dw_mlp_in = _mlp_bwd(dy_mlp, h2, a_mlp, w_mlp_out, 1024, 512, "mlp_bwd")
    dw_mlp_out = dw_mlp_out.reshape(dw_mlp_in.shape)
    sent = None if hooks is None else hooks.mlp_grads_to_sibling(dw_mlp_in, dw_mlp_out)
    dh2 = _matmul(da_mlp, w_mlp_in, "nt", F32, 512, 1024, 2048, "mlp_in_bwd", after=sent)
    sent = None if hooks is None else hooks.mlp_grads_to_chips(dh2)

    def post_mix_bwd(r, c):
        x_v, mix_v, dx1_v, dh2_v = r
        _, vjp = jax.vjp(_post_mix, x_v, mix_v, *c)
        return list(vjp((dx1_v, dh2_v)))

    post_consts_bwd = post_consts if sent is None else [_tie(post_consts[0], sent)] + post_consts[1:]
    dx_a, dmix, dg_post_mix, dgt1, dg_pre_mlp, dsc2, dsh2 = _rowwise(
        post_mix_bwd, [x, mix, dx2, dh2], post_consts_bwd, [(D_MODEL, F32), (D_MODEL, BF16)], [d_acc] * 5, tm,
        "post_mix_bwd")

    dcat = _matmul(dmix, wts["w_out"], "nt", F32, 512, 1280, 2048, "out_proj_bwd")
    dw_out = _matmul(cat, dmix, "tn", F32, 1280, 1024, 512, "out_proj_wgrad")

    def mixers_out_bwd(r, c):
        w_glu, b_glu, g_att, g_ssm = c
        dcat_v = r[n_mix]
        _, vjp_att = jax.vjp(_att_mix, *r[:6], g_att)
        *d_ol, dg_att = vjp_att(dcat_v[:, :KV_WIDTH])
        y2, vjp_gelu = jax.vjp(_gelu, jnp.concatenate(r[6:n_mix], axis=1))
        y2b = y2.astype(BF16)
        z = _nn(y2b, w_glu) + b_glu
        _, vjp_glu = jax.vjp(_glu_out, y2, z, g_ssm)
        dy2, dz, dg_ssm = vjp_glu(dcat_v[:, KV_WIDTH:])
        dzb = dz.astype(BF16)
        (dy,) = vjp_gelu(dy2 + _nt(dzb, w_glu))
        return d_ol + [dy, dg_att, _tn(y2b, dzb), jnp.sum(dz, axis=0, keepdims=True), dg_ssm]

    *d_att, dy_ssm, dg_attn_out, dw_glu, db_glu, dg_ssm_out = _rowwise(
        mixers_out_bwd, att_rows + y_tiles + [dcat], mix_consts,
        [(KV_WIDTH, F32)] * 6 + [(SSM_WIDTH, F32)],
        [(1, KV_WIDTH), (SSM_WIDTH, SSM_WIDTH), (1, SSM_WIDTH), (1, SSM_WIDTH)], tm, "mixers_out_bwd")

    du_tiles, dab_re, dab_im, dbblk_re, dbblk_im, dcblk_re, dcblk_im, dd_row = _ssm_bwd(
        proj, dy_ssm, cin_re, cin_im, ab_re_row, ab_im_row, *bblk, *cblk, d_row, "ssm_bwd")
    prep_cts = (dab_re.reshape(SSM_G, SSM_N), dab_im.reshape(SSM_G, SSM_N),
                _block_diag_in_grad(dbblk_re).reshape(SSM_G, -1), _block_diag_in_grad(dbblk_im).reshape(SSM_G, -1))
    da_re, da_im, dlog_dt, db_re, db_im = _ssm_prep_bwd(*prep_in, prep_cts, "ssm_prep_bwd")

    dqkv = [_att_bwd(proj, tabs, att[gi][0], att[gi][1], d_att[gi], d_att[3 + gi], gi, f"att_bwd_{gi}")
            for gi in range(3)]

    def gather_dproj(r, c):
        dq = [r[0], r[3], r[6]]
        dk = r[1] + r[4] + r[7]
        dv = r[2] + r[5] + r[8]
        return [jnp.concatenate([t.astype(BF16) for t in dq + [dk, dv] + r[9:]], axis=1)]

    (dproj,) = _rowwise(gather_dproj, [t for g in dqkv for t in g] + list(du_tiles), [], [(IN_WIDTH, BF16)], [], tm,
                        "gather_dproj")
    dh1 = _matmul(dproj, wts["w_in"], "nt", F32, 512, 1024, 2816, "in_proj_bwd")
    dw_in = _matmul(h1, dproj, "tn", F32, 1024, 1408, 512, "in_proj_wgrad")

    def pre_mix_bwd(r, c):
        x_v, dh1_v, dxa_v = r
        _, vjp = jax.vjp(_mod_norm, x_v, *c)
        dx, dg, dsc, dsh = vjp(dh1_v)
        return [dx + dxa_v, dg, dsc, dsh]

    grad_x, dg_pre_mix, dsc1, dsh1 = _rowwise(
        pre_mix_bwd, [x, dh1, dx_a], [small["g_pre_mix"], sc1, sh1], [(D_MODEL, F32)], [d_acc] * 3, tm, "pre_mix_bwd")

    dmod = jnp.concatenate([dsh1, dsc1, dgt1, dsh2, dsc2, dgt2], axis=1)
    big = dict(w_in=dw_in, w_out=dw_out, w_mlp_in=dw_mlp_in, w_mlp_out=dw_mlp_out, w_glu=dw_glu)
    small_g = dict(
        g_pre_mix=dg_pre_mix, g_post_mix=dg_post_mix, ssm_a_re=da_re, ssm_a_im=da_im,
        ssm_log_dt=dlog_dt.reshape(1, SSM_G), ssm_b_re=db_re.reshape(SSM_G, SSM_N, SSM_P),
        ssm_b_im=db_im.reshape(SSM_G, SSM_N, SSM_P), ssm_c_re=_block_diag_out_grad(dcblk_re),
        ssm_c_im=_block_diag_out_grad(dcblk_im), ssm_d=dd_row.reshape(SSM_G, SSM_P), b_glu=db_glu,
        g_attn_out=dg_attn_out, g_ssm_out=dg_ssm_out, g_pre_mlp=dg_pre_mlp, g_post_mlp=dg_post_mlp)
    return loss, grad_x, dmod, big, small_g


MESH_ID = pl.DeviceIdType.MESH
N_DEV = 8
N_CHIPS = 4
HBM_SPEC = pl.BlockSpec(memory_space=pltpu.HBM)


def _place():
    x, y, c = lax.axis_index("x"), lax.axis_index("y"), lax.axis_index("c")
    other_chips = [(1 - x, y), (x, 1 - y), (1 - x, 1 - y)]
    return x, y, c, other_chips


def _half_rows(index, half):
    return pl.ds(pl.multiple_of(index * half, ROW_PAD), half)


def _remote(src, dst, send_sem, recv_sem, dev):
    return pltpu.make_async_remote_copy(src_ref=src, dst_ref=dst, send_sem=send_sem, recv_sem=recv_sem,
                                        device_id=dev, device_id_type=MESH_ID)


def _all_gather8(block, name):
    m_per, n = block.shape

    def body(x_ref, out_ref, send_sems, recv_sems, local_sem):
        x, y, c, chips = _place()
        me, sibling = (x, y, c), (x, y, 1 - c)

        def rows(px, py, pc):
            return out_ref.at[pl.ds((4 * px + 2 * py + pc) * m_per, m_per), :]

        def copy(k, blk, to, src=None):
            return _remote(rows(*blk) if src is None else src, rows(*blk), send_sems.at[k], recv_sems.at[k], to)

        mine = pltpu.make_async_copy(x_ref, rows(*me), local_sem)
        mine.start()
        first = [copy(0, me, sibling, src=x_ref)]
        first += [copy(1 + j, me, (*chip, c), src=x_ref) for j, chip in enumerate(chips)]
        for cp in first:
            cp.start()
        passed = [copy(4 + j, (*chip, c), sibling) for j, chip in enumerate(chips)]
        for j, chip in enumerate(chips):
            copy(1 + j, (*chip, c), me).wait_recv()
            passed[j].start()
        copy(0, sibling, me).wait_recv()
        for j, chip in enumerate(chips):
            copy(4 + j, (*chip, 1 - c), me).wait_recv()
        for cp in first + passed:
            cp.wait_send()
        mine.wait()

    return pl.pallas_call(
        body,
        name=name,
        out_shape=jax.ShapeDtypeStruct((N_DEV * m_per, n), block.dtype),
        in_specs=[pl.BlockSpec(memory_space=pltpu.VMEM)],
        out_specs=pl.BlockSpec(memory_space=pltpu.VMEM),
        scratch_shapes=[pltpu.SemaphoreType.DMA((7,)), pltpu.SemaphoreType.DMA((7,)), pltpu.SemaphoreType.DMA],
        compiler_params=_params(),
    )(block)


def _weight_gather(shards, name):
    n = len(shards)
    shapes = [s.shape for s in shards]

    def body(*refs):
        ins, outs = refs[:n], refs[n:2 * n]
        send, recv, fsend, frecv = refs[2 * n:]
        x, y, c, chips = _place()
        k_me = 2 * x + y
        sibling = (x, y, 1 - c)
        pending = []
        for a in range(n):
            half = shapes[a][0] // 2
            mine = _half_rows(c, half)
            for j, chip in enumerate(chips):
                cp = _remote(ins[a].at[mine, :], outs[a].at[k_me, mine, :], send.at[a, j], recv.at[a, j], (*chip, c))
                cp.start()
                pending.append(cp.wait_send)
        for a in range(n):
            half = shapes[a][0] // 2
            for j, (px, py) in enumerate(chips):
                piece = outs[a].at[2 * px + py, _half_rows(c, half), :]
                _remote(piece, piece, send.at[a, j], recv.at[a, j], (px, py, c)).wait_recv()
                fw = _remote(piece, piece, fsend.at[a, j], frecv.at[a, j], sibling)
                fw.start()
                pending.append(fw.wait_send)
        for a in range(n):
            half = shapes[a][0] // 2
            for j, (px, py) in enumerate(chips):
                piece = outs[a].at[2 * px + py, _half_rows(1 - c, half), :]
                _remote(piece, piece, fsend.at[a, j], frecv.at[a, j], sibling).wait_recv()
        for wait in pending:
            wait()

    sems = pltpu.SemaphoreType.DMA((n, 3))
    return pl.pallas_call(
        body,
        name=name,
        out_shape=[jax.ShapeDtypeStruct((N_CHIPS,) + s, BF16) for s in shapes],
        in_specs=[HBM_SPEC] * n,
        out_specs=[HBM_SPEC] * n,
        scratch_shapes=[sems, sems, sems, sems],
        compiler_params=_params(),
    )(*shards)


def _sibling_halves(stacks, name):
    n = len(stacks)
    shapes = [s.shape for s in stacks]

    def body(*refs):
        ins, outs = refs[:n], refs[n:2 * n]
        send, recv = refs[2 * n:]
        x, y, c, _ = _place()
        copies = []
        for a in range(n):
            half = shapes[a][1] // 2
            cp = _remote(ins[a].at[:, _half_rows(1 - c, half), :], outs[a], send.at[a], recv.at[a], (x, y, 1 - c))
            cp.start()
            copies.append(cp)
        for cp in copies:
            cp.wait()

    return pl.pallas_call(
        body,
        name=name,
        out_shape=[jax.ShapeDtypeStruct((N_CHIPS, s[1] // 2, s[2]), F32) for s in shapes],
        in_specs=[HBM_SPEC] * n,
        out_specs=[HBM_SPEC] * n,
        scratch_shapes=[pltpu.SemaphoreType.DMA((n,)), pltpu.SemaphoreType.DMA((n,))],
        compiler_params=_params(),
    )(*stacks)


def _chip_exchange(parts, name):
    n = len(parts)
    shapes = [p.shape for p in parts]

    def body(*refs):
        ins, outs = refs[:n], refs[n:2 * n]
        send, recv = refs[2 * n:]
        x, y, c, chips = _place()
        copies = []
        for a in range(n):
            for j, (px, py) in enumerate(chips):
                cp = _remote(ins[a].at[2 * px + py], outs[a].at[j], send.at[a, j], recv.at[a, j], (px, py, c))
                cp.start()
                copies.append(cp)
        for cp in copies:
            cp.wait()

    return pl.pallas_call(
        body,
        name=name,
        out_shape=[jax.ShapeDtypeStruct((3,) + s[1:], BF16) for s in shapes],
        in_specs=[HBM_SPEC] * n,
        out_specs=[HBM_SPEC] * n,
        scratch_shapes=[pltpu.SemaphoreType.DMA((n, 3)), pltpu.SemaphoreType.DMA((n, 3))],
        compiler_params=_params(),
    )(*parts)


def _sibling_swap(halves, name):
    n = len(halves)
    shapes = [h.shape for h in halves]

    def body(*refs):
        ins, outs = refs[:n], refs[n:2 * n]
        send, recv = refs[2 * n:]
        x, y, c, _ = _place()
        pending = []
        for a in range(n):
            half = shapes[a][0]
            mine = outs[a].at[_half_rows(c, half), :]
            cp = _remote(ins[a], mine, send.at[a], recv.at[a], (x, y, 1 - c))
            cp.start()
            pending.append(cp.wait_send)
        for a in range(n):
            half = shapes[a][0]
            theirs = outs[a].at[_half_rows(1 - c, half), :]
            _remote(theirs, theirs, send.at[a], recv.at[a], (x, y, 1 - c)).wait_recv()
        for wait in pending:
            wait()

    return pl.pallas_call(
        body,
        name=name,
        out_shape=[jax.ShapeDtypeStruct((2 * s[0], s[1]), F32) for s in shapes],
        in_specs=[HBM_SPEC] * n,
        out_specs=[HBM_SPEC] * n,
        scratch_shapes=[pltpu.SemaphoreType.DMA((n,)), pltpu.SemaphoreType.DMA((n,))],
        compiler_params=_params(),
    )(*halves)


SEM_SPEC = pl.BlockSpec(memory_space=pltpu.SEMAPHORE)
ANY_SPEC = pl.BlockSpec(memory_space=pl.ANY)
DATAFLOW = pltpu.SideEffectType.DATAFLOW_SIDE_EFFECTING


def _split_copy_start(srcs, lands, plan, n_sems, name, after=None):
    bufs = list(srcs) + list(lands)
    ns, nb = len(srcs), len(bufs)
    extra = [] if after is None else [after]

    def body(*refs):
        outs = refs[nb + len(extra):]
        for outgoing, _ in plan(refs[:ns], refs[ns:nb], outs[0], outs[1]):
            outgoing.start()
        outs[-1][...] = jnp.zeros_like(outs[-1])

    sems = pltpu.SemaphoreType.DMA((n_sems,))
    return pl.pallas_call(
        body,
        name=name,
        out_shape=(sems, sems, *[pltpu.HBM(b.shape, b.dtype) for b in bufs], jax.ShapeDtypeStruct((8, LANES), F32)),
        in_specs=[HBM_SPEC] * nb + [ANY_SPEC] * len(extra),
        out_specs=(SEM_SPEC, SEM_SPEC, *[HBM_SPEC] * nb, pl.BlockSpec(memory_space=pltpu.VMEM)),
        input_output_aliases={i: 2 + i for i in range(nb)},
        compiler_params=pltpu.CompilerParams(has_side_effects=DATAFLOW),
    )(*[pltpu.with_memory_space_constraint(b, pltpu.HBM) for b in bufs], *extra)


def _split_copy_wait(started, plan, after, name):
    send, recv, *bufs = started[:-1]
    nb = len(bufs)
    ns = nb // 2

    def body(*refs):
        for outgoing, incoming in plan(refs[:ns], refs[ns:nb], refs[nb], refs[nb + 1]):
            outgoing.wait_send()
            incoming.wait_recv()

    return pl.pallas_call(
        body,
        name=name,
        out_shape=tuple(pltpu.HBM(b.shape, b.dtype) for b in bufs),
        in_specs=[HBM_SPEC] * nb + [SEM_SPEC, SEM_SPEC, ANY_SPEC],
        out_specs=tuple([HBM_SPEC] * nb),
        input_output_aliases={i: i for i in range(nb)},
        compiler_params=pltpu.CompilerParams(has_side_effects=DATAFLOW),
    )(*bufs, send, recv, after)


def _weight_plan(shapes):
    def plan(srcs, lands, send, recv):
        x, y, c, chips = _place()
        copies = []
        for a in range(len(shapes)):
            mine = _half_rows(c, shapes[a][0] // 2)
            for j, (px, py) in enumerate(chips):
                s = 3 * a + j
                arrival = lands[a].at[2 * px + py, mine, :]
                copies.append((_remote(srcs[a].at[mine, :], lands[a].at[2 * x + y, mine, :], send.at[s], recv.at[s], (px, py, c)),
                               _remote(arrival, arrival, send.at[s], recv.at[s], (px, py, c))))
        return copies
    return plan


def _halves_plan(shapes):
    def plan(srcs, lands, send, recv):
        x, y, c, _ = _place()
        copies = []
        for a in range(len(shapes)):
            theirs = srcs[a].at[:, _half_rows(1 - c, shapes[a][1] // 2), :]
            copies.append((_remote(theirs, lands[a], send.at[a], recv.at[a], (x, y, 1 - c)),
                           _remote(lands[a], lands[a], send.at[a], recv.at[a], (x, y, 1 - c))))
        return copies
    return plan


def _exchange_plan(n):
    def plan(srcs, lands, send, recv):
        x, y, c, chips = _place()
        copies = []
        for a in range(n):
            for j, (px, py) in enumerate(chips):
                s = 3 * a + j
                copies.append((_remote(srcs[a].at[2 * px + py], lands[a].at[j], send.at[s], recv.at[s], (px, py, c)),
                               _remote(lands[a].at[j], lands[a].at[j], send.at[s], recv.at[s], (px, py, c))))
        return copies
    return plan


def _forward_to_sibling(stacks, name):
    n = len(stacks)
    shapes = [s.shape for s in stacks]

    def body(*refs):
        ins, outs = refs[:n], refs[n:2 * n]
        send, recv = refs[2 * n:]
        x, y, c, chips = _place()
        sibling = (x, y, 1 - c)
        copies = []
        for a in range(n):
            half = shapes[a][1] // 2
            for j, (px, py) in enumerate(chips):
                rows = _half_rows(c, half)
                cp = _remote(ins[a].at[2 * px + py, rows, :], outs[a].at[2 * px + py, rows, :], send.at[a, j],
                             recv.at[a, j], sibling)
                cp.start()
                copies.append(cp)
        for a in range(n):
            half = shapes[a][1] // 2
            for j, (px, py) in enumerate(chips):
                theirs = outs[a].at[2 * px + py, _half_rows(1 - c, half), :]
                _remote(theirs, theirs, send.at[a, j], recv.at[a, j], sibling).wait_recv()
        for cp in copies:
            cp.wait_send()

    return pl.pallas_call(
        body,
        name=name,
        out_shape=[jax.ShapeDtypeStruct(s, BF16) for s in shapes],
        in_specs=[HBM_SPEC] * n,
        out_specs=[HBM_SPEC] * n,
        input_output_aliases={a: a for a in range(n)},
        scratch_shapes=[pltpu.SemaphoreType.DMA((n, 3)), pltpu.SemaphoreType.DMA((n, 3))],
        compiler_params=_params(),
    )(*stacks)


def _tie(x, token):
    return x + token[0:1, 0:1].astype(x.dtype)


ROW_PAD = 16


def _silu(x):
    return x * _sigmoid(x)


def _ada_fwd(c_all, w_ada, b_ada, name):
    dm, cols = w_ada.shape
    tn = 512

    def body(c_ref, w_ref, b_ref, o_ref):
        o_ref[...] = _nn(_silu(c_ref[...]).astype(BF16), w_ref[...].astype(BF16)) + b_ref[...]

    return pl.pallas_call(
        body,
        name=name,
        grid=(cols // tn,),
        in_specs=[pl.BlockSpec((ROW_PAD, dm), lambda j: (0, 0)), pl.BlockSpec((dm, tn), lambda j: (0, j)),
                  pl.BlockSpec((1, tn), lambda j: (0, j))],
        out_specs=pl.BlockSpec((ROW_PAD, tn), lambda j: (0, j)),
        out_shape=jax.ShapeDtypeStruct((ROW_PAD, cols), F32),
        compiler_params=_params(("parallel",)),
    )(c_all, w_ada, b_ada)


def _adamw(w, g, m, v):
    m = ADAM_B1 * m + (1.0 - ADAM_B1) * g
    v = ADAM_B2 * v + (1.0 - ADAM_B2) * (g * g)
    m_hat = m / (1.0 - ADAM_B1 ** ADAM_STEP)
    v_hat = v / (1.0 - ADAM_B2 ** ADAM_STEP)
    delta = -ADAM_LR * (m_hat / (jnp.sqrt(v_hat) + ADAM_EPS) + ADAM_WD * w)
    return delta, m, v


def _ada_bwd_adamw(c_all, dmod_cols, w, m, v, name):
    dm, cols = w.shape
    tm, tn = 512, 512

    def body(c_ref, d_ref, w_ref, m_ref, v_ref, g_ref, dl_ref, nm_ref, nv_ref):
        g = _tn(_silu(c_ref[...]).astype(BF16), d_ref[...].astype(BF16))
        g_ref[...] = g
        dl_ref[...], nm_ref[...], nv_ref[...] = _adamw(w_ref[...], g, m_ref[...], v_ref[...])

    tile = pl.BlockSpec((tm, tn), lambda i, j: (i, j))
    shape = jax.ShapeDtypeStruct((dm, cols), F32)
    return pl.pallas_call(
        body,
        name=name,
        grid=(dm // tm, cols // tn),
        in_specs=[pl.BlockSpec((ROW_PAD, tm), lambda i, j: (0, i)), pl.BlockSpec((ROW_PAD, tn), lambda i, j: (0, j)),
                  tile, tile, tile],
        out_specs=[tile] * 4,
        out_shape=[shape] * 4,
        compiler_params=_params(("parallel", "parallel")),
    )(c_all, dmod_cols, w, m, v)


def _sum_blocks(parts, nblk, name):
    rows, cols = parts.shape[0] // nblk, parts.shape[1]

    def body(p_ref, o_ref):
        tot = p_ref[0:rows, :]
        for b in range(1, nblk):
            tot = tot + p_ref[b * rows:(b + 1) * rows, :]
        o_ref[...] = tot

    return pl.pallas_call(body, name=name, out_shape=jax.ShapeDtypeStruct((rows, cols), F32), compiler_params=_params())(parts)


def _adamw_rows(w, g, m, v, tm, name):
    return _rowwise(lambda r, c: list(_adamw(*r)), [w, g, m, v], [], [(w.shape[1], F32)] * 3, [], tm, name)


BIG = ("w_in", "w_out", "w_mlp_in", "w_mlp_out", "w_glu")
COL_SHARDED = ("w_in", "w_out", "w_mlp_in")
SMALL = ("b_ada", "g_pre_mix", "g_post_mix", "ssm_a_re", "ssm_a_im", "ssm_log_dt", "ssm_b_re", "ssm_b_im",
         "ssm_c_re", "ssm_c_im", "ssm_d", "b_glu", "g_attn_out", "g_ssm_out", "g_pre_mlp", "g_post_mlp")
WEIGHTS = ("w_ada", "b_ada", "g_pre_mix", "g_post_mix", "w_in", "ssm_a_re", "ssm_a_im", "ssm_log_dt", "ssm_b_re",
           "ssm_b_im", "ssm_c_re", "ssm_c_im", "ssm_d", "w_glu", "b_glu", "g_attn_out", "g_ssm_out", "w_out",
           "g_pre_mlp", "g_post_mlp", "w_mlp_in", "w_mlp_out")
FLAT_COLS = 1024
FLAT_ROWS = 256
ROW_TILE = {"w_in": 256, "w_out": 128, "w_mlp_in": 256, "w_mlp_out": 256, "w_glu": 112}


def _flatten_small(tree):
    flat = jnp.concatenate([tree[k].reshape(-1) for k in SMALL])
    return jnp.pad(flat, (0, FLAT_ROWS * FLAT_COLS - flat.shape[0])).reshape(FLAT_ROWS, FLAT_COLS)


def _unflatten_small(flat, like):
    flat = flat.reshape(-1)
    out, at = {}, 0
    for k in SMALL:
        size = math.prod(like[k].shape)
        out[k] = flat[at:at + size].reshape(like[k].shape)
        at += size
    return out


def _unstack(stack, name):
    if name in COL_SHARDED:
        return stack.transpose(1, 0, 2).reshape(stack.shape[1], N_CHIPS * stack.shape[2])
    return stack.reshape(N_CHIPS * stack.shape[1], stack.shape[2])


def _stack(full, name):
    if name in COL_SHARDED:
        return full.reshape(full.shape[0], N_CHIPS, full.shape[1] // N_CHIPS).transpose(1, 0, 2)
    return full.reshape(N_CHIPS, full.shape[0] // N_CHIPS, full.shape[1])


EARLY = ("w_in", "w_out", "w_glu")
LATE = ("w_mlp_in", "w_mlp_out")


def _chip_sums(names, g_stacks, from_sibling, ic, chip):
    own, to_send = [], []
    place = jnp.stack([ic, chip]).astype(jnp.int32)
    for k, gs, fs in zip(names, g_stacks, from_sibling):
        _, rows, cols = gs.shape
        half, tm = rows // 2, ROW_TILE[k]
        nt = half // tm

        def body(place_ref, g_ref, f_ref, own_ref, send_ref):
            s = g_ref[...] + f_ref[...]
            send_ref[...] = s.astype(BF16)

            @pl.when(pl.program_id(1) == place_ref[1])
            def _():
                own_ref[...] = s

        slab = lambda index: pl.BlockSpec((None, tm, cols), index)
        mine, to_chips = pl.pallas_call(
            body,
            name="grad_chip_sum_" + k,
            grid_spec=pltpu.PrefetchScalarGridSpec(
                num_scalar_prefetch=1,
                grid=(nt, N_CHIPS),
                in_specs=[slab(lambda i, kk, p, nt=nt: (kk, p[0] * nt + i, 0)), slab(lambda i, kk, p: (kk, i, 0))],
                out_specs=[pl.BlockSpec((tm, cols), lambda i, kk, p: (i, 0)), slab(lambda i, kk, p: (kk, i, 0))]),
            out_shape=[jax.ShapeDtypeStruct((half, cols), F32), jax.ShapeDtypeStruct((N_CHIPS, half, cols), BF16)],
            compiler_params=_params(("arbitrary", "arbitrary")),
        )(place, gs, fs)
        own.append(mine)
        to_send.append(to_chips)
    return own, to_send


def _grad_totals(names, own, from_chips):
    totals = []
    for k, mine, fc in zip(names, own, from_chips):
        half, cols = mine.shape
        tm = ROW_TILE[k]

        def body(m_ref, a_ref, b_ref, c_ref, o_ref):
            o_ref[...] = m_ref[...] + a_ref[...].astype(F32) + b_ref[...].astype(F32) + c_ref[...].astype(F32)

        rows = pl.BlockSpec((tm, cols), lambda i: (i, 0))
        totals.append(pl.pallas_call(
            body,
            name="grad_total_" + k,
            grid=(half // tm,),
            in_specs=[rows] + [pl.BlockSpec((None, tm, cols), lambda i, j=j: (j, i, 0)) for j in range(3)],
            out_specs=rows,
            out_shape=jax.ShapeDtypeStruct((half, cols), F32),
            compiler_params=_params(("parallel",)),
        )(mine, fc, fc, fc))
    return totals


class _Overlap:
    def __init__(self, own_shards, ic, chip, after):
        self.ic, self.chip = ic, chip
        self.shapes = [o.shape for o in own_shards]
        lands = [lax.empty((N_CHIPS,) + s, BF16) for s in self.shapes]
        self.gather = _split_copy_start(own_shards, lands, _weight_plan(self.shapes), 3 * len(LATE),
                                        "mlp_weight_gather_start", after=after)
        self.token = self.gather[-1]

    def mlp_weights(self, after):
        n = len(LATE)
        done = _split_copy_wait(self.gather, _weight_plan(self.shapes), after, "mlp_weight_gather_wait")
        own, stacks = done[:n], done[n:]
        stacks = _forward_to_sibling(stacks, "mlp_weight_forward")
        stacks = [lax.dynamic_update_index_in_dim(s, o, self.chip, 0) for s, o in zip(stacks, own)]
        return [_unstack(s, k) for k, s in zip(LATE, stacks)]

    def mlp_grads_to_sibling(self, dw_in, dw_out):
        stacks = [dw_in, dw_out]
        self.g_shapes = [s.shape for s in stacks]
        lands = [lax.empty((N_CHIPS, s[1] // 2, s[2]), F32) for s in self.g_shapes]
        self.halves = _split_copy_start(stacks, lands, _halves_plan(self.g_shapes), len(LATE), "mlp_grad_halves_start")
        return self.halves[-1]

    def mlp_grads_to_chips(self, after):
        n = len(LATE)
        done = _split_copy_wait(self.halves, _halves_plan(self.g_shapes), after, "mlp_grad_halves_wait")
        self.own, to_send = _chip_sums(LATE, done[:n], done[n:], self.ic, self.chip)
        lands = [lax.empty((3,) + s.shape[1:], BF16) for s in to_send]
        self.exchange = _split_copy_start(to_send, lands, _exchange_plan(n), 3 * n, "mlp_grad_exchange_start")
        return self.exchange[-1]

    def mlp_grads_reduced(self, after):
        n = len(LATE)
        done = _split_copy_wait(self.exchange, _exchange_plan(n), after, "mlp_grad_exchange_wait")
        return _grad_totals(LATE, self.own, done[n:])


def _pad_rows(row):
    return jnp.pad(row, ((0, 8 - row.shape[0]), (0, 0)))


def _every_eighth(gathered):
    rows = gathered.reshape(N_DEV, 8, gathered.shape[1])[:, 0, :]
    return jnp.pad(rows, ((0, ROW_PAD - N_DEV), (0, 0)))


def kernel(x, c, positions, w_ada, b_ada, g_pre_mix, g_post_mix, w_in, ssm_a_re, ssm_a_im, ssm_log_dt, ssm_b_re, ssm_b_im, ssm_c_re, ssm_c_im, ssm_d, w_glu, b_glu, g_attn_out, g_ssm_out, w_out, g_pre_mlp, g_post_mlp, w_mlp_in, w_mlp_out, loss_target, m_w_ada, m_b_ada, m_g_pre_mix, m_g_post_mix, m_w_in, m_ssm_a_re, m_ssm_a_im, m_ssm_log_dt, m_ssm_b_re, m_ssm_b_im, m_ssm_c_re, m_ssm_c_im, m_ssm_d, m_w_glu, m_b_glu, m_g_attn_out, m_g_ssm_out, m_w_out, m_g_pre_mlp, m_g_post_mlp, m_w_mlp_in, m_w_mlp_out, v_w_ada, v_b_ada, v_g_pre_mix, v_g_post_mix, v_w_in, v_ssm_a_re, v_ssm_a_im, v_ssm_log_dt, v_ssm_b_re, v_ssm_b_im, v_ssm_c_re, v_ssm_c_im, v_ssm_d, v_w_glu, v_b_glu, v_g_attn_out, v_g_ssm_out, v_w_out, v_g_pre_mlp, v_g_post_mlp, v_w_mlp_in, v_w_mlp_out):
    given = dict(locals())
    w = {k: given[k][0] for k in WEIGHTS}
    mom = {k: given["m_" + k][0] for k in WEIGHTS}
    var = {k: given["v_" + k][0] for k in WEIGHTS}
    for tree in (w, mom, var):
        for k in ("b_ada", "g_pre_mix", "g_post_mix", "ssm_log_dt", "b_glu", "g_attn_out", "g_ssm_out", "g_pre_mlp",
                  "g_post_mlp"):
            tree[k] = tree[k].reshape(1, -1)
    ix, iy, ic = lax.axis_index("x"), lax.axis_index("y"), lax.axis_index("c")
    chip = 2 * ix + iy
    me = 4 * ix + 2 * iy + ic
    shard_cols = w["w_ada"].shape[1]

    c_all = _every_eighth(_all_gather8(_pad_rows(c), "gather_c"))
    b_ada_cols = lax.dynamic_slice_in_dim(w["b_ada"], chip * shard_cols, shard_cols, axis=1)
    mod_cols = _ada_fwd(c_all, w["w_ada"], b_ada_cols, "ada_fwd")[:N_DEV]
    mod_all = _all_gather8(mod_cols, "gather_mod").reshape(N_CHIPS, 2, N_DEV, shard_cols)[:, 0]
    mod = lax.dynamic_index_in_dim(mod_all, me, axis=1, keepdims=False).reshape(1, N_MOD * D_MODEL)

    early_own = [w[k].astype(BF16) for k in EARLY]
    stacks = _weight_gather(early_own, "weight_gather")
    stacks = [lax.dynamic_update_index_in_dim(s, o, chip, 0) for s, o in zip(stacks, early_own)]
    wts = {k: _unstack(s, k) for k, s in zip(EARLY, stacks)}
    overlap = _Overlap([w[k].astype(BF16) for k in LATE], ic, chip, after=stacks[0])
    mod = _tie(mod, overlap.token)

    small = {k: w[k] for k in SMALL if k != "b_ada"}
    loss, grad_x, dmod, big_g, small_g = _local_step(x[0], positions.reshape(-1, 1), mod, loss_target[0], wts, small,
                                                     hooks=overlap)
    loss = lax.psum(loss[0, 0], ("x", "y", "c"))

    small_g["b_ada"] = dmod
    parts = _all_gather8(_flatten_small(small_g), "gather_small_grads")

    g_stacks = [_stack(big_g[k], k) for k in EARLY]
    from_sibling = _sibling_halves(g_stacks, "grad_sibling_halves")
    chip_f32, chip_bf16 = _chip_sums(EARLY, g_stacks, from_sibling, ic, chip)
    exchange_plan = _exchange_plan(len(EARLY))
    lands = [lax.empty((3,) + s.shape[1:], BF16) for s in chip_bf16]
    exchange = _split_copy_start(chip_bf16, lands, exchange_plan, 3 * len(EARLY), "grad_exchange_start", after=parts)

    small_flat = _sum_blocks(parts, N_DEV, "small_grad_sum")
    grads = _unflatten_small(small_flat, w)

    mod_rows = N_MOD * D_MODEL // FLAT_COLS
    dmod_all = parts.reshape(N_DEV, FLAT_ROWS, FLAT_COLS)[:, :mod_rows].reshape(N_DEV, N_MOD * D_MODEL)
    dmod_all = jnp.pad(dmod_all, ((0, ROW_PAD - N_DEV), (0, 0)))
    dmod_cols = lax.dynamic_slice_in_dim(dmod_all, chip * shard_cols, shard_cols, axis=1)
    g_ada, d_ada, m_ada, v_ada = _ada_bwd_adamw(_tie(c_all, exchange[-1]), dmod_cols, w["w_ada"], mom["w_ada"],
                                                var["w_ada"], "ada_bwd_adamw")
    grads["w_ada"] = g_ada
    delta, new_m, new_v = {"w_ada": d_ada}, {"w_ada": m_ada}, {"w_ada": v_ada}

    def finish(names, reduced, tag):
        swapped = _sibling_swap(reduced, tag)
        for k, s, r in zip(names, swapped, reduced):
            grads[k] = lax.dynamic_update_slice_in_dim(s, r, ic * r.shape[0], axis=0)
            delta[k], new_m[k], new_v[k] = _adamw_rows(w[k], grads[k], mom[k], var[k], ROW_TILE[k], "adamw_" + k)

    finish(LATE, overlap.mlp_grads_reduced(g_ada), "mlp_grad_sibling_swap")
    from_chips = _split_copy_wait(exchange, exchange_plan, new_v[LATE[-1]], "grad_exchange_wait")[len(EARLY):]
    finish(EARLY, _grad_totals(EARLY, chip_f32, from_chips), "grad_sibling_swap")

    flat_upd = _adamw_rows(_flatten_small(w), small_flat, _flatten_small(mom), _flatten_small(var), FLAT_ROWS,
                           "adamw_small")
    for tree, flat in zip((delta, new_m, new_v), flat_upd):
        tree.update(_unflatten_small(flat, w))

    shaped = lambda tree: [tree[k].reshape(given[k].shape) for k in WEIGHTS]
    return (loss, grad_x[None], *shaped(grads), *shaped(delta), *shaped(new_m), *shaped(new_v))
```

```python
import functools
import math

import jax
import jax.numpy as jnp
import numpy as np
from jax import lax
from jax.experimental import pallas as pl
from jax.experimental.pallas import tpu as pltpu

F32 = jnp.float32
BF16 = jnp.bfloat16

D_MODEL = 2048
HEAD_DIM = 64
DILATIONS = (1, 4, 16)
ATT_SPAN = 128
ATT_BLK = 128
HEADS_PER_GROUP = 6
KV_WIDTH = HEADS_PER_GROUP * HEAD_DIM
ATT_Q_WIDTH = 3 * KV_WIDTH
ROT_DIM = 16
ROPE_THETA = 500000.0
SSM_WIDTH = 896
SSM_P = 16
SSM_G = 56
SSM_N = 64
SSM_GN = SSM_G * SSM_N
SSM_TILES = SSM_WIDTH // 128
SSM_TILE_GN = 8 * SSM_N
IN_WIDTH = 2816
OUT_IN_WIDTH = 1280
D_FF = 8192
N_MOD = 6
EPS = 1e-6
LANES = 128
SSM_SEGS = 8
SSM_CHUNK = 256
SSM_SEG_LEN = SSM_CHUNK // SSM_SEGS

ADAM_LR = 0.001
ADAM_B1 = 0.9
ADAM_B2 = 0.999
ADAM_EPS = 1e-08
ADAM_WD = 0.01
ADAM_STEP = 10

VMEM_LIMIT = 56 * 1024 * 1024


def _params(sem=None):
    return pltpu.CompilerParams(dimension_semantics=sem, vmem_limit_bytes=VMEM_LIMIT)


def _dot(a, b, dims):
    return lax.dot_general(a, b, (dims, ((), ())), preferred_element_type=F32)


def _nn(a, b):
    return _dot(a, b, ((1,), (0,)))


def _nt(a, b):
    return _dot(a, b, ((1,), (1,)))


def _tn(a, b):
    return _dot(a, b, ((0,), (0,)))


def _matmul(a, b, mode, out_dtype, tm, tn, tk, name, after=None):
    if mode == "nn":
        (m, k), (_, n) = a.shape, b.shape
        a_spec = pl.BlockSpec((tm, tk), lambda i, j, kk: (i, kk))
        b_spec = pl.BlockSpec((tk, tn), lambda i, j, kk: (kk, j))
        op = _nn
    elif mode == "nt":
        (m, k), (n, _) = a.shape, b.shape
        a_spec = pl.BlockSpec((tm, tk), lambda i, j, kk: (i, kk))
        b_spec = pl.BlockSpec((tn, tk), lambda i, j, kk: (j, kk))
        op = _nt
    else:
        (k, m), (_, n) = a.shape, b.shape
        a_spec = pl.BlockSpec((tk, tm), lambda i, j, kk: (kk, i))
        b_spec = pl.BlockSpec((tk, tn), lambda i, j, kk: (kk, j))
        op = _tn
    assert m % tm == 0 and n % tn == 0 and k % tk == 0, (name, m, n, k)
    nk = k // tk

    def body(a_ref, b_ref, *rest):
        o_ref, acc_ref = rest[-2:]
        kk = pl.program_id(2)

        @pl.when(kk == 0)
        def _():
            acc_ref[...] = jnp.zeros_like(acc_ref)

        acc_ref[...] += op(a_ref[...], b_ref[...])

        @pl.when(kk == nk - 1)
        def _():
            o_ref[...] = acc_ref[...].astype(o_ref.dtype)

    extra = [] if after is None else [after]
    return pl.pallas_call(
        body,
        name=name,
        grid=(m // tm, n // tn, nk),
        in_specs=[a_spec, b_spec] + [pl.BlockSpec(t.shape, lambda i, j, kk: (0, 0)) for t in extra],
        out_specs=pl.BlockSpec((tm, tn), lambda i, j, kk: (i, j)),
        out_shape=jax.ShapeDtypeStruct((m, n), out_dtype),
        scratch_shapes=[pltpu.VMEM((tm, tn), F32)],
        compiler_params=_params(("parallel", "parallel", "arbitrary")),
    )(a, b, *extra)


def _rowwise(fn, rows, consts, out_rows, out_accs, tm, name):
    n_rows = rows[0].shape[0]
    assert n_rows % tm == 0
    nr, nc, no = len(rows), len(consts), len(out_rows)

    def body(*refs):
        r_in, c_in = refs[:nr], refs[nr:nr + nc]
        o_row, o_acc = refs[nr + nc:nr + nc + no], refs[nr + nc + no:]
        outs = fn([r[...] for r in r_in], [c[...] for c in c_in])
        assert len(outs) == len(o_row) + len(o_acc), name
        for ref, v in zip(o_row, outs[:no]):
            ref[...] = v.astype(ref.dtype)
        first = pl.program_id(0) == 0
        for ref, v in zip(o_acc, outs[no:]):
            @pl.when(first)
            def _(ref=ref, v=v):
                ref[...] = v.astype(F32)

            @pl.when(jnp.logical_not(first))
            def _(ref=ref, v=v):
                ref[...] += v.astype(F32)

    in_specs = [pl.BlockSpec((tm, r.shape[1]), lambda i: (i, 0)) for r in rows]
    in_specs += [pl.BlockSpec(c.shape, lambda i: (0, 0)) for c in consts]
    out_specs = [pl.BlockSpec((tm, w), lambda i: (i, 0)) for w, _ in out_rows]
    out_specs += [pl.BlockSpec(s, lambda i: (0, 0)) for s in out_accs]
    out_shape = [jax.ShapeDtypeStruct((n_rows, w), dt) for w, dt in out_rows]
    out_shape += [jax.ShapeDtypeStruct(s, F32) for s in out_accs]
    return pl.pallas_call(
        body,
        name=name,
        grid=(n_rows // tm,),
        in_specs=in_specs,
        out_specs=out_specs,
        out_shape=out_shape,
        compiler_params=_params(("arbitrary",)),
    )(*rows, *consts)


def _rms(x, g):
    return x * lax.rsqrt(jnp.mean(x * x, axis=-1, keepdims=True) + EPS) * g


def _mod_norm(x, g, sc, sh):
    return _rms(x, g) * (1.0 + sc) + sh


def _gelu(x):
    return 0.5 * x * (1.0 + jnp.tanh(math.sqrt(2.0 / math.pi) * (x + 0.044715 * (x * x * x))))


def _sigmoid(x):
    return 1.0 / (1.0 + jnp.exp(-x))


def _post_mix(x, mix, g_post, gt1, g_pre, sc2, sh2):
    x1 = x + gt1 * _rms(mix, g_post)
    return x1, _mod_norm(x1, g_pre, sc2, sh2)


def _att_mix(o0, o1, o2, l0, l1, l2, g):
    m = jnp.maximum(jnp.maximum(l0, l1), l2)
    e0, e1, e2 = jnp.exp(l0 - m), jnp.exp(l1 - m), jnp.exp(l2 - m)
    att = (e0 * o0 + e1 * o1 + e2 * o2) / (e0 + e1 + e2)
    return _rms(att, g)


def _glu_out(y2, z, g):
    return _rms(y2 * _sigmoid(z), g)


def _rope_tables(pos_col, freq_lane, name):
    n_rows = pos_col.shape[0]
    tm = 512

    def body(p_ref, f_ref, cos_ref, lo_ref, hi_ref):
        ang = p_ref[...].astype(F32) * f_ref[...]
        lane = lax.broadcasted_iota(jnp.int32, ang.shape, 1) % HEAD_DIM
        c, s = jnp.cos(ang), jnp.sin(ang)
        cos_ref[...] = jnp.where(lane < ROT_DIM, c, 1.0)
        lo_ref[...] = jnp.where(lane < ROT_DIM // 2, -s, 0.0)
        hi_ref[...] = jnp.where((lane >= ROT_DIM // 2) & (lane < ROT_DIM), s, 0.0)

    tab = jax.ShapeDtypeStruct((n_rows, LANES), F32)
    return pl.pallas_call(
        body,
        name=name,
        grid=(n_rows // tm,),
        in_specs=[pl.BlockSpec((tm, 1), lambda i: (i, 0)), pl.BlockSpec((1, LANES), lambda i: (0, 0))],
        out_specs=[pl.BlockSpec((tm, LANES), lambda i: (i, 0))] * 3,
        out_shape=[tab] * 3,
        compiler_params=_params(("parallel",)),
    )(pos_col, freq_lane)


def _rope(x, cos_t, lo_t, hi_t):
    half = ROT_DIM // 2
    return x * cos_t + pltpu.roll(x, LANES - half, 1) * lo_t + pltpu.roll(x, half, 1) * hi_t


def _rope_transposed(dy, cos_t, lo_t, hi_t):
    half = ROT_DIM // 2
    return dy * cos_t + pltpu.roll(dy * lo_t, half, 1) + pltpu.roll(dy * hi_t, LANES - half, 1)


def _att_masks(i, k0):
    q_pos = i * ATT_BLK + lax.broadcasted_iota(jnp.int32, (ATT_BLK, 2 * ATT_BLK), 0)
    k_pos = k0 + lax.broadcasted_iota(jnp.int32, (ATT_BLK, 2 * ATT_BLK), 1)
    dist = q_pos - k_pos
    return (dist >= 0) & (dist <= ATT_SPAN)


def _head_lane_masks():
    lane = lax.broadcasted_iota(jnp.int32, (1, LANES), 1)
    return lane < HEAD_DIM, lane >= HEAD_DIM


def _att_specs(gi, n_rows):
    col = lambda at: pl.BlockSpec((n_rows, LANES), lambda hp: (0, at + hp))
    qkv = [col(gi * 3), col(9), col(12)]
    tabs = [pl.BlockSpec((n_rows, LANES), lambda hp: (0, 0), pipeline_mode=pl.Buffered(1))] * 3
    head_in = col(0) if DILATIONS[gi] > 1 else pl.BlockSpec((n_rows, LANES), lambda hp: (0, hp),
                                                            pipeline_mode=pl.Buffered(1))
    return qkv, tabs, head_in, col(0)


def _sub_rows(d, n, r):
    return pl.ds(r, n, stride=d) if d > 1 else pl.ds(0, n)


def _att_load(q_ref, k_ref, v_ref, tabs, sub, qs, ks, vs):
    cos_t, lo_t, hi_t = tabs
    qs[...] = (_rope(q_ref[sub, :], cos_t, lo_t, hi_t) * (1.0 / math.sqrt(HEAD_DIM))).astype(BF16)
    ks[...] = _rope(k_ref[sub, :], cos_t, lo_t, hi_t).astype(BF16)
    vs[...] = v_ref[sub, :].astype(BF16)


def _att_fwd(proj, tabs, gi, name):
    n_rows = proj.shape[0]
    d = DILATIONS[gi]
    n = n_rows // d
    nb = n // ATT_BLK

    def body(q_ref, k_ref, v_ref, cos_ref, lo_ref, hi_ref, o_ref, l_ref, qs, ks, vs, o_s, l_s):
        m0, m1 = _head_lane_masks()

        def step(i, carry):
            k0 = pl.multiple_of(jnp.maximum(i - 1, 0) * ATT_BLK, ATT_BLK)
            q0 = pl.multiple_of(i * ATT_BLK, ATT_BLK)
            q = qs[pl.ds(q0, ATT_BLK), :]
            k = ks[pl.ds(k0, 2 * ATT_BLK), :]
            v = vs[pl.ds(k0, 2 * ATT_BLK), :]
            valid = _att_masks(i, k0)
            outs, lses = [], []
            for hm in (m0, m1):
                s = _nt(jnp.where(hm, q, jnp.zeros_like(q)), k)
                s = jnp.where(valid, s, -1e30)
                mx = jnp.max(s, axis=1, keepdims=True)
                p = jnp.exp(s - mx)
                den = jnp.sum(p, axis=1, keepdims=True)
                outs.append(_nn(p.astype(BF16), v) / den)
                lses.append(mx + jnp.log(den))
            o_s[pl.ds(q0, ATT_BLK), :] = jnp.where(m0, outs[0], outs[1])
            l_s[pl.ds(q0, ATT_BLK), :] = jnp.where(m0, lses[0], lses[1])
            return carry

        for r in range(d):
            sub = _sub_rows(d, n, r)
            _att_load(q_ref, k_ref, v_ref, (cos_ref[sub, :], lo_ref[sub, :], hi_ref[sub, :]), sub, qs, ks, vs)
            lax.fori_loop(0, nb, step, 0, unroll=2)
            o_ref[sub, :] = o_s[...]
            l_ref[sub, :] = l_s[...]

    qkv, tab_specs, _, head_out = _att_specs(gi, n_rows)
    out = jax.ShapeDtypeStruct((n_rows, KV_WIDTH), F32)
    return pl.pallas_call(
        body,
        name=name,
        grid=(3,),
        in_specs=qkv + tab_specs,
        out_specs=[head_out, head_out],
        out_shape=[out, out],
        scratch_shapes=[pltpu.VMEM((n, LANES), BF16)] * 3 + [pltpu.VMEM((n, LANES), F32)] * 2,
        compiler_params=_params(("parallel",)),
    )(proj, proj, proj, *tabs)


def _att_bwd(proj, tabs, o, l, do, dl, gi, name):
    n_rows = proj.shape[0]
    d = DILATIONS[gi]
    n = n_rows // d
    nb = n // ATT_BLK

    def body(q_ref, k_ref, v_ref, cos_ref, lo_ref, hi_ref, o_ref, l_ref, do_ref, dl_ref,
             dq_ref, dk_ref, dv_ref, qs, ks, vs, dq_s, dk_acc, dv_acc, *gathered):
        m0, m1 = _head_lane_masks()
        o_s, l_s, do_s, dl_s = gathered if d > 1 else (o_ref, l_ref, do_ref, dl_ref)

        def step(i, carry):
            k0 = pl.multiple_of(jnp.maximum(i - 1, 0) * ATT_BLK, ATT_BLK)
            q0 = pl.multiple_of(i * ATT_BLK, ATT_BLK)
            rows = pl.ds(q0, ATT_BLK)
            keys = pl.ds(k0, 2 * ATT_BLK)
            q, k, v = qs[rows, :], ks[keys, :], vs[keys, :]
            d_o, lse = do_s[rows, :], l_s[rows, :]
            o_do = o_s[rows, :] * d_o
            d_l = dl_s[rows, :]
            valid = _att_masks(i, k0)
            dq = jnp.zeros((ATT_BLK, LANES), F32)
            dk = jnp.zeros((2 * ATT_BLK, LANES), F32)
            dv = jnp.zeros((2 * ATT_BLK, LANES), F32)
            for hm in (m0, m1):
                qh, kh = jnp.where(hm, q, jnp.zeros_like(q)), jnp.where(hm, k, jnp.zeros_like(k))
                doh = jnp.where(hm, d_o, 0.0).astype(BF16)
                lse_h = jnp.max(jnp.where(hm, lse, -1e30), axis=1, keepdims=True)
                delta = jnp.sum(jnp.where(hm, o_do, 0.0), axis=1, keepdims=True)
                dlse = jnp.sum(jnp.where(hm, d_l, 0.0), axis=1, keepdims=True)
                s = jnp.where(valid, _nt(qh, k), -1e30)
                p = jnp.exp(s - lse_h)
                dv = dv + _tn(p.astype(BF16), doh)
                ds = (p * (_nt(doh, v) - delta + dlse)).astype(BF16)
                dq = dq + _nn(ds, kh)
                dk = dk + _tn(ds, qh)
            dq_s[rows, :] = dq * (1.0 / math.sqrt(HEAD_DIM))
            dk_acc[keys, :] += dk
            dv_acc[keys, :] += dv
            return carry

        for r in range(d):
            sub = _sub_rows(d, n, r)
            rot = (cos_ref[sub, :], lo_ref[sub, :], hi_ref[sub, :])
            _att_load(q_ref, k_ref, v_ref, rot, sub, qs, ks, vs)
            if d > 1:
                for dst, src in zip(gathered, (o_ref, l_ref, do_ref, dl_ref)):
                    dst[...] = src[sub, :]
            dk_acc[...] = jnp.zeros_like(dk_acc)
            dv_acc[...] = jnp.zeros_like(dv_acc)
            lax.fori_loop(0, nb, step, 0, unroll=2)
            dq_ref[sub, :] = _rope_transposed(dq_s[...], *rot)
            dk_ref[sub, :] = _rope_transposed(dk_acc[...], *rot)
            dv_ref[sub, :] = dv_acc[...]

    qkv, tab_specs, head_in, head_out = _att_specs(gi, n_rows)
    out = jax.ShapeDtypeStruct((n_rows, KV_WIDTH), F32)
    sub_f32 = pltpu.VMEM((n, LANES), F32)
    return pl.pallas_call(
        body,
        name=name,
        grid=(3,),
        in_specs=qkv + tab_specs + [head_in] * 4,
        out_specs=[head_out] * 3,
        out_shape=[out] * 3,
        scratch_shapes=[pltpu.VMEM((n, LANES), BF16)] * 3 + [sub_f32] * (3 if d == 1 else 7),
        compiler_params=_params(("parallel",)),
    )(proj, proj, proj, *tabs, o, l, do, dl)


def _expand_np():
    e = np.zeros((SSM_N, SSM_N * SSM_P), np.float32)
    for nn in range(SSM_N):
        e[nn, nn * SSM_P:(nn + 1) * SSM_P] = 1.0
    return e


def _ssm_prep_math(a_re, a_im, log_dt, b_re, b_im, expand):
    dt = jnp.exp(log_dt)
    mag = jnp.exp(a_re * dt)
    ab_re, ab_im = mag * jnp.cos(a_im * dt), mag * jnp.sin(a_im * dt)
    den = a_re * a_re + a_im * a_im
    num_re, num_im = ab_re - 1.0, ab_im
    co_re = (num_re * a_re + num_im * a_im) / den
    co_im = (num_im * a_re - num_re * a_im) / den
    hi = lax.Precision.HIGHEST
    co_re_x = jnp.dot(co_re, expand, precision=hi, preferred_element_type=F32)
    co_im_x = jnp.dot(co_im, expand, precision=hi, preferred_element_type=F32)
    bb_re = co_re_x * b_re - co_im_x * b_im
    bb_im = co_re_x * b_im + co_im_x * b_re
    return ab_re, ab_im, bb_re, bb_im


def _ssm_prep(a_re, a_im, log_dt, b_re, b_im, expand, name):
    def body(ar, ai, ld, br, bi, ex, o0, o1, o2, o3):
        outs = _ssm_prep_math(ar[...], ai[...], ld[...], br[...], bi[...], ex[...])
        for ref, v in zip((o0, o1, o2, o3), outs):
            ref[...] = v

    gn = jax.ShapeDtypeStruct((SSM_G, SSM_N), F32)
    gnp = jax.ShapeDtypeStruct((SSM_G, SSM_N * SSM_P), F32)
    return pl.pallas_call(body, name=name, out_shape=[gn, gn, gnp, gnp], compiler_params=_params())(
        a_re, a_im, log_dt, b_re, b_im, expand)


def _ssm_prep_bwd(a_re, a_im, log_dt, b_re, b_im, expand, cts, name):
    def body(ar, ai, ld, br, bi, ex, c0, c1, c2, c3, o0, o1, o2, o3, o4):
        ex_v = ex[...]
        _, vjp = jax.vjp(lambda *p: _ssm_prep_math(*p, ex_v), ar[...], ai[...], ld[...], br[...], bi[...])
        for ref, v in zip((o0, o1, o2, o3, o4), vjp((c0[...], c1[...], c2[...], c3[...]))):
            ref[...] = v

    gn = jax.ShapeDtypeStruct((SSM_G, SSM_N), F32)
    gnp = jax.ShapeDtypeStruct((SSM_G, SSM_N * SSM_P), F32)
    g1 = jax.ShapeDtypeStruct((SSM_G, 1), F32)
    return pl.pallas_call(body, name=name, out_shape=[gn, gn, g1, gnp, gnp], compiler_params=_params())(
        a_re, a_im, log_dt, b_re, b_im, expand, *cts)


def _block_diag_in(bb):
    t = bb.reshape(SSM_TILES, 8, SSM_N, SSM_P).transpose(0, 1, 3, 2)
    eye = jnp.eye(8, dtype=bb.dtype)
    return (t[:, :, :, None, :] * eye[None, :, None, :, None]).reshape(SSM_TILES, LANES, SSM_TILE_GN)


def _block_diag_in_grad(dblk):
    t = dblk.reshape(SSM_TILES, 8, SSM_P, 8, SSM_N)
    t = jnp.einsum("tapbn,ab->tapn", t, jnp.eye(8, dtype=dblk.dtype))
    return t.transpose(0, 1, 3, 2).reshape(SSM_G, SSM_N, SSM_P)


def _block_diag_out(cm):
    t = cm.reshape(SSM_TILES, 8, SSM_P, SSM_N).transpose(0, 1, 3, 2)
    eye = jnp.eye(8, dtype=cm.dtype)
    return (t[:, :, :, None, :] * eye[None, :, None, :, None]).reshape(SSM_TILES, SSM_TILE_GN, LANES)


def _block_diag_out_grad(dblk):
    t = dblk.reshape(SSM_TILES, 8, SSM_N, 8, SSM_P)
    t = jnp.einsum("tanbp,ab->tanp", t, jnp.eye(8, dtype=dblk.dtype))
    return t.transpose(0, 1, 3, 2).reshape(SSM_G, SSM_P, SSM_N)


def _cmul_add(a_re, a_im, s_re, s_im, b_re, b_im):
    return a_re * s_re - a_im * s_im + b_re, a_re * s_im + a_im * s_re + b_im


def _lane_tile_specs(first_tile, index):
    return [pl.BlockSpec((SSM_CHUNK, LANES), lambda c, t=t: (index(c), first_tile + t)) for t in range(SSM_TILES)]


def _ssm_load_rows(src_refs, dst):
    for t in range(SSM_TILES):
        for i in range(SSM_SEG_LEN):
            dst[i * SSM_SEGS:(i + 1) * SSM_SEGS, t * LANES:(t + 1) * LANES] = (
                src_refs[t][pl.ds(i, SSM_SEGS, stride=SSM_SEG_LEN), :])


def _ssm_store_rows(src, dst_refs):
    for t in range(SSM_TILES):
        for i in range(SSM_SEG_LEN):
            dst_refs[t][pl.ds(i, SSM_SEGS, stride=SSM_SEG_LEN), :] = (
                src[i * SSM_SEGS:(i + 1) * SSM_SEGS, t * LANES:(t + 1) * LANES])


def _ssm_powers(ab_re_ref, ab_im_ref, pw_re, pw_im):
    a_re, a_im = ab_re_ref[...], ab_im_ref[...]
    p_re, p_im = a_re, a_im
    for i in range(SSM_SEG_LEN):
        pw_re[i:i + 1, :] = p_re
        pw_im[i:i + 1, :] = p_im
        p_re, p_im = _cmul_add(a_re, a_im, p_re, p_im, 0.0, 0.0)


def _ssm_input_proj(u_s, bblk_re_ref, bblk_im_ref, s_re, s_im):
    for t in range(SSM_TILES):
        ub = u_s[:, t * LANES:(t + 1) * LANES].astype(BF16)
        cols = slice(t * SSM_TILE_GN, (t + 1) * SSM_TILE_GN)
        s_re[:, cols] = _nn(ub, bblk_re_ref[t])
        s_im[:, cols] = _nn(ub, bblk_im_ref[t])


def _ssm_scan(ab_re_ref, ab_im_ref, s_re, s_im, init_re, init_im, conj, reverse):
    sign = -1.0 if conj else 1.0
    for t in range(SSM_TILES):
        cols = slice(t * SSM_TILE_GN, (t + 1) * SSM_TILE_GN)
        a_re = jnp.broadcast_to(ab_re_ref[:, cols], (SSM_SEGS, SSM_TILE_GN))
        a_im = jnp.broadcast_to(ab_im_ref[:, cols], (SSM_SEGS, SSM_TILE_GN)) * sign
        if init_re is None:
            st = (jnp.zeros((SSM_SEGS, SSM_TILE_GN), F32),) * 2
        else:
            st = (init_re[:, cols], init_im[:, cols])

        def step(i, st, cols=cols, a_re=a_re, a_im=a_im):
            idx = (SSM_SEG_LEN - 1 - i) if reverse else i
            rows = pl.ds(pl.multiple_of(idx * SSM_SEGS, SSM_SEGS), SSM_SEGS)
            n_re, n_im = _cmul_add(a_re, a_im, st[0], st[1], s_re[rows, cols], s_im[rows, cols])
            s_re[rows, cols] = n_re
            s_im[rows, cols] = n_im
            return n_re, n_im
        lax.fori_loop(0, SSM_SEG_LEN, step, st, unroll=4)


def _ssm_fixup(pw_re, pw_im, s_re, s_im, cin_re, cin_im, conj, reverse):
    sign = -1.0 if conj else 1.0
    c_re, c_im = cin_re[...], cin_im[...]

    def step(i, c):
        k = (SSM_SEG_LEN - 1 - i) if reverse else i
        rows = pl.ds(pl.multiple_of(i * SSM_SEGS, SSM_SEGS), SSM_SEGS)
        p_re = jnp.broadcast_to(pw_re[pl.ds(k, 1), :], (SSM_SEGS, SSM_GN))
        p_im = jnp.broadcast_to(pw_im[pl.ds(k, 1), :], (SSM_SEGS, SSM_GN)) * sign
        n_re, n_im = _cmul_add(p_re, p_im, c_re, c_im, s_re[rows, :], s_im[rows, :])
        s_re[rows, :] = n_re
        s_im[rows, :] = n_im
        return c
    lax.fori_loop(0, SSM_SEG_LEN, step, 0)


def _ssm_fwd(proj, ab_re, ab_im, bblk_re, bblk_im, cblk_re, cblk_im, d_row, name):
    n_rows = proj.shape[0]
    nchunk = n_rows // SSM_CHUNK
    last = SSM_SEG_LEN - 1
    nt = SSM_TILES

    def body(*refs):
        u_ref, y_ref = refs[:nt], refs[nt + 7:2 * nt + 7]
        ar_ref, ai_ref, br_ref, bi_ref, cr_ref, ci_ref, d_ref = refs[nt:nt + 7]
        cin_re_ref, cin_im_ref, u_s, s_re, s_im, pw_re, pw_im, st_re, st_im = refs[2 * nt + 7:]

        @pl.when(pl.program_id(0) == 0)
        def _():
            _ssm_powers(ar_ref, ai_ref, pw_re, pw_im)
            st_re[...] = jnp.zeros_like(st_re)
            st_im[...] = jnp.zeros_like(st_im)

        _ssm_load_rows(u_ref, u_s)
        _ssm_input_proj(u_s, br_ref, bi_ref, s_re, s_im)
        _ssm_scan(ar_ref, ai_ref, s_re, s_im, None, None, conj=False, reverse=False)
        p_re, p_im = pw_re[last:last + 1, :], pw_im[last:last + 1, :]
        c_re, c_im = st_re[...], st_im[...]
        for j in range(SSM_SEGS):
            cin_re_ref[j:j + 1, :] = c_re
            cin_im_ref[j:j + 1, :] = c_im
            row = last * SSM_SEGS + j
            c_re, c_im = _cmul_add(p_re, p_im, c_re, c_im, s_re[row:row + 1, :], s_im[row:row + 1, :])
        st_re[...] = c_re
        st_im[...] = c_im
        _ssm_fixup(pw_re, pw_im, s_re, s_im, cin_re_ref, cin_im_ref, conj=False, reverse=False)
        for t in range(SSM_TILES):
            cols = slice(t * SSM_TILE_GN, (t + 1) * SSM_TILE_GN)
            lanes = slice(t * LANES, (t + 1) * LANES)
            y = _nn(s_re[:, cols].astype(BF16), cr_ref[t]) - _nn(s_im[:, cols].astype(BF16), ci_ref[t])
            u_s[:, lanes] = y + d_ref[:, lanes] * u_s[:, lanes]
        _ssm_store_rows(u_s, y_ref)

    whole2 = lambda a: pl.BlockSpec(a.shape, lambda c: (0, 0))
    whole3 = lambda a: pl.BlockSpec(a.shape, lambda c: (0, 0, 0))
    seg = pl.BlockSpec((SSM_SEGS, SSM_GN), lambda c: (c, 0))
    seg_shape = jax.ShapeDtypeStruct((nchunk * SSM_SEGS, SSM_GN), F32)
    res = pl.pallas_call(
        body,
        name=name,
        grid=(nchunk,),
        in_specs=_lane_tile_specs((IN_WIDTH - SSM_WIDTH) // LANES, lambda c: c) + [
            whole2(ab_re), whole2(ab_im), whole3(bblk_re), whole3(bblk_im), whole3(cblk_re), whole3(cblk_im),
            whole2(d_row)],
        out_specs=[pl.BlockSpec((SSM_CHUNK, LANES), lambda c: (c, 0))] * nt + [seg, seg],
        out_shape=[jax.ShapeDtypeStruct((n_rows, LANES), F32)] * nt + [seg_shape, seg_shape],
        scratch_shapes=[pltpu.VMEM((SSM_CHUNK, SSM_WIDTH), F32), pltpu.VMEM((SSM_CHUNK, SSM_GN), F32),
                        pltpu.VMEM((SSM_CHUNK, SSM_GN), F32), pltpu.VMEM((SSM_SEG_LEN, SSM_GN), F32),
                        pltpu.VMEM((SSM_SEG_LEN, SSM_GN), F32), pltpu.VMEM((1, SSM_GN), F32),
                        pltpu.VMEM((1, SSM_GN), F32)],
        compiler_params=_params(("arbitrary",)),
    )(*[proj] * nt, ab_re, ab_im, bblk_re, bblk_im, cblk_re, cblk_im, d_row)
    return res[:nt], res[nt], res[nt + 1]


def _ssm_bwd(proj, dy, cin_re, cin_im, ab_re, ab_im, bblk_re, bblk_im, cblk_re, cblk_im, d_row, name):
    n_rows = proj.shape[0]
    nchunk = n_rows // SSM_CHUNK
    nt = SSM_TILES

    def body(*refs):
        u_ref, dy_ref, du_ref = refs[:nt], refs[nt:2 * nt], refs[2 * nt + 9:3 * nt + 9]
        cin_re_ref, cin_im_ref, ar_ref, ai_ref, br_ref, bi_ref, cr_ref, ci_ref, d_ref = refs[2 * nt:2 * nt + 9]
        (dar_ref, dai_ref, dbr_ref, dbi_ref, dcr_ref, dci_ref, dd_ref,
         u_s, dy_s, s_re, s_im, q_re, q_im, pw_re, pw_im, qst_re, qst_im, qin_re, qin_im) = refs[3 * nt + 9:]

        @pl.when(pl.program_id(0) == 0)
        def _():
            _ssm_powers(ar_ref, ai_ref, pw_re, pw_im)
            qst_re[...] = jnp.zeros_like(qst_re)
            qst_im[...] = jnp.zeros_like(qst_im)
            for ref in (dar_ref, dai_ref, dbr_ref, dbi_ref, dcr_ref, dci_ref, dd_ref):
                ref[...] = jnp.zeros_like(ref)

        _ssm_load_rows(u_ref, u_s)
        _ssm_load_rows(dy_ref, dy_s)
        _ssm_input_proj(u_s, br_ref, bi_ref, s_re, s_im)
        _ssm_scan(ar_ref, ai_ref, s_re, s_im, cin_re_ref, cin_im_ref, conj=False, reverse=False)
        for t in range(SSM_TILES):
            cols = slice(t * SSM_TILE_GN, (t + 1) * SSM_TILE_GN)
            dyb = dy_s[:, t * LANES:(t + 1) * LANES].astype(BF16)
            q_re[:, cols] = _nt(dyb, cr_ref[t])
            q_im[:, cols] = -_nt(dyb, ci_ref[t])
            dcr_ref[t] += _tn(s_re[:, cols].astype(BF16), dyb)
            dci_ref[t] -= _tn(s_im[:, cols].astype(BF16), dyb)
        _ssm_scan(ar_ref, ai_ref, q_re, q_im, None, None, conj=True, reverse=True)
        last = SSM_SEG_LEN - 1
        p_re, p_im = pw_re[last:last + 1, :], -pw_im[last:last + 1, :]
        c_re, c_im = qst_re[...], qst_im[...]
        for j in reversed(range(SSM_SEGS)):
            qin_re[j:j + 1, :] = c_re
            qin_im[j:j + 1, :] = c_im
            c_re, c_im = _cmul_add(p_re, p_im, c_re, c_im, q_re[j:j + 1, :], q_im[j:j + 1, :])
        qst_re[...] = c_re
        qst_im[...] = c_im
        _ssm_fixup(pw_re, pw_im, q_re, q_im, qin_re, qin_im, conj=True, reverse=True)
        for t in range(SSM_TILES):
            cols = slice(t * SSM_TILE_GN, (t + 1) * SSM_TILE_GN)

            def step(i, acc, cols=cols):
                rows = pl.ds(pl.multiple_of(i * SSM_SEGS, SSM_SEGS), SSM_SEGS)
                prev = pl.ds(pl.multiple_of((i - 1) * SSM_SEGS, SSM_SEGS), SSM_SEGS)
                qr, qi = q_re[rows, cols], q_im[rows, cols]
                sr, si = s_re[prev, cols], s_im[prev, cols]
                return acc[0] + qr * sr + qi * si, acc[1] + qi * sr - qr * si

            qr, qi = q_re[0:SSM_SEGS, cols], q_im[0:SSM_SEGS, cols]
            sr, si = cin_re_ref[:, cols], cin_im_ref[:, cols]
            acc = lax.fori_loop(1, SSM_SEG_LEN, step, (qr * sr + qi * si, qi * sr - qr * si))
            dar_ref[:, cols] += jnp.sum(acc[0], axis=0, keepdims=True)
            dai_ref[:, cols] += jnp.sum(acc[1], axis=0, keepdims=True)
        for t in range(SSM_TILES):
            cols = slice(t * SSM_TILE_GN, (t + 1) * SSM_TILE_GN)
            lanes = slice(t * LANES, (t + 1) * LANES)
            qrb, qib = q_re[:, cols].astype(BF16), q_im[:, cols].astype(BF16)
            u_t, dy_t = u_s[:, lanes], dy_s[:, lanes]
            ub = u_t.astype(BF16)
            dbr_ref[t] += _tn(ub, qrb)
            dbi_ref[t] += _tn(ub, qib)
            dd_ref[:, lanes] += jnp.sum(dy_t * u_t, axis=0, keepdims=True)
            u_s[:, lanes] = _nt(qrb, br_ref[t]) + _nt(qib, bi_ref[t]) + dy_t * d_ref[:, lanes]
        _ssm_store_rows(u_s, du_ref)

    whole2 = lambda a: pl.BlockSpec(a.shape, lambda c: (0, 0))
    whole3 = lambda a: pl.BlockSpec(a.shape, lambda c: (0, 0, 0))
    back = lambda c: nchunk - 1 - c
    seg = pl.BlockSpec((SSM_SEGS, SSM_GN), lambda c: (back(c), 0))
    gn_row = jax.ShapeDtypeStruct((1, SSM_GN), F32)
    b_shape = jax.ShapeDtypeStruct((SSM_TILES, LANES, SSM_TILE_GN), F32)
    c_shape = jax.ShapeDtypeStruct((SSM_TILES, SSM_TILE_GN, LANES), F32)
    d_shape = jax.ShapeDtypeStruct((1, SSM_WIDTH), F32)
    big = pltpu.VMEM((SSM_CHUNK, SSM_GN), F32)
    res = pl.pallas_call(
        body,
        name=name,
        grid=(nchunk,),
        in_specs=_lane_tile_specs((IN_WIDTH - SSM_WIDTH) // LANES, back) + _lane_tile_specs(0, back) + [
            seg, seg, whole2(ab_re), whole2(ab_im), whole3(bblk_re), whole3(bblk_im), whole3(cblk_re),
            whole3(cblk_im), whole2(d_row)],
        out_specs=[pl.BlockSpec((SSM_CHUNK, LANES), lambda c: (back(c), 0))] * nt + [
            whole2(ab_re), whole2(ab_im), whole3(bblk_re), whole3(bblk_im), whole3(cblk_re), whole3(cblk_im),
            whole2(d_row)],
        out_shape=[jax.ShapeDtypeStruct((n_rows, LANES), F32)] * nt + [gn_row, gn_row, b_shape, b_shape, c_shape,
                                                                       c_shape, d_shape],
        scratch_shapes=[pltpu.VMEM((SSM_CHUNK, SSM_WIDTH), F32), pltpu.VMEM((SSM_CHUNK, SSM_WIDTH), F32),
                        big, big, big, big,
                        pltpu.VMEM((SSM_SEG_LEN, SSM_GN), F32), pltpu.VMEM((SSM_SEG_LEN, SSM_GN), F32),
                        pltpu.VMEM((1, SSM_GN), F32), pltpu.VMEM((1, SSM_GN), F32),
                        pltpu.VMEM((SSM_SEGS, SSM_GN), F32), pltpu.VMEM((SSM_SEGS, SSM_GN), F32)],
        compiler_params=_params(("arbitrary",)),
    )(*[proj] * nt, *[dy] * nt, cin_re, cin_im, ab_re, ab_im, bblk_re, bblk_im, cblk_re, cblk_im, d_row)
    return (res[:nt], *res[nt:])


def _mlp_fwd(h2, w1, w2, tm, tf, name):
    n_rows, dm = h2.shape
    dff = w1.shape[1]

    def body(h_ref, w1_ref, w2_ref, a_ref, y_ref):
        a = _nn(h_ref[...], w1_ref[...])
        a_ref[...] = a.astype(BF16)
        r = jnp.maximum(a, 0.0)
        part = _nn((r * r).astype(BF16), w2_ref[...])
        j = pl.program_id(1)

        @pl.when(j == 0)
        def _():
            y_ref[...] = part

        @pl.when(j > 0)
        def _():
            y_ref[...] += part

    return pl.pallas_call(
        body,
        name=name,
        grid=(n_rows // tm, dff // tf),
        in_specs=[pl.BlockSpec((tm, dm), lambda i, j: (i, 0)), pl.BlockSpec((dm, tf), lambda i, j: (0, j)),
                  pl.BlockSpec((tf, dm), lambda i, j: (j, 0))],
        out_specs=[pl.BlockSpec((tm, tf), lambda i, j: (i, j)), pl.BlockSpec((tm, dm), lambda i, j: (i, 0))],
        out_shape=[jax.ShapeDtypeStruct((n_rows, dff), BF16), jax.ShapeDtypeStruct((n_rows, dm), F32)],
        compiler_params=_params(("parallel", "arbitrary")),
    )(h2, w1, w2)


def _mlp_bwd(dy, h2, a, w2, tm, tf, name):
    n_rows, dm = h2.shape
    dff = a.shape[1]
    per_chip = dff // N_CHIPS // tf

    def body(dy_ref, h_ref, a_ref, w2_ref, da_ref, dw2_ref, dw1_ref):
        dyb = dy_ref[...]
        r = jnp.maximum(a_ref[...].astype(F32), 0.0)
        da = (_nt(dyb, w2_ref[...]) * (2.0 * r)).astype(BF16)
        da_ref[...] = da
        p2 = _tn((r * r).astype(BF16), dyb)
        p1 = _tn(h_ref[...], da)
        i = pl.program_id(1)

        @pl.when(i == 0)
        def _():
            dw2_ref[...] = p2
            dw1_ref[...] = p1

        @pl.when(i > 0)
        def _():
            dw2_ref[...] += p2
            dw1_ref[...] += p1

    return pl.pallas_call(
        body,
        name=name,
        grid=(dff // tf, n_rows // tm),
        in_specs=[pl.BlockSpec((tm, dm), lambda j, i: (i, 0)), pl.BlockSpec((tm, dm), lambda j, i: (i, 0)),
                  pl.BlockSpec((tm, tf), lambda j, i: (i, j)), pl.BlockSpec((tf, dm), lambda j, i: (j, 0))],
        out_specs=[pl.BlockSpec((tm, tf), lambda j, i: (i, j)), pl.BlockSpec((tf, dm), lambda j, i: (j, 0)),
                   pl.BlockSpec((None, dm, tf), lambda j, i: (j // per_chip, 0, j % per_chip))],
        out_shape=[jax.ShapeDtypeStruct((n_rows, dff), BF16), jax.ShapeDtypeStruct((dff, dm), F32),
                   jax.ShapeDtypeStruct((N_CHIPS, dm, dff // N_CHIPS), F32)],
        compiler_params=_params(("parallel", "arbitrary")),
    )(dy, h2, a, w2)


def _local_step(x, pos_col, mod, target, wts, small, hooks=None):
    n_rows = x.shape[0]
    sh1, sc1, gt1, sh2, sc2, gt2 = (mod[:, i * D_MODEL:(i + 1) * D_MODEL] for i in range(N_MOD))
    tm = 256
    d_acc = (1, D_MODEL)

    (h1,) = _rowwise(lambda r, c: [_mod_norm(r[0], *c)], [x], [small["g_pre_mix"], sc1, sh1],
                     [(D_MODEL, BF16)], [], tm, "pre_mix_fwd")
    proj = _matmul(h1, wts["w_in"], "nn", F32, 1024, 1408, 2048, "in_proj")

    freqs = ROPE_THETA ** (-jnp.arange(0, ROT_DIM, 2, dtype=F32) / ROT_DIM)
    freq_lane = jnp.tile(freqs, LANES // (ROT_DIM // 2))[None, :]
    tabs = _rope_tables(pos_col, freq_lane, "rope_tables")
    att = [_att_fwd(proj, tabs, gi, f"att_fwd_{gi}") for gi in range(3)]

    expand = jnp.asarray(_expand_np())
    b_re2, b_im2 = small["ssm_b_re"].reshape(SSM_G, -1), small["ssm_b_im"].reshape(SSM_G, -1)
    log_dt = small["ssm_log_dt"].reshape(SSM_G, 1)
    prep_in = (small["ssm_a_re"], small["ssm_a_im"], log_dt, b_re2, b_im2, expand)
    ab_re, ab_im, bb_re, bb_im = _ssm_prep(*prep_in, "ssm_prep")
    ab_re_row, ab_im_row = ab_re.reshape(1, SSM_GN), ab_im.reshape(1, SSM_GN)
    bblk = [_block_diag_in(t.reshape(SSM_G, SSM_N, SSM_P)).astype(BF16) for t in (bb_re, bb_im)]
    cblk = [_block_diag_out(small[k]).astype(BF16) for k in ("ssm_c_re", "ssm_c_im")]
    d_row = small["ssm_d"].reshape(1, SSM_WIDTH)
    y_tiles, cin_re, cin_im = _ssm_fwd(proj, ab_re_row, ab_im_row, *bblk, *cblk, d_row, "ssm_fwd")
    y_tiles = list(y_tiles)
    n_mix = 6 + SSM_TILES

    def mixers_out(r, c):
        w_glu, b_glu, g_att, g_ssm = c
        att_n = _att_mix(*r[:6], g_att)
        y2 = _gelu(jnp.concatenate(r[6:n_mix], axis=1))
        z = _nn(y2.astype(BF16), w_glu) + b_glu
        return [jnp.concatenate([att_n.astype(BF16), _glu_out(y2, z, g_ssm).astype(BF16)], axis=1)]

    att_rows = [a[0] for a in att] + [a[1] for a in att]
    mix_consts = [wts["w_glu"], small["b_glu"], small["g_attn_out"], small["g_ssm_out"]]
    (cat,) = _rowwise(mixers_out, att_rows + y_tiles, mix_consts, [(OUT_IN_WIDTH, BF16)], [], tm, "mixers_out_fwd")
    mix = _matmul(cat, wts["w_out"], "nn", F32, 1024, 1024, 1280, "out_proj")

    post_consts = [small["g_post_mix"], gt1, small["g_pre_mlp"], sc2, sh2]
    x1, h2 = _rowwise(lambda r, c: list(_post_mix(r[0], r[1], *c)), [x, mix], post_consts,
                      [(D_MODEL, F32), (D_MODEL, BF16)], [], tm, "post_mix_fwd")
    w_mlp_in, w_mlp_out = (wts["w_mlp_in"], wts["w_mlp_out"]) if hooks is None else hooks.mlp_weights(h2)
    a_mlp, y_mlp = _mlp_fwd(h2, w_mlp_in, w_mlp_out, 1024, 512, "mlp_fwd")

    def loss_head(r, c):
        x1_v, y_v, t_v = r
        g, gt = c
        fn = lambda y_, g_, gt_: gt_ * _rms(y_, g_)
        out, vjp = jax.vjp(fn, y_v, g, gt)
        err = x1_v + out - t_v
        dx2 = err * (1.0 / D_MODEL)
        dy, dg, dgt = vjp(dx2)
        loss = 0.5 * jnp.sum(jnp.sum(err * err, axis=1, keepdims=True), axis=0, keepdims=True) * (1.0 / D_MODEL)
        return [dx2, dy, loss, dg, dgt]

    dx2, dy_mlp, loss, dg_post_mlp, dgt2 = _rowwise(
        loss_head, [x1, y_mlp, target], [small["g_post_mlp"], gt2],
        [(D_MODEL, F32), (D_MODEL, BF16)], [(1, 1), d_acc, d_acc], tm, "loss_head")

    da_mlp, dw_mlp_out, dw_mlp_in = _mlp_bwd(dy_mlp, h2, a_mlp, w_mlp_out, 1024, 512, "mlp_bwd")
    dw_mlp_out = dw_mlp_out.reshape(dw_mlp_in.shape)
    sent = None if hooks is None else hooks.mlp_grads_to_sibling(dw_mlp_in, dw_mlp_out)
    dh2 = _matmul(da_mlp, w_mlp_in, "nt", F32, 1024, 1024, 2048, "mlp_in_bwd", after=sent)
    sent = None if hooks is None else hooks.mlp_grads_to_chips(dh2)

    def post_mix_bwd(r, c):
        x_v, mix_v, dx1_v, dh2_v = r
        _, vjp = jax.vjp(_post_mix, x_v, mix_v, *c)
        return list(vjp((dx1_v, dh2_v)))

    post_consts_bwd = post_consts if sent is None else [_tie(post_consts[0], sent)] + post_consts[1:]
    dx_a, dmix, dg_post_mix, dgt1, dg_pre_mlp, dsc2, dsh2 = _rowwise(
        post_mix_bwd, [x, mix, dx2, dh2], post_consts_bwd, [(D_MODEL, F32), (D_MODEL, BF16)], [d_acc] * 5, tm,
        "post_mix_bwd")

    dcat = _matmul(dmix, wts["w_out"], "nt", F32, 1024, 1280, 2048, "out_proj_bwd")
    dw_out = _matmul(cat, dmix, "tn", F32, 1280, 1024, 1024, "out_proj_wgrad")

    def mixers_out_bwd(r, c):
        w_glu, b_glu, g_att, g_ssm = c
        dcat_v = r[n_mix]
        _, vjp_att = jax.vjp(_att_mix, *r[:6], g_att)
        *d_ol, dg_att = vjp_att(dcat_v[:, :KV_WIDTH])
        y2, vjp_gelu = jax.vjp(_gelu, jnp.concatenate(r[6:n_mix], axis=1))
        y2b = y2.astype(BF16)
        z = _nn(y2b, w_glu) + b_glu
        _, vjp_glu = jax.vjp(_glu_out, y2, z, g_ssm)
        dy2, dz, dg_ssm = vjp_glu(dcat_v[:, KV_WIDTH:])
        dzb = dz.astype(BF16)
        (dy,) = vjp_gelu(dy2 + _nt(dzb, w_glu))
        return d_ol + [dy, dg_att, _tn(y2b, dzb), jnp.sum(dz, axis=0, keepdims=True), dg_ssm]

    *d_att, dy_ssm, dg_attn_out, dw_glu, db_glu, dg_ssm_out = _rowwise(
        mixers_out_bwd, att_rows + y_tiles + [dcat], mix_consts,
        [(KV_WIDTH, F32)] * 6 + [(SSM_WIDTH, F32)],
        [(1, KV_WIDTH), (SSM_WIDTH, SSM_WIDTH), (1, SSM_WIDTH), (1, SSM_WIDTH)], tm, "mixers_out_bwd")

    du_tiles, dab_re, dab_im, dbblk_re, dbblk_im, dcblk_re, dcblk_im, dd_row = _ssm_bwd(
        proj, dy_ssm, cin_re, cin_im, ab_re_row, ab_im_row, *bblk, *cblk, d_row, "ssm_bwd")
    prep_cts = (dab_re.reshape(SSM_G, SSM_N), dab_im.reshape(SSM_G, SSM_N),
                _block_diag_in_grad(dbblk_re).reshape(SSM_G, -1), _block_diag_in_grad(dbblk_im).reshape(SSM_G, -1))
    da_re, da_im, dlog_dt, db_re, db_im = _ssm_prep_bwd(*prep_in, prep_cts, "ssm_prep_bwd")

    dqkv = [_att_bwd(proj, tabs, att[gi][0], att[gi][1], d_att[gi], d_att[3 + gi], gi, f"att_bwd_{gi}")
            for gi in range(3)]

    def gather_dproj(r, c):
        dq = [r[0], r[3], r[6]]
        dk = r[1] + r[4] + r[7]
        dv = r[2] + r[5] + r[8]
        return [jnp.concatenate([t.astype(BF16) for t in dq + [dk, dv] + r[9:]], axis=1)]

    (dproj,) = _rowwise(gather_dproj, [t for g in dqkv for t in g] + list(du_tiles), [], [(IN_WIDTH, BF16)], [], tm,
                        "gather_dproj")
    dh1 = _matmul(dproj, wts["w_in"], "nt", F32, 1024, 1024, 2816, "in_proj_bwd")
    dw_in = _matmul(h1, dproj, "tn", F32, 1024, 1408, 1024, "in_proj_wgrad")

    def pre_mix_bwd(r, c):
        x_v, dh1_v, dxa_v = r
        _, vjp = jax.vjp(_mod_norm, x_v, *c)
        dx, dg, dsc, dsh = vjp(dh1_v)
        return [dx + dxa_v, dg, dsc, dsh]

    grad_x, dg_pre_mix, dsc1, dsh1 = _rowwise(
        pre_mix_bwd, [x, dh1, dx_a], [small["g_pre_mix"], sc1, sh1], [(D_MODEL, F32)], [d_acc] * 3, tm, "pre_mix_bwd")

    dmod = jnp.concatenate([dsh1, dsc1, dgt1, dsh2, dsc2, dgt2], axis=1)
    big = dict(w_in=dw_in, w_out=dw_out, w_mlp_in=dw_mlp_in, w_mlp_out=dw_mlp_out, w_glu=dw_glu)
    small_g = dict(
        g_pre_mix=dg_pre_mix, g_post_mix=dg_post_mix, ssm_a_re=da_re, ssm_a_im=da_im,
        ssm_log_dt=dlog_dt.reshape(1, SSM_G), ssm_b_re=db_re.reshape(SSM_G, SSM_N, SSM_P),
        ssm_b_im=db_im.reshape(SSM_G, SSM_N, SSM_P), ssm_c_re=_block_diag_out_grad(dcblk_re),
        ssm_c_im=_block_diag_out_grad(dcblk_im), ssm_d=dd_row.reshape(SSM_G, SSM_P), b_glu=db_glu,
        g_attn_out=dg_attn_out, g_ssm_out=dg_ssm_out, g_pre_mlp=dg_pre_mlp, g_post_mlp=dg_post_mlp)
    return loss, grad_x, dmod, big, small_g


MESH_ID = pl.DeviceIdType.MESH
N_DEV = 8
N_CHIPS = 4
HBM_SPEC = pl.BlockSpec(memory_space=pltpu.HBM)


def _place():
    x, y, c = lax.axis_index("x"), lax.axis_index("y"), lax.axis_index("c")
    other_chips = [(1 - x, y), (x, 1 - y), (1 - x, 1 - y)]
    return x, y, c, other_chips


def _half_rows(index, half):
    return pl.ds(pl.multiple_of(index * half, ROW_PAD), half)


def _remote(src, dst, send_sem, recv_sem, dev):
    return pltpu.make_async_remote_copy(src_ref=src, dst_ref=dst, send_sem=send_sem, recv_sem=recv_sem,
                                        device_id=dev, device_id_type=MESH_ID)


def _all_gather8(block, name):
    m_per, n = block.shape

    def body(x_ref, out_ref, send_sems, recv_sems, local_sem):
        x, y, c, chips = _place()
        me, sibling = (x, y, c), (x, y, 1 - c)

        def rows(px, py, pc):
            return out_ref.at[pl.ds((4 * px + 2 * py + pc) * m_per, m_per), :]

        def copy(k, blk, to, src=None):
            return _remote(rows(*blk) if src is None else src, rows(*blk), send_sems.at[k], recv_sems.at[k], to)

        mine = pltpu.make_async_copy(x_ref, rows(*me), local_sem)
        mine.start()
        first = [copy(0, me, sibling, src=x_ref)]
        first += [copy(1 + j, me, (*chip, c), src=x_ref) for j, chip in enumerate(chips)]
        for cp in first:
            cp.start()
        passed = [copy(4 + j, (*chip, c), sibling) for j, chip in enumerate(chips)]
        for j, chip in enumerate(chips):
            copy(1 + j, (*chip, c), me).wait_recv()
            passed[j].start()
        copy(0, sibling, me).wait_recv()
        for j, chip in enumerate(chips):
            copy(4 + j, (*chip, 1 - c), me).wait_recv()
        for cp in first + passed:
            cp.wait_send()
        mine.wait()

    return pl.pallas_call(
        body,
        name=name,
        out_shape=jax.ShapeDtypeStruct((N_DEV * m_per, n), block.dtype),
        in_specs=[pl.BlockSpec(memory_space=pltpu.VMEM)],
        out_specs=pl.BlockSpec(memory_space=pltpu.VMEM),
        scratch_shapes=[pltpu.SemaphoreType.DMA((7,)), pltpu.SemaphoreType.DMA((7,)), pltpu.SemaphoreType.DMA],
        compiler_params=_params(),
    )(block)


def _weight_gather(shards, name):
    n = len(shards)
    shapes = [s.shape for s in shards]

    def body(*refs):
        ins, outs = refs[:n], refs[n:2 * n]
        send, recv, fsend, frecv = refs[2 * n:]
        x, y, c, chips = _place()
        k_me = 2 * x + y
        sibling = (x, y, 1 - c)
        pending = []
        for a in range(n):
            half = shapes[a][0] // 2
            mine = _half_rows(c, half)
            for j, chip in enumerate(chips):
                cp = _remote(ins[a].at[mine, :], outs[a].at[k_me, mine, :], send.at[a, j], recv.at[a, j], (*chip, c))
                cp.start()
                pending.append(cp.wait_send)
        for a in range(n):
            half = shapes[a][0] // 2
            for j, (px, py) in enumerate(chips):
                piece = outs[a].at[2 * px + py, _half_rows(c, half), :]
                _remote(piece, piece, send.at[a, j], recv.at[a, j], (px, py, c)).wait_recv()
                fw = _remote(piece, piece, fsend.at[a, j], frecv.at[a, j], sibling)
                fw.start()
                pending.append(fw.wait_send)
        for a in range(n):
            half = shapes[a][0] // 2
            for j, (px, py) in enumerate(chips):
                piece = outs[a].at[2 * px + py, _half_rows(1 - c, half), :]
                _remote(piece, piece, fsend.at[a, j], frecv.at[a, j], sibling).wait_recv()
        for wait in pending:
            wait()

    sems = pltpu.SemaphoreType.DMA((n, 3))
    return pl.pallas_call(
        body,
        name=name,
        out_shape=[jax.ShapeDtypeStruct((N_CHIPS,) + s, BF16) for s in shapes],
        in_specs=[HBM_SPEC] * n,
        out_specs=[HBM_SPEC] * n,
        scratch_shapes=[sems, sems, sems, sems],
        compiler_params=_params(),
    )(*shards)


def _sibling_halves(stacks, name):
    n = len(stacks)
    shapes = [s.shape for s in stacks]

    def body(*refs):
        ins, outs = refs[:n], refs[n:2 * n]
        send, recv = refs[2 * n:]
        x, y, c, _ = _place()
        copies = []
        for a in range(n):
            half = shapes[a][1] // 2
            cp = _remote(ins[a].at[:, _half_rows(1 - c, half), :], outs[a], send.at[a], recv.at[a], (x, y, 1 - c))
            cp.start()
            copies.append(cp)
        for cp in copies:
            cp.wait()

    return pl.pallas_call(
        body,
        name=name,
        out_shape=[jax.ShapeDtypeStruct((N_CHIPS, s[1] // 2, s[2]), F32) for s in shapes],
        in_specs=[HBM_SPEC] * n,
        out_specs=[HBM_SPEC] * n,
        scratch_shapes=[pltpu.SemaphoreType.DMA((n,)), pltpu.SemaphoreType.DMA((n,))],
        compiler_params=_params(),
    )(*stacks)


def _chip_exchange(parts, name):
    n = len(parts)
    shapes = [p.shape for p in parts]

    def body(*refs):
        ins, outs = refs[:n], refs[n:2 * n]
        send, recv = refs[2 * n:]
        x, y, c, chips = _place()
        copies = []
        for a in range(n):
            for j, (px, py) in enumerate(chips):
                cp = _remote(ins[a].at[2 * px + py], outs[a].at[j], send.at[a, j], recv.at[a, j], (px, py, c))
                cp.start()
                copies.append(cp)
        for cp in copies:
            cp.wait()

    return pl.pallas_call(
        body,
        name=name,
        out_shape=[jax.ShapeDtypeStruct((3,) + s[1:], BF16) for s in shapes],
        in_specs=[HBM_SPEC] * n,
        out_specs=[HBM_SPEC] * n,
        scratch_shapes=[pltpu.SemaphoreType.DMA((n, 3)), pltpu.SemaphoreType.DMA((n, 3))],
        compiler_params=_params(),
    )(*parts)


def _sibling_swap(halves, name):
    n = len(halves)
    shapes = [h.shape for h in halves]

    def body(*refs):
        ins, outs = refs[:n], refs[n:2 * n]
        send, recv = refs[2 * n:]
        x, y, c, _ = _place()
        pending = []
        for a in range(n):
            half = shapes[a][0]
            mine = outs[a].at[_half_rows(c, half), :]
            cp = _remote(ins[a], mine, send.at[a], recv.at[a], (x, y, 1 - c))
            cp.start()
            pending.append(cp.wait_send)
        for a in range(n):
            half = shapes[a][0]
            theirs = outs[a].at[_half_rows(1 - c, half), :]
            _remote(theirs, theirs, send.at[a], recv.at[a], (x, y, 1 - c)).wait_recv()
        for wait in pending:
            wait()

    return pl.pallas_call(
        body,
        name=name,
        out_shape=[jax.ShapeDtypeStruct((2 * s[0], s[1]), F32) for s in shapes],
        in_specs=[HBM_SPEC] * n,
        out_specs=[HBM_SPEC] * n,
        scratch_shapes=[pltpu.SemaphoreType.DMA((n,)), pltpu.SemaphoreType.DMA((n,))],
        compiler_params=_params(),
    )(*halves)


SEM_SPEC = pl.BlockSpec(memory_space=pltpu.SEMAPHORE)
ANY_SPEC = pl.BlockSpec(memory_space=pl.ANY)
DATAFLOW = pltpu.SideEffectType.DATAFLOW_SIDE_EFFECTING


def _split_copy_start(srcs, lands, plan, n_sems, name, after=None):
    bufs = list(srcs) + list(lands)
    ns, nb = len(srcs), len(bufs)
    extra = [] if after is None else [after]

    def body(*refs):
        outs = refs[nb + len(extra):]
        for outgoing, _ in plan(refs[:ns], refs[ns:nb], outs[0], outs[1]):
            outgoing.start()
        outs[-1][...] = jnp.zeros_like(outs[-1])

    sems = pltpu.SemaphoreType.DMA((n_sems,))
    return pl.pallas_call(
        body,
        name=name,
        out_shape=(sems, sems, *[pltpu.HBM(b.shape, b.dtype) for b in bufs], jax.ShapeDtypeStruct((8, LANES), F32)),
        in_specs=[HBM_SPEC] * nb + [ANY_SPEC] * len(extra),
        out_specs=(SEM_SPEC, SEM_SPEC, *[HBM_SPEC] * nb, pl.BlockSpec(memory_space=pltpu.VMEM)),
        input_output_aliases={i: 2 + i for i in range(nb)},
        compiler_params=pltpu.CompilerParams(has_side_effects=DATAFLOW),
    )(*[pltpu.with_memory_space_constraint(b, pltpu.HBM) for b in bufs], *extra)


def _split_copy_wait(started, plan, after, name):
    send, recv, *bufs = started[:-1]
    nb = len(bufs)
    ns = nb // 2

    def body(*refs):
        for outgoing, incoming in plan(refs[:ns], refs[ns:nb], refs[nb], refs[nb + 1]):
            outgoing.wait_send()
            incoming.wait_recv()

    return pl.pallas_call(
        body,
        name=name,
        out_shape=tuple(pltpu.HBM(b.shape, b.dtype) for b in bufs),
        in_specs=[HBM_SPEC] * nb + [SEM_SPEC, SEM_SPEC, ANY_SPEC],
        out_specs=tuple([HBM_SPEC] * nb),
        input_output_aliases={i: i for i in range(nb)},
        compiler_params=pltpu.CompilerParams(has_side_effects=DATAFLOW),
    )(*bufs, send, recv, after)


def _weight_plan(shapes):
    def plan(srcs, lands, send, recv):
        x, y, c, chips = _place()
        copies = []
        for a in range(len(shapes)):
            mine = _half_rows(c, shapes[a][0] // 2)
            for j, (px, py) in enumerate(chips):
                s = 3 * a + j
                arrival = lands[a].at[2 * px + py, mine, :]
                copies.append((_remote(srcs[a].at[mine, :], lands[a].at[2 * x + y, mine, :], send.at[s], recv.at[s], (px, py, c)),
                               _remote(arrival, arrival, send.at[s], recv.at[s], (px, py, c))))
        return copies
    return plan


def _halves_plan(shapes):
    def plan(srcs, lands, send, recv):
        x, y, c, _ = _place()
        copies = []
        for a in range(len(shapes)):
            theirs = srcs[a].at[:, _half_rows(1 - c, shapes[a][1] // 2), :]
            copies.append((_remote(theirs, lands[a], send.at[a], recv.at[a], (x, y, 1 - c)),
                           _remote(lands[a], lands[a], send.at[a], recv.at[a], (x, y, 1 - c))))
        return copies
    return plan


def _exchange_plan(n):
    def plan(srcs, lands, send, recv):
        x, y, c, chips = _place()
        copies = []
        for a in range(n):
            for j, (px, py) in enumerate(chips):
                s = 3 * a + j
                copies.append((_remote(srcs[a].at[2 * px + py], lands[a].at[j], send.at[s], recv.at[s], (px, py, c)),
                               _remote(lands[a].at[j], lands[a].at[j], send.at[s], recv.at[s], (px, py, c))))
        return copies
    return plan


def _forward_to_sibling(stacks, name):
    n = len(stacks)
    shapes = [s.shape for s in stacks]

    def body(*refs):
        ins, outs = refs[:n], refs[n:2 * n]
        send, recv = refs[2 * n:]
        x, y, c, chips = _place()
        sibling = (x, y, 1 - c)
        copies = []
        for a in range(n):
            half = shapes[a][1] // 2
            for j, (px, py) in enumerate(chips):
                rows = _half_rows(c, half)
                cp = _remote(ins[a].at[2 * px + py, rows, :], outs[a].at[2 * px + py, rows, :], send.at[a, j],
                             recv.at[a, j], sibling)
                cp.start()
                copies.append(cp)
        for a in range(n):
            half = shapes[a][1] // 2
            for j, (px, py) in enumerate(chips):
                theirs = outs[a].at[2 * px + py, _half_rows(1 - c, half), :]
                _remote(theirs, theirs, send.at[a, j], recv.at[a, j], sibling).wait_recv()
        for cp in copies:
            cp.wait_send()

    return pl.pallas_call(
        body,
        name=name,
        out_shape=[jax.ShapeDtypeStruct(s, BF16) for s in shapes],
        in_specs=[HBM_SPEC] * n,
        out_specs=[HBM_SPEC] * n,
        input_output_aliases={a: a for a in range(n)},
        scratch_shapes=[pltpu.SemaphoreType.DMA((n, 3)), pltpu.SemaphoreType.DMA((n, 3))],
        compiler_params=_params(),
    )(*stacks)


def _tie(x, token):
    return x + token[0:1, 0:1].astype(x.dtype)


ROW_PAD = 16


def _silu(x):
    return x * _sigmoid(x)


def _ada_fwd(c_all, w_ada, b_ada, name):
    dm, cols = w_ada.shape
    tn = 512

    def body(c_ref, w_ref, b_ref, o_ref):
        o_ref[...] = _nn(_silu(c_ref[...]).astype(BF16), w_ref[...].astype(BF16)) + b_ref[...]

    return pl.pallas_call(
        body,
        name=name,
        grid=(cols // tn,),
        in_specs=[pl.BlockSpec((ROW_PAD, dm), lambda j: (0, 0)), pl.BlockSpec((dm, tn), lambda j: (0, j)),
                  pl.BlockSpec((1, tn), lambda j: (0, j))],
        out_specs=pl.BlockSpec((ROW_PAD, tn), lambda j: (0, j)),
        out_shape=jax.ShapeDtypeStruct((ROW_PAD, cols), F32),
        compiler_params=_params(("parallel",)),
    )(c_all, w_ada, b_ada)


def _adamw(w, g, m, v):
    m = ADAM_B1 * m + (1.0 - ADAM_B1) * g
    v = ADAM_B2 * v + (1.0 - ADAM_B2) * (g * g)
    m_hat = m / (1.0 - ADAM_B1 ** ADAM_STEP)
    v_hat = v / (1.0 - ADAM_B2 ** ADAM_STEP)
    delta = -ADAM_LR * (m_hat / (jnp.sqrt(v_hat) + ADAM_EPS) + ADAM_WD * w)
    return delta, m, v


def _ada_bwd_adamw(c_all, dmod_cols, w, m, v, name):
    dm, cols = w.shape
    tm, tn = 512, 512

    def body(c_ref, d_ref, w_ref, m_ref, v_ref, g_ref, dl_ref, nm_ref, nv_ref):
        g = _tn(_silu(c_ref[...]).astype(BF16), d_ref[...].astype(BF16))
        g_ref[...] = g
        dl_ref[...], nm_ref[...], nv_ref[...] = _adamw(w_ref[...], g, m_ref[...], v_ref[...])

    tile = pl.BlockSpec((tm, tn), lambda i, j: (i, j))
    shape = jax.ShapeDtypeStruct((dm, cols), F32)
    return pl.pallas_call(
        body,
        name=name,
        grid=(dm // tm, cols // tn),
        in_specs=[pl.BlockSpec((ROW_PAD, tm), lambda i, j: (0, i)), pl.BlockSpec((ROW_PAD, tn), lambda i, j: (0, j)),
                  tile, tile, tile],
        out_specs=[tile] * 4,
        out_shape=[shape] * 4,
        compiler_params=_params(("parallel", "parallel")),
    )(c_all, dmod_cols, w, m, v)


def _sum_blocks(parts, nblk, name):
    rows, cols = parts.shape[0] // nblk, parts.shape[1]

    def body(p_ref, o_ref):
        tot = p_ref[0:rows, :]
        for b in range(1, nblk):
            tot = tot + p_ref[b * rows:(b + 1) * rows, :]
        o_ref[...] = tot

    return pl.pallas_call(body, name=name, out_shape=jax.ShapeDtypeStruct((rows, cols), F32), compiler_params=_params())(parts)


def _adamw_rows(w, g, m, v, tm, name):
    return _rowwise(lambda r, c: list(_adamw(*r)), [w, g, m, v], [], [(w.shape[1], F32)] * 3, [], tm, name)


BIG = ("w_in", "w_out", "w_mlp_in", "w_mlp_out", "w_glu")
COL_SHARDED = ("w_in", "w_out", "w_mlp_in")
SMALL = ("b_ada", "g_pre_mix", "g_post_mix", "ssm_a_re", "ssm_a_im", "ssm_log_dt", "ssm_b_re", "ssm_b_im",
         "ssm_c_re", "ssm_c_im", "ssm_d", "b_glu", "g_attn_out", "g_ssm_out", "g_pre_mlp", "g_post_mlp")
WEIGHTS = ("w_ada", "b_ada", "g_pre_mix", "g_post_mix", "w_in", "ssm_a_re", "ssm_a_im", "ssm_log_dt", "ssm_b_re",
           "ssm_b_im", "ssm_c_re", "ssm_c_im", "ssm_d", "w_glu", "b_glu", "g_attn_out", "g_ssm_out", "w_out",
           "g_pre_mlp", "g_post_mlp", "w_mlp_in", "w_mlp_out")
FLAT_COLS = 1024
FLAT_ROWS = 256
ROW_TILE = {"w_in": 256, "w_out": 128, "w_mlp_in": 256, "w_mlp_out": 256, "w_glu": 112}


def _flatten_small(tree):
    flat = jnp.concatenate([tree[k].reshape(-1) for k in SMALL])
    return jnp.pad(flat, (0, FLAT_ROWS * FLAT_COLS - flat.shape[0])).reshape(FLAT_ROWS, FLAT_COLS)


def _unflatten_small(flat, like):
    flat = flat.reshape(-1)
    out, at = {}, 0
    for k in SMALL:
        size = math.prod(like[k].shape)
        out[k] = flat[at:at + size].reshape(like[k].shape)
        at += size
    return out


def _unstack(stack, name):
    if name in COL_SHARDED:
        return stack.transpose(1, 0, 2).reshape(stack.shape[1], N_CHIPS * stack.shape[2])
    return stack.reshape(N_CHIPS * stack.shape[1], stack.shape[2])


def _stack(full, name):
    if name in COL_SHARDED:
        return full.reshape(full.shape[0], N_CHIPS, full.shape[1] // N_CHIPS).transpose(1, 0, 2)
    return full.reshape(N_CHIPS, full.shape[0] // N_CHIPS, full.shape[1])


EARLY = ("w_in", "w_out", "w_glu")
LATE = ("w_mlp_in", "w_mlp_out")


def _chip_sums(names, g_stacks, from_sibling, ic, chip):
    own, to_send = [], []
    place = jnp.stack([ic, chip]).astype(jnp.int32)
    for k, gs, fs in zip(names, g_stacks, from_sibling):
        _, rows, cols = gs.shape
        half, tm = rows // 2, ROW_TILE[k]
        nt = half // tm

        def body(place_ref, g_ref, f_ref, own_ref, send_ref):
            s = g_ref[...] + f_ref[...]
            send_ref[...] = s.astype(BF16)

            @pl.when(pl.program_id(1) == place_ref[1])
            def _():
                own_ref[...] = s

        slab = lambda index: pl.BlockSpec((None, tm, cols), index)
        mine, to_chips = pl.pallas_call(
            body,
            name="grad_chip_sum_" + k,
            grid_spec=pltpu.PrefetchScalarGridSpec(
                num_scalar_prefetch=1,
                grid=(nt, N_CHIPS),
                in_specs=[slab(lambda i, kk, p, nt=nt: (kk, p[0] * nt + i, 0)), slab(lambda i, kk, p: (kk, i, 0))],
                out_specs=[pl.BlockSpec((tm, cols), lambda i, kk, p: (i, 0)), slab(lambda i, kk, p: (kk, i, 0))]),
            out_shape=[jax.ShapeDtypeStruct((half, cols), F32), jax.ShapeDtypeStruct((N_CHIPS, half, cols), BF16)],
            compiler_params=_params(("arbitrary", "arbitrary")),
        )(place, gs, fs)
        own.append(mine)
        to_send.append(to_chips)
    return own, to_send


def _grad_totals(names, own, from_chips):
    totals = []
    for k, mine, fc in zip(names, own, from_chips):
        half, cols = mine.shape
        tm = ROW_TILE[k]

        def body(m_ref, a_ref, b_ref, c_ref, o_ref):
            o_ref[...] = m_ref[...] + a_ref[...].astype(F32) + b_ref[...].astype(F32) + c_ref[...].astype(F32)

        rows = pl.BlockSpec((tm, cols), lambda i: (i, 0))
        totals.append(pl.pallas_call(
            body,
            name="grad_total_" + k,
            grid=(half // tm,),
            in_specs=[rows] + [pl.BlockSpec((None, tm, cols), lambda i, j=j: (j, i, 0)) for j in range(3)],
            out_specs=rows,
            out_shape=jax.ShapeDtypeStruct((half, cols), F32),
            compiler_params=_params(("parallel",)),
        )(mine, fc, fc, fc))
    return totals


class _Overlap:
    def __init__(self, own_shards, ic, chip, after):
        self.ic, self.chip = ic, chip
        self.shapes = [o.shape for o in own_shards]
        lands = [lax.empty((N_CHIPS,) + s, BF16) for s in self.shapes]
        self.gather = _split_copy_start(own_shards, lands, _weight_plan(self.shapes), 3 * len(LATE),
                                        "mlp_weight_gather_start", after=after)
        self.token = self.gather[-1]

    def mlp_weights(self, after):
        n = len(LATE)
        done = _split_copy_wait(self.gather, _weight_plan(self.shapes), after, "mlp_weight_gather_wait")
        own, stacks = done[:n], done[n:]
        stacks = _forward_to_sibling(stacks, "mlp_weight_forward")
        stacks = [lax.dynamic_update_index_in_dim(s, o, self.chip, 0) for s, o in zip(stacks, own)]
        return [_unstack(s, k) for k, s in zip(LATE, stacks)]

    def mlp_grads_to_sibling(self, dw_in, dw_out):
        stacks = [dw_in, dw_out]
        self.g_shapes = [s.shape for s in stacks]
        lands = [lax.empty((N_CHIPS, s[1] // 2, s[2]), F32) for s in self.g_shapes]
        self.halves = _split_copy_start(stacks, lands, _halves_plan(self.g_shapes), len(LATE), "mlp_grad_halves_start")
        return self.halves[-1]

    def mlp_grads_to_chips(self, after):
        n = len(LATE)
        done = _split_copy_wait(self.halves, _halves_plan(self.g_shapes), after, "mlp_grad_halves_wait")
        self.own, to_send = _chip_sums(LATE, done[:n], done[n:], self.ic, self.chip)
        lands = [lax.empty((3,) + s.shape[1:], BF16) for s in to_send]
        self.exchange = _split_copy_start(to_send, lands, _exchange_plan(n), 3 * n, "mlp_grad_exchange_start")
        return self.exchange[-1]

    def mlp_grads_reduced(self, after):
        n = len(LATE)
        done = _split_copy_wait(self.exchange, _exchange_plan(n), after, "mlp_grad_exchange_wait")
        return _grad_totals(LATE, self.own, done[n:])


def _pad_rows(row):
    return jnp.pad(row, ((0, 8 - row.shape[0]), (0, 0)))


def _every_eighth(gathered):
    rows = gathered.reshape(N_DEV, 8, gathered.shape[1])[:, 0, :]
    return jnp.pad(rows, ((0, ROW_PAD - N_DEV), (0, 0)))


def kernel(x, c, positions, w_ada, b_ada, g_pre_mix, g_post_mix, w_in, ssm_a_re, ssm_a_im, ssm_log_dt, ssm_b_re, ssm_b_im, ssm_c_re, ssm_c_im, ssm_d, w_glu, b_glu, g_attn_out, g_ssm_out, w_out, g_pre_mlp, g_post_mlp, w_mlp_in, w_mlp_out, loss_target, m_w_ada, m_b_ada, m_g_pre_mix, m_g_post_mix, m_w_in, m_ssm_a_re, m_ssm_a_im, m_ssm_log_dt, m_ssm_b_re, m_ssm_b_im, m_ssm_c_re, m_ssm_c_im, m_ssm_d, m_w_glu, m_b_glu, m_g_attn_out, m_g_ssm_out, m_w_out, m_g_pre_mlp, m_g_post_mlp, m_w_mlp_in, m_w_mlp_out, v_w_ada, v_b_ada, v_g_pre_mix, v_g_post_mix, v_w_in, v_ssm_a_re, v_ssm_a_im, v_ssm_log_dt, v_ssm_b_re, v_ssm_b_im, v_ssm_c_re, v_ssm_c_im, v_ssm_d, v_w_glu, v_b_glu, v_g_attn_out, v_g_ssm_out, v_w_out, v_g_pre_mlp, v_g_post_mlp, v_w_mlp_in, v_w_mlp_out):
    given = dict(locals())
    w = {k: given[k][0] for k in WEIGHTS}
    mom = {k: given["m_" + k][0] for k in WEIGHTS}
    var = {k: given["v_" + k][0] for k in WEIGHTS}
    for tree in (w, mom, var):
        for k in ("b_ada", "g_pre_mix", "g_post_mix", "ssm_log_dt", "b_glu", "g_attn_out", "g_ssm_out", "g_pre_mlp",
                  "g_post_mlp"):
            tree[k] = tree[k].reshape(1, -1)
    ix, iy, ic = lax.axis_index("x"), lax.axis_index("y"), lax.axis_index("c")
    chip = 2 * ix + iy
    me = 4 * ix + 2 * iy + ic
    shard_cols = w["w_ada"].shape[1]

    c_all = _every_eighth(_all_gather8(_pad_rows(c), "gather_c"))
    b_ada_cols = lax.dynamic_slice_in_dim(w["b_ada"], chip * shard_cols, shard_cols, axis=1)
    mod_cols = _ada_fwd(c_all, w["w_ada"], b_ada_cols, "ada_fwd")[:N_DEV]
    mod_all = _all_gather8(mod_cols, "gather_mod").reshape(N_CHIPS, 2, N_DEV, shard_cols)[:, 0]
    mod = lax.dynamic_index_in_dim(mod_all, me, axis=1, keepdims=False).reshape(1, N_MOD * D_MODEL)

    early_own = [w[k].astype(BF16) for k in EARLY]
    stacks = _weight_gather(early_own, "weight_gather")
    stacks = [lax.dynamic_update_index_in_dim(s, o, chip, 0) for s, o in zip(stacks, early_own)]
    wts = {k: _unstack(s, k) for k, s in zip(EARLY, stacks)}
    overlap = _Overlap([w[k].astype(BF16) for k in LATE], ic, chip, after=stacks[0])
    mod = _tie(mod, overlap.token)

    small = {k: w[k] for k in SMALL if k != "b_ada"}
    loss, grad_x, dmod, big_g, small_g = _local_step(x[0], positions.reshape(-1, 1), mod, loss_target[0], wts, small,
                                                     hooks=overlap)
    loss = lax.psum(loss[0, 0], ("x", "y", "c"))

    small_g["b_ada"] = dmod
    parts = _all_gather8(_flatten_small(small_g), "gather_small_grads")

    g_stacks = [_stack(big_g[k], k) for k in EARLY]
    from_sibling = _sibling_halves(g_stacks, "grad_sibling_halves")
    chip_f32, chip_bf16 = _chip_sums(EARLY, g_stacks, from_sibling, ic, chip)
    exchange_plan = _exchange_plan(len(EARLY))
    lands = [lax.empty((3,) + s.shape[1:], BF16) for s in chip_bf16]
    exchange = _split_copy_start(chip_bf16, lands, exchange_plan, 3 * len(EARLY), "grad_exchange_start", after=parts)

    small_flat = _sum_blocks(parts, N_DEV, "small_grad_sum")
    grads = _unflatten_small(small_flat, w)

    mod_rows = N_MOD * D_MODEL // FLAT_COLS
    dmod_all = parts.reshape(N_DEV, FLAT_ROWS, FLAT_COLS)[:, :mod_rows].reshape(N_DEV, N_MOD * D_MODEL)
    dmod_all = jnp.pad(dmod_all, ((0, ROW_PAD - N_DEV), (0, 0)))
    dmod_cols = lax.dynamic_slice_in_dim(dmod_all, chip * shard_cols, shard_cols, axis=1)
    g_ada, d_ada, m_ada, v_ada = _ada_bwd_adamw(_tie(c_all, exchange[-1]), dmod_cols, w["w_ada"], mom["w_ada"],
                                                var["w_ada"], "ada_bwd_adamw")
    grads["w_ada"] = g_ada
    delta, new_m, new_v = {"w_ada": d_ada}, {"w_ada": m_ada}, {"w_ada": v_ada}

    def finish(names, reduced, tag):
        swapped = _sibling_swap(reduced, tag)
        for k, s, r in zip(names, swapped, reduced):
            grads[k] = lax.dynamic_update_slice_in_dim(s, r, ic * r.shape[0], axis=0)
            delta[k], new_m[k], new_v[k] = _adamw_rows(w[k], grads[k], mom[k], var[k], ROW_TILE[k], "adamw_" + k)

    finish(LATE, overlap.mlp_grads_reduced(g_ada), "mlp_grad_sibling_swap")
    from_chips = _split_copy_wait(exchange, exchange_plan, new_v[LATE[-1]], "grad_exchange_wait")[len(EARLY):]
    finish(EARLY, _grad_totals(EARLY, chip_f32, from_chips), "grad_sibling_swap")

    flat_upd = _adamw_rows(_flatten_small(w), small_flat, _flatten_small(mom), _flatten_small(var), FLAT_ROWS,
                           "adamw_small")
    for tree, flat in zip((delta, new_m, new_v), flat_upd):
        tree.update(_unflatten_small(flat, w))

    shaped = lambda tree: [tree[k].reshape(given[k].shape) for k in WEIGHTS]
    return (loss, grad_x[None], *shaped(grads), *shaped(delta), *shaped(new_m), *shaped(new_v))
```

```python
import functools
import math

import jax
import jax.numpy as jnp
import numpy as np
from jax import lax
from jax.experimental import pallas as pl
from jax.experimental.pallas import tpu as pltpu

F32 = jnp.float32
BF16 = jnp.bfloat16

D_MODEL = 2048
HEAD_DIM = 64
DILATIONS = (1, 4, 16)
ATT_SPAN = 128
ATT_BLK = 128
HEADS_PER_GROUP = 6
KV_WIDTH = HEADS_PER_GROUP * HEAD_DIM
ATT_Q_WIDTH = 3 * KV_WIDTH
ROT_DIM = 16
ROPE_THETA = 500000.0
SSM_WIDTH = 896
SSM_P = 16
SSM_G = 56
SSM_N = 64
SSM_GN = SSM_G * SSM_N
SSM_TILES = SSM_WIDTH // 128
SSM_TILE_GN = 8 * SSM_N
IN_WIDTH = 2816
OUT_IN_WIDTH = 1280
D_FF = 8192
N_MOD = 6
EPS = 1e-6
LANES = 128
SSM_SEGS = 8
SSM_CHUNK = 256
SSM_SEG_LEN = SSM_CHUNK // SSM_SEGS

ADAM_LR = 0.001
ADAM_B1 = 0.9
ADAM_B2 = 0.999
ADAM_EPS = 1e-08
ADAM_WD = 0.01
ADAM_STEP = 10

VMEM_LIMIT = 56 * 1024 * 1024


def _params(sem=None):
    return pltpu.CompilerParams(dimension_semantics=sem, vmem_limit_bytes=VMEM_LIMIT)


def _dot(a, b, dims):
    return lax.dot_general(a, b, (dims, ((), ())), preferred_element_type=F32)


def _nn(a, b):
    return _dot(a, b, ((1,), (0,)))


def _nt(a, b):
    return _dot(a, b, ((1,), (1,)))


def _tn(a, b):
    return _dot(a, b, ((0,), (0,)))


def _matmul(a, b, mode, out_dtype, tm, tn, tk, name, after=None):
    if mode == "nn":
        (m, k), (_, n) = a.shape, b.shape
        a_spec = pl.BlockSpec((tm, tk), lambda i, j, kk: (i, kk))
        b_spec = pl.BlockSpec((tk, tn), lambda i, j, kk: (kk, j))
        op = _nn
    elif mode == "nt":
        (m, k), (n, _) = a.shape, b.shape
        a_spec = pl.BlockSpec((tm, tk), lambda i, j, kk: (i, kk))
        b_spec = pl.BlockSpec((tn, tk), lambda i, j, kk: (j, kk))
        op = _nt
    else:
        (k, m), (_, n) = a.shape, b.shape
        a_spec = pl.BlockSpec((tk, tm), lambda i, j, kk: (kk, i))
        b_spec = pl.BlockSpec((tk, tn), lambda i, j, kk: (kk, j))
        op = _tn
    assert m % tm == 0 and n % tn == 0 and k % tk == 0, (name, m, n, k)
    nk = k // tk

    def body(a_ref, b_ref, *rest):
        o_ref, acc_ref = rest[-2:]
        kk = pl.program_id(2)

        @pl.when(kk == 0)
        def _():
            acc_ref[...] = jnp.zeros_like(acc_ref)

        acc_ref[...] += op(a_ref[...], b_ref[...])

        @pl.when(kk == nk - 1)
        def _():
            o_ref[...] = acc_ref[...].astype(o_ref.dtype)

    extra = [] if after is None else [after]
    return pl.pallas_call(
        body,
        name=name,
        grid=(m // tm, n // tn, nk),
        in_specs=[a_spec, b_spec] + [pl.BlockSpec(t.shape, lambda i, j, kk: (0, 0)) for t in extra],
        out_specs=pl.BlockSpec((tm, tn), lambda i, j, kk: (i, j)),
        out_shape=jax.ShapeDtypeStruct((m, n), out_dtype),
        scratch_shapes=[pltpu.VMEM((tm, tn), F32)],
        compiler_params=_params(("parallel", "parallel", "arbitrary")),
    )(a, b, *extra)


def _rowwise(fn, rows, consts, out_rows, out_accs, tm, name):
    n_rows = rows[0].shape[0]
    assert n_rows % tm == 0
    nr, nc, no = len(rows), len(consts), len(out_rows)

    def body(*refs):
        r_in, c_in = refs[:nr], refs[nr:nr + nc]
        o_row, o_acc = refs[nr + nc:nr + nc + no], refs[nr + nc + no:]
        outs = fn([r[...] for r in r_in], [c[...] for c in c_in])
        assert len(outs) == len(o_row) + len(o_acc), name
        for ref, v in zip(o_row, outs[:no]):
            ref[...] = v.astype(ref.dtype)
        first = pl.program_id(0) == 0
        for ref, v in zip(o_acc, outs[no:]):
            @pl.when(first)
            def _(ref=ref, v=v):
                ref[...] = v.astype(F32)

            @pl.when(jnp.logical_not(first))
            def _(ref=ref, v=v):
                ref[...] += v.astype(F32)

    in_specs = [pl.BlockSpec((tm, r.shape[1]), lambda i: (i, 0)) for r in rows]
    in_specs += [pl.BlockSpec(c.shape, lambda i: (0, 0)) for c in consts]
    out_specs = [pl.BlockSpec((tm, w), lambda i: (i, 0)) for w, _ in out_rows]
    out_specs += [pl.BlockSpec(s, lambda i: (0, 0)) for s in out_accs]
    out_shape = [jax.ShapeDtypeStruct((n_rows, w), dt) for w, dt in out_rows]
    out_shape += [jax.ShapeDtypeStruct(s, F32) for s in out_accs]
    return pl.pallas_call(
        body,
        name=name,
        grid=(n_rows // tm,),
        in_specs=in_specs,
        out_specs=out_specs,
        out_shape=out_shape,
        compiler_params=_params(("arbitrary",)),
    )(*rows, *consts)


def _rms(x, g):
    return x * lax.rsqrt(jnp.mean(x * x, axis=-1, keepdims=True) + EPS) * g


def _mod_norm(x, g, sc, sh):
    return _rms(x, g) * (1.0 + sc) + sh


def _gelu(x):
    return 0.5 * x * (1.0 + jnp.tanh(math.sqrt(2.0 / math.pi) * (x + 0.044715 * (x * x * x))))


def _sigmoid(x):
    return 1.0 / (1.0 + jnp.exp(-x))


def _post_mix(x, mix, g_post, gt1, g_pre, sc2, sh2):
    x1 = x + gt1 * _rms(mix, g_post)
    return x1, _mod_norm(x1, g_pre, sc2, sh2)


def _att_mix(o0, o1, o2, l0, l1, l2, g):
    m = jnp.maximum(jnp.maximum(l0, l1), l2)
    e0, e1, e2 = jnp.exp(l0 - m), jnp.exp(l1 - m), jnp.exp(l2 - m)
    att = (e0 * o0 + e1 * o1 + e2 * o2) / (e0 + e1 + e2)
    return _rms(att, g)


def _glu_out(y2, z, g):
    return _rms(y2 * _sigmoid(z), g)


def _rope_tables(pos_col, freq_lane, name):
    n_rows = pos_col.shape[0]
    tm = 512

    def body(p_ref, f_ref, cos_ref, lo_ref, hi_ref):
        ang = p_ref[...].astype(F32) * f_ref[...]
        lane = lax.broadcasted_iota(jnp.int32, ang.shape, 1) % HEAD_DIM
        c, s = jnp.cos(ang), jnp.sin(ang)
        cos_ref[...] = jnp.where(lane < ROT_DIM, c, 1.0)
        lo_ref[...] = jnp.where(lane < ROT_DIM // 2, -s, 0.0)
        hi_ref[...] = jnp.where((lane >= ROT_DIM // 2) & (lane < ROT_DIM), s, 0.0)

    tab = jax.ShapeDtypeStruct((n_rows, LANES), F32)
    return pl.pallas_call(
        body,
        name=name,
        grid=(n_rows // tm,),
        in_specs=[pl.BlockSpec((tm, 1), lambda i: (i, 0)), pl.BlockSpec((1, LANES), lambda i: (0, 0))],
        out_specs=[pl.BlockSpec((tm, LANES), lambda i: (i, 0))] * 3,
        out_shape=[tab] * 3,
        compiler_params=_params(("parallel",)),
    )(pos_col, freq_lane)


def _rope(x, cos_t, lo_t, hi_t):
    half = ROT_DIM // 2
    return x * cos_t + pltpu.roll(x, LANES - half, 1) * lo_t + pltpu.roll(x, half, 1) * hi_t


def _rope_transposed(dy, cos_t, lo_t, hi_t):
    half = ROT_DIM // 2
    return dy * cos_t + pltpu.roll(dy * lo_t, half, 1) + pltpu.roll(dy * hi_t, LANES - half, 1)


def _att_masks(i, k0):
    q_pos = i * ATT_BLK + lax.broadcasted_iota(jnp.int32, (ATT_BLK, 2 * ATT_BLK), 0)
    k_pos = k0 + lax.broadcasted_iota(jnp.int32, (ATT_BLK, 2 * ATT_BLK), 1)
    dist = q_pos - k_pos
    return (dist >= 0) & (dist <= ATT_SPAN)


def _head_lane_masks():
    lane = lax.broadcasted_iota(jnp.int32, (1, LANES), 1)
    return lane < HEAD_DIM, lane >= HEAD_DIM


def _att_specs(gi, n_rows):
    col = lambda at: pl.BlockSpec((n_rows, LANES), lambda hp: (0, at + hp))
    qkv = [col(gi * 3), col(9), col(12)]
    tabs = [pl.BlockSpec((n_rows, LANES), lambda hp: (0, 0), pipeline_mode=pl.Buffered(1))] * 3
    head_in = col(0) if DILATIONS[gi] > 1 else pl.BlockSpec((n_rows, LANES), lambda hp: (0, hp),
                                                            pipeline_mode=pl.Buffered(1))
    return qkv, tabs, head_in, col(0)


def _sub_rows(d, n, r):
    return pl.ds(r, n, stride=d) if d > 1 else pl.ds(0, n)


def _att_load(q_ref, k_ref, v_ref, tabs, sub, qs, ks, vs):
    cos_t, lo_t, hi_t = tabs
    qs[...] = (_rope(q_ref[sub, :], cos_t, lo_t, hi_t) * (1.0 / math.sqrt(HEAD_DIM))).astype(BF16)
    ks[...] = _rope(k_ref[sub, :], cos_t, lo_t, hi_t).astype(BF16)
    vs[...] = v_ref[sub, :].astype(BF16)


def _att_fwd(proj, tabs, gi, name):
    n_rows = proj.shape[0]
    d = DILATIONS[gi]
    n = n_rows // d
    nb = n // ATT_BLK

    def body(q_ref, k_ref, v_ref, cos_ref, lo_ref, hi_ref, o_ref, l_ref, qs, ks, vs, o_s, l_s):
        m0, m1 = _head_lane_masks()

        def step(i, carry):
            k0 = pl.multiple_of(jnp.maximum(i - 1, 0) * ATT_BLK, ATT_BLK)
            q0 = pl.multiple_of(i * ATT_BLK, ATT_BLK)
            q = qs[pl.ds(q0, ATT_BLK), :]
            k = ks[pl.ds(k0, 2 * ATT_BLK), :]
            v = vs[pl.ds(k0, 2 * ATT_BLK), :]
            valid = _att_masks(i, k0)
            outs, lses = [], []
            for hm in (m0, m1):
                s = _nt(jnp.where(hm, q, jnp.zeros_like(q)), k)
                s = jnp.where(valid, s, -1e30)
                mx = jnp.max(s, axis=1, keepdims=True)
                p = jnp.exp(s - mx)
                den = jnp.sum(p, axis=1, keepdims=True)
                outs.append(_nn(p.astype(BF16), v) / den)
                lses.append(mx + jnp.log(den))
            o_s[pl.ds(q0, ATT_BLK), :] = jnp.where(m0, outs[0], outs[1])
            l_s[pl.ds(q0, ATT_BLK), :] = jnp.where(m0, lses[0], lses[1])
            return carry

        for r in range(d):
            sub = _sub_rows(d, n, r)
            _att_load(q_ref, k_ref, v_ref, (cos_ref[sub, :], lo_ref[sub, :], hi_ref[sub, :]), sub, qs, ks, vs)
            lax.fori_loop(0, nb, step, 0, unroll=2)
            o_ref[sub, :] = o_s[...]
            l_ref[sub, :] = l_s[...]

    qkv, tab_specs, _, head_out = _att_specs(gi, n_rows)
    out = jax.ShapeDtypeStruct((n_rows, KV_WIDTH), F32)
    return pl.pallas_call(
        body,
        name=name,
        grid=(3,),
        in_specs=qkv + tab_specs,
        out_specs=[head_out, head_out],
        out_shape=[out, out],
        scratch_shapes=[pltpu.VMEM((n, LANES), BF16)] * 3 + [pltpu.VMEM((n, LANES), F32)] * 2,
        compiler_params=_params(("parallel",)),
    )(proj, proj, proj, *tabs)


def _att_bwd(proj, tabs, o, l, do, dl, gi, name):
    n_rows = proj.shape[0]
    d = DILATIONS[gi]
    n = n_rows // d
    nb = n // ATT_BLK

    def body(q_ref, k_ref, v_ref, cos_ref, lo_ref, hi_ref, o_ref, l_ref, do_ref, dl_ref,
             dq_ref, dk_ref, dv_ref, qs, ks, vs, dq_s, dk_acc, dv_acc, *gathered):
        m0, m1 = _head_lane_masks()
        o_s, l_s, do_s, dl_s = gathered if d > 1 else (o_ref, l_ref, do_ref, dl_ref)

        def step(i, carry):
            k0 = pl.multiple_of(jnp.maximum(i - 1, 0) * ATT_BLK, ATT_BLK)
            q0 = pl.multiple_of(i * ATT_BLK, ATT_BLK)
            rows = pl.ds(q0, ATT_BLK)
            keys = pl.ds(k0, 2 * ATT_BLK)
            q, k, v = qs[rows, :], ks[keys, :], vs[keys, :]
            d_o, lse = do_s[rows, :], l_s[rows, :]
            o_do = o_s[rows, :] * d_o
            d_l = dl_s[rows, :]
            valid = _att_masks(i, k0)
            dq = jnp.zeros((ATT_BLK, LANES), F32)
            dk = jnp.zeros((2 * ATT_BLK, LANES), F32)
            dv = jnp.zeros((2 * ATT_BLK, LANES), F32)
            for hm in (m0, m1):
                qh, kh = jnp.where(hm, q, jnp.zeros_like(q)), jnp.where(hm, k, jnp.zeros_like(k))
                doh = jnp.where(hm, d_o, 0.0).astype(BF16)
                lse_h = jnp.max(jnp.where(hm, lse, -1e30), axis=1, keepdims=True)
                delta = jnp.sum(jnp.where(hm, o_do, 0.0), axis=1, keepdims=True)
                dlse = jnp.sum(jnp.where(hm, d_l, 0.0), axis=1, keepdims=True)
                s = jnp.where(valid, _nt(qh, k), -1e30)
                p = jnp.exp(s - lse_h)
                dv = dv + _tn(p.astype(BF16), doh)
                ds = (p * (_nt(doh, v) - delta + dlse)).astype(BF16)
                dq = dq + _nn(ds, kh)
                dk = dk + _tn(ds, qh)
            dq_s[rows, :] = dq * (1.0 / math.sqrt(HEAD_DIM))
            dk_acc[keys, :] += dk
            dv_acc[keys, :] += dv
            return carry

        for r in range(d):
            sub = _sub_rows(d, n, r)
            rot = (cos_ref[sub, :], lo_ref[sub, :], hi_ref[sub, :])
            _att_load(q_ref, k_ref, v_ref, rot, sub, qs, ks, vs)
            if d > 1:
                for dst, src in zip(gathered, (o_ref, l_ref, do_ref, dl_ref)):
                    dst[...] = src[sub, :]
            dk_acc[...] = jnp.zeros_like(dk_acc)
            dv_acc[...] = jnp.zeros_like(dv_acc)
            lax.fori_loop(0, nb, step, 0, unroll=2)
            dq_ref[sub, :] = _rope_transposed(dq_s[...], *rot)
            dk_ref[sub, :] = _rope_transposed(dk_acc[...], *rot)
            dv_ref[sub, :] = dv_acc[...]

    qkv, tab_specs, head_in, head_out = _att_specs(gi, n_rows)
    out = jax.ShapeDtypeStruct((n_rows, KV_WIDTH), F32)
    sub_f32 = pltpu.VMEM((n, LANES), F32)
    return pl.pallas_call(
        body,
        name=name,
        grid=(3,),
        in_specs=qkv + tab_specs + [head_in] * 4,
        out_specs=[head_out] * 3,
        out_shape=[out] * 3,
        scratch_shapes=[pltpu.VMEM((n, LANES), BF16)] * 3 + [sub_f32] * (3 if d == 1 else 7),
        compiler_params=_params(("parallel",)),
    )(proj, proj, proj, *tabs, o, l, do, dl)


def _expand_np():
    e = np.zeros((SSM_N, SSM_N * SSM_P), np.float32)
    for nn in range(SSM_N):
        e[nn, nn * SSM_P:(nn + 1) * SSM_P] = 1.0
    return e


def _ssm_prep_math(a_re, a_im, log_dt, b_re, b_im, expand):
    dt = jnp.exp(log_dt)
    mag = jnp.exp(a_re * dt)
    ab_re, ab_im = mag * jnp.cos(a_im * dt), mag * jnp.sin(a_im * dt)
    den = a_re * a_re + a_im * a_im
    num_re, num_im = ab_re - 1.0, ab_im
    co_re = (num_re * a_re + num_im * a_im) / den
    co_im = (num_im * a_re - num_re * a_im) / den
    hi = lax.Precision.HIGHEST
    co_re_x = jnp.dot(co_re, expand, precision=hi, preferred_element_type=F32)
    co_im_x = jnp.dot(co_im, expand, precision=hi, preferred_element_type=F32)
    bb_re = co_re_x * b_re - co_im_x * b_im
    bb_im = co_re_x * b_im + co_im_x * b_re
    return ab_re, ab_im, bb_re, bb_im


def _ssm_prep(a_re, a_im, log_dt, b_re, b_im, expand, name):
    def body(ar, ai, ld, br, bi, ex, o0, o1, o2, o3):
        outs = _ssm_prep_math(ar[...], ai[...], ld[...], br[...], bi[...], ex[...])
        for ref, v in zip((o0, o1, o2, o3), outs):
            ref[...] = v

    gn = jax.ShapeDtypeStruct((SSM_G, SSM_N), F32)
    gnp = jax.ShapeDtypeStruct((SSM_G, SSM_N * SSM_P), F32)
    return pl.pallas_call(body, name=name, out_shape=[gn, gn, gnp, gnp], compiler_params=_params())(
        a_re, a_im, log_dt, b_re, b_im, expand)


def _ssm_prep_bwd(a_re, a_im, log_dt, b_re, b_im, expand, cts, name):
    def body(ar, ai, ld, br, bi, ex, c0, c1, c2, c3, o0, o1, o2, o3, o4):
        ex_v = ex[...]
        _, vjp = jax.vjp(lambda *p: _ssm_prep_math(*p, ex_v), ar[...], ai[...], ld[...], br[...], bi[...])
        for ref, v in zip((o0, o1, o2, o3, o4), vjp((c0[...], c1[...], c2[...], c3[...]))):
            ref[...] = v

    gn = jax.ShapeDtypeStruct((SSM_G, SSM_N), F32)
    gnp = jax.ShapeDtypeStruct((SSM_G, SSM_N * SSM_P), F32)
    g1 = jax.ShapeDtypeStruct((SSM_G, 1), F32)
    return pl.pallas_call(body, name=name, out_shape=[gn, gn, g1, gnp, gnp], compiler_params=_params())(
        a_re, a_im, log_dt, b_re, b_im, expand, *cts)


def _block_diag_in(bb):
    t = bb.reshape(SSM_TILES, 8, SSM_N, SSM_P).transpose(0, 1, 3, 2)
    eye = jnp.eye(8, dtype=bb.dtype)
    return (t[:, :, :, None, :] * eye[None, :, None, :, None]).reshape(SSM_TILES, LANES, SSM_TILE_GN)


def _block_diag_in_grad(dblk):
    t = dblk.reshape(SSM_TILES, 8, SSM_P, 8, SSM_N)
    t = jnp.einsum("tapbn,ab->tapn", t, jnp.eye(8, dtype=dblk.dtype))
    return t.transpose(0, 1, 3, 2).reshape(SSM_G, SSM_N, SSM_P)


def _block_diag_out(cm):
    t = cm.reshape(SSM_TILES, 8, SSM_P, SSM_N).transpose(0, 1, 3, 2)
    eye = jnp.eye(8, dtype=cm.dtype)
    return (t[:, :, :, None, :] * eye[None, :, None, :, None]).reshape(SSM_TILES, SSM_TILE_GN, LANES)


def _block_diag_out_grad(dblk):
    t = dblk.reshape(SSM_TILES, 8, SSM_N, 8, SSM_P)
    t = jnp.einsum("tanbp,ab->tanp", t, jnp.eye(8, dtype=dblk.dtype))
    return t.transpose(0, 1, 3, 2).reshape(SSM_G, SSM_P, SSM_N)


def _cmul_add(a_re, a_im, s_re, s_im, b_re, b_im):
    return a_re * s_re - a_im * s_im + b_re, a_re * s_im + a_im * s_re + b_im


def _lane_tile_specs(first_tile, index):
    return [pl.BlockSpec((SSM_CHUNK, LANES), lambda c, t=t: (index(c), first_tile + t)) for t in range(SSM_TILES)]


def _ssm_load_rows(src_refs, dst):
    for t in range(SSM_TILES):
        for i in range(SSM_SEG_LEN):
            dst[i * SSM_SEGS:(i + 1) * SSM_SEGS, t * LANES:(t + 1) * LANES] = (
                src_refs[t][pl.ds(i, SSM_SEGS, stride=SSM_SEG_LEN), :])


def _ssm_store_rows(src, dst_refs):
    for t in range(SSM_TILES):
        for i in range(SSM_SEG_LEN):
            dst_refs[t][pl.ds(i, SSM_SEGS, stride=SSM_SEG_LEN), :] = (
                src[i * SSM_SEGS:(i + 1) * SSM_SEGS, t * LANES:(t + 1) * LANES])


def _ssm_powers(ab_re_ref, ab_im_ref, pw_re, pw_im):
    a_re, a_im = ab_re_ref[...], ab_im_ref[...]
    p_re, p_im = a_re, a_im
    for i in range(SSM_SEG_LEN):
        pw_re[i:i + 1, :] = p_re
        pw_im[i:i + 1, :] = p_im
        p_re, p_im = _cmul_add(a_re, a_im, p_re, p_im, 0.0, 0.0)


def _ssm_input_proj(u_s, bblk_re_ref, bblk_im_ref, s_re, s_im):
    for t in range(SSM_TILES):
        ub = u_s[:, t * LANES:(t + 1) * LANES].astype(BF16)
        cols = slice(t * SSM_TILE_GN, (t + 1) * SSM_TILE_GN)
        s_re[:, cols] = _nn(ub, bblk_re_ref[t])
        s_im[:, cols] = _nn(ub, bblk_im_ref[t])


def _ssm_scan(ab_re_ref, ab_im_ref, s_re, s_im, init_re, init_im, conj, reverse):
    sign = -1.0 if conj else 1.0
    for t in range(SSM_TILES):
        cols = slice(t * SSM_TILE_GN, (t + 1) * SSM_TILE_GN)
        a_re = jnp.broadcast_to(ab_re_ref[:, cols], (SSM_SEGS, SSM_TILE_GN))
        a_im = jnp.broadcast_to(ab_im_ref[:, cols], (SSM_SEGS, SSM_TILE_GN)) * sign
        if init_re is None:
            st = (jnp.zeros((SSM_SEGS, SSM_TILE_GN), F32),) * 2
        else:
            st = (init_re[:, cols], init_im[:, cols])

        def step(i, st, cols=cols, a_re=a_re, a_im=a_im):
            idx = (SSM_SEG_LEN - 1 - i) if reverse else i
            rows = pl.ds(pl.multiple_of(idx * SSM_SEGS, SSM_SEGS), SSM_SEGS)
            n_re, n_im = _cmul_add(a_re, a_im, st[0], st[1], s_re[rows, cols], s_im[rows, cols])
            s_re[rows, cols] = n_re
            s_im[rows, cols] = n_im
            return n_re, n_im
        lax.fori_loop(0, SSM_SEG_LEN, step, st, unroll=4)


def _ssm_fixup(pw_re, pw_im, s_re, s_im, cin_re, cin_im, conj, reverse):
    sign = -1.0 if conj else 1.0
    c_re, c_im = cin_re[...], cin_im[...]

    def step(i, c):
        k = (SSM_SEG_LEN - 1 - i) if reverse else i
        rows = pl.ds(pl.multiple_of(i * SSM_SEGS, SSM_SEGS), SSM_SEGS)
        p_re = jnp.broadcast_to(pw_re[pl.ds(k, 1), :], (SSM_SEGS, SSM_GN))
        p_im = jnp.broadcast_to(pw_im[pl.ds(k, 1), :], (SSM_SEGS, SSM_GN)) * sign
        n_re, n_im = _cmul_add(p_re, p_im, c_re, c_im, s_re[rows, :], s_im[rows, :])
        s_re[rows, :] = n_re
        s_im[rows, :] = n_im
        return c
    lax.fori_loop(0, SSM_SEG_LEN, step, 0)


def _ssm_fwd(proj, ab_re, ab_im, bblk_re, bblk_im, cblk_re, cblk_im, d_row, name):
    n_rows = proj.shape[0]
    nchunk = n_rows // SSM_CHUNK
    last = SSM_SEG_LEN - 1
    nt = SSM_TILES

    def body(*refs):
        u_ref, y_ref = refs[:nt], refs[nt + 7:2 * nt + 7]
        ar_ref, ai_ref, br_ref, bi_ref, cr_ref, ci_ref, d_ref = refs[nt:nt + 7]
        cin_re_ref, cin_im_ref, u_s, s_re, s_im, pw_re, pw_im, st_re, st_im = refs[2 * nt + 7:]

        @pl.when(pl.program_id(0) == 0)
        def _():
            _ssm_powers(ar_ref, ai_ref, pw_re, pw_im)
            st_re[...] = jnp.zeros_like(st_re)
            st_im[...] = jnp.zeros_like(st_im)

        _ssm_load_rows(u_ref, u_s)
        _ssm_input_proj(u_s, br_ref, bi_ref, s_re, s_im)
        _ssm_scan(ar_ref, ai_ref, s_re, s_im, None, None, conj=False, reverse=False)
        p_re, p_im = pw_re[last:last + 1, :], pw_im[last:last + 1, :]
        c_re, c_im = st_re[...], st_im[...]
        for j in range(SSM_SEGS):
            cin_re_ref[j:j + 1, :] = c_re
            cin_im_ref[j:j + 1, :] = c_im
            row = last * SSM_SEGS + j
            c_re, c_im = _cmul_add(p_re, p_im, c_re, c_im, s_re[row:row + 1, :], s_im[row:row + 1, :])
        st_re[...] = c_re
        st_im[...] = c_im
        _ssm_fixup(pw_re, pw_im, s_re, s_im, cin_re_ref, cin_im_ref, conj=False, reverse=False)
        for t in range(SSM_TILES):
            cols = slice(t * SSM_TILE_GN, (t + 1) * SSM_TILE_GN)
            lanes = slice(t * LANES, (t + 1) * LANES)
            y = _nn(s_re[:, cols].astype(BF16), cr_ref[t]) - _nn(s_im[:, cols].astype(BF16), ci_ref[t])
            u_s[:, lanes] = y + d_ref[:, lanes] * u_s[:, lanes]
        _ssm_store_rows(u_s, y_ref)

    whole2 = lambda a: pl.BlockSpec(a.shape, lambda c: (0, 0))
    whole3 = lambda a: pl.BlockSpec(a.shape, lambda c: (0, 0, 0))
    seg = pl.BlockSpec((SSM_SEGS, SSM_GN), lambda c: (c, 0))
    seg_shape = jax.ShapeDtypeStruct((nchunk * SSM_SEGS, SSM_GN), F32)
    res = pl.pallas_call(
        body,
        name=name,
        grid=(nchunk,),
        in_specs=_lane_tile_specs((IN_WIDTH - SSM_WIDTH) // LANES, lambda c: c) + [
            whole2(ab_re), whole2(ab_im), whole3(bblk_re), whole3(bblk_im), whole3(cblk_re), whole3(cblk_im),
            whole2(d_row)],
        out_specs=[pl.BlockSpec((SSM_CHUNK, LANES), lambda c: (c, 0))] * nt + [seg, seg],
        out_shape=[jax.ShapeDtypeStruct((n_rows, LANES), F32)] * nt + [seg_shape, seg_shape],
        scratch_shapes=[pltpu.VMEM((SSM_CHUNK, SSM_WIDTH), F32), pltpu.VMEM((SSM_CHUNK, SSM_GN), F32),
                        pltpu.VMEM((SSM_CHUNK, SSM_GN), F32), pltpu.VMEM((SSM_SEG_LEN, SSM_GN), F32),
                        pltpu.VMEM((SSM_SEG_LEN, SSM_GN), F32), pltpu.VMEM((1, SSM_GN), F32),
                        pltpu.VMEM((1, SSM_GN), F32)],
        compiler_params=_params(("arbitrary",)),
    )(*[proj] * nt, ab_re, ab_im, bblk_re, bblk_im, cblk_re, cblk_im, d_row)
    return res[:nt], res[nt], res[nt + 1]


def _ssm_bwd(proj, dy, cin_re, cin_im, ab_re, ab_im, bblk_re, bblk_im, cblk_re, cblk_im, d_row, name):
    n_rows = proj.shape[0]
    nchunk = n_rows // SSM_CHUNK
    nt = SSM_TILES

    def body(*refs):
        u_ref, dy_ref, du_ref = refs[:nt], refs[nt:2 * nt], refs[2 * nt + 9:3 * nt + 9]
        cin_re_ref, cin_im_ref, ar_ref, ai_ref, br_ref, bi_ref, cr_ref, ci_ref, d_ref = refs[2 * nt:2 * nt + 9]
        (dar_ref, dai_ref, dbr_ref, dbi_ref, dcr_ref, dci_ref, dd_ref,
         u_s, dy_s, s_re, s_im, q_re, q_im, pw_re, pw_im, qst_re, qst_im, qin_re, qin_im) = refs[3 * nt + 9:]

        @pl.when(pl.program_id(0) == 0)
        def _():
            _ssm_powers(ar_ref, ai_ref, pw_re, pw_im)
            qst_re[...] = jnp.zeros_like(qst_re)
            qst_im[...] = jnp.zeros_like(qst_im)
            for ref in (dar_ref, dai_ref, dbr_ref, dbi_ref, dcr_ref, dci_ref, dd_ref):
                ref[...] = jnp.zeros_like(ref)

        _ssm_load_rows(u_ref, u_s)
        _ssm_load_rows(dy_ref, dy_s)
        _ssm_input_proj(u_s, br_ref, bi_ref, s_re, s_im)
        _ssm_scan(ar_ref, ai_ref, s_re, s_im, cin_re_ref, cin_im_ref, conj=False, reverse=False)
        for t in range(SSM_TILES):
            cols = slice(t * SSM_TILE_GN, (t + 1) * SSM_TILE_GN)
            dyb = dy_s[:, t * LANES:(t + 1) * LANES].astype(BF16)
            q_re[:, cols] = _nt(dyb, cr_ref[t])
            q_im[:, cols] = -_nt(dyb, ci_ref[t])
            dcr_ref[t] += _tn(s_re[:, cols].astype(BF16), dyb)
            dci_ref[t] -= _tn(s_im[:, cols].astype(BF16), dyb)
        _ssm_scan(ar_ref, ai_ref, q_re, q_im, None, None, conj=True, reverse=True)
        last = SSM_SEG_LEN - 1
        p_re, p_im = pw_re[last:last + 1, :], -pw_im[last:last + 1, :]
        c_re, c_im = qst_re[...], qst_im[...]
        for j in reversed(range(SSM_SEGS)):
            qin_re[j:j + 1, :] = c_re
            qin_im[j:j + 1, :] = c_im
            c_re, c_im = _cmul_add(p_re, p_im, c_re, c_im, q_re[j:j + 1, :], q_im[j:j + 1, :])
        qst_re[...] = c_re
        qst_im[...] = c_im
        _ssm_fixup(pw_re, pw_im, q_re, q_im, qin_re, qin_im, conj=True, reverse=True)
        for t in range(SSM_TILES):
            cols = slice(t * SSM_TILE_GN, (t + 1) * SSM_TILE_GN)

            def step(i, acc, cols=cols):
                rows = pl.ds(pl.multiple_of(i * SSM_SEGS, SSM_SEGS), SSM_SEGS)
                prev = pl.ds(pl.multiple_of((i - 1) * SSM_SEGS, SSM_SEGS), SSM_SEGS)
                qr, qi = q_re[rows, cols], q_im[rows, cols]
                sr, si = s_re[prev, cols], s_im[prev, cols]
                return acc[0] + qr * sr + qi * si, acc[1] + qi * sr - qr * si

            qr, qi = q_re[0:SSM_SEGS, cols], q_im[0:SSM_SEGS, cols]
            sr, si = cin_re_ref[:, cols], cin_im_ref[:, cols]
            acc = lax.fori_loop(1, SSM_SEG_LEN, step, (qr * sr + qi * si, qi * sr - qr * si))
            dar_ref[:, cols] += jnp.sum(acc[0], axis=0, keepdims=True)
            dai_ref[:, cols] += jnp.sum(acc[1], axis=0, keepdims=True)
        for t in range(SSM_TILES):
            cols = slice(t * SSM_TILE_GN, (t + 1) * SSM_TILE_GN)
            lanes = slice(t * LANES, (t + 1) * LANES)
            qrb, qib = q_re[:, cols].astype(BF16), q_im[:, cols].astype(BF16)
            u_t, dy_t = u_s[:, lanes], dy_s[:, lanes]
            ub = u_t.astype(BF16)
            dbr_ref[t] += _tn(ub, qrb)
            dbi_ref[t] += _tn(ub, qib)
            dd_ref[:, lanes] += jnp.sum(dy_t * u_t, axis=0, keepdims=True)
            u_s[:, lanes] = _nt(qrb, br_ref[t]) + _nt(qib, bi_ref[t]) + dy_t * d_ref[:, lanes]
        _ssm_store_rows(u_s, du_ref)

    whole2 = lambda a: pl.BlockSpec(a.shape, lambda c: (0, 0))
    whole3 = lambda a: pl.BlockSpec(a.shape, lambda c: (0, 0, 0))
    back = lambda c: nchunk - 1 - c
    seg = pl.BlockSpec((SSM_SEGS, SSM_GN), lambda c: (back(c), 0))
    gn_row = jax.ShapeDtypeStruct((1, SSM_GN), F32)
    b_shape = jax.ShapeDtypeStruct((SSM_TILES, LANES, SSM_TILE_GN), F32)
    c_shape = jax.ShapeDtypeStruct((SSM_TILES, SSM_TILE_GN, LANES), F32)
    d_shape = jax.ShapeDtypeStruct((1, SSM_WIDTH), F32)
    big = pltpu.VMEM((SSM_CHUNK, SSM_GN), F32)
    res = pl.pallas_call(
        body,
        name=name,
        grid=(nchunk,),
        in_specs=_lane_tile_specs((IN_WIDTH - SSM_WIDTH) // LANES, back) + _lane_tile_specs(0, back) + [
            seg, seg, whole2(ab_re), whole2(ab_im), whole3(bblk_re), whole3(bblk_im), whole3(cblk_re),
            whole3(cblk_im), whole2(d_row)],
        out_specs=[pl.BlockSpec((SSM_CHUNK, LANES), lambda c: (back(c), 0))] * nt + [
            whole2(ab_re), whole2(ab_im), whole3(bblk_re), whole3(bblk_im), whole3(cblk_re), whole3(cblk_im),
            whole2(d_row)],
        out_shape=[jax.ShapeDtypeStruct((n_rows, LANES), F32)] * nt + [gn_row, gn_row, b_shape, b_shape, c_shape,
                                                                       c_shape, d_shape],
        scratch_shapes=[pltpu.VMEM((SSM_CHUNK, SSM_WIDTH), F32), pltpu.VMEM((SSM_CHUNK, SSM_WIDTH), F32),
                        big, big, big, big,
                        pltpu.VMEM((SSM_SEG_LEN, SSM_GN), F32), pltpu.VMEM((SSM_SEG_LEN, SSM_GN), F32),
                        pltpu.VMEM((1, SSM_GN), F32), pltpu.VMEM((1, SSM_GN), F32),
                        pltpu.VMEM((SSM_SEGS, SSM_GN), F32), pltpu.VMEM((SSM_SEGS, SSM_GN), F32)],
        compiler_params=_params(("arbitrary",)),
    )(*[proj] * nt, *[dy] * nt, cin_re, cin_im, ab_re, ab_im, bblk_re, bblk_im, cblk_re, cblk_im, d_row)
    return (res[:nt], *res[nt:])


def _mlp_fwd(h2, w1, w2, tm, tf, name):
    n_rows, dm = h2.shape
    dff = w1.shape[1]

    def body(h_ref, w1_ref, w2_ref, a_ref, y_ref):
        a = _nn(h_ref[...], w1_ref[...])
        a_ref[...] = a.astype(BF16)
        r = jnp.maximum(a, 0.0)
        part = _nn((r * r).astype(BF16), w2_ref[...])
        j = pl.program_id(1)

        @pl.when(j == 0)
        def _():
            y_ref[...] = part

        @pl.when(j > 0)
        def _():
            y_ref[...] += part

    return pl.pallas_call(
        body,
        name=name,
        grid=(n_rows // tm, dff // tf),
        in_specs=[pl.BlockSpec((tm, dm), lambda i, j: (i, 0)), pl.BlockSpec((dm, tf), lambda i, j: (0, j)),
                  pl.BlockSpec((tf, dm), lambda i, j: (j, 0))],
        out_specs=[pl.BlockSpec((tm, tf), lambda i, j: (i, j)), pl.BlockSpec((tm, dm), lambda i, j: (i, 0))],
        out_shape=[jax.ShapeDtypeStruct((n_rows, dff), BF16), jax.ShapeDtypeStruct((n_rows, dm), F32)],
        compiler_params=_params(("parallel", "arbitrary")),
    )(h2, w1, w2)


def _mlp_bwd(dy, h2, a, w2, tm, tf, name):
    n_rows, dm = h2.shape
    dff = a.shape[1]
    per_chip = dff // N_CHIPS // tf

    def body(dy_ref, h_ref, a_ref, w2_ref, da_ref, dw2_ref, dw1_ref):
        dyb = dy_ref[...]
        r = jnp.maximum(a_ref[...].astype(F32), 0.0)
        da = (_nt(dyb, w2_ref[...]) * (2.0 * r)).astype(BF16)
        da_ref[...] = da
        p2 = _tn((r * r).astype(BF16), dyb)
        p1 = _tn(h_ref[...], da)
        i = pl.program_id(1)

        @pl.when(i == 0)
        def _():
            dw2_ref[...] = p2
            dw1_ref[...] = p1

        @pl.when(i > 0)
        def _():
            dw2_ref[...] += p2
            dw1_ref[...] += p1

    return pl.pallas_call(
        body,
        name=name,
        grid=(dff // tf, n_rows // tm),
        in_specs=[pl.BlockSpec((tm, dm), lambda j, i: (i, 0)), pl.BlockSpec((tm, dm), lambda j, i: (i, 0)),
                  pl.BlockSpec((tm, tf), lambda j, i: (i, j)), pl.BlockSpec((tf, dm), lambda j, i: (j, 0))],
        out_specs=[pl.BlockSpec((tm, tf), lambda j, i: (i, j)), pl.BlockSpec((tf, dm), lambda j, i: (j, 0)),
                   pl.BlockSpec((None, dm, tf), lambda j, i: (j // per_chip, 0, j % per_chip))],
        out_shape=[jax.ShapeDtypeStruct((n_rows, dff), BF16), jax.ShapeDtypeStruct((dff, dm), F32),
                   jax.ShapeDtypeStruct((N_CHIPS, dm, dff // N_CHIPS), F32)],
        compiler_params=_params(("parallel", "arbitrary")),
    )(dy, h2, a, w2)


def _local_step(x, pos_col, mod, target, wts, small, hooks=None):
    n_rows = x.shape[0]
    sh1, sc1, gt1, sh2, sc2, gt2 = (mod[:, i * D_MODEL:(i + 1) * D_MODEL] for i in range(N_MOD))
    tm = 256
    d_acc = (1, D_MODEL)

    (h1,) = _rowwise(lambda r, c: [_mod_norm(r[0], *c)], [x], [small["g_pre_mix"], sc1, sh1],
                     [(D_MODEL, BF16)], [], tm, "pre_mix_fwd")
    proj = _matmul(h1, wts["w_in"], "nn", F32, 1024, 1408, 2048, "in_proj")

    freqs = ROPE_THETA ** (-jnp.arange(0, ROT_DIM, 2, dtype=F32) / ROT_DIM)
    freq_lane = jnp.tile(freqs, LANES // (ROT_DIM // 2))[None, :]
    tabs = _rope_tables(pos_col, freq_lane, "rope_tables")
    att = [_att_fwd(proj, tabs, gi, f"att_fwd_{gi}") for gi in range(3)]

    expand = jnp.asarray(_expand_np())
    b_re2, b_im2 = small["ssm_b_re"].reshape(SSM_G, -1), small["ssm_b_im"].reshape(SSM_G, -1)
    log_dt = small["ssm_log_dt"].reshape(SSM_G, 1)
    prep_in = (small["ssm_a_re"], small["ssm_a_im"], log_dt, b_re2, b_im2, expand)
    ab_re, ab_im, bb_re, bb_im = _ssm_prep(*prep_in, "ssm_prep")
    ab_re_row, ab_im_row = ab_re.reshape(1, SSM_GN), ab_im.reshape(1, SSM_GN)
    bblk = [_block_diag_in(t.reshape(SSM_G, SSM_N, SSM_P)).astype(BF16) for t in (bb_re, bb_im)]
    cblk = [_block_diag_out(small[k]).astype(BF16) for k in ("ssm_c_re", "ssm_c_im")]
    d_row = small["ssm_d"].reshape(1, SSM_WIDTH)
    if hooks is not None:
        d_row = _tie(d_row, hooks.late_weights_arrived(att[2][0]))
    y_tiles, cin_re, cin_im = _ssm_fwd(proj, ab_re_row, ab_im_row, *bblk, *cblk, d_row, "ssm_fwd")
    y_tiles = list(y_tiles)
    n_mix = 6 + SSM_TILES

    def mixers_out(r, c):
        w_glu, b_glu, g_att, g_ssm = c
        att_n = _att_mix(*r[:6], g_att)
        y2 = _gelu(jnp.concatenate(r[6:n_mix], axis=1))
        z = _nn(y2.astype(BF16), w_glu) + b_glu
        return [jnp.concatenate([att_n.astype(BF16), _glu_out(y2, z, g_ssm).astype(BF16)], axis=1)]

    att_rows = [a[0] for a in att] + [a[1] for a in att]
    if hooks is not None:
        wts = {**wts, **hooks.late_weights(y_tiles[0])}
    mix_consts = [wts["w_glu"], small["b_glu"], small["g_attn_out"], small["g_ssm_out"]]
    (cat,) = _rowwise(mixers_out, att_rows + y_tiles, mix_consts, [(OUT_IN_WIDTH, BF16)], [], tm, "mixers_out_fwd")
    mix = _matmul(cat, wts["w_out"], "nn", F32, 1024, 1024, 1280, "out_proj")

    post_consts = [small["g_post_mix"], gt1, small["g_pre_mlp"], sc2, sh2]
    x1, h2 = _rowwise(lambda r, c: list(_post_mix(r[0], r[1], *c)), [x, mix], post_consts,
                      [(D_MODEL, F32), (D_MODEL, BF16)], [], tm, "post_mix_fwd")
    w_mlp_in, w_mlp_out = wts["w_mlp_in"], wts["w_mlp_out"]
    a_mlp, y_mlp = _mlp_fwd(h2, w_mlp_in, w_mlp_out, 1024, 512, "mlp_fwd")

    def loss_head(r, c):
        x1_v, y_v, t_v = r
        g, gt = c
        fn = lambda y_, g_, gt_: gt_ * _rms(y_, g_)
        out, vjp = jax.vjp(fn, y_v, g, gt)
        err = x1_v + out - t_v
        dx2 = err * (1.0 / D_MODEL)
        dy, dg, dgt = vjp(dx2)
        loss = 0.5 * jnp.sum(jnp.sum(err * err, axis=1, keepdims=True), axis=0, keepdims=True) * (1.0 / D_MODEL)
        return [dx2, dy, loss, dg, dgt]

    dx2, dy_mlp, loss, dg_post_mlp, dgt2 = _rowwise(
        loss_head, [x1, y_mlp, target], [small["g_post_mlp"], gt2],
        [(D_MODEL, F32), (D_MODEL, BF16)], [(1, 1), d_acc, d_acc], tm, "loss_head")

    da_mlp, dw_mlp_out, dw_mlp_in = _mlp_bwd(dy_mlp, h2, a_mlp, w_mlp_out, 1024, 512, "mlp_bwd")
    dw_mlp_out = dw_mlp_out.reshape(dw_mlp_in.shape)
    sent = None if hooks is None else hooks.mlp_grads_to_sibling(dw_mlp_in, dw_mlp_out)
    dh2 = _matmul(da_mlp, w_mlp_in, "nt", F32, 1024, 1024, 2048, "mlp_in_bwd", after=sent)
    sent = None if hooks is None else hooks.mlp_grads_to_chips(dh2)

    def post_mix_bwd(r, c):
        x_v, mix_v, dx1_v, dh2_v = r
        _, vjp = jax.vjp(_post_mix, x_v, mix_v, *c)
        return list(vjp((dx1_v, dh2_v)))

    post_consts_bwd = post_consts if sent is None else [_tie(post_consts[0], sent)] + post_consts[1:]
    dx_a, dmix, dg_post_mix, dgt1, dg_pre_mlp, dsc2, dsh2 = _rowwise(
        post_mix_bwd, [x, mix, dx2, dh2], post_consts_bwd, [(D_MODEL, F32), (D_MODEL, BF16)], [d_acc] * 5, tm,
        "post_mix_bwd")

    dcat = _matmul(dmix, wts["w_out"], "nt", F32, 1024, 1280, 2048, "out_proj_bwd")
    dw_out = _matmul(cat, dmix, "tn", F32, 1280, 1024, 1024, "out_proj_wgrad")

    def mixers_out_bwd(r, c):
        w_glu, b_glu, g_att, g_ssm = c
        dcat_v = r[n_mix]
        _, vjp_att = jax.vjp(_att_mix, *r[:6], g_att)
        *d_ol, dg_att = vjp_att(dcat_v[:, :KV_WIDTH])
        y2, vjp_gelu = jax.vjp(_gelu, jnp.concatenate(r[6:n_mix], axis=1))
        y2b = y2.astype(BF16)
        z = _nn(y2b, w_glu) + b_glu
        _, vjp_glu = jax.vjp(_glu_out, y2, z, g_ssm)
        dy2, dz, dg_ssm = vjp_glu(dcat_v[:, KV_WIDTH:])
        dzb = dz.astype(BF16)
        (dy,) = vjp_gelu(dy2 + _nt(dzb, w_glu))
        return d_ol + [dy, dg_att, _tn(y2b, dzb), jnp.sum(dz, axis=0, keepdims=True), dg_ssm]

    *d_att, dy_ssm, dg_attn_out, dw_glu, db_glu, dg_ssm_out = _rowwise(
        mixers_out_bwd, att_rows + y_tiles + [dcat], mix_consts,
        [(KV_WIDTH, F32)] * 6 + [(SSM_WIDTH, F32)],
        [(1, KV_WIDTH), (SSM_WIDTH, SSM_WIDTH), (1, SSM_WIDTH), (1, SSM_WIDTH)], tm, "mixers_out_bwd")

    du_tiles, dab_re, dab_im, dbblk_re, dbblk_im, dcblk_re, dcblk_im, dd_row = _ssm_bwd(
        proj, dy_ssm, cin_re, cin_im, ab_re_row, ab_im_row, *bblk, *cblk, d_row, "ssm_bwd")
    prep_cts = (dab_re.reshape(SSM_G, SSM_N), dab_im.reshape(SSM_G, SSM_N),
                _block_diag_in_grad(dbblk_re).reshape(SSM_G, -1), _block_diag_in_grad(dbblk_im).reshape(SSM_G, -1))
    da_re, da_im, dlog_dt, db_re, db_im = _ssm_prep_bwd(*prep_in, prep_cts, "ssm_prep_bwd")

    dqkv = [_att_bwd(proj, tabs, att[gi][0], att[gi][1], d_att[gi], d_att[3 + gi], gi, f"att_bwd_{gi}")
            for gi in range(3)]

    def gather_dproj(r, c):
        dq = [r[0], r[3], r[6]]
        dk = r[1] + r[4] + r[7]
        dv = r[2] + r[5] + r[8]
        return [jnp.concatenate([t.astype(BF16) for t in dq + [dk, dv] + r[9:]], axis=1)]

    (dproj,) = _rowwise(gather_dproj, [t for g in dqkv for t in g] + list(du_tiles), [], [(IN_WIDTH, BF16)], [], tm,
                        "gather_dproj")
    dh1 = _matmul(dproj, wts["w_in"], "nt", F32, 1024, 1024, 2816, "in_proj_bwd")
    dw_in = _matmul(h1, dproj, "tn", F32, 1024, 1408, 1024, "in_proj_wgrad")

    def pre_mix_bwd(r, c):
        x_v, dh1_v, dxa_v = r
        _, vjp = jax.vjp(_mod_norm, x_v, *c)
        dx, dg, dsc, dsh = vjp(dh1_v)
        return [dx + dxa_v, dg, dsc, dsh]

    grad_x, dg_pre_mix, dsc1, dsh1 = _rowwise(
        pre_mix_bwd, [x, dh1, dx_a], [small["g_pre_mix"], sc1, sh1], [(D_MODEL, F32)], [d_acc] * 3, tm, "pre_mix_bwd")

    dmod = jnp.concatenate([dsh1, dsc1, dgt1, dsh2, dsc2, dgt2], axis=1)
    big = dict(w_in=dw_in, w_out=dw_out, w_mlp_in=dw_mlp_in, w_mlp_out=dw_mlp_out, w_glu=dw_glu)
    small_g = dict(
        g_pre_mix=dg_pre_mix, g_post_mix=dg_post_mix, ssm_a_re=da_re, ssm_a_im=da_im,
        ssm_log_dt=dlog_dt.reshape(1, SSM_G), ssm_b_re=db_re.reshape(SSM_G, SSM_N, SSM_P),
        ssm_b_im=db_im.reshape(SSM_G, SSM_N, SSM_P), ssm_c_re=_block_diag_out_grad(dcblk_re),
        ssm_c_im=_block_diag_out_grad(dcblk_im), ssm_d=dd_row.reshape(SSM_G, SSM_P), b_glu=db_glu,
        g_attn_out=dg_attn_out, g_ssm_out=dg_ssm_out, g_pre_mlp=dg_pre_mlp, g_post_mlp=dg_post_mlp)
    return loss, grad_x, dmod, big, small_g


MESH_ID = pl.DeviceIdType.MESH
N_DEV = 8
N_CHIPS = 4
HBM_SPEC = pl.BlockSpec(memory_space=pltpu.HBM)


def _place():
    x, y, c = lax.axis_index("x"), lax.axis_index("y"), lax.axis_index("c")
    other_chips = [(1 - x, y), (x, 1 - y), (1 - x, 1 - y)]
    return x, y, c, other_chips


def _half_rows(index, half):
    return pl.ds(pl.multiple_of(index * half, ROW_PAD), half)


def _remote(src, dst, send_sem, recv_sem, dev):
    return pltpu.make_async_remote_copy(src_ref=src, dst_ref=dst, send_sem=send_sem, recv_sem=recv_sem,
                                        device_id=dev, device_id_type=MESH_ID)


def _all_gather8(block, name):
    m_per, n = block.shape

    def body(x_ref, out_ref, send_sems, recv_sems, local_sem):
        x, y, c, chips = _place()
        me, sibling = (x, y, c), (x, y, 1 - c)

        def rows(px, py, pc):
            return out_ref.at[pl.ds((4 * px + 2 * py + pc) * m_per, m_per), :]

        def copy(k, blk, to, src=None):
            return _remote(rows(*blk) if src is None else src, rows(*blk), send_sems.at[k], recv_sems.at[k], to)

        mine = pltpu.make_async_copy(x_ref, rows(*me), local_sem)
        mine.start()
        first = [copy(0, me, sibling, src=x_ref)]
        first += [copy(1 + j, me, (*chip, c), src=x_ref) for j, chip in enumerate(chips)]
        for cp in first:
            cp.start()
        passed = [copy(4 + j, (*chip, c), sibling) for j, chip in enumerate(chips)]
        for j, chip in enumerate(chips):
            copy(1 + j, (*chip, c), me).wait_recv()
            passed[j].start()
        copy(0, sibling, me).wait_recv()
        for j, chip in enumerate(chips):
            copy(4 + j, (*chip, 1 - c), me).wait_recv()
        for cp in first + passed:
            cp.wait_send()
        mine.wait()

    return pl.pallas_call(
        body,
        name=name,
        out_shape=jax.ShapeDtypeStruct((N_DEV * m_per, n), block.dtype),
        in_specs=[pl.BlockSpec(memory_space=pltpu.VMEM)],
        out_specs=pl.BlockSpec(memory_space=pltpu.VMEM),
        scratch_shapes=[pltpu.SemaphoreType.DMA((7,)), pltpu.SemaphoreType.DMA((7,)), pltpu.SemaphoreType.DMA],
        compiler_params=_params(),
    )(block)


def _weight_gather(shards, name):
    n = len(shards)
    shapes = [s.shape for s in shards]

    def body(*refs):
        ins, outs = refs[:n], refs[n:2 * n]
        send, recv, fsend, frecv = refs[2 * n:]
        x, y, c, chips = _place()
        k_me = 2 * x + y
        sibling = (x, y, 1 - c)
        pending = []
        for a in range(n):
            half = shapes[a][0] // 2
            mine = _half_rows(c, half)
            for j, chip in enumerate(chips):
                cp = _remote(ins[a].at[mine, :], outs[a].at[k_me, mine, :], send.at[a, j], recv.at[a, j], (*chip, c))
                cp.start()
                pending.append(cp.wait_send)
        for a in range(n):
            half = shapes[a][0] // 2
            for j, (px, py) in enumerate(chips):
                piece = outs[a].at[2 * px + py, _half_rows(c, half), :]
                _remote(piece, piece, send.at[a, j], recv.at[a, j], (px, py, c)).wait_recv()
                fw = _remote(piece, piece, fsend.at[a, j], frecv.at[a, j], sibling)
                fw.start()
                pending.append(fw.wait_send)
        for a in range(n):
            half = shapes[a][0] // 2
            for j, (px, py) in enumerate(chips):
                piece = outs[a].at[2 * px + py, _half_rows(1 - c, half), :]
                _remote(piece, piece, fsend.at[a, j], frecv.at[a, j], sibling).wait_recv()
        for wait in pending:
            wait()

    sems = pltpu.SemaphoreType.DMA((n, 3))
    return pl.pallas_call(
        body,
        name=name,
        out_shape=[jax.ShapeDtypeStruct((N_CHIPS,) + s, BF16) for s in shapes],
        in_specs=[HBM_SPEC] * n,
        out_specs=[HBM_SPEC] * n,
        scratch_shapes=[sems, sems, sems, sems],
        compiler_params=_params(),
    )(*shards)


def _sibling_halves(stacks, name):
    n = len(stacks)
    shapes = [s.shape for s in stacks]

    def body(*refs):
        ins, outs = refs[:n], refs[n:2 * n]
        send, recv = refs[2 * n:]
        x, y, c, _ = _place()
        copies = []
        for a in range(n):
            half = shapes[a][1] // 2
            cp = _remote(ins[a].at[:, _half_rows(1 - c, half), :], outs[a], send.at[a], recv.at[a], (x, y, 1 - c))
            cp.start()
            copies.append(cp)
        for cp in copies:
            cp.wait()

    return pl.pallas_call(
        body,
        name=name,
        out_shape=[jax.ShapeDtypeStruct((N_CHIPS, s[1] // 2, s[2]), F32) for s in shapes],
        in_specs=[HBM_SPEC] * n,
        out_specs=[HBM_SPEC] * n,
        scratch_shapes=[pltpu.SemaphoreType.DMA((n,)), pltpu.SemaphoreType.DMA((n,))],
        compiler_params=_params(),
    )(*stacks)


def _sibling_swap(halves, name):
    n = len(halves)
    shapes = [h.shape for h in halves]

    def body(*refs):
        ins, outs = refs[:n], refs[n:2 * n]
        send, recv = refs[2 * n:]
        x, y, c, _ = _place()
        pending = []
        for a in range(n):
            half = shapes[a][0]
            mine = outs[a].at[_half_rows(c, half), :]
            cp = _remote(ins[a], mine, send.at[a], recv.at[a], (x, y, 1 - c))
            cp.start()
            pending.append(cp.wait_send)
        for a in range(n):
            half = shapes[a][0]
            theirs = outs[a].at[_half_rows(1 - c, half), :]
            _remote(theirs, theirs, send.at[a], recv.at[a], (x, y, 1 - c)).wait_recv()
        for wait in pending:
            wait()

    return pl.pallas_call(
        body,
        name=name,
        out_shape=[jax.ShapeDtypeStruct((2 * s[0], s[1]), F32) for s in shapes],
        in_specs=[HBM_SPEC] * n,
        out_specs=[HBM_SPEC] * n,
        scratch_shapes=[pltpu.SemaphoreType.DMA((n,)), pltpu.SemaphoreType.DMA((n,))],
        compiler_params=_params(),
    )(*halves)


SEM_SPEC = pl.BlockSpec(memory_space=pltpu.SEMAPHORE)
ANY_SPEC = pl.BlockSpec(memory_space=pl.ANY)
DATAFLOW = pltpu.SideEffectType.DATAFLOW_SIDE_EFFECTING


def _split_copy_start(srcs, lands, plan, n_sems, name, after=None):
    bufs = list(srcs) + list(lands)
    ns, nb = len(srcs), len(bufs)
    extra = [] if after is None else [after]

    def body(*refs):
        outs = refs[nb + len(extra):]
        for outgoing, _ in plan(refs[:ns], refs[ns:nb], outs[0], outs[1]):
            outgoing.start()
        outs[-1][...] = jnp.zeros_like(outs[-1])

    sems = pltpu.SemaphoreType.DMA((n_sems,))
    return pl.pallas_call(
        body,
        name=name,
        out_shape=(sems, sems, *[pltpu.HBM(b.shape, b.dtype) for b in bufs], jax.ShapeDtypeStruct((8, LANES), F32)),
        in_specs=[HBM_SPEC] * nb + [ANY_SPEC] * len(extra),
        out_specs=(SEM_SPEC, SEM_SPEC, *[HBM_SPEC] * nb, pl.BlockSpec(memory_space=pltpu.VMEM)),
        input_output_aliases={i: 2 + i for i in range(nb)},
        compiler_params=pltpu.CompilerParams(has_side_effects=DATAFLOW),
    )(*[pltpu.with_memory_space_constraint(b, pltpu.HBM) for b in bufs], *extra)


def _split_copy_wait(started, plan, after, name, n_srcs=None):
    send, recv, *bufs = started[:-1]
    nb = len(bufs)
    ns = nb // 2 if n_srcs is None else n_srcs

    def body(*refs):
        for outgoing, incoming in plan(refs[:ns], refs[ns:nb], refs[nb], refs[nb + 1]):
            outgoing.wait_send()
            incoming.wait_recv()

    return pl.pallas_call(
        body,
        name=name,
        out_shape=tuple(pltpu.HBM(b.shape, b.dtype) for b in bufs),
        in_specs=[HBM_SPEC] * nb + [SEM_SPEC, SEM_SPEC, ANY_SPEC],
        out_specs=tuple([HBM_SPEC] * nb),
        input_output_aliases={i: i for i in range(nb)},
        compiler_params=pltpu.CompilerParams(has_side_effects=DATAFLOW),
    )(*bufs, send, recv, after)


def _weight_plan(shapes):
    def plan(srcs, lands, send, recv):
        x, y, c, chips = _place()
        copies = []
        for a in range(len(shapes)):
            mine = _half_rows(c, shapes[a][0] // 2)
            for j, (px, py) in enumerate(chips):
                s = 3 * a + j
                arrival = lands[a].at[2 * px + py, mine, :]
                copies.append((_remote(srcs[a].at[mine, :], lands[a].at[2 * x + y, mine, :], send.at[s], recv.at[s], (px, py, c)),
                               _remote(arrival, arrival, send.at[s], recv.at[s], (px, py, c))))
        return copies
    return plan


def _halves_plan(shapes):
    def plan(srcs, lands, send, recv):
        x, y, c, _ = _place()
        copies = []
        for a in range(len(shapes)):
            theirs = srcs[a].at[:, _half_rows(1 - c, shapes[a][1] // 2), :]
            copies.append((_remote(theirs, lands[a], send.at[a], recv.at[a], (x, y, 1 - c)),
                           _remote(lands[a], lands[a], send.at[a], recv.at[a], (x, y, 1 - c))))
        return copies
    return plan


def _exchange_plan(n):
    def plan(srcs, lands, send, recv):
        x, y, c, chips = _place()
        copies = []
        for a in range(n):
            for j, (px, py) in enumerate(chips):
                s = 3 * a + j
                copies.append((_remote(srcs[a].at[2 * px + py], lands[a].at[j], send.at[s], recv.at[s], (px, py, c)),
                               _remote(lands[a].at[j], lands[a].at[j], send.at[s], recv.at[s], (px, py, c))))
        return copies
    return plan


def _forward_plan(shapes):
    def plan(stacks, _, send, recv):
        x, y, c, chips = _place()
        copies = []
        for a in range(len(shapes)):
            half = shapes[a][0] // 2
            for j, (px, py) in enumerate(chips):
                s = 3 * a + j
                mine = stacks[a].at[2 * px + py, _half_rows(c, half), :]
                theirs = stacks[a].at[2 * px + py, _half_rows(1 - c, half), :]
                copies.append((_remote(mine, mine, send.at[s], recv.at[s], (x, y, 1 - c)),
                               _remote(theirs, theirs, send.at[s], recv.at[s], (x, y, 1 - c))))
        return copies
    return plan


def _tie(x, token):
    return x + token[0:1, 0:1].astype(x.dtype)


ROW_PAD = 16


def _silu(x):
    return x * _sigmoid(x)


def _ada_fwd(c_all, w_ada, b_ada, name):
    dm, cols = w_ada.shape
    tn = 512

    def body(c_ref, w_ref, b_ref, o_ref):
        o_ref[...] = _nn(_silu(c_ref[...]).astype(BF16), w_ref[...].astype(BF16)) + b_ref[...]

    return pl.pallas_call(
        body,
        name=name,
        grid=(cols // tn,),
        in_specs=[pl.BlockSpec((ROW_PAD, dm), lambda j: (0, 0)), pl.BlockSpec((dm, tn), lambda j: (0, j)),
                  pl.BlockSpec((1, tn), lambda j: (0, j))],
        out_specs=pl.BlockSpec((ROW_PAD, tn), lambda j: (0, j)),
        out_shape=jax.ShapeDtypeStruct((ROW_PAD, cols), F32),
        compiler_params=_params(("parallel",)),
    )(c_all, w_ada, b_ada)


def _adamw(w, g, m, v):
    m = ADAM_B1 * m + (1.0 - ADAM_B1) * g
    v = ADAM_B2 * v + (1.0 - ADAM_B2) * (g * g)
    m_hat = m / (1.0 - ADAM_B1 ** ADAM_STEP)
    v_hat = v / (1.0 - ADAM_B2 ** ADAM_STEP)
    delta = -ADAM_LR * (m_hat / (jnp.sqrt(v_hat) + ADAM_EPS) + ADAM_WD * w)
    return delta, m, v


def _ada_bwd_adamw(c_all, dmod_cols, w, m, v, name):
    dm, cols = w.shape
    tm, tn = 512, 512

    def body(c_ref, d_ref, w_ref, m_ref, v_ref, g_ref, dl_ref, nm_ref, nv_ref):
        g = _tn(_silu(c_ref[...]).astype(BF16), d_ref[...].astype(BF16))
        g_ref[...] = g
        dl_ref[...], nm_ref[...], nv_ref[...] = _adamw(w_ref[...], g, m_ref[...], v_ref[...])

    tile = pl.BlockSpec((tm, tn), lambda i, j: (i, j))
    shape = jax.ShapeDtypeStruct((dm, cols), F32)
    return pl.pallas_call(
        body,
        name=name,
        grid=(dm // tm, cols // tn),
        in_specs=[pl.BlockSpec((ROW_PAD, tm), lambda i, j: (0, i)), pl.BlockSpec((ROW_PAD, tn), lambda i, j: (0, j)),
                  tile, tile, tile],
        out_specs=[tile] * 4,
        out_shape=[shape] * 4,
        compiler_params=_params(("parallel", "parallel")),
    )(c_all, dmod_cols, w, m, v)


def _sum_blocks(parts, nblk, name):
    rows, cols = parts.shape[0] // nblk, parts.shape[1]

    def body(p_ref, o_ref):
        tot = p_ref[0:rows, :]
        for b in range(1, nblk):
            tot = tot + p_ref[b * rows:(b + 1) * rows, :]
        o_ref[...] = tot

    return pl.pallas_call(body, name=name, out_shape=jax.ShapeDtypeStruct((rows, cols), F32), compiler_params=_params())(parts)


def _adamw_rows(w, g, m, v, tm, name):
    return _rowwise(lambda r, c: list(_adamw(*r)), [w, g, m, v], [], [(w.shape[1], F32)] * 3, [], tm, name)


BIG = ("w_in", "w_out", "w_mlp_in", "w_mlp_out", "w_glu")
COL_SHARDED = ("w_in", "w_out", "w_mlp_in")
SMALL = ("b_ada", "g_pre_mix", "g_post_mix", "ssm_a_re", "ssm_a_im", "ssm_log_dt", "ssm_b_re", "ssm_b_im",
         "ssm_c_re", "ssm_c_im", "ssm_d", "b_glu", "g_attn_out", "g_ssm_out", "g_pre_mlp", "g_post_mlp")
WEIGHTS = ("w_ada", "b_ada", "g_pre_mix", "g_post_mix", "w_in", "ssm_a_re", "ssm_a_im", "ssm_log_dt", "ssm_b_re",
           "ssm_b_im", "ssm_c_re", "ssm_c_im", "ssm_d", "w_glu", "b_glu", "g_attn_out", "g_ssm_out", "w_out",
           "g_pre_mlp", "g_post_mlp", "w_mlp_in", "w_mlp_out")
FLAT_COLS = 1024
FLAT_ROWS = 256
ROW_TILE = {"w_in": 256, "w_out": 128, "w_mlp_in": 256, "w_mlp_out": 256, "w_glu": 112}


def _flatten_small(tree):
    flat = jnp.concatenate([tree[k].reshape(-1) for k in SMALL])
    return jnp.pad(flat, (0, FLAT_ROWS * FLAT_COLS - flat.shape[0])).reshape(FLAT_ROWS, FLAT_COLS)


def _unflatten_small(flat, like):
    flat = flat.reshape(-1)
    out, at = {}, 0
    for k in SMALL:
        size = math.prod(like[k].shape)
        out[k] = flat[at:at + size].reshape(like[k].shape)
        at += size
    return out


def _unstack(stack, name):
    if name in COL_SHARDED:
        return stack.transpose(1, 0, 2).reshape(stack.shape[1], N_CHIPS * stack.shape[2])
    return stack.reshape(N_CHIPS * stack.shape[1], stack.shape[2])


def _stack(full, name):
    if name in COL_SHARDED:
        return full.reshape(full.shape[0], N_CHIPS, full.shape[1] // N_CHIPS).transpose(1, 0, 2)
    return full.reshape(N_CHIPS, full.shape[0] // N_CHIPS, full.shape[1])


EARLY = ("w_in", "w_out", "w_glu")
LATE = ("w_mlp_in", "w_mlp_out")
LATE_W = ("w_out", "w_glu", "w_mlp_in", "w_mlp_out")


def _chip_sums(names, g_stacks, from_sibling, ic, chip):
    own, to_send = [], []
    place = jnp.stack([ic, chip]).astype(jnp.int32)
    for k, gs, fs in zip(names, g_stacks, from_sibling):
        _, rows, cols = gs.shape
        half, tm = rows // 2, ROW_TILE[k]
        nt = half // tm

        def body(place_ref, g_ref, f_ref, own_ref, send_ref):
            s = g_ref[...] + f_ref[...]
            send_ref[...] = s.astype(BF16)

            @pl.when(pl.program_id(1) == place_ref[1])
            def _():
                own_ref[...] = s

        slab = lambda index: pl.BlockSpec((None, tm, cols), index)
        mine, to_chips = pl.pallas_call(
            body,
            name="grad_chip_sum_" + k,
            grid_spec=pltpu.PrefetchScalarGridSpec(
                num_scalar_prefetch=1,
                grid=(nt, N_CHIPS),
                in_specs=[slab(lambda i, kk, p, nt=nt: (kk, p[0] * nt + i, 0)), slab(lambda i, kk, p: (kk, i, 0))],
                out_specs=[pl.BlockSpec((tm, cols), lambda i, kk, p: (i, 0)), slab(lambda i, kk, p: (kk, i, 0))]),
            out_shape=[jax.ShapeDtypeStruct((half, cols), F32), jax.ShapeDtypeStruct((N_CHIPS, half, cols), BF16)],
            compiler_params=_params(("arbitrary", "arbitrary")),
        )(place, gs, fs)
        own.append(mine)
        to_send.append(to_chips)
    return own, to_send


def _grad_totals(names, own, from_chips):
    totals = []
    for k, mine, fc in zip(names, own, from_chips):
        half, cols = mine.shape
        tm = ROW_TILE[k]

        def body(m_ref, a_ref, b_ref, c_ref, o_ref):
            o_ref[...] = m_ref[...] + a_ref[...].astype(F32) + b_ref[...].astype(F32) + c_ref[...].astype(F32)

        rows = pl.BlockSpec((tm, cols), lambda i: (i, 0))
        totals.append(pl.pallas_call(
            body,
            name="grad_total_" + k,
            grid=(half // tm,),
            in_specs=[rows] + [pl.BlockSpec((None, tm, cols), lambda i, j=j: (j, i, 0)) for j in range(3)],
            out_specs=rows,
            out_shape=jax.ShapeDtypeStruct((half, cols), F32),
            compiler_params=_params(("parallel",)),
        )(mine, fc, fc, fc))
    return totals


class _Overlap:
    def __init__(self, own_shards, ic, chip, after):
        self.ic, self.chip = ic, chip
        self.shapes = [o.shape for o in own_shards]
        lands = [lax.empty((N_CHIPS,) + s, BF16) for s in self.shapes]
        self.gather = _split_copy_start(own_shards, lands, _weight_plan(self.shapes), 3 * len(LATE_W),
                                        "late_weight_gather_start", after=after)
        self.token = self.gather[-1]

    def late_weights_arrived(self, after):
        n = len(LATE_W)
        done = _split_copy_wait(self.gather, _weight_plan(self.shapes), after, "late_weight_gather_wait")
        self.own_shards = done[:n]
        self.forward = _split_copy_start(done[n:], [], _forward_plan(self.shapes), 3 * n, "late_weight_forward_start")
        return self.forward[-1]

    def late_weights(self, after):
        n = len(LATE_W)
        stacks = _split_copy_wait(self.forward, _forward_plan(self.shapes), after, "late_weight_forward_wait", n_srcs=n)
        stacks = [lax.dynamic_update_index_in_dim(s, o, self.chip, 0) for s, o in zip(stacks, self.own_shards)]
        return {k: _unstack(s, k) for k, s in zip(LATE_W, stacks)}

    def mlp_grads_to_sibling(self, dw_in, dw_out):
        stacks = [dw_in, dw_out]
        self.g_shapes = [s.shape for s in stacks]
        lands = [lax.empty((N_CHIPS, s[1] // 2, s[2]), F32) for s in self.g_shapes]
        self.halves = _split_copy_start(stacks, lands, _halves_plan(self.g_shapes), len(LATE), "mlp_grad_halves_start")
        return self.halves[-1]

    def mlp_grads_to_chips(self, after):
        n = len(LATE)
        done = _split_copy_wait(self.halves, _halves_plan(self.g_shapes), after, "mlp_grad_halves_wait")
        self.own, to_send = _chip_sums(LATE, done[:n], done[n:], self.ic, self.chip)
        lands = [lax.empty((3,) + s.shape[1:], BF16) for s in to_send]
        self.exchange = _split_copy_start(to_send, lands, _exchange_plan(n), 3 * n, "mlp_grad_exchange_start")
        return self.exchange[-1]

    def mlp_grads_reduced(self, after):
        n = len(LATE)
        done = _split_copy_wait(self.exchange, _exchange_plan(n), after, "mlp_grad_exchange_wait")
        return _grad_totals(LATE, self.own, done[n:])


def _pad_rows(row):
    return jnp.pad(row, ((0, 8 - row.shape[0]), (0, 0)))


def _every_eighth(gathered):
    rows = gathered.reshape(N_DEV, 8, gathered.shape[1])[:, 0, :]
    return jnp.pad(rows, ((0, ROW_PAD - N_DEV), (0, 0)))


def kernel(x, c, positions, w_ada, b_ada, g_pre_mix, g_post_mix, w_in, ssm_a_re, ssm_a_im, ssm_log_dt, ssm_b_re, ssm_b_im, ssm_c_re, ssm_c_im, ssm_d, w_glu, b_glu, g_attn_out, g_ssm_out, w_out, g_pre_mlp, g_post_mlp, w_mlp_in, w_mlp_out, loss_target, m_w_ada, m_b_ada, m_g_pre_mix, m_g_post_mix, m_w_in, m_ssm_a_re, m_ssm_a_im, m_ssm_log_dt, m_ssm_b_re, m_ssm_b_im, m_ssm_c_re, m_ssm_c_im, m_ssm_d, m_w_glu, m_b_glu, m_g_attn_out, m_g_ssm_out, m_w_out, m_g_pre_mlp, m_g_post_mlp, m_w_mlp_in, m_w_mlp_out, v_w_ada, v_b_ada, v_g_pre_mix, v_g_post_mix, v_w_in, v_ssm_a_re, v_ssm_a_im, v_ssm_log_dt, v_ssm_b_re, v_ssm_b_im, v_ssm_c_re, v_ssm_c_im, v_ssm_d, v_w_glu, v_b_glu, v_g_attn_out, v_g_ssm_out, v_w_out, v_g_pre_mlp, v_g_post_mlp, v_w_mlp_in, v_w_mlp_out):
    given = dict(locals())
    w = {k: given[k][0] for k in WEIGHTS}
    mom = {k: given["m_" + k][0] for k in WEIGHTS}
    var = {k: given["v_" + k][0] for k in WEIGHTS}
    for tree in (w, mom, var):
        for k in ("b_ada", "g_pre_mix", "g_post_mix", "ssm_log_dt", "b_glu", "g_attn_out", "g_ssm_out", "g_pre_mlp",
                  "g_post_mlp"):
            tree[k] = tree[k].reshape(1, -1)
    ix, iy, ic = lax.axis_index("x"), lax.axis_index("y"), lax.axis_index("c")
    chip = 2 * ix + iy
    me = 4 * ix + 2 * iy + ic
    shard_cols = w["w_ada"].shape[1]

    c_all = _every_eighth(_all_gather8(_pad_rows(c), "gather_c"))
    b_ada_cols = lax.dynamic_slice_in_dim(w["b_ada"], chip * shard_cols, shard_cols, axis=1)
    mod_cols = _ada_fwd(c_all, w["w_ada"], b_ada_cols, "ada_fwd")[:N_DEV]
    mod_all = _all_gather8(mod_cols, "gather_mod").reshape(N_CHIPS, 2, N_DEV, shard_cols)[:, 0]
    mod = lax.dynamic_index_in_dim(mod_all, me, axis=1, keepdims=False).reshape(1, N_MOD * D_MODEL)

    w_in_own = w["w_in"].astype(BF16)
    (w_in_stack,) = _weight_gather([w_in_own], "weight_gather")
    w_in_stack = lax.dynamic_update_index_in_dim(w_in_stack, w_in_own, chip, 0)
    wts = {"w_in": _unstack(w_in_stack, "w_in")}
    overlap = _Overlap([w[k].astype(BF16) for k in LATE_W], ic, chip, after=w_in_stack)
    mod = _tie(mod, overlap.token)

    small = {k: w[k] for k in SMALL if k != "b_ada"}
    loss, grad_x, dmod, big_g, small_g = _local_step(x[0], positions.reshape(-1, 1), mod, loss_target[0], wts, small,
                                                     hooks=overlap)
    loss = lax.psum(loss[0, 0], ("x", "y", "c"))

    small_g["b_ada"] = dmod
    parts = _all_gather8(_flatten_small(small_g), "gather_small_grads")

    g_stacks = [_stack(big_g[k], k) for k in EARLY]
    from_sibling = _sibling_halves(g_stacks, "grad_sibling_halves")
    chip_f32, chip_bf16 = _chip_sums(EARLY, g_stacks, from_sibling, ic, chip)
    exchange_plan = _exchange_plan(len(EARLY))
    lands = [lax.empty((3,) + s.shape[1:], BF16) for s in chip_bf16]
    exchange = _split_copy_start(chip_bf16, lands, exchange_plan, 3 * len(EARLY), "grad_exchange_start", after=parts)

    small_flat = _sum_blocks(parts, N_DEV, "small_grad_sum")
    grads = _unflatten_small(small_flat, w)

    mod_rows = N_MOD * D_MODEL // FLAT_COLS
    dmod_all = parts.reshape(N_DEV, FLAT_ROWS, FLAT_COLS)[:, :mod_rows].reshape(N_DEV, N_MOD * D_MODEL)
    dmod_all = jnp.pad(dmod_all, ((0, ROW_PAD - N_DEV), (0, 0)))
    dmod_cols = lax.dynamic_slice_in_dim(dmod_all, chip * shard_cols, shard_cols, axis=1)
    g_ada, d_ada, m_ada, v_ada = _ada_bwd_adamw(_tie(c_all, exchange[-1]), dmod_cols, w["w_ada"], mom["w_ada"],
                                                var["w_ada"], "ada_bwd_adamw")
    grads["w_ada"] = g_ada
    delta, new_m, new_v = {"w_ada": d_ada}, {"w_ada": m_ada}, {"w_ada": v_ada}

    def finish(names, reduced, tag):
        swapped = _sibling_swap(reduced, tag)
        for k, s, r in zip(names, swapped, reduced):
            grads[k] = lax.dynamic_update_slice_in_dim(s, r, ic * r.shape[0], axis=0)
            delta[k], new_m[k], new_v[k] = _adamw_rows(w[k], grads[k], mom[k], var[k], ROW_TILE[k], "adamw_" + k)

    finish(LATE, overlap.mlp_grads_reduced(g_ada), "mlp_grad_sibling_swap")
    from_chips = _split_copy_wait(exchange, exchange_plan, new_v[LATE[-1]], "grad_exchange_wait")[len(EARLY):]
    finish(EARLY, _grad_totals(EARLY, chip_f32, from_chips), "grad_sibling_swap")

    flat_upd = _adamw_rows(_flatten_small(w), small_flat, _flatten_small(mom), _flatten_small(var), FLAT_ROWS,
                           "adamw_small")
    for tree, flat in zip((delta, new_m, new_v), flat_upd):
        tree.update(_unflatten_small(flat, w))

    shaped = lambda tree: [tree[k].reshape(given[k].shape) for k in WEIGHTS]
    return (loss, grad_x[None], *shaped(grads), *shaped(delta), *shaped(new_m), *shaped(new_v))
```

```python
import functools
import math

import jax
import jax.numpy as jnp
import numpy as np
from jax import lax
from jax.experimental import pallas as pl
from jax.experimental.pallas import tpu as pltpu

F32 = jnp.float32
BF16 = jnp.bfloat16

D_MODEL = 2048
HEAD_DIM = 64
DILATIONS = (1, 4, 16)
ATT_SPAN = 128
ATT_BLK = 128
HEADS_PER_GROUP = 6
KV_WIDTH = HEADS_PER_GROUP * HEAD_DIM
ATT_Q_WIDTH = 3 * KV_WIDTH
ROT_DIM = 16
ROPE_THETA = 500000.0
SSM_WIDTH = 896
SSM_P = 16
SSM_G = 56
SSM_N = 64
SSM_GN = SSM_G * SSM_N
SSM_TILES = SSM_WIDTH // 128
SSM_TILE_GN = 8 * SSM_N
IN_WIDTH = 2816
OUT_IN_WIDTH = 1280
D_FF = 8192
N_MOD = 6
EPS = 1e-6
LANES = 128
SSM_SEGS = 8
SSM_CHUNK = 256
SSM_SEG_LEN = SSM_CHUNK // SSM_SEGS

ADAM_LR = 0.001
ADAM_B1 = 0.9
ADAM_B2 = 0.999
ADAM_EPS = 1e-08
ADAM_WD = 0.01
ADAM_STEP = 10

VMEM_LIMIT = 56 * 1024 * 1024


def _params(sem=None):
    return pltpu.CompilerParams(dimension_semantics=sem, vmem_limit_bytes=VMEM_LIMIT)


def _dot(a, b, dims):
    return lax.dot_general(a, b, (dims, ((), ())), preferred_element_type=F32)


def _nn(a, b):
    return _dot(a, b, ((1,), (0,)))


def _nt(a, b):
    return _dot(a, b, ((1,), (1,)))


def _tn(a, b):
    return _dot(a, b, ((0,), (0,)))


def _matmul(a, b, mode, out_dtype, tm, tn, tk, name, after=None):
    if mode == "nn":
        (m, k), (_, n) = a.shape, b.shape
        a_spec = pl.BlockSpec((tm, tk), lambda i, j, kk: (i, kk))
        b_spec = pl.BlockSpec((tk, tn), lambda i, j, kk: (kk, j))
        op = _nn
    elif mode == "nt":
        (m, k), (n, _) = a.shape, b.shape
        a_spec = pl.BlockSpec((tm, tk), lambda i, j, kk: (i, kk))
        b_spec = pl.BlockSpec((tn, tk), lambda i, j, kk: (j, kk))
        op = _nt
    else:
        (k, m), (_, n) = a.shape, b.shape
        a_spec = pl.BlockSpec((tk, tm), lambda i, j, kk: (kk, i))
        b_spec = pl.BlockSpec((tk, tn), lambda i, j, kk: (kk, j))
        op = _tn
    assert m % tm == 0 and n % tn == 0 and k % tk == 0, (name, m, n, k)
    nk = k // tk

    def body(a_ref, b_ref, *rest):
        o_ref, acc_ref = rest[-2:]
        kk = pl.program_id(2)

        @pl.when(kk == 0)
        def _():
            acc_ref[...] = jnp.zeros_like(acc_ref)

        acc_ref[...] += op(a_ref[...], b_ref[...])

        @pl.when(kk == nk - 1)
        def _():
            o_ref[...] = acc_ref[...].astype(o_ref.dtype)

    extra = [] if after is None else [after]
    return pl.pallas_call(
        body,
        name=name,
        grid=(m // tm, n // tn, nk),
        in_specs=[a_spec, b_spec] + [pl.BlockSpec(t.shape, lambda i, j, kk: (0, 0)) for t in extra],
        out_specs=pl.BlockSpec((tm, tn), lambda i, j, kk: (i, j)),
        out_shape=jax.ShapeDtypeStruct((m, n), out_dtype),
        scratch_shapes=[pltpu.VMEM((tm, tn), F32)],
        compiler_params=_params(("parallel", "parallel", "arbitrary")),
    )(a, b, *extra)


def _rowwise(fn, rows, consts, out_rows, out_accs, tm, name):
    n_rows = rows[0].shape[0]
    assert n_rows % tm == 0
    nr, nc, no = len(rows), len(consts), len(out_rows)

    def body(*refs):
        r_in, c_in = refs[:nr], refs[nr:nr + nc]
        o_row, o_acc = refs[nr + nc:nr + nc + no], refs[nr + nc + no:]
        outs = fn([r[...] for r in r_in], [c[...] for c in c_in])
        assert len(outs) == len(o_row) + len(o_acc), name
        for ref, v in zip(o_row, outs[:no]):
            ref[...] = v.astype(ref.dtype)
        first = pl.program_id(0) == 0
        for ref, v in zip(o_acc, outs[no:]):
            @pl.when(first)
            def _(ref=ref, v=v):
                ref[...] = v.astype(F32)

            @pl.when(jnp.logical_not(first))
            def _(ref=ref, v=v):
                ref[...] += v.astype(F32)

    in_specs = [pl.BlockSpec((tm, r.shape[1]), lambda i: (i, 0)) for r in rows]
    in_specs += [pl.BlockSpec(c.shape, lambda i: (0, 0)) for c in consts]
    out_specs = [pl.BlockSpec((tm, w), lambda i: (i, 0)) for w, _ in out_rows]
    out_specs += [pl.BlockSpec(s, lambda i: (0, 0)) for s in out_accs]
    out_shape = [jax.ShapeDtypeStruct((n_rows, w), dt) for w, dt in out_rows]
    out_shape += [jax.ShapeDtypeStruct(s, F32) for s in out_accs]
    return pl.pallas_call(
        body,
        name=name,
        grid=(n_rows // tm,),
        in_specs=in_specs,
        out_specs=out_specs,
        out_shape=out_shape,
        compiler_params=_params(("arbitrary",)),
    )(*rows, *consts)


def _rms(x, g):
    return x * lax.rsqrt(jnp.mean(x * x, axis=-1, keepdims=True) + EPS) * g


def _mod_norm(x, g, sc, sh):
    return _rms(x, g) * (1.0 + sc) + sh


def _gelu(x):
    return 0.5 * x * (1.0 + jnp.tanh(math.sqrt(2.0 / math.pi) * (x + 0.044715 * (x * x * x))))


def _sigmoid(x):
    return 1.0 / (1.0 + jnp.exp(-x))


def _post_mix(x, mix, g_post, gt1, g_pre, sc2, sh2):
    x1 = x + gt1 * _rms(mix, g_post)
    return x1, _mod_norm(x1, g_pre, sc2, sh2)


def _att_mix(o0, o1, o2, l0, l1, l2, g):
    m = jnp.maximum(jnp.maximum(l0, l1), l2)
    e0, e1, e2 = jnp.exp(l0 - m), jnp.exp(l1 - m), jnp.exp(l2 - m)
    att = (e0 * o0 + e1 * o1 + e2 * o2) / (e0 + e1 + e2)
    return _rms(att, g)


def _glu_out(y2, z, g):
    return _rms(y2 * _sigmoid(z), g)


def _rope_tables(pos_col, freq_lane, name):
    n_rows = pos_col.shape[0]
    tm = 512

    def body(p_ref, f_ref, cos_ref, lo_ref, hi_ref):
        ang = p_ref[...].astype(F32) * f_ref[...]
        lane = lax.broadcasted_iota(jnp.int32, ang.shape, 1) % HEAD_DIM
        c, s = jnp.cos(ang), jnp.sin(ang)
        cos_ref[...] = jnp.where(lane < ROT_DIM, c, 1.0)
        lo_ref[...] = jnp.where(lane < ROT_DIM // 2, -s, 0.0)
        hi_ref[...] = jnp.where((lane >= ROT_DIM // 2) & (lane < ROT_DIM), s, 0.0)

    tab = jax.ShapeDtypeStruct((n_rows, LANES), F32)
    return pl.pallas_call(
        body,
        name=name,
        grid=(n_rows // tm,),
        in_specs=[pl.BlockSpec((tm, 1), lambda i: (i, 0)), pl.BlockSpec((1, LANES), lambda i: (0, 0))],
        out_specs=[pl.BlockSpec((tm, LANES), lambda i: (i, 0))] * 3,
        out_shape=[tab] * 3,
        compiler_params=_params(("parallel",)),
    )(pos_col, freq_lane)


def _rope(x, cos_t, lo_t, hi_t):
    half = ROT_DIM // 2
    return x * cos_t + pltpu.roll(x, LANES - half, 1) * lo_t + pltpu.roll(x, half, 1) * hi_t


def _rope_transposed(dy, cos_t, lo_t, hi_t):
    half = ROT_DIM // 2
    return dy * cos_t + pltpu.roll(dy * lo_t, half, 1) + pltpu.roll(dy * hi_t, LANES - half, 1)


def _att_masks(i, k0):
    q_pos = i * ATT_BLK + lax.broadcasted_iota(jnp.int32, (ATT_BLK, 2 * ATT_BLK), 0)
    k_pos = k0 + lax.broadcasted_iota(jnp.int32, (ATT_BLK, 2 * ATT_BLK), 1)
    dist = q_pos - k_pos
    return (dist >= 0) & (dist <= ATT_SPAN)


def _head_lane_masks():
    lane = lax.broadcasted_iota(jnp.int32, (1, LANES), 1)
    return lane < HEAD_DIM, lane >= HEAD_DIM


def _att_specs(gi, n_rows):
    col = lambda at: pl.BlockSpec((n_rows, LANES), lambda hp: (0, at + hp))
    qkv = [col(gi * 3), col(9), col(12)]
    tabs = [pl.BlockSpec((n_rows, LANES), lambda hp: (0, 0), pipeline_mode=pl.Buffered(1))] * 3
    head_in = col(0) if DILATIONS[gi] > 1 else pl.BlockSpec((n_rows, LANES), lambda hp: (0, hp),
                                                            pipeline_mode=pl.Buffered(1))
    return qkv, tabs, head_in, col(0)


def _sub_rows(d, n, r):
    return pl.ds(r, n, stride=d) if d > 1 else pl.ds(0, n)


def _att_load(q_ref, k_ref, v_ref, tabs, sub, qs, ks, vs):
    cos_t, lo_t, hi_t = tabs
    qs[...] = (_rope(q_ref[sub, :], cos_t, lo_t, hi_t) * (1.0 / math.sqrt(HEAD_DIM))).astype(BF16)
    ks[...] = _rope(k_ref[sub, :], cos_t, lo_t, hi_t).astype(BF16)
    vs[...] = v_ref[sub, :].astype(BF16)


def _att_fwd(proj, tabs, gi, name):
    n_rows = proj.shape[0]
    d = DILATIONS[gi]
    n = n_rows // d
    nb = n // ATT_BLK

    def body(q_ref, k_ref, v_ref, cos_ref, lo_ref, hi_ref, o_ref, l_ref, qs, ks, vs, o_s, l_s):
        m0, m1 = _head_lane_masks()

        def step(i, carry):
            k0 = pl.multiple_of(jnp.maximum(i - 1, 0) * ATT_BLK, ATT_BLK)
            q0 = pl.multiple_of(i * ATT_BLK, ATT_BLK)
            q = qs[pl.ds(q0, ATT_BLK), :]
            k = ks[pl.ds(k0, 2 * ATT_BLK), :]
            v = vs[pl.ds(k0, 2 * ATT_BLK), :]
            valid = _att_masks(i, k0)
            outs, lses = [], []
            for hm in (m0, m1):
                s = _nt(jnp.where(hm, q, jnp.zeros_like(q)), k)
                s = jnp.where(valid, s, -1e30)
                mx = jnp.max(s, axis=1, keepdims=True)
                p = jnp.exp(s - mx)
                den = jnp.sum(p, axis=1, keepdims=True)
                outs.append(_nn(p.astype(BF16), v) / den)
                lses.append(mx + jnp.log(den))
            o_s[pl.ds(q0, ATT_BLK), :] = jnp.where(m0, outs[0], outs[1])
            l_s[pl.ds(q0, ATT_BLK), :] = jnp.where(m0, lses[0], lses[1])
            return carry

        for r in range(d):
            sub = _sub_rows(d, n, r)
            _att_load(q_ref, k_ref, v_ref, (cos_ref[sub, :], lo_ref[sub, :], hi_ref[sub, :]), sub, qs, ks, vs)
            lax.fori_loop(0, nb, step, 0, unroll=2)
            o_ref[sub, :] = o_s[...]
            l_ref[sub, :] = l_s[...]

    qkv, tab_specs, _, head_out = _att_specs(gi, n_rows)
    out = jax.ShapeDtypeStruct((n_rows, KV_WIDTH), F32)
    return pl.pallas_call(
        body,
        name=name,
        grid=(3,),
        in_specs=qkv + tab_specs,
        out_specs=[head_out, head_out],
        out_shape=[out, out],
        scratch_shapes=[pltpu.VMEM((n, LANES), BF16)] * 3 + [pltpu.VMEM((n, LANES), F32)] * 2,
        compiler_params=_params(("parallel",)),
    )(proj, proj, proj, *tabs)


def _att_bwd(proj, tabs, o, l, do, dl, gi, name):
    n_rows = proj.shape[0]
    d = DILATIONS[gi]
    n = n_rows // d
    nb = n // ATT_BLK

    def body(q_ref, k_ref, v_ref, cos_ref, lo_ref, hi_ref, o_ref, l_ref, do_ref, dl_ref,
             dq_ref, dk_ref, dv_ref, qs, ks, vs, dq_s, dk_acc, dv_acc, *gathered):
        m0, m1 = _head_lane_masks()
        o_s, l_s, do_s, dl_s = gathered if d > 1 else (o_ref, l_ref, do_ref, dl_ref)

        def step(i, carry):
            k0 = pl.multiple_of(jnp.maximum(i - 1, 0) * ATT_BLK, ATT_BLK)
            q0 = pl.multiple_of(i * ATT_BLK, ATT_BLK)
            rows = pl.ds(q0, ATT_BLK)
            keys = pl.ds(k0, 2 * ATT_BLK)
            q, k, v = qs[rows, :], ks[keys, :], vs[keys, :]
            d_o, lse = do_s[rows, :], l_s[rows, :]
            o_do = o_s[rows, :] * d_o
            d_l = dl_s[rows, :]
            valid = _att_masks(i, k0)
            dq = jnp.zeros((ATT_BLK, LANES), F32)
            dk = jnp.zeros((2 * ATT_BLK, LANES), F32)
            dv = jnp.zeros((2 * ATT_BLK, LANES), F32)
            for hm in (m0, m1):
                qh, kh = jnp.where(hm, q, jnp.zeros_like(q)), jnp.where(hm, k, jnp.zeros_like(k))
                doh = jnp.where(hm, d_o, 0.0).astype(BF16)
                lse_h = jnp.max(jnp.where(hm, lse, -1e30), axis=1, keepdims=True)
                delta = jnp.sum(jnp.where(hm, o_do, 0.0), axis=1, keepdims=True)
                dlse = jnp.sum(jnp.where(hm, d_l, 0.0), axis=1, keepdims=True)
                s = jnp.where(valid, _nt(qh, k), -1e30)
                p = jnp.exp(s - lse_h)
                dv = dv + _tn(p.astype(BF16), doh)
                ds = (p * (_nt(doh, v) - delta + dlse)).astype(BF16)
                dq = dq + _nn(ds, kh)
                dk = dk + _tn(ds, qh)
            dq_s[rows, :] = dq * (1.0 / math.sqrt(HEAD_DIM))
            dk_acc[keys, :] += dk
            dv_acc[keys, :] += dv
            return carry

        for r in range(d):
            sub = _sub_rows(d, n, r)
            rot = (cos_ref[sub, :], lo_ref[sub, :], hi_ref[sub, :])
            _att_load(q_ref, k_ref, v_ref, rot, sub, qs, ks, vs)
            if d > 1:
                for dst, src in zip(gathered, (o_ref, l_ref, do_ref, dl_ref)):
                    dst[...] = src[sub, :]
            dk_acc[...] = jnp.zeros_like(dk_acc)
            dv_acc[...] = jnp.zeros_like(dv_acc)
            lax.fori_loop(0, nb, step, 0, unroll=2)
            dq_ref[sub, :] = _rope_transposed(dq_s[...], *rot)
            dk_ref[sub, :] = _rope_transposed(dk_acc[...], *rot)
            dv_ref[sub, :] = dv_acc[...]

    qkv, tab_specs, head_in, head_out = _att_specs(gi, n_rows)
    out = jax.ShapeDtypeStruct((n_rows, KV_WIDTH), F32)
    sub_f32 = pltpu.VMEM((n, LANES), F32)
    return pl.pallas_call(
        body,
        name=name,
        grid=(3,),
        in_specs=qkv + tab_specs + [head_in] * 4,
        out_specs=[head_out] * 3,
        out_shape=[out] * 3,
        scratch_shapes=[pltpu.VMEM((n, LANES), BF16)] * 3 + [sub_f32] * (3 if d == 1 else 7),
        compiler_params=_params(("parallel",)),
    )(proj, proj, proj, *tabs, o, l, do, dl)


def _expand_np():
    e = np.zeros((SSM_N, SSM_N * SSM_P), np.float32)
    for nn in range(SSM_N):
        e[nn, nn * SSM_P:(nn + 1) * SSM_P] = 1.0
    return e


def _ssm_prep_math(a_re, a_im, log_dt, b_re, b_im, expand):
    dt = jnp.exp(log_dt)
    mag = jnp.exp(a_re * dt)
    ab_re, ab_im = mag * jnp.cos(a_im * dt), mag * jnp.sin(a_im * dt)
    den = a_re * a_re + a_im * a_im
    num_re, num_im = ab_re - 1.0, ab_im
    co_re = (num_re * a_re + num_im * a_im) / den
    co_im = (num_im * a_re - num_re * a_im) / den
    hi = lax.Precision.HIGHEST
    co_re_x = jnp.dot(co_re, expand, precision=hi, preferred_element_type=F32)
    co_im_x = jnp.dot(co_im, expand, precision=hi, preferred_element_type=F32)
    bb_re = co_re_x * b_re - co_im_x * b_im
    bb_im = co_re_x * b_im + co_im_x * b_re
    return ab_re, ab_im, bb_re, bb_im


def _ssm_prep(a_re, a_im, log_dt, b_re, b_im, expand, name):
    def body(ar, ai, ld, br, bi, ex, o0, o1, o2, o3):
        outs = _ssm_prep_math(ar[...], ai[...], ld[...], br[...], bi[...], ex[...])
        for ref, v in zip((o0, o1, o2, o3), outs):
            ref[...] = v

    gn = jax.ShapeDtypeStruct((SSM_G, SSM_N), F32)
    gnp = jax.ShapeDtypeStruct((SSM_G, SSM_N * SSM_P), F32)
    return pl.pallas_call(body, name=name, out_shape=[gn, gn, gnp, gnp], compiler_params=_params())(
        a_re, a_im, log_dt, b_re, b_im, expand)


def _ssm_prep_bwd(a_re, a_im, log_dt, b_re, b_im, expand, cts, name):
    def body(ar, ai, ld, br, bi, ex, c0, c1, c2, c3, o0, o1, o2, o3, o4):
        ex_v = ex[...]
        _, vjp = jax.vjp(lambda *p: _ssm_prep_math(*p, ex_v), ar[...], ai[...], ld[...], br[...], bi[...])
        for ref, v in zip((o0, o1, o2, o3, o4), vjp((c0[...], c1[...], c2[...], c3[...]))):
            ref[...] = v

    gn = jax.ShapeDtypeStruct((SSM_G, SSM_N), F32)
    gnp = jax.ShapeDtypeStruct((SSM_G, SSM_N * SSM_P), F32)
    g1 = jax.ShapeDtypeStruct((SSM_G, 1), F32)
    return pl.pallas_call(body, name=name, out_shape=[gn, gn, g1, gnp, gnp], compiler_params=_params())(
        a_re, a_im, log_dt, b_re, b_im, expand, *cts)


def _block_diag_in(bb):
    t = bb.reshape(SSM_TILES, 8, SSM_N, SSM_P).transpose(0, 1, 3, 2)
    eye = jnp.eye(8, dtype=bb.dtype)
    return (t[:, :, :, None, :] * eye[None, :, None, :, None]).reshape(SSM_TILES, LANES, SSM_TILE_GN)


def _block_diag_in_grad(dblk):
    t = dblk.reshape(SSM_TILES, 8, SSM_P, 8, SSM_N)
    t = jnp.einsum("tapbn,ab->tapn", t, jnp.eye(8, dtype=dblk.dtype))
    return t.transpose(0, 1, 3, 2).reshape(SSM_G, SSM_N, SSM_P)


def _block_diag_out(cm):
    t = cm.reshape(SSM_TILES, 8, SSM_P, SSM_N).transpose(0, 1, 3, 2)
    eye = jnp.eye(8, dtype=cm.dtype)
    return (t[:, :, :, None, :] * eye[None, :, None, :, None]).reshape(SSM_TILES, SSM_TILE_GN, LANES)


def _block_diag_out_grad(dblk):
    t = dblk.reshape(SSM_TILES, 8, SSM_N, 8, SSM_P)
    t = jnp.einsum("tanbp,ab->tanp", t, jnp.eye(8, dtype=dblk.dtype))
    return t.transpose(0, 1, 3, 2).reshape(SSM_G, SSM_P, SSM_N)


def _cmul_add(a_re, a_im, s_re, s_im, b_re, b_im):
    return a_re * s_re - a_im * s_im + b_re, a_re * s_im + a_im * s_re + b_im


def _lane_tile_specs(first_tile, index):
    return [pl.BlockSpec((SSM_CHUNK, LANES), lambda c, t=t: (index(c), first_tile + t)) for t in range(SSM_TILES)]


def _ssm_load_rows(src_refs, dst):
    for t in range(SSM_TILES):
        for i in range(SSM_SEG_LEN):
            dst[i * SSM_SEGS:(i + 1) * SSM_SEGS, t * LANES:(t + 1) * LANES] = (
                src_refs[t][pl.ds(i, SSM_SEGS, stride=SSM_SEG_LEN), :])


def _ssm_store_rows(src, dst_refs):
    for t in range(SSM_TILES):
        for i in range(SSM_SEG_LEN):
            dst_refs[t][pl.ds(i, SSM_SEGS, stride=SSM_SEG_LEN), :] = (
                src[i * SSM_SEGS:(i + 1) * SSM_SEGS, t * LANES:(t + 1) * LANES])


def _ssm_powers(ab_re_ref, ab_im_ref, pw_re, pw_im):
    a_re, a_im = ab_re_ref[...], ab_im_ref[...]
    p_re, p_im = a_re, a_im
    for i in range(SSM_SEG_LEN):
        pw_re[i:i + 1, :] = p_re
        pw_im[i:i + 1, :] = p_im
        p_re, p_im = _cmul_add(a_re, a_im, p_re, p_im, 0.0, 0.0)


def _ssm_input_proj(u_s, bblk_re_ref, bblk_im_ref, s_re, s_im):
    for t in range(SSM_TILES):
        ub = u_s[:, t * LANES:(t + 1) * LANES].astype(BF16)
        cols = slice(t * SSM_TILE_GN, (t + 1) * SSM_TILE_GN)
        s_re[:, cols] = _nn(ub, bblk_re_ref[t])
        s_im[:, cols] = _nn(ub, bblk_im_ref[t])


def _ssm_scan(ab_re_ref, ab_im_ref, s_re, s_im, init_re, init_im, conj, reverse):
    sign = -1.0 if conj else 1.0
    for t in range(SSM_TILES):
        cols = slice(t * SSM_TILE_GN, (t + 1) * SSM_TILE_GN)
        a_re = jnp.broadcast_to(ab_re_ref[:, cols], (SSM_SEGS, SSM_TILE_GN))
        a_im = jnp.broadcast_to(ab_im_ref[:, cols], (SSM_SEGS, SSM_TILE_GN)) * sign
        if init_re is None:
            st = (jnp.zeros((SSM_SEGS, SSM_TILE_GN), F32),) * 2
        else:
            st = (init_re[:, cols], init_im[:, cols])

        def step(i, st, cols=cols, a_re=a_re, a_im=a_im):
            idx = (SSM_SEG_LEN - 1 - i) if reverse else i
            rows = pl.ds(pl.multiple_of(idx * SSM_SEGS, SSM_SEGS), SSM_SEGS)
            n_re, n_im = _cmul_add(a_re, a_im, st[0], st[1], s_re[rows, cols], s_im[rows, cols])
            s_re[rows, cols] = n_re
            s_im[rows, cols] = n_im
            return n_re, n_im
        lax.fori_loop(0, SSM_SEG_LEN, step, st, unroll=4)


def _ssm_fixup(pw_re, pw_im, s_re, s_im, cin_re, cin_im, conj, reverse):
    sign = -1.0 if conj else 1.0
    c_re, c_im = cin_re[...], cin_im[...]

    def step(i, c):
        k = (SSM_SEG_LEN - 1 - i) if reverse else i
        rows = pl.ds(pl.multiple_of(i * SSM_SEGS, SSM_SEGS), SSM_SEGS)
        p_re = jnp.broadcast_to(pw_re[pl.ds(k, 1), :], (SSM_SEGS, SSM_GN))
        p_im = jnp.broadcast_to(pw_im[pl.ds(k, 1), :], (SSM_SEGS, SSM_GN)) * sign
        n_re, n_im = _cmul_add(p_re, p_im, c_re, c_im, s_re[rows, :], s_im[rows, :])
        s_re[rows, :] = n_re
        s_im[rows, :] = n_im
        return c
    lax.fori_loop(0, SSM_SEG_LEN, step, 0)


def _ssm_fwd(proj, ab_re, ab_im, bblk_re, bblk_im, cblk_re, cblk_im, d_row, name):
    n_rows = proj.shape[0]
    nchunk = n_rows // SSM_CHUNK
    last = SSM_SEG_LEN - 1
    nt = SSM_TILES

    def body(*refs):
        u_ref, y_ref = refs[:nt], refs[nt + 7:2 * nt + 7]
        ar_ref, ai_ref, br_ref, bi_ref, cr_ref, ci_ref, d_ref = refs[nt:nt + 7]
        cin_re_ref, cin_im_ref, u_s, s_re, s_im, pw_re, pw_im, st_re, st_im = refs[2 * nt + 7:]

        @pl.when(pl.program_id(0) == 0)
        def _():
            _ssm_powers(ar_ref, ai_ref, pw_re, pw_im)
            st_re[...] = jnp.zeros_like(st_re)
            st_im[...] = jnp.zeros_like(st_im)

        _ssm_load_rows(u_ref, u_s)
        _ssm_input_proj(u_s, br_ref, bi_ref, s_re, s_im)
        _ssm_scan(ar_ref, ai_ref, s_re, s_im, None, None, conj=False, reverse=False)
        p_re, p_im = pw_re[last:last + 1, :], pw_im[last:last + 1, :]
        c_re, c_im = st_re[...], st_im[...]
        for j in range(SSM_SEGS):
            cin_re_ref[j:j + 1, :] = c_re
            cin_im_ref[j:j + 1, :] = c_im
            row = last * SSM_SEGS + j
            c_re, c_im = _cmul_add(p_re, p_im, c_re, c_im, s_re[row:row + 1, :], s_im[row:row + 1, :])
        st_re[...] = c_re
        st_im[...] = c_im
        _ssm_fixup(pw_re, pw_im, s_re, s_im, cin_re_ref, cin_im_ref, conj=False, reverse=False)
        for t in range(SSM_TILES):
            cols = slice(t * SSM_TILE_GN, (t + 1) * SSM_TILE_GN)
            lanes = slice(t * LANES, (t + 1) * LANES)
            y = _nn(s_re[:, cols].astype(BF16), cr_ref[t]) - _nn(s_im[:, cols].astype(BF16), ci_ref[t])
            u_s[:, lanes] = y + d_ref[:, lanes] * u_s[:, lanes]
        _ssm_store_rows(u_s, y_ref)

    whole2 = lambda a: pl.BlockSpec(a.shape, lambda c: (0, 0))
    whole3 = lambda a: pl.BlockSpec(a.shape, lambda c: (0, 0, 0))
    seg = pl.BlockSpec((SSM_SEGS, SSM_GN), lambda c: (c, 0))
    seg_shape = jax.ShapeDtypeStruct((nchunk * SSM_SEGS, SSM_GN), F32)
    res = pl.pallas_call(
        body,
        name=name,
        grid=(nchunk,),
        in_specs=_lane_tile_specs((IN_WIDTH - SSM_WIDTH) // LANES, lambda c: c) + [
            whole2(ab_re), whole2(ab_im), whole3(bblk_re), whole3(bblk_im), whole3(cblk_re), whole3(cblk_im),
            whole2(d_row)],
        out_specs=[pl.BlockSpec((SSM_CHUNK, LANES), lambda c: (c, 0))] * nt + [seg, seg],
        out_shape=[jax.ShapeDtypeStruct((n_rows, LANES), F32)] * nt + [seg_shape, seg_shape],
        scratch_shapes=[pltpu.VMEM((SSM_CHUNK, SSM_WIDTH), F32), pltpu.VMEM((SSM_CHUNK, SSM_GN), F32),
                        pltpu.VMEM((SSM_CHUNK, SSM_GN), F32), pltpu.VMEM((SSM_SEG_LEN, SSM_GN), F32),
                        pltpu.VMEM((SSM_SEG_LEN, SSM_GN), F32), pltpu.VMEM((1, SSM_GN), F32),
                        pltpu.VMEM((1, SSM_GN), F32)],
        compiler_params=_params(("arbitrary",)),
    )(*[proj] * nt, ab_re, ab_im, bblk_re, bblk_im, cblk_re, cblk_im, d_row)
    return res[:nt], res[nt], res[nt + 1]


def _ssm_bwd(proj, dy, cin_re, cin_im, ab_re, ab_im, bblk_re, bblk_im, cblk_re, cblk_im, d_row, name):
    n_rows = proj.shape[0]
    nchunk = n_rows // SSM_CHUNK
    nt = SSM_TILES

    def body(*refs):
        u_ref, dy_ref, du_ref = refs[:nt], refs[nt:2 * nt], refs[2 * nt + 9:3 * nt + 9]
        cin_re_ref, cin_im_ref, ar_ref, ai_ref, br_ref, bi_ref, cr_ref, ci_ref, d_ref = refs[2 * nt:2 * nt + 9]
        (dar_ref, dai_ref, dbr_ref, dbi_ref, dcr_ref, dci_ref, dd_ref,
         u_s, dy_s, s_re, s_im, q_re, q_im, pw_re, pw_im, qst_re, qst_im, qin_re, qin_im) = refs[3 * nt + 9:]

        @pl.when(pl.program_id(0) == 0)
        def _():
            _ssm_powers(ar_ref, ai_ref, pw_re, pw_im)
            qst_re[...] = jnp.zeros_like(qst_re)
            qst_im[...] = jnp.zeros_like(qst_im)
            for ref in (dar_ref, dai_ref, dbr_ref, dbi_ref, dcr_ref, dci_ref, dd_ref):
                ref[...] = jnp.zeros_like(ref)

        _ssm_load_rows(u_ref, u_s)
        _ssm_load_rows(dy_ref, dy_s)
        _ssm_input_proj(u_s, br_ref, bi_ref, s_re, s_im)
        _ssm_scan(ar_ref, ai_ref, s_re, s_im, cin_re_ref, cin_im_ref, conj=False, reverse=False)
        for t in range(SSM_TILES):
            cols = slice(t * SSM_TILE_GN, (t + 1) * SSM_TILE_GN)
            dyb = dy_s[:, t * LANES:(t + 1) * LANES].astype(BF16)
            q_re[:, cols] = _nt(dyb, cr_ref[t])
            q_im[:, cols] = -_nt(dyb, ci_ref[t])
            dcr_ref[t] += _tn(s_re[:, cols].astype(BF16), dyb)
            dci_ref[t] -= _tn(s_im[:, cols].astype(BF16), dyb)
        _ssm_scan(ar_ref, ai_ref, q_re, q_im, None, None, conj=True, reverse=True)
        last = SSM_SEG_LEN - 1
        p_re, p_im = pw_re[last:last + 1, :], -pw_im[last:last + 1, :]
        c_re, c_im = qst_re[...], qst_im[...]
        for j in reversed(range(SSM_SEGS)):
            qin_re[j:j + 1, :] = c_re
            qin_im[j:j + 1, :] = c_im
            c_re, c_im = _cmul_add(p_re, p_im, c_re, c_im, q_re[j:j + 1, :], q_im[j:j + 1, :])
        qst_re[...] = c_re
        qst_im[...] = c_im
        _ssm_fixup(pw_re, pw_im, q_re, q_im, qin_re, qin_im, conj=True, reverse=True)
        for t in range(SSM_TILES):
            cols = slice(t * SSM_TILE_GN, (t + 1) * SSM_TILE_GN)

            def step(i, acc, cols=cols):
                rows = pl.ds(pl.multiple_of(i * SSM_SEGS, SSM_SEGS), SSM_SEGS)
                prev = pl.ds(pl.multiple_of((i - 1) * SSM_SEGS, SSM_SEGS), SSM_SEGS)
                qr, qi = q_re[rows, cols], q_im[rows, cols]
                sr, si = s_re[prev, cols], s_im[prev, cols]
                return acc[0] + qr * sr + qi * si, acc[1] + qi * sr - qr * si

            qr, qi = q_re[0:SSM_SEGS, cols], q_im[0:SSM_SEGS, cols]
            sr, si = cin_re_ref[:, cols], cin_im_ref[:, cols]
            acc = lax.fori_loop(1, SSM_SEG_LEN, step, (qr * sr + qi * si, qi * sr - qr * si))
            dar_ref[:, cols] += jnp.sum(acc[0], axis=0, keepdims=True)
            dai_ref[:, cols] += jnp.sum(acc[1], axis=0, keepdims=True)
        for t in range(SSM_TILES):
            cols = slice(t * SSM_TILE_GN, (t + 1) * SSM_TILE_GN)
            lanes = slice(t * LANES, (t + 1) * LANES)
            qrb, qib = q_re[:, cols].astype(BF16), q_im[:, cols].astype(BF16)
            u_t, dy_t = u_s[:, lanes], dy_s[:, lanes]
            ub = u_t.astype(BF16)
            dbr_ref[t] += _tn(ub, qrb)
            dbi_ref[t] += _tn(ub, qib)
            dd_ref[:, lanes] += jnp.sum(dy_t * u_t, axis=0, keepdims=True)
            u_s[:, lanes] = _nt(qrb, br_ref[t]) + _nt(qib, bi_ref[t]) + dy_t * d_ref[:, lanes]
        _ssm_store_rows(u_s, du_ref)

    whole2 = lambda a: pl.BlockSpec(a.shape, lambda c: (0, 0))
    whole3 = lambda a: pl.BlockSpec(a.shape, lambda c: (0, 0, 0))
    back = lambda c: nchunk - 1 - c
    seg = pl.BlockSpec((SSM_SEGS, SSM_GN), lambda c: (back(c), 0))
    gn_row = jax.ShapeDtypeStruct((1, SSM_GN), F32)
    b_shape = jax.ShapeDtypeStruct((SSM_TILES, LANES, SSM_TILE_GN), F32)
    c_shape = jax.ShapeDtypeStruct((SSM_TILES, SSM_TILE_GN, LANES), F32)
    d_shape = jax.ShapeDtypeStruct((1, SSM_WIDTH), F32)
    big = pltpu.VMEM((SSM_CHUNK, SSM_GN), F32)
    res = pl.pallas_call(
        body,
        name=name,
        grid=(nchunk,),
        in_specs=_lane_tile_specs((IN_WIDTH - SSM_WIDTH) // LANES, back) + _lane_tile_specs(0, back) + [
            seg, seg, whole2(ab_re), whole2(ab_im), whole3(bblk_re), whole3(bblk_im), whole3(cblk_re),
            whole3(cblk_im), whole2(d_row)],
        out_specs=[pl.BlockSpec((SSM_CHUNK, LANES), lambda c: (back(c), 0))] * nt + [
            whole2(ab_re), whole2(ab_im), whole3(bblk_re), whole3(bblk_im), whole3(cblk_re), whole3(cblk_im),
            whole2(d_row)],
        out_shape=[jax.ShapeDtypeStruct((n_rows, LANES), F32)] * nt + [gn_row, gn_row, b_shape, b_shape, c_shape,
                                                                       c_shape, d_shape],
        scratch_shapes=[pltpu.VMEM((SSM_CHUNK, SSM_WIDTH), F32), pltpu.VMEM((SSM_CHUNK, SSM_WIDTH), F32),
                        big, big, big, big,
                        pltpu.VMEM((SSM_SEG_LEN, SSM_GN), F32), pltpu.VMEM((SSM_SEG_LEN, SSM_GN), F32),
                        pltpu.VMEM((1, SSM_GN), F32), pltpu.VMEM((1, SSM_GN), F32),
                        pltpu.VMEM((SSM_SEGS, SSM_GN), F32), pltpu.VMEM((SSM_SEGS, SSM_GN), F32)],
        compiler_params=_params(("arbitrary",)),
    )(*[proj] * nt, *[dy] * nt, cin_re, cin_im, ab_re, ab_im, bblk_re, bblk_im, cblk_re, cblk_im, d_row)
    return (res[:nt], *res[nt:])


def _mlp_fwd(h2, w1, w2, tm, tf, name):
    n_rows, dm = h2.shape
    dff = w1.shape[1]

    def body(h_ref, w1_ref, w2_ref, a_ref, y_ref):
        a = _nn(h_ref[...], w1_ref[...])
        a_ref[...] = a.astype(BF16)
        r = jnp.maximum(a, 0.0)
        part = _nn((r * r).astype(BF16), w2_ref[...])
        j = pl.program_id(1)

        @pl.when(j == 0)
        def _():
            y_ref[...] = part

        @pl.when(j > 0)
        def _():
            y_ref[...] += part

    return pl.pallas_call(
        body,
        name=name,
        grid=(n_rows // tm, dff // tf),
        in_specs=[pl.BlockSpec((tm, dm), lambda i, j: (i, 0)), pl.BlockSpec((dm, tf), lambda i, j: (0, j)),
                  pl.BlockSpec((tf, dm), lambda i, j: (j, 0))],
        out_specs=[pl.BlockSpec((tm, tf), lambda i, j: (i, j)), pl.BlockSpec((tm, dm), lambda i, j: (i, 0))],
        out_shape=[jax.ShapeDtypeStruct((n_rows, dff), BF16), jax.ShapeDtypeStruct((n_rows, dm), F32)],
        compiler_params=_params(("parallel", "arbitrary")),
    )(h2, w1, w2)


def _mlp_bwd(dy, h2, a, w2, tm, tf, name):
    n_rows, dm = h2.shape
    dff = a.shape[1]
    per_chip = dff // N_CHIPS // tf

    def body(dy_ref, h_ref, a_ref, w2_ref, da_ref, dw2_ref, dw1_ref):
        dyb = dy_ref[...]
        r = jnp.maximum(a_ref[...].astype(F32), 0.0)
        da = (_nt(dyb, w2_ref[...]) * (2.0 * r)).astype(BF16)
        da_ref[...] = da
        p2 = _tn((r * r).astype(BF16), dyb)
        p1 = _tn(h_ref[...], da)
        i = pl.program_id(1)

        @pl.when(i == 0)
        def _():
            dw2_ref[...] = p2
            dw1_ref[...] = p1

        @pl.when(i > 0)
        def _():
            dw2_ref[...] += p2
            dw1_ref[...] += p1

    return pl.pallas_call(
        body,
        name=name,
        grid=(dff // tf, n_rows // tm),
        in_specs=[pl.BlockSpec((tm, dm), lambda j, i: (i, 0)), pl.BlockSpec((tm, dm), lambda j, i: (i, 0)),
                  pl.BlockSpec((tm, tf), lambda j, i: (i, j)), pl.BlockSpec((tf, dm), lambda j, i: (j, 0))],
        out_specs=[pl.BlockSpec((tm, tf), lambda j, i: (i, j)), pl.BlockSpec((tf, dm), lambda j, i: (j, 0)),
                   pl.BlockSpec((None, dm, tf), lambda j, i: (j // per_chip, 0, j % per_chip))],
        out_shape=[jax.ShapeDtypeStruct((n_rows, dff), BF16), jax.ShapeDtypeStruct((dff, dm), F32),
                   jax.ShapeDtypeStruct((N_CHIPS, dm, dff // N_CHIPS), F32)],
        compiler_params=_params(("parallel", "arbitrary")),
    )(dy, h2, a, w2)


def _local_step(x, pos_col, mod, target, wts, small, hooks=None):
    n_rows = x.shape[0]
    sh1, sc1, gt1, sh2, sc2, gt2 = (mod[:, i * D_MODEL:(i + 1) * D_MODEL] for i in range(N_MOD))
    tm = 256
    d_acc = (1, D_MODEL)

    (h1,) = _rowwise(lambda r, c: [_mod_norm(r[0], *c)], [x], [small["g_pre_mix"], sc1, sh1],
                     [(D_MODEL, BF16)], [], tm, "pre_mix_fwd")
    proj = _matmul(h1, wts["w_in"], "nn", F32, 1024, 1408, 2048, "in_proj")

    freqs = ROPE_THETA ** (-jnp.arange(0, ROT_DIM, 2, dtype=F32) / ROT_DIM)
    freq_lane = jnp.tile(freqs, LANES // (ROT_DIM // 2))[None, :]
    tabs = _rope_tables(pos_col, freq_lane, "rope_tables")
    att = [_att_fwd(proj, tabs, gi, f"att_fwd_{gi}") for gi in range(3)]

    expand = jnp.asarray(_expand_np())
    b_re2, b_im2 = small["ssm_b_re"].reshape(SSM_G, -1), small["ssm_b_im"].reshape(SSM_G, -1)
    log_dt = small["ssm_log_dt"].reshape(SSM_G, 1)
    prep_in = (small["ssm_a_re"], small["ssm_a_im"], log_dt, b_re2, b_im2, expand)
    ab_re, ab_im, bb_re, bb_im = _ssm_prep(*prep_in, "ssm_prep")
    ab_re_row, ab_im_row = ab_re.reshape(1, SSM_GN), ab_im.reshape(1, SSM_GN)
    bblk = [_block_diag_in(t.reshape(SSM_G, SSM_N, SSM_P)).astype(BF16) for t in (bb_re, bb_im)]
    cblk = [_block_diag_out(small[k]).astype(BF16) for k in ("ssm_c_re", "ssm_c_im")]
    d_row = small["ssm_d"].reshape(1, SSM_WIDTH)
    if hooks is not None:
        d_row = _tie(d_row, hooks.mixer.arrived(att[2][0]))
    y_tiles, cin_re, cin_im = _ssm_fwd(proj, ab_re_row, ab_im_row, *bblk, *cblk, d_row, "ssm_fwd")
    y_tiles = list(y_tiles)
    n_mix = 6 + SSM_TILES

    def mixers_out(r, c):
        w_glu, b_glu, g_att, g_ssm = c
        att_n = _att_mix(*r[:6], g_att)
        y2 = _gelu(jnp.concatenate(r[6:n_mix], axis=1))
        z = _nn(y2.astype(BF16), w_glu) + b_glu
        return [jnp.concatenate([att_n.astype(BF16), _glu_out(y2, z, g_ssm).astype(BF16)], axis=1)]

    att_rows = [a[0] for a in att] + [a[1] for a in att]
    if hooks is not None:
        wts = {**wts, **hooks.mixer.weights(y_tiles[0])}
    mix_consts = [wts["w_glu"], small["b_glu"], small["g_attn_out"], small["g_ssm_out"]]
    (cat,) = _rowwise(mixers_out, att_rows + y_tiles, mix_consts, [(OUT_IN_WIDTH, BF16)], [], tm, "mixers_out_fwd")
    mix = _matmul(cat, wts["w_out"], "nn", F32, 1024, 1024, 1280, "out_proj")

    post_consts = [small["g_post_mix"], gt1, small["g_pre_mlp"], sc2, sh2]
    if hooks is not None:
        post_consts[0] = _tie(post_consts[0], hooks.mlp.arrived(mix))
    x1, h2 = _rowwise(lambda r, c: list(_post_mix(r[0], r[1], *c)), [x, mix], post_consts,
                      [(D_MODEL, F32), (D_MODEL, BF16)], [], tm, "post_mix_fwd")
    if hooks is not None:
        wts = {**wts, **hooks.mlp.weights(h2)}
    w_mlp_in, w_mlp_out = wts["w_mlp_in"], wts["w_mlp_out"]
    a_mlp, y_mlp = _mlp_fwd(h2, w_mlp_in, w_mlp_out, 1024, 512, "mlp_fwd")

    def loss_head(r, c):
        x1_v, y_v, t_v = r
        g, gt = c
        fn = lambda y_, g_, gt_: gt_ * _rms(y_, g_)
        out, vjp = jax.vjp(fn, y_v, g, gt)
        err = x1_v + out - t_v
        dx2 = err * (1.0 / D_MODEL)
        dy, dg, dgt = vjp(dx2)
        loss = 0.5 * jnp.sum(jnp.sum(err * err, axis=1, keepdims=True), axis=0, keepdims=True) * (1.0 / D_MODEL)
        return [dx2, dy, loss, dg, dgt]

    dx2, dy_mlp, loss, dg_post_mlp, dgt2 = _rowwise(
        loss_head, [x1, y_mlp, target], [small["g_post_mlp"], gt2],
        [(D_MODEL, F32), (D_MODEL, BF16)], [(1, 1), d_acc, d_acc], tm, "loss_head")

    da_mlp, dw_mlp_out, dw_mlp_in = _mlp_bwd(dy_mlp, h2, a_mlp, w_mlp_out, 1024, 512, "mlp_bwd")
    dw_mlp_out = dw_mlp_out.reshape(dw_mlp_in.shape)
    sent = None if hooks is None else hooks.mlp_grads_to_sibling(dw_mlp_in, dw_mlp_out)
    dh2 = _matmul(da_mlp, w_mlp_in, "nt", F32, 1024, 1024, 2048, "mlp_in_bwd", after=sent)
    sent = None if hooks is None else hooks.mlp_grads_to_chips(dh2)

    def post_mix_bwd(r, c):
        x_v, mix_v, dx1_v, dh2_v = r
        _, vjp = jax.vjp(_post_mix, x_v, mix_v, *c)
        return list(vjp((dx1_v, dh2_v)))

    post_consts_bwd = post_consts if sent is None else [_tie(post_consts[0], sent)] + post_consts[1:]
    dx_a, dmix, dg_post_mix, dgt1, dg_pre_mlp, dsc2, dsh2 = _rowwise(
        post_mix_bwd, [x, mix, dx2, dh2], post_consts_bwd, [(D_MODEL, F32), (D_MODEL, BF16)], [d_acc] * 5, tm,
        "post_mix_bwd")

    dcat = _matmul(dmix, wts["w_out"], "nt", F32, 1024, 1280, 2048, "out_proj_bwd")
    dw_out = _matmul(cat, dmix, "tn", F32, 1280, 1024, 1024, "out_proj_wgrad")

    def mixers_out_bwd(r, c):
        w_glu, b_glu, g_att, g_ssm = c
        dcat_v = r[n_mix]
        _, vjp_att = jax.vjp(_att_mix, *r[:6], g_att)
        *d_ol, dg_att = vjp_att(dcat_v[:, :KV_WIDTH])
        y2, vjp_gelu = jax.vjp(_gelu, jnp.concatenate(r[6:n_mix], axis=1))
        y2b = y2.astype(BF16)
        z = _nn(y2b, w_glu) + b_glu
        _, vjp_glu = jax.vjp(_glu_out, y2, z, g_ssm)
        dy2, dz, dg_ssm = vjp_glu(dcat_v[:, KV_WIDTH:])
        dzb = dz.astype(BF16)
        (dy,) = vjp_gelu(dy2 + _nt(dzb, w_glu))
        return d_ol + [dy, dg_att, _tn(y2b, dzb), jnp.sum(dz, axis=0, keepdims=True), dg_ssm]

    *d_att, dy_ssm, dg_attn_out, dw_glu, db_glu, dg_ssm_out = _rowwise(
        mixers_out_bwd, att_rows + y_tiles + [dcat], mix_consts,
        [(KV_WIDTH, F32)] * 6 + [(SSM_WIDTH, F32)],
        [(1, KV_WIDTH), (SSM_WIDTH, SSM_WIDTH), (1, SSM_WIDTH), (1, SSM_WIDTH)], tm, "mixers_out_bwd")

    du_tiles, dab_re, dab_im, dbblk_re, dbblk_im, dcblk_re, dcblk_im, dd_row = _ssm_bwd(
        proj, dy_ssm, cin_re, cin_im, ab_re_row, ab_im_row, *bblk, *cblk, d_row, "ssm_bwd")
    prep_cts = (dab_re.reshape(SSM_G, SSM_N), dab_im.reshape(SSM_G, SSM_N),
                _block_diag_in_grad(dbblk_re).reshape(SSM_G, -1), _block_diag_in_grad(dbblk_im).reshape(SSM_G, -1))
    da_re, da_im, dlog_dt, db_re, db_im = _ssm_prep_bwd(*prep_in, prep_cts, "ssm_prep_bwd")

    dqkv = [_att_bwd(proj, tabs, att[gi][0], att[gi][1], d_att[gi], d_att[3 + gi], gi, f"att_bwd_{gi}")
            for gi in range(3)]

    def gather_dproj(r, c):
        dq = [r[0], r[3], r[6]]
        dk = r[1] + r[4] + r[7]
        dv = r[2] + r[5] + r[8]
        return [jnp.concatenate([t.astype(BF16) for t in dq + [dk, dv] + r[9:]], axis=1)]

    (dproj,) = _rowwise(gather_dproj, [t for g in dqkv for t in g] + list(du_tiles), [], [(IN_WIDTH, BF16)], [], tm,
                        "gather_dproj")
    dh1 = _matmul(dproj, wts["w_in"], "nt", F32, 1024, 1024, 2816, "in_proj_bwd")
    dw_in = _matmul(h1, dproj, "tn", F32, 1024, 1408, 1024, "in_proj_wgrad")

    def pre_mix_bwd(r, c):
        x_v, dh1_v, dxa_v = r
        _, vjp = jax.vjp(_mod_norm, x_v, *c)
        dx, dg, dsc, dsh = vjp(dh1_v)
        return [dx + dxa_v, dg, dsc, dsh]

    grad_x, dg_pre_mix, dsc1, dsh1 = _rowwise(
        pre_mix_bwd, [x, dh1, dx_a], [small["g_pre_mix"], sc1, sh1], [(D_MODEL, F32)], [d_acc] * 3, tm, "pre_mix_bwd")

    dmod = jnp.concatenate([dsh1, dsc1, dgt1, dsh2, dsc2, dgt2], axis=1)
    big = dict(w_in=dw_in, w_out=dw_out, w_mlp_in=dw_mlp_in, w_mlp_out=dw_mlp_out, w_glu=dw_glu)
    small_g = dict(
        g_pre_mix=dg_pre_mix, g_post_mix=dg_post_mix, ssm_a_re=da_re, ssm_a_im=da_im,
        ssm_log_dt=dlog_dt.reshape(1, SSM_G), ssm_b_re=db_re.reshape(SSM_G, SSM_N, SSM_P),
        ssm_b_im=db_im.reshape(SSM_G, SSM_N, SSM_P), ssm_c_re=_block_diag_out_grad(dcblk_re),
        ssm_c_im=_block_diag_out_grad(dcblk_im), ssm_d=dd_row.reshape(SSM_G, SSM_P), b_glu=db_glu,
        g_attn_out=dg_attn_out, g_ssm_out=dg_ssm_out, g_pre_mlp=dg_pre_mlp, g_post_mlp=dg_post_mlp)
    return loss, grad_x, dmod, big, small_g


MESH_ID = pl.DeviceIdType.MESH
N_DEV = 8
N_CHIPS = 4
HBM_SPEC = pl.BlockSpec(memory_space=pltpu.HBM)


def _place():
    x, y, c = lax.axis_index("x"), lax.axis_index("y"), lax.axis_index("c")
    other_chips = [(1 - x, y), (x, 1 - y), (1 - x, 1 - y)]
    return x, y, c, other_chips


def _half_rows(index, half):
    return pl.ds(pl.multiple_of(index * half, ROW_PAD), half)


def _remote(src, dst, send_sem, recv_sem, dev):
    return pltpu.make_async_remote_copy(src_ref=src, dst_ref=dst, send_sem=send_sem, recv_sem=recv_sem,
                                        device_id=dev, device_id_type=MESH_ID)


def _all_gather8(block, name):
    m_per, n = block.shape

    def body(x_ref, out_ref, send_sems, recv_sems, local_sem):
        x, y, c, chips = _place()
        me, sibling = (x, y, c), (x, y, 1 - c)

        def rows(px, py, pc):
            return out_ref.at[pl.ds((4 * px + 2 * py + pc) * m_per, m_per), :]

        def copy(k, blk, to, src=None):
            return _remote(rows(*blk) if src is None else src, rows(*blk), send_sems.at[k], recv_sems.at[k], to)

        mine = pltpu.make_async_copy(x_ref, rows(*me), local_sem)
        mine.start()
        first = [copy(0, me, sibling, src=x_ref)]
        first += [copy(1 + j, me, (*chip, c), src=x_ref) for j, chip in enumerate(chips)]
        for cp in first:
            cp.start()
        passed = [copy(4 + j, (*chip, c), sibling) for j, chip in enumerate(chips)]
        for j, chip in enumerate(chips):
            copy(1 + j, (*chip, c), me).wait_recv()
            passed[j].start()
        copy(0, sibling, me).wait_recv()
        for j, chip in enumerate(chips):
            copy(4 + j, (*chip, 1 - c), me).wait_recv()
        for cp in first + passed:
            cp.wait_send()
        mine.wait()

    return pl.pallas_call(
        body,
        name=name,
        out_shape=jax.ShapeDtypeStruct((N_DEV * m_per, n), block.dtype),
        in_specs=[pl.BlockSpec(memory_space=pltpu.VMEM)],
        out_specs=pl.BlockSpec(memory_space=pltpu.VMEM),
        scratch_shapes=[pltpu.SemaphoreType.DMA((7,)), pltpu.SemaphoreType.DMA((7,)), pltpu.SemaphoreType.DMA],
        compiler_params=_params(),
    )(block)


def _weight_gather(shards, name):
    n = len(shards)
    shapes = [s.shape for s in shards]

    def body(*refs):
        ins, outs = refs[:n], refs[n:2 * n]
        send, recv, fsend, frecv = refs[2 * n:]
        x, y, c, chips = _place()
        k_me = 2 * x + y
        sibling = (x, y, 1 - c)
        pending = []
        for a in range(n):
            half = shapes[a][0] // 2
            mine = _half_rows(c, half)
            for j, chip in enumerate(chips):
                cp = _remote(ins[a].at[mine, :], outs[a].at[k_me, mine, :], send.at[a, j], recv.at[a, j], (*chip, c))
                cp.start()
                pending.append(cp.wait_send)
        for a in range(n):
            half = shapes[a][0] // 2
            for j, (px, py) in enumerate(chips):
                piece = outs[a].at[2 * px + py, _half_rows(c, half), :]
                _remote(piece, piece, send.at[a, j], recv.at[a, j], (px, py, c)).wait_recv()
                fw = _remote(piece, piece, fsend.at[a, j], frecv.at[a, j], sibling)
                fw.start()
                pending.append(fw.wait_send)
        for a in range(n):
            half = shapes[a][0] // 2
            for j, (px, py) in enumerate(chips):
                piece = outs[a].at[2 * px + py, _half_rows(1 - c, half), :]
                _remote(piece, piece, fsend.at[a, j], frecv.at[a, j], sibling).wait_recv()
        for wait in pending:
            wait()

    sems = pltpu.SemaphoreType.DMA((n, 3))
    return pl.pallas_call(
        body,
        name=name,
        out_shape=[jax.ShapeDtypeStruct((N_CHIPS,) + s, BF16) for s in shapes],
        in_specs=[HBM_SPEC] * n,
        out_specs=[HBM_SPEC] * n,
        scratch_shapes=[sems, sems, sems, sems],
        compiler_params=_params(),
    )(*shards)


def _sibling_halves(stacks, name):
    n = len(stacks)
    shapes = [s.shape for s in stacks]

    def body(*refs):
        ins, outs = refs[:n], refs[n:2 * n]
        send, recv = refs[2 * n:]
        x, y, c, _ = _place()
        copies = []
        for a in range(n):
            half = shapes[a][1] // 2
            cp = _remote(ins[a].at[:, _half_rows(1 - c, half), :], outs[a], send.at[a], recv.at[a], (x, y, 1 - c))
            cp.start()
            copies.append(cp)
        for cp in copies:
            cp.wait()

    return pl.pallas_call(
        body,
        name=name,
        out_shape=[jax.ShapeDtypeStruct((N_CHIPS, s[1] // 2, s[2]), F32) for s in shapes],
        in_specs=[HBM_SPEC] * n,
        out_specs=[HBM_SPEC] * n,
        scratch_shapes=[pltpu.SemaphoreType.DMA((n,)), pltpu.SemaphoreType.DMA((n,))],
        compiler_params=_params(),
    )(*stacks)


def _sibling_swap(halves, name):
    n = len(halves)
    shapes = [h.shape for h in halves]

    def body(*refs):
        ins, outs = refs[:n], refs[n:2 * n]
        send, recv = refs[2 * n:]
        x, y, c, _ = _place()
        pending = []
        for a in range(n):
            half = shapes[a][0]
            mine = outs[a].at[_half_rows(c, half), :]
            cp = _remote(ins[a], mine, send.at[a], recv.at[a], (x, y, 1 - c))
            cp.start()
            pending.append(cp.wait_send)
        for a in range(n):
            half = shapes[a][0]
            theirs = outs[a].at[_half_rows(1 - c, half), :]
            _remote(theirs, theirs, send.at[a], recv.at[a], (x, y, 1 - c)).wait_recv()
        for wait in pending:
            wait()

    return pl.pallas_call(
        body,
        name=name,
        out_shape=[jax.ShapeDtypeStruct((2 * s[0], s[1]), F32) for s in shapes],
        in_specs=[HBM_SPEC] * n,
        out_specs=[HBM_SPEC] * n,
        scratch_shapes=[pltpu.SemaphoreType.DMA((n,)), pltpu.SemaphoreType.DMA((n,))],
        compiler_params=_params(),
    )(*halves)


SEM_SPEC = pl.BlockSpec(memory_space=pltpu.SEMAPHORE)
ANY_SPEC = pl.BlockSpec(memory_space=pl.ANY)
DATAFLOW = pltpu.SideEffectType.DATAFLOW_SIDE_EFFECTING


def _split_copy_start(srcs, lands, plan, n_sems, name, after=None):
    bufs = list(srcs) + list(lands)
    ns, nb = len(srcs), len(bufs)
    extra = [] if after is None else [after]

    def body(*refs):
        outs = refs[nb + len(extra):]
        for outgoing, _ in plan(refs[:ns], refs[ns:nb], outs[0], outs[1]):
            outgoing.start()
        outs[-1][...] = jnp.zeros_like(outs[-1])

    sems = pltpu.SemaphoreType.DMA((n_sems,))
    return pl.pallas_call(
        body,
        name=name,
        out_shape=(sems, sems, *[pltpu.HBM(b.shape, b.dtype) for b in bufs], jax.ShapeDtypeStruct((8, LANES), F32)),
        in_specs=[HBM_SPEC] * nb + [ANY_SPEC] * len(extra),
        out_specs=(SEM_SPEC, SEM_SPEC, *[HBM_SPEC] * nb, pl.BlockSpec(memory_space=pltpu.VMEM)),
        input_output_aliases={i: 2 + i for i in range(nb)},
        compiler_params=pltpu.CompilerParams(has_side_effects=DATAFLOW),
    )(*[pltpu.with_memory_space_constraint(b, pltpu.HBM) for b in bufs], *extra)


def _split_copy_wait(started, plan, after, name, n_srcs=None):
    send, recv, *bufs = started[:-1]
    nb = len(bufs)
    ns = nb // 2 if n_srcs is None else n_srcs

    def body(*refs):
        for outgoing, incoming in plan(refs[:ns], refs[ns:nb], refs[nb], refs[nb + 1]):
            outgoing.wait_send()
            incoming.wait_recv()

    return pl.pallas_call(
        body,
        name=name,
        out_shape=tuple(pltpu.HBM(b.shape, b.dtype) for b in bufs),
        in_specs=[HBM_SPEC] * nb + [SEM_SPEC, SEM_SPEC, ANY_SPEC],
        out_specs=tuple([HBM_SPEC] * nb),
        input_output_aliases={i: i for i in range(nb)},
        compiler_params=pltpu.CompilerParams(has_side_effects=DATAFLOW),
    )(*bufs, send, recv, after)


def _weight_plan(shapes):
    def plan(srcs, lands, send, recv):
        x, y, c, chips = _place()
        copies = []
        for a in range(len(shapes)):
            mine = _half_rows(c, shapes[a][0] // 2)
            for j, (px, py) in enumerate(chips):
                s = 3 * a + j
                arrival = lands[a].at[2 * px + py, mine, :]
                copies.append((_remote(srcs[a].at[mine, :], lands[a].at[2 * x + y, mine, :], send.at[s], recv.at[s], (px, py, c)),
                               _remote(arrival, arrival, send.at[s], recv.at[s], (px, py, c))))
        return copies
    return plan


def _halves_plan(shapes):
    def plan(srcs, lands, send, recv):
        x, y, c, _ = _place()
        copies = []
        for a in range(len(shapes)):
            theirs = srcs[a].at[:, _half_rows(1 - c, shapes[a][1] // 2), :]
            copies.append((_remote(theirs, lands[a], send.at[a], recv.at[a], (x, y, 1 - c)),
                           _remote(lands[a], lands[a], send.at[a], recv.at[a], (x, y, 1 - c))))
        return copies
    return plan


def _exchange_plan(n):
    def plan(srcs, lands, send, recv):
        x, y, c, chips = _place()
        copies = []
        for a in range(n):
            for j, (px, py) in enumerate(chips):
                s = 3 * a + j
                copies.append((_remote(srcs[a].at[2 * px + py], lands[a].at[j], send.at[s], recv.at[s], (px, py, c)),
                               _remote(lands[a].at[j], lands[a].at[j], send.at[s], recv.at[s], (px, py, c))))
        return copies
    return plan


def _forward_plan(shapes):
    def plan(stacks, _, send, recv):
        x, y, c, chips = _place()
        copies = []
        for a in range(len(shapes)):
            half = shapes[a][0] // 2
            for j, (px, py) in enumerate(chips):
                s = 3 * a + j
                mine = stacks[a].at[2 * px + py, _half_rows(c, half), :]
                theirs = stacks[a].at[2 * px + py, _half_rows(1 - c, half), :]
                copies.append((_remote(mine, mine, send.at[s], recv.at[s], (x, y, 1 - c)),
                               _remote(theirs, theirs, send.at[s], recv.at[s], (x, y, 1 - c))))
        return copies
    return plan


def _tie(x, token):
    return x + token[0:1, 0:1].astype(x.dtype)


ROW_PAD = 16


def _silu(x):
    return x * _sigmoid(x)


def _ada_fwd(c_all, w_ada, b_ada, name):
    dm, cols = w_ada.shape
    tn = 512

    def body(c_ref, w_ref, b_ref, o_ref):
        o_ref[...] = _nn(_silu(c_ref[...]).astype(BF16), w_ref[...].astype(BF16)) + b_ref[...]

    return pl.pallas_call(
        body,
        name=name,
        grid=(cols // tn,),
        in_specs=[pl.BlockSpec((ROW_PAD, dm), lambda j: (0, 0)), pl.BlockSpec((dm, tn), lambda j: (0, j)),
                  pl.BlockSpec((1, tn), lambda j: (0, j))],
        out_specs=pl.BlockSpec((ROW_PAD, tn), lambda j: (0, j)),
        out_shape=jax.ShapeDtypeStruct((ROW_PAD, cols), F32),
        compiler_params=_params(("parallel",)),
    )(c_all, w_ada, b_ada)


def _adamw(w, g, m, v):
    m = ADAM_B1 * m + (1.0 - ADAM_B1) * g
    v = ADAM_B2 * v + (1.0 - ADAM_B2) * (g * g)
    m_hat = m / (1.0 - ADAM_B1 ** ADAM_STEP)
    v_hat = v / (1.0 - ADAM_B2 ** ADAM_STEP)
    delta = -ADAM_LR * (m_hat / (jnp.sqrt(v_hat) + ADAM_EPS) + ADAM_WD * w)
    return delta, m, v


def _ada_bwd_adamw(c_all, dmod_cols, w, m, v, name):
    dm, cols = w.shape
    tm, tn = 512, 512

    def body(c_ref, d_ref, w_ref, m_ref, v_ref, g_ref, dl_ref, nm_ref, nv_ref):
        g = _tn(_silu(c_ref[...]).astype(BF16), d_ref[...].astype(BF16))
        g_ref[...] = g
        dl_ref[...], nm_ref[...], nv_ref[...] = _adamw(w_ref[...], g, m_ref[...], v_ref[...])

    tile = pl.BlockSpec((tm, tn), lambda i, j: (i, j))
    shape = jax.ShapeDtypeStruct((dm, cols), F32)
    return pl.pallas_call(
        body,
        name=name,
        grid=(dm // tm, cols // tn),
        in_specs=[pl.BlockSpec((ROW_PAD, tm), lambda i, j: (0, i)), pl.BlockSpec((ROW_PAD, tn), lambda i, j: (0, j)),
                  tile, tile, tile],
        out_specs=[tile] * 4,
        out_shape=[shape] * 4,
        compiler_params=_params(("parallel", "parallel")),
    )(c_all, dmod_cols, w, m, v)


def _sum_blocks(parts, nblk, name):
    rows, cols = parts.shape[0] // nblk, parts.shape[1]

    def body(p_ref, o_ref):
        tot = p_ref[0:rows, :]
        for b in range(1, nblk):
            tot = tot + p_ref[b * rows:(b + 1) * rows, :]
        o_ref[...] = tot

    return pl.pallas_call(body, name=name, out_shape=jax.ShapeDtypeStruct((rows, cols), F32), compiler_params=_params())(parts)


def _adamw_rows(w, g, m, v, tm, name):
    return _rowwise(lambda r, c: list(_adamw(*r)), [w, g, m, v], [], [(w.shape[1], F32)] * 3, [], tm, name)


BIG = ("w_in", "w_out", "w_mlp_in", "w_mlp_out", "w_glu")
COL_SHARDED = ("w_in", "w_out", "w_mlp_in")
SMALL = ("b_ada", "g_pre_mix", "g_post_mix", "ssm_a_re", "ssm_a_im", "ssm_log_dt", "ssm_b_re", "ssm_b_im",
         "ssm_c_re", "ssm_c_im", "ssm_d", "b_glu", "g_attn_out", "g_ssm_out", "g_pre_mlp", "g_post_mlp")
WEIGHTS = ("w_ada", "b_ada", "g_pre_mix", "g_post_mix", "w_in", "ssm_a_re", "ssm_a_im", "ssm_log_dt", "ssm_b_re",
           "ssm_b_im", "ssm_c_re", "ssm_c_im", "ssm_d", "w_glu", "b_glu", "g_attn_out", "g_ssm_out", "w_out",
           "g_pre_mlp", "g_post_mlp", "w_mlp_in", "w_mlp_out")
FLAT_COLS = 1024
FLAT_ROWS = 256
ROW_TILE = {"w_in": 256, "w_out": 128, "w_mlp_in": 256, "w_mlp_out": 256, "w_glu": 112}


def _flatten_small(tree):
    flat = jnp.concatenate([tree[k].reshape(-1) for k in SMALL])
    return jnp.pad(flat, (0, FLAT_ROWS * FLAT_COLS - flat.shape[0])).reshape(FLAT_ROWS, FLAT_COLS)


def _unflatten_small(flat, like):
    flat = flat.reshape(-1)
    out, at = {}, 0
    for k in SMALL:
        size = math.prod(like[k].shape)
        out[k] = flat[at:at + size].reshape(like[k].shape)
        at += size
    return out


def _unstack(stack, name):
    if name in COL_SHARDED:
        return stack.transpose(1, 0, 2).reshape(stack.shape[1], N_CHIPS * stack.shape[2])
    return stack.reshape(N_CHIPS * stack.shape[1], stack.shape[2])


def _stack(full, name):
    if name in COL_SHARDED:
        return full.reshape(full.shape[0], N_CHIPS, full.shape[1] // N_CHIPS).transpose(1, 0, 2)
    return full.reshape(N_CHIPS, full.shape[0] // N_CHIPS, full.shape[1])


EARLY = ("w_in", "w_out", "w_glu")
LATE = ("w_mlp_in", "w_mlp_out")
MIXER_W = ("w_out", "w_glu")


def _chip_sums(names, g_stacks, from_sibling, ic, chip):
    own, to_send = [], []
    place = jnp.stack([ic, chip]).astype(jnp.int32)
    for k, gs, fs in zip(names, g_stacks, from_sibling):
        _, rows, cols = gs.shape
        half, tm = rows // 2, ROW_TILE[k]
        nt = half // tm

        def body(place_ref, g_ref, f_ref, own_ref, send_ref):
            s = g_ref[...] + f_ref[...]
            send_ref[...] = s.astype(BF16)

            @pl.when(pl.program_id(1) == place_ref[1])
            def _():
                own_ref[...] = s

        slab = lambda index: pl.BlockSpec((None, tm, cols), index)
        mine, to_chips = pl.pallas_call(
            body,
            name="grad_chip_sum_" + k,
            grid_spec=pltpu.PrefetchScalarGridSpec(
                num_scalar_prefetch=1,
                grid=(nt, N_CHIPS),
                in_specs=[slab(lambda i, kk, p, nt=nt: (kk, p[0] * nt + i, 0)), slab(lambda i, kk, p: (kk, i, 0))],
                out_specs=[pl.BlockSpec((tm, cols), lambda i, kk, p: (i, 0)), slab(lambda i, kk, p: (kk, i, 0))]),
            out_shape=[jax.ShapeDtypeStruct((half, cols), F32), jax.ShapeDtypeStruct((N_CHIPS, half, cols), BF16)],
            compiler_params=_params(("arbitrary", "arbitrary")),
        )(place, gs, fs)
        own.append(mine)
        to_send.append(to_chips)
    return own, to_send


def _grad_totals(names, own, from_chips):
    totals = []
    for k, mine, fc in zip(names, own, from_chips):
        half, cols = mine.shape
        tm = ROW_TILE[k]

        def body(m_ref, a_ref, b_ref, c_ref, o_ref):
            o_ref[...] = m_ref[...] + a_ref[...].astype(F32) + b_ref[...].astype(F32) + c_ref[...].astype(F32)

        rows = pl.BlockSpec((tm, cols), lambda i: (i, 0))
        totals.append(pl.pallas_call(
            body,
            name="grad_total_" + k,
            grid=(half // tm,),
            in_specs=[rows] + [pl.BlockSpec((None, tm, cols), lambda i, j=j: (j, i, 0)) for j in range(3)],
            out_specs=rows,
            out_shape=jax.ShapeDtypeStruct((half, cols), F32),
            compiler_params=_params(("parallel",)),
        )(mine, fc, fc, fc))
    return totals


class _LateWeights:
    def __init__(self, names, own_shards, chip, after, tag):
        self.names, self.chip, self.tag = names, chip, tag
        self.shapes = [o.shape for o in own_shards]
        lands = [lax.empty((N_CHIPS,) + s, BF16) for s in self.shapes]
        self.gather = _split_copy_start(own_shards, lands, _weight_plan(self.shapes), 3 * len(names),
                                        tag + "_gather_start", after=after)
        self.token = self.gather[-1]

    def arrived(self, after):
        n = len(self.names)
        done = _split_copy_wait(self.gather, _weight_plan(self.shapes), after, self.tag + "_gather_wait")
        self.own_shards = done[:n]
        self.forward = _split_copy_start(done[n:], [], _forward_plan(self.shapes), 3 * n, self.tag + "_forward_start")
        return self.forward[-1]

    def weights(self, after):
        n = len(self.names)
        stacks = _split_copy_wait(self.forward, _forward_plan(self.shapes), after, self.tag + "_forward_wait", n_srcs=n)
        stacks = [lax.dynamic_update_index_in_dim(s, o, self.chip, 0) for s, o in zip(stacks, self.own_shards)]
        return {k: _unstack(s, k) for k, s in zip(self.names, stacks)}


class _Overlap:
    def __init__(self, own, ic, chip, after):
        self.ic, self.chip = ic, chip
        self.mixer = _LateWeights(MIXER_W, [own[k] for k in MIXER_W], chip, after, "mixer_weight")
        self.mlp = _LateWeights(LATE, [own[k] for k in LATE], chip, self.mixer.token, "mlp_weight")
        self.token = self.mlp.token

    def mlp_grads_to_sibling(self, dw_in, dw_out):
        stacks = [dw_in, dw_out]
        self.g_shapes = [s.shape for s in stacks]
        lands = [lax.empty((N_CHIPS, s[1] // 2, s[2]), F32) for s in self.g_shapes]
        self.halves = _split_copy_start(stacks, lands, _halves_plan(self.g_shapes), len(LATE), "mlp_grad_halves_start")
        return self.halves[-1]

    def mlp_grads_to_chips(self, after):
        n = len(LATE)
        done = _split_copy_wait(self.halves, _halves_plan(self.g_shapes), after, "mlp_grad_halves_wait")
        self.own, to_send = _chip_sums(LATE, done[:n], done[n:], self.ic, self.chip)
        lands = [lax.empty((3,) + s.shape[1:], BF16) for s in to_send]
        self.exchange = _split_copy_start(to_send, lands, _exchange_plan(n), 3 * n, "mlp_grad_exchange_start")
        return self.exchange[-1]

    def mlp_grads_reduced(self, after):
        n = len(LATE)
        done = _split_copy_wait(self.exchange, _exchange_plan(n), after, "mlp_grad_exchange_wait")
        return _grad_totals(LATE, self.own, done[n:])


def _pad_rows(row):
    return jnp.pad(row, ((0, 8 - row.shape[0]), (0, 0)))


def _every_eighth(gathered):
    rows = gathered.reshape(N_DEV, 8, gathered.shape[1])[:, 0, :]
    return jnp.pad(rows, ((0, ROW_PAD - N_DEV), (0, 0)))


def kernel(x, c, positions, w_ada, b_ada, g_pre_mix, g_post_mix, w_in, ssm_a_re, ssm_a_im, ssm_log_dt, ssm_b_re, ssm_b_im, ssm_c_re, ssm_c_im, ssm_d, w_glu, b_glu, g_attn_out, g_ssm_out, w_out, g_pre_mlp, g_post_mlp, w_mlp_in, w_mlp_out, loss_target, m_w_ada, m_b_ada, m_g_pre_mix, m_g_post_mix, m_w_in, m_ssm_a_re, m_ssm_a_im, m_ssm_log_dt, m_ssm_b_re, m_ssm_b_im, m_ssm_c_re, m_ssm_c_im, m_ssm_d, m_w_glu, m_b_glu, m_g_attn_out, m_g_ssm_out, m_w_out, m_g_pre_mlp, m_g_post_mlp, m_w_mlp_in, m_w_mlp_out, v_w_ada, v_b_ada, v_g_pre_mix, v_g_post_mix, v_w_in, v_ssm_a_re, v_ssm_a_im, v_ssm_log_dt, v_ssm_b_re, v_ssm_b_im, v_ssm_c_re, v_ssm_c_im, v_ssm_d, v_w_glu, v_b_glu, v_g_attn_out, v_g_ssm_out, v_w_out, v_g_pre_mlp, v_g_post_mlp, v_w_mlp_in, v_w_mlp_out):
    given = dict(locals())
    w = {k: given[k][0] for k in WEIGHTS}
    mom = {k: given["m_" + k][0] for k in WEIGHTS}
    var = {k: given["v_" + k][0] for k in WEIGHTS}
    for tree in (w, mom, var):
        for k in ("b_ada", "g_pre_mix", "g_post_mix", "ssm_log_dt", "b_glu", "g_attn_out", "g_ssm_out", "g_pre_mlp",
                  "g_post_mlp"):
            tree[k] = tree[k].reshape(1, -1)
    ix, iy, ic = lax.axis_index("x"), lax.axis_index("y"), lax.axis_index("c")
    chip = 2 * ix + iy
    me = 4 * ix + 2 * iy + ic
    shard_cols = w["w_ada"].shape[1]

    c_all = _every_eighth(_all_gather8(_pad_rows(c), "gather_c"))
    b_ada_cols = lax.dynamic_slice_in_dim(w["b_ada"], chip * shard_cols, shard_cols, axis=1)
    mod_cols = _ada_fwd(c_all, w["w_ada"], b_ada_cols, "ada_fwd")[:N_DEV]
    mod_all = _all_gather8(mod_cols, "gather_mod").reshape(N_CHIPS, 2, N_DEV, shard_cols)[:, 0]
    mod = lax.dynamic_index_in_dim(mod_all, me, axis=1, keepdims=False).reshape(1, N_MOD * D_MODEL)

    w_in_own = w["w_in"].astype(BF16)
    (w_in_stack,) = _weight_gather([w_in_own], "weight_gather")
    w_in_stack = lax.dynamic_update_index_in_dim(w_in_stack, w_in_own, chip, 0)
    wts = {"w_in": _unstack(w_in_stack, "w_in")}
    overlap = _Overlap({k: w[k].astype(BF16) for k in MIXER_W + LATE}, ic, chip, after=w_in_stack)
    mod = _tie(mod, overlap.token)

    small = {k: w[k] for k in SMALL if k != "b_ada"}
    loss, grad_x, dmod, big_g, small_g = _local_step(x[0], positions.reshape(-1, 1), mod, loss_target[0], wts, small,
                                                     hooks=overlap)
    loss = lax.psum(loss[0, 0], ("x", "y", "c"))

    small_g["b_ada"] = dmod
    parts = _all_gather8(_flatten_small(small_g), "gather_small_grads")

    g_stacks = [_stack(big_g[k], k) for k in EARLY]
    from_sibling = _sibling_halves(g_stacks, "grad_sibling_halves")
    chip_f32, chip_bf16 = _chip_sums(EARLY, g_stacks, from_sibling, ic, chip)
    exchange_plan = _exchange_plan(len(EARLY))
    lands = [lax.empty((3,) + s.shape[1:], BF16) for s in chip_bf16]
    exchange = _split_copy_start(chip_bf16, lands, exchange_plan, 3 * len(EARLY), "grad_exchange_start", after=parts)

    small_flat = _sum_blocks(parts, N_DEV, "small_grad_sum")
    grads = _unflatten_small(small_flat, w)

    mod_rows = N_MOD * D_MODEL // FLAT_COLS
    dmod_all = parts.reshape(N_DEV, FLAT_ROWS, FLAT_COLS)[:, :mod_rows].reshape(N_DEV, N_MOD * D_MODEL)
    dmod_all = jnp.pad(dmod_all, ((0, ROW_PAD - N_DEV), (0, 0)))
    dmod_cols = lax.dynamic_slice_in_dim(dmod_all, chip * shard_cols, shard_cols, axis=1)
    g_ada, d_ada, m_ada, v_ada = _ada_bwd_adamw(_tie(c_all, exchange[-1]), dmod_cols, w["w_ada"], mom["w_ada"],
                                                var["w_ada"], "ada_bwd_adamw")
    grads["w_ada"] = g_ada
    delta, new_m, new_v = {"w_ada": d_ada}, {"w_ada": m_ada}, {"w_ada": v_ada}

    def finish(names, reduced, tag):
        swapped = _sibling_swap(reduced, tag)
        for k, s, r in zip(names, swapped, reduced):
            grads[k] = lax.dynamic_update_slice_in_dim(s, r, ic * r.shape[0], axis=0)
            delta[k], new_m[k], new_v[k] = _adamw_rows(w[k], grads[k], mom[k], var[k], ROW_TILE[k], "adamw_" + k)

    finish(LATE, overlap.mlp_grads_reduced(g_ada), "mlp_grad_sibling_swap")
    from_chips = _split_copy_wait(exchange, exchange_plan, new_v[LATE[-1]], "grad_exchange_wait")[len(EARLY):]
    finish(EARLY, _grad_totals(EARLY, chip_f32, from_chips), "grad_sibling_swap")

    flat_upd = _adamw_rows(_flatten_small(w), small_flat, _flatten_small(mom), _flatten_small(var), FLAT_ROWS,
                           "adamw_small")
    for tree, flat in zip((delta, new_m, new_v), flat_upd):
        tree.update(_unflatten_small(flat, w))

    shaped = lambda tree: [tree[k].reshape(given[k].shape) for k in WEIGHTS]
    return (loss, grad_x[None], *shaped(grads), *shaped(delta), *shaped(new_m), *shaped(new_v))
```

```python
import functools
import math

import jax
import jax.numpy as jnp
import numpy as np
from jax import lax
from jax.experimental import pallas as pl
from jax.experimental.pallas import tpu as pltpu

F32 = jnp.float32
BF16 = jnp.bfloat16

D_MODEL = 2048
HEAD_DIM = 64
DILATIONS = (1, 4, 16)
ATT_SPAN = 128
ATT_BLK = 128
HEADS_PER_GROUP = 6
KV_WIDTH = HEADS_PER_GROUP * HEAD_DIM
ATT_Q_WIDTH = 3 * KV_WIDTH
ROT_DIM = 16
ROPE_THETA = 500000.0
SSM_WIDTH = 896
SSM_P = 16
SSM_G = 56
SSM_N = 64
SSM_GN = SSM_G * SSM_N
SSM_TILES = SSM_WIDTH // 128
SSM_TILE_GN = 8 * SSM_N
IN_WIDTH = 2816
OUT_IN_WIDTH = 1280
D_FF = 8192
N_MOD = 6
EPS = 1e-6
LANES = 128
SSM_SEGS = 8
SSM_CHUNK = 256
SSM_SEG_LEN = SSM_CHUNK // SSM_SEGS

ADAM_LR = 0.001
ADAM_B1 = 0.9
ADAM_B2 = 0.999
ADAM_EPS = 1e-08
ADAM_WD = 0.01
ADAM_STEP = 10

VMEM_LIMIT = 56 * 1024 * 1024


def _params(sem=None):
    return pltpu.CompilerParams(dimension_semantics=sem, vmem_limit_bytes=VMEM_LIMIT)


def _dot(a, b, dims):
    return lax.dot_general(a, b, (dims, ((), ())), preferred_element_type=F32)


def _nn(a, b):
    return _dot(a, b, ((1,), (0,)))


def _nt(a, b):
    return _dot(a, b, ((1,), (1,)))


def _tn(a, b):
    return _dot(a, b, ((0,), (0,)))


def _matmul(a, b, mode, out_dtype, tm, tn, tk, name, after=None):
    if mode == "nn":
        (m, k), (_, n) = a.shape, b.shape
        a_spec = pl.BlockSpec((tm, tk), lambda i, j, kk: (i, kk))
        b_spec = pl.BlockSpec((tk, tn), lambda i, j, kk: (kk, j))
        op = _nn
    elif mode == "nt":
        (m, k), (n, _) = a.shape, b.shape
        a_spec = pl.BlockSpec((tm, tk), lambda i, j, kk: (i, kk))
        b_spec = pl.BlockSpec((tn, tk), lambda i, j, kk: (j, kk))
        op = _nt
    else:
        (k, m), (_, n) = a.shape, b.shape
        a_spec = pl.BlockSpec((tk, tm), lambda i, j, kk: (kk, i))
        b_spec = pl.BlockSpec((tk, tn), lambda i, j, kk: (kk, j))
        op = _tn
    assert m % tm == 0 and n % tn == 0 and k % tk == 0, (name, m, n, k)
    nk = k // tk

    def body(a_ref, b_ref, *rest):
        o_ref, acc_ref = rest[-2:]
        kk = pl.program_id(2)

        @pl.when(kk == 0)
        def _():
            acc_ref[...] = jnp.zeros_like(acc_ref)

        acc_ref[...] += op(a_ref[...], b_ref[...])

        @pl.when(kk == nk - 1)
        def _():
            o_ref[...] = acc_ref[...].astype(o_ref.dtype)

    extra = [] if after is None else [after]
    return pl.pallas_call(
        body,
        name=name,
        grid=(m // tm, n // tn, nk),
        in_specs=[a_spec, b_spec] + [pl.BlockSpec(t.shape, lambda i, j, kk: (0, 0)) for t in extra],
        out_specs=pl.BlockSpec((tm, tn), lambda i, j, kk: (i, j)),
        out_shape=jax.ShapeDtypeStruct((m, n), out_dtype),
        scratch_shapes=[pltpu.VMEM((tm, tn), F32)],
        compiler_params=_params(("parallel", "parallel", "arbitrary")),
    )(a, b, *extra)


def _rowwise(fn, rows, consts, out_rows, out_accs, tm, name):
    n_rows = rows[0].shape[0]
    assert n_rows % tm == 0
    nr, nc, no = len(rows), len(consts), len(out_rows)

    def body(*refs):
        r_in, c_in = refs[:nr], refs[nr:nr + nc]
        o_row, o_acc = refs[nr + nc:nr + nc + no], refs[nr + nc + no:]
        outs = fn([r[...] for r in r_in], [c[...] for c in c_in])
        assert len(outs) == len(o_row) + len(o_acc), name
        for ref, v in zip(o_row, outs[:no]):
            ref[...] = v.astype(ref.dtype)
        first = pl.program_id(0) == 0
        for ref, v in zip(o_acc, outs[no:]):
            @pl.when(first)
            def _(ref=ref, v=v):
                ref[...] = v.astype(F32)

            @pl.when(jnp.logical_not(first))
            def _(ref=ref, v=v):
                ref[...] += v.astype(F32)

    in_specs = [pl.BlockSpec((tm, r.shape[1]), lambda i: (i, 0)) for r in rows]
    in_specs += [pl.BlockSpec(c.shape, lambda i: (0, 0)) for c in consts]
    out_specs = [pl.BlockSpec((tm, w), lambda i: (i, 0)) for w, _ in out_rows]
    out_specs += [pl.BlockSpec(s, lambda i: (0, 0)) for s in out_accs]
    out_shape = [jax.ShapeDtypeStruct((n_rows, w), dt) for w, dt in out_rows]
    out_shape += [jax.ShapeDtypeStruct(s, F32) for s in out_accs]
    return pl.pallas_call(
        body,
        name=name,
        grid=(n_rows // tm,),
        in_specs=in_specs,
        out_specs=out_specs,
        out_shape=out_shape,
        compiler_params=_params(("arbitrary",)),
    )(*rows, *consts)


def _rms(x, g):
    return x * lax.rsqrt(jnp.mean(x * x, axis=-1, keepdims=True) + EPS) * g


def _mod_norm(x, g, sc, sh):
    return _rms(x, g) * (1.0 + sc) + sh


def _gelu(x):
    return 0.5 * x * (1.0 + jnp.tanh(math.sqrt(2.0 / math.pi) * (x + 0.044715 * (x * x * x))))


def _sigmoid(x):
    return 1.0 / (1.0 + jnp.exp(-x))


def _post_mix(x, mix, g_post, gt1, g_pre, sc2, sh2):
    x1 = x + gt1 * _rms(mix, g_post)
    return x1, _mod_norm(x1, g_pre, sc2, sh2)


def _att_mix(o0, o1, o2, l0, l1, l2, g):
    m = jnp.maximum(jnp.maximum(l0, l1), l2)
    e0, e1, e2 = jnp.exp(l0 - m), jnp.exp(l1 - m), jnp.exp(l2 - m)
    att = (e0 * o0 + e1 * o1 + e2 * o2) / (e0 + e1 + e2)
    return _rms(att, g)


def _glu_out(y2, z, g):
    return _rms(y2 * _sigmoid(z), g)


def _rope_tables(pos_col, freq_lane, name):
    n_rows = pos_col.shape[0]
    tm = 512

    def body(p_ref, f_ref, cos_ref, lo_ref, hi_ref):
        ang = p_ref[...].astype(F32) * f_ref[...]
        lane = lax.broadcasted_iota(jnp.int32, ang.shape, 1) % HEAD_DIM
        c, s = jnp.cos(ang), jnp.sin(ang)
        cos_ref[...] = jnp.where(lane < ROT_DIM, c, 1.0)
        lo_ref[...] = jnp.where(lane < ROT_DIM // 2, -s, 0.0)
        hi_ref[...] = jnp.where((lane >= ROT_DIM // 2) & (lane < ROT_DIM), s, 0.0)

    tab = jax.ShapeDtypeStruct((n_rows, LANES), F32)
    return pl.pallas_call(
        body,
        name=name,
        grid=(n_rows // tm,),
        in_specs=[pl.BlockSpec((tm, 1), lambda i: (i, 0)), pl.BlockSpec((1, LANES), lambda i: (0, 0))],
        out_specs=[pl.BlockSpec((tm, LANES), lambda i: (i, 0))] * 3,
        out_shape=[tab] * 3,
        compiler_params=_params(("parallel",)),
    )(pos_col, freq_lane)


def _rope(x, cos_t, lo_t, hi_t):
    half = ROT_DIM // 2
    return x * cos_t + pltpu.roll(x, LANES - half, 1) * lo_t + pltpu.roll(x, half, 1) * hi_t


def _rope_transposed(dy, cos_t, lo_t, hi_t):
    half = ROT_DIM // 2
    return dy * cos_t + pltpu.roll(dy * lo_t, half, 1) + pltpu.roll(dy * hi_t, LANES - half, 1)


def _att_masks(i, k0):
    q_pos = i * ATT_BLK + lax.broadcasted_iota(jnp.int32, (ATT_BLK, 2 * ATT_BLK), 0)
    k_pos = k0 + lax.broadcasted_iota(jnp.int32, (ATT_BLK, 2 * ATT_BLK), 1)
    dist = q_pos - k_pos
    return (dist >= 0) & (dist <= ATT_SPAN)


def _head_lane_masks():
    lane = lax.broadcasted_iota(jnp.int32, (1, LANES), 1)
    return lane < HEAD_DIM, lane >= HEAD_DIM


def _att_specs(gi, n_rows):
    col = lambda at: pl.BlockSpec((n_rows, LANES), lambda hp: (0, at + hp))
    qkv = [col(gi * 3), col(9), col(12)]
    tabs = [pl.BlockSpec((n_rows, LANES), lambda hp: (0, 0), pipeline_mode=pl.Buffered(1))] * 3
    head_in = col(0) if DILATIONS[gi] > 1 else pl.BlockSpec((n_rows, LANES), lambda hp: (0, hp),
                                                            pipeline_mode=pl.Buffered(1))
    return qkv, tabs, head_in, col(0)


def _sub_rows(d, n, r):
    return pl.ds(r, n, stride=d) if d > 1 else pl.ds(0, n)


def _att_load(q_ref, k_ref, v_ref, tabs, sub, qs, ks, vs):
    cos_t, lo_t, hi_t = tabs
    qs[...] = (_rope(q_ref[sub, :], cos_t, lo_t, hi_t) * (1.0 / math.sqrt(HEAD_DIM))).astype(BF16)
    ks[...] = _rope(k_ref[sub, :], cos_t, lo_t, hi_t).astype(BF16)
    vs[...] = v_ref[sub, :].astype(BF16)


def _att_fwd(proj, tabs, gi, name):
    n_rows = proj.shape[0]
    d = DILATIONS[gi]
    n = n_rows // d
    nb = n // ATT_BLK

    def body(q_ref, k_ref, v_ref, cos_ref, lo_ref, hi_ref, o_ref, l_ref, qs, ks, vs, o_s, l_s):
        m0, m1 = _head_lane_masks()

        def step(i, carry):
            k0 = pl.multiple_of(jnp.maximum(i - 1, 0) * ATT_BLK, ATT_BLK)
            q0 = pl.multiple_of(i * ATT_BLK, ATT_BLK)
            q = qs[pl.ds(q0, ATT_BLK), :]
            k = ks[pl.ds(k0, 2 * ATT_BLK), :]
            v = vs[pl.ds(k0, 2 * ATT_BLK), :]
            valid = _att_masks(i, k0)
            outs, lses = [], []
            for hm in (m0, m1):
                s = _nt(jnp.where(hm, q, jnp.zeros_like(q)), k)
                s = jnp.where(valid, s, -1e30)
                mx = jnp.max(s, axis=1, keepdims=True)
                p = jnp.exp(s - mx)
                den = jnp.sum(p, axis=1, keepdims=True)
                outs.append(_nn(p.astype(BF16), v) / den)
                lses.append(mx + jnp.log(den))
            o_s[pl.ds(q0, ATT_BLK), :] = jnp.where(m0, outs[0], outs[1])
            l_s[pl.ds(q0, ATT_BLK), :] = jnp.where(m0, lses[0], lses[1])
            return carry

        for r in range(d):
            sub = _sub_rows(d, n, r)
            _att_load(q_ref, k_ref, v_ref, (cos_ref[sub, :], lo_ref[sub, :], hi_ref[sub, :]), sub, qs, ks, vs)
            lax.fori_loop(0, nb, step, 0, unroll=2)
            o_ref[sub, :] = o_s[...]
            l_ref[sub, :] = l_s[...]

    qkv, tab_specs, _, head_out = _att_specs(gi, n_rows)
    out = jax.ShapeDtypeStruct((n_rows, KV_WIDTH), F32)
    return pl.pallas_call(
        body,
        name=name,
        grid=(3,),
        in_specs=qkv + tab_specs,
        out_specs=[head_out, head_out],
        out_shape=[out, out],
        scratch_shapes=[pltpu.VMEM((n, LANES), BF16)] * 3 + [pltpu.VMEM((n, LANES), F32)] * 2,
        compiler_params=_params(("parallel",)),
    )(proj, proj, proj, *tabs)


def _att_bwd(proj, tabs, o, l, do, dl, gi, name):
    n_rows = proj.shape[0]
    d = DILATIONS[gi]
    n = n_rows // d
    nb = n // ATT_BLK

    def body(q_ref, k_ref, v_ref, cos_ref, lo_ref, hi_ref, o_ref, l_ref, do_ref, dl_ref,
             dq_ref, dk_ref, dv_ref, qs, ks, vs, dq_s, dk_acc, dv_acc, *gathered):
        m0, m1 = _head_lane_masks()
        o_s, l_s, do_s, dl_s = gathered if d > 1 else (o_ref, l_ref, do_ref, dl_ref)

        def step(i, carry):
            k0 = pl.multiple_of(jnp.maximum(i - 1, 0) * ATT_BLK, ATT_BLK)
            q0 = pl.multiple_of(i * ATT_BLK, ATT_BLK)
            rows = pl.ds(q0, ATT_BLK)
            keys = pl.ds(k0, 2 * ATT_BLK)
            q, k, v = qs[rows, :], ks[keys, :], vs[keys, :]
            d_o, lse = do_s[rows, :], l_s[rows, :]
            o_do = o_s[rows, :] * d_o
            d_l = dl_s[rows, :]
            valid = _att_masks(i, k0)
            dq = jnp.zeros((ATT_BLK, LANES), F32)
            dk = jnp.zeros((2 * ATT_BLK, LANES), F32)
            dv = jnp.zeros((2 * ATT_BLK, LANES), F32)
            for hm in (m0, m1):
                qh, kh = jnp.where(hm, q, jnp.zeros_like(q)), jnp.where(hm, k, jnp.zeros_like(k))
                doh = jnp.where(hm, d_o, 0.0).astype(BF16)
                lse_h = jnp.max(jnp.where(hm, lse, -1e30), axis=1, keepdims=True)
                delta = jnp.sum(jnp.where(hm, o_do, 0.0), axis=1, keepdims=True)
                dlse = jnp.sum(jnp.where(hm, d_l, 0.0), axis=1, keepdims=True)
                s = jnp.where(valid, _nt(qh, k), -1e30)
                p = jnp.exp(s - lse_h)
                dv = dv + _tn(p.astype(BF16), doh)
                ds = (p * (_nt(doh, v) - delta + dlse)).astype(BF16)
                dq = dq + _nn(ds, kh)
                dk = dk + _tn(ds, qh)
            dq_s[rows, :] = dq * (1.0 / math.sqrt(HEAD_DIM))
            dk_acc[keys, :] += dk
            dv_acc[keys, :] += dv
            return carry

        for r in range(d):
            sub = _sub_rows(d, n, r)
            rot = (cos_ref[sub, :], lo_ref[sub, :], hi_ref[sub, :])
            _att_load(q_ref, k_ref, v_ref, rot, sub, qs, ks, vs)
            if d > 1:
                for dst, src in zip(gathered, (o_ref, l_ref, do_ref, dl_ref)):
                    dst[...] = src[sub, :]
            dk_acc[...] = jnp.zeros_like(dk_acc)
            dv_acc[...] = jnp.zeros_like(dv_acc)
            lax.fori_loop(0, nb, step, 0, unroll=2)
            dq_ref[sub, :] = _rope_transposed(dq_s[...], *rot)
            dk_ref[sub, :] = _rope_transposed(dk_acc[...], *rot)
            dv_ref[sub, :] = dv_acc[...]

    qkv, tab_specs, head_in, head_out = _att_specs(gi, n_rows)
    out = jax.ShapeDtypeStruct((n_rows, KV_WIDTH), F32)
    sub_f32 = pltpu.VMEM((n, LANES), F32)
    return pl.pallas_call(
        body,
        name=name,
        grid=(3,),
        in_specs=qkv + tab_specs + [head_in] * 4,
        out_specs=[head_out] * 3,
        out_shape=[out] * 3,
        scratch_shapes=[pltpu.VMEM((n, LANES), BF16)] * 3 + [sub_f32] * (3 if d == 1 else 7),
        compiler_params=_params(("parallel",)),
    )(proj, proj, proj, *tabs, o, l, do, dl)


def _expand_np():
    e = np.zeros((SSM_N, SSM_N * SSM_P), np.float32)
    for nn in range(SSM_N):
        e[nn, nn * SSM_P:(nn + 1) * SSM_P] = 1.0
    return e


def _ssm_prep_math(a_re, a_im, log_dt, b_re, b_im, expand):
    dt = jnp.exp(log_dt)
    mag = jnp.exp(a_re * dt)
    ab_re, ab_im = mag * jnp.cos(a_im * dt), mag * jnp.sin(a_im * dt)
    den = a_re * a_re + a_im * a_im
    num_re, num_im = ab_re - 1.0, ab_im
    co_re = (num_re * a_re + num_im * a_im) / den
    co_im = (num_im * a_re - num_re * a_im) / den
    hi = lax.Precision.HIGHEST
    co_re_x = jnp.dot(co_re, expand, precision=hi, preferred_element_type=F32)
    co_im_x = jnp.dot(co_im, expand, precision=hi, preferred_element_type=F32)
    bb_re = co_re_x * b_re - co_im_x * b_im
    bb_im = co_re_x * b_im + co_im_x * b_re
    return ab_re, ab_im, bb_re, bb_im


def _ssm_prep(a_re, a_im, log_dt, b_re, b_im, expand, name):
    def body(ar, ai, ld, br, bi, ex, o0, o1, o2, o3):
        outs = _ssm_prep_math(ar[...], ai[...], ld[...], br[...], bi[...], ex[...])
        for ref, v in zip((o0, o1, o2, o3), outs):
            ref[...] = v

    gn = jax.ShapeDtypeStruct((SSM_G, SSM_N), F32)
    gnp = jax.ShapeDtypeStruct((SSM_G, SSM_N * SSM_P), F32)
    return pl.pallas_call(body, name=name, out_shape=[gn, gn, gnp, gnp], compiler_params=_params())(
        a_re, a_im, log_dt, b_re, b_im, expand)


def _ssm_prep_bwd(a_re, a_im, log_dt, b_re, b_im, expand, cts, name):
    def body(ar, ai, ld, br, bi, ex, c0, c1, c2, c3, o0, o1, o2, o3, o4):
        ex_v = ex[...]
        _, vjp = jax.vjp(lambda *p: _ssm_prep_math(*p, ex_v), ar[...], ai[...], ld[...], br[...], bi[...])
        for ref, v in zip((o0, o1, o2, o3, o4), vjp((c0[...], c1[...], c2[...], c3[...]))):
            ref[...] = v

    gn = jax.ShapeDtypeStruct((SSM_G, SSM_N), F32)
    gnp = jax.ShapeDtypeStruct((SSM_G, SSM_N * SSM_P), F32)
    g1 = jax.ShapeDtypeStruct((SSM_G, 1), F32)
    return pl.pallas_call(body, name=name, out_shape=[gn, gn, g1, gnp, gnp], compiler_params=_params())(
        a_re, a_im, log_dt, b_re, b_im, expand, *cts)


def _block_diag_in(bb):
    t = bb.reshape(SSM_TILES, 8, SSM_N, SSM_P).transpose(0, 1, 3, 2)
    eye = jnp.eye(8, dtype=bb.dtype)
    return (t[:, :, :, None, :] * eye[None, :, None, :, None]).reshape(SSM_TILES, LANES, SSM_TILE_GN)


def _block_diag_in_grad(dblk):
    t = dblk.reshape(SSM_TILES, 8, SSM_P, 8, SSM_N)
    t = jnp.einsum("tapbn,ab->tapn", t, jnp.eye(8, dtype=dblk.dtype))
    return t.transpose(0, 1, 3, 2).reshape(SSM_G, SSM_N, SSM_P)


def _block_diag_out(cm):
    t = cm.reshape(SSM_TILES, 8, SSM_P, SSM_N).transpose(0, 1, 3, 2)
    eye = jnp.eye(8, dtype=cm.dtype)
    return (t[:, :, :, None, :] * eye[None, :, None, :, None]).reshape(SSM_TILES, SSM_TILE_GN, LANES)


def _block_diag_out_grad(dblk):
    t = dblk.reshape(SSM_TILES, 8, SSM_N, 8, SSM_P)
    t = jnp.einsum("tanbp,ab->tanp", t, jnp.eye(8, dtype=dblk.dtype))
    return t.transpose(0, 1, 3, 2).reshape(SSM_G, SSM_P, SSM_N)


def _cmul_add(a_re, a_im, s_re, s_im, b_re, b_im):
    return a_re * s_re - a_im * s_im + b_re, a_re * s_im + a_im * s_re + b_im


def _lane_tile_specs(first_tile, index):
    return [pl.BlockSpec((SSM_CHUNK, LANES), lambda c, t=t: (index(c), first_tile + t)) for t in range(SSM_TILES)]


def _ssm_load_rows(src_refs, dst):
    for t in range(SSM_TILES):
        for i in range(SSM_SEG_LEN):
            dst[i * SSM_SEGS:(i + 1) * SSM_SEGS, t * LANES:(t + 1) * LANES] = (
                src_refs[t][pl.ds(i, SSM_SEGS, stride=SSM_SEG_LEN), :])


def _ssm_store_rows(src, dst_refs):
    for t in range(SSM_TILES):
        for i in range(SSM_SEG_LEN):
            dst_refs[t][pl.ds(i, SSM_SEGS, stride=SSM_SEG_LEN), :] = (
                src[i * SSM_SEGS:(i + 1) * SSM_SEGS, t * LANES:(t + 1) * LANES])


def _ssm_powers(ab_re_ref, ab_im_ref, pw_re, pw_im):
    a_re, a_im = ab_re_ref[...], ab_im_ref[...]
    p_re, p_im = a_re, a_im
    for i in range(SSM_SEG_LEN):
        pw_re[i:i + 1, :] = p_re
        pw_im[i:i + 1, :] = p_im
        p_re, p_im = _cmul_add(a_re, a_im, p_re, p_im, 0.0, 0.0)


def _ssm_input_proj(u_s, bblk_re_ref, bblk_im_ref, s_re, s_im):
    for t in range(SSM_TILES):
        ub = u_s[:, t * LANES:(t + 1) * LANES].astype(BF16)
        cols = slice(t * SSM_TILE_GN, (t + 1) * SSM_TILE_GN)
        s_re[:, cols] = _nn(ub, bblk_re_ref[t])
        s_im[:, cols] = _nn(ub, bblk_im_ref[t])


def _ssm_scan(ab_re_ref, ab_im_ref, s_re, s_im, init_re, init_im, conj, reverse):
    sign = -1.0 if conj else 1.0
    for t in range(SSM_TILES):
        cols = slice(t * SSM_TILE_GN, (t + 1) * SSM_TILE_GN)
        a_re = jnp.broadcast_to(ab_re_ref[:, cols], (SSM_SEGS, SSM_TILE_GN))
        a_im = jnp.broadcast_to(ab_im_ref[:, cols], (SSM_SEGS, SSM_TILE_GN)) * sign
        if init_re is None:
            st = (jnp.zeros((SSM_SEGS, SSM_TILE_GN), F32),) * 2
        else:
            st = (init_re[:, cols], init_im[:, cols])

        def step(i, st, cols=cols, a_re=a_re, a_im=a_im):
            idx = (SSM_SEG_LEN - 1 - i) if reverse else i
            rows = pl.ds(pl.multiple_of(idx * SSM_SEGS, SSM_SEGS), SSM_SEGS)
            n_re, n_im = _cmul_add(a_re, a_im, st[0], st[1], s_re[rows, cols], s_im[rows, cols])
            s_re[rows, cols] = n_re
            s_im[rows, cols] = n_im
            return n_re, n_im
        lax.fori_loop(0, SSM_SEG_LEN, step, st, unroll=4)


def _ssm_fixup(pw_re, pw_im, s_re, s_im, cin_re, cin_im, conj, reverse):
    sign = -1.0 if conj else 1.0
    c_re, c_im = cin_re[...], cin_im[...]

    def step(i, c):
        k = (SSM_SEG_LEN - 1 - i) if reverse else i
        rows = pl.ds(pl.multiple_of(i * SSM_SEGS, SSM_SEGS), SSM_SEGS)
        p_re = jnp.broadcast_to(pw_re[pl.ds(k, 1), :], (SSM_SEGS, SSM_GN))
        p_im = jnp.broadcast_to(pw_im[pl.ds(k, 1), :], (SSM_SEGS, SSM_GN)) * sign
        n_re, n_im = _cmul_add(p_re, p_im, c_re, c_im, s_re[rows, :], s_im[rows, :])
        s_re[rows, :] = n_re
        s_im[rows, :] = n_im
        return c
    lax.fori_loop(0, SSM_SEG_LEN, step, 0)


def _ssm_fwd(proj, ab_re, ab_im, bblk_re, bblk_im, cblk_re, cblk_im, d_row, name):
    n_rows = proj.shape[0]
    nchunk = n_rows // SSM_CHUNK
    last = SSM_SEG_LEN - 1
    nt = SSM_TILES

    def body(*refs):
        u_ref, y_ref = refs[:nt], refs[nt + 7:2 * nt + 7]
        ar_ref, ai_ref, br_ref, bi_ref, cr_ref, ci_ref, d_ref = refs[nt:nt + 7]
        cin_re_ref, cin_im_ref, u_s, s_re, s_im, pw_re, pw_im, st_re, st_im = refs[2 * nt + 7:]

        @pl.when(pl.program_id(0) == 0)
        def _():
            _ssm_powers(ar_ref, ai_ref, pw_re, pw_im)
            st_re[...] = jnp.zeros_like(st_re)
            st_im[...] = jnp.zeros_like(st_im)

        _ssm_load_rows(u_ref, u_s)
        _ssm_input_proj(u_s, br_ref, bi_ref, s_re, s_im)
        _ssm_scan(ar_ref, ai_ref, s_re, s_im, None, None, conj=False, reverse=False)
        p_re, p_im = pw_re[last:last + 1, :], pw_im[last:last + 1, :]
        c_re, c_im = st_re[...], st_im[...]
        for j in range(SSM_SEGS):
            cin_re_ref[j:j + 1, :] = c_re
            cin_im_ref[j:j + 1, :] = c_im
            row = last * SSM_SEGS + j
            c_re, c_im = _cmul_add(p_re, p_im, c_re, c_im, s_re[row:row + 1, :], s_im[row:row + 1, :])
        st_re[...] = c_re
        st_im[...] = c_im
        _ssm_fixup(pw_re, pw_im, s_re, s_im, cin_re_ref, cin_im_ref, conj=False, reverse=False)
        for t in range(SSM_TILES):
            cols = slice(t * SSM_TILE_GN, (t + 1) * SSM_TILE_GN)
            lanes = slice(t * LANES, (t + 1) * LANES)
            y = _nn(s_re[:, cols].astype(BF16), cr_ref[t]) - _nn(s_im[:, cols].astype(BF16), ci_ref[t])
            u_s[:, lanes] = y + d_ref[:, lanes] * u_s[:, lanes]
        _ssm_store_rows(u_s, y_ref)

    whole2 = lambda a: pl.BlockSpec(a.shape, lambda c: (0, 0))
    whole3 = lambda a: pl.BlockSpec(a.shape, lambda c: (0, 0, 0))
    seg = pl.BlockSpec((SSM_SEGS, SSM_GN), lambda c: (c, 0))
    seg_shape = jax.ShapeDtypeStruct((nchunk * SSM_SEGS, SSM_GN), F32)
    res = pl.pallas_call(
        body,
        name=name,
        grid=(nchunk,),
        in_specs=_lane_tile_specs((IN_WIDTH - SSM_WIDTH) // LANES, lambda c: c) + [
            whole2(ab_re), whole2(ab_im), whole3(bblk_re), whole3(bblk_im), whole3(cblk_re), whole3(cblk_im),
            whole2(d_row)],
        out_specs=[pl.BlockSpec((SSM_CHUNK, LANES), lambda c: (c, 0))] * nt + [seg, seg],
        out_shape=[jax.ShapeDtypeStruct((n_rows, LANES), F32)] * nt + [seg_shape, seg_shape],
        scratch_shapes=[pltpu.VMEM((SSM_CHUNK, SSM_WIDTH), F32), pltpu.VMEM((SSM_CHUNK, SSM_GN), F32),
                        pltpu.VMEM((SSM_CHUNK, SSM_GN), F32), pltpu.VMEM((SSM_SEG_LEN, SSM_GN), F32),
                        pltpu.VMEM((SSM_SEG_LEN, SSM_GN), F32), pltpu.VMEM((1, SSM_GN), F32),
                        pltpu.VMEM((1, SSM_GN), F32)],
        compiler_params=_params(("arbitrary",)),
    )(*[proj] * nt, ab_re, ab_im, bblk_re, bblk_im, cblk_re, cblk_im, d_row)
    return res[:nt], res[nt], res[nt + 1]


def _ssm_bwd(proj, dy, cin_re, cin_im, ab_re, ab_im, bblk_re, bblk_im, cblk_re, cblk_im, d_row, name):
    n_rows = proj.shape[0]
    nchunk = n_rows // SSM_CHUNK
    nt = SSM_TILES

    def body(*refs):
        u_ref, dy_ref, du_ref = refs[:nt], refs[nt:2 * nt], refs[2 * nt + 9:3 * nt + 9]
        cin_re_ref, cin_im_ref, ar_ref, ai_ref, br_ref, bi_ref, cr_ref, ci_ref, d_ref = refs[2 * nt:2 * nt + 9]
        (dar_ref, dai_ref, dbr_ref, dbi_ref, dcr_ref, dci_ref, dd_ref,
         u_s, dy_s, s_re, s_im, q_re, q_im, pw_re, pw_im, qst_re, qst_im, qin_re, qin_im) = refs[3 * nt + 9:]

        @pl.when(pl.program_id(0) == 0)
        def _():
            _ssm_powers(ar_ref, ai_ref, pw_re, pw_im)
            qst_re[...] = jnp.zeros_like(qst_re)
            qst_im[...] = jnp.zeros_like(qst_im)
            for ref in (dar_ref, dai_ref, dbr_ref, dbi_ref, dcr_ref, dci_ref, dd_ref):
                ref[...] = jnp.zeros_like(ref)

        _ssm_load_rows(u_ref, u_s)
        _ssm_load_rows(dy_ref, dy_s)
        _ssm_input_proj(u_s, br_ref, bi_ref, s_re, s_im)
        _ssm_scan(ar_ref, ai_ref, s_re, s_im, cin_re_ref, cin_im_ref, conj=False, reverse=False)
        for t in range(SSM_TILES):
            cols = slice(t * SSM_TILE_GN, (t + 1) * SSM_TILE_GN)
            dyb = dy_s[:, t * LANES:(t + 1) * LANES].astype(BF16)
            q_re[:, cols] = _nt(dyb, cr_ref[t])
            q_im[:, cols] = -_nt(dyb, ci_ref[t])
            dcr_ref[t] += _tn(s_re[:, cols].astype(BF16), dyb)
            dci_ref[t] -= _tn(s_im[:, cols].astype(BF16), dyb)
        _ssm_scan(ar_ref, ai_ref, q_re, q_im, None, None, conj=True, reverse=True)
        last = SSM_SEG_LEN - 1
        p_re, p_im = pw_re[last:last + 1, :], -pw_im[last:last + 1, :]
        c_re, c_im = qst_re[...], qst_im[...]
        for j in reversed(range(SSM_SEGS)):
            qin_re[j:j + 1, :] = c_re
            qin_im[j:j + 1, :] = c_im
            c_re, c_im = _cmul_add(p_re, p_im, c_re, c_im, q_re[j:j + 1, :], q_im[j:j + 1, :])
        qst_re[...] = c_re
        qst_im[...] = c_im
        _ssm_fixup(pw_re, pw_im, q_re, q_im, qin_re, qin_im, conj=True, reverse=True)
        for t in range(SSM_TILES):
            cols = slice(t * SSM_TILE_GN, (t + 1) * SSM_TILE_GN)

            def step(i, acc, cols=cols):
                rows = pl.ds(pl.multiple_of(i * SSM_SEGS, SSM_SEGS), SSM_SEGS)
                prev = pl.ds(pl.multiple_of((i - 1) * SSM_SEGS, SSM_SEGS), SSM_SEGS)
                qr, qi = q_re[rows, cols], q_im[rows, cols]
                sr, si = s_re[prev, cols], s_im[prev, cols]
                return acc[0] + qr * sr + qi * si, acc[1] + qi * sr - qr * si

            qr, qi = q_re[0:SSM_SEGS, cols], q_im[0:SSM_SEGS, cols]
            sr, si = cin_re_ref[:, cols], cin_im_ref[:, cols]
            acc = lax.fori_loop(1, SSM_SEG_LEN, step, (qr * sr + qi * si, qi * sr - qr * si))
            dar_ref[:, cols] += jnp.sum(acc[0], axis=0, keepdims=True)
            dai_ref[:, cols] += jnp.sum(acc[1], axis=0, keepdims=True)
        for t in range(SSM_TILES):
            cols = slice(t * SSM_TILE_GN, (t + 1) * SSM_TILE_GN)
            lanes = slice(t * LANES, (t + 1) * LANES)
            qrb, qib = q_re[:, cols].astype(BF16), q_im[:, cols].astype(BF16)
            u_t, dy_t = u_s[:, lanes], dy_s[:, lanes]
            ub = u_t.astype(BF16)
            dbr_ref[t] += _tn(ub, qrb)
            dbi_ref[t] += _tn(ub, qib)
            dd_ref[:, lanes] += jnp.sum(dy_t * u_t, axis=0, keepdims=True)
            u_s[:, lanes] = _nt(qrb, br_ref[t]) + _nt(qib, bi_ref[t]) + dy_t * d_ref[:, lanes]
        _ssm_store_rows(u_s, du_ref)

    whole2 = lambda a: pl.BlockSpec(a.shape, lambda c: (0, 0))
    whole3 = lambda a: pl.BlockSpec(a.shape, lambda c: (0, 0, 0))
    back = lambda c: nchunk - 1 - c
    seg = pl.BlockSpec((SSM_SEGS, SSM_GN), lambda c: (back(c), 0))
    gn_row = jax.ShapeDtypeStruct((1, SSM_GN), F32)
    b_shape = jax.ShapeDtypeStruct((SSM_TILES, LANES, SSM_TILE_GN), F32)
    c_shape = jax.ShapeDtypeStruct((SSM_TILES, SSM_TILE_GN, LANES), F32)
    d_shape = jax.ShapeDtypeStruct((1, SSM_WIDTH), F32)
    big = pltpu.VMEM((SSM_CHUNK, SSM_GN), F32)
    res = pl.pallas_call(
        body,
        name=name,
        grid=(nchunk,),
        in_specs=_lane_tile_specs((IN_WIDTH - SSM_WIDTH) // LANES, back) + _lane_tile_specs(0, back) + [
            seg, seg, whole2(ab_re), whole2(ab_im), whole3(bblk_re), whole3(bblk_im), whole3(cblk_re),
            whole3(cblk_im), whole2(d_row)],
        out_specs=[pl.BlockSpec((SSM_CHUNK, LANES), lambda c: (back(c), 0))] * nt + [
            whole2(ab_re), whole2(ab_im), whole3(bblk_re), whole3(bblk_im), whole3(cblk_re), whole3(cblk_im),
            whole2(d_row)],
        out_shape=[jax.ShapeDtypeStruct((n_rows, LANES), F32)] * nt + [gn_row, gn_row, b_shape, b_shape, c_shape,
                                                                       c_shape, d_shape],
        scratch_shapes=[pltpu.VMEM((SSM_CHUNK, SSM_WIDTH), F32), pltpu.VMEM((SSM_CHUNK, SSM_WIDTH), F32),
                        big, big, big, big,
                        pltpu.VMEM((SSM_SEG_LEN, SSM_GN), F32), pltpu.VMEM((SSM_SEG_LEN, SSM_GN), F32),
                        pltpu.VMEM((1, SSM_GN), F32), pltpu.VMEM((1, SSM_GN), F32),
                        pltpu.VMEM((SSM_SEGS, SSM_GN), F32), pltpu.VMEM((SSM_SEGS, SSM_GN), F32)],
        compiler_params=_params(("arbitrary",)),
    )(*[proj] * nt, *[dy] * nt, cin_re, cin_im, ab_re, ab_im, bblk_re, bblk_im, cblk_re, cblk_im, d_row)
    return (res[:nt], *res[nt:])


def _mlp_fwd(h2, w1, w2, tm, tf, name):
    n_rows, dm = h2.shape
    dff = w1.shape[1]

    def body(h_ref, w1_ref, w2_ref, a_ref, y_ref):
        a = _nn(h_ref[...], w1_ref[...])
        a_ref[...] = a.astype(BF16)
        r = jnp.maximum(a, 0.0)
        part = _nn((r * r).astype(BF16), w2_ref[...])
        j = pl.program_id(1)

        @pl.when(j == 0)
        def _():
            y_ref[...] = part

        @pl.when(j > 0)
        def _():
            y_ref[...] += part

    return pl.pallas_call(
        body,
        name=name,
        grid=(n_rows // tm, dff // tf),
        in_specs=[pl.BlockSpec((tm, dm), lambda i, j: (i, 0)), pl.BlockSpec((dm, tf), lambda i, j: (0, j)),
                  pl.BlockSpec((tf, dm), lambda i, j: (j, 0))],
        out_specs=[pl.BlockSpec((tm, tf), lambda i, j: (i, j)), pl.BlockSpec((tm, dm), lambda i, j: (i, 0))],
        out_shape=[jax.ShapeDtypeStruct((n_rows, dff), BF16), jax.ShapeDtypeStruct((n_rows, dm), F32)],
        compiler_params=_params(("parallel", "arbitrary")),
    )(h2, w1, w2)


def _mlp_bwd(dy, h2, a, w2, tm, tf, name):
    n_rows, dm = h2.shape
    dff = a.shape[1]
    per_chip = dff // N_CHIPS // tf

    def body(dy_ref, h_ref, a_ref, w2_ref, da_ref, dw2_ref, dw1_ref):
        dyb = dy_ref[...]
        r = jnp.maximum(a_ref[...].astype(F32), 0.0)
        da = (_nt(dyb, w2_ref[...]) * (2.0 * r)).astype(BF16)
        da_ref[...] = da
        p2 = _tn((r * r).astype(BF16), dyb)
        p1 = _tn(h_ref[...], da)
        i = pl.program_id(1)

        @pl.when(i == 0)
        def _():
            dw2_ref[...] = p2
            dw1_ref[...] = p1

        @pl.when(i > 0)
        def _():
            dw2_ref[...] += p2
            dw1_ref[...] += p1

    return pl.pallas_call(
        body,
        name=name,
        grid=(dff // tf, n_rows // tm),
        in_specs=[pl.BlockSpec((tm, dm), lambda j, i: (i, 0)), pl.BlockSpec((tm, dm), lambda j, i: (i, 0)),
                  pl.BlockSpec((tm, tf), lambda j, i: (i, j)), pl.BlockSpec((tf, dm), lambda j, i: (j, 0))],
        out_specs=[pl.BlockSpec((tm, tf), lambda j, i: (i, j)), pl.BlockSpec((tf, dm), lambda j, i: (j, 0)),
                   pl.BlockSpec((None, dm, tf), lambda j, i: (j // per_chip, 0, j % per_chip))],
        out_shape=[jax.ShapeDtypeStruct((n_rows, dff), BF16), jax.ShapeDtypeStruct((dff, dm), F32),
                   jax.ShapeDtypeStruct((N_CHIPS, dm, dff // N_CHIPS), F32)],
        compiler_params=_params(("parallel", "arbitrary")),
    )(dy, h2, a, w2)


def _local_step(x, pos_col, mod, target, wts, small, hooks=None):
    n_rows = x.shape[0]
    sh1, sc1, gt1, sh2, sc2, gt2 = (mod[:, i * D_MODEL:(i + 1) * D_MODEL] for i in range(N_MOD))
    tm = 256
    d_acc = (1, D_MODEL)

    (h1,) = _rowwise(lambda r, c: [_mod_norm(r[0], *c)], [x], [small["g_pre_mix"], sc1, sh1],
                     [(D_MODEL, BF16)], [], tm, "pre_mix_fwd")
    proj = _matmul(h1, wts["w_in"], "nn", F32, 1024, 1408, 2048, "in_proj")

    freqs = ROPE_THETA ** (-jnp.arange(0, ROT_DIM, 2, dtype=F32) / ROT_DIM)
    freq_lane = jnp.tile(freqs, LANES // (ROT_DIM // 2))[None, :]
    tabs = _rope_tables(pos_col, freq_lane, "rope_tables")
    att = [_att_fwd(proj, tabs, gi, f"att_fwd_{gi}") for gi in range(3)]

    expand = jnp.asarray(_expand_np())
    b_re2, b_im2 = small["ssm_b_re"].reshape(SSM_G, -1), small["ssm_b_im"].reshape(SSM_G, -1)
    log_dt = small["ssm_log_dt"].reshape(SSM_G, 1)
    prep_in = (small["ssm_a_re"], small["ssm_a_im"], log_dt, b_re2, b_im2, expand)
    ab_re, ab_im, bb_re, bb_im = _ssm_prep(*prep_in, "ssm_prep")
    ab_re_row, ab_im_row = ab_re.reshape(1, SSM_GN), ab_im.reshape(1, SSM_GN)
    bblk = [_block_diag_in(t.reshape(SSM_G, SSM_N, SSM_P)).astype(BF16) for t in (bb_re, bb_im)]
    cblk = [_block_diag_out(small[k]).astype(BF16) for k in ("ssm_c_re", "ssm_c_im")]
    d_row = small["ssm_d"].reshape(1, SSM_WIDTH)
    if hooks is not None:
        d_row = _tie(d_row, hooks.mixer.arrived(att[2][0]))
    y_tiles, cin_re, cin_im = _ssm_fwd(proj, ab_re_row, ab_im_row, *bblk, *cblk, d_row, "ssm_fwd")
    y_tiles = list(y_tiles)
    n_mix = 6 + SSM_TILES

    def mixers_out(r, c):
        w_glu, b_glu, g_att, g_ssm = c
        att_n = _att_mix(*r[:6], g_att)
        y2 = _gelu(jnp.concatenate(r[6:n_mix], axis=1))
        z = _nn(y2.astype(BF16), w_glu) + b_glu
        return [jnp.concatenate([att_n.astype(BF16), _glu_out(y2, z, g_ssm).astype(BF16)], axis=1)]

    att_rows = [a[0] for a in att] + [a[1] for a in att]
    if hooks is not None:
        wts = {**wts, **hooks.mixer.weights(y_tiles[0])}
    mix_consts = [wts["w_glu"], small["b_glu"], small["g_attn_out"], small["g_ssm_out"]]
    (cat,) = _rowwise(mixers_out, att_rows + y_tiles, mix_consts, [(OUT_IN_WIDTH, BF16)], [], tm, "mixers_out_fwd")
    mix = _matmul(cat, wts["w_out"], "nn", F32, 1024, 1024, 1280, "out_proj")

    post_consts = [small["g_post_mix"], gt1, small["g_pre_mlp"], sc2, sh2]
    if hooks is not None:
        post_consts[0] = _tie(post_consts[0], hooks.mlp.arrived(mix))
    x1, h2 = _rowwise(lambda r, c: list(_post_mix(r[0], r[1], *c)), [x, mix], post_consts,
                      [(D_MODEL, F32), (D_MODEL, BF16)], [], tm, "post_mix_fwd")
    if hooks is not None:
        wts = {**wts, **hooks.mlp.weights(h2)}
    w_mlp_in, w_mlp_out = wts["w_mlp_in"], wts["w_mlp_out"]
    a_mlp, y_mlp = _mlp_fwd(h2, w_mlp_in, w_mlp_out, 1024, 512, "mlp_fwd")

    def loss_head(r, c):
        x1_v, y_v, t_v = r
        g, gt = c
        fn = lambda y_, g_, gt_: gt_ * _rms(y_, g_)
        out, vjp = jax.vjp(fn, y_v, g, gt)
        err = x1_v + out - t_v
        dx2 = err * (1.0 / D_MODEL)
        dy, dg, dgt = vjp(dx2)
        loss = 0.5 * jnp.sum(jnp.sum(err * err, axis=1, keepdims=True), axis=0, keepdims=True) * (1.0 / D_MODEL)
        return [dx2, dy, loss, dg, dgt]

    dx2, dy_mlp, loss, dg_post_mlp, dgt2 = _rowwise(
        loss_head, [x1, y_mlp, target], [small["g_post_mlp"], gt2],
        [(D_MODEL, F32), (D_MODEL, BF16)], [(1, 1), d_acc, d_acc], tm, "loss_head")

    da_mlp, dw_mlp_out, dw_mlp_in = _mlp_bwd(dy_mlp, h2, a_mlp, w_mlp_out, 1024, 512, "mlp_bwd")
    dw_mlp_out = dw_mlp_out.reshape(dw_mlp_in.shape)
    sent = None if hooks is None else hooks.mlp_grads_to_sibling(dw_mlp_in, dw_mlp_out)
    dh2 = _matmul(da_mlp, w_mlp_in, "nt", F32, 1024, 1024, 2048, "mlp_in_bwd", after=sent)
    sent = None if hooks is None else hooks.mlp_grads_to_chips(dh2)

    def post_mix_bwd(r, c):
        x_v, mix_v, dx1_v, dh2_v = r
        _, vjp = jax.vjp(_post_mix, x_v, mix_v, *c)
        return list(vjp((dx1_v, dh2_v)))

    post_consts_bwd = post_consts if sent is None else [_tie(post_consts[0], sent)] + post_consts[1:]
    dx_a, dmix, dg_post_mix, dgt1, dg_pre_mlp, dsc2, dsh2 = _rowwise(
        post_mix_bwd, [x, mix, dx2, dh2], post_consts_bwd, [(D_MODEL, F32), (D_MODEL, BF16)], [d_acc] * 5, tm,
        "post_mix_bwd")

    dcat = _matmul(dmix, wts["w_out"], "nt", F32, 1024, 1280, 2048, "out_proj_bwd")
    dw_out = _matmul(cat, dmix, "tn", F32, 1280, 1024, 1024, "out_proj_wgrad")

    def mixers_out_bwd(r, c):
        w_glu, b_glu, g_att, g_ssm = c
        dcat_v = r[n_mix]
        _, vjp_att = jax.vjp(_att_mix, *r[:6], g_att)
        *d_ol, dg_att = vjp_att(dcat_v[:, :KV_WIDTH])
        y2, vjp_gelu = jax.vjp(_gelu, jnp.concatenate(r[6:n_mix], axis=1))
        y2b = y2.astype(BF16)
        z = _nn(y2b, w_glu) + b_glu
        _, vjp_glu = jax.vjp(_glu_out, y2, z, g_ssm)
        dy2, dz, dg_ssm = vjp_glu(dcat_v[:, KV_WIDTH:])
        dzb = dz.astype(BF16)
        (dy,) = vjp_gelu(dy2 + _nt(dzb, w_glu))
        return d_ol + [dy, dg_att, _tn(y2b, dzb), jnp.sum(dz, axis=0, keepdims=True), dg_ssm]

    *d_att, dy_ssm, dg_attn_out, dw_glu, db_glu, dg_ssm_out = _rowwise(
        mixers_out_bwd, att_rows + y_tiles + [dcat], mix_consts,
        [(KV_WIDTH, F32)] * 6 + [(SSM_WIDTH, F32)],
        [(1, KV_WIDTH), (SSM_WIDTH, SSM_WIDTH), (1, SSM_WIDTH), (1, SSM_WIDTH)], tm, "mixers_out_bwd")

    du_tiles, dab_re, dab_im, dbblk_re, dbblk_im, dcblk_re, dcblk_im, dd_row = _ssm_bwd(
        proj, dy_ssm, cin_re, cin_im, ab_re_row, ab_im_row, *bblk, *cblk, d_row, "ssm_bwd")
    prep_cts = (dab_re.reshape(SSM_G, SSM_N), dab_im.reshape(SSM_G, SSM_N),
                _block_diag_in_grad(dbblk_re).reshape(SSM_G, -1), _block_diag_in_grad(dbblk_im).reshape(SSM_G, -1))
    da_re, da_im, dlog_dt, db_re, db_im = _ssm_prep_bwd(*prep_in, prep_cts, "ssm_prep_bwd")

    dqkv = [_att_bwd(proj, tabs, att[gi][0], att[gi][1], d_att[gi], d_att[3 + gi], gi, f"att_bwd_{gi}")
            for gi in range(3)]

    def gather_dproj(r, c):
        dq = [r[0], r[3], r[6]]
        dk = r[1] + r[4] + r[7]
        dv = r[2] + r[5] + r[8]
        return [jnp.concatenate([t.astype(BF16) for t in dq + [dk, dv] + r[9:]], axis=1)]

    (dproj,) = _rowwise(gather_dproj, [t for g in dqkv for t in g] + list(du_tiles), [], [(IN_WIDTH, BF16)], [], tm,
                        "gather_dproj")
    dh1 = _matmul(dproj, wts["w_in"], "nt", F32, 1024, 1024, 2816, "in_proj_bwd")
    dw_in = _matmul(h1, dproj, "tn", F32, 1024, 1408, 1024, "in_proj_wgrad")

    def pre_mix_bwd(r, c):
        x_v, dh1_v, dxa_v = r
        _, vjp = jax.vjp(_mod_norm, x_v, *c)
        dx, dg, dsc, dsh = vjp(dh1_v)
        return [dx + dxa_v, dg, dsc, dsh]

    grad_x, dg_pre_mix, dsc1, dsh1 = _rowwise(
        pre_mix_bwd, [x, dh1, dx_a], [small["g_pre_mix"], sc1, sh1], [(D_MODEL, F32)], [d_acc] * 3, tm, "pre_mix_bwd")

    dmod = jnp.concatenate([dsh1, dsc1, dgt1, dsh2, dsc2, dgt2], axis=1)
    big = dict(w_in=dw_in, w_out=dw_out, w_mlp_in=dw_mlp_in, w_mlp_out=dw_mlp_out, w_glu=dw_glu)
    small_g = dict(
        g_pre_mix=dg_pre_mix, g_post_mix=dg_post_mix, ssm_a_re=da_re, ssm_a_im=da_im,
        ssm_log_dt=dlog_dt.reshape(1, SSM_G), ssm_b_re=db_re.reshape(SSM_G, SSM_N, SSM_P),
        ssm_b_im=db_im.reshape(SSM_G, SSM_N, SSM_P), ssm_c_re=_block_diag_out_grad(dcblk_re),
        ssm_c_im=_block_diag_out_grad(dcblk_im), ssm_d=dd_row.reshape(SSM_G, SSM_P), b_glu=db_glu,
        g_attn_out=dg_attn_out, g_ssm_out=dg_ssm_out, g_pre_mlp=dg_pre_mlp, g_post_mlp=dg_post_mlp)
    return loss, grad_x, dmod, big, small_g


MESH_ID = pl.DeviceIdType.MESH
N_DEV = 8
N_CHIPS = 4
HBM_SPEC = pl.BlockSpec(memory_space=pltpu.HBM)


def _place():
    x, y, c = lax.axis_index("x"), lax.axis_index("y"), lax.axis_index("c")
    other_chips = [(1 - x, y), (x, 1 - y), (1 - x, 1 - y)]
    return x, y, c, other_chips


def _half_rows(index, half):
    return pl.ds(pl.multiple_of(index * half, ROW_PAD), half)


def _remote(src, dst, send_sem, recv_sem, dev):
    return pltpu.make_async_remote_copy(src_ref=src, dst_ref=dst, send_sem=send_sem, recv_sem=recv_sem,
                                        device_id=dev, device_id_type=MESH_ID)


def _all_gather8(block, name):
    m_per, n = block.shape

    def body(x_ref, out_ref, send_sems, recv_sems, local_sem):
        x, y, c, chips = _place()
        me, sibling = (x, y, c), (x, y, 1 - c)

        def rows(px, py, pc):
            return out_ref.at[pl.ds((4 * px + 2 * py + pc) * m_per, m_per), :]

        def copy(k, blk, to, src=None):
            return _remote(rows(*blk) if src is None else src, rows(*blk), send_sems.at[k], recv_sems.at[k], to)

        mine = pltpu.make_async_copy(x_ref, rows(*me), local_sem)
        mine.start()
        first = [copy(0, me, sibling, src=x_ref)]
        first += [copy(1 + j, me, (*chip, c), src=x_ref) for j, chip in enumerate(chips)]
        for cp in first:
            cp.start()
        passed = [copy(4 + j, (*chip, c), sibling) for j, chip in enumerate(chips)]
        for j, chip in enumerate(chips):
            copy(1 + j, (*chip, c), me).wait_recv()
            passed[j].start()
        copy(0, sibling, me).wait_recv()
        for j, chip in enumerate(chips):
            copy(4 + j, (*chip, 1 - c), me).wait_recv()
        for cp in first + passed:
            cp.wait_send()
        mine.wait()

    return pl.pallas_call(
        body,
        name=name,
        out_shape=jax.ShapeDtypeStruct((N_DEV * m_per, n), block.dtype),
        in_specs=[pl.BlockSpec(memory_space=pltpu.VMEM)],
        out_specs=pl.BlockSpec(memory_space=pltpu.VMEM),
        scratch_shapes=[pltpu.SemaphoreType.DMA((7,)), pltpu.SemaphoreType.DMA((7,)), pltpu.SemaphoreType.DMA],
        compiler_params=_params(),
    )(block)


def _weight_gather(shards, name):
    n = len(shards)
    shapes = [s.shape for s in shards]

    def body(*refs):
        ins, outs = refs[:n], refs[n:2 * n]
        send, recv, fsend, frecv = refs[2 * n:]
        x, y, c, chips = _place()
        k_me = 2 * x + y
        sibling = (x, y, 1 - c)
        pending = []
        for a in range(n):
            half = shapes[a][0] // 2
            mine = _half_rows(c, half)
            for j, chip in enumerate(chips):
                cp = _remote(ins[a].at[mine, :], outs[a].at[k_me, mine, :], send.at[a, j], recv.at[a, j], (*chip, c))
                cp.start()
                pending.append(cp.wait_send)
        for a in range(n):
            half = shapes[a][0] // 2
            for j, (px, py) in enumerate(chips):
                piece = outs[a].at[2 * px + py, _half_rows(c, half), :]
                _remote(piece, piece, send.at[a, j], recv.at[a, j], (px, py, c)).wait_recv()
                fw = _remote(piece, piece, fsend.at[a, j], frecv.at[a, j], sibling)
                fw.start()
                pending.append(fw.wait_send)
        for a in range(n):
            half = shapes[a][0] // 2
            for j, (px, py) in enumerate(chips):
                piece = outs[a].at[2 * px + py, _half_rows(1 - c, half), :]
                _remote(piece, piece, fsend.at[a, j], frecv.at[a, j], sibling).wait_recv()
        for wait in pending:
            wait()

    sems = pltpu.SemaphoreType.DMA((n, 3))
    return pl.pallas_call(
        body,
        name=name,
        out_shape=[jax.ShapeDtypeStruct((N_CHIPS,) + s, BF16) for s in shapes],
        in_specs=[HBM_SPEC] * n,
        out_specs=[HBM_SPEC] * n,
        scratch_shapes=[sems, sems, sems, sems],
        compiler_params=_params(),
    )(*shards)


def _sibling_swap(halves, name):
    n = len(halves)
    shapes = [h.shape for h in halves]

    def body(*refs):
        ins, outs = refs[:n], refs[n:2 * n]
        send, recv = refs[2 * n:]
        x, y, c, _ = _place()
        pending = []
        for a in range(n):
            half = shapes[a][0]
            mine = outs[a].at[_half_rows(c, half), :]
            cp = _remote(ins[a], mine, send.at[a], recv.at[a], (x, y, 1 - c))
            cp.start()
            pending.append(cp.wait_send)
        for a in range(n):
            half = shapes[a][0]
            theirs = outs[a].at[_half_rows(1 - c, half), :]
            _remote(theirs, theirs, send.at[a], recv.at[a], (x, y, 1 - c)).wait_recv()
        for wait in pending:
            wait()

    return pl.pallas_call(
        body,
        name=name,
        out_shape=[jax.ShapeDtypeStruct((2 * s[0], s[1]), F32) for s in shapes],
        in_specs=[HBM_SPEC] * n,
        out_specs=[HBM_SPEC] * n,
        scratch_shapes=[pltpu.SemaphoreType.DMA((n,)), pltpu.SemaphoreType.DMA((n,))],
        compiler_params=_params(),
    )(*halves)


SEM_SPEC = pl.BlockSpec(memory_space=pltpu.SEMAPHORE)
ANY_SPEC = pl.BlockSpec(memory_space=pl.ANY)
DATAFLOW = pltpu.SideEffectType.DATAFLOW_SIDE_EFFECTING


def _split_copy_start(srcs, lands, plan, n_sems, name, after=None):
    bufs = list(srcs) + list(lands)
    ns, nb = len(srcs), len(bufs)
    extra = [] if after is None else [after]

    def body(*refs):
        outs = refs[nb + len(extra):]
        for outgoing, _ in plan(refs[:ns], refs[ns:nb], outs[0], outs[1]):
            outgoing.start()
        outs[-1][...] = jnp.zeros_like(outs[-1])

    sems = pltpu.SemaphoreType.DMA((n_sems,))
    return pl.pallas_call(
        body,
        name=name,
        out_shape=(sems, sems, *[pltpu.HBM(b.shape, b.dtype) for b in bufs], jax.ShapeDtypeStruct((8, LANES), F32)),
        in_specs=[HBM_SPEC] * nb + [ANY_SPEC] * len(extra),
        out_specs=(SEM_SPEC, SEM_SPEC, *[HBM_SPEC] * nb, pl.BlockSpec(memory_space=pltpu.VMEM)),
        input_output_aliases={i: 2 + i for i in range(nb)},
        compiler_params=pltpu.CompilerParams(has_side_effects=DATAFLOW),
    )(*[pltpu.with_memory_space_constraint(b, pltpu.HBM) for b in bufs], *extra)


def _split_copy_wait(started, plan, after, name, n_srcs=None):
    send, recv, *bufs = started[:-1]
    nb = len(bufs)
    ns = nb // 2 if n_srcs is None else n_srcs

    def body(*refs):
        for outgoing, incoming in plan(refs[:ns], refs[ns:nb], refs[nb], refs[nb + 1]):
            outgoing.wait_send()
            incoming.wait_recv()

    return pl.pallas_call(
        body,
        name=name,
        out_shape=tuple(pltpu.HBM(b.shape, b.dtype) for b in bufs),
        in_specs=[HBM_SPEC] * nb + [SEM_SPEC, SEM_SPEC, ANY_SPEC],
        out_specs=tuple([HBM_SPEC] * nb),
        input_output_aliases={i: i for i in range(nb)},
        compiler_params=pltpu.CompilerParams(has_side_effects=DATAFLOW),
    )(*bufs, send, recv, after)


def _weight_plan(shapes):
    def plan(srcs, lands, send, recv):
        x, y, c, chips = _place()
        copies = []
        for a in range(len(shapes)):
            mine = _half_rows(c, shapes[a][0] // 2)
            for j, (px, py) in enumerate(chips):
                s = 3 * a + j
                arrival = lands[a].at[2 * px + py, mine, :]
                copies.append((_remote(srcs[a].at[mine, :], lands[a].at[2 * x + y, mine, :], send.at[s], recv.at[s], (px, py, c)),
                               _remote(arrival, arrival, send.at[s], recv.at[s], (px, py, c))))
        return copies
    return plan


def _halves_plan(shapes):
    def plan(srcs, lands, send, recv):
        x, y, c, _ = _place()
        copies = []
        for a in range(len(shapes)):
            theirs = srcs[a].at[:, _half_rows(1 - c, shapes[a][1] // 2), :]
            copies.append((_remote(theirs, lands[a], send.at[a], recv.at[a], (x, y, 1 - c)),
                           _remote(lands[a], lands[a], send.at[a], recv.at[a], (x, y, 1 - c))))
        return copies
    return plan


def _exchange_plan(n):
    def plan(srcs, lands, send, recv):
        x, y, c, chips = _place()
        copies = []
        for a in range(n):
            for j, (px, py) in enumerate(chips):
                s = 3 * a + j
                copies.append((_remote(srcs[a].at[2 * px + py], lands[a].at[j], send.at[s], recv.at[s], (px, py, c)),
                               _remote(lands[a].at[j], lands[a].at[j], send.at[s], recv.at[s], (px, py, c))))
        return copies
    return plan


def _swap_plan(shapes):
    def plan(srcs, lands, send, recv):
        x, y, c, _ = _place()
        copies = []
        for a in range(len(shapes)):
            half = shapes[a][0]
            theirs = lands[a].at[_half_rows(1 - c, half), :]
            copies.append((_remote(srcs[a], lands[a].at[_half_rows(c, half), :], send.at[a], recv.at[a], (x, y, 1 - c)),
                           _remote(theirs, theirs, send.at[a], recv.at[a], (x, y, 1 - c))))
        return copies
    return plan


def _forward_plan(shapes):
    def plan(stacks, _, send, recv):
        x, y, c, chips = _place()
        copies = []
        for a in range(len(shapes)):
            half = shapes[a][0] // 2
            for j, (px, py) in enumerate(chips):
                s = 3 * a + j
                mine = stacks[a].at[2 * px + py, _half_rows(c, half), :]
                theirs = stacks[a].at[2 * px + py, _half_rows(1 - c, half), :]
                copies.append((_remote(mine, mine, send.at[s], recv.at[s], (x, y, 1 - c)),
                               _remote(theirs, theirs, send.at[s], recv.at[s], (x, y, 1 - c))))
        return copies
    return plan


def _tie(x, token):
    return x + token[0:1, 0:1].astype(x.dtype)


ROW_PAD = 16


def _silu(x):
    return x * _sigmoid(x)


def _ada_fwd(c_all, w_ada, b_ada, name):
    dm, cols = w_ada.shape
    tn = 512

    def body(c_ref, w_ref, b_ref, o_ref):
        o_ref[...] = _nn(_silu(c_ref[...]).astype(BF16), w_ref[...].astype(BF16)) + b_ref[...]

    return pl.pallas_call(
        body,
        name=name,
        grid=(cols // tn,),
        in_specs=[pl.BlockSpec((ROW_PAD, dm), lambda j: (0, 0)), pl.BlockSpec((dm, tn), lambda j: (0, j)),
                  pl.BlockSpec((1, tn), lambda j: (0, j))],
        out_specs=pl.BlockSpec((ROW_PAD, tn), lambda j: (0, j)),
        out_shape=jax.ShapeDtypeStruct((ROW_PAD, cols), F32),
        compiler_params=_params(("parallel",)),
    )(c_all, w_ada, b_ada)


def _adamw(w, g, m, v):
    m = ADAM_B1 * m + (1.0 - ADAM_B1) * g
    v = ADAM_B2 * v + (1.0 - ADAM_B2) * (g * g)
    m_hat = m / (1.0 - ADAM_B1 ** ADAM_STEP)
    v_hat = v / (1.0 - ADAM_B2 ** ADAM_STEP)
    delta = -ADAM_LR * (m_hat / (jnp.sqrt(v_hat) + ADAM_EPS) + ADAM_WD * w)
    return delta, m, v


def _ada_bwd_adamw(c_all, dmod_cols, w, m, v, name):
    dm, cols = w.shape
    tm, tn = 512, 512

    def body(c_ref, d_ref, w_ref, m_ref, v_ref, g_ref, dl_ref, nm_ref, nv_ref):
        g = _tn(_silu(c_ref[...]).astype(BF16), d_ref[...].astype(BF16))
        g_ref[...] = g
        dl_ref[...], nm_ref[...], nv_ref[...] = _adamw(w_ref[...], g, m_ref[...], v_ref[...])

    tile = pl.BlockSpec((tm, tn), lambda i, j: (i, j))
    shape = jax.ShapeDtypeStruct((dm, cols), F32)
    return pl.pallas_call(
        body,
        name=name,
        grid=(dm // tm, cols // tn),
        in_specs=[pl.BlockSpec((ROW_PAD, tm), lambda i, j: (0, i)), pl.BlockSpec((ROW_PAD, tn), lambda i, j: (0, j)),
                  tile, tile, tile],
        out_specs=[tile] * 4,
        out_shape=[shape] * 4,
        compiler_params=_params(("parallel", "parallel")),
    )(c_all, dmod_cols, w, m, v)


def _sum_blocks(parts, nblk, name):
    rows, cols = parts.shape[0] // nblk, parts.shape[1]

    def body(p_ref, o_ref):
        tot = p_ref[0:rows, :]
        for b in range(1, nblk):
            tot = tot + p_ref[b * rows:(b + 1) * rows, :]
        o_ref[...] = tot

    return pl.pallas_call(body, name=name, out_shape=jax.ShapeDtypeStruct((rows, cols), F32), compiler_params=_params())(parts)


def _adamw_rows(w, g, m, v, tm, name):
    return _rowwise(lambda r, c: list(_adamw(*r)), [w, g, m, v], [], [(w.shape[1], F32)] * 3, [], tm, name)


BIG = ("w_in", "w_out", "w_mlp_in", "w_mlp_out", "w_glu")
COL_SHARDED = ("w_in", "w_out", "w_mlp_in")
SMALL = ("b_ada", "g_pre_mix", "g_post_mix", "ssm_a_re", "ssm_a_im", "ssm_log_dt", "ssm_b_re", "ssm_b_im",
         "ssm_c_re", "ssm_c_im", "ssm_d", "b_glu", "g_attn_out", "g_ssm_out", "g_pre_mlp", "g_post_mlp")
WEIGHTS = ("w_ada", "b_ada", "g_pre_mix", "g_post_mix", "w_in", "ssm_a_re", "ssm_a_im", "ssm_log_dt", "ssm_b_re",
           "ssm_b_im", "ssm_c_re", "ssm_c_im", "ssm_d", "w_glu", "b_glu", "g_attn_out", "g_ssm_out", "w_out",
           "g_pre_mlp", "g_post_mlp", "w_mlp_in", "w_mlp_out")
FLAT_COLS = 1024
FLAT_ROWS = 256
ROW_TILE = {"w_in": 256, "w_out": 128, "w_mlp_in": 256, "w_mlp_out": 256, "w_glu": 112}


def _flatten_small(tree):
    flat = jnp.concatenate([tree[k].reshape(-1) for k in SMALL])
    return jnp.pad(flat, (0, FLAT_ROWS * FLAT_COLS - flat.shape[0])).reshape(FLAT_ROWS, FLAT_COLS)


def _unflatten_small(flat, like):
    flat = flat.reshape(-1)
    out, at = {}, 0
    for k in SMALL:
        size = math.prod(like[k].shape)
        out[k] = flat[at:at + size].reshape(like[k].shape)
        at += size
    return out


def _unstack(stack, name):
    if name in COL_SHARDED:
        return stack.transpose(1, 0, 2).reshape(stack.shape[1], N_CHIPS * stack.shape[2])
    return stack.reshape(N_CHIPS * stack.shape[1], stack.shape[2])


def _stack(full, name):
    if name in COL_SHARDED:
        return full.reshape(full.shape[0], N_CHIPS, full.shape[1] // N_CHIPS).transpose(1, 0, 2)
    return full.reshape(N_CHIPS, full.shape[0] // N_CHIPS, full.shape[1])


EARLY = ("w_in", "w_out", "w_glu")
LATE = ("w_mlp_in", "w_mlp_out")
MIXER_W = ("w_out", "w_glu")


def _chip_sums(names, g_stacks, from_sibling, ic, chip):
    own, to_send = [], []
    place = jnp.stack([ic, chip]).astype(jnp.int32)
    for k, gs, fs in zip(names, g_stacks, from_sibling):
        _, rows, cols = gs.shape
        half, tm = rows // 2, ROW_TILE[k]
        nt = half // tm

        def body(place_ref, g_ref, f_ref, own_ref, send_ref):
            s = g_ref[...] + f_ref[...]
            send_ref[...] = s.astype(BF16)

            @pl.when(pl.program_id(1) == place_ref[1])
            def _():
                own_ref[...] = s

        slab = lambda index: pl.BlockSpec((None, tm, cols), index)
        mine, to_chips = pl.pallas_call(
            body,
            name="grad_chip_sum_" + k,
            grid_spec=pltpu.PrefetchScalarGridSpec(
                num_scalar_prefetch=1,
                grid=(nt, N_CHIPS),
                in_specs=[slab(lambda i, kk, p, nt=nt: (kk, p[0] * nt + i, 0)), slab(lambda i, kk, p: (kk, i, 0))],
                out_specs=[pl.BlockSpec((tm, cols), lambda i, kk, p: (i, 0)), slab(lambda i, kk, p: (kk, i, 0))]),
            out_shape=[jax.ShapeDtypeStruct((half, cols), F32), jax.ShapeDtypeStruct((N_CHIPS, half, cols), BF16)],
            compiler_params=_params(("arbitrary", "arbitrary")),
        )(place, gs, fs)
        own.append(mine)
        to_send.append(to_chips)
    return own, to_send


def _grad_totals(names, own, from_chips):
    totals = []
    for k, mine, fc in zip(names, own, from_chips):
        half, cols = mine.shape
        tm = ROW_TILE[k]

        def body(m_ref, a_ref, b_ref, c_ref, o_ref):
            o_ref[...] = m_ref[...] + a_ref[...].astype(F32) + b_ref[...].astype(F32) + c_ref[...].astype(F32)

        rows = pl.BlockSpec((tm, cols), lambda i: (i, 0))
        totals.append(pl.pallas_call(
            body,
            name="grad_total_" + k,
            grid=(half // tm,),
            in_specs=[rows] + [pl.BlockSpec((None, tm, cols), lambda i, j=j: (j, i, 0)) for j in range(3)],
            out_specs=rows,
            out_shape=jax.ShapeDtypeStruct((half, cols), F32),
            compiler_params=_params(("parallel",)),
        )(mine, fc, fc, fc))
    return totals


class _LateWeights:
    def __init__(self, names, own_shards, chip, after, tag):
        self.names, self.chip, self.tag = names, chip, tag
        self.shapes = [o.shape for o in own_shards]
        lands = [lax.empty((N_CHIPS,) + s, BF16) for s in self.shapes]
        self.gather = _split_copy_start(own_shards, lands, _weight_plan(self.shapes), 3 * len(names),
                                        tag + "_gather_start", after=after)
        self.token = self.gather[-1]

    def arrived(self, after):
        n = len(self.names)
        done = _split_copy_wait(self.gather, _weight_plan(self.shapes), after, self.tag + "_gather_wait")
        self.own_shards = done[:n]
        self.forward = _split_copy_start(done[n:], [], _forward_plan(self.shapes), 3 * n, self.tag + "_forward_start")
        return self.forward[-1]

    def weights(self, after):
        n = len(self.names)
        stacks = _split_copy_wait(self.forward, _forward_plan(self.shapes), after, self.tag + "_forward_wait", n_srcs=n)
        stacks = [lax.dynamic_update_index_in_dim(s, o, self.chip, 0) for s, o in zip(stacks, self.own_shards)]
        return {k: _unstack(s, k) for k, s in zip(self.names, stacks)}


class _Overlap:
    def __init__(self, own, ic, chip, after):
        self.ic, self.chip = ic, chip
        self.mixer = _LateWeights(MIXER_W, [own[k] for k in MIXER_W], chip, after, "mixer_weight")
        self.mlp = _LateWeights(LATE, [own[k] for k in LATE], chip, self.mixer.token, "mlp_weight")
        self.token = self.mlp.token

    def mlp_grads_to_sibling(self, dw_in, dw_out):
        stacks = [dw_in, dw_out]
        self.g_shapes = [s.shape for s in stacks]
        lands = [lax.empty((N_CHIPS, s[1] // 2, s[2]), F32) for s in self.g_shapes]
        self.halves = _split_copy_start(stacks, lands, _halves_plan(self.g_shapes), len(LATE), "mlp_grad_halves_start")
        return self.halves[-1]

    def mlp_grads_to_chips(self, after):
        n = len(LATE)
        done = _split_copy_wait(self.halves, _halves_plan(self.g_shapes), after, "mlp_grad_halves_wait")
        self.own, to_send = _chip_sums(LATE, done[:n], done[n:], self.ic, self.chip)
        lands = [lax.empty((3,) + s.shape[1:], BF16) for s in to_send]
        self.exchange = _split_copy_start(to_send, lands, _exchange_plan(n), 3 * n, "mlp_grad_exchange_start")
        return self.exchange[-1]

    def mlp_grads_reduced(self, after):
        n = len(LATE)
        done = _split_copy_wait(self.exchange, _exchange_plan(n), after, "mlp_grad_exchange_wait")
        return _grad_totals(LATE, self.own, done[n:])


def _pad_rows(row):
    return jnp.pad(row, ((0, 8 - row.shape[0]), (0, 0)))


def _every_eighth(gathered):
    rows = gathered.reshape(N_DEV, 8, gathered.shape[1])[:, 0, :]
    return jnp.pad(rows, ((0, ROW_PAD - N_DEV), (0, 0)))


def kernel(x, c, positions, w_ada, b_ada, g_pre_mix, g_post_mix, w_in, ssm_a_re, ssm_a_im, ssm_log_dt, ssm_b_re, ssm_b_im, ssm_c_re, ssm_c_im, ssm_d, w_glu, b_glu, g_attn_out, g_ssm_out, w_out, g_pre_mlp, g_post_mlp, w_mlp_in, w_mlp_out, loss_target, m_w_ada, m_b_ada, m_g_pre_mix, m_g_post_mix, m_w_in, m_ssm_a_re, m_ssm_a_im, m_ssm_log_dt, m_ssm_b_re, m_ssm_b_im, m_ssm_c_re, m_ssm_c_im, m_ssm_d, m_w_glu, m_b_glu, m_g_attn_out, m_g_ssm_out, m_w_out, m_g_pre_mlp, m_g_post_mlp, m_w_mlp_in, m_w_mlp_out, v_w_ada, v_b_ada, v_g_pre_mix, v_g_post_mix, v_w_in, v_ssm_a_re, v_ssm_a_im, v_ssm_log_dt, v_ssm_b_re, v_ssm_b_im, v_ssm_c_re, v_ssm_c_im, v_ssm_d, v_w_glu, v_b_glu, v_g_attn_out, v_g_ssm_out, v_w_out, v_g_pre_mlp, v_g_post_mlp, v_w_mlp_in, v_w_mlp_out):
    given = dict(locals())
    w = {k: given[k][0] for k in WEIGHTS}
    mom = {k: given["m_" + k][0] for k in WEIGHTS}
    var = {k: given["v_" + k][0] for k in WEIGHTS}
    for tree in (w, mom, var):
        for k in ("b_ada", "g_pre_mix", "g_post_mix", "ssm_log_dt", "b_glu", "g_attn_out", "g_ssm_out", "g_pre_mlp",
                  "g_post_mlp"):
            tree[k] = tree[k].reshape(1, -1)
    ix, iy, ic = lax.axis_index("x"), lax.axis_index("y"), lax.axis_index("c")
    chip = 2 * ix + iy
    me = 4 * ix + 2 * iy + ic
    shard_cols = w["w_ada"].shape[1]

    c_all = _every_eighth(_all_gather8(_pad_rows(c), "gather_c"))
    b_ada_cols = lax.dynamic_slice_in_dim(w["b_ada"], chip * shard_cols, shard_cols, axis=1)
    mod_cols = _ada_fwd(c_all, w["w_ada"], b_ada_cols, "ada_fwd")[:N_DEV]
    mod_all = _all_gather8(mod_cols, "gather_mod").reshape(N_CHIPS, 2, N_DEV, shard_cols)[:, 0]
    mod = lax.dynamic_index_in_dim(mod_all, me, axis=1, keepdims=False).reshape(1, N_MOD * D_MODEL)

    w_in_own = w["w_in"].astype(BF16)
    (w_in_stack,) = _weight_gather([w_in_own], "weight_gather")
    w_in_stack = lax.dynamic_update_index_in_dim(w_in_stack, w_in_own, chip, 0)
    wts = {"w_in": _unstack(w_in_stack, "w_in")}
    overlap = _Overlap({k: w[k].astype(BF16) for k in MIXER_W + LATE}, ic, chip, after=w_in_stack)
    mod = _tie(mod, overlap.token)

    small = {k: w[k] for k in SMALL if k != "b_ada"}
    loss, grad_x, dmod, big_g, small_g = _local_step(x[0], positions.reshape(-1, 1), mod, loss_target[0], wts, small,
                                                     hooks=overlap)
    loss = lax.psum(loss[0, 0], ("x", "y", "c"))

    n_early = len(EARLY)
    g_stacks = [_stack(big_g[k], k) for k in EARLY]
    halves_plan = _halves_plan([s.shape for s in g_stacks])
    lands = [lax.empty((N_CHIPS, s.shape[1] // 2, s.shape[2]), F32) for s in g_stacks]
    halves = _split_copy_start(g_stacks, lands, halves_plan, n_early, "grad_halves_start")

    small_g["b_ada"] = dmod
    parts = _all_gather8(_tie(_flatten_small(small_g), halves[-1]), "gather_small_grads")

    halves = _split_copy_wait(halves, halves_plan, parts, "grad_halves_wait")
    chip_f32, chip_bf16 = _chip_sums(EARLY, halves[:n_early], halves[n_early:], ic, chip)
    exchange_plan = _exchange_plan(len(EARLY))
    lands = [lax.empty((3,) + s.shape[1:], BF16) for s in chip_bf16]
    exchange = _split_copy_start(chip_bf16, lands, exchange_plan, 3 * len(EARLY), "grad_exchange_start", after=parts)

    small_flat = _sum_blocks(parts, N_DEV, "small_grad_sum")
    grads = _unflatten_small(small_flat, w)

    mod_rows = N_MOD * D_MODEL // FLAT_COLS
    dmod_all = parts.reshape(N_DEV, FLAT_ROWS, FLAT_COLS)[:, :mod_rows].reshape(N_DEV, N_MOD * D_MODEL)
    dmod_all = jnp.pad(dmod_all, ((0, ROW_PAD - N_DEV), (0, 0)))
    dmod_cols = lax.dynamic_slice_in_dim(dmod_all, chip * shard_cols, shard_cols, axis=1)
    mlp_reduced = overlap.mlp_grads_reduced(exchange[-1])
    swap_plan = _swap_plan([r.shape for r in mlp_reduced])
    lands = [lax.empty((2 * r.shape[0], r.shape[1]), F32) for r in mlp_reduced]
    swap = _split_copy_start(mlp_reduced, lands, swap_plan, len(LATE), "mlp_grad_swap_start")
    g_ada, d_ada, m_ada, v_ada = _ada_bwd_adamw(_tie(c_all, swap[-1]), dmod_cols, w["w_ada"], mom["w_ada"],
                                                var["w_ada"], "ada_bwd_adamw")
    grads["w_ada"] = g_ada
    delta, new_m, new_v = {"w_ada": d_ada}, {"w_ada": m_ada}, {"w_ada": v_ada}

    def finish(names, reduced, swapped):
        for k, s, r in zip(names, swapped, reduced):
            grads[k] = lax.dynamic_update_slice_in_dim(s, r, ic * r.shape[0], axis=0)
            delta[k], new_m[k], new_v[k] = _adamw_rows(w[k], grads[k], mom[k], var[k], ROW_TILE[k], "adamw_" + k)

    swap = _split_copy_wait(swap, swap_plan, g_ada, "mlp_grad_swap_wait")
    finish(LATE, swap[:len(LATE)], swap[len(LATE):])
    from_chips = _split_copy_wait(exchange, exchange_plan, new_v[LATE[-1]], "grad_exchange_wait")[n_early:]
    reduced = _grad_totals(EARLY, chip_f32, from_chips)
    finish(EARLY, reduced, _sibling_swap(reduced, "grad_sibling_swap"))

    flat_upd = _adamw_rows(_flatten_small(w), small_flat, _flatten_small(mom), _flatten_small(var), FLAT_ROWS,
                           "adamw_small")
    for tree, flat in zip((delta, new_m, new_v), flat_upd):
        tree.update(_unflatten_small(flat, w))

    shaped = lambda tree: [tree[k].reshape(given[k].shape) for k in WEIGHTS]
    return (loss, grad_x[None], *shaped(grads), *shaped(delta), *shaped(new_m), *shaped(new_v))
```

```python
import functools
import math

import jax
import jax.numpy as jnp
import numpy as np
from jax import lax
from jax.experimental import pallas as pl
from jax.experimental.pallas import tpu as pltpu

F32 = jnp.float32
BF16 = jnp.bfloat16

D_MODEL = 2048
HEAD_DIM = 64
DILATIONS = (1, 4, 16)
ATT_SPAN = 128
ATT_BLK = 128
HEADS_PER_GROUP = 6
KV_WIDTH = HEADS_PER_GROUP * HEAD_DIM
ATT_Q_WIDTH = 3 * KV_WIDTH
ROT_DIM = 16
ROPE_THETA = 500000.0
SSM_WIDTH = 896
SSM_P = 16
SSM_G = 56
SSM_N = 64
SSM_GN = SSM_G * SSM_N
SSM_TILES = SSM_WIDTH // 128
SSM_TILE_GN = 8 * SSM_N
IN_WIDTH = 2816
OUT_IN_WIDTH = 1280
D_FF = 8192
N_MOD = 6
EPS = 1e-6
LANES = 128
SSM_SEGS = 8
SSM_CHUNK = 256
SSM_SEG_LEN = SSM_CHUNK // SSM_SEGS

ADAM_LR = 0.001
ADAM_B1 = 0.9
ADAM_B2 = 0.999
ADAM_EPS = 1e-08
ADAM_WD = 0.01
ADAM_STEP = 10

VMEM_LIMIT = 56 * 1024 * 1024


def _params(sem=None):
    return pltpu.CompilerParams(dimension_semantics=sem, vmem_limit_bytes=VMEM_LIMIT)


def _dot(a, b, dims):
    return lax.dot_general(a, b, (dims, ((), ())), preferred_element_type=F32)


def _nn(a, b):
    return _dot(a, b, ((1,), (0,)))


def _nt(a, b):
    return _dot(a, b, ((1,), (1,)))


def _tn(a, b):
    return _dot(a, b, ((0,), (0,)))


def _matmul(a, b, mode, out_dtype, tm, tn, tk, name, after=None):
    if mode == "nn":
        (m, k), (_, n) = a.shape, b.shape
        a_spec = pl.BlockSpec((tm, tk), lambda i, j, kk: (i, kk))
        b_spec = pl.BlockSpec((tk, tn), lambda i, j, kk: (kk, j))
        op = _nn
    elif mode == "nt":
        (m, k), (n, _) = a.shape, b.shape
        a_spec = pl.BlockSpec((tm, tk), lambda i, j, kk: (i, kk))
        b_spec = pl.BlockSpec((tn, tk), lambda i, j, kk: (j, kk))
        op = _nt
    else:
        (k, m), (_, n) = a.shape, b.shape
        a_spec = pl.BlockSpec((tk, tm), lambda i, j, kk: (kk, i))
        b_spec = pl.BlockSpec((tk, tn), lambda i, j, kk: (kk, j))
        op = _tn
    assert m % tm == 0 and n % tn == 0 and k % tk == 0, (name, m, n, k)
    nk = k // tk

    def body(a_ref, b_ref, *rest):
        o_ref, acc_ref = rest[-2:]
        kk = pl.program_id(2)

        @pl.when(kk == 0)
        def _():
            acc_ref[...] = jnp.zeros_like(acc_ref)

        acc_ref[...] += op(a_ref[...], b_ref[...])

        @pl.when(kk == nk - 1)
        def _():
            o_ref[...] = acc_ref[...].astype(o_ref.dtype)

    extra = [] if after is None else [after]
    return pl.pallas_call(
        body,
        name=name,
        grid=(m // tm, n // tn, nk),
        in_specs=[a_spec, b_spec] + [pl.BlockSpec(t.shape, lambda i, j, kk: (0, 0)) for t in extra],
        out_specs=pl.BlockSpec((tm, tn), lambda i, j, kk: (i, j)),
        out_shape=jax.ShapeDtypeStruct((m, n), out_dtype),
        scratch_shapes=[pltpu.VMEM((tm, tn), F32)],
        compiler_params=_params(("parallel", "parallel", "arbitrary")),
    )(a, b, *extra)


def _rowwise(fn, rows, consts, out_rows, out_accs, tm, name):
    n_rows = rows[0].shape[0]
    assert n_rows % tm == 0
    nr, nc, no = len(rows), len(consts), len(out_rows)

    def body(*refs):
        r_in, c_in = refs[:nr], refs[nr:nr + nc]
        o_row, o_acc = refs[nr + nc:nr + nc + no], refs[nr + nc + no:]
        outs = fn([r[...] for r in r_in], [c[...] for c in c_in])
        assert len(outs) == len(o_row) + len(o_acc), name
        for ref, v in zip(o_row, outs[:no]):
            ref[...] = v.astype(ref.dtype)
        first = pl.program_id(0) == 0
        for ref, v in zip(o_acc, outs[no:]):
            @pl.when(first)
            def _(ref=ref, v=v):
                ref[...] = v.astype(F32)

            @pl.when(jnp.logical_not(first))
            def _(ref=ref, v=v):
                ref[...] += v.astype(F32)

    in_specs = [pl.BlockSpec((tm, r.shape[1]), lambda i: (i, 0)) for r in rows]
    in_specs += [pl.BlockSpec(c.shape, lambda i: (0, 0)) for c in consts]
    out_specs = [pl.BlockSpec((tm, w), lambda i: (i, 0)) for w, _ in out_rows]
    out_specs += [pl.BlockSpec(s, lambda i: (0, 0)) for s in out_accs]
    out_shape = [jax.ShapeDtypeStruct((n_rows, w), dt) for w, dt in out_rows]
    out_shape += [jax.ShapeDtypeStruct(s, F32) for s in out_accs]
    return pl.pallas_call(
        body,
        name=name,
        grid=(n_rows // tm,),
        in_specs=in_specs,
        out_specs=out_specs,
        out_shape=out_shape,
        compiler_params=_params(("arbitrary",)),
    )(*rows, *consts)


def _rms(x, g):
    return x * lax.rsqrt(jnp.mean(x * x, axis=-1, keepdims=True) + EPS) * g


def _mod_norm(x, g, sc, sh):
    return _rms(x, g) * (1.0 + sc) + sh


def _gelu(x):
    return 0.5 * x * (1.0 + jnp.tanh(math.sqrt(2.0 / math.pi) * (x + 0.044715 * (x * x * x))))


def _sigmoid(x):
    return 1.0 / (1.0 + jnp.exp(-x))


def _post_mix(x, mix, g_post, gt1, g_pre, sc2, sh2):
    x1 = x + gt1 * _rms(mix, g_post)
    return x1, _mod_norm(x1, g_pre, sc2, sh2)


def _att_mix(o0, o1, o2, l0, l1, l2, g):
    m = jnp.maximum(jnp.maximum(l0, l1), l2)
    e0, e1, e2 = jnp.exp(l0 - m), jnp.exp(l1 - m), jnp.exp(l2 - m)
    att = (e0 * o0 + e1 * o1 + e2 * o2) / (e0 + e1 + e2)
    return _rms(att, g)


def _glu_out(y2, z, g):
    return _rms(y2 * _sigmoid(z), g)


def _rope_tables(pos_col, freq_lane, name):
    n_rows = pos_col.shape[0]
    tm = 512

    def body(p_ref, f_ref, cos_ref, lo_ref, hi_ref):
        ang = p_ref[...].astype(F32) * f_ref[...]
        lane = lax.broadcasted_iota(jnp.int32, ang.shape, 1) % HEAD_DIM
        c, s = jnp.cos(ang), jnp.sin(ang)
        cos_ref[...] = jnp.where(lane < ROT_DIM, c, 1.0)
        lo_ref[...] = jnp.where(lane < ROT_DIM // 2, -s, 0.0)
        hi_ref[...] = jnp.where((lane >= ROT_DIM // 2) & (lane < ROT_DIM), s, 0.0)

    tab = jax.ShapeDtypeStruct((n_rows, LANES), F32)
    return pl.pallas_call(
        body,
        name=name,
        grid=(n_rows // tm,),
        in_specs=[pl.BlockSpec((tm, 1), lambda i: (i, 0)), pl.BlockSpec((1, LANES), lambda i: (0, 0))],
        out_specs=[pl.BlockSpec((tm, LANES), lambda i: (i, 0))] * 3,
        out_shape=[tab] * 3,
        compiler_params=_params(("parallel",)),
    )(pos_col, freq_lane)


def _rope(x, cos_t, lo_t, hi_t):
    half = ROT_DIM // 2
    return x * cos_t + pltpu.roll(x, LANES - half, 1) * lo_t + pltpu.roll(x, half, 1) * hi_t


def _rope_transposed(dy, cos_t, lo_t, hi_t):
    half = ROT_DIM // 2
    return dy * cos_t + pltpu.roll(dy * lo_t, half, 1) + pltpu.roll(dy * hi_t, LANES - half, 1)


def _att_masks(i, k0):
    q_pos = i * ATT_BLK + lax.broadcasted_iota(jnp.int32, (ATT_BLK, 2 * ATT_BLK), 0)
    k_pos = k0 + lax.broadcasted_iota(jnp.int32, (ATT_BLK, 2 * ATT_BLK), 1)
    dist = q_pos - k_pos
    return (dist >= 0) & (dist <= ATT_SPAN)


def _head_lane_masks():
    lane = lax.broadcasted_iota(jnp.int32, (1, LANES), 1)
    return lane < HEAD_DIM, lane >= HEAD_DIM


def _att_specs(gi, n_rows):
    col = lambda at: pl.BlockSpec((n_rows, LANES), lambda hp: (0, at + hp))
    qkv = [col(gi * 3), col(9), col(12)]
    tabs = [pl.BlockSpec((n_rows, LANES), lambda hp: (0, 0), pipeline_mode=pl.Buffered(1))] * 3
    head_in = col(0) if DILATIONS[gi] > 1 else pl.BlockSpec((n_rows, LANES), lambda hp: (0, hp),
                                                            pipeline_mode=pl.Buffered(1))
    return qkv, tabs, head_in, col(0)


def _sub_rows(d, n, r):
    return pl.ds(r, n, stride=d) if d > 1 else pl.ds(0, n)


def _att_load(q_ref, k_ref, v_ref, tabs, sub, qs, ks, vs):
    cos_t, lo_t, hi_t = tabs
    qs[...] = (_rope(q_ref[sub, :], cos_t, lo_t, hi_t) * (1.0 / math.sqrt(HEAD_DIM))).astype(BF16)
    ks[...] = _rope(k_ref[sub, :], cos_t, lo_t, hi_t).astype(BF16)
    vs[...] = v_ref[sub, :].astype(BF16)


def _att_fwd(proj, tabs, gi, name):
    n_rows = proj.shape[0]
    d = DILATIONS[gi]
    n = n_rows // d
    nb = n // ATT_BLK

    def body(q_ref, k_ref, v_ref, cos_ref, lo_ref, hi_ref, o_ref, l_ref, qs, ks, vs, o_s, l_s):
        m0, m1 = _head_lane_masks()

        def step(i, carry):
            k0 = pl.multiple_of(jnp.maximum(i - 1, 0) * ATT_BLK, ATT_BLK)
            q0 = pl.multiple_of(i * ATT_BLK, ATT_BLK)
            q = qs[pl.ds(q0, ATT_BLK), :]
            k = ks[pl.ds(k0, 2 * ATT_BLK), :]
            v = vs[pl.ds(k0, 2 * ATT_BLK), :]
            valid = _att_masks(i, k0)
            outs, lses = [], []
            for hm in (m0, m1):
                s = _nt(jnp.where(hm, q, jnp.zeros_like(q)), k)
                s = jnp.where(valid, s, -1e30)
                mx = jnp.max(s, axis=1, keepdims=True)
                p = jnp.exp(s - mx)
                den = jnp.sum(p, axis=1, keepdims=True)
                outs.append(_nn(p.astype(BF16), v) / den)
                lses.append(mx + jnp.log(den))
            o_s[pl.ds(q0, ATT_BLK), :] = jnp.where(m0, outs[0], outs[1])
            l_s[pl.ds(q0, ATT_BLK), :] = jnp.where(m0, lses[0], lses[1])
            return carry

        for r in range(d):
            sub = _sub_rows(d, n, r)
            _att_load(q_ref, k_ref, v_ref, (cos_ref[sub, :], lo_ref[sub, :], hi_ref[sub, :]), sub, qs, ks, vs)
            lax.fori_loop(0, nb, step, 0, unroll=min(nb, 4))
            o_ref[sub, :] = o_s[...]
            l_ref[sub, :] = l_s[...]

    qkv, tab_specs, _, head_out = _att_specs(gi, n_rows)
    out = jax.ShapeDtypeStruct((n_rows, KV_WIDTH), F32)
    return pl.pallas_call(
        body,
        name=name,
        grid=(3,),
        in_specs=qkv + tab_specs,
        out_specs=[head_out, head_out],
        out_shape=[out, out],
        scratch_shapes=[pltpu.VMEM((n, LANES), BF16)] * 3 + [pltpu.VMEM((n, LANES), F32)] * 2,
        compiler_params=_params(("parallel",)),
    )(proj, proj, proj, *tabs)


def _att_bwd(proj, tabs, o, l, do, dl, gi, name):
    n_rows = proj.shape[0]
    d = DILATIONS[gi]
    n = n_rows // d
    nb = n // ATT_BLK

    def body(q_ref, k_ref, v_ref, cos_ref, lo_ref, hi_ref, o_ref, l_ref, do_ref, dl_ref,
             dq_ref, dk_ref, dv_ref, qs, ks, vs, dq_s, dk_acc, dv_acc, *gathered):
        m0, m1 = _head_lane_masks()
        o_s, l_s, do_s, dl_s = gathered if d > 1 else (o_ref, l_ref, do_ref, dl_ref)

        def step(i, carry):
            k0 = pl.multiple_of(jnp.maximum(i - 1, 0) * ATT_BLK, ATT_BLK)
            q0 = pl.multiple_of(i * ATT_BLK, ATT_BLK)
            rows = pl.ds(q0, ATT_BLK)
            keys = pl.ds(k0, 2 * ATT_BLK)
            q, k, v = qs[rows, :], ks[keys, :], vs[keys, :]
            d_o, lse = do_s[rows, :], l_s[rows, :]
            o_do = o_s[rows, :] * d_o
            d_l = dl_s[rows, :]
            valid = _att_masks(i, k0)
            dq = jnp.zeros((ATT_BLK, LANES), F32)
            dk = jnp.zeros((2 * ATT_BLK, LANES), F32)
            dv = jnp.zeros((2 * ATT_BLK, LANES), F32)
            for hm in (m0, m1):
                qh, kh = jnp.where(hm, q, jnp.zeros_like(q)), jnp.where(hm, k, jnp.zeros_like(k))
                doh = jnp.where(hm, d_o, 0.0).astype(BF16)
                lse_h = jnp.max(jnp.where(hm, lse, -1e30), axis=1, keepdims=True)
                delta = jnp.sum(jnp.where(hm, o_do, 0.0), axis=1, keepdims=True)
                dlse = jnp.sum(jnp.where(hm, d_l, 0.0), axis=1, keepdims=True)
                s = jnp.where(valid, _nt(qh, k), -1e30)
                p = jnp.exp(s - lse_h)
                dv = dv + _tn(p.astype(BF16), doh)
                ds = (p * (_nt(doh, v) - delta + dlse)).astype(BF16)
                dq = dq + _nn(ds, kh)
                dk = dk + _tn(ds, qh)
            dq_s[rows, :] = dq * (1.0 / math.sqrt(HEAD_DIM))
            dk_acc[keys, :] += dk
            dv_acc[keys, :] += dv
            return carry

        for r in range(d):
            sub = _sub_rows(d, n, r)
            rot = (cos_ref[sub, :], lo_ref[sub, :], hi_ref[sub, :])
            _att_load(q_ref, k_ref, v_ref, rot, sub, qs, ks, vs)
            if d > 1:
                for dst, src in zip(gathered, (o_ref, l_ref, do_ref, dl_ref)):
                    dst[...] = src[sub, :]
            dk_acc[...] = jnp.zeros_like(dk_acc)
            dv_acc[...] = jnp.zeros_like(dv_acc)
            lax.fori_loop(0, nb, step, 0, unroll=min(nb, 4))
            dq_ref[sub, :] = _rope_transposed(dq_s[...], *rot)
            dk_ref[sub, :] = _rope_transposed(dk_acc[...], *rot)
            dv_ref[sub, :] = dv_acc[...]

    qkv, tab_specs, head_in, head_out = _att_specs(gi, n_rows)
    out = jax.ShapeDtypeStruct((n_rows, KV_WIDTH), F32)
    sub_f32 = pltpu.VMEM((n, LANES), F32)
    return pl.pallas_call(
        body,
        name=name,
        grid=(3,),
        in_specs=qkv + tab_specs + [head_in] * 4,
        out_specs=[head_out] * 3,
        out_shape=[out] * 3,
        scratch_shapes=[pltpu.VMEM((n, LANES), BF16)] * 3 + [sub_f32] * (3 if d == 1 else 7),
        compiler_params=_params(("parallel",)),
    )(proj, proj, proj, *tabs, o, l, do, dl)


def _expand_np():
    e = np.zeros((SSM_N, SSM_N * SSM_P), np.float32)
    for nn in range(SSM_N):
        e[nn, nn * SSM_P:(nn + 1) * SSM_P] = 1.0
    return e


def _ssm_prep_math(a_re, a_im, log_dt, b_re, b_im, expand):
    dt = jnp.exp(log_dt)
    mag = jnp.exp(a_re * dt)
    ab_re, ab_im = mag * jnp.cos(a_im * dt), mag * jnp.sin(a_im * dt)
    den = a_re * a_re + a_im * a_im
    num_re, num_im = ab_re - 1.0, ab_im
    co_re = (num_re * a_re + num_im * a_im) / den
    co_im = (num_im * a_re - num_re * a_im) / den
    hi = lax.Precision.HIGHEST
    co_re_x = jnp.dot(co_re, expand, precision=hi, preferred_element_type=F32)
    co_im_x = jnp.dot(co_im, expand, precision=hi, preferred_element_type=F32)
    bb_re = co_re_x * b_re - co_im_x * b_im
    bb_im = co_re_x * b_im + co_im_x * b_re
    return ab_re, ab_im, bb_re, bb_im


def _ssm_prep(a_re, a_im, log_dt, b_re, b_im, expand, name):
    def body(ar, ai, ld, br, bi, ex, o0, o1, o2, o3):
        outs = _ssm_prep_math(ar[...], ai[...], ld[...], br[...], bi[...], ex[...])
        for ref, v in zip((o0, o1, o2, o3), outs):
            ref[...] = v

    gn = jax.ShapeDtypeStruct((SSM_G, SSM_N), F32)
    gnp = jax.ShapeDtypeStruct((SSM_G, SSM_N * SSM_P), F32)
    return pl.pallas_call(body, name=name, out_shape=[gn, gn, gnp, gnp], compiler_params=_params())(
        a_re, a_im, log_dt, b_re, b_im, expand)


def _ssm_prep_bwd(a_re, a_im, log_dt, b_re, b_im, expand, cts, name):
    def body(ar, ai, ld, br, bi, ex, c0, c1, c2, c3, o0, o1, o2, o3, o4):
        ex_v = ex[...]
        _, vjp = jax.vjp(lambda *p: _ssm_prep_math(*p, ex_v), ar[...], ai[...], ld[...], br[...], bi[...])
        for ref, v in zip((o0, o1, o2, o3, o4), vjp((c0[...], c1[...], c2[...], c3[...]))):
            ref[...] = v

    gn = jax.ShapeDtypeStruct((SSM_G, SSM_N), F32)
    gnp = jax.ShapeDtypeStruct((SSM_G, SSM_N * SSM_P), F32)
    g1 = jax.ShapeDtypeStruct((SSM_G, 1), F32)
    return pl.pallas_call(body, name=name, out_shape=[gn, gn, g1, gnp, gnp], compiler_params=_params())(
        a_re, a_im, log_dt, b_re, b_im, expand, *cts)


def _block_diag_in(bb):
    t = bb.reshape(SSM_TILES, 8, SSM_N, SSM_P).transpose(0, 1, 3, 2)
    eye = jnp.eye(8, dtype=bb.dtype)
    return (t[:, :, :, None, :] * eye[None, :, None, :, None]).reshape(SSM_TILES, LANES, SSM_TILE_GN)


def _block_diag_in_grad(dblk):
    t = dblk.reshape(SSM_TILES, 8, SSM_P, 8, SSM_N)
    t = jnp.einsum("tapbn,ab->tapn", t, jnp.eye(8, dtype=dblk.dtype))
    return t.transpose(0, 1, 3, 2).reshape(SSM_G, SSM_N, SSM_P)


def _block_diag_out(cm):
    t = cm.reshape(SSM_TILES, 8, SSM_P, SSM_N).transpose(0, 1, 3, 2)
    eye = jnp.eye(8, dtype=cm.dtype)
    return (t[:, :, :, None, :] * eye[None, :, None, :, None]).reshape(SSM_TILES, SSM_TILE_GN, LANES)


def _block_diag_out_grad(dblk):
    t = dblk.reshape(SSM_TILES, 8, SSM_N, 8, SSM_P)
    t = jnp.einsum("tanbp,ab->tanp", t, jnp.eye(8, dtype=dblk.dtype))
    return t.transpose(0, 1, 3, 2).reshape(SSM_G, SSM_P, SSM_N)


def _cmul_add(a_re, a_im, s_re, s_im, b_re, b_im):
    return a_re * s_re - a_im * s_im + b_re, a_re * s_im + a_im * s_re + b_im


def _lane_tile_specs(first_tile, index):
    return [pl.BlockSpec((SSM_CHUNK, LANES), lambda c, t=t: (index(c), first_tile + t)) for t in range(SSM_TILES)]


def _ssm_load_rows(src_refs, dst):
    for t in range(SSM_TILES):
        for i in range(SSM_SEG_LEN):
            dst[i * SSM_SEGS:(i + 1) * SSM_SEGS, t * LANES:(t + 1) * LANES] = (
                src_refs[t][pl.ds(i, SSM_SEGS, stride=SSM_SEG_LEN), :])


def _ssm_store_rows(src, dst_refs):
    for t in range(SSM_TILES):
        for i in range(SSM_SEG_LEN):
            dst_refs[t][pl.ds(i, SSM_SEGS, stride=SSM_SEG_LEN), :] = (
                src[i * SSM_SEGS:(i + 1) * SSM_SEGS, t * LANES:(t + 1) * LANES])


def _ssm_powers(ab_re_ref, ab_im_ref, pw_re, pw_im):
    a_re, a_im = ab_re_ref[...], ab_im_ref[...]
    p_re, p_im = a_re, a_im
    for i in range(SSM_SEG_LEN):
        pw_re[i:i + 1, :] = p_re
        pw_im[i:i + 1, :] = p_im
        p_re, p_im = _cmul_add(a_re, a_im, p_re, p_im, 0.0, 0.0)


def _ssm_input_proj(u_s, bblk_re_ref, bblk_im_ref, s_re, s_im):
    for t in range(SSM_TILES):
        ub = u_s[:, t * LANES:(t + 1) * LANES].astype(BF16)
        cols = slice(t * SSM_TILE_GN, (t + 1) * SSM_TILE_GN)
        s_re[:, cols] = _nn(ub, bblk_re_ref[t])
        s_im[:, cols] = _nn(ub, bblk_im_ref[t])


def _ssm_scan(ab_re_ref, ab_im_ref, s_re, s_im, init_re, init_im, conj, reverse):
    sign = -1.0 if conj else 1.0
    for t in range(SSM_TILES):
        cols = slice(t * SSM_TILE_GN, (t + 1) * SSM_TILE_GN)
        a_re = jnp.broadcast_to(ab_re_ref[:, cols], (SSM_SEGS, SSM_TILE_GN))
        a_im = jnp.broadcast_to(ab_im_ref[:, cols], (SSM_SEGS, SSM_TILE_GN)) * sign
        if init_re is None:
            st = (jnp.zeros((SSM_SEGS, SSM_TILE_GN), F32),) * 2
        else:
            st = (init_re[:, cols], init_im[:, cols])

        def step(i, st, cols=cols, a_re=a_re, a_im=a_im):
            idx = (SSM_SEG_LEN - 1 - i) if reverse else i
            rows = pl.ds(pl.multiple_of(idx * SSM_SEGS, SSM_SEGS), SSM_SEGS)
            n_re, n_im = _cmul_add(a_re, a_im, st[0], st[1], s_re[rows, cols], s_im[rows, cols])
            s_re[rows, cols] = n_re
            s_im[rows, cols] = n_im
            return n_re, n_im
        lax.fori_loop(0, SSM_SEG_LEN, step, st, unroll=4)


def _ssm_fixup(pw_re, pw_im, s_re, s_im, cin_re, cin_im, conj, reverse):
    sign = -1.0 if conj else 1.0
    c_re, c_im = cin_re[...], cin_im[...]

    def step(i, c):
        k = (SSM_SEG_LEN - 1 - i) if reverse else i
        rows = pl.ds(pl.multiple_of(i * SSM_SEGS, SSM_SEGS), SSM_SEGS)
        p_re = jnp.broadcast_to(pw_re[pl.ds(k, 1), :], (SSM_SEGS, SSM_GN))
        p_im = jnp.broadcast_to(pw_im[pl.ds(k, 1), :], (SSM_SEGS, SSM_GN)) * sign
        n_re, n_im = _cmul_add(p_re, p_im, c_re, c_im, s_re[rows, :], s_im[rows, :])
        s_re[rows, :] = n_re
        s_im[rows, :] = n_im
        return c
    lax.fori_loop(0, SSM_SEG_LEN, step, 0)


def _ssm_fwd(proj, ab_re, ab_im, bblk_re, bblk_im, cblk_re, cblk_im, d_row, name):
    n_rows = proj.shape[0]
    nchunk = n_rows // SSM_CHUNK
    last = SSM_SEG_LEN - 1
    nt = SSM_TILES

    def body(*refs):
        u_ref, y_ref = refs[:nt], refs[nt + 7:2 * nt + 7]
        ar_ref, ai_ref, br_ref, bi_ref, cr_ref, ci_ref, d_ref = refs[nt:nt + 7]
        cin_re_ref, cin_im_ref, u_s, s_re, s_im, pw_re, pw_im, st_re, st_im = refs[2 * nt + 7:]

        @pl.when(pl.program_id(0) == 0)
        def _():
            _ssm_powers(ar_ref, ai_ref, pw_re, pw_im)
            st_re[...] = jnp.zeros_like(st_re)
            st_im[...] = jnp.zeros_like(st_im)

        _ssm_load_rows(u_ref, u_s)
        _ssm_input_proj(u_s, br_ref, bi_ref, s_re, s_im)
        _ssm_scan(ar_ref, ai_ref, s_re, s_im, None, None, conj=False, reverse=False)
        p_re, p_im = pw_re[last:last + 1, :], pw_im[last:last + 1, :]
        c_re, c_im = st_re[...], st_im[...]
        for j in range(SSM_SEGS):
            cin_re_ref[j:j + 1, :] = c_re
            cin_im_ref[j:j + 1, :] = c_im
            row = last * SSM_SEGS + j
            c_re, c_im = _cmul_add(p_re, p_im, c_re, c_im, s_re[row:row + 1, :], s_im[row:row + 1, :])
        st_re[...] = c_re
        st_im[...] = c_im
        _ssm_fixup(pw_re, pw_im, s_re, s_im, cin_re_ref, cin_im_ref, conj=False, reverse=False)
        for t in range(SSM_TILES):
            cols = slice(t * SSM_TILE_GN, (t + 1) * SSM_TILE_GN)
            lanes = slice(t * LANES, (t + 1) * LANES)
            y = _nn(s_re[:, cols].astype(BF16), cr_ref[t]) - _nn(s_im[:, cols].astype(BF16), ci_ref[t])
            u_s[:, lanes] = y + d_ref[:, lanes] * u_s[:, lanes]
        _ssm_store_rows(u_s, y_ref)

    whole2 = lambda a: pl.BlockSpec(a.shape, lambda c: (0, 0))
    whole3 = lambda a: pl.BlockSpec(a.shape, lambda c: (0, 0, 0))
    seg = pl.BlockSpec((SSM_SEGS, SSM_GN), lambda c: (c, 0))
    seg_shape = jax.ShapeDtypeStruct((nchunk * SSM_SEGS, SSM_GN), F32)
    res = pl.pallas_call(
        body,
        name=name,
        grid=(nchunk,),
        in_specs=_lane_tile_specs((IN_WIDTH - SSM_WIDTH) // LANES, lambda c: c) + [
            whole2(ab_re), whole2(ab_im), whole3(bblk_re), whole3(bblk_im), whole3(cblk_re), whole3(cblk_im),
            whole2(d_row)],
        out_specs=[pl.BlockSpec((SSM_CHUNK, LANES), lambda c: (c, 0))] * nt + [seg, seg],
        out_shape=[jax.ShapeDtypeStruct((n_rows, LANES), F32)] * nt + [seg_shape, seg_shape],
        scratch_shapes=[pltpu.VMEM((SSM_CHUNK, SSM_WIDTH), F32), pltpu.VMEM((SSM_CHUNK, SSM_GN), F32),
                        pltpu.VMEM((SSM_CHUNK, SSM_GN), F32), pltpu.VMEM((SSM_SEG_LEN, SSM_GN), F32),
                        pltpu.VMEM((SSM_SEG_LEN, SSM_GN), F32), pltpu.VMEM((1, SSM_GN), F32),
                        pltpu.VMEM((1, SSM_GN), F32)],
        compiler_params=_params(("arbitrary",)),
    )(*[proj] * nt, ab_re, ab_im, bblk_re, bblk_im, cblk_re, cblk_im, d_row)
    return res[:nt], res[nt], res[nt + 1]


def _ssm_bwd(proj, dy, cin_re, cin_im, ab_re, ab_im, bblk_re, bblk_im, cblk_re, cblk_im, d_row, name):
    n_rows = proj.shape[0]
    nchunk = n_rows // SSM_CHUNK
    nt = SSM_TILES

    def body(*refs):
        u_ref, dy_ref, du_ref = refs[:nt], refs[nt:2 * nt], refs[2 * nt + 9:3 * nt + 9]
        cin_re_ref, cin_im_ref, ar_ref, ai_ref, br_ref, bi_ref, cr_ref, ci_ref, d_ref = refs[2 * nt:2 * nt + 9]
        (dar_ref, dai_ref, dbr_ref, dbi_ref, dcr_ref, dci_ref, dd_ref,
         u_s, dy_s, s_re, s_im, q_re, q_im, pw_re, pw_im, qst_re, qst_im, qin_re, qin_im) = refs[3 * nt + 9:]

        @pl.when(pl.program_id(0) == 0)
        def _():
            _ssm_powers(ar_ref, ai_ref, pw_re, pw_im)
            qst_re[...] = jnp.zeros_like(qst_re)
            qst_im[...] = jnp.zeros_like(qst_im)
            for ref in (dar_ref, dai_ref, dbr_ref, dbi_ref, dcr_ref, dci_ref, dd_ref):
                ref[...] = jnp.zeros_like(ref)

        _ssm_load_rows(u_ref, u_s)
        _ssm_load_rows(dy_ref, dy_s)
        _ssm_input_proj(u_s, br_ref, bi_ref, s_re, s_im)
        _ssm_scan(ar_ref, ai_ref, s_re, s_im, cin_re_ref, cin_im_ref, conj=False, reverse=False)
        for t in range(SSM_TILES):
            cols = slice(t * SSM_TILE_GN, (t + 1) * SSM_TILE_GN)
            dyb = dy_s[:, t * LANES:(t + 1) * LANES].astype(BF16)
            q_re[:, cols] = _nt(dyb, cr_ref[t])
            q_im[:, cols] = -_nt(dyb, ci_ref[t])
            dcr_ref[t] += _tn(s_re[:, cols].astype(BF16), dyb)
            dci_ref[t] -= _tn(s_im[:, cols].astype(BF16), dyb)
        _ssm_scan(ar_ref, ai_ref, q_re, q_im, None, None, conj=True, reverse=True)
        last = SSM_SEG_LEN - 1
        p_re, p_im = pw_re[last:last + 1, :], -pw_im[last:last + 1, :]
        c_re, c_im = qst_re[...], qst_im[...]
        for j in reversed(range(SSM_SEGS)):
            qin_re[j:j + 1, :] = c_re
            qin_im[j:j + 1, :] = c_im
            c_re, c_im = _cmul_add(p_re, p_im, c_re, c_im, q_re[j:j + 1, :], q_im[j:j + 1, :])
        qst_re[...] = c_re
        qst_im[...] = c_im
        _ssm_fixup(pw_re, pw_im, q_re, q_im, qin_re, qin_im, conj=True, reverse=True)
        for t in range(SSM_TILES):
            cols = slice(t * SSM_TILE_GN, (t + 1) * SSM_TILE_GN)

            def step(i, acc, cols=cols):
                rows = pl.ds(pl.multiple_of(i * SSM_SEGS, SSM_SEGS), SSM_SEGS)
                prev = pl.ds(pl.multiple_of((i - 1) * SSM_SEGS, SSM_SEGS), SSM_SEGS)
                qr, qi = q_re[rows, cols], q_im[rows, cols]
                sr, si = s_re[prev, cols], s_im[prev, cols]
                return acc[0] + qr * sr + qi * si, acc[1] + qi * sr - qr * si

            qr, qi = q_re[0:SSM_SEGS, cols], q_im[0:SSM_SEGS, cols]
            sr, si = cin_re_ref[:, cols], cin_im_ref[:, cols]
            acc = lax.fori_loop(1, SSM_SEG_LEN, step, (qr * sr + qi * si, qi * sr - qr * si))
            dar_ref[:, cols] += jnp.sum(acc[0], axis=0, keepdims=True)
            dai_ref[:, cols] += jnp.sum(acc[1], axis=0, keepdims=True)
        for t in range(SSM_TILES):
            cols = slice(t * SSM_TILE_GN, (t + 1) * SSM_TILE_GN)
            lanes = slice(t * LANES, (t + 1) * LANES)
            qrb, qib = q_re[:, cols].astype(BF16), q_im[:, cols].astype(BF16)
            u_t, dy_t = u_s[:, lanes], dy_s[:, lanes]
            ub = u_t.astype(BF16)
            dbr_ref[t] += _tn(ub, qrb)
            dbi_ref[t] += _tn(ub, qib)
            dd_ref[:, lanes] += jnp.sum(dy_t * u_t, axis=0, keepdims=True)
            u_s[:, lanes] = _nt(qrb, br_ref[t]) + _nt(qib, bi_ref[t]) + dy_t * d_ref[:, lanes]
        _ssm_store_rows(u_s, du_ref)

    whole2 = lambda a: pl.BlockSpec(a.shape, lambda c: (0, 0))
    whole3 = lambda a: pl.BlockSpec(a.shape, lambda c: (0, 0, 0))
    back = lambda c: nchunk - 1 - c
    seg = pl.BlockSpec((SSM_SEGS, SSM_GN), lambda c: (back(c), 0))
    gn_row = jax.ShapeDtypeStruct((1, SSM_GN), F32)
    b_shape = jax.ShapeDtypeStruct((SSM_TILES, LANES, SSM_TILE_GN), F32)
    c_shape = jax.ShapeDtypeStruct((SSM_TILES, SSM_TILE_GN, LANES), F32)
    d_shape = jax.ShapeDtypeStruct((1, SSM_WIDTH), F32)
    big = pltpu.VMEM((SSM_CHUNK, SSM_GN), F32)
    res = pl.pallas_call(
        body,
        name=name,
        grid=(nchunk,),
        in_specs=_lane_tile_specs((IN_WIDTH - SSM_WIDTH) // LANES, back) + _lane_tile_specs(0, back) + [
            seg, seg, whole2(ab_re), whole2(ab_im), whole3(bblk_re), whole3(bblk_im), whole3(cblk_re),
            whole3(cblk_im), whole2(d_row)],
        out_specs=[pl.BlockSpec((SSM_CHUNK, LANES), lambda c: (back(c), 0))] * nt + [
            whole2(ab_re), whole2(ab_im), whole3(bblk_re), whole3(bblk_im), whole3(cblk_re), whole3(cblk_im),
            whole2(d_row)],
        out_shape=[jax.ShapeDtypeStruct((n_rows, LANES), F32)] * nt + [gn_row, gn_row, b_shape, b_shape, c_shape,
                                                                       c_shape, d_shape],
        scratch_shapes=[pltpu.VMEM((SSM_CHUNK, SSM_WIDTH), F32), pltpu.VMEM((SSM_CHUNK, SSM_WIDTH), F32),
                        big, big, big, big,
                        pltpu.VMEM((SSM_SEG_LEN, SSM_GN), F32), pltpu.VMEM((SSM_SEG_LEN, SSM_GN), F32),
                        pltpu.VMEM((1, SSM_GN), F32), pltpu.VMEM((1, SSM_GN), F32),
                        pltpu.VMEM((SSM_SEGS, SSM_GN), F32), pltpu.VMEM((SSM_SEGS, SSM_GN), F32)],
        compiler_params=_params(("arbitrary",)),
    )(*[proj] * nt, *[dy] * nt, cin_re, cin_im, ab_re, ab_im, bblk_re, bblk_im, cblk_re, cblk_im, d_row)
    return (res[:nt], *res[nt:])


def _mlp_fwd(h2, w1, w2, tm, tf, name):
    n_rows, dm = h2.shape
    dff = w1.shape[1]

    def body(h_ref, w1_ref, w2_ref, a_ref, y_ref):
        a = _nn(h_ref[...], w1_ref[...])
        a_ref[...] = a.astype(BF16)
        r = jnp.maximum(a, 0.0)
        part = _nn((r * r).astype(BF16), w2_ref[...])
        j = pl.program_id(1)

        @pl.when(j == 0)
        def _():
            y_ref[...] = part

        @pl.when(j > 0)
        def _():
            y_ref[...] += part

    return pl.pallas_call(
        body,
        name=name,
        grid=(n_rows // tm, dff // tf),
        in_specs=[pl.BlockSpec((tm, dm), lambda i, j: (i, 0)), pl.BlockSpec((dm, tf), lambda i, j: (0, j)),
                  pl.BlockSpec((tf, dm), lambda i, j: (j, 0))],
        out_specs=[pl.BlockSpec((tm, tf), lambda i, j: (i, j)), pl.BlockSpec((tm, dm), lambda i, j: (i, 0))],
        out_shape=[jax.ShapeDtypeStruct((n_rows, dff), BF16), jax.ShapeDtypeStruct((n_rows, dm), F32)],
        compiler_params=_params(("parallel", "arbitrary")),
    )(h2, w1, w2)


def _mlp_bwd(dy, h2, a, w2, tm, tf, name):
    n_rows, dm = h2.shape
    dff = a.shape[1]
    per_chip = dff // N_CHIPS // tf

    def body(dy_ref, h_ref, a_ref, w2_ref, da_ref, dw2_ref, dw1_ref):
        dyb = dy_ref[...]
        r = jnp.maximum(a_ref[...].astype(F32), 0.0)
        da = (_nt(dyb, w2_ref[...]) * (2.0 * r)).astype(BF16)
        da_ref[...] = da
        p2 = _tn((r * r).astype(BF16), dyb)
        p1 = _tn(h_ref[...], da)
        i = pl.program_id(1)

        @pl.when(i == 0)
        def _():
            dw2_ref[...] = p2
            dw1_ref[...] = p1

        @pl.when(i > 0)
        def _():
            dw2_ref[...] += p2
            dw1_ref[...] += p1

    return pl.pallas_call(
        body,
        name=name,
        grid=(dff // tf, n_rows // tm),
        in_specs=[pl.BlockSpec((tm, dm), lambda j, i: (i, 0)), pl.BlockSpec((tm, dm), lambda j, i: (i, 0)),
                  pl.BlockSpec((tm, tf), lambda j, i: (i, j)), pl.BlockSpec((tf, dm), lambda j, i: (j, 0))],
        out_specs=[pl.BlockSpec((tm, tf), lambda j, i: (i, j)), pl.BlockSpec((tf, dm), lambda j, i: (j, 0)),
                   pl.BlockSpec((None, dm, tf), lambda j, i: (j // per_chip, 0, j % per_chip))],
        out_shape=[jax.ShapeDtypeStruct((n_rows, dff), BF16), jax.ShapeDtypeStruct((dff, dm), F32),
                   jax.ShapeDtypeStruct((N_CHIPS, dm, dff // N_CHIPS), F32)],
        compiler_params=_params(("parallel", "arbitrary")),
    )(dy, h2, a, w2)


def _local_step(x, pos_col, mod, target, wts, small, hooks=None):
    n_rows = x.shape[0]
    sh1, sc1, gt1, sh2, sc2, gt2 = (mod[:, i * D_MODEL:(i + 1) * D_MODEL] for i in range(N_MOD))
    tm = 256
    d_acc = (1, D_MODEL)

    (h1,) = _rowwise(lambda r, c: [_mod_norm(r[0], *c)], [x], [small["g_pre_mix"], sc1, sh1],
                     [(D_MODEL, BF16)], [], tm, "pre_mix_fwd")
    proj = _matmul(h1, wts["w_in"], "nn", F32, 1024, 1408, 2048, "in_proj")

    freqs = ROPE_THETA ** (-jnp.arange(0, ROT_DIM, 2, dtype=F32) / ROT_DIM)
    freq_lane = jnp.tile(freqs, LANES // (ROT_DIM // 2))[None, :]
    tabs = _rope_tables(pos_col, freq_lane, "rope_tables")
    att = [_att_fwd(proj, tabs, gi, f"att_fwd_{gi}") for gi in range(3)]

    expand = jnp.asarray(_expand_np())
    b_re2, b_im2 = small["ssm_b_re"].reshape(SSM_G, -1), small["ssm_b_im"].reshape(SSM_G, -1)
    log_dt = small["ssm_log_dt"].reshape(SSM_G, 1)
    prep_in = (small["ssm_a_re"], small["ssm_a_im"], log_dt, b_re2, b_im2, expand)
    ab_re, ab_im, bb_re, bb_im = _ssm_prep(*prep_in, "ssm_prep")
    ab_re_row, ab_im_row = ab_re.reshape(1, SSM_GN), ab_im.reshape(1, SSM_GN)
    bblk = [_block_diag_in(t.reshape(SSM_G, SSM_N, SSM_P)).astype(BF16) for t in (bb_re, bb_im)]
    cblk = [_block_diag_out(small[k]).astype(BF16) for k in ("ssm_c_re", "ssm_c_im")]
    d_row = small["ssm_d"].reshape(1, SSM_WIDTH)
    if hooks is not None:
        d_row = _tie(d_row, hooks.mixer.arrived(att[2][0]))
    y_tiles, cin_re, cin_im = _ssm_fwd(proj, ab_re_row, ab_im_row, *bblk, *cblk, d_row, "ssm_fwd")
    y_tiles = list(y_tiles)
    n_mix = 6 + SSM_TILES

    def mixers_out(r, c):
        w_glu, b_glu, g_att, g_ssm = c
        att_n = _att_mix(*r[:6], g_att)
        y2 = _gelu(jnp.concatenate(r[6:n_mix], axis=1))
        z = _nn(y2.astype(BF16), w_glu) + b_glu
        return [jnp.concatenate([att_n.astype(BF16), _glu_out(y2, z, g_ssm).astype(BF16)], axis=1)]

    att_rows = [a[0] for a in att] + [a[1] for a in att]
    if hooks is not None:
        wts = {**wts, **hooks.mixer.weights(y_tiles[0])}
    mix_consts = [wts["w_glu"], small["b_glu"], small["g_attn_out"], small["g_ssm_out"]]
    (cat,) = _rowwise(mixers_out, att_rows + y_tiles, mix_consts, [(OUT_IN_WIDTH, BF16)], [], tm, "mixers_out_fwd")
    mix = _matmul(cat, wts["w_out"], "nn", F32, 1024, 1024, 1280, "out_proj")

    post_consts = [small["g_post_mix"], gt1, small["g_pre_mlp"], sc2, sh2]
    if hooks is not None:
        post_consts[0] = _tie(post_consts[0], hooks.mlp.arrived(mix))
    x1, h2 = _rowwise(lambda r, c: list(_post_mix(r[0], r[1], *c)), [x, mix], post_consts,
                      [(D_MODEL, F32), (D_MODEL, BF16)], [], tm, "post_mix_fwd")
    if hooks is not None:
        wts = {**wts, **hooks.mlp.weights(h2)}
    w_mlp_in, w_mlp_out = wts["w_mlp_in"], wts["w_mlp_out"]
    a_mlp, y_mlp = _mlp_fwd(h2, w_mlp_in, w_mlp_out, 1024, 1024, "mlp_fwd")

    def loss_head(r, c):
        x1_v, y_v, t_v = r
        g, gt = c
        fn = lambda y_, g_, gt_: gt_ * _rms(y_, g_)
        out, vjp = jax.vjp(fn, y_v, g, gt)
        err = x1_v + out - t_v
        dx2 = err * (1.0 / D_MODEL)
        dy, dg, dgt = vjp(dx2)
        loss = 0.5 * jnp.sum(jnp.sum(err * err, axis=1, keepdims=True), axis=0, keepdims=True) * (1.0 / D_MODEL)
        return [dx2, dy, loss, dg, dgt]

    dx2, dy_mlp, loss, dg_post_mlp, dgt2 = _rowwise(
        loss_head, [x1, y_mlp, target], [small["g_post_mlp"], gt2],
        [(D_MODEL, F32), (D_MODEL, BF16)], [(1, 1), d_acc, d_acc], tm, "loss_head")

    da_mlp, dw_mlp_out, dw_mlp_in = _mlp_bwd(dy_mlp, h2, a_mlp, w_mlp_out, 1024, 512, "mlp_bwd")
    dw_mlp_out = dw_mlp_out.reshape(dw_mlp_in.shape)
    sent = None if hooks is None else hooks.mlp_grads_to_sibling(dw_mlp_in, dw_mlp_out)
    dh2 = _matmul(da_mlp, w_mlp_in, "nt", F32, 1024, 1024, 2048, "mlp_in_bwd", after=sent)
    sent = None if hooks is None else hooks.mlp_grads_to_chips(dh2)

    def post_mix_bwd(r, c):
        x_v, mix_v, dx1_v, dh2_v = r
        _, vjp = jax.vjp(_post_mix, x_v, mix_v, *c)
        return list(vjp((dx1_v, dh2_v)))

    post_consts_bwd = post_consts if sent is None else [_tie(post_consts[0], sent)] + post_consts[1:]
    dx_a, dmix, dg_post_mix, dgt1, dg_pre_mlp, dsc2, dsh2 = _rowwise(
        post_mix_bwd, [x, mix, dx2, dh2], post_consts_bwd, [(D_MODEL, F32), (D_MODEL, BF16)], [d_acc] * 5, tm,
        "post_mix_bwd")

    dcat = _matmul(dmix, wts["w_out"], "nt", F32, 1024, 1280, 2048, "out_proj_bwd")
    dw_out = _matmul(cat, dmix, "tn", F32, 1280, 1024, 1024, "out_proj_wgrad")

    def mixers_out_bwd(r, c):
        w_glu, b_glu, g_att, g_ssm = c
        dcat_v = r[n_mix]
        _, vjp_att = jax.vjp(_att_mix, *r[:6], g_att)
        *d_ol, dg_att = vjp_att(dcat_v[:, :KV_WIDTH])
        y2, vjp_gelu = jax.vjp(_gelu, jnp.concatenate(r[6:n_mix], axis=1))
        y2b = y2.astype(BF16)
        z = _nn(y2b, w_glu) + b_glu
        _, vjp_glu = jax.vjp(_glu_out, y2, z, g_ssm)
        dy2, dz, dg_ssm = vjp_glu(dcat_v[:, KV_WIDTH:])
        dzb = dz.astype(BF16)
        (dy,) = vjp_gelu(dy2 + _nt(dzb, w_glu))
        return d_ol + [dy, dg_att, _tn(y2b, dzb), jnp.sum(dz, axis=0, keepdims=True), dg_ssm]

    *d_att, dy_ssm, dg_attn_out, dw_glu, db_glu, dg_ssm_out = _rowwise(
        mixers_out_bwd, att_rows + y_tiles + [dcat], mix_consts,
        [(KV_WIDTH, F32)] * 6 + [(SSM_WIDTH, F32)],
        [(1, KV_WIDTH), (SSM_WIDTH, SSM_WIDTH), (1, SSM_WIDTH), (1, SSM_WIDTH)], tm, "mixers_out_bwd")

    du_tiles, dab_re, dab_im, dbblk_re, dbblk_im, dcblk_re, dcblk_im, dd_row = _ssm_bwd(
        proj, dy_ssm, cin_re, cin_im, ab_re_row, ab_im_row, *bblk, *cblk, d_row, "ssm_bwd")
    prep_cts = (dab_re.reshape(SSM_G, SSM_N), dab_im.reshape(SSM_G, SSM_N),
                _block_diag_in_grad(dbblk_re).reshape(SSM_G, -1), _block_diag_in_grad(dbblk_im).reshape(SSM_G, -1))
    da_re, da_im, dlog_dt, db_re, db_im = _ssm_prep_bwd(*prep_in, prep_cts, "ssm_prep_bwd")

    dqkv = [_att_bwd(proj, tabs, att[gi][0], att[gi][1], d_att[gi], d_att[3 + gi], gi, f"att_bwd_{gi}")
            for gi in range(3)]

    def gather_dproj(r, c):
        dq = [r[0], r[3], r[6]]
        dk = r[1] + r[4] + r[7]
        dv = r[2] + r[5] + r[8]
        return [jnp.concatenate([t.astype(BF16) for t in dq + [dk, dv] + r[9:]], axis=1)]

    (dproj,) = _rowwise(gather_dproj, [t for g in dqkv for t in g] + list(du_tiles), [], [(IN_WIDTH, BF16)], [], tm,
                        "gather_dproj")
    dh1 = _matmul(dproj, wts["w_in"], "nt", F32, 1024, 1024, 2816, "in_proj_bwd")
    dw_in = _matmul(h1, dproj, "tn", F32, 1024, 1408, 1024, "in_proj_wgrad")

    def pre_mix_bwd(r, c):
        x_v, dh1_v, dxa_v = r
        _, vjp = jax.vjp(_mod_norm, x_v, *c)
        dx, dg, dsc, dsh = vjp(dh1_v)
        return [dx + dxa_v, dg, dsc, dsh]

    grad_x, dg_pre_mix, dsc1, dsh1 = _rowwise(
        pre_mix_bwd, [x, dh1, dx_a], [small["g_pre_mix"], sc1, sh1], [(D_MODEL, F32)], [d_acc] * 3, tm, "pre_mix_bwd")

    dmod = jnp.concatenate([dsh1, dsc1, dgt1, dsh2, dsc2, dgt2], axis=1)
    big = dict(w_in=dw_in, w_out=dw_out, w_mlp_in=dw_mlp_in, w_mlp_out=dw_mlp_out, w_glu=dw_glu)
    small_g = dict(
        g_pre_mix=dg_pre_mix, g_post_mix=dg_post_mix, ssm_a_re=da_re, ssm_a_im=da_im,
        ssm_log_dt=dlog_dt.reshape(1, SSM_G), ssm_b_re=db_re.reshape(SSM_G, SSM_N, SSM_P),
        ssm_b_im=db_im.reshape(SSM_G, SSM_N, SSM_P), ssm_c_re=_block_diag_out_grad(dcblk_re),
        ssm_c_im=_block_diag_out_grad(dcblk_im), ssm_d=dd_row.reshape(SSM_G, SSM_P), b_glu=db_glu,
        g_attn_out=dg_attn_out, g_ssm_out=dg_ssm_out, g_pre_mlp=dg_pre_mlp, g_post_mlp=dg_post_mlp)
    return loss, grad_x, dmod, big, small_g


MESH_ID = pl.DeviceIdType.MESH
N_DEV = 8
N_CHIPS = 4
HBM_SPEC = pl.BlockSpec(memory_space=pltpu.HBM)


def _place():
    x, y, c = lax.axis_index("x"), lax.axis_index("y"), lax.axis_index("c")
    other_chips = [(1 - x, y), (x, 1 - y), (1 - x, 1 - y)]
    return x, y, c, other_chips


def _half_rows(index, half):
    return pl.ds(pl.multiple_of(index * half, ROW_PAD), half)


def _remote(src, dst, send_sem, recv_sem, dev):
    return pltpu.make_async_remote_copy(src_ref=src, dst_ref=dst, send_sem=send_sem, recv_sem=recv_sem,
                                        device_id=dev, device_id_type=MESH_ID)


def _all_gather8(block, name):
    m_per, n = block.shape

    def body(x_ref, out_ref, send_sems, recv_sems, local_sem):
        x, y, c, chips = _place()
        me, sibling = (x, y, c), (x, y, 1 - c)

        def rows(px, py, pc):
            return out_ref.at[pl.ds((4 * px + 2 * py + pc) * m_per, m_per), :]

        def copy(k, blk, to, src=None):
            return _remote(rows(*blk) if src is None else src, rows(*blk), send_sems.at[k], recv_sems.at[k], to)

        mine = pltpu.make_async_copy(x_ref, rows(*me), local_sem)
        mine.start()
        first = [copy(0, me, sibling, src=x_ref)]
        first += [copy(1 + j, me, (*chip, c), src=x_ref) for j, chip in enumerate(chips)]
        for cp in first:
            cp.start()
        passed = [copy(4 + j, (*chip, c), sibling) for j, chip in enumerate(chips)]
        for j, chip in enumerate(chips):
            copy(1 + j, (*chip, c), me).wait_recv()
            passed[j].start()
        copy(0, sibling, me).wait_recv()
        for j, chip in enumerate(chips):
            copy(4 + j, (*chip, 1 - c), me).wait_recv()
        for cp in first + passed:
            cp.wait_send()
        mine.wait()

    return pl.pallas_call(
        body,
        name=name,
        out_shape=jax.ShapeDtypeStruct((N_DEV * m_per, n), block.dtype),
        in_specs=[pl.BlockSpec(memory_space=pltpu.VMEM)],
        out_specs=pl.BlockSpec(memory_space=pltpu.VMEM),
        scratch_shapes=[pltpu.SemaphoreType.DMA((7,)), pltpu.SemaphoreType.DMA((7,)), pltpu.SemaphoreType.DMA],
        compiler_params=_params(),
    )(block)


def _weight_gather(shards, name):
    n = len(shards)
    shapes = [s.shape for s in shards]

    def body(*refs):
        ins, outs = refs[:n], refs[n:2 * n]
        send, recv, fsend, frecv = refs[2 * n:]
        x, y, c, chips = _place()
        k_me = 2 * x + y
        sibling = (x, y, 1 - c)
        pending = []
        for a in range(n):
            half = shapes[a][0] // 2
            mine = _half_rows(c, half)
            for j, chip in enumerate(chips):
                cp = _remote(ins[a].at[mine, :], outs[a].at[k_me, mine, :], send.at[a, j], recv.at[a, j], (*chip, c))
                cp.start()
                pending.append(cp.wait_send)
        for a in range(n):
            half = shapes[a][0] // 2
            for j, (px, py) in enumerate(chips):
                piece = outs[a].at[2 * px + py, _half_rows(c, half), :]
                _remote(piece, piece, send.at[a, j], recv.at[a, j], (px, py, c)).wait_recv()
                fw = _remote(piece, piece, fsend.at[a, j], frecv.at[a, j], sibling)
                fw.start()
                pending.append(fw.wait_send)
        for a in range(n):
            half = shapes[a][0] // 2
            for j, (px, py) in enumerate(chips):
                piece = outs[a].at[2 * px + py, _half_rows(1 - c, half), :]
                _remote(piece, piece, fsend.at[a, j], frecv.at[a, j], sibling).wait_recv()
        for wait in pending:
            wait()

    sems = pltpu.SemaphoreType.DMA((n, 3))
    return pl.pallas_call(
        body,
        name=name,
        out_shape=[jax.ShapeDtypeStruct((N_CHIPS,) + s, BF16) for s in shapes],
        in_specs=[HBM_SPEC] * n,
        out_specs=[HBM_SPEC] * n,
        scratch_shapes=[sems, sems, sems, sems],
        compiler_params=_params(),
    )(*shards)


def _sibling_swap(halves, name):
    n = len(halves)
    shapes = [h.shape for h in halves]

    def body(*refs):
        ins, outs = refs[:n], refs[n:2 * n]
        send, recv = refs[2 * n:]
        x, y, c, _ = _place()
        pending = []
        for a in range(n):
            half = shapes[a][0]
            mine = outs[a].at[_half_rows(c, half), :]
            cp = _remote(ins[a], mine, send.at[a], recv.at[a], (x, y, 1 - c))
            cp.start()
            pending.append(cp.wait_send)
        for a in range(n):
            half = shapes[a][0]
            theirs = outs[a].at[_half_rows(1 - c, half), :]
            _remote(theirs, theirs, send.at[a], recv.at[a], (x, y, 1 - c)).wait_recv()
        for wait in pending:
            wait()

    return pl.pallas_call(
        body,
        name=name,
        out_shape=[jax.ShapeDtypeStruct((2 * s[0], s[1]), F32) for s in shapes],
        in_specs=[HBM_SPEC] * n,
        out_specs=[HBM_SPEC] * n,
        scratch_shapes=[pltpu.SemaphoreType.DMA((n,)), pltpu.SemaphoreType.DMA((n,))],
        compiler_params=_params(),
    )(*halves)


SEM_SPEC = pl.BlockSpec(memory_space=pltpu.SEMAPHORE)
ANY_SPEC = pl.BlockSpec(memory_space=pl.ANY)
DATAFLOW = pltpu.SideEffectType.DATAFLOW_SIDE_EFFECTING


def _split_copy_start(srcs, lands, plan, n_sems, name, after=None):
    bufs = list(srcs) + list(lands)
    ns, nb = len(srcs), len(bufs)
    extra = [] if after is None else [after]

    def body(*refs):
        outs = refs[nb + len(extra):]
        for outgoing, _ in plan(refs[:ns], refs[ns:nb], outs[0], outs[1]):
            outgoing.start()
        outs[-1][...] = jnp.zeros_like(outs[-1])

    sems = pltpu.SemaphoreType.DMA((n_sems,))
    return pl.pallas_call(
        body,
        name=name,
        out_shape=(sems, sems, *[pltpu.HBM(b.shape, b.dtype) for b in bufs], jax.ShapeDtypeStruct((8, LANES), F32)),
        in_specs=[HBM_SPEC] * nb + [ANY_SPEC] * len(extra),
        out_specs=(SEM_SPEC, SEM_SPEC, *[HBM_SPEC] * nb, pl.BlockSpec(memory_space=pltpu.VMEM)),
        input_output_aliases={i: 2 + i for i in range(nb)},
        compiler_params=pltpu.CompilerParams(has_side_effects=DATAFLOW),
    )(*[pltpu.with_memory_space_constraint(b, pltpu.HBM) for b in bufs], *extra)


def _split_copy_wait(started, plan, after, name, n_srcs=None):
    send, recv, *bufs = started[:-1]
    nb = len(bufs)
    ns = nb // 2 if n_srcs is None else n_srcs

    def body(*refs):
        for outgoing, incoming in plan(refs[:ns], refs[ns:nb], refs[nb], refs[nb + 1]):
            outgoing.wait_send()
            incoming.wait_recv()

    return pl.pallas_call(
        body,
        name=name,
        out_shape=tuple(pltpu.HBM(b.shape, b.dtype) for b in bufs),
        in_specs=[HBM_SPEC] * nb + [SEM_SPEC, SEM_SPEC, ANY_SPEC],
        out_specs=tuple([HBM_SPEC] * nb),
        input_output_aliases={i: i for i in range(nb)},
        compiler_params=pltpu.CompilerParams(has_side_effects=DATAFLOW),
    )(*bufs, send, recv, after)


def _weight_plan(shapes):
    def plan(srcs, lands, send, recv):
        x, y, c, chips = _place()
        copies = []
        for a in range(len(shapes)):
            mine = _half_rows(c, shapes[a][0] // 2)
            for j, (px, py) in enumerate(chips):
                s = 3 * a + j
                arrival = lands[a].at[2 * px + py, mine, :]
                copies.append((_remote(srcs[a].at[mine, :], lands[a].at[2 * x + y, mine, :], send.at[s], recv.at[s], (px, py, c)),
                               _remote(arrival, arrival, send.at[s], recv.at[s], (px, py, c))))
        return copies
    return plan


def _halves_plan(shapes):
    def plan(srcs, lands, send, recv):
        x, y, c, _ = _place()
        copies = []
        for a in range(len(shapes)):
            theirs = srcs[a].at[:, _half_rows(1 - c, shapes[a][1] // 2), :]
            copies.append((_remote(theirs, lands[a], send.at[a], recv.at[a], (x, y, 1 - c)),
                           _remote(lands[a], lands[a], send.at[a], recv.at[a], (x, y, 1 - c))))
        return copies
    return plan


def _exchange_plan(n):
    def plan(srcs, lands, send, recv):
        x, y, c, chips = _place()
        copies = []
        for a in range(n):
            for j, (px, py) in enumerate(chips):
                s = 3 * a + j
                copies.append((_remote(srcs[a].at[2 * px + py], lands[a].at[j], send.at[s], recv.at[s], (px, py, c)),
                               _remote(lands[a].at[j], lands[a].at[j], send.at[s], recv.at[s], (px, py, c))))
        return copies
    return plan


def _swap_plan(shapes):
    def plan(srcs, lands, send, recv):
        x, y, c, _ = _place()
        copies = []
        for a in range(len(shapes)):
            half = shapes[a][0]
            theirs = lands[a].at[_half_rows(1 - c, half), :]
            copies.append((_remote(srcs[a], lands[a].at[_half_rows(c, half), :], send.at[a], recv.at[a], (x, y, 1 - c)),
                           _remote(theirs, theirs, send.at[a], recv.at[a], (x, y, 1 - c))))
        return copies
    return plan


def _forward_plan(shapes):
    def plan(stacks, _, send, recv):
        x, y, c, chips = _place()
        copies = []
        for a in range(len(shapes)):
            half = shapes[a][0] // 2
            for j, (px, py) in enumerate(chips):
                s = 3 * a + j
                mine = stacks[a].at[2 * px + py, _half_rows(c, half), :]
                theirs = stacks[a].at[2 * px + py, _half_rows(1 - c, half), :]
                copies.append((_remote(mine, mine, send.at[s], recv.at[s], (x, y, 1 - c)),
                               _remote(theirs, theirs, send.at[s], recv.at[s], (x, y, 1 - c))))
        return copies
    return plan


def _tie(x, token):
    return x + token[0:1, 0:1].astype(x.dtype)


ROW_PAD = 16


def _silu(x):
    return x * _sigmoid(x)


def _ada_fwd(c_all, w_ada, b_ada, name):
    dm, cols = w_ada.shape
    tn = 512

    def body(c_ref, w_ref, b_ref, o_ref):
        o_ref[...] = _nn(_silu(c_ref[...]).astype(BF16), w_ref[...].astype(BF16)) + b_ref[...]

    return pl.pallas_call(
        body,
        name=name,
        grid=(cols // tn,),
        in_specs=[pl.BlockSpec((ROW_PAD, dm), lambda j: (0, 0)), pl.BlockSpec((dm, tn), lambda j: (0, j)),
                  pl.BlockSpec((1, tn), lambda j: (0, j))],
        out_specs=pl.BlockSpec((ROW_PAD, tn), lambda j: (0, j)),
        out_shape=jax.ShapeDtypeStruct((ROW_PAD, cols), F32),
        compiler_params=_params(("parallel",)),
    )(c_all, w_ada, b_ada)


def _adamw(w, g, m, v):
    m = ADAM_B1 * m + (1.0 - ADAM_B1) * g
    v = ADAM_B2 * v + (1.0 - ADAM_B2) * (g * g)
    m_hat = m / (1.0 - ADAM_B1 ** ADAM_STEP)
    v_hat = v / (1.0 - ADAM_B2 ** ADAM_STEP)
    delta = -ADAM_LR * (m_hat / (jnp.sqrt(v_hat) + ADAM_EPS) + ADAM_WD * w)
    return delta, m, v


def _ada_bwd_adamw(c_all, dmod_cols, w, m, v, name):
    dm, cols = w.shape
    tm, tn = 512, 512

    def body(c_ref, d_ref, w_ref, m_ref, v_ref, g_ref, dl_ref, nm_ref, nv_ref):
        g = _tn(_silu(c_ref[...]).astype(BF16), d_ref[...].astype(BF16))
        g_ref[...] = g
        dl_ref[...], nm_ref[...], nv_ref[...] = _adamw(w_ref[...], g, m_ref[...], v_ref[...])

    tile = pl.BlockSpec((tm, tn), lambda i, j: (i, j))
    shape = jax.ShapeDtypeStruct((dm, cols), F32)
    return pl.pallas_call(
        body,
        name=name,
        grid=(dm // tm, cols // tn),
        in_specs=[pl.BlockSpec((ROW_PAD, tm), lambda i, j: (0, i)), pl.BlockSpec((ROW_PAD, tn), lambda i, j: (0, j)),
                  tile, tile, tile],
        out_specs=[tile] * 4,
        out_shape=[shape] * 4,
        compiler_params=_params(("parallel", "parallel")),
    )(c_all, dmod_cols, w, m, v)


def _sum_blocks(parts, nblk, name):
    rows, cols = parts.shape[0] // nblk, parts.shape[1]

    def body(p_ref, o_ref):
        tot = p_ref[0:rows, :]
        for b in range(1, nblk):
            tot = tot + p_ref[b * rows:(b + 1) * rows, :]
        o_ref[...] = tot

    return pl.pallas_call(body, name=name, out_shape=jax.ShapeDtypeStruct((rows, cols), F32), compiler_params=_params())(parts)


def _adamw_rows(w, g, m, v, tm, name):
    return _rowwise(lambda r, c: list(_adamw(*r)), [w, g, m, v], [], [(w.shape[1], F32)] * 3, [], tm, name)


BIG = ("w_in", "w_out", "w_mlp_in", "w_mlp_out", "w_glu")
COL_SHARDED = ("w_in", "w_out", "w_mlp_in")
SMALL = ("b_ada", "g_pre_mix", "g_post_mix", "ssm_a_re", "ssm_a_im", "ssm_log_dt", "ssm_b_re", "ssm_b_im",
         "ssm_c_re", "ssm_c_im", "ssm_d", "b_glu", "g_attn_out", "g_ssm_out", "g_pre_mlp", "g_post_mlp")
WEIGHTS = ("w_ada", "b_ada", "g_pre_mix", "g_post_mix", "w_in", "ssm_a_re", "ssm_a_im", "ssm_log_dt", "ssm_b_re",
           "ssm_b_im", "ssm_c_re", "ssm_c_im", "ssm_d", "w_glu", "b_glu", "g_attn_out", "g_ssm_out", "w_out",
           "g_pre_mlp", "g_post_mlp", "w_mlp_in", "w_mlp_out")
FLAT_COLS = 1024
FLAT_ROWS = 256
ROW_TILE = {"w_in": 256, "w_out": 128, "w_mlp_in": 256, "w_mlp_out": 256, "w_glu": 112}


def _flatten_small(tree):
    flat = jnp.concatenate([tree[k].reshape(-1) for k in SMALL])
    return jnp.pad(flat, (0, FLAT_ROWS * FLAT_COLS - flat.shape[0])).reshape(FLAT_ROWS, FLAT_COLS)


def _unflatten_small(flat, like):
    flat = flat.reshape(-1)
    out, at = {}, 0
    for k in SMALL:
        size = math.prod(like[k].shape)
        out[k] = flat[at:at + size].reshape(like[k].shape)
        at += size
    return out


def _unstack(stack, name):
    if name in COL_SHARDED:
        return stack.transpose(1, 0, 2).reshape(stack.shape[1], N_CHIPS * stack.shape[2])
    return stack.reshape(N_CHIPS * stack.shape[1], stack.shape[2])


def _stack(full, name):
    if name in COL_SHARDED:
        return full.reshape(full.shape[0], N_CHIPS, full.shape[1] // N_CHIPS).transpose(1, 0, 2)
    return full.reshape(N_CHIPS, full.shape[0] // N_CHIPS, full.shape[1])


EARLY = ("w_in", "w_out", "w_glu")
LATE = ("w_mlp_in", "w_mlp_out")
MIXER_W = ("w_out", "w_glu")


def _chip_sums(names, g_stacks, from_sibling, ic, chip):
    own, to_send = [], []
    place = jnp.stack([ic, chip]).astype(jnp.int32)
    for k, gs, fs in zip(names, g_stacks, from_sibling):
        _, rows, cols = gs.shape
        half, tm = rows // 2, ROW_TILE[k]
        nt = half // tm

        def body(place_ref, g_ref, f_ref, own_ref, send_ref):
            s = g_ref[...] + f_ref[...]
            send_ref[...] = s.astype(BF16)

            @pl.when(pl.program_id(1) == place_ref[1])
            def _():
                own_ref[...] = s

        slab = lambda index: pl.BlockSpec((None, tm, cols), index)
        mine, to_chips = pl.pallas_call(
            body,
            name="grad_chip_sum_" + k,
            grid_spec=pltpu.PrefetchScalarGridSpec(
                num_scalar_prefetch=1,
                grid=(nt, N_CHIPS),
                in_specs=[slab(lambda i, kk, p, nt=nt: (kk, p[0] * nt + i, 0)), slab(lambda i, kk, p: (kk, i, 0))],
                out_specs=[pl.BlockSpec((tm, cols), lambda i, kk, p: (i, 0)), slab(lambda i, kk, p: (kk, i, 0))]),
            out_shape=[jax.ShapeDtypeStruct((half, cols), F32), jax.ShapeDtypeStruct((N_CHIPS, half, cols), BF16)],
            compiler_params=_params(("arbitrary", "arbitrary")),
        )(place, gs, fs)
        own.append(mine)
        to_send.append(to_chips)
    return own, to_send


def _grad_totals(names, own, from_chips):
    totals = []
    for k, mine, fc in zip(names, own, from_chips):
        half, cols = mine.shape
        tm = ROW_TILE[k]

        def body(m_ref, a_ref, b_ref, c_ref, o_ref):
            o_ref[...] = m_ref[...] + a_ref[...].astype(F32) + b_ref[...].astype(F32) + c_ref[...].astype(F32)

        rows = pl.BlockSpec((tm, cols), lambda i: (i, 0))
        totals.append(pl.pallas_call(
            body,
            name="grad_total_" + k,
            grid=(half // tm,),
            in_specs=[rows] + [pl.BlockSpec((None, tm, cols), lambda i, j=j: (j, i, 0)) for j in range(3)],
            out_specs=rows,
            out_shape=jax.ShapeDtypeStruct((half, cols), F32),
            compiler_params=_params(("parallel",)),
        )(mine, fc, fc, fc))
    return totals


class _LateWeights:
    def __init__(self, names, own_shards, chip, after, tag):
        self.names, self.chip, self.tag = names, chip, tag
        self.shapes = [o.shape for o in own_shards]
        lands = [lax.empty((N_CHIPS,) + s, BF16) for s in self.shapes]
        self.gather = _split_copy_start(own_shards, lands, _weight_plan(self.shapes), 3 * len(names),
                                        tag + "_gather_start", after=after)
        self.token = self.gather[-1]

    def arrived(self, after):
        n = len(self.names)
        done = _split_copy_wait(self.gather, _weight_plan(self.shapes), after, self.tag + "_gather_wait")
        self.own_shards = done[:n]
        self.forward = _split_copy_start(done[n:], [], _forward_plan(self.shapes), 3 * n, self.tag + "_forward_start")
        return self.forward[-1]

    def weights(self, after):
        n = len(self.names)
        stacks = _split_copy_wait(self.forward, _forward_plan(self.shapes), after, self.tag + "_forward_wait", n_srcs=n)
        stacks = [lax.dynamic_update_index_in_dim(s, o, self.chip, 0) for s, o in zip(stacks, self.own_shards)]
        return {k: _unstack(s, k) for k, s in zip(self.names, stacks)}


class _Overlap:
    def __init__(self, own, ic, chip, after):
        self.ic, self.chip = ic, chip
        self.mixer = _LateWeights(MIXER_W, [own[k] for k in MIXER_W], chip, after, "mixer_weight")
        self.mlp = _LateWeights(LATE, [own[k] for k in LATE], chip, self.mixer.token, "mlp_weight")
        self.token = self.mlp.token

    def mlp_grads_to_sibling(self, dw_in, dw_out):
        stacks = [dw_in, dw_out]
        self.g_shapes = [s.shape for s in stacks]
        lands = [lax.empty((N_CHIPS, s[1] // 2, s[2]), F32) for s in self.g_shapes]
        self.halves = _split_copy_start(stacks, lands, _halves_plan(self.g_shapes), len(LATE), "mlp_grad_halves_start")
        return self.halves[-1]

    def mlp_grads_to_chips(self, after):
        n = len(LATE)
        done = _split_copy_wait(self.halves, _halves_plan(self.g_shapes), after, "mlp_grad_halves_wait")
        self.own, to_send = _chip_sums(LATE, done[:n], done[n:], self.ic, self.chip)
        lands = [lax.empty((3,) + s.shape[1:], BF16) for s in to_send]
        self.exchange = _split_copy_start(to_send, lands, _exchange_plan(n), 3 * n, "mlp_grad_exchange_start")
        return self.exchange[-1]

    def mlp_grads_reduced(self, after):
        n = len(LATE)
        done = _split_copy_wait(self.exchange, _exchange_plan(n), after, "mlp_grad_exchange_wait")
        return _grad_totals(LATE, self.own, done[n:])


def _pad_rows(row):
    return jnp.pad(row, ((0, 8 - row.shape[0]), (0, 0)))


def _every_eighth(gathered):
    rows = gathered.reshape(N_DEV, 8, gathered.shape[1])[:, 0, :]
    return jnp.pad(rows, ((0, ROW_PAD - N_DEV), (0, 0)))


def kernel(x, c, positions, w_ada, b_ada, g_pre_mix, g_post_mix, w_in, ssm_a_re, ssm_a_im, ssm_log_dt, ssm_b_re, ssm_b_im, ssm_c_re, ssm_c_im, ssm_d, w_glu, b_glu, g_attn_out, g_ssm_out, w_out, g_pre_mlp, g_post_mlp, w_mlp_in, w_mlp_out, loss_target, m_w_ada, m_b_ada, m_g_pre_mix, m_g_post_mix, m_w_in, m_ssm_a_re, m_ssm_a_im, m_ssm_log_dt, m_ssm_b_re, m_ssm_b_im, m_ssm_c_re, m_ssm_c_im, m_ssm_d, m_w_glu, m_b_glu, m_g_attn_out, m_g_ssm_out, m_w_out, m_g_pre_mlp, m_g_post_mlp, m_w_mlp_in, m_w_mlp_out, v_w_ada, v_b_ada, v_g_pre_mix, v_g_post_mix, v_w_in, v_ssm_a_re, v_ssm_a_im, v_ssm_log_dt, v_ssm_b_re, v_ssm_b_im, v_ssm_c_re, v_ssm_c_im, v_ssm_d, v_w_glu, v_b_glu, v_g_attn_out, v_g_ssm_out, v_w_out, v_g_pre_mlp, v_g_post_mlp, v_w_mlp_in, v_w_mlp_out):
    given = dict(locals())
    w = {k: given[k][0] for k in WEIGHTS}
    mom = {k: given["m_" + k][0] for k in WEIGHTS}
    var = {k: given["v_" + k][0] for k in WEIGHTS}
    for tree in (w, mom, var):
        for k in ("b_ada", "g_pre_mix", "g_post_mix", "ssm_log_dt", "b_glu", "g_attn_out", "g_ssm_out", "g_pre_mlp",
                  "g_post_mlp"):
            tree[k] = tree[k].reshape(1, -1)
    ix, iy, ic = lax.axis_index("x"), lax.axis_index("y"), lax.axis_index("c")
    chip = 2 * ix + iy
    me = 4 * ix + 2 * iy + ic
    shard_cols = w["w_ada"].shape[1]

    c_all = _every_eighth(_all_gather8(_pad_rows(c), "gather_c"))
    b_ada_cols = lax.dynamic_slice_in_dim(w["b_ada"], chip * shard_cols, shard_cols, axis=1)
    mod_cols = _ada_fwd(c_all, w["w_ada"], b_ada_cols, "ada_fwd")[:N_DEV]
    mod_all = _all_gather8(mod_cols, "gather_mod").reshape(N_CHIPS, 2, N_DEV, shard_cols)[:, 0]
    mod = lax.dynamic_index_in_dim(mod_all, me, axis=1, keepdims=False).reshape(1, N_MOD * D_MODEL)

    w_in_own = w["w_in"].astype(BF16)
    (w_in_stack,) = _weight_gather([w_in_own], "weight_gather")
    w_in_stack = lax.dynamic_update_index_in_dim(w_in_stack, w_in_own, chip, 0)
    wts = {"w_in": _unstack(w_in_stack, "w_in")}
    overlap = _Overlap({k: w[k].astype(BF16) for k in MIXER_W + LATE}, ic, chip, after=w_in_stack)
    mod = _tie(mod, overlap.token)

    small = {k: w[k] for k in SMALL if k != "b_ada"}
    loss, grad_x, dmod, big_g, small_g = _local_step(x[0], positions.reshape(-1, 1), mod, loss_target[0], wts, small,
                                                     hooks=overlap)
    loss = lax.psum(loss[0, 0], ("x", "y", "c"))

    n_early = len(EARLY)
    g_stacks = [_stack(big_g[k], k) for k in EARLY]
    halves_plan = _halves_plan([s.shape for s in g_stacks])
    lands = [lax.empty((N_CHIPS, s.shape[1] // 2, s.shape[2]), F32) for s in g_stacks]
    halves = _split_copy_start(g_stacks, lands, halves_plan, n_early, "grad_halves_start")

    small_g["b_ada"] = dmod
    parts = _all_gather8(_tie(_flatten_small(small_g), halves[-1]), "gather_small_grads")

    halves = _split_copy_wait(halves, halves_plan, parts, "grad_halves_wait")
    chip_f32, chip_bf16 = _chip_sums(EARLY, halves[:n_early], halves[n_early:], ic, chip)
    exchange_plan = _exchange_plan(len(EARLY))
    lands = [lax.empty((3,) + s.shape[1:], BF16) for s in chip_bf16]
    exchange = _split_copy_start(chip_bf16, lands, exchange_plan, 3 * len(EARLY), "grad_exchange_start", after=parts)

    small_flat = _sum_blocks(parts, N_DEV, "small_grad_sum")
    grads = _unflatten_small(small_flat, w)

    mod_rows = N_MOD * D_MODEL // FLAT_COLS
    dmod_all = parts.reshape(N_DEV, FLAT_ROWS, FLAT_COLS)[:, :mod_rows].reshape(N_DEV, N_MOD * D_MODEL)
    dmod_all = jnp.pad(dmod_all, ((0, ROW_PAD - N_DEV), (0, 0)))
    dmod_cols = lax.dynamic_slice_in_dim(dmod_all, chip * shard_cols, shard_cols, axis=1)
    mlp_reduced = overlap.mlp_grads_reduced(exchange[-1])
    swap_plan = _swap_plan([r.shape for r in mlp_reduced])
    lands = [lax.empty((2 * r.shape[0], r.shape[1]), F32) for r in mlp_reduced]
    swap = _split_copy_start(mlp_reduced, lands, swap_plan, len(LATE), "mlp_grad_swap_start")
    g_ada, d_ada, m_ada, v_ada = _ada_bwd_adamw(_tie(c_all, swap[-1]), dmod_cols, w["w_ada"], mom["w_ada"],
                                                var["w_ada"], "ada_bwd_adamw")
    grads["w_ada"] = g_ada
    delta, new_m, new_v = {"w_ada": d_ada}, {"w_ada": m_ada}, {"w_ada": v_ada}

    def finish(names, reduced, swapped):
        for k, s, r in zip(names, swapped, reduced):
            grads[k] = lax.dynamic_update_slice_in_dim(s, r, ic * r.shape[0], axis=0)
            delta[k], new_m[k], new_v[k] = _adamw_rows(w[k], grads[k], mom[k], var[k], ROW_TILE[k], "adamw_" + k)

    swap = _split_copy_wait(swap, swap_plan, g_ada, "mlp_grad_swap_wait")
    finish(LATE, swap[:len(LATE)], swap[len(LATE):])
    from_chips = _split_copy_wait(exchange, exchange_plan, new_v[LATE[-1]], "grad_exchange_wait")[n_early:]
    reduced = _grad_totals(EARLY, chip_f32, from_chips)
    finish(EARLY, reduced, _sibling_swap(reduced, "grad_sibling_swap"))

    flat_upd = _adamw_rows(_flatten_small(w), small_flat, _flatten_small(mom), _flatten_small(var), FLAT_ROWS,
                           "adamw_small")
    for tree, flat in zip((delta, new_m, new_v), flat_upd):
        tree.update(_unflatten_small(flat, w))

    shaped = lambda tree: [tree[k].reshape(given[k].shape) for k in WEIGHTS]
    return (loss, grad_x[None], *shaped(grads), *shaped(delta), *shaped(new_m), *shaped(new_v))
```

```python
import functools
import math

import jax
import jax.numpy as jnp
import numpy as np
from jax import lax
from jax.experimental import pallas as pl
from jax.experimental.pallas import tpu as pltpu

F32 = jnp.float32
BF16 = jnp.bfloat16

D_MODEL = 2048
HEAD_DIM = 64
DILATIONS = (1, 4, 16)
ATT_SPAN = 128
ATT_BLK = 128
HEADS_PER_GROUP = 6
KV_WIDTH = HEADS_PER_GROUP * HEAD_DIM
ATT_Q_WIDTH = 3 * KV_WIDTH
ROT_DIM = 16
ROPE_THETA = 500000.0
SSM_WIDTH = 896
SSM_P = 16
SSM_G = 56
SSM_N = 64
SSM_GN = SSM_G * SSM_N
SSM_TILES = SSM_WIDTH // 128
SSM_TILE_GN = 8 * SSM_N
IN_WIDTH = 2816
OUT_IN_WIDTH = 1280
D_FF = 8192
N_MOD = 6
EPS = 1e-6
LANES = 128
SSM_SEGS = 8
SSM_CHUNK = 256
SSM_SEG_LEN = SSM_CHUNK // SSM_SEGS

ADAM_LR = 0.001
ADAM_B1 = 0.9
ADAM_B2 = 0.999
ADAM_EPS = 1e-08
ADAM_WD = 0.01
ADAM_STEP = 10

VMEM_LIMIT = 56 * 1024 * 1024


def _params(sem=None):
    return pltpu.CompilerParams(dimension_semantics=sem, vmem_limit_bytes=VMEM_LIMIT)


def _dot(a, b, dims):
    return lax.dot_general(a, b, (dims, ((), ())), preferred_element_type=F32)


def _nn(a, b):
    return _dot(a, b, ((1,), (0,)))


def _nt(a, b):
    return _dot(a, b, ((1,), (1,)))


def _tn(a, b):
    return _dot(a, b, ((0,), (0,)))


def _matmul(a, b, mode, out_dtype, tm, tn, tk, name, after=None):
    if mode == "nn":
        (m, k), (_, n) = a.shape, b.shape
        a_spec = pl.BlockSpec((tm, tk), lambda i, j, kk: (i, kk))
        b_spec = pl.BlockSpec((tk, tn), lambda i, j, kk: (kk, j))
        op = _nn
    elif mode == "nt":
        (m, k), (n, _) = a.shape, b.shape
        a_spec = pl.BlockSpec((tm, tk), lambda i, j, kk: (i, kk))
        b_spec = pl.BlockSpec((tn, tk), lambda i, j, kk: (j, kk))
        op = _nt
    else:
        (k, m), (_, n) = a.shape, b.shape
        a_spec = pl.BlockSpec((tk, tm), lambda i, j, kk: (kk, i))
        b_spec = pl.BlockSpec((tk, tn), lambda i, j, kk: (kk, j))
        op = _tn
    assert m % tm == 0 and n % tn == 0 and k % tk == 0, (name, m, n, k)
    nk = k // tk

    def body(a_ref, b_ref, *rest):
        o_ref, acc_ref = rest[-2:]
        kk = pl.program_id(2)

        @pl.when(kk == 0)
        def _():
            acc_ref[...] = jnp.zeros_like(acc_ref)

        acc_ref[...] += op(a_ref[...], b_ref[...])

        @pl.when(kk == nk - 1)
        def _():
            o_ref[...] = acc_ref[...].astype(o_ref.dtype)

    extra = [] if after is None else [after]
    return pl.pallas_call(
        body,
        name=name,
        grid=(m // tm, n // tn, nk),
        in_specs=[a_spec, b_spec] + [pl.BlockSpec(t.shape, lambda i, j, kk: (0, 0)) for t in extra],
        out_specs=pl.BlockSpec((tm, tn), lambda i, j, kk: (i, j)),
        out_shape=jax.ShapeDtypeStruct((m, n), out_dtype),
        scratch_shapes=[pltpu.VMEM((tm, tn), F32)],
        compiler_params=_params(("parallel", "parallel", "arbitrary")),
    )(a, b, *extra)


def _rowwise(fn, rows, consts, out_rows, out_accs, tm, name):
    n_rows = rows[0].shape[0]
    assert n_rows % tm == 0
    nr, nc, no = len(rows), len(consts), len(out_rows)

    def body(*refs):
        r_in, c_in = refs[:nr], refs[nr:nr + nc]
        o_row, o_acc = refs[nr + nc:nr + nc + no], refs[nr + nc + no:]
        outs = fn([r[...] for r in r_in], [c[...] for c in c_in])
        assert len(outs) == len(o_row) + len(o_acc), name
        for ref, v in zip(o_row, outs[:no]):
            ref[...] = v.astype(ref.dtype)
        first = pl.program_id(0) == 0
        for ref, v in zip(o_acc, outs[no:]):
            @pl.when(first)
            def _(ref=ref, v=v):
                ref[...] = v.astype(F32)

            @pl.when(jnp.logical_not(first))
            def _(ref=ref, v=v):
                ref[...] += v.astype(F32)

    in_specs = [pl.BlockSpec((tm, r.shape[1]), lambda i: (i, 0)) for r in rows]
    in_specs += [pl.BlockSpec(c.shape, lambda i: (0, 0)) for c in consts]
    out_specs = [pl.BlockSpec((tm, w), lambda i: (i, 0)) for w, _ in out_rows]
    out_specs += [pl.BlockSpec(s, lambda i: (0, 0)) for s in out_accs]
    out_shape = [jax.ShapeDtypeStruct((n_rows, w), dt) for w, dt in out_rows]
    out_shape += [jax.ShapeDtypeStruct(s, F32) for s in out_accs]
    return pl.pallas_call(
        body,
        name=name,
        grid=(n_rows // tm,),
        in_specs=in_specs,
        out_specs=out_specs,
        out_shape=out_shape,
        compiler_params=_params(("arbitrary",)),
    )(*rows, *consts)


def _rms(x, g):
    return x * lax.rsqrt(jnp.mean(x * x, axis=-1, keepdims=True) + EPS) * g


def _mod_norm(x, g, sc, sh):
    return _rms(x, g) * (1.0 + sc) + sh


def _gelu(x):
    return 0.5 * x * (1.0 + jnp.tanh(math.sqrt(2.0 / math.pi) * (x + 0.044715 * (x * x * x))))


def _sigmoid(x):
    return 1.0 / (1.0 + jnp.exp(-x))


def _post_mix(x, mix, g_post, gt1, g_pre, sc2, sh2):
    x1 = x + gt1 * _rms(mix, g_post)
    return x1, _mod_norm(x1, g_pre, sc2, sh2)


def _att_mix(o0, o1, o2, l0, l1, l2, g):
    m = jnp.maximum(jnp.maximum(l0, l1), l2)
    e0, e1, e2 = jnp.exp(l0 - m), jnp.exp(l1 - m), jnp.exp(l2 - m)
    att = (e0 * o0 + e1 * o1 + e2 * o2) / (e0 + e1 + e2)
    return _rms(att, g)


def _glu_out(y2, z, g):
    return _rms(y2 * _sigmoid(z), g)


def _rope_tables(pos_col, freq_lane, name):
    n_rows = pos_col.shape[0]
    tm = 512

    def body(p_ref, f_ref, cos_ref, lo_ref, hi_ref):
        ang = p_ref[...].astype(F32) * f_ref[...]
        lane = lax.broadcasted_iota(jnp.int32, ang.shape, 1) % HEAD_DIM
        c, s = jnp.cos(ang), jnp.sin(ang)
        cos_ref[...] = jnp.where(lane < ROT_DIM, c, 1.0)
        lo_ref[...] = jnp.where(lane < ROT_DIM // 2, -s, 0.0)
        hi_ref[...] = jnp.where((lane >= ROT_DIM // 2) & (lane < ROT_DIM), s, 0.0)

    tab = jax.ShapeDtypeStruct((n_rows, LANES), F32)
    return pl.pallas_call(
        body,
        name=name,
        grid=(n_rows // tm,),
        in_specs=[pl.BlockSpec((tm, 1), lambda i: (i, 0)), pl.BlockSpec((1, LANES), lambda i: (0, 0))],
        out_specs=[pl.BlockSpec((tm, LANES), lambda i: (i, 0))] * 3,
        out_shape=[tab] * 3,
        compiler_params=_params(("parallel",)),
    )(pos_col, freq_lane)


def _rope(x, cos_t, lo_t, hi_t):
    half = ROT_DIM // 2
    return x * cos_t + pltpu.roll(x, LANES - half, 1) * lo_t + pltpu.roll(x, half, 1) * hi_t


def _rope_transposed(dy, cos_t, lo_t, hi_t):
    half = ROT_DIM // 2
    return dy * cos_t + pltpu.roll(dy * lo_t, half, 1) + pltpu.roll(dy * hi_t, LANES - half, 1)


def _att_masks(i, k0):
    q_pos = i * ATT_BLK + lax.broadcasted_iota(jnp.int32, (ATT_BLK, 2 * ATT_BLK), 0)
    k_pos = k0 + lax.broadcasted_iota(jnp.int32, (ATT_BLK, 2 * ATT_BLK), 1)
    dist = q_pos - k_pos
    return (dist >= 0) & (dist <= ATT_SPAN)


def _head_lane_masks():
    lane = lax.broadcasted_iota(jnp.int32, (1, LANES), 1)
    return lane < HEAD_DIM, lane >= HEAD_DIM


def _att_specs(gi, n_rows):
    col = lambda at: pl.BlockSpec((n_rows, LANES), lambda hp: (0, at + hp))
    qkv = [col(gi * 3), col(9), col(12)]
    tabs = [pl.BlockSpec((n_rows, LANES), lambda hp: (0, 0), pipeline_mode=pl.Buffered(1))] * 3
    head_in = col(0) if DILATIONS[gi] > 1 else pl.BlockSpec((n_rows, LANES), lambda hp: (0, hp),
                                                            pipeline_mode=pl.Buffered(1))
    return qkv, tabs, head_in, col(0)


def _sub_rows(d, n, r):
    return pl.ds(r, n, stride=d) if d > 1 else pl.ds(0, n)


def _att_load(q_ref, k_ref, v_ref, tabs, sub, qs, ks, vs):
    cos_t, lo_t, hi_t = tabs
    qs[...] = (_rope(q_ref[sub, :], cos_t, lo_t, hi_t) * (1.0 / math.sqrt(HEAD_DIM))).astype(BF16)
    ks[...] = _rope(k_ref[sub, :], cos_t, lo_t, hi_t).astype(BF16)
    vs[...] = v_ref[sub, :].astype(BF16)


def _att_fwd(proj, tabs, gi, name):
    n_rows = proj.shape[0]
    d = DILATIONS[gi]
    n = n_rows // d
    nb = n // ATT_BLK

    def body(q_ref, k_ref, v_ref, cos_ref, lo_ref, hi_ref, o_ref, l_ref, qs, ks, vs, o_s, l_s):
        m0, m1 = _head_lane_masks()

        def step(i, carry):
            k0 = pl.multiple_of(jnp.maximum(i - 1, 0) * ATT_BLK, ATT_BLK)
            q0 = pl.multiple_of(i * ATT_BLK, ATT_BLK)
            q = qs[pl.ds(q0, ATT_BLK), :]
            k = ks[pl.ds(k0, 2 * ATT_BLK), :]
            v = vs[pl.ds(k0, 2 * ATT_BLK), :]
            valid = _att_masks(i, k0)
            outs, lses = [], []
            for hm in (m0, m1):
                s = _nt(jnp.where(hm, q, jnp.zeros_like(q)), k)
                s = jnp.where(valid, s, -1e30)
                mx = jnp.max(s, axis=1, keepdims=True)
                p = jnp.exp(s - mx)
                den = jnp.sum(p, axis=1, keepdims=True)
                outs.append(_nn(p.astype(BF16), v) / den)
                lses.append(mx + jnp.log(den))
            o_s[pl.ds(q0, ATT_BLK), :] = jnp.where(m0, outs[0], outs[1])
            l_s[pl.ds(q0, ATT_BLK), :] = jnp.where(m0, lses[0], lses[1])
            return carry

        for r in range(d):
            sub = _sub_rows(d, n, r)
            _att_load(q_ref, k_ref, v_ref, (cos_ref[sub, :], lo_ref[sub, :], hi_ref[sub, :]), sub, qs, ks, vs)
            lax.fori_loop(0, nb, step, 0, unroll=min(nb, 4))
            o_ref[sub, :] = o_s[...]
            l_ref[sub, :] = l_s[...]

    qkv, tab_specs, _, head_out = _att_specs(gi, n_rows)
    out = jax.ShapeDtypeStruct((n_rows, KV_WIDTH), F32)
    return pl.pallas_call(
        body,
        name=name,
        grid=(3,),
        in_specs=qkv + tab_specs,
        out_specs=[head_out, head_out],
        out_shape=[out, out],
        scratch_shapes=[pltpu.VMEM((n, LANES), BF16)] * 3 + [pltpu.VMEM((n, LANES), F32)] * 2,
        compiler_params=_params(("parallel",)),
    )(proj, proj, proj, *tabs)


def _att_bwd(proj, tabs, o, l, do, dl, gi, name):
    n_rows = proj.shape[0]
    d = DILATIONS[gi]
    n = n_rows // d
    nb = n // ATT_BLK

    def body(q_ref, k_ref, v_ref, cos_ref, lo_ref, hi_ref, o_ref, l_ref, do_ref, dl_ref,
             dq_ref, dk_ref, dv_ref, qs, ks, vs, dq_s, dk_acc, dv_acc, *gathered):
        m0, m1 = _head_lane_masks()
        o_s, l_s, do_s, dl_s = gathered if d > 1 else (o_ref, l_ref, do_ref, dl_ref)

        def step(i, carry):
            k0 = pl.multiple_of(jnp.maximum(i - 1, 0) * ATT_BLK, ATT_BLK)
            q0 = pl.multiple_of(i * ATT_BLK, ATT_BLK)
            rows = pl.ds(q0, ATT_BLK)
            keys = pl.ds(k0, 2 * ATT_BLK)
            q, k, v = qs[rows, :], ks[keys, :], vs[keys, :]
            d_o, lse = do_s[rows, :], l_s[rows, :]
            o_do = o_s[rows, :] * d_o
            d_l = dl_s[rows, :]
            valid = _att_masks(i, k0)
            dq = jnp.zeros((ATT_BLK, LANES), F32)
            dk = jnp.zeros((2 * ATT_BLK, LANES), F32)
            dv = jnp.zeros((2 * ATT_BLK, LANES), F32)
            for hm in (m0, m1):
                qh, kh = jnp.where(hm, q, jnp.zeros_like(q)), jnp.where(hm, k, jnp.zeros_like(k))
                doh = jnp.where(hm, d_o, 0.0).astype(BF16)
                lse_h = jnp.max(jnp.where(hm, lse, -1e30), axis=1, keepdims=True)
                delta = jnp.sum(jnp.where(hm, o_do, 0.0), axis=1, keepdims=True)
                dlse = jnp.sum(jnp.where(hm, d_l, 0.0), axis=1, keepdims=True)
                s = jnp.where(valid, _nt(qh, k), -1e30)
                p = jnp.exp(s - lse_h)
                dv = dv + _tn(p.astype(BF16), doh)
                ds = (p * (_nt(doh, v) - delta + dlse)).astype(BF16)
                dq = dq + _nn(ds, kh)
                dk = dk + _tn(ds, qh)
            dq_s[rows, :] = dq * (1.0 / math.sqrt(HEAD_DIM))
            dk_acc[keys, :] += dk
            dv_acc[keys, :] += dv
            return carry

        for r in range(d):
            sub = _sub_rows(d, n, r)
            rot = (cos_ref[sub, :], lo_ref[sub, :], hi_ref[sub, :])
            _att_load(q_ref, k_ref, v_ref, rot, sub, qs, ks, vs)
            if d > 1:
                for dst, src in zip(gathered, (o_ref, l_ref, do_ref, dl_ref)):
                    dst[...] = src[sub, :]
            dk_acc[...] = jnp.zeros_like(dk_acc)
            dv_acc[...] = jnp.zeros_like(dv_acc)
            lax.fori_loop(0, nb, step, 0, unroll=min(nb, 4))
            dq_ref[sub, :] = _rope_transposed(dq_s[...], *rot)
            dk_ref[sub, :] = _rope_transposed(dk_acc[...], *rot)
            dv_ref[sub, :] = dv_acc[...]

    qkv, tab_specs, head_in, head_out = _att_specs(gi, n_rows)
    out = jax.ShapeDtypeStruct((n_rows, KV_WIDTH), F32)
    sub_f32 = pltpu.VMEM((n, LANES), F32)
    return pl.pallas_call(
        body,
        name=name,
        grid=(3,),
        in_specs=qkv + tab_specs + [head_in] * 4,
        out_specs=[head_out] * 3,
        out_shape=[out] * 3,
        scratch_shapes=[pltpu.VMEM((n, LANES), BF16)] * 3 + [sub_f32] * (3 if d == 1 else 7),
        compiler_params=_params(("parallel",)),
    )(proj, proj, proj, *tabs, o, l, do, dl)


def _expand_np():
    e = np.zeros((SSM_N, SSM_N * SSM_P), np.float32)
    for nn in range(SSM_N):
        e[nn, nn * SSM_P:(nn + 1) * SSM_P] = 1.0
    return e


def _ssm_prep_math(a_re, a_im, log_dt, b_re, b_im, expand):
    dt = jnp.exp(log_dt)
    mag = jnp.exp(a_re * dt)
    ab_re, ab_im = mag * jnp.cos(a_im * dt), mag * jnp.sin(a_im * dt)
    den = a_re * a_re + a_im * a_im
    num_re, num_im = ab_re - 1.0, ab_im
    co_re = (num_re * a_re + num_im * a_im) / den
    co_im = (num_im * a_re - num_re * a_im) / den
    hi = lax.Precision.HIGHEST
    co_re_x = jnp.dot(co_re, expand, precision=hi, preferred_element_type=F32)
    co_im_x = jnp.dot(co_im, expand, precision=hi, preferred_element_type=F32)
    bb_re = co_re_x * b_re - co_im_x * b_im
    bb_im = co_re_x * b_im + co_im_x * b_re
    return ab_re, ab_im, bb_re, bb_im


def _ssm_prep(a_re, a_im, log_dt, b_re, b_im, expand, name):
    def body(ar, ai, ld, br, bi, ex, o0, o1, o2, o3):
        outs = _ssm_prep_math(ar[...], ai[...], ld[...], br[...], bi[...], ex[...])
        for ref, v in zip((o0, o1, o2, o3), outs):
            ref[...] = v

    gn = jax.ShapeDtypeStruct((SSM_G, SSM_N), F32)
    gnp = jax.ShapeDtypeStruct((SSM_G, SSM_N * SSM_P), F32)
    return pl.pallas_call(body, name=name, out_shape=[gn, gn, gnp, gnp], compiler_params=_params())(
        a_re, a_im, log_dt, b_re, b_im, expand)


def _ssm_prep_bwd(a_re, a_im, log_dt, b_re, b_im, expand, cts, name):
    def body(ar, ai, ld, br, bi, ex, c0, c1, c2, c3, o0, o1, o2, o3, o4):
        ex_v = ex[...]
        _, vjp = jax.vjp(lambda *p: _ssm_prep_math(*p, ex_v), ar[...], ai[...], ld[...], br[...], bi[...])
        for ref, v in zip((o0, o1, o2, o3, o4), vjp((c0[...], c1[...], c2[...], c3[...]))):
            ref[...] = v

    gn = jax.ShapeDtypeStruct((SSM_G, SSM_N), F32)
    gnp = jax.ShapeDtypeStruct((SSM_G, SSM_N * SSM_P), F32)
    g1 = jax.ShapeDtypeStruct((SSM_G, 1), F32)
    return pl.pallas_call(body, name=name, out_shape=[gn, gn, g1, gnp, gnp], compiler_params=_params())(
        a_re, a_im, log_dt, b_re, b_im, expand, *cts)


def _block_diag_in(bb):
    t = bb.reshape(SSM_TILES, 8, SSM_N, SSM_P).transpose(0, 1, 3, 2)
    eye = jnp.eye(8, dtype=bb.dtype)
    return (t[:, :, :, None, :] * eye[None, :, None, :, None]).reshape(SSM_TILES, LANES, SSM_TILE_GN)


def _block_diag_in_grad(dblk):
    t = dblk.reshape(SSM_TILES, 8, SSM_P, 8, SSM_N)
    t = jnp.einsum("tapbn,ab->tapn", t, jnp.eye(8, dtype=dblk.dtype))
    return t.transpose(0, 1, 3, 2).reshape(SSM_G, SSM_N, SSM_P)


def _block_diag_out(cm):
    t = cm.reshape(SSM_TILES, 8, SSM_P, SSM_N).transpose(0, 1, 3, 2)
    eye = jnp.eye(8, dtype=cm.dtype)
    return (t[:, :, :, None, :] * eye[None, :, None, :, None]).reshape(SSM_TILES, SSM_TILE_GN, LANES)


def _block_diag_out_grad(dblk):
    t = dblk.reshape(SSM_TILES, 8, SSM_N, 8, SSM_P)
    t = jnp.einsum("tanbp,ab->tanp", t, jnp.eye(8, dtype=dblk.dtype))
    return t.transpose(0, 1, 3, 2).reshape(SSM_G, SSM_P, SSM_N)


def _cmul_add(a_re, a_im, s_re, s_im, b_re, b_im):
    return a_re * s_re - a_im * s_im + b_re, a_re * s_im + a_im * s_re + b_im


def _lane_tile_specs(first_tile, index):
    return [pl.BlockSpec((SSM_CHUNK, LANES), lambda c, t=t: (index(c), first_tile + t)) for t in range(SSM_TILES)]


def _ssm_load_rows(src_refs, dst):
    for t in range(SSM_TILES):
        for i in range(SSM_SEG_LEN):
            dst[i * SSM_SEGS:(i + 1) * SSM_SEGS, t * LANES:(t + 1) * LANES] = (
                src_refs[t][pl.ds(i, SSM_SEGS, stride=SSM_SEG_LEN), :])


def _ssm_store_rows(src, dst_refs):
    for t in range(SSM_TILES):
        for i in range(SSM_SEG_LEN):
            dst_refs[t][pl.ds(i, SSM_SEGS, stride=SSM_SEG_LEN), :] = (
                src[i * SSM_SEGS:(i + 1) * SSM_SEGS, t * LANES:(t + 1) * LANES])


def _ssm_powers(ab_re_ref, ab_im_ref, pw_re, pw_im):
    a_re, a_im = ab_re_ref[...], ab_im_ref[...]
    p_re, p_im = a_re, a_im
    for i in range(SSM_SEG_LEN):
        pw_re[i:i + 1, :] = p_re
        pw_im[i:i + 1, :] = p_im
        p_re, p_im = _cmul_add(a_re, a_im, p_re, p_im, 0.0, 0.0)


def _ssm_input_proj(u_s, bblk_re_ref, bblk_im_ref, s_re, s_im):
    for t in range(SSM_TILES):
        ub = u_s[:, t * LANES:(t + 1) * LANES].astype(BF16)
        cols = slice(t * SSM_TILE_GN, (t + 1) * SSM_TILE_GN)
        s_re[:, cols] = _nn(ub, bblk_re_ref[t])
        s_im[:, cols] = _nn(ub, bblk_im_ref[t])


def _ssm_scan(ab_re_ref, ab_im_ref, s_re, s_im, init_re, init_im, conj, reverse):
    sign = -1.0 if conj else 1.0
    for t in range(SSM_TILES):
        cols = slice(t * SSM_TILE_GN, (t + 1) * SSM_TILE_GN)
        a_re = jnp.broadcast_to(ab_re_ref[:, cols], (SSM_SEGS, SSM_TILE_GN))
        a_im = jnp.broadcast_to(ab_im_ref[:, cols], (SSM_SEGS, SSM_TILE_GN)) * sign
        if init_re is None:
            st = (jnp.zeros((SSM_SEGS, SSM_TILE_GN), F32),) * 2
        else:
            st = (init_re[:, cols], init_im[:, cols])

        def step(i, st, cols=cols, a_re=a_re, a_im=a_im):
            idx = (SSM_SEG_LEN - 1 - i) if reverse else i
            rows = pl.ds(pl.multiple_of(idx * SSM_SEGS, SSM_SEGS), SSM_SEGS)
            n_re, n_im = _cmul_add(a_re, a_im, st[0], st[1], s_re[rows, cols], s_im[rows, cols])
            s_re[rows, cols] = n_re
            s_im[rows, cols] = n_im
            return n_re, n_im
        lax.fori_loop(0, SSM_SEG_LEN, step, st, unroll=4)


def _ssm_fixup(pw_re, pw_im, s_re, s_im, cin_re, cin_im, conj, reverse):
    sign = -1.0 if conj else 1.0
    c_re, c_im = cin_re[...], cin_im[...]

    def step(i, c):
        k = (SSM_SEG_LEN - 1 - i) if reverse else i
        rows = pl.ds(pl.multiple_of(i * SSM_SEGS, SSM_SEGS), SSM_SEGS)
        p_re = jnp.broadcast_to(pw_re[pl.ds(k, 1), :], (SSM_SEGS, SSM_GN))
        p_im = jnp.broadcast_to(pw_im[pl.ds(k, 1), :], (SSM_SEGS, SSM_GN)) * sign
        n_re, n_im = _cmul_add(p_re, p_im, c_re, c_im, s_re[rows, :], s_im[rows, :])
        s_re[rows, :] = n_re
        s_im[rows, :] = n_im
        return c
    lax.fori_loop(0, SSM_SEG_LEN, step, 0)


def _ssm_fwd(proj, ab_re, ab_im, bblk_re, bblk_im, cblk_re, cblk_im, d_row, name):
    n_rows = proj.shape[0]
    nchunk = n_rows // SSM_CHUNK
    last = SSM_SEG_LEN - 1
    nt = SSM_TILES

    def body(*refs):
        u_ref, y_ref = refs[:nt], refs[nt + 7:2 * nt + 7]
        ar_ref, ai_ref, br_ref, bi_ref, cr_ref, ci_ref, d_ref = refs[nt:nt + 7]
        cin_re_ref, cin_im_ref, u_s, s_re, s_im, pw_re, pw_im, st_re, st_im = refs[2 * nt + 7:]

        @pl.when(pl.program_id(0) == 0)
        def _():
            _ssm_powers(ar_ref, ai_ref, pw_re, pw_im)
            st_re[...] = jnp.zeros_like(st_re)
            st_im[...] = jnp.zeros_like(st_im)

        _ssm_load_rows(u_ref, u_s)
        _ssm_input_proj(u_s, br_ref, bi_ref, s_re, s_im)
        _ssm_scan(ar_ref, ai_ref, s_re, s_im, None, None, conj=False, reverse=False)
        p_re, p_im = pw_re[last:last + 1, :], pw_im[last:last + 1, :]
        c_re, c_im = st_re[...], st_im[...]
        for j in range(SSM_SEGS):
            cin_re_ref[j:j + 1, :] = c_re
            cin_im_ref[j:j + 1, :] = c_im
            row = last * SSM_SEGS + j
            c_re, c_im = _cmul_add(p_re, p_im, c_re, c_im, s_re[row:row + 1, :], s_im[row:row + 1, :])
        st_re[...] = c_re
        st_im[...] = c_im
        _ssm_fixup(pw_re, pw_im, s_re, s_im, cin_re_ref, cin_im_ref, conj=False, reverse=False)
        for t in range(SSM_TILES):
            cols = slice(t * SSM_TILE_GN, (t + 1) * SSM_TILE_GN)
            lanes = slice(t * LANES, (t + 1) * LANES)
            y = _nn(s_re[:, cols].astype(BF16), cr_ref[t]) - _nn(s_im[:, cols].astype(BF16), ci_ref[t])
            u_s[:, lanes] = y + d_ref[:, lanes] * u_s[:, lanes]
        _ssm_store_rows(u_s, y_ref)

    whole2 = lambda a: pl.BlockSpec(a.shape, lambda c: (0, 0))
    whole3 = lambda a: pl.BlockSpec(a.shape, lambda c: (0, 0, 0))
    seg = pl.BlockSpec((SSM_SEGS, SSM_GN), lambda c: (c, 0))
    seg_shape = jax.ShapeDtypeStruct((nchunk * SSM_SEGS, SSM_GN), F32)
    res = pl.pallas_call(
        body,
        name=name,
        grid=(nchunk,),
        in_specs=_lane_tile_specs((IN_WIDTH - SSM_WIDTH) // LANES, lambda c: c) + [
            whole2(ab_re), whole2(ab_im), whole3(bblk_re), whole3(bblk_im), whole3(cblk_re), whole3(cblk_im),
            whole2(d_row)],
        out_specs=[pl.BlockSpec((SSM_CHUNK, LANES), lambda c: (c, 0))] * nt + [seg, seg],
        out_shape=[jax.ShapeDtypeStruct((n_rows, LANES), F32)] * nt + [seg_shape, seg_shape],
        scratch_shapes=[pltpu.VMEM((SSM_CHUNK, SSM_WIDTH), F32), pltpu.VMEM((SSM_CHUNK, SSM_GN), F32),
                        pltpu.VMEM((SSM_CHUNK, SSM_GN), F32), pltpu.VMEM((SSM_SEG_LEN, SSM_GN), F32),
                        pltpu.VMEM((SSM_SEG_LEN, SSM_GN), F32), pltpu.VMEM((1, SSM_GN), F32),
                        pltpu.VMEM((1, SSM_GN), F32)],
        compiler_params=_params(("arbitrary",)),
    )(*[proj] * nt, ab_re, ab_im, bblk_re, bblk_im, cblk_re, cblk_im, d_row)
    return res[:nt], res[nt], res[nt + 1]


def _ssm_bwd(proj, dy, cin_re, cin_im, ab_re, ab_im, bblk_re, bblk_im, cblk_re, cblk_im, d_row, name):
    n_rows = proj.shape[0]
    nchunk = n_rows // SSM_CHUNK
    nt = SSM_TILES

    def body(*refs):
        u_ref, dy_ref, du_ref = refs[:nt], refs[nt:2 * nt], refs[2 * nt + 9:3 * nt + 9]
        cin_re_ref, cin_im_ref, ar_ref, ai_ref, br_ref, bi_ref, cr_ref, ci_ref, d_ref = refs[2 * nt:2 * nt + 9]
        (dar_ref, dai_ref, dbr_ref, dbi_ref, dcr_ref, dci_ref, dd_ref,
         u_s, dy_s, s_re, s_im, q_re, q_im, pw_re, pw_im, qst_re, qst_im, qin_re, qin_im) = refs[3 * nt + 9:]

        @pl.when(pl.program_id(0) == 0)
        def _():
            _ssm_powers(ar_ref, ai_ref, pw_re, pw_im)
            qst_re[...] = jnp.zeros_like(qst_re)
            qst_im[...] = jnp.zeros_like(qst_im)
            for ref in (dar_ref, dai_ref, dbr_ref, dbi_ref, dcr_ref, dci_ref, dd_ref):
                ref[...] = jnp.zeros_like(ref)

        _ssm_load_rows(u_ref, u_s)
        _ssm_load_rows(dy_ref, dy_s)
        _ssm_input_proj(u_s, br_ref, bi_ref, s_re, s_im)
        _ssm_scan(ar_ref, ai_ref, s_re, s_im, cin_re_ref, cin_im_ref, conj=False, reverse=False)
        for t in range(SSM_TILES):
            cols = slice(t * SSM_TILE_GN, (t + 1) * SSM_TILE_GN)
            dyb = dy_s[:, t * LANES:(t + 1) * LANES].astype(BF16)
            q_re[:, cols] = _nt(dyb, cr_ref[t])
            q_im[:, cols] = -_nt(dyb, ci_ref[t])
            dcr_ref[t] += _tn(s_re[:, cols].astype(BF16), dyb)
            dci_ref[t] -= _tn(s_im[:, cols].astype(BF16), dyb)
        _ssm_scan(ar_ref, ai_ref, q_re, q_im, None, None, conj=True, reverse=True)
        last = SSM_SEG_LEN - 1
        p_re, p_im = pw_re[last:last + 1, :], -pw_im[last:last + 1, :]
        c_re, c_im = qst_re[...], qst_im[...]
        for j in reversed(range(SSM_SEGS)):
            qin_re[j:j + 1, :] = c_re
            qin_im[j:j + 1, :] = c_im
            c_re, c_im = _cmul_add(p_re, p_im, c_re, c_im, q_re[j:j + 1, :], q_im[j:j + 1, :])
        qst_re[...] = c_re
        qst_im[...] = c_im
        _ssm_fixup(pw_re, pw_im, q_re, q_im, qin_re, qin_im, conj=True, reverse=True)
        for t in range(SSM_TILES):
            cols = slice(t * SSM_TILE_GN, (t + 1) * SSM_TILE_GN)

            def step(i, acc, cols=cols):
                rows = pl.ds(pl.multiple_of(i * SSM_SEGS, SSM_SEGS), SSM_SEGS)
                prev = pl.ds(pl.multiple_of((i - 1) * SSM_SEGS, SSM_SEGS), SSM_SEGS)
                qr, qi = q_re[rows, cols], q_im[rows, cols]
                sr, si = s_re[prev, cols], s_im[prev, cols]
                return acc[0] + qr * sr + qi * si, acc[1] + qi * sr - qr * si

            qr, qi = q_re[0:SSM_SEGS, cols], q_im[0:SSM_SEGS, cols]
            sr, si = cin_re_ref[:, cols], cin_im_ref[:, cols]
            acc = lax.fori_loop(1, SSM_SEG_LEN, step, (qr * sr + qi * si, qi * sr - qr * si))
            dar_ref[:, cols] += jnp.sum(acc[0], axis=0, keepdims=True)
            dai_ref[:, cols] += jnp.sum(acc[1], axis=0, keepdims=True)
        for t in range(SSM_TILES):
            cols = slice(t * SSM_TILE_GN, (t + 1) * SSM_TILE_GN)
            lanes = slice(t * LANES, (t + 1) * LANES)
            qrb, qib = q_re[:, cols].astype(BF16), q_im[:, cols].astype(BF16)
            u_t, dy_t = u_s[:, lanes], dy_s[:, lanes]
            ub = u_t.astype(BF16)
            dbr_ref[t] += _tn(ub, qrb)
            dbi_ref[t] += _tn(ub, qib)
            dd_ref[:, lanes] += jnp.sum(dy_t * u_t, axis=0, keepdims=True)
            u_s[:, lanes] = _nt(qrb, br_ref[t]) + _nt(qib, bi_ref[t]) + dy_t * d_ref[:, lanes]
        _ssm_store_rows(u_s, du_ref)

    whole2 = lambda a: pl.BlockSpec(a.shape, lambda c: (0, 0))
    whole3 = lambda a: pl.BlockSpec(a.shape, lambda c: (0, 0, 0))
    back = lambda c: nchunk - 1 - c
    seg = pl.BlockSpec((SSM_SEGS, SSM_GN), lambda c: (back(c), 0))
    gn_row = jax.ShapeDtypeStruct((1, SSM_GN), F32)
    b_shape = jax.ShapeDtypeStruct((SSM_TILES, LANES, SSM_TILE_GN), F32)
    c_shape = jax.ShapeDtypeStruct((SSM_TILES, SSM_TILE_GN, LANES), F32)
    d_shape = jax.ShapeDtypeStruct((1, SSM_WIDTH), F32)
    big = pltpu.VMEM((SSM_CHUNK, SSM_GN), F32)
    res = pl.pallas_call(
        body,
        name=name,
        grid=(nchunk,),
        in_specs=_lane_tile_specs((IN_WIDTH - SSM_WIDTH) // LANES, back) + _lane_tile_specs(0, back) + [
            seg, seg, whole2(ab_re), whole2(ab_im), whole3(bblk_re), whole3(bblk_im), whole3(cblk_re),
            whole3(cblk_im), whole2(d_row)],
        out_specs=[pl.BlockSpec((SSM_CHUNK, LANES), lambda c: (back(c), 0))] * nt + [
            whole2(ab_re), whole2(ab_im), whole3(bblk_re), whole3(bblk_im), whole3(cblk_re), whole3(cblk_im),
            whole2(d_row)],
        out_shape=[jax.ShapeDtypeStruct((n_rows, LANES), F32)] * nt + [gn_row, gn_row, b_shape, b_shape, c_shape,
                                                                       c_shape, d_shape],
        scratch_shapes=[pltpu.VMEM((SSM_CHUNK, SSM_WIDTH), F32), pltpu.VMEM((SSM_CHUNK, SSM_WIDTH), F32),
                        big, big, big, big,
                        pltpu.VMEM((SSM_SEG_LEN, SSM_GN), F32), pltpu.VMEM((SSM_SEG_LEN, SSM_GN), F32),
                        pltpu.VMEM((1, SSM_GN), F32), pltpu.VMEM((1, SSM_GN), F32),
                        pltpu.VMEM((SSM_SEGS, SSM_GN), F32), pltpu.VMEM((SSM_SEGS, SSM_GN), F32)],
        compiler_params=_params(("arbitrary",)),
    )(*[proj] * nt, *[dy] * nt, cin_re, cin_im, ab_re, ab_im, bblk_re, bblk_im, cblk_re, cblk_im, d_row)
    return (res[:nt], *res[nt:])


def _mlp_fwd(h2, w1, w2, tm, tf, name):
    n_rows, dm = h2.shape
    dff = w1.shape[1]

    def body(h_ref, w1_ref, w2_ref, a_ref, y_ref):
        a = _nn(h_ref[...], w1_ref[...])
        a_ref[...] = a.astype(BF16)
        r = jnp.maximum(a, 0.0)
        part = _nn((r * r).astype(BF16), w2_ref[...])
        j = pl.program_id(1)

        @pl.when(j == 0)
        def _():
            y_ref[...] = part

        @pl.when(j > 0)
        def _():
            y_ref[...] += part

    return pl.pallas_call(
        body,
        name=name,
        grid=(n_rows // tm, dff // tf),
        in_specs=[pl.BlockSpec((tm, dm), lambda i, j: (i, 0)), pl.BlockSpec((dm, tf), lambda i, j: (0, j)),
                  pl.BlockSpec((tf, dm), lambda i, j: (j, 0))],
        out_specs=[pl.BlockSpec((tm, tf), lambda i, j: (i, j)), pl.BlockSpec((tm, dm), lambda i, j: (i, 0))],
        out_shape=[jax.ShapeDtypeStruct((n_rows, dff), BF16), jax.ShapeDtypeStruct((n_rows, dm), F32)],
        compiler_params=_params(("parallel", "arbitrary")),
    )(h2, w1, w2)


def _mlp_bwd(dy, h2, a, w2, tm, tf, name):
    n_rows, dm = h2.shape
    dff = a.shape[1]
    per_chip = dff // N_CHIPS // tf

    def body(dy_ref, h_ref, a_ref, w2_ref, da_ref, dw2_ref, dw1_ref):
        dyb = dy_ref[...]
        r = jnp.maximum(a_ref[...].astype(F32), 0.0)
        da = (_nt(dyb, w2_ref[...]) * (2.0 * r)).astype(BF16)
        da_ref[...] = da
        p2 = _tn((r * r).astype(BF16), dyb)
        p1 = _tn(h_ref[...], da)
        i = pl.program_id(1)

        @pl.when(i == 0)
        def _():
            dw2_ref[...] = p2
            dw1_ref[...] = p1

        @pl.when(i > 0)
        def _():
            dw2_ref[...] += p2
            dw1_ref[...] += p1

    return pl.pallas_call(
        body,
        name=name,
        grid=(dff // tf, n_rows // tm),
        in_specs=[pl.BlockSpec((tm, dm), lambda j, i: (i, 0)), pl.BlockSpec((tm, dm), lambda j, i: (i, 0)),
                  pl.BlockSpec((tm, tf), lambda j, i: (i, j)), pl.BlockSpec((tf, dm), lambda j, i: (j, 0))],
        out_specs=[pl.BlockSpec((tm, tf), lambda j, i: (i, j)), pl.BlockSpec((tf, dm), lambda j, i: (j, 0)),
                   pl.BlockSpec((None, dm, tf), lambda j, i: (j // per_chip, 0, j % per_chip))],
        out_shape=[jax.ShapeDtypeStruct((n_rows, dff), BF16), jax.ShapeDtypeStruct((dff, dm), F32),
                   jax.ShapeDtypeStruct((N_CHIPS, dm, dff // N_CHIPS), F32)],
        compiler_params=_params(("parallel", "arbitrary")),
    )(dy, h2, a, w2)


def _local_step(x, pos_col, mod, target, wts, small, hooks=None):
    n_rows = x.shape[0]
    sh1, sc1, gt1, sh2, sc2, gt2 = (mod[:, i * D_MODEL:(i + 1) * D_MODEL] for i in range(N_MOD))
    tm = 256
    d_acc = (1, D_MODEL)

    (h1,) = _rowwise(lambda r, c: [_mod_norm(r[0], *c)], [x], [small["g_pre_mix"], sc1, sh1],
                     [(D_MODEL, BF16)], [], tm, "pre_mix_fwd")
    proj = _matmul(h1, wts["w_in"], "nn", F32, 1024, 1408, 2048, "in_proj")

    freqs = ROPE_THETA ** (-jnp.arange(0, ROT_DIM, 2, dtype=F32) / ROT_DIM)
    freq_lane = jnp.tile(freqs, LANES // (ROT_DIM // 2))[None, :]
    tabs = _rope_tables(pos_col, freq_lane, "rope_tables")
    att = [_att_fwd(proj, tabs, gi, f"att_fwd_{gi}") for gi in range(3)]

    expand = jnp.asarray(_expand_np())
    b_re2, b_im2 = small["ssm_b_re"].reshape(SSM_G, -1), small["ssm_b_im"].reshape(SSM_G, -1)
    log_dt = small["ssm_log_dt"].reshape(SSM_G, 1)
    prep_in = (small["ssm_a_re"], small["ssm_a_im"], log_dt, b_re2, b_im2, expand)
    ab_re, ab_im, bb_re, bb_im = _ssm_prep(*prep_in, "ssm_prep")
    ab_re_row, ab_im_row = ab_re.reshape(1, SSM_GN), ab_im.reshape(1, SSM_GN)
    bblk = [_block_diag_in(t.reshape(SSM_G, SSM_N, SSM_P)).astype(BF16) for t in (bb_re, bb_im)]
    cblk = [_block_diag_out(small[k]).astype(BF16) for k in ("ssm_c_re", "ssm_c_im")]
    d_row = small["ssm_d"].reshape(1, SSM_WIDTH)
    if hooks is not None:
        d_row = _tie(d_row, hooks.mixer.arrived(att[2][0]))
    y_tiles, cin_re, cin_im = _ssm_fwd(proj, ab_re_row, ab_im_row, *bblk, *cblk, d_row, "ssm_fwd")
    y_tiles = list(y_tiles)
    n_mix = 6 + SSM_TILES

    def mixers_out(r, c):
        w_glu, b_glu, g_att, g_ssm = c
        att_n = _att_mix(*r[:6], g_att)
        y2 = _gelu(jnp.concatenate(r[6:n_mix], axis=1))
        z = _nn(y2.astype(BF16), w_glu) + b_glu
        return [jnp.concatenate([att_n.astype(BF16), _glu_out(y2, z, g_ssm).astype(BF16)], axis=1)]

    att_rows = [a[0] for a in att] + [a[1] for a in att]
    if hooks is not None:
        wts = {**wts, **hooks.mixer.weights(y_tiles[0])}
    mix_consts = [wts["w_glu"], small["b_glu"], small["g_attn_out"], small["g_ssm_out"]]
    (cat,) = _rowwise(mixers_out, att_rows + y_tiles, mix_consts, [(OUT_IN_WIDTH, BF16)], [], tm, "mixers_out_fwd")
    mix = _matmul(cat, wts["w_out"], "nn", F32, 1024, 1024, 1280, "out_proj")

    post_consts = [small["g_post_mix"], gt1, small["g_pre_mlp"], sc2, sh2]
    if hooks is not None:
        post_consts[0] = _tie(post_consts[0], hooks.mlp.arrived(mix))
    x1, h2 = _rowwise(lambda r, c: list(_post_mix(r[0], r[1], *c)), [x, mix], post_consts,
                      [(D_MODEL, F32), (D_MODEL, BF16)], [], tm, "post_mix_fwd")
    if hooks is not None:
        wts = {**wts, **hooks.mlp.weights(h2)}
    w_mlp_in, w_mlp_out = wts["w_mlp_in"], wts["w_mlp_out"]
    a_mlp, y_mlp = _mlp_fwd(h2, w_mlp_in, w_mlp_out, 1024, 1024, "mlp_fwd")

    def loss_head(r, c):
        x1_v, y_v, t_v = r
        g, gt = c
        fn = lambda y_, g_, gt_: gt_ * _rms(y_, g_)
        out, vjp = jax.vjp(fn, y_v, g, gt)
        err = x1_v + out - t_v
        dx2 = err * (1.0 / D_MODEL)
        dy, dg, dgt = vjp(dx2)
        loss = 0.5 * jnp.sum(jnp.sum(err * err, axis=1, keepdims=True), axis=0, keepdims=True) * (1.0 / D_MODEL)
        return [dx2, dy, loss, dg, dgt]

    dx2, dy_mlp, loss, dg_post_mlp, dgt2 = _rowwise(
        loss_head, [x1, y_mlp, target], [small["g_post_mlp"], gt2],
        [(D_MODEL, F32), (D_MODEL, BF16)], [(1, 1), d_acc, d_acc], tm, "loss_head")

    da_mlp, dw_mlp_out, dw_mlp_in = _mlp_bwd(dy_mlp, h2, a_mlp, w_mlp_out, 1024, 512, "mlp_bwd")
    dw_mlp_out = dw_mlp_out.reshape(dw_mlp_in.shape)
    sent = None if hooks is None else hooks.mlp_grads_to_sibling(dw_mlp_in, dw_mlp_out)
    dh2 = _matmul(da_mlp, w_mlp_in, "nt", F32, 1024, 1024, 2048, "mlp_in_bwd", after=sent)
    sent = None if hooks is None else hooks.mlp_grads_to_chips(dh2)

    def post_mix_bwd(r, c):
        x_v, mix_v, dx1_v, dh2_v = r
        _, vjp = jax.vjp(_post_mix, x_v, mix_v, *c)
        return list(vjp((dx1_v, dh2_v)))

    post_consts_bwd = post_consts if sent is None else [_tie(post_consts[0], sent)] + post_consts[1:]
    dx_a, dmix, dg_post_mix, dgt1, dg_pre_mlp, dsc2, dsh2 = _rowwise(
        post_mix_bwd, [x, mix, dx2, dh2], post_consts_bwd, [(D_MODEL, F32), (D_MODEL, BF16)], [d_acc] * 5, tm,
        "post_mix_bwd")

    dcat = _matmul(dmix, wts["w_out"], "nt", F32, 1024, 1280, 2048, "out_proj_bwd")
    dw_out = _matmul(cat, dmix, "tn", F32, 1280, 1024, 1024, "out_proj_wgrad")

    def mixers_out_bwd(r, c):
        w_glu, b_glu, g_att, g_ssm = c
        dcat_v = r[n_mix]
        _, vjp_att = jax.vjp(_att_mix, *r[:6], g_att)
        *d_ol, dg_att = vjp_att(dcat_v[:, :KV_WIDTH])
        y2, vjp_gelu = jax.vjp(_gelu, jnp.concatenate(r[6:n_mix], axis=1))
        y2b = y2.astype(BF16)
        z = _nn(y2b, w_glu) + b_glu
        _, vjp_glu = jax.vjp(_glu_out, y2, z, g_ssm)
        dy2, dz, dg_ssm = vjp_glu(dcat_v[:, KV_WIDTH:])
        dzb = dz.astype(BF16)
        (dy,) = vjp_gelu(dy2 + _nt(dzb, w_glu))
        return d_ol + [dy, dg_att, _tn(y2b, dzb), jnp.sum(dz, axis=0, keepdims=True), dg_ssm]

    *d_att, dy_ssm, dg_attn_out, dw_glu, db_glu, dg_ssm_out = _rowwise(
        mixers_out_bwd, att_rows + y_tiles + [dcat], mix_consts,
        [(KV_WIDTH, F32)] * 6 + [(SSM_WIDTH, F32)],
        [(1, KV_WIDTH), (SSM_WIDTH, SSM_WIDTH), (1, SSM_WIDTH), (1, SSM_WIDTH)], tm, "mixers_out_bwd")

    du_tiles, dab_re, dab_im, dbblk_re, dbblk_im, dcblk_re, dcblk_im, dd_row = _ssm_bwd(
        proj, dy_ssm, cin_re, cin_im, ab_re_row, ab_im_row, *bblk, *cblk, d_row, "ssm_bwd")
    prep_cts = (dab_re.reshape(SSM_G, SSM_N), dab_im.reshape(SSM_G, SSM_N),
                _block_diag_in_grad(dbblk_re).reshape(SSM_G, -1), _block_diag_in_grad(dbblk_im).reshape(SSM_G, -1))
    da_re, da_im, dlog_dt, db_re, db_im = _ssm_prep_bwd(*prep_in, prep_cts, "ssm_prep_bwd")

    dqkv = [_att_bwd(proj, tabs, att[gi][0], att[gi][1], d_att[gi], d_att[3 + gi], gi, f"att_bwd_{gi}")
            for gi in range(3)]

    def gather_dproj(r, c):
        dq = [r[0], r[3], r[6]]
        dk = r[1] + r[4] + r[7]
        dv = r[2] + r[5] + r[8]
        return [jnp.concatenate([t.astype(BF16) for t in dq + [dk, dv] + r[9:]], axis=1)]

    (dproj,) = _rowwise(gather_dproj, [t for g in dqkv for t in g] + list(du_tiles), [], [(IN_WIDTH, BF16)], [], tm,
                        "gather_dproj")
    dh1 = _matmul(dproj, wts["w_in"], "nt", F32, 1024, 1024, 2816, "in_proj_bwd")
    dw_in = _matmul(h1, dproj, "tn", F32, 1024, 1408, 1024, "in_proj_wgrad")

    def pre_mix_bwd(r, c):
        x_v, dh1_v, dxa_v = r
        _, vjp = jax.vjp(_mod_norm, x_v, *c)
        dx, dg, dsc, dsh = vjp(dh1_v)
        return [dx + dxa_v, dg, dsc, dsh]

    grad_x, dg_pre_mix, dsc1, dsh1 = _rowwise(
        pre_mix_bwd, [x, dh1, dx_a], [small["g_pre_mix"], sc1, sh1], [(D_MODEL, F32)], [d_acc] * 3, tm, "pre_mix_bwd")

    dmod = jnp.concatenate([dsh1, dsc1, dgt1, dsh2, dsc2, dgt2], axis=1)
    big = dict(w_in=dw_in, w_out=dw_out, w_mlp_in=dw_mlp_in, w_mlp_out=dw_mlp_out, w_glu=dw_glu)
    small_g = dict(
        g_pre_mix=dg_pre_mix, g_post_mix=dg_post_mix, ssm_a_re=da_re, ssm_a_im=da_im,
        ssm_log_dt=dlog_dt.reshape(1, SSM_G), ssm_b_re=db_re.reshape(SSM_G, SSM_N, SSM_P),
        ssm_b_im=db_im.reshape(SSM_G, SSM_N, SSM_P), ssm_c_re=_block_diag_out_grad(dcblk_re),
        ssm_c_im=_block_diag_out_grad(dcblk_im), ssm_d=dd_row.reshape(SSM_G, SSM_P), b_glu=db_glu,
        g_attn_out=dg_attn_out, g_ssm_out=dg_ssm_out, g_pre_mlp=dg_pre_mlp, g_post_mlp=dg_post_mlp)
    return loss, grad_x, dmod, big, small_g


MESH_ID = pl.DeviceIdType.MESH
N_DEV = 8
N_CHIPS = 4
HBM_SPEC = pl.BlockSpec(memory_space=pltpu.HBM)


def _place():
    x, y, c = lax.axis_index("x"), lax.axis_index("y"), lax.axis_index("c")
    other_chips = [(1 - x, y), (x, 1 - y), (1 - x, 1 - y)]
    return x, y, c, other_chips


def _half_rows(index, half):
    return pl.ds(pl.multiple_of(index * half, ROW_PAD), half)


def _remote(src, dst, send_sem, recv_sem, dev):
    return pltpu.make_async_remote_copy(src_ref=src, dst_ref=dst, send_sem=send_sem, recv_sem=recv_sem,
                                        device_id=dev, device_id_type=MESH_ID)


def _all_gather8(block, name):
    m_per, n = block.shape

    def body(x_ref, out_ref, send_sems, recv_sems, local_sem):
        x, y, c, chips = _place()
        me, sibling = (x, y, c), (x, y, 1 - c)

        def rows(px, py, pc):
            return out_ref.at[pl.ds((4 * px + 2 * py + pc) * m_per, m_per), :]

        def copy(k, blk, to, src=None):
            return _remote(rows(*blk) if src is None else src, rows(*blk), send_sems.at[k], recv_sems.at[k], to)

        mine = pltpu.make_async_copy(x_ref, rows(*me), local_sem)
        mine.start()
        first = [copy(0, me, sibling, src=x_ref)]
        first += [copy(1 + j, me, (*chip, c), src=x_ref) for j, chip in enumerate(chips)]
        for cp in first:
            cp.start()
        passed = [copy(4 + j, (*chip, c), sibling) for j, chip in enumerate(chips)]
        for j, chip in enumerate(chips):
            copy(1 + j, (*chip, c), me).wait_recv()
            passed[j].start()
        copy(0, sibling, me).wait_recv()
        for j, chip in enumerate(chips):
            copy(4 + j, (*chip, 1 - c), me).wait_recv()
        for cp in first + passed:
            cp.wait_send()
        mine.wait()

    return pl.pallas_call(
        body,
        name=name,
        out_shape=jax.ShapeDtypeStruct((N_DEV * m_per, n), block.dtype),
        in_specs=[pl.BlockSpec(memory_space=pltpu.VMEM)],
        out_specs=pl.BlockSpec(memory_space=pltpu.VMEM),
        scratch_shapes=[pltpu.SemaphoreType.DMA((7,)), pltpu.SemaphoreType.DMA((7,)), pltpu.SemaphoreType.DMA],
        compiler_params=_params(),
    )(block)


def _weight_gather(shards, name):
    n = len(shards)
    shapes = [s.shape for s in shards]

    def body(*refs):
        ins, outs = refs[:n], refs[n:2 * n]
        send, recv, fsend, frecv = refs[2 * n:]
        x, y, c, chips = _place()
        k_me = 2 * x + y
        sibling = (x, y, 1 - c)
        pending = []
        for a in range(n):
            half = shapes[a][0] // 2
            mine = _half_rows(c, half)
            for j, chip in enumerate(chips):
                cp = _remote(ins[a].at[mine, :], outs[a].at[k_me, mine, :], send.at[a, j], recv.at[a, j], (*chip, c))
                cp.start()
                pending.append(cp.wait_send)
        for a in range(n):
            half = shapes[a][0] // 2
            for j, (px, py) in enumerate(chips):
                piece = outs[a].at[2 * px + py, _half_rows(c, half), :]
                _remote(piece, piece, send.at[a, j], recv.at[a, j], (px, py, c)).wait_recv()
                fw = _remote(piece, piece, fsend.at[a, j], frecv.at[a, j], sibling)
                fw.start()
                pending.append(fw.wait_send)
        for a in range(n):
            half = shapes[a][0] // 2
            for j, (px, py) in enumerate(chips):
                piece = outs[a].at[2 * px + py, _half_rows(1 - c, half), :]
                _remote(piece, piece, fsend.at[a, j], frecv.at[a, j], sibling).wait_recv()
        for wait in pending:
            wait()

    sems = pltpu.SemaphoreType.DMA((n, 3))
    return pl.pallas_call(
        body,
        name=name,
        out_shape=[jax.ShapeDtypeStruct((N_CHIPS,) + s, BF16) for s in shapes],
        in_specs=[HBM_SPEC] * n,
        out_specs=[HBM_SPEC] * n,
        scratch_shapes=[sems, sems, sems, sems],
        compiler_params=_params(),
    )(*shards)


def _sibling_swap(halves, name):
    n = len(halves)
    shapes = [h.shape for h in halves]

    def body(*refs):
        ins, outs = refs[:n], refs[n:2 * n]
        send, recv = refs[2 * n:]
        x, y, c, _ = _place()
        pending = []
        for a in range(n):
            half = shapes[a][0]
            mine = outs[a].at[_half_rows(c, half), :]
            cp = _remote(ins[a], mine, send.at[a], recv.at[a], (x, y, 1 - c))
            cp.start()
            pending.append(cp.wait_send)
        for a in range(n):
            half = shapes[a][0]
            theirs = outs[a].at[_half_rows(1 - c, half), :]
            _remote(theirs, theirs, send.at[a], recv.at[a], (x, y, 1 - c)).wait_recv()
        for wait in pending:
            wait()

    return pl.pallas_call(
        body,
        name=name,
        out_shape=[jax.ShapeDtypeStruct((2 * s[0], s[1]), F32) for s in shapes],
        in_specs=[HBM_SPEC] * n,
        out_specs=[HBM_SPEC] * n,
        scratch_shapes=[pltpu.SemaphoreType.DMA((n,)), pltpu.SemaphoreType.DMA((n,))],
        compiler_params=_params(),
    )(*halves)


SEM_SPEC = pl.BlockSpec(memory_space=pltpu.SEMAPHORE)
ANY_SPEC = pl.BlockSpec(memory_space=pl.ANY)
DATAFLOW = pltpu.SideEffectType.DATAFLOW_SIDE_EFFECTING


def _split_copy_start(srcs, lands, plan, n_sems, name, after=None):
    bufs = list(srcs) + list(lands)
    ns, nb = len(srcs), len(bufs)
    extra = [] if after is None else [after]

    def body(*refs):
        outs = refs[nb + len(extra):]
        for outgoing, _ in plan(refs[:ns], refs[ns:nb], outs[0], outs[1]):
            outgoing.start()
        outs[-1][...] = jnp.zeros_like(outs[-1])

    sems = pltpu.SemaphoreType.DMA((n_sems,))
    return pl.pallas_call(
        body,
        name=name,
        out_shape=(sems, sems, *[pltpu.HBM(b.shape, b.dtype) for b in bufs], jax.ShapeDtypeStruct((8, LANES), F32)),
        in_specs=[HBM_SPEC] * nb + [ANY_SPEC] * len(extra),
        out_specs=(SEM_SPEC, SEM_SPEC, *[HBM_SPEC] * nb, pl.BlockSpec(memory_space=pltpu.VMEM)),
        input_output_aliases={i: 2 + i for i in range(nb)},
        compiler_params=pltpu.CompilerParams(has_side_effects=DATAFLOW),
    )(*[pltpu.with_memory_space_constraint(b, pltpu.HBM) for b in bufs], *extra)


def _split_copy_wait(started, plan, after, name, n_srcs=None):
    send, recv, *bufs = started[:-1]
    nb = len(bufs)
    ns = nb // 2 if n_srcs is None else n_srcs

    def body(*refs):
        for outgoing, incoming in plan(refs[:ns], refs[ns:nb], refs[nb], refs[nb + 1]):
            outgoing.wait_send()
            incoming.wait_recv()

    return pl.pallas_call(
        body,
        name=name,
        out_shape=tuple(pltpu.HBM(b.shape, b.dtype) for b in bufs),
        in_specs=[HBM_SPEC] * nb + [SEM_SPEC, SEM_SPEC, ANY_SPEC],
        out_specs=tuple([HBM_SPEC] * nb),
        input_output_aliases={i: i for i in range(nb)},
        compiler_params=pltpu.CompilerParams(has_side_effects=DATAFLOW),
    )(*bufs, send, recv, after)


def _weight_plan(shapes):
    def plan(srcs, lands, send, recv):
        x, y, c, chips = _place()
        copies = []
        for a in range(len(shapes)):
            mine = _half_rows(c, shapes[a][0] // 2)
            for j, (px, py) in enumerate(chips):
                s = 3 * a + j
                arrival = lands[a].at[2 * px + py, mine, :]
                copies.append((_remote(srcs[a].at[mine, :], lands[a].at[2 * x + y, mine, :], send.at[s], recv.at[s], (px, py, c)),
                               _remote(arrival, arrival, send.at[s], recv.at[s], (px, py, c))))
        return copies
    return plan


def _halves_plan(shapes):
    def plan(srcs, lands, send, recv):
        x, y, c, _ = _place()
        copies = []
        for a in range(len(shapes)):
            theirs = srcs[a].at[:, _half_rows(1 - c, shapes[a][1] // 2), :]
            copies.append((_remote(theirs, lands[a], send.at[a], recv.at[a], (x, y, 1 - c)),
                           _remote(lands[a], lands[a], send.at[a], recv.at[a], (x, y, 1 - c))))
        return copies
    return plan


def _exchange_plan(n):
    def plan(srcs, lands, send, recv):
        x, y, c, chips = _place()
        copies = []
        for a in range(n):
            for j, (px, py) in enumerate(chips):
                s = 3 * a + j
                copies.append((_remote(srcs[a].at[2 * px + py], lands[a].at[j], send.at[s], recv.at[s], (px, py, c)),
                               _remote(lands[a].at[j], lands[a].at[j], send.at[s], recv.at[s], (px, py, c))))
        return copies
    return plan


def _swap_plan(shapes):
    def plan(shards, _, send, recv):
        x, y, c, _ = _place()
        copies = []
        for a in range(len(shapes)):
            half = shapes[a][0] // 2
            mine = shards[a].at[_half_rows(c, half), :]
            theirs = shards[a].at[_half_rows(1 - c, half), :]
            copies.append((_remote(mine, mine, send.at[a], recv.at[a], (x, y, 1 - c)),
                           _remote(theirs, theirs, send.at[a], recv.at[a], (x, y, 1 - c))))
        return copies
    return plan


def _forward_plan(shapes):
    def plan(stacks, _, send, recv):
        x, y, c, chips = _place()
        copies = []
        for a in range(len(shapes)):
            half = shapes[a][0] // 2
            for j, (px, py) in enumerate(chips):
                s = 3 * a + j
                mine = stacks[a].at[2 * px + py, _half_rows(c, half), :]
                theirs = stacks[a].at[2 * px + py, _half_rows(1 - c, half), :]
                copies.append((_remote(mine, mine, send.at[s], recv.at[s], (x, y, 1 - c)),
                               _remote(theirs, theirs, send.at[s], recv.at[s], (x, y, 1 - c))))
        return copies
    return plan


def _tie(x, token):
    return x + token[0:1, 0:1].astype(x.dtype)


ROW_PAD = 16


def _silu(x):
    return x * _sigmoid(x)


def _ada_fwd(c_all, w_ada, b_ada, name):
    dm, cols = w_ada.shape
    tn = 512

    def body(c_ref, w_ref, b_ref, o_ref):
        o_ref[...] = _nn(_silu(c_ref[...]).astype(BF16), w_ref[...].astype(BF16)) + b_ref[...]

    return pl.pallas_call(
        body,
        name=name,
        grid=(cols // tn,),
        in_specs=[pl.BlockSpec((ROW_PAD, dm), lambda j: (0, 0)), pl.BlockSpec((dm, tn), lambda j: (0, j)),
                  pl.BlockSpec((1, tn), lambda j: (0, j))],
        out_specs=pl.BlockSpec((ROW_PAD, tn), lambda j: (0, j)),
        out_shape=jax.ShapeDtypeStruct((ROW_PAD, cols), F32),
        compiler_params=_params(("parallel",)),
    )(c_all, w_ada, b_ada)


def _adamw(w, g, m, v):
    m = ADAM_B1 * m + (1.0 - ADAM_B1) * g
    v = ADAM_B2 * v + (1.0 - ADAM_B2) * (g * g)
    m_hat = m / (1.0 - ADAM_B1 ** ADAM_STEP)
    v_hat = v / (1.0 - ADAM_B2 ** ADAM_STEP)
    delta = -ADAM_LR * (m_hat / (jnp.sqrt(v_hat) + ADAM_EPS) + ADAM_WD * w)
    return delta, m, v


def _ada_bwd_adamw(c_all, dmod_cols, w, m, v, name):
    dm, cols = w.shape
    tm, tn = 512, 512

    def body(c_ref, d_ref, w_ref, m_ref, v_ref, g_ref, dl_ref, nm_ref, nv_ref):
        g = _tn(_silu(c_ref[...]).astype(BF16), d_ref[...].astype(BF16))
        g_ref[...] = g
        dl_ref[...], nm_ref[...], nv_ref[...] = _adamw(w_ref[...], g, m_ref[...], v_ref[...])

    tile = pl.BlockSpec((tm, tn), lambda i, j: (i, j))
    shape = jax.ShapeDtypeStruct((dm, cols), F32)
    return pl.pallas_call(
        body,
        name=name,
        grid=(dm // tm, cols // tn),
        in_specs=[pl.BlockSpec((ROW_PAD, tm), lambda i, j: (0, i)), pl.BlockSpec((ROW_PAD, tn), lambda i, j: (0, j)),
                  tile, tile, tile],
        out_specs=[tile] * 4,
        out_shape=[shape] * 4,
        compiler_params=_params(("parallel", "parallel")),
    )(c_all, dmod_cols, w, m, v)


def _sum_blocks(parts, nblk, name):
    rows, cols = parts.shape[0] // nblk, parts.shape[1]

    def body(p_ref, o_ref):
        tot = p_ref[0:rows, :]
        for b in range(1, nblk):
            tot = tot + p_ref[b * rows:(b + 1) * rows, :]
        o_ref[...] = tot

    return pl.pallas_call(body, name=name, out_shape=jax.ShapeDtypeStruct((rows, cols), F32), compiler_params=_params())(parts)


def _adamw_rows(w, g, m, v, tm, name):
    return _rowwise(lambda r, c: list(_adamw(*r)), [w, g, m, v], [], [(w.shape[1], F32)] * 3, [], tm, name)


BIG = ("w_in", "w_out", "w_mlp_in", "w_mlp_out", "w_glu")
COL_SHARDED = ("w_in", "w_out", "w_mlp_in")
SMALL = ("b_ada", "g_pre_mix", "g_post_mix", "ssm_a_re", "ssm_a_im", "ssm_log_dt", "ssm_b_re", "ssm_b_im",
         "ssm_c_re", "ssm_c_im", "ssm_d", "b_glu", "g_attn_out", "g_ssm_out", "g_pre_mlp", "g_post_mlp")
WEIGHTS = ("w_ada", "b_ada", "g_pre_mix", "g_post_mix", "w_in", "ssm_a_re", "ssm_a_im", "ssm_log_dt", "ssm_b_re",
           "ssm_b_im", "ssm_c_re", "ssm_c_im", "ssm_d", "w_glu", "b_glu", "g_attn_out", "g_ssm_out", "w_out",
           "g_pre_mlp", "g_post_mlp", "w_mlp_in", "w_mlp_out")
FLAT_COLS = 1024
FLAT_ROWS = 256
ROW_TILE = {"w_in": 256, "w_out": 128, "w_mlp_in": 256, "w_mlp_out": 256, "w_glu": 112}


def _flatten_small(tree):
    flat = jnp.concatenate([tree[k].reshape(-1) for k in SMALL])
    return jnp.pad(flat, (0, FLAT_ROWS * FLAT_COLS - flat.shape[0])).reshape(FLAT_ROWS, FLAT_COLS)


def _unflatten_small(flat, like):
    flat = flat.reshape(-1)
    out, at = {}, 0
    for k in SMALL:
        size = math.prod(like[k].shape)
        out[k] = flat[at:at + size].reshape(like[k].shape)
        at += size
    return out


def _unstack(stack, name):
    if name in COL_SHARDED:
        return stack.transpose(1, 0, 2).reshape(stack.shape[1], N_CHIPS * stack.shape[2])
    return stack.reshape(N_CHIPS * stack.shape[1], stack.shape[2])


def _stack(full, name):
    if name in COL_SHARDED:
        return full.reshape(full.shape[0], N_CHIPS, full.shape[1] // N_CHIPS).transpose(1, 0, 2)
    return full.reshape(N_CHIPS, full.shape[0] // N_CHIPS, full.shape[1])


EARLY = ("w_in", "w_out", "w_glu")
LATE = ("w_mlp_in", "w_mlp_out")
MIXER_W = ("w_out", "w_glu")


def _chip_sums(names, g_stacks, from_sibling, ic, chip):
    own, to_send = [], []
    place = jnp.stack([ic, chip]).astype(jnp.int32)
    for k, gs, fs in zip(names, g_stacks, from_sibling):
        _, rows, cols = gs.shape
        half, tm = rows // 2, ROW_TILE[k]
        nt = half // tm

        def body(place_ref, g_ref, f_ref, own_ref, send_ref):
            s = g_ref[...] + f_ref[...]
            send_ref[...] = s.astype(BF16)

            @pl.when(pl.program_id(1) == place_ref[1])
            def _():
                own_ref[...] = s

        slab = lambda index: pl.BlockSpec((None, tm, cols), index)
        mine, to_chips = pl.pallas_call(
            body,
            name="grad_chip_sum_" + k,
            grid_spec=pltpu.PrefetchScalarGridSpec(
                num_scalar_prefetch=1,
                grid=(nt, N_CHIPS),
                in_specs=[slab(lambda i, kk, p, nt=nt: (kk, p[0] * nt + i, 0)), slab(lambda i, kk, p: (kk, i, 0))],
                out_specs=[pl.BlockSpec((tm, cols), lambda i, kk, p: (i, 0)), slab(lambda i, kk, p: (kk, i, 0))]),
            out_shape=[jax.ShapeDtypeStruct((half, cols), F32), jax.ShapeDtypeStruct((N_CHIPS, half, cols), BF16)],
            compiler_params=_params(("arbitrary", "arbitrary")),
        )(place, gs, fs)
        own.append(mine)
        to_send.append(to_chips)
    return own, to_send


def _grad_totals(names, own, from_chips, ic):
    totals = []
    place = jnp.reshape(ic, (1,)).astype(jnp.int32)
    for k, mine, fc in zip(names, own, from_chips):
        half, cols = mine.shape
        tm = ROW_TILE[k]
        nt = half // tm

        def body(place_ref, m_ref, a_ref, b_ref, c_ref, o_ref):
            o_ref[...] = m_ref[...] + a_ref[...].astype(F32) + b_ref[...].astype(F32) + c_ref[...].astype(F32)

        totals.append(pl.pallas_call(
            body,
            name="grad_total_" + k,
            grid_spec=pltpu.PrefetchScalarGridSpec(
                num_scalar_prefetch=1,
                grid=(nt,),
                in_specs=[pl.BlockSpec((tm, cols), lambda i, p: (i, 0))] + [
                    pl.BlockSpec((None, tm, cols), lambda i, p, j=j: (j, i, 0)) for j in range(3)],
                out_specs=pl.BlockSpec((tm, cols), lambda i, p, nt=nt: (p[0] * nt + i, 0))),
            out_shape=jax.ShapeDtypeStruct((2 * half, cols), F32),
            compiler_params=_params(("arbitrary",)),
        )(place, mine, fc, fc, fc))
    return totals


class _LateWeights:
    def __init__(self, names, own_shards, chip, after, tag):
        self.names, self.chip, self.tag = names, chip, tag
        self.shapes = [o.shape for o in own_shards]
        lands = [lax.empty((N_CHIPS,) + s, BF16) for s in self.shapes]
        self.gather = _split_copy_start(own_shards, lands, _weight_plan(self.shapes), 3 * len(names),
                                        tag + "_gather_start", after=after)
        self.token = self.gather[-1]

    def arrived(self, after):
        n = len(self.names)
        done = _split_copy_wait(self.gather, _weight_plan(self.shapes), after, self.tag + "_gather_wait")
        self.own_shards = done[:n]
        self.forward = _split_copy_start(done[n:], [], _forward_plan(self.shapes), 3 * n, self.tag + "_forward_start")
        return self.forward[-1]

    def weights(self, after):
        n = len(self.names)
        stacks = _split_copy_wait(self.forward, _forward_plan(self.shapes), after, self.tag + "_forward_wait", n_srcs=n)
        stacks = [lax.dynamic_update_index_in_dim(s, o, self.chip, 0) for s, o in zip(stacks, self.own_shards)]
        return {k: _unstack(s, k) for k, s in zip(self.names, stacks)}


class _Overlap:
    def __init__(self, own, ic, chip, after):
        self.ic, self.chip = ic, chip
        self.mixer = _LateWeights(MIXER_W, [own[k] for k in MIXER_W], chip, after, "mixer_weight")
        self.mlp = _LateWeights(LATE, [own[k] for k in LATE], chip, self.mixer.token, "mlp_weight")
        self.token = self.mlp.token

    def mlp_grads_to_sibling(self, dw_in, dw_out):
        stacks = [dw_in, dw_out]
        self.g_shapes = [s.shape for s in stacks]
        lands = [lax.empty((N_CHIPS, s[1] // 2, s[2]), F32) for s in self.g_shapes]
        self.halves = _split_copy_start(stacks, lands, _halves_plan(self.g_shapes), len(LATE), "mlp_grad_halves_start")
        return self.halves[-1]

    def mlp_grads_to_chips(self, after):
        n = len(LATE)
        done = _split_copy_wait(self.halves, _halves_plan(self.g_shapes), after, "mlp_grad_halves_wait")
        self.own, to_send = _chip_sums(LATE, done[:n], done[n:], self.ic, self.chip)
        lands = [lax.empty((3,) + s.shape[1:], BF16) for s in to_send]
        self.exchange = _split_copy_start(to_send, lands, _exchange_plan(n), 3 * n, "mlp_grad_exchange_start")
        return self.exchange[-1]

    def mlp_grads_reduced(self, after):
        n = len(LATE)
        done = _split_copy_wait(self.exchange, _exchange_plan(n), after, "mlp_grad_exchange_wait")
        return _grad_totals(LATE, self.own, done[n:], self.ic)


def _pad_rows(row):
    return jnp.pad(row, ((0, 8 - row.shape[0]), (0, 0)))


def _every_eighth(gathered):
    rows = gathered.reshape(N_DEV, 8, gathered.shape[1])[:, 0, :]
    return jnp.pad(rows, ((0, ROW_PAD - N_DEV), (0, 0)))


def kernel(x, c, positions, w_ada, b_ada, g_pre_mix, g_post_mix, w_in, ssm_a_re, ssm_a_im, ssm_log_dt, ssm_b_re, ssm_b_im, ssm_c_re, ssm_c_im, ssm_d, w_glu, b_glu, g_attn_out, g_ssm_out, w_out, g_pre_mlp, g_post_mlp, w_mlp_in, w_mlp_out, loss_target, m_w_ada, m_b_ada, m_g_pre_mix, m_g_post_mix, m_w_in, m_ssm_a_re, m_ssm_a_im, m_ssm_log_dt, m_ssm_b_re, m_ssm_b_im, m_ssm_c_re, m_ssm_c_im, m_ssm_d, m_w_glu, m_b_glu, m_g_attn_out, m_g_ssm_out, m_w_out, m_g_pre_mlp, m_g_post_mlp, m_w_mlp_in, m_w_mlp_out, v_w_ada, v_b_ada, v_g_pre_mix, v_g_post_mix, v_w_in, v_ssm_a_re, v_ssm_a_im, v_ssm_log_dt, v_ssm_b_re, v_ssm_b_im, v_ssm_c_re, v_ssm_c_im, v_ssm_d, v_w_glu, v_b_glu, v_g_attn_out, v_g_ssm_out, v_w_out, v_g_pre_mlp, v_g_post_mlp, v_w_mlp_in, v_w_mlp_out):
    given = dict(locals())
    w = {k: given[k][0] for k in WEIGHTS}
    mom = {k: given["m_" + k][0] for k in WEIGHTS}
    var = {k: given["v_" + k][0] for k in WEIGHTS}
    for tree in (w, mom, var):
        for k in ("b_ada", "g_pre_mix", "g_post_mix", "ssm_log_dt", "b_glu", "g_attn_out", "g_ssm_out", "g_pre_mlp",
                  "g_post_mlp"):
            tree[k] = tree[k].reshape(1, -1)
    ix, iy, ic = lax.axis_index("x"), lax.axis_index("y"), lax.axis_index("c")
    chip = 2 * ix + iy
    me = 4 * ix + 2 * iy + ic
    shard_cols = w["w_ada"].shape[1]

    c_all = _every_eighth(_all_gather8(_pad_rows(c), "gather_c"))
    b_ada_cols = lax.dynamic_slice_in_dim(w["b_ada"], chip * shard_cols, shard_cols, axis=1)
    mod_cols = _ada_fwd(c_all, w["w_ada"], b_ada_cols, "ada_fwd")[:N_DEV]
    mod_all = _all_gather8(mod_cols, "gather_mod").reshape(N_CHIPS, 2, N_DEV, shard_cols)[:, 0]
    mod = lax.dynamic_index_in_dim(mod_all, me, axis=1, keepdims=False).reshape(1, N_MOD * D_MODEL)

    w_in_own = w["w_in"].astype(BF16)
    (w_in_stack,) = _weight_gather([w_in_own], "weight_gather")
    w_in_stack = lax.dynamic_update_index_in_dim(w_in_stack, w_in_own, chip, 0)
    wts = {"w_in": _unstack(w_in_stack, "w_in")}
    overlap = _Overlap({k: w[k].astype(BF16) for k in MIXER_W + LATE}, ic, chip, after=w_in_stack)
    mod = _tie(mod, overlap.token)

    small = {k: w[k] for k in SMALL if k != "b_ada"}
    loss, grad_x, dmod, big_g, small_g = _local_step(x[0], positions.reshape(-1, 1), mod, loss_target[0], wts, small,
                                                     hooks=overlap)
    loss = lax.psum(loss[0, 0], ("x", "y", "c"))

    n_early = len(EARLY)
    g_stacks = [_stack(big_g[k], k) for k in EARLY]
    halves_plan = _halves_plan([s.shape for s in g_stacks])
    lands = [lax.empty((N_CHIPS, s.shape[1] // 2, s.shape[2]), F32) for s in g_stacks]
    halves = _split_copy_start(g_stacks, lands, halves_plan, n_early, "grad_halves_start")

    small_g["b_ada"] = dmod
    parts = _all_gather8(_tie(_flatten_small(small_g), halves[-1]), "gather_small_grads")

    halves = _split_copy_wait(halves, halves_plan, parts, "grad_halves_wait")
    chip_f32, chip_bf16 = _chip_sums(EARLY, halves[:n_early], halves[n_early:], ic, chip)
    exchange_plan = _exchange_plan(len(EARLY))
    lands = [lax.empty((3,) + s.shape[1:], BF16) for s in chip_bf16]
    exchange = _split_copy_start(chip_bf16, lands, exchange_plan, 3 * len(EARLY), "grad_exchange_start", after=parts)

    small_flat = _sum_blocks(parts, N_DEV, "small_grad_sum")
    grads = _unflatten_small(small_flat, w)

    mod_rows = N_MOD * D_MODEL // FLAT_COLS
    dmod_all = parts.reshape(N_DEV, FLAT_ROWS, FLAT_COLS)[:, :mod_rows].reshape(N_DEV, N_MOD * D_MODEL)
    dmod_all = jnp.pad(dmod_all, ((0, ROW_PAD - N_DEV), (0, 0)))
    dmod_cols = lax.dynamic_slice_in_dim(dmod_all, chip * shard_cols, shard_cols, axis=1)
    mlp_reduced = overlap.mlp_grads_reduced(exchange[-1])
    swap_plan = _swap_plan([r.shape for r in mlp_reduced])
    swap = _split_copy_start(mlp_reduced, [], swap_plan, len(LATE), "mlp_grad_swap_start")
    g_ada, d_ada, m_ada, v_ada = _ada_bwd_adamw(_tie(c_all, swap[-1]), dmod_cols, w["w_ada"], mom["w_ada"],
                                                var["w_ada"], "ada_bwd_adamw")
    grads["w_ada"] = g_ada
    delta, new_m, new_v = {"w_ada": d_ada}, {"w_ada": m_ada}, {"w_ada": v_ada}

    def finish(names, shards):
        for k, g in zip(names, shards):
            grads[k] = g
            delta[k], new_m[k], new_v[k] = _adamw_rows(w[k], g, mom[k], var[k], ROW_TILE[k], "adamw_" + k)

    finish(LATE, _split_copy_wait(swap, swap_plan, g_ada, "mlp_grad_swap_wait", n_srcs=len(LATE)))
    from_chips = _split_copy_wait(exchange, exchange_plan, new_v[LATE[-1]], "grad_exchange_wait")[n_early:]
    reduced = _grad_totals(EARLY, chip_f32, from_chips, ic)
    swap_plan = _swap_plan([r.shape for r in reduced])
    swap = _split_copy_start(reduced, [], swap_plan, n_early, "grad_swap_start")
    finish(EARLY, _split_copy_wait(swap, swap_plan, swap[-1], "grad_swap_wait", n_srcs=n_early))

    flat_upd = _adamw_rows(_flatten_small(w), small_flat, _flatten_small(mom), _flatten_small(var), FLAT_ROWS,
                           "adamw_small")
    for tree, flat in zip((delta, new_m, new_v), flat_upd):
        tree.update(_unflatten_small(flat, w))

    shaped = lambda tree: [tree[k].reshape(given[k].shape) for k in WEIGHTS]
    return (loss, grad_x[None], *shaped(grads), *shaped(delta), *shaped(new_m), *shaped(new_v))
```
